```python
import jax
import jax.numpy as jnp
from jax import lax


D_MODEL = 1024
BATCH = 2
SEQ = 8192
DEPTH = 2

GRID_W = 64
CTX_LEN = 256
NORM_EPS = 1e-6
GLA_HEADS = 4
GLA_DK = 64
GLA_DV = 128
GLA_KW = GLA_HEADS * GLA_DK
GLA_WIDTH = GLA_HEADS * GLA_DV
GLA_GATE_RANK = 16
GLA_TAU = 16.0
GLA_CHUNK = 64
SWA_HEADS = 8
SWA_KV_HEADS = 2
SWA_GROUP = SWA_HEADS // SWA_KV_HEADS
SWA_HD = 64
SWA_WIDTH = SWA_HEADS * SWA_HD
SWA_KVW = SWA_KV_HEADS * SWA_HD
WINDOW = 128
ATT_BLOCK = 128
ROPE_THETA = 10000.0
ROPE_AXIS_DIM = SWA_HD // 2
MIX_WIDTH = GLA_WIDTH + SWA_WIDTH
N_EXPERTS = 16
N_GROUPS = 4
EXPERTS_PER_GROUP = N_EXPERTS // N_GROUPS
TOP_K = 2
D_EXPERT = 512
MOE_BLOCK = 256
IN_SIZES = (GLA_KW, GLA_KW, GLA_WIDTH, GLA_WIDTH, GLA_GATE_RANK, GLA_GATE_RANK, SWA_WIDTH, SWA_KVW, SWA_KVW)
IN_COLS = sum(IN_SIZES)

kernel_name = 'hybrid_gla_swa_moe_dit_block'


def rmsnorm(x, g):
    xf = x.astype(jnp.float32)
    y = xf * lax.rsqrt(jnp.mean(xf * xf, axis=-1, keepdims=True) + NORM_EPS)
    return (y * g.astype(jnp.float32)).astype(x.dtype)


def split_in_proj(p):
    points, acc = [], 0
    for s in IN_SIZES[:-1]:
        acc += s
        points.append(acc)
    return jnp.split(p, points, axis=-1)


def axial_rope_tables(n_tok):
    rows = n_tok // GRID_W
    row = jnp.repeat(jnp.arange(rows, dtype=jnp.float32), GRID_W)
    col = jnp.tile(jnp.arange(GRID_W, dtype=jnp.float32), rows)
    n_freq = ROPE_AXIS_DIM // 2
    inv = ROPE_THETA ** (-(jnp.arange(n_freq, dtype=jnp.float32) * 2.0 / ROPE_AXIS_DIM))
    ang_r = row[:, None] * inv[None, :]
    ang_c = col[:, None] * inv[None, :]
    return (jnp.cos(ang_r), jnp.sin(ang_r), jnp.cos(ang_c), jnp.sin(ang_c))


def rotate_half(x, cos, sin):
    x1, x2 = jnp.split(x, 2, axis=-1)
    return jnp.concatenate([x1 * cos - x2 * sin, x2 * cos + x1 * sin], axis=-1)


def apply_axial_rope(x, tabs):
    shape = (1, x.shape[1]) + (1,) * (x.ndim - 3) + (ROPE_AXIS_DIM // 2,)
    cr, sr, cc, sc = [t.reshape(shape).astype(x.dtype) for t in tabs]
    x_row, x_col = jnp.split(x, 2, axis=-1)
    return jnp.concatenate([rotate_half(x_row, cr, sr), rotate_half(x_col, cc, sc)], axis=-1)


def gla_chunked(q, k, v, log_a, s0):
    B, H, L, dk = q.shape
    dv = v.shape[-1]
    C = GLA_CHUNK
    N = L // C
    q = q.reshape(B, H, N, C, dk)
    k = k.reshape(B, H, N, C, dk)
    v = v.reshape(B, H, N, C, dv)
    b = jnp.cumsum(log_a.reshape(B, H, N, C, dk), axis=3)
    b_last = b[:, :, :, -1:, :]
    q_in = q * jnp.exp(b)
    k_in = k * jnp.exp(-b)
    k_out = k * jnp.exp(b_last - b)
    causal_in_chunk = jnp.tril(jnp.ones((C, C), dtype=bool))
    A = jnp.where(causal_in_chunk, jnp.einsum('bhncd,bhnsd->bhncs', q_in, k_in), 0.0)
    o_intra = jnp.einsum('bhncs,bhnse->bhnce', A, v)
    U = jnp.einsum('bhncd,bhnce->bhnde', k_out, v)
    dec = jnp.exp(b_last[:, :, :, 0, :])

    def step(S, inp):
        u, d = inp
        return d[..., None] * S + u, S

    s_final, s_prev = lax.scan(step, s0, (jnp.moveaxis(U, 2, 0), jnp.moveaxis(dec, 2, 0)))
    s_prev = jnp.moveaxis(s_prev, 0, 2)
    o_inter = jnp.einsum('bhncd,bhnde->bhnce', q_in, s_prev)
    return (o_intra + o_inter).reshape(B, H, L, dv), s_final


def gla_heads(q, k, v, z_f, z_b, up_f, bias_f, up_b, bias_b):
    B, L, _ = q.shape

    def heads(t, d):
        return t.reshape(B, L, GLA_HEADS, d).transpose(0, 2, 1, 3).astype(jnp.float32)

    log_a_f = jax.nn.log_sigmoid((z_f @ up_f + bias_f).astype(jnp.float32)) / GLA_TAU
    log_a_b = jax.nn.log_sigmoid((z_b @ up_b + bias_b).astype(jnp.float32)) / GLA_TAU
    return (heads(q, GLA_DK) * GLA_DK ** -0.5, heads(k, GLA_DK), heads(v, GLA_DV),
            heads(log_a_f, GLA_DK), heads(log_a_b, GLA_DK))


def gla_bidir(q, k, v, la_f, la_b, s_f0, s_b0):
    o_f, s_f = gla_chunked(q, k, v, la_f, s_f0)
    fl = lambda t: jnp.flip(t, axis=2)
    o_b, s_b = gla_chunked(fl(q), fl(k), fl(v), fl(la_b), s_b0)
    return o_f + fl(o_b), s_f, s_b


def gla_output(o, gate, norm_g):
    B, H, L, dv = o.shape
    o = o * lax.rsqrt(jnp.mean(o * o, axis=-1, keepdims=True) + NORM_EPS)
    o = o.transpose(0, 2, 1, 3).reshape(B, L, H * dv) * norm_g.astype(jnp.float32)
    return o.astype(gate.dtype) * jax.nn.silu(gate)


def swa_latent(q, k, v, kc, vc, sink):
    B, L, KV, G, hd = q.shape
    W = ATT_BLOCK
    N = L // W
    scale = hd ** -0.5
    qb = q.reshape(B, N, W, KV, G, hd)
    pad = ((0, 0), (W, W), (0, 0), (0, 0))
    kp = jnp.pad(k, pad).reshape(B, N + 2, W, KV, hd)
    vp = jnp.pad(v, pad).reshape(B, N + 2, W, KV, hd)
    kb = jnp.concatenate([kp[:, :-2], kp[:, 1:-1], kp[:, 2:]], axis=2)
    vb = jnp.concatenate([vp[:, :-2], vp[:, 1:-1], vp[:, 2:]], axis=2)
    qpos = jnp.arange(N)[:, None, None] * W + jnp.arange(W)[None, :, None]
    kpos = jnp.arange(N)[:, None, None] * W - W + jnp.arange(3 * W)[None, None, :]
    valid = (jnp.abs(kpos - qpos) <= WINDOW) & (kpos >= 0) & (kpos < L)
    s_loc = jnp.einsum('bnqhgd,bnkhd->bhgnqk', qb, kb).astype(jnp.float32) * scale
    s_loc = jnp.where(valid, s_loc, -jnp.inf)
    s_ctx = jnp.einsum('bnqhgd,bkhd->bhgnqk', qb, kc).astype(jnp.float32) * scale
    sk = sink.reshape(KV, G)[None, :, :, None, None, None].astype(jnp.float32)
    m = jnp.maximum(jnp.maximum(s_loc.max(-1, keepdims=True), s_ctx.max(-1, keepdims=True)), sk)
    p_loc = jnp.exp(s_loc - m)
    p_ctx = jnp.exp(s_ctx - m)
    denom = p_loc.sum(-1, keepdims=True) + p_ctx.sum(-1, keepdims=True) + jnp.exp(sk - m)
    o = (jnp.einsum('bhgnqk,bnkhd->bnqhgd', p_loc, vb.astype(jnp.float32))
         + jnp.einsum('bhgnqk,bkhd->bnqhgd', p_ctx, vc.astype(jnp.float32)))
    o = o / denom.transpose(0, 3, 4, 1, 2, 5)
    return o.reshape(B, L, KV * G * hd).astype(v.dtype)


def swa_context(q, k, v, sink):
    B, Lc, KV, G, hd = q.shape
    s = jnp.einsum('bqhgd,bkhd->bhgqk', q, k).astype(jnp.float32) * hd ** -0.5
    sk = sink.reshape(KV, G)[None, :, :, None, None].astype(jnp.float32)
    m = jnp.maximum(s.max(-1, keepdims=True), sk)
    p = jnp.exp(s - m)
    p = p / (p.sum(-1, keepdims=True) + jnp.exp(sk - m))
    o = jnp.einsum('bhgqk,bkhd->bqhgd', p, v.astype(jnp.float32))
    return o.reshape(B, Lc, KV * G * hd).astype(v.dtype)


def moe(h, w_router, b_router, w_gate, w_up, w_down):
    T, D = h.shape
    s = jax.nn.sigmoid((h @ w_router).astype(jnp.float32))
    sb = s + b_router.astype(jnp.float32)
    g_score = lax.top_k(sb.reshape(T, N_GROUPS, EXPERTS_PER_GROUP), 2)[0].sum(-1)
    g_sel = jnp.argmax(g_score, axis=-1)
    in_group = (jnp.arange(N_EXPERTS) // EXPERTS_PER_GROUP)[None, :] == g_sel[:, None]
    _, idx = lax.top_k(jnp.where(in_group, sb, -jnp.inf), TOP_K)
    w_sel = jnp.take_along_axis(s, idx, axis=-1)
    w_sel = w_sel / w_sel.sum(-1, keepdims=True)
    M = MOE_BLOCK
    A = T * TOP_K
    e_flat = idx.reshape(A)
    tok_flat = jnp.repeat(jnp.arange(T), TOP_K)
    w_flat = w_sel.reshape(A)
    order = jnp.argsort(e_flat)
    e_s, tok_s, w_s = e_flat[order], tok_flat[order], w_flat[order]
    counts = jnp.bincount(e_flat, length=N_EXPERTS)
    padded = ((counts + M - 1) // M) * M
    pend = jnp.cumsum(padded)
    pstart = pend - padded
    ustart = jnp.cumsum(counts) - counts
    dest = pstart[e_s] + jnp.arange(A) - ustart[e_s]
    n_blocks = -(-A // M) + N_EXPERTS
    P = n_blocks * M
    row_tok = jnp.full((P,), T, dtype=jnp.int32).at[dest].set(tok_s.astype(jnp.int32))
    row_w = jnp.zeros((P,), jnp.float32).at[dest].set(w_s)
    block_e = jnp.minimum(jnp.searchsorted(pend, jnp.arange(n_blocks) * M, side='right'), N_EXPERTS - 1)
    h_pad = jnp.concatenate([h, jnp.zeros((1, D), h.dtype)], axis=0)
    xb = h_pad[row_tok].reshape(n_blocks, M, D)

    def expert_block(args):
        xblk, e = args
        return (jax.nn.silu(xblk @ w_gate[e]) * (xblk @ w_up[e])) @ w_down[e]

    yb = lax.map(expert_block, (xb, block_e)).reshape(P, D)
    y = jax.ops.segment_sum(yb.astype(jnp.float32) * row_w[:, None], row_tok, num_segments=T + 1)[:T]
    return y.astype(h.dtype)


def setup_inputs(seed: int = 0) -> dict:
    key = jax.random.key(seed)
    ks = jax.random.split(key, 24)
    f32 = jnp.float32
    D = D_MODEL

    def nrm(k, shape, s):
        return jax.random.normal(k, shape, f32) * s

    return {
        'x': nrm(ks[0], (BATCH, SEQ, D), 1.0),
        'c': nrm(ks[1], (BATCH, D), 1.0),
        'ctx': nrm(ks[2], (BATCH, CTX_LEN, D), 1.0),
        'c_ctx': nrm(ks[3], (D,), 1.0),
        'w_ada': nrm(ks[4], (DEPTH, D, 6 * D), 0.5 * D ** -0.5),
        'b_ada': nrm(ks[5], (DEPTH, 6 * D), 0.02),
        'norm1': 1.0 + nrm(ks[6], (DEPTH, D), 0.02),
        'norm2': 1.0 + nrm(ks[7], (DEPTH, D), 0.02),
        'w_in': nrm(ks[8], (DEPTH, D, IN_COLS), D ** -0.5),
        'gla_up_f': nrm(ks[9], (DEPTH, GLA_GATE_RANK, GLA_KW), GLA_GATE_RANK ** -0.5),
        'gla_bias_f': nrm(ks[10], (DEPTH, GLA_KW), 0.1),
        'gla_up_b': nrm(ks[11], (DEPTH, GLA_GATE_RANK, GLA_KW), GLA_GATE_RANK ** -0.5),
        'gla_bias_b': nrm(ks[12], (DEPTH, GLA_KW), 0.1),
        'gla_norm': 1.0 + nrm(ks[13], (DEPTH, GLA_WIDTH), 0.02),
        'swa_sink': nrm(ks[14], (DEPTH, SWA_HEADS), 1.0),
        'w_out': nrm(ks[15], (DEPTH, MIX_WIDTH, D), MIX_WIDTH ** -0.5),
        'w_router': nrm(ks[16], (D, N_EXPERTS), D ** -0.5),
        'b_router': nrm(ks[17], (N_EXPERTS,), 0.01),
        'w_gate': nrm(ks[18], (DEPTH, N_EXPERTS, D, D_EXPERT), D ** -0.5),
        'w_up': nrm(ks[19], (DEPTH, N_EXPERTS, D, D_EXPERT), D ** -0.5),
        'w_down': nrm(ks[20], (DEPTH, N_EXPERTS, D_EXPERT, D), D_EXPERT ** -0.5),
        'final_norm': 1.0 + nrm(ks[21], (D,), 0.02),
    }


def reference(x, c, ctx, c_ctx, w_ada, b_ada, norm1, norm2, w_in, gla_up_f, gla_bias_f, gla_up_b,
              gla_bias_b, gla_norm, swa_sink, w_out, w_router, b_router, w_gate, w_up, w_down, final_norm):
    B, L, D = x.shape
    Lc = ctx.shape[1]
    rope_tabs = axial_rope_tables(L)
    c_act = jax.nn.silu(c)
    cc_act = jax.nn.silu(c_ctx)
    xl, xc = x, ctx
    for i in range(DEPTH):
        last = i == DEPTH - 1
        mod_l = (c_act @ w_ada[i] + b_ada[i])[:, None, :]
        mod_c = (cc_act @ w_ada[i] + b_ada[i])[None, None, :]
        sh1, sc1, g1, sh2, sc2, g2 = jnp.split(mod_l, 6, axis=-1)
        csh1, csc1, cg1, csh2, csc2, cg2 = jnp.split(mod_c, 6, axis=-1)

        hl = rmsnorm(xl, norm1[i]) * (1.0 + sc1) + sh1
        hc = rmsnorm(xc, norm1[i]) * (1.0 + csc1) + csh1
        lq, lk, lv, lg, lzf, lzb, lsq, lsk, lsv = split_in_proj(hl @ w_in[i])
        cq, ck, cv, cg, czf, czb, csq, csk, csv = split_in_proj(hc @ w_in[i])

        gq, gk, gv, gaf, gab = gla_heads(cq, ck, cv, czf, czb, gla_up_f[i], gla_bias_f[i], gla_up_b[i], gla_bias_b[i])
        s_zero = jnp.zeros((B, GLA_HEADS, GLA_DK, GLA_DV), jnp.float32)
        o_gla_c, s_f, s_b = gla_bidir(gq, gk, gv, gaf, gab, s_zero, s_zero)
        gq, gk, gv, gaf, gab = gla_heads(lq, lk, lv, lzf, lzb, gla_up_f[i], gla_bias_f[i], gla_up_b[i], gla_bias_b[i])
        o_gla_l, _, _ = gla_bidir(gq, gk, gv, gaf, gab, s_f, s_b)
        gla_l = gla_output(o_gla_l, lg, gla_norm[i])

        q_l = apply_axial_rope(lsq.reshape(B, L, SWA_KV_HEADS, SWA_GROUP, SWA_HD), rope_tabs)
        k_l = apply_axial_rope(lsk.reshape(B, L, SWA_KV_HEADS, SWA_HD), rope_tabs)
        v_l = lsv.reshape(B, L, SWA_KV_HEADS, SWA_HD)
        k_c = csk.reshape(B, Lc, SWA_KV_HEADS, SWA_HD)
        v_c = csv.reshape(B, Lc, SWA_KV_HEADS, SWA_HD)
        swa_l = swa_latent(q_l, k_l, v_l, k_c, v_c, swa_sink[i])

        xl = xl + g1 * (jnp.concatenate([gla_l, swa_l], axis=-1) @ w_out[i])

        h2l = rmsnorm(xl, norm2[i]) * (1.0 + sc2) + sh2
        if last:
            xl = xl + g2 * moe(h2l.reshape(B * L, D), w_router, b_router, w_gate[i], w_up[i], w_down[i]).reshape(B, L, D)
        else:
            gla_c = gla_output(o_gla_c, cg, gla_norm[i])
            q_c = csq.reshape(B, Lc, SWA_KV_HEADS, SWA_GROUP, SWA_HD)
            swa_c = swa_context(q_c, k_c, v_c, swa_sink[i])
            xc = xc + cg1 * (jnp.concatenate([gla_c, swa_c], axis=-1) @ w_out[i])
            h2c = rmsnorm(xc, norm2[i]) * (1.0 + csc2) + csh2
            tokens = jnp.concatenate([h2l.reshape(B * L, D), h2c.reshape(B * Lc, D)], axis=0)
            y = moe(tokens, w_router, b_router, w_gate[i], w_up[i], w_down[i])
            xl = xl + g2 * y[:B * L].reshape(B, L, D)
            xc = xc + cg2 * y[B * L:].reshape(B, Lc, D)
    return rmsnorm(xl, final_norm)
```

```python
import functools

import jax
import jax.numpy as jnp
from jax import lax
from jax.experimental import pallas as pl
from jax.experimental.pallas import tpu as pltpu

F32 = jnp.float32
BF16 = jnp.bfloat16
I32 = jnp.int32
HIGHEST = lax.Precision.HIGHEST

GRID_W = 64
NORM_EPS = 1e-6
GLA_HEADS = 4
GLA_DK = 64
GLA_DV = 128
GLA_KW = GLA_HEADS * GLA_DK
GLA_WIDTH = GLA_HEADS * GLA_DV
GLA_GATE_RANK = 16
GLA_TAU = 16.0
GLA_CHUNK = 64
SWA_HEADS = 8
SWA_KV_HEADS = 2
SWA_HD = 64
SWA_WIDTH = SWA_HEADS * SWA_HD
SWA_KVW = SWA_KV_HEADS * SWA_HD
WINDOW = 128
ATT_BLOCK = 128
ROPE_THETA = 10000.0
ROPE_AXIS_DIM = SWA_HD // 2
N_EXPERTS = 16
N_GROUPS = 4
EXPERTS_PER_GROUP = N_EXPERTS // N_GROUPS
TOP_K = 2
D_EXPERT = 512
MOE_BLOCK = 256
IN_SIZES = (GLA_KW, GLA_KW, GLA_WIDTH, GLA_WIDTH, GLA_GATE_RANK, GLA_GATE_RANK, SWA_WIDTH, SWA_KVW, SWA_KVW)

LANES = 128
SUBLANES = 8
VMEM_LIMIT = 48 * 1024 * 1024

C_QK = 0
C_V = 512
C_G = 1024
C_Z = 1536
C_SQ = 1664
C_SK = 2176
C_SV = 2432
C_END = 2688
NEG_BIG = -1e30


def _cparams(sem):
    return pltpu.CompilerParams(dimension_semantics=sem, vmem_limit_bytes=VMEM_LIMIT)


def _ada_kernel(a_ref, w_ref, b_ref, o_ref):
    a = a_ref[...]
    act = a * jax.nn.sigmoid(a)
    o_ref[...] = jnp.dot(act, w_ref[...], preferred_element_type=F32, precision=HIGHEST) + b_ref[...]


def _ada_call(rows, w_ada, b_ada):
    depth, d, n6 = w_ada.shape
    tn = 1536
    return pl.pallas_call(
        _ada_kernel,
        grid=(depth, n6 // tn),
        in_specs=[
            pl.BlockSpec((SUBLANES, d), lambda l, j: (0, 0)),
            pl.BlockSpec((None, d, tn), lambda l, j: (l, 0, j)),
            pl.BlockSpec((None, 1, tn), lambda l, j: (l, 0, j)),
        ],
        out_specs=pl.BlockSpec((None, SUBLANES, tn), lambda l, j: (l, 0, j)),
        out_shape=jax.ShapeDtypeStruct((depth, SUBLANES, n6), F32),
        compiler_params=_cparams(("arbitrary", "arbitrary")),
        name="adaln",
    )(rows, w_ada, b_ada.reshape(depth, 1, n6))


def _rope_tile(xj, cos, sin, lane_lo):
    partner = jnp.where(lane_lo, pltpu.roll(xj, LANES - 16, 1), pltpu.roll(xj, 16, 1))
    return xj * cos + partner * sin


def _inproj_kernel(*refs, rope):
    if rope:
        (x_ref, g_ref, sc_ref, sh_ref, w_ref, cos_ref, sin_ref,
         qk_ref, v_ref, gate_ref, z_ref, sq_ref, sk_ref, sv_ref) = refs
    else:
        (x_ref, g_ref, sc_ref, sh_ref, w_ref,
         qk_ref, v_ref, gate_ref, z_ref, sq_ref, sk_ref, sv_ref) = refs
    x = x_ref[...]
    ms = jnp.mean(x * x, axis=-1, keepdims=True)
    h = (x * lax.rsqrt(ms + NORM_EPS)) * g_ref[...]
    h = h * (1.0 + sc_ref[...]) + sh_ref[...]
    hb = h.astype(BF16)

    def proj(a, b):
        return jnp.dot(hb, w_ref[:, a:b], preferred_element_type=F32)

    qk = proj(C_QK, C_V)
    qk_ref[:, :GLA_KW] = (qk[:, :GLA_KW] * (GLA_DK ** -0.5)).astype(qk_ref.dtype)
    qk_ref[:, GLA_KW:] = qk[:, GLA_KW:].astype(qk_ref.dtype)
    v_ref[...] = proj(C_V, C_G).astype(v_ref.dtype)
    gate_ref[...] = proj(C_G, C_Z).astype(gate_ref.dtype)
    z_ref[...] = proj(C_Z, C_SQ)
    sv_ref[...] = proj(C_SV, C_END).astype(sv_ref.dtype)
    sq = proj(C_SQ, C_SK) * (SWA_HD ** -0.5)
    sk = proj(C_SK, C_SV)
    if rope:
        cos = cos_ref[...]
        sin = sin_ref[...]
        lane_lo = (lax.broadcasted_iota(I32, cos.shape, 1) % 32) < 16
        for j in range(SWA_WIDTH // LANES):
            sl = slice(j * LANES, (j + 1) * LANES)
            sq_ref[:, sl] = _rope_tile(sq[:, sl], cos, sin, lane_lo).astype(sq_ref.dtype)
        for j in range(2 * SWA_KVW // LANES):
            sl = slice(j * LANES, (j + 1) * LANES)
            sk_ref[:, sl] = _rope_tile(sk[:, sl], cos, sin, lane_lo).astype(sk_ref.dtype)
    else:
        sq_ref[...] = sq.astype(sq_ref.dtype)
        sk_ref[...] = sk.astype(sk_ref.dtype)


def _inproj_call(x2d, g, mods5, layer, row_of_tile, w_packed, tabs, tm, tiles_per_seq):
    n, d = x2d.shape
    nt = n // tm
    rope = tabs is not None

    def mod_spec(k):
        return pl.BlockSpec((None, None, None, 1, d), lambda t: (layer, row_of_tile(t), k, 0, 0))

    in_specs = [
        pl.BlockSpec((tm, d), lambda t: (t, 0)),
        pl.BlockSpec((1, d), lambda t: (0, 0)),
        mod_spec(1), mod_spec(0),
        pl.BlockSpec((d, C_END), lambda t: (0, 0)),
    ]
    args = [x2d, g.reshape(1, d), mods5, mods5, w_packed]
    if rope:
        in_specs += [pl.BlockSpec((tm, LANES), lambda t: (t % tiles_per_seq, 0))] * 2
        args += [tabs[0], tabs[1]]
    widths = (2 * GLA_KW, GLA_WIDTH, GLA_WIDTH, LANES, SWA_WIDTH, 2 * SWA_KVW, 2 * SWA_KVW)
    dtypes = (BF16, BF16, BF16, F32, BF16, BF16, BF16)
    return pl.pallas_call(
        functools.partial(_inproj_kernel, rope=rope),
        grid=(nt,),
        in_specs=in_specs,
        out_specs=[pl.BlockSpec((tm, w), lambda t: (t, 0)) for w in widths],
        out_shape=[jax.ShapeDtypeStruct((n, w), dt) for w, dt in zip(widths, dtypes)],
        compiler_params=_cparams(("arbitrary",)),
        name="inproj",
    )(*args)


def _log_sigmoid(x):
    return jnp.minimum(x, 0.0) - jnp.log(1.0 + jnp.exp(-jnp.abs(x)))


def _gla_chunk(q_ref, k_ref, v_ref, o_ref, la_ref, s_ref, r0, fwd):
    c = GLA_CHUNK
    rows = pl.ds(r0, c)
    ri = lax.broadcasted_iota(I32, (c, c), 0)
    ci = lax.broadcasted_iota(I32, (c, c), 1)
    tri = ((ci <= ri) if fwd else (ci >= ri)).astype(F32)
    la = la_ref[rows, :]
    b = jnp.dot(tri, la, preferred_element_type=F32, precision=HIGHEST)
    btot = b[c - 1:c, :] if fwd else b[0:1, :]
    q = q_ref[rows, :].astype(F32)
    k = k_ref[rows, :].astype(F32)
    v = v_ref[rows, :]
    q_in = (q * jnp.exp(b)).astype(BF16)
    k_in = k * jnp.exp(-b)
    k_out = (k * jnp.exp(btot - b)).astype(BF16)
    dec = jnp.exp(btot)

    lane = lax.broadcasted_iota(I32, (c, LANES), 1)
    head0 = lane < GLA_DK
    kst = jnp.concatenate([jnp.where(head0, k_in, 0.0), jnp.where(head0, 0.0, k_in)], axis=0).astype(BF16)
    a = lax.dot_general(q_in, kst, (((1,), (1,)), ((), ())), preferred_element_type=F32)
    r2 = lax.broadcasted_iota(I32, (c, 2 * c), 0)
    c2 = lax.broadcasted_iota(I32, (c, 2 * c), 1) % c
    keep = (c2 <= r2) if fwd else (c2 >= r2)
    a = jnp.where(keep, a, 0.0).astype(BF16)
    vf = v.astype(F32)
    vlane = lax.broadcasted_iota(I32, (c, 2 * GLA_DV), 1)
    vhead0 = vlane < GLA_DV
    vbd = jnp.concatenate([jnp.where(vhead0, vf, 0.0), jnp.where(vhead0, 0.0, vf)], axis=0).astype(BF16)
    s_t = s_ref[...]
    o = jnp.dot(a, vbd, preferred_element_type=F32)
    o = o + lax.dot_general(q_in, s_t.astype(BF16), (((1,), (1,)), ((), ())), preferred_element_type=F32)
    o_ref[rows, :] = o
    u_t = lax.dot_general(v, k_out, (((0,), (0,)), ((), ())), preferred_element_type=F32)
    srow = lax.broadcasted_iota(I32, (2 * GLA_DV, LANES), 0) // GLA_DV
    scol = lax.broadcasted_iota(I32, (2 * GLA_DV, LANES), 1) // GLA_DK
    s_ref[...] = s_t * dec + jnp.where(srow == scol, u_t, 0.0)


def _gla_kernel(qf_ref, kf_ref, vf_ref, zf_ref, qb_ref, kb_ref, vb_ref, zb_ref,
                upf_ref, upb_ref, bf_ref, bb_ref, s0_ref,
                of_ref, ob_ref, sfin_ref, s_scr, la_scr, *, nchunk):
    i = pl.program_id(2)
    nt = pl.num_programs(2)

    @pl.when(i == 0)
    def _():
        s_scr[...] = s0_ref[...]

    xf = jnp.dot(zf_ref[...], upf_ref[...], preferred_element_type=F32, precision=HIGHEST) + bf_ref[...]
    la_scr[0] = _log_sigmoid(xf) * (1.0 / GLA_TAU)
    xb = jnp.dot(zb_ref[...], upb_ref[...], preferred_element_type=F32, precision=HIGHEST) + bb_ref[...]
    la_scr[1] = _log_sigmoid(xb) * (1.0 / GLA_TAU)
    for cidx in range(nchunk):
        _gla_chunk(qf_ref, kf_ref, vf_ref, of_ref, la_scr.at[0], s_scr.at[0], cidx * GLA_CHUNK, True)
        _gla_chunk(qb_ref, kb_ref, vb_ref, ob_ref, la_scr.at[1], s_scr.at[1],
                   (nchunk - 1 - cidx) * GLA_CHUNK, False)

    @pl.when(i == nt - 1)
    def _():
        sfin_ref[...] = s_scr[...]


def _gla_call(qk, v, z, upf_pad, upb_pad, bias_f, bias_b, s0, batch, seq, tl):
    n = qk.shape[0]
    nt = seq // tl
    npair = GLA_HEADS // 2

    def fwd_row(b, p, i):
        return b * nt + i

    def bwd_row(b, p, i):
        return b * nt + (nt - 1 - i)

    def specs(row):
        return [
            pl.BlockSpec((tl, LANES), lambda b, p, i: (row(b, p, i), p)),
            pl.BlockSpec((tl, LANES), lambda b, p, i: (row(b, p, i), npair + p)),
            pl.BlockSpec((tl, 2 * GLA_DV), lambda b, p, i: (row(b, p, i), p)),
            pl.BlockSpec((tl, LANES), lambda b, p, i: (row(b, p, i), 0)),
        ]

    in_specs = specs(fwd_row) + specs(bwd_row) + [
        pl.BlockSpec((None, LANES, LANES), lambda b, p, i: (p, 0, 0)),
        pl.BlockSpec((None, LANES, LANES), lambda b, p, i: (p, 0, 0)),
        pl.BlockSpec((None, 1, LANES), lambda b, p, i: (p, 0, 0)),
        pl.BlockSpec((None, 1, LANES), lambda b, p, i: (p, 0, 0)),
        pl.BlockSpec((None, None, 2, 2 * GLA_DV, LANES), lambda b, p, i: (b, p, 0, 0, 0)),
    ]
    out_specs = [
        pl.BlockSpec((tl, 2 * GLA_DV), lambda b, p, i: (fwd_row(b, p, i), p)),
        pl.BlockSpec((tl, 2 * GLA_DV), lambda b, p, i: (bwd_row(b, p, i), p)),
        pl.BlockSpec((None, None, 2, 2 * GLA_DV, LANES), lambda b, p, i: (b, p, 0, 0, 0)),
    ]
    out_shape = [
        jax.ShapeDtypeStruct((n, GLA_WIDTH), F32),
        jax.ShapeDtypeStruct((n, GLA_WIDTH), F32),
        jax.ShapeDtypeStruct((batch, npair, 2, 2 * GLA_DV, LANES), F32),
    ]
    return pl.pallas_call(
        functools.partial(_gla_kernel, nchunk=tl // GLA_CHUNK),
        grid=(batch, npair, nt),
        in_specs=in_specs,
        out_specs=out_specs,
        out_shape=out_shape,
        scratch_shapes=[pltpu.VMEM((2, 2 * GLA_DV, LANES), F32), pltpu.VMEM((2, tl, LANES), F32)],
        compiler_params=_cparams(("arbitrary", "arbitrary", "arbitrary")),
        name="gla",
    )(qk, qk, v, z, qk, qk, v, z, upf_pad, upb_pad, bias_f, bias_b, s0)


def _swa_kernel(*refs, nb, local):
    if local:
        sink_ref, q_ref, kp_ref, kc_ref, kn_ref, vp_ref, vc_ref, vn_ref, kx_ref, vx_ref, o_ref = refs
        kall = jnp.concatenate([kp_ref[...], kc_ref[...], kn_ref[...], kx_ref[...]], axis=0)
        vall = jnp.concatenate([vp_ref[...], vc_ref[...], vn_ref[...], vx_ref[...]], axis=0)
    else:
        sink_ref, q_ref, kx_ref, vx_ref, o_ref = refs
        kall = kx_ref[...]
        vall = vx_ref[...]
    w = ATT_BLOCK
    nk = kall.shape[0]
    if local:
        i = pl.program_id(1)
        r = lax.broadcasted_iota(I32, (w, nk), 0)
        c = lax.broadcasted_iota(I32, (w, nk), 1)
        rel = c - w - r
        valid = (jnp.abs(rel) <= WINDOW) & ((c >= w) | (i > 0)) & ((c < 2 * w) | (i < nb - 1))
        valid = valid | (c >= 3 * w)
    lane = lax.broadcasted_iota(I32, (nk, LANES), 1)
    lo = lane < SWA_HD
    group = SWA_HEADS // SWA_KV_HEADS
    for g in range(SWA_KV_HEADS):
        kd = kall[:, g * LANES:(g + 1) * LANES].astype(F32)
        vd = vall[:, g * LANES:(g + 1) * LANES].astype(F32)
        k_half = (jnp.where(lo, kd, 0.0).astype(BF16), jnp.where(lo, 0.0, kd).astype(BF16))
        v_half = (jnp.where(lo, vd, 0.0).astype(BF16), jnp.where(lo, 0.0, vd).astype(BF16))
        for jj in range(group // 2):
            j = g * (group // 2) + jj
            qt = q_ref[:, j * LANES:(j + 1) * LANES]
            acc = jnp.zeros((w, LANES), F32)
            for half in range(2):
                sk = sink_ref[2 * j + half]
                s = lax.dot_general(qt, k_half[half], (((1,), (1,)), ((), ())), preferred_element_type=F32)
                if local:
                    s = jnp.where(valid, s, NEG_BIG)
                m = jnp.maximum(jnp.max(s, axis=-1, keepdims=True), sk)
                p = jnp.exp(s - m)
                den = jnp.sum(p, axis=-1, keepdims=True) + jnp.exp(sk - m)
                pv = jnp.dot(p.astype(BF16), v_half[half], preferred_element_type=F32)
                acc = acc + pv / den
            o_ref[:, j * LANES:(j + 1) * LANES] = acc.astype(o_ref.dtype)


def _swa_call(sq, skd, svd, kcd, vcd, sink, batch, seq, lc):
    n = sq.shape[0]
    w = ATT_BLOCK
    nb = seq // w
    kvw = 2 * SWA_KVW

    def blk(off):
        return lambda b, i: (b * nb + jnp.clip(i + off, 0, nb - 1), 0)

    kv_specs = [pl.BlockSpec((w, kvw), blk(-1)), pl.BlockSpec((w, kvw), blk(0)), pl.BlockSpec((w, kvw), blk(1))]
    ctx_spec = pl.BlockSpec((lc, kvw), lambda b, i: (b, 0))
    return pl.pallas_call(
        functools.partial(_swa_kernel, nb=nb, local=True),
        grid=(batch, nb),
        in_specs=[pl.BlockSpec(memory_space=pltpu.SMEM), pl.BlockSpec((w, SWA_WIDTH), blk(0))]
        + kv_specs + kv_specs + [ctx_spec, ctx_spec],
        out_specs=pl.BlockSpec((w, SWA_WIDTH), blk(0)),
        out_shape=jax.ShapeDtypeStruct((n, SWA_WIDTH), BF16),
        compiler_params=_cparams(("arbitrary", "arbitrary")),
        name="swa",
    )(sink, sq, skd, skd, skd, svd, svd, svd, kcd, vcd)


def _swa_ctx_call(sq, kcd, vcd, sink, batch, lc):
    n = sq.shape[0]
    w = ATT_BLOCK
    nb = lc // w
    kvw = 2 * SWA_KVW
    ctx_spec = pl.BlockSpec((lc, kvw), lambda b, i: (b, 0))
    return pl.pallas_call(
        functools.partial(_swa_kernel, nb=nb, local=False),
        grid=(batch, nb),
        in_specs=[pl.BlockSpec(memory_space=pltpu.SMEM),
                  pl.BlockSpec((w, SWA_WIDTH), lambda b, i: (b * nb + i, 0)), ctx_spec, ctx_spec],
        out_specs=pl.BlockSpec((w, SWA_WIDTH), lambda b, i: (b * nb + i, 0)),
        out_shape=jax.ShapeDtypeStruct((n, SWA_WIDTH), BF16),
        compiler_params=_cparams(("arbitrary", "arbitrary")),
        name="swa_ctx",
    )(sink, sq, kcd, vcd)


def _outproj_kernel(of_ref, ob_ref, gate_ref, gn_ref, swa_ref, w_ref, x_ref, g1_ref, n2_ref, sc_ref, sh_ref,
                    wr_ref, xo_ref, h2_ref, lg_ref):
    o = of_ref[...] + ob_ref[...]
    parts = []
    for h in range(GLA_HEADS):
        oh = o[:, h * GLA_DV:(h + 1) * GLA_DV]
        ms = jnp.mean(oh * oh, axis=-1, keepdims=True)
        parts.append(oh * lax.rsqrt(ms + NORM_EPS))
    on = jnp.concatenate(parts, axis=-1) * gn_ref[...]
    gate = gate_ref[...].astype(F32)
    gla = on * (gate * jax.nn.sigmoid(gate))
    mix = jnp.concatenate([gla.astype(BF16), swa_ref[...]], axis=-1)
    y = jnp.dot(mix, w_ref[...], preferred_element_type=F32)
    xo = x_ref[...] + g1_ref[...] * y
    xo_ref[...] = xo
    ms = jnp.mean(xo * xo, axis=-1, keepdims=True)
    h2 = (xo * lax.rsqrt(ms + NORM_EPS)) * n2_ref[...]
    h2 = h2 * (1.0 + sc_ref[...]) + sh_ref[...]
    h2_ref[...] = h2
    hi = h2.astype(BF16)
    lo = (h2 - hi.astype(F32)).astype(BF16)
    both = jnp.dot(hi, wr_ref[...], preferred_element_type=F32)
    lg = both[:, :LANES] + both[:, LANES:] + jnp.dot(lo, wr_ref[:, :LANES], preferred_element_type=F32)
    lg_ref[...] = jnp.transpose(lg)[:N_EXPERTS, :]


def _outproj_call(o_f, o_b, gate, gn, swa, w_out_b, x2d, mods5, layer, row_of_tile, n2, wr_cat, tm):
    n, d = x2d.shape
    nt = n // tm

    def mod_spec(k):
        return pl.BlockSpec((None, None, None, 1, d), lambda t: (layer, row_of_tile(t), k, 0, 0))

    return pl.pallas_call(
        _outproj_kernel,
        grid=(nt,),
        in_specs=[
            pl.BlockSpec((tm, GLA_WIDTH), lambda t: (t, 0)),
            pl.BlockSpec((tm, GLA_WIDTH), lambda t: (t, 0)),
            pl.BlockSpec((tm, GLA_WIDTH), lambda t: (t, 0)),
            pl.BlockSpec((1, GLA_WIDTH), lambda t: (0, 0)),
            pl.BlockSpec((tm, SWA_WIDTH), lambda t: (t, 0)),
            pl.BlockSpec((d, d), lambda t: (0, 0)),
            pl.BlockSpec((tm, d), lambda t: (t, 0)),
            mod_spec(2),
            pl.BlockSpec((1, d), lambda t: (0, 0)),
            mod_spec(4), mod_spec(3),
            pl.BlockSpec((d, 2 * LANES), lambda t: (0, 0)),
        ],
        out_specs=[
            pl.BlockSpec((tm, d), lambda t: (t, 0)),
            pl.BlockSpec((tm, d), lambda t: (t, 0)),
            pl.BlockSpec((N_EXPERTS, tm), lambda t: (0, t)),
        ],
        out_shape=[
            jax.ShapeDtypeStruct((n, d), F32),
            jax.ShapeDtypeStruct((n, d), F32),
            jax.ShapeDtypeStruct((N_EXPERTS, n), F32),
        ],
        compiler_params=_cparams(("arbitrary",)),
        name="outproj",
    )(o_f, o_b, gate, gn.reshape(1, GLA_WIDTH), swa, w_out_b, x2d, mods5, n2.reshape(1, d), mods5, mods5, wr_cat)


def _first_index(vals, target):
    idx = jnp.full(target.shape, len(vals) - 1, I32)
    for i in range(len(vals) - 2, -1, -1):
        idx = jnp.where(vals[i] == target, i, idx)
    return idx


def _route_kernel(lg_ref, br_ref, io_ref, wo_ref, cnt_ref, base_scr, *, tn):
    t = pl.program_id(0)

    @pl.when(t == 0)
    def _():
        base_scr[...] = jnp.zeros_like(base_scr)

    s = jax.nn.sigmoid(lg_ref[...])
    sb = s + br_ref[...]
    rows_s = [s[e:e + 1, :] for e in range(N_EXPERTS)]
    rows_b = [sb[e:e + 1, :] for e in range(N_EXPERTS)]
    gscore, gi1, gi2 = [], [], []
    epg = EXPERTS_PER_GROUP
    for g in range(N_GROUPS):
        a = rows_b[g * epg:(g + 1) * epg]
        m1 = functools.reduce(jnp.maximum, a)
        i1 = _first_index(a, m1)
        rest = [jnp.where(i1 == i, -jnp.inf, a[i]) for i in range(epg)]
        m2 = functools.reduce(jnp.maximum, rest)
        i2 = _first_index(rest, m2)
        gscore.append(m1 + m2)
        gi1.append(i1)
        gi2.append(i2)
    gm = functools.reduce(jnp.maximum, gscore)
    gsel = _first_index(gscore, gm)
    i1 = gi1[N_GROUPS - 1]
    i2 = gi2[N_GROUPS - 1]
    for g in range(N_GROUPS - 2, -1, -1):
        i1 = jnp.where(gsel == g, gi1[g], i1)
        i2 = jnp.where(gsel == g, gi2[g], i2)
    idx0 = gsel * epg + i1
    idx1 = gsel * epg + i2
    s0 = jnp.zeros_like(rows_s[0])
    s1 = jnp.zeros_like(rows_s[0])
    for e in range(N_EXPERTS):
        s0 = jnp.where(idx0 == e, rows_s[e], s0)
        s1 = jnp.where(idx1 == e, rows_s[e], s1)
    tot = s0 + s1
    w0 = s0 / tot
    w1 = s1 / tot

    eidx = lax.broadcasted_iota(I32, (N_EXPERTS, tn), 0)
    oh0 = eidx == idx0
    oh1 = eidx == idx1
    oh = jnp.where(oh0 | oh1, 1.0, 0.0)
    rr = lax.broadcasted_iota(I32, (tn, tn), 0)
    cc = lax.broadcasted_iota(I32, (tn, tn), 1)
    upper = jnp.where(rr < cc, 1.0, 0.0).astype(BF16)
    before = jnp.dot(oh.astype(BF16), upper, preferred_element_type=F32)
    pos = before + base_scr[:, 0:1]
    rank0 = jnp.sum(jnp.where(oh0, pos, 0.0), axis=0, keepdims=True)
    rank1 = jnp.sum(jnp.where(oh1, pos, 0.0), axis=0, keepdims=True)
    base_scr[...] = base_scr[...] + jnp.sum(oh, axis=1, keepdims=True)
    zi = jnp.zeros((SUBLANES - 4, tn), I32)
    io_ref[...] = jnp.concatenate([idx0, idx1, rank0.astype(I32), rank1.astype(I32), zi], axis=0)
    wo_ref[...] = jnp.concatenate([w0, w1, jnp.zeros((SUBLANES - 2, tn), F32)], axis=0)
    cnt_ref[...] = base_scr[...].astype(I32)


def _route_call(logits_t, b_router, tn):
    ne, t_all = logits_t.shape
    return pl.pallas_call(
        functools.partial(_route_kernel, tn=tn),
        grid=(t_all // tn,),
        in_specs=[pl.BlockSpec((ne, tn), lambda t: (0, t)), pl.BlockSpec((ne, 1), lambda t: (0, 0))],
        out_specs=[
            pl.BlockSpec((SUBLANES, tn), lambda t: (0, t)),
            pl.BlockSpec((SUBLANES, tn), lambda t: (0, t)),
            pl.BlockSpec((ne, LANES), lambda t: (0, 0)),
        ],
        out_shape=[
            jax.ShapeDtypeStruct((SUBLANES, t_all), I32),
            jax.ShapeDtypeStruct((SUBLANES, t_all), F32),
            jax.ShapeDtypeStruct((ne, LANES), I32),
        ],
        scratch_shapes=[pltpu.VMEM((ne, LANES), F32)],
        compiler_params=_cparams(("arbitrary",)),
        name="route",
    )(logits_t, b_router.reshape(ne, 1))


def _dispatch_kernel(dest_ref, zblk_ref, *refs, tiles, tile, t_all):
    srcs = refs[:len(tiles)]
    xb_hbm, zbuf, sem = refs[len(tiles):]
    i = pl.program_id(0)
    m = MOE_BLOCK

    def zero_copy(j):
        start = pl.multiple_of(zblk_ref[j] * m, m)
        return pltpu.make_async_copy(zbuf, xb_hbm.at[pl.ds(start, m), :], sem)

    @pl.when(i == 0)
    def _():
        zbuf[...] = jnp.zeros_like(zbuf)
        for j in range(2 * N_EXPERTS):
            @pl.when(zblk_ref[j] >= 0)
            def _():
                zero_copy(j).start()
        for j in range(2 * N_EXPERTS):
            @pl.when(zblk_ref[j] >= 0)
            def _():
                zero_copy(j).wait()

    def scatter_rows(h_hbm, base, tok0):
        def row_copy(t, k):
            d = dest_ref[k * t_all + tok0 + t]
            return pltpu.make_async_copy(h_hbm.at[pl.ds(base + t, 1), :], xb_hbm.at[pl.ds(d, 1), :], sem)

        def issue(t, carry):
            row_copy(t, 0).start()
            row_copy(t, 1).start()
            return carry

        lax.fori_loop(0, tile, issue, 0)

        def drain(t, carry):
            row_copy(t, 0).wait()
            row_copy(t, 1).wait()
            return carry

        lax.fori_loop(0, tile, drain, 0)

    first = 0
    for src, ntile in zip(srcs, tiles):
        @pl.when((i >= first) & (i < first + ntile))
        def _(src=src, first=first):
            base = (i - first) * tile
            scatter_rows(src, base, first * tile + base)
        first += ntile


def _dispatch_call(dest_flat, zblk, sources, p_rows, t_all, tile):
    d = sources[0].shape[1]
    tiles = tuple(s.shape[0] // tile for s in sources)
    any_spec = pl.BlockSpec(memory_space=pl.ANY)
    grid_spec = pltpu.PrefetchScalarGridSpec(
        num_scalar_prefetch=2,
        grid=(sum(tiles),),
        in_specs=[any_spec] * len(sources),
        out_specs=any_spec,
        scratch_shapes=[pltpu.VMEM((MOE_BLOCK, d), F32), pltpu.SemaphoreType.DMA],
    )
    return pl.pallas_call(
        functools.partial(_dispatch_kernel, tiles=tiles, tile=tile, t_all=t_all),
        grid_spec=grid_spec,
        out_shape=jax.ShapeDtypeStruct((p_rows, d), F32),
        compiler_params=_cparams(("arbitrary",)),
        name="dispatch",
    )(dest_flat, zblk, *sources)


def _expert_kernel(be_ref, nu_ref, x_ref, wg_ref, wu_ref, wd_ref, o_ref):
    j = pl.program_id(0)

    @pl.when(j < nu_ref[0])
    def _():
        x = x_ref[...].astype(BF16)
        a = jnp.dot(x, wg_ref[...], preferred_element_type=F32)
        u = jnp.dot(x, wu_ref[...], preferred_element_type=F32)
        hmid = (a * jax.nn.sigmoid(a)) * u
        o_ref[...] = jnp.dot(hmid.astype(BF16), wd_ref[...], preferred_element_type=F32)

    @pl.when(j >= nu_ref[0])
    def _():
        o_ref[...] = jnp.zeros_like(o_ref)


def _expert_call(block_e, nused, xb, wg, wu, wd):
    p_rows, d = xb.shape
    m = MOE_BLOCK
    nb = p_rows // m
    de = wg.shape[-1]

    def xmap(j, be, nu):
        return (jnp.minimum(j, nu[0] - 1), 0)

    grid_spec = pltpu.PrefetchScalarGridSpec(
        num_scalar_prefetch=2,
        grid=(nb,),
        in_specs=[
            pl.BlockSpec((m, d), xmap),
            pl.BlockSpec((None, d, de), lambda j, be, nu: (be[j], 0, 0)),
            pl.BlockSpec((None, d, de), lambda j, be, nu: (be[j], 0, 0)),
            pl.BlockSpec((None, de, d), lambda j, be, nu: (be[j], 0, 0)),
        ],
        out_specs=pl.BlockSpec((m, d), lambda j, be, nu: (j, 0)),
    )
    return pl.pallas_call(
        _expert_kernel,
        grid_spec=grid_spec,
        out_shape=jax.ShapeDtypeStruct((p_rows, d), F32),
        compiler_params=_cparams(("arbitrary",)),
        name="experts",
    )(block_e, nused, xb, wg, wu, wd)


def _combine_kernel(*refs, tm, t_off, t_all, final):
    if final:
        dest_ref, yb_hbm, x_ref, w_ref, g2_ref, fn_ref, o_ref, gbuf, sem = refs
    else:
        dest_ref, yb_hbm, x_ref, w_ref, g2_ref, o_ref, gbuf, sem = refs
    i = pl.program_id(0)
    base = i * tm

    def row_copy(t, k):
        d = dest_ref[k * t_all + t_off + base + t]
        return pltpu.make_async_copy(yb_hbm.at[pl.ds(d, 1), :], gbuf.at[k, pl.ds(t, 1), :], sem)

    def issue(t, carry):
        row_copy(t, 0).start()
        row_copy(t, 1).start()
        return carry

    lax.fori_loop(0, tm, issue, 0)

    def drain(t, carry):
        row_copy(t, 0).wait()
        row_copy(t, 1).wait()
        return carry

    lax.fori_loop(0, tm, drain, 0)
    w = w_ref[...]
    y = w[:, 0:1] * gbuf[0] + w[:, 1:2] * gbuf[1]
    xo = x_ref[...] + g2_ref[...] * y
    if final:
        ms = jnp.mean(xo * xo, axis=-1, keepdims=True)
        xo = (xo * lax.rsqrt(ms + NORM_EPS)) * fn_ref[...]
    o_ref[...] = xo


def _combine_call(dest_flat, yb, x2d, w_tok, mods5, layer, row_of_tile, final_g, t_off, t_all, tm):
    n, d = x2d.shape
    final = final_g is not None
    in_specs = [
        pl.BlockSpec(memory_space=pl.ANY),
        pl.BlockSpec((tm, d), lambda t, dr: (t, 0)),
        pl.BlockSpec((tm, TOP_K), lambda t, dr: (t, 0)),
        pl.BlockSpec((None, None, None, 1, d), lambda t, dr: (layer, row_of_tile(t), 5, 0, 0)),
    ]
    args = [yb, x2d, w_tok, mods5]
    if final:
        in_specs.append(pl.BlockSpec((1, d), lambda t, dr: (0, 0)))
        args.append(final_g.reshape(1, d))
    grid_spec = pltpu.PrefetchScalarGridSpec(
        num_scalar_prefetch=1,
        grid=(n // tm,),
        in_specs=in_specs,
        out_specs=pl.BlockSpec((tm, d), lambda t, dr: (t, 0)),
        scratch_shapes=[pltpu.VMEM((TOP_K, tm, d), F32), pltpu.SemaphoreType.DMA],
    )
    return pl.pallas_call(
        functools.partial(_combine_kernel, tm=tm, t_off=t_off, t_all=t_all, final=final),
        grid_spec=grid_spec,
        out_shape=jax.ShapeDtypeStruct((n, d), F32),
        compiler_params=_cparams(("arbitrary",)),
        name="combine",
    )(dest_flat, *args)


def _pack_w_in(w):
    d = w.shape[0]
    pts, acc = [], 0
    for s in IN_SIZES[:-1]:
        acc += s
        pts.append(acc)
    wq, wk, wv, wg, wzf, wzb, wsq, wsk, wsv = jnp.split(w, pts, axis=-1)
    zpad = jnp.zeros((d, LANES - 2 * GLA_GATE_RANK), w.dtype)

    def dup(m):
        hs = [m[:, h * SWA_HD:(h + 1) * SWA_HD] for h in range(SWA_KV_HEADS)]
        return jnp.concatenate([x for h in hs for x in (h, h)], axis=-1)

    return jnp.concatenate([wq, wk, wv, wg, wzf, wzb, zpad, wsq, dup(wsk), dup(wsv)], axis=-1).astype(BF16)


def _pad_up(up, row0):
    out = jnp.zeros((GLA_HEADS // 2, LANES, LANES), F32)
    return out.at[:, row0:row0 + GLA_GATE_RANK, :].set(up.reshape(GLA_GATE_RANK, GLA_HEADS // 2, LANES).transpose(1, 0, 2))


def _rope_tables(seq):
    rows = seq // GRID_W
    row = jnp.repeat(jnp.arange(rows, dtype=F32), GRID_W)
    col = jnp.tile(jnp.arange(GRID_W, dtype=F32), rows)
    n_freq = ROPE_AXIS_DIM // 2
    inv = ROPE_THETA ** (-(jnp.arange(n_freq, dtype=F32) * 2.0 / ROPE_AXIS_DIM))
    ang_r = row[:, None] * inv[None, :]
    ang_c = col[:, None] * inv[None, :]
    cr, sr, cc, sc = jnp.cos(ang_r), jnp.sin(ang_r), jnp.cos(ang_c), jnp.sin(ang_c)
    cos64 = jnp.concatenate([cr, cr, cc, cc], axis=-1)
    sin64 = jnp.concatenate([-sr, sr, -sc, sc], axis=-1)
    reps = LANES // SWA_HD
    return jnp.tile(cos64, (1, reps)), jnp.tile(sin64, (1, reps))


def _tile(n, pref):
    t = pref
    while n % t:
        t //= 2
    return t


def kernel(x, c, ctx, c_ctx, w_ada, b_ada, norm1, norm2, w_in, gla_up_f, gla_bias_f, gla_up_b, gla_bias_b,
           gla_norm, swa_sink, w_out, w_router, b_router, w_gate, w_up, w_down, final_norm):
    batch, seq, d = x.shape
    lc = ctx.shape[1]
    depth = w_ada.shape[0]
    n_l = batch * seq
    n_c = batch * lc
    npair = GLA_HEADS // 2
    assert batch + 1 <= SUBLANES

    tm_l = _tile(seq, 512)
    tm_c = _tile(lc, 256)
    tl_l = _tile(seq, 512)
    tl_c = _tile(lc, 512)

    rows = jnp.zeros((SUBLANES, d), F32).at[:batch].set(c).at[batch].set(c_ctx)
    mods = _ada_call(rows, w_ada, b_ada)
    mods5 = mods.reshape(depth, SUBLANES, 6, 1, d)

    tabs = _rope_tables(seq)
    wr_pad = jnp.zeros((d, LANES), F32).at[:, :N_EXPERTS].set(w_router)
    wr_hi = wr_pad.astype(BF16)
    wr_lo = (wr_pad - wr_hi.astype(F32)).astype(BF16)
    wr_cat = jnp.concatenate([wr_hi, wr_lo], axis=-1)

    def lat_row(tm):
        return lambda t: t // (seq // tm)

    def ctx_row(tm):
        return lambda t: batch

    xl = x.reshape(n_l, d)
    xc = ctx.reshape(n_c, d)
    for i in range(depth):
        last = i == depth - 1
        w_packed = _pack_w_in(w_in[i])
        upf = _pad_up(gla_up_f[i], 0)
        upb = _pad_up(gla_up_b[i], GLA_GATE_RANK)
        bias_f = gla_bias_f[i].reshape(npair, 1, LANES)
        bias_b = gla_bias_b[i].reshape(npair, 1, LANES)
        w_out_b = w_out[i].astype(BF16)

        c_qk, c_v, c_g, c_z, c_sq, c_sk, c_sv = _inproj_call(
            xc, norm1[i], mods5, i, ctx_row(tm_c), w_packed, None, tm_c, lc // tm_c)
        l_qk, l_v, l_g, l_z, l_sq, l_sk, l_sv = _inproj_call(
            xl, norm1[i], mods5, i, lat_row(tm_l), w_packed, tabs, tm_l, seq // tm_l)

        s_zero = jnp.zeros((batch, npair, 2, 2 * GLA_DV, LANES), F32)
        oc_f, oc_b, s_ctx = _gla_call(c_qk, c_v, c_z, upf, upb, bias_f, bias_b, s_zero, batch, lc, tl_c)
        ol_f, ol_b, _ = _gla_call(l_qk, l_v, l_z, upf, upb, bias_f, bias_b, s_ctx, batch, seq, tl_l)

        swa_l = _swa_call(l_sq, l_sk, l_sv, c_sk, c_sv, swa_sink[i], batch, seq, lc)
        xl, h2l, lg_l = _outproj_call(ol_f, ol_b, l_g, gla_norm[i], swa_l, w_out_b, xl, mods5, i, lat_row(tm_l),
                                      norm2[i], wr_cat, tm_l)
        if last:
            logits_t = lg_l
            t_all = n_l
        else:
            swa_c = _swa_ctx_call(c_sq, c_sk, c_sv, swa_sink[i], batch, lc)
            xc, h2c, lg_c = _outproj_call(oc_f, oc_b, c_g, gla_norm[i], swa_c, w_out_b, xc, mods5, i, ctx_row(tm_c),
                                          norm2[i], wr_cat, tm_c)
            logits_t = jnp.concatenate([lg_l, lg_c], axis=1)
            t_all = n_l + n_c

        io, wo, cnt = _route_call(logits_t, b_router, _tile(t_all, 512))
        counts = cnt[:, 0]
        m = MOE_BLOCK
        padded = ((counts + m - 1) // m) * m
        pend = jnp.cumsum(padded)
        pstart = pend - padded
        nb = -(-(t_all * TOP_K) // m) + N_EXPERTS
        p_rows = nb * m
        dest = pstart[io[0:2]] + io[2:4]
        dest_flat = dest.reshape(-1).astype(I32)
        tail = pend[-1] // m + jnp.arange(N_EXPERTS, dtype=I32)
        zblk = jnp.concatenate([jnp.where(padded > 0, pend // m - 1, -1),
                                jnp.where(tail < nb, tail, -1)]).astype(I32)
        block_e = jnp.minimum(jnp.searchsorted(pend, jnp.arange(nb, dtype=I32) * m, side='right'),
                              N_EXPERTS - 1).astype(I32)
        nused = (pend[-1:] // m).astype(I32)
        w_tok = wo[0:2].T

        sources = (h2l,) if last else (h2l, h2c)
        xb = _dispatch_call(dest_flat, zblk, sources, p_rows, t_all, _tile(n_c, _tile(n_l, 512)))
        yb = _expert_call(block_e, nused, xb, w_gate[i].astype(BF16), w_up[i].astype(BF16), w_down[i].astype(BF16))
        tm_y = _tile(seq, 256)
        xl_new = _combine_call(dest_flat, yb, xl, w_tok[:n_l], mods5, i, lat_row(tm_y),
                               final_norm if last else None, 0, t_all, tm_y)
        if not last:
            xc = _combine_call(dest_flat, yb, xc, w_tok[n_l:], mods5, i, ctx_row(0), None, n_l, t_all, _tile(lc, 256))
        xl = xl_new
    return xl.reshape(batch, seq, d)
```

```python
import functools

import jax
import jax.numpy as jnp
from jax import lax
from jax.experimental import pallas as pl
from jax.experimental.pallas import tpu as pltpu

F32 = jnp.float32
BF16 = jnp.bfloat16
I32 = jnp.int32
HIGHEST = lax.Precision.HIGHEST

GRID_W = 64
NORM_EPS = 1e-6
GLA_HEADS = 4
GLA_DK = 64
GLA_DV = 128
GLA_KW = GLA_HEADS * GLA_DK
GLA_WIDTH = GLA_HEADS * GLA_DV
GLA_GATE_RANK = 16
GLA_TAU = 16.0
GLA_CHUNK = 64
SWA_HEADS = 8
SWA_KV_HEADS = 2
SWA_HD = 64
SWA_WIDTH = SWA_HEADS * SWA_HD
SWA_KVW = SWA_KV_HEADS * SWA_HD
WINDOW = 128
ATT_BLOCK = 128
ROPE_THETA = 10000.0
ROPE_AXIS_DIM = SWA_HD // 2
N_EXPERTS = 16
N_GROUPS = 4
EXPERTS_PER_GROUP = N_EXPERTS // N_GROUPS
TOP_K = 2
D_EXPERT = 512
MOE_BLOCK = 256
IN_SIZES = (GLA_KW, GLA_KW, GLA_WIDTH, GLA_WIDTH, GLA_GATE_RANK, GLA_GATE_RANK, SWA_WIDTH, SWA_KVW, SWA_KVW)

LANES = 128
SUBLANES = 8
VMEM_LIMIT = 48 * 1024 * 1024

C_QK = 0
C_V = 512
C_G = 1024
C_Z = 1536
C_SQ = 1664
C_SK = 2176
C_SV = 2432
C_END = 2688
NEG_BIG = -1e30


def _cparams(sem):
    return pltpu.CompilerParams(dimension_semantics=sem, vmem_limit_bytes=VMEM_LIMIT)


def _ada_kernel(a_ref, w_ref, b_ref, o_ref):
    a = a_ref[...]
    act = a * jax.nn.sigmoid(a)
    o_ref[...] = jnp.dot(act, w_ref[...], preferred_element_type=F32, precision=HIGHEST) + b_ref[...]


def _ada_call(rows, w_ada, b_ada):
    depth, d, n6 = w_ada.shape
    tn = 1536
    return pl.pallas_call(
        _ada_kernel,
        grid=(depth, n6 // tn),
        in_specs=[
            pl.BlockSpec((SUBLANES, d), lambda l, j: (0, 0)),
            pl.BlockSpec((None, d, tn), lambda l, j: (l, 0, j)),
            pl.BlockSpec((None, 1, tn), lambda l, j: (l, 0, j)),
        ],
        out_specs=pl.BlockSpec((None, SUBLANES, tn), lambda l, j: (l, 0, j)),
        out_shape=jax.ShapeDtypeStruct((depth, SUBLANES, n6), F32),
        compiler_params=_cparams(("arbitrary", "arbitrary")),
        name="adaln",
    )(rows, w_ada, b_ada.reshape(depth, 1, n6))


def _rope_tile(xj, cos, sin, lane_lo):
    partner = jnp.where(lane_lo, pltpu.roll(xj, LANES - 16, 1), pltpu.roll(xj, 16, 1))
    return xj * cos + partner * sin


def _inproj_kernel(*refs, rope):
    if rope:
        (x_ref, g_ref, sc_ref, sh_ref, w_ref, cos_ref, sin_ref,
         qk_ref, v_ref, gate_ref, z_ref, sq_ref, sk_ref, sv_ref) = refs
    else:
        (x_ref, g_ref, sc_ref, sh_ref, w_ref,
         qk_ref, v_ref, gate_ref, z_ref, sq_ref, sk_ref, sv_ref) = refs
    x = x_ref[...]
    ms = jnp.mean(x * x, axis=-1, keepdims=True)
    h = (x * lax.rsqrt(ms + NORM_EPS)) * g_ref[...]
    h = h * (1.0 + sc_ref[...]) + sh_ref[...]
    hb = h.astype(BF16)

    def proj(a, b):
        return jnp.dot(hb, w_ref[:, a:b], preferred_element_type=F32)

    qk = proj(C_QK, C_V)
    qk_ref[:, :GLA_KW] = (qk[:, :GLA_KW] * (GLA_DK ** -0.5)).astype(qk_ref.dtype)
    qk_ref[:, GLA_KW:] = qk[:, GLA_KW:].astype(qk_ref.dtype)
    v_ref[...] = proj(C_V, C_G).astype(v_ref.dtype)
    gate_ref[...] = proj(C_G, C_Z).astype(gate_ref.dtype)
    z_ref[...] = proj(C_Z, C_SQ)
    sv_ref[...] = proj(C_SV, C_END).astype(sv_ref.dtype)
    sq = proj(C_SQ, C_SK) * (SWA_HD ** -0.5)
    sk = proj(C_SK, C_SV)
    if rope:
        cos = cos_ref[...]
        sin = sin_ref[...]
        lane_lo = (lax.broadcasted_iota(I32, cos.shape, 1) % 32) < 16
        for j in range(SWA_WIDTH // LANES):
            sl = slice(j * LANES, (j + 1) * LANES)
            sq_ref[:, sl] = _rope_tile(sq[:, sl], cos, sin, lane_lo).astype(sq_ref.dtype)
        for j in range(2 * SWA_KVW // LANES):
            sl = slice(j * LANES, (j + 1) * LANES)
            sk_ref[:, sl] = _rope_tile(sk[:, sl], cos, sin, lane_lo).astype(sk_ref.dtype)
    else:
        sq_ref[...] = sq.astype(sq_ref.dtype)
        sk_ref[...] = sk.astype(sk_ref.dtype)


def _inproj_call(x2d, g, mods5, layer, row_of_tile, w_packed, tabs, tm, tiles_per_seq):
    n, d = x2d.shape
    nt = n // tm
    rope = tabs is not None

    def mod_spec(k):
        return pl.BlockSpec((None, None, None, 1, d), lambda t: (layer, row_of_tile(t), k, 0, 0))

    in_specs = [
        pl.BlockSpec((tm, d), lambda t: (t, 0)),
        pl.BlockSpec((1, d), lambda t: (0, 0)),
        mod_spec(1), mod_spec(0),
        pl.BlockSpec((d, C_END), lambda t: (0, 0)),
    ]
    args = [x2d, g.reshape(1, d), mods5, mods5, w_packed]
    if rope:
        in_specs += [pl.BlockSpec((tm, LANES), lambda t: (t % tiles_per_seq, 0))] * 2
        args += [tabs[0], tabs[1]]
    widths = (2 * GLA_KW, GLA_WIDTH, GLA_WIDTH, LANES, SWA_WIDTH, 2 * SWA_KVW, 2 * SWA_KVW)
    dtypes = (BF16, BF16, BF16, F32, BF16, BF16, BF16)
    return pl.pallas_call(
        functools.partial(_inproj_kernel, rope=rope),
        grid=(nt,),
        in_specs=in_specs,
        out_specs=[pl.BlockSpec((tm, w), lambda t: (t, 0)) for w in widths],
        out_shape=[jax.ShapeDtypeStruct((n, w), dt) for w, dt in zip(widths, dtypes)],
        compiler_params=_cparams(("arbitrary",)),
        name="inproj",
    )(*args)


def _log_sigmoid(x):
    return jnp.minimum(x, 0.0) - jnp.log(1.0 + jnp.exp(-jnp.abs(x)))


def _gla_chunk(q_ref, k_ref, v_ref, o_ref, la_ref, s_ref, r0, fwd):
    c = GLA_CHUNK
    rows = pl.ds(r0, c)
    ri = lax.broadcasted_iota(I32, (c, c), 0)
    ci = lax.broadcasted_iota(I32, (c, c), 1)
    tri = ((ci <= ri) if fwd else (ci >= ri)).astype(F32)
    la = la_ref[rows, :]
    b = jnp.dot(tri, la, preferred_element_type=F32, precision=HIGHEST)
    btot = b[c - 1:c, :] if fwd else b[0:1, :]
    q = q_ref[rows, :].astype(F32)
    k = k_ref[rows, :].astype(F32)
    v = v_ref[rows, :]
    q_in = (q * jnp.exp(b)).astype(BF16)
    k_in = k * jnp.exp(-b)
    k_out = (k * jnp.exp(btot - b)).astype(BF16)
    dec = jnp.exp(btot)

    lane = lax.broadcasted_iota(I32, (c, LANES), 1)
    head0 = lane < GLA_DK
    kst = jnp.concatenate([jnp.where(head0, k_in, 0.0), jnp.where(head0, 0.0, k_in)], axis=0).astype(BF16)
    a = lax.dot_general(q_in, kst, (((1,), (1,)), ((), ())), preferred_element_type=F32)
    r2 = lax.broadcasted_iota(I32, (c, 2 * c), 0)
    c2 = lax.broadcasted_iota(I32, (c, 2 * c), 1) % c
    keep = (c2 <= r2) if fwd else (c2 >= r2)
    a = jnp.where(keep, a, 0.0).astype(BF16)
    vf = v.astype(F32)
    vlane = lax.broadcasted_iota(I32, (c, 2 * GLA_DV), 1)
    vhead0 = vlane < GLA_DV
    vbd = jnp.concatenate([jnp.where(vhead0, vf, 0.0), jnp.where(vhead0, 0.0, vf)], axis=0).astype(BF16)
    s_t = s_ref[...]
    o = jnp.dot(a, vbd, preferred_element_type=F32)
    o = o + lax.dot_general(q_in, s_t.astype(BF16), (((1,), (1,)), ((), ())), preferred_element_type=F32)
    o_ref[rows, :] = o
    u_t = lax.dot_general(v, k_out, (((0,), (0,)), ((), ())), preferred_element_type=F32)
    srow = lax.broadcasted_iota(I32, (2 * GLA_DV, LANES), 0) // GLA_DV
    scol = lax.broadcasted_iota(I32, (2 * GLA_DV, LANES), 1) // GLA_DK
    s_ref[...] = s_t * dec + jnp.where(srow == scol, u_t, 0.0)


def _gla_kernel(qf_ref, kf_ref, vf_ref, zf_ref, qb_ref, kb_ref, vb_ref, zb_ref,
                upf_ref, upb_ref, bf_ref, bb_ref, s0_ref,
                of_ref, ob_ref, sfin_ref, s_scr, la_scr, *, nchunk):
    i = pl.program_id(2)
    nt = pl.num_programs(2)

    @pl.when(i == 0)
    def _():
        s_scr[...] = s0_ref[...]

    xf = jnp.dot(zf_ref[...], upf_ref[...], preferred_element_type=F32, precision=HIGHEST) + bf_ref[...]
    la_scr[0] = _log_sigmoid(xf) * (1.0 / GLA_TAU)
    xb = jnp.dot(zb_ref[...], upb_ref[...], preferred_element_type=F32, precision=HIGHEST) + bb_ref[...]
    la_scr[1] = _log_sigmoid(xb) * (1.0 / GLA_TAU)
    for cidx in range(nchunk):
        _gla_chunk(qf_ref, kf_ref, vf_ref, of_ref, la_scr.at[0], s_scr.at[0], cidx * GLA_CHUNK, True)
        _gla_chunk(qb_ref, kb_ref, vb_ref, ob_ref, la_scr.at[1], s_scr.at[1],
                   (nchunk - 1 - cidx) * GLA_CHUNK, False)

    @pl.when(i == nt - 1)
    def _():
        sfin_ref[...] = s_scr[...]


def _gla_call(qk, v, z, upf_pad, upb_pad, bias_f, bias_b, s0, batch, seq, tl):
    n = qk.shape[0]
    nt = seq // tl
    npair = GLA_HEADS // 2

    def fwd_row(b, p, i):
        return b * nt + i

    def bwd_row(b, p, i):
        return b * nt + (nt - 1 - i)

    def specs(row):
        return [
            pl.BlockSpec((tl, LANES), lambda b, p, i: (row(b, p, i), p)),
            pl.BlockSpec((tl, LANES), lambda b, p, i: (row(b, p, i), npair + p)),
            pl.BlockSpec((tl, 2 * GLA_DV), lambda b, p, i: (row(b, p, i), p)),
            pl.BlockSpec((tl, LANES), lambda b, p, i: (row(b, p, i), 0)),
        ]

    in_specs = specs(fwd_row) + specs(bwd_row) + [
        pl.BlockSpec((None, LANES, LANES), lambda b, p, i: (p, 0, 0)),
        pl.BlockSpec((None, LANES, LANES), lambda b, p, i: (p, 0, 0)),
        pl.BlockSpec((None, 1, LANES), lambda b, p, i: (p, 0, 0)),
        pl.BlockSpec((None, 1, LANES), lambda b, p, i: (p, 0, 0)),
        pl.BlockSpec((None, None, 2, 2 * GLA_DV, LANES), lambda b, p, i: (b, p, 0, 0, 0)),
    ]
    out_specs = [
        pl.BlockSpec((tl, 2 * GLA_DV), lambda b, p, i: (fwd_row(b, p, i), p)),
        pl.BlockSpec((tl, 2 * GLA_DV), lambda b, p, i: (bwd_row(b, p, i), p)),
        pl.BlockSpec((None, None, 2, 2 * GLA_DV, LANES), lambda b, p, i: (b, p, 0, 0, 0)),
    ]
    out_shape = [
        jax.ShapeDtypeStruct((n, GLA_WIDTH), F32),
        jax.ShapeDtypeStruct((n, GLA_WIDTH), F32),
        jax.ShapeDtypeStruct((batch, npair, 2, 2 * GLA_DV, LANES), F32),
    ]
    return pl.pallas_call(
        functools.partial(_gla_kernel, nchunk=tl // GLA_CHUNK),
        grid=(batch, npair, nt),
        in_specs=in_specs,
        out_specs=out_specs,
        out_shape=out_shape,
        scratch_shapes=[pltpu.VMEM((2, 2 * GLA_DV, LANES), F32), pltpu.VMEM((2, tl, LANES), F32)],
        compiler_params=_cparams(("arbitrary", "arbitrary", "arbitrary")),
        name="gla",
    )(qk, qk, v, z, qk, qk, v, z, upf_pad, upb_pad, bias_f, bias_b, s0)


def _swa_kernel(*refs, nb, local):
    if local:
        sink_ref, q_ref, kp_ref, kc_ref, kn_ref, vp_ref, vc_ref, vn_ref, kx_ref, vx_ref, o_ref = refs
        kall = jnp.concatenate([kp_ref[...], kc_ref[...], kn_ref[...], kx_ref[...]], axis=0)
        vall = jnp.concatenate([vp_ref[...], vc_ref[...], vn_ref[...], vx_ref[...]], axis=0)
    else:
        sink_ref, q_ref, kx_ref, vx_ref, o_ref = refs
        kall = kx_ref[...]
        vall = vx_ref[...]
    w = ATT_BLOCK
    nk = kall.shape[0]
    if local:
        i = pl.program_id(1)
        r = lax.broadcasted_iota(I32, (w, nk), 0)
        c = lax.broadcasted_iota(I32, (w, nk), 1)
        rel = c - w - r
        valid = (jnp.abs(rel) <= WINDOW) & ((c >= w) | (i > 0)) & ((c < 2 * w) | (i < nb - 1))
        valid = valid | (c >= 3 * w)
    lane = lax.broadcasted_iota(I32, (nk, LANES), 1)
    lo = lane < SWA_HD
    group = SWA_HEADS // SWA_KV_HEADS
    for g in range(SWA_KV_HEADS):
        kd = kall[:, g * LANES:(g + 1) * LANES].astype(F32)
        vd = vall[:, g * LANES:(g + 1) * LANES].astype(F32)
        k_half = (jnp.where(lo, kd, 0.0).astype(BF16), jnp.where(lo, 0.0, kd).astype(BF16))
        v_half = (jnp.where(lo, vd, 0.0).astype(BF16), jnp.where(lo, 0.0, vd).astype(BF16))
        for jj in range(group // 2):
            j = g * (group // 2) + jj
            qt = q_ref[:, j * LANES:(j + 1) * LANES]
            acc = jnp.zeros((w, LANES), F32)
            for half in range(2):
                sk = sink_ref[2 * j + half]
                s = lax.dot_general(qt, k_half[half], (((1,), (1,)), ((), ())), preferred_element_type=F32)
                if local:
                    s = jnp.where(valid, s, NEG_BIG)
                m = jnp.maximum(jnp.max(s, axis=-1, keepdims=True), sk)
                p = jnp.exp(s - m)
                den = jnp.sum(p, axis=-1, keepdims=True) + jnp.exp(sk - m)
                pv = jnp.dot(p.astype(BF16), v_half[half], preferred_element_type=F32)
                acc = acc + pv / den
            o_ref[:, j * LANES:(j + 1) * LANES] = acc.astype(o_ref.dtype)


def _swa_call(sq, skd, svd, kcd, vcd, sink, batch, seq, lc):
    n = sq.shape[0]
    w = ATT_BLOCK
    nb = seq // w
    kvw = 2 * SWA_KVW

    def blk(off):
        return lambda b, i: (b * nb + jnp.clip(i + off, 0, nb - 1), 0)

    kv_specs = [pl.BlockSpec((w, kvw), blk(-1)), pl.BlockSpec((w, kvw), blk(0)), pl.BlockSpec((w, kvw), blk(1))]
    ctx_spec = pl.BlockSpec((lc, kvw), lambda b, i: (b, 0))
    return pl.pallas_call(
        functools.partial(_swa_kernel, nb=nb, local=True),
        grid=(batch, nb),
        in_specs=[pl.BlockSpec(memory_space=pltpu.SMEM), pl.BlockSpec((w, SWA_WIDTH), blk(0))]
        + kv_specs + kv_specs + [ctx_spec, ctx_spec],
        out_specs=pl.BlockSpec((w, SWA_WIDTH), blk(0)),
        out_shape=jax.ShapeDtypeStruct((n, SWA_WIDTH), BF16),
        compiler_params=_cparams(("arbitrary", "arbitrary")),
        name="swa",
    )(sink, sq, skd, skd, skd, svd, svd, svd, kcd, vcd)


def _swa_ctx_call(sq, kcd, vcd, sink, batch, lc):
    n = sq.shape[0]
    w = ATT_BLOCK
    nb = lc // w
    kvw = 2 * SWA_KVW
    ctx_spec = pl.BlockSpec((lc, kvw), lambda b, i: (b, 0))
    return pl.pallas_call(
        functools.partial(_swa_kernel, nb=nb, local=False),
        grid=(batch, nb),
        in_specs=[pl.BlockSpec(memory_space=pltpu.SMEM),
                  pl.BlockSpec((w, SWA_WIDTH), lambda b, i: (b * nb + i, 0)), ctx_spec, ctx_spec],
        out_specs=pl.BlockSpec((w, SWA_WIDTH), lambda b, i: (b * nb + i, 0)),
        out_shape=jax.ShapeDtypeStruct((n, SWA_WIDTH), BF16),
        compiler_params=_cparams(("arbitrary", "arbitrary")),
        name="swa_ctx",
    )(sink, sq, kcd, vcd)


def _outproj_kernel(of_ref, ob_ref, gate_ref, gn_ref, swa_ref, w_ref, x_ref, g1_ref, n2_ref, sc_ref, sh_ref,
                    wr_ref, xo_ref, h2_ref, lg_ref):
    o = of_ref[...] + ob_ref[...]
    parts = []
    for h in range(GLA_HEADS):
        oh = o[:, h * GLA_DV:(h + 1) * GLA_DV]
        ms = jnp.mean(oh * oh, axis=-1, keepdims=True)
        parts.append(oh * lax.rsqrt(ms + NORM_EPS))
    on = jnp.concatenate(parts, axis=-1) * gn_ref[...]
    gate = gate_ref[...].astype(F32)
    gla = on * (gate * jax.nn.sigmoid(gate))
    mix = jnp.concatenate([gla.astype(BF16), swa_ref[...]], axis=-1)
    y = jnp.dot(mix, w_ref[...], preferred_element_type=F32)
    xo = x_ref[...] + g1_ref[...] * y
    xo_ref[...] = xo
    ms = jnp.mean(xo * xo, axis=-1, keepdims=True)
    h2 = (xo * lax.rsqrt(ms + NORM_EPS)) * n2_ref[...]
    h2 = h2 * (1.0 + sc_ref[...]) + sh_ref[...]
    h2_ref[...] = h2
    hi = h2.astype(BF16)
    lo = (h2 - hi.astype(F32)).astype(BF16)
    both = jnp.dot(hi, wr_ref[...], preferred_element_type=F32)
    lg = both[:, :LANES] + both[:, LANES:] + jnp.dot(lo, wr_ref[:, :LANES], preferred_element_type=F32)
    lg_ref[...] = jnp.transpose(lg)[:N_EXPERTS, :]


def _outproj_call(o_f, o_b, gate, gn, swa, w_out_b, x2d, mods5, layer, row_of_tile, n2, wr_cat, tm):
    n, d = x2d.shape
    nt = n // tm

    def mod_spec(k):
        return pl.BlockSpec((None, None, None, 1, d), lambda t: (layer, row_of_tile(t), k, 0, 0))

    return pl.pallas_call(
        _outproj_kernel,
        grid=(nt,),
        in_specs=[
            pl.BlockSpec((tm, GLA_WIDTH), lambda t: (t, 0)),
            pl.BlockSpec((tm, GLA_WIDTH), lambda t: (t, 0)),
            pl.BlockSpec((tm, GLA_WIDTH), lambda t: (t, 0)),
            pl.BlockSpec((1, GLA_WIDTH), lambda t: (0, 0)),
            pl.BlockSpec((tm, SWA_WIDTH), lambda t: (t, 0)),
            pl.BlockSpec((d, d), lambda t: (0, 0)),
            pl.BlockSpec((tm, d), lambda t: (t, 0)),
            mod_spec(2),
            pl.BlockSpec((1, d), lambda t: (0, 0)),
            mod_spec(4), mod_spec(3),
            pl.BlockSpec((d, 2 * LANES), lambda t: (0, 0)),
        ],
        out_specs=[
            pl.BlockSpec((tm, d), lambda t: (t, 0)),
            pl.BlockSpec((tm, d), lambda t: (t, 0)),
            pl.BlockSpec((N_EXPERTS, tm), lambda t: (0, t)),
        ],
        out_shape=[
            jax.ShapeDtypeStruct((n, d), F32),
            jax.ShapeDtypeStruct((n, d), F32),
            jax.ShapeDtypeStruct((N_EXPERTS, n), F32),
        ],
        compiler_params=_cparams(("arbitrary",)),
        name="outproj",
    )(o_f, o_b, gate, gn.reshape(1, GLA_WIDTH), swa, w_out_b, x2d, mods5, n2.reshape(1, d), mods5, mods5, wr_cat)


def _first_index(vals, target):
    idx = jnp.full(target.shape, len(vals) - 1, I32)
    for i in range(len(vals) - 2, -1, -1):
        idx = jnp.where(vals[i] == target, i, idx)
    return idx


def _route_kernel(lg_ref, br_ref, io_ref, wo_ref, cnt_ref, base_scr, start_scr, *, tn):
    phase = pl.program_id(0)
    t = pl.program_id(1)

    @pl.when((phase == 0) & (t == 0))
    def _():
        base_scr[...] = jnp.zeros_like(base_scr)
        start_scr[...] = jnp.zeros_like(start_scr)

    @pl.when((phase == 1) & (t == 0))
    def _():
        counts = base_scr[...]
        cnt_ref[...] = counts.astype(I32)
        padded = jnp.floor((counts + (MOE_BLOCK - 1)) * (1.0 / MOE_BLOCK)) * MOE_BLOCK
        run = jnp.zeros((1, LANES), F32)
        for e in range(N_EXPERTS):
            start_scr[e:e + 1, :] = run
            run = run + padded[e:e + 1, :]
        base_scr[...] = jnp.zeros_like(base_scr)

    s = jax.nn.sigmoid(lg_ref[...])
    sb = s + br_ref[...]
    rows_s = [s[e:e + 1, :] for e in range(N_EXPERTS)]
    rows_b = [sb[e:e + 1, :] for e in range(N_EXPERTS)]
    gscore, gi1, gi2 = [], [], []
    epg = EXPERTS_PER_GROUP
    for g in range(N_GROUPS):
        a = rows_b[g * epg:(g + 1) * epg]
        m1 = functools.reduce(jnp.maximum, a)
        i1 = _first_index(a, m1)
        rest = [jnp.where(i1 == i, -jnp.inf, a[i]) for i in range(epg)]
        m2 = functools.reduce(jnp.maximum, rest)
        i2 = _first_index(rest, m2)
        gscore.append(m1 + m2)
        gi1.append(i1)
        gi2.append(i2)
    gm = functools.reduce(jnp.maximum, gscore)
    gsel = _first_index(gscore, gm)
    i1 = gi1[N_GROUPS - 1]
    i2 = gi2[N_GROUPS - 1]
    for g in range(N_GROUPS - 2, -1, -1):
        i1 = jnp.where(gsel == g, gi1[g], i1)
        i2 = jnp.where(gsel == g, gi2[g], i2)
    idx0 = gsel * epg + i1
    idx1 = gsel * epg + i2
    s0 = jnp.zeros_like(rows_s[0])
    s1 = jnp.zeros_like(rows_s[0])
    for e in range(N_EXPERTS):
        s0 = jnp.where(idx0 == e, rows_s[e], s0)
        s1 = jnp.where(idx1 == e, rows_s[e], s1)
    tot = s0 + s1
    w0 = s0 / tot
    w1 = s1 / tot

    eidx = lax.broadcasted_iota(I32, (N_EXPERTS, tn), 0)
    oh0 = eidx == idx0
    oh1 = eidx == idx1
    oh = jnp.where(oh0 | oh1, 1.0, 0.0)
    rr = lax.broadcasted_iota(I32, (tn, tn), 0)
    cc = lax.broadcasted_iota(I32, (tn, tn), 1)
    upper = jnp.where(rr < cc, 1.0, 0.0).astype(BF16)
    before = jnp.dot(oh.astype(BF16), upper, preferred_element_type=F32)
    pos = before + base_scr[:, 0:1] + start_scr[:, 0:1]
    dest0 = jnp.sum(jnp.where(oh0, pos, 0.0), axis=0, keepdims=True)
    dest1 = jnp.sum(jnp.where(oh1, pos, 0.0), axis=0, keepdims=True)
    base_scr[...] = base_scr[...] + jnp.sum(oh, axis=1, keepdims=True)
    zi = jnp.zeros((SUBLANES - 4, tn), I32)
    io_ref[...] = jnp.concatenate([idx0, idx1, dest0.astype(I32), dest1.astype(I32), zi], axis=0)
    wo_ref[...] = jnp.concatenate([w0, w1, jnp.zeros((SUBLANES - 2, tn), F32)], axis=0)


def _route_call(logits_t, b_router, tn):
    ne, t_all = logits_t.shape
    tok_out = pl.BlockSpec((SUBLANES, tn), lambda p, t: (0, t * p))
    return pl.pallas_call(
        functools.partial(_route_kernel, tn=tn),
        grid=(2, t_all // tn),
        in_specs=[pl.BlockSpec((ne, tn), lambda p, t: (0, t)), pl.BlockSpec((ne, 1), lambda p, t: (0, 0))],
        out_specs=[tok_out, tok_out, pl.BlockSpec((ne, LANES), lambda p, t: (0, 0))],
        out_shape=[
            jax.ShapeDtypeStruct((SUBLANES, t_all), I32),
            jax.ShapeDtypeStruct((SUBLANES, t_all), F32),
            jax.ShapeDtypeStruct((ne, LANES), I32),
        ],
        scratch_shapes=[pltpu.VMEM((ne, LANES), F32), pltpu.VMEM((ne, LANES), F32)],
        compiler_params=_cparams(("arbitrary", "arbitrary")),
        name="route",
    )(logits_t, b_router.reshape(ne, 1))


def _dispatch_kernel(dest_ref, zblk_ref, *refs, tiles, tile, t_all):
    srcs = refs[:len(tiles)]
    xb_hbm, zbuf, sem = refs[len(tiles):]
    i = pl.program_id(0)
    m = MOE_BLOCK

    def zero_copy(j):
        start = pl.multiple_of(zblk_ref[j] * m, m)
        return pltpu.make_async_copy(zbuf, xb_hbm.at[pl.ds(start, m), :], sem)

    @pl.when(i == 0)
    def _():
        zbuf[...] = jnp.zeros_like(zbuf)
        for j in range(2 * N_EXPERTS):
            @pl.when(zblk_ref[j] >= 0)
            def _():
                zero_copy(j).start()
        for j in range(2 * N_EXPERTS):
            @pl.when(zblk_ref[j] >= 0)
            def _():
                zero_copy(j).wait()

    def scatter_rows(h_ref, tok0):
        def row_copy(t, k):
            d = dest_ref[k * t_all + tok0 + t]
            return pltpu.make_async_copy(h_ref.at[pl.ds(t, 1), :], xb_hbm.at[pl.ds(d, 1), :], sem)

        def issue(t, carry):
            row_copy(t, 0).start()
            row_copy(t, 1).start()
            return carry

        lax.fori_loop(0, tile, issue, 0, unroll=8)

        def drain(t, carry):
            row_copy(t, 0).wait()
            row_copy(t, 1).wait()
            return carry

        lax.fori_loop(0, tile, drain, 0, unroll=8)

    first = 0
    for src, ntile in zip(srcs, tiles):
        @pl.when((i >= first) & (i < first + ntile))
        def _(src=src, first=first):
            scatter_rows(src, i * tile)
        first += ntile


def _dispatch_call(dest_flat, zblk, sources, p_rows, t_all, tile):
    d = sources[0].shape[1]
    tiles = tuple(s.shape[0] // tile for s in sources)
    firsts = tuple(sum(tiles[:k]) for k in range(len(tiles)))

    def src_spec(first, ntile):
        return pl.BlockSpec((tile, d), lambda i, dr, zb: (jnp.clip(i - first, 0, ntile - 1), 0))

    grid_spec = pltpu.PrefetchScalarGridSpec(
        num_scalar_prefetch=2,
        grid=(sum(tiles),),
        in_specs=[src_spec(f, n) for f, n in zip(firsts, tiles)],
        out_specs=pl.BlockSpec(memory_space=pl.ANY),
        scratch_shapes=[pltpu.VMEM((MOE_BLOCK, d), F32), pltpu.SemaphoreType.DMA],
    )
    return pl.pallas_call(
        functools.partial(_dispatch_kernel, tiles=tiles, tile=tile, t_all=t_all),
        grid_spec=grid_spec,
        out_shape=jax.ShapeDtypeStruct((p_rows, d), F32),
        compiler_params=_cparams(("arbitrary",)),
        name="dispatch",
    )(dest_flat, zblk, *sources)


def _expert_kernel(be_ref, nu_ref, x_ref, wg_ref, wu_ref, wd_ref, o_ref, wg_b, wu_b, wd_b):
    j = pl.program_id(0)

    @pl.when((j == 0) | (be_ref[j] != be_ref[jnp.maximum(j - 1, 0)]))
    def _():
        wg_b[...] = wg_ref[...].astype(BF16)
        wu_b[...] = wu_ref[...].astype(BF16)
        wd_b[...] = wd_ref[...].astype(BF16)

    @pl.when(j < nu_ref[0])
    def _():
        x = x_ref[...].astype(BF16)
        a = jnp.dot(x, wg_b[...], preferred_element_type=F32)
        u = jnp.dot(x, wu_b[...], preferred_element_type=F32)
        hmid = (a * jax.nn.sigmoid(a)) * u
        o_ref[...] = jnp.dot(hmid.astype(BF16), wd_b[...], preferred_element_type=F32)

    @pl.when(j >= nu_ref[0])
    def _():
        o_ref[...] = jnp.zeros_like(o_ref)


def _expert_call(block_e, nused, xb, wg, wu, wd):
    p_rows, d = xb.shape
    m = MOE_BLOCK
    nb = p_rows // m
    de = wg.shape[-1]

    def xmap(j, be, nu):
        return (jnp.minimum(j, nu[0] - 1), 0)

    grid_spec = pltpu.PrefetchScalarGridSpec(
        num_scalar_prefetch=2,
        grid=(nb,),
        in_specs=[
            pl.BlockSpec((m, d), xmap),
            pl.BlockSpec((None, d, de), lambda j, be, nu: (be[j], 0, 0)),
            pl.BlockSpec((None, d, de), lambda j, be, nu: (be[j], 0, 0)),
            pl.BlockSpec((None, de, d), lambda j, be, nu: (be[j], 0, 0)),
        ],
        out_specs=pl.BlockSpec((m, d), lambda j, be, nu: (j, 0)),
        scratch_shapes=[pltpu.VMEM((d, de), BF16), pltpu.VMEM((d, de), BF16), pltpu.VMEM((de, d), BF16)],
    )
    return pl.pallas_call(
        _expert_kernel,
        grid_spec=grid_spec,
        out_shape=jax.ShapeDtypeStruct((p_rows, d), F32),
        compiler_params=_cparams(("arbitrary",)),
        name="experts",
    )(block_e, nused, xb, wg, wu, wd)


def _combine_kernel(*refs, tm, t_off, t_all, final):
    if final:
        dest_ref, yb_hbm, x_ref, w_ref, g2_ref, fn_ref, o_ref, gbuf, sem = refs
    else:
        dest_ref, yb_hbm, x_ref, w_ref, g2_ref, o_ref, gbuf, sem = refs
    i = pl.program_id(0)
    base = i * tm

    def row_copy(t, k):
        d = dest_ref[k * t_all + t_off + base + t]
        return pltpu.make_async_copy(yb_hbm.at[pl.ds(d, 1), :], gbuf.at[k, pl.ds(t, 1), :], sem)

    def issue(t, carry):
        row_copy(t, 0).start()
        row_copy(t, 1).start()
        return carry

    lax.fori_loop(0, tm, issue, 0)

    def drain(t, carry):
        row_copy(t, 0).wait()
        row_copy(t, 1).wait()
        return carry

    lax.fori_loop(0, tm, drain, 0)
    w = w_ref[...]
    y = w[:, 0:1] * gbuf[0] + w[:, 1:2] * gbuf[1]
    xo = x_ref[...] + g2_ref[...] * y
    if final:
        ms = jnp.mean(xo * xo, axis=-1, keepdims=True)
        xo = (xo * lax.rsqrt(ms + NORM_EPS)) * fn_ref[...]
    o_ref[...] = xo


def _combine_call(dest_flat, yb, x2d, w_tok, mods5, layer, row_of_tile, final_g, t_off, t_all, tm):
    n, d = x2d.shape
    final = final_g is not None
    in_specs = [
        pl.BlockSpec(memory_space=pl.ANY),
        pl.BlockSpec((tm, d), lambda t, dr: (t, 0)),
        pl.BlockSpec((tm, TOP_K), lambda t, dr: (t, 0)),
        pl.BlockSpec((None, None, None, 1, d), lambda t, dr: (layer, row_of_tile(t), 5, 0, 0)),
    ]
    args = [yb, x2d, w_tok, mods5]
    if final:
        in_specs.append(pl.BlockSpec((1, d), lambda t, dr: (0, 0)))
        args.append(final_g.reshape(1, d))
    grid_spec = pltpu.PrefetchScalarGridSpec(
        num_scalar_prefetch=1,
        grid=(n // tm,),
        in_specs=in_specs,
        out_specs=pl.BlockSpec((tm, d), lambda t, dr: (t, 0)),
        scratch_shapes=[pltpu.VMEM((TOP_K, tm, d), F32), pltpu.SemaphoreType.DMA],
    )
    return pl.pallas_call(
        functools.partial(_combine_kernel, tm=tm, t_off=t_off, t_all=t_all, final=final),
        grid_spec=grid_spec,
        out_shape=jax.ShapeDtypeStruct((n, d), F32),
        compiler_params=_cparams(("arbitrary",)),
        name="combine",
    )(dest_flat, *args)


def _pack_w_in(w):
    d = w.shape[0]
    pts, acc = [], 0
    for s in IN_SIZES[:-1]:
        acc += s
        pts.append(acc)
    wq, wk, wv, wg, wzf, wzb, wsq, wsk, wsv = jnp.split(w, pts, axis=-1)
    zpad = jnp.zeros((d, LANES - 2 * GLA_GATE_RANK), w.dtype)

    def dup(m):
        hs = [m[:, h * SWA_HD:(h + 1) * SWA_HD] for h in range(SWA_KV_HEADS)]
        return jnp.concatenate([x for h in hs for x in (h, h)], axis=-1)

    return jnp.concatenate([wq, wk, wv, wg, wzf, wzb, zpad, wsq, dup(wsk), dup(wsv)], axis=-1).astype(BF16)


def _pad_up(up, row0):
    out = jnp.zeros((GLA_HEADS // 2, LANES, LANES), F32)
    return out.at[:, row0:row0 + GLA_GATE_RANK, :].set(up.reshape(GLA_GATE_RANK, GLA_HEADS // 2, LANES).transpose(1, 0, 2))


def _rope_tables(seq):
    rows = seq // GRID_W
    row = jnp.repeat(jnp.arange(rows, dtype=F32), GRID_W)
    col = jnp.tile(jnp.arange(GRID_W, dtype=F32), rows)
    n_freq = ROPE_AXIS_DIM // 2
    inv = ROPE_THETA ** (-(jnp.arange(n_freq, dtype=F32) * 2.0 / ROPE_AXIS_DIM))
    ang_r = row[:, None] * inv[None, :]
    ang_c = col[:, None] * inv[None, :]
    cr, sr, cc, sc = jnp.cos(ang_r), jnp.sin(ang_r), jnp.cos(ang_c), jnp.sin(ang_c)
    cos64 = jnp.concatenate([cr, cr, cc, cc], axis=-1)
    sin64 = jnp.concatenate([-sr, sr, -sc, sc], axis=-1)
    reps = LANES // SWA_HD
    return jnp.tile(cos64, (1, reps)), jnp.tile(sin64, (1, reps))


def _tile(n, pref):
    t = pref
    while n % t:
        t //= 2
    return t


def kernel(x, c, ctx, c_ctx, w_ada, b_ada, norm1, norm2, w_in, gla_up_f, gla_bias_f, gla_up_b, gla_bias_b,
           gla_norm, swa_sink, w_out, w_router, b_router, w_gate, w_up, w_down, final_norm):
    batch, seq, d = x.shape
    lc = ctx.shape[1]
    depth = w_ada.shape[0]
    n_l = batch * seq
    n_c = batch * lc
    npair = GLA_HEADS // 2
    assert batch + 1 <= SUBLANES

    tm_l = _tile(seq, 512)
    tm_c = _tile(lc, 256)
    tl_l = _tile(seq, 512)
    tl_c = _tile(lc, 512)

    rows = jnp.zeros((SUBLANES, d), F32).at[:batch].set(c).at[batch].set(c_ctx)
    mods = _ada_call(rows, w_ada, b_ada)
    mods5 = mods.reshape(depth, SUBLANES, 6, 1, d)

    tabs = _rope_tables(seq)
    wr_pad = jnp.zeros((d, LANES), F32).at[:, :N_EXPERTS].set(w_router)
    wr_hi = wr_pad.astype(BF16)
    wr_lo = (wr_pad - wr_hi.astype(F32)).astype(BF16)
    wr_cat = jnp.concatenate([wr_hi, wr_lo], axis=-1)

    def lat_row(tm):
        return lambda t: t // (seq // tm)

    def ctx_row(tm):
        return lambda t: batch

    xl = x.reshape(n_l, d)
    xc = ctx.reshape(n_c, d)
    for i in range(depth):
        last = i == depth - 1
        w_packed = _pack_w_in(w_in[i])
        upf = _pad_up(gla_up_f[i], 0)
        upb = _pad_up(gla_up_b[i], GLA_GATE_RANK)
        bias_f = gla_bias_f[i].reshape(npair, 1, LANES)
        bias_b = gla_bias_b[i].reshape(npair, 1, LANES)
        w_out_b = w_out[i].astype(BF16)

        c_qk, c_v, c_g, c_z, c_sq, c_sk, c_sv = _inproj_call(
            xc, norm1[i], mods5, i, ctx_row(tm_c), w_packed, None, tm_c, lc // tm_c)
        l_qk, l_v, l_g, l_z, l_sq, l_sk, l_sv = _inproj_call(
            xl, norm1[i], mods5, i, lat_row(tm_l), w_packed, tabs, tm_l, seq // tm_l)

        s_zero = jnp.zeros((batch, npair, 2, 2 * GLA_DV, LANES), F32)
        oc_f, oc_b, s_ctx = _gla_call(c_qk, c_v, c_z, upf, upb, bias_f, bias_b, s_zero, batch, lc, tl_c)
        ol_f, ol_b, _ = _gla_call(l_qk, l_v, l_z, upf, upb, bias_f, bias_b, s_ctx, batch, seq, tl_l)

        swa_l = _swa_call(l_sq, l_sk, l_sv, c_sk, c_sv, swa_sink[i], batch, seq, lc)
        xl, h2l, lg_l = _outproj_call(ol_f, ol_b, l_g, gla_norm[i], swa_l, w_out_b, xl, mods5, i, lat_row(tm_l),
                                      norm2[i], wr_cat, tm_l)
        if last:
            logits_t = lg_l
            t_all = n_l
        else:
            swa_c = _swa_ctx_call(c_sq, c_sk, c_sv, swa_sink[i], batch, lc)
            xc, h2c, lg_c = _outproj_call(oc_f, oc_b, c_g, gla_norm[i], swa_c, w_out_b, xc, mods5, i, ctx_row(tm_c),
                                          norm2[i], wr_cat, tm_c)
            logits_t = jnp.concatenate([lg_l, lg_c], axis=1)
            t_all = n_l + n_c

        io, wo, cnt = _route_call(logits_t, b_router, _tile(t_all, 512))
        counts = cnt[:, 0]
        m = MOE_BLOCK
        padded = ((counts + m - 1) // m) * m
        pend = jnp.cumsum(padded)
        nb = -(-(t_all * TOP_K) // m) + N_EXPERTS
        p_rows = nb * m
        dest_flat = io[2:4].reshape(-1)
        nused = (pend[-1:] // m).astype(I32)
        tail = nused + jnp.arange(N_EXPERTS, dtype=I32)
        zblk = jnp.concatenate([jnp.where(padded > 0, pend // m - 1, -1),
                                jnp.where(tail < nb, tail, -1)]).astype(I32)
        blk_row = jnp.minimum(jnp.arange(nb, dtype=I32), nused - 1) * m
        block_e = jnp.sum((pend[None, :] <= blk_row[:, None]).astype(I32), axis=1)
        w_tok = wo[0:2].T

        sources = (h2l,) if last else (h2l, h2c)
        xb = _dispatch_call(dest_flat, zblk, sources, p_rows, t_all, _tile(n_c, _tile(n_l, 512)))
        yb = _expert_call(block_e, nused, xb, w_gate[i], w_up[i], w_down[i])
        tm_y = _tile(seq, 256)
        xl_new = _combine_call(dest_flat, yb, xl, w_tok[:n_l], mods5, i, lat_row(tm_y),
                               final_norm if last else None, 0, t_all, tm_y)
        if not last:
            xc = _combine_call(dest_flat, yb, xc, w_tok[n_l:], mods5, i, ctx_row(0), None, n_l, t_all, _tile(lc, 256))
        xl = xl_new
    return xl.reshape(batch, seq, d)
```

```python
import functools

import jax
import jax.numpy as jnp
from jax import lax
from jax.experimental import pallas as pl
from jax.experimental.pallas import tpu as pltpu

F32 = jnp.float32
BF16 = jnp.bfloat16
I32 = jnp.int32
U32 = jnp.uint32
HIGHEST = lax.Precision.HIGHEST

GRID_W = 64
NORM_EPS = 1e-6
GLA_HEADS = 4
GLA_DK = 64
GLA_DV = 128
GLA_KW = GLA_HEADS * GLA_DK
GLA_WIDTH = GLA_HEADS * GLA_DV
GLA_GATE_RANK = 16
GLA_TAU = 16.0
GLA_CHUNK = 64
SWA_HEADS = 8
SWA_KV_HEADS = 2
SWA_HD = 64
SWA_WIDTH = SWA_HEADS * SWA_HD
SWA_KVW = SWA_KV_HEADS * SWA_HD
WINDOW = 128
ATT_BLOCK = 128
ROPE_THETA = 10000.0
ROPE_AXIS_DIM = SWA_HD // 2
N_EXPERTS = 16
N_GROUPS = 4
EXPERTS_PER_GROUP = N_EXPERTS // N_GROUPS
TOP_K = 2
D_EXPERT = 512
MOE_BLOCK = 256
IN_SIZES = (GLA_KW, GLA_KW, GLA_WIDTH, GLA_WIDTH, GLA_GATE_RANK, GLA_GATE_RANK, SWA_WIDTH, SWA_KVW, SWA_KVW)

LANES = 128
SUBLANES = 8
VMEM_LIMIT = 48 * 1024 * 1024

C_QK = 0
C_V = 512
C_G = 1024
C_Z = 1536
C_SQ = 1664
C_SK = 2176
C_SV = 2304
C_END = 2432
SV_LANES = 4 * LANES
NEG_BIG = -1e30


def _cparams(sem):
    return pltpu.CompilerParams(dimension_semantics=sem, vmem_limit_bytes=VMEM_LIMIT)


def _pack_bf16_pairs(x):
    n = x.shape[1] // 2
    lo = lax.bitcast_convert_type(x[:, :n].astype(BF16).astype(F32), U32)
    hi = lax.bitcast_convert_type(x[:, n:].astype(BF16).astype(F32), U32)
    return (lo >> 16) | hi


def _unpack_bf16_pairs(w):
    lo = lax.bitcast_convert_type(w << 16, F32)
    hi = lax.bitcast_convert_type(w & jnp.uint32(0xFFFF0000), F32)
    return jnp.concatenate([lo.astype(BF16), hi.astype(BF16)], axis=-1)


def _ada_kernel(a_ref, w_ref, b_ref, o_ref):
    a = a_ref[...]
    act = a * jax.nn.sigmoid(a)
    o_ref[...] = jnp.dot(act, w_ref[...], preferred_element_type=F32, precision=HIGHEST) + b_ref[...]


def _ada_call(rows, w_ada, b_ada):
    depth, d, n6 = w_ada.shape
    tn = 1536
    return pl.pallas_call(
        _ada_kernel,
        grid=(depth, n6 // tn),
        in_specs=[
            pl.BlockSpec((SUBLANES, d), lambda l, j: (0, 0)),
            pl.BlockSpec((None, d, tn), lambda l, j: (l, 0, j)),
            pl.BlockSpec((None, 1, tn), lambda l, j: (l, 0, j)),
        ],
        out_specs=pl.BlockSpec((None, SUBLANES, tn), lambda l, j: (l, 0, j)),
        out_shape=jax.ShapeDtypeStruct((depth, SUBLANES, n6), F32),
        compiler_params=_cparams(("arbitrary", "arbitrary")),
        name="adaln",
    )(rows, w_ada, b_ada.reshape(depth, 1, n6))


def _rope_tile(xj, cos, sin, lane_lo):
    partner = jnp.where(lane_lo, pltpu.roll(xj, LANES - 16, 1), pltpu.roll(xj, 16, 1))
    return xj * cos + partner * sin


def _inproj_kernel(*refs, rope):
    if rope:
        (x_ref, g_ref, sc_ref, sh_ref, w_ref, cos_ref, sin_ref,
         qk_ref, v_ref, gate_ref, z_ref, sq_ref, sk_ref, sv_ref) = refs
    else:
        (x_ref, g_ref, sc_ref, sh_ref, w_ref,
         qk_ref, v_ref, gate_ref, z_ref, sq_ref, sk_ref, sv_ref) = refs
    x = x_ref[...]
    ms = jnp.mean(x * x, axis=-1, keepdims=True)
    h = (x * lax.rsqrt(ms + NORM_EPS)) * g_ref[...]
    h = h * (1.0 + sc_ref[...]) + sh_ref[...]
    hb = h.astype(BF16)

    def proj(a, b):
        return jnp.dot(hb, w_ref[:, a:b], preferred_element_type=F32)

    qk = proj(C_QK, C_V)
    qk_ref[:, :GLA_KW] = (qk[:, :GLA_KW] * (GLA_DK ** -0.5)).astype(qk_ref.dtype)
    qk_ref[:, GLA_KW:] = qk[:, GLA_KW:].astype(qk_ref.dtype)
    v_ref[...] = proj(C_V, C_G).astype(v_ref.dtype)
    gate_ref[...] = proj(C_G, C_Z).astype(gate_ref.dtype)
    z_ref[...] = proj(C_Z, C_SQ)
    sq = proj(C_SQ, C_SK) * (SWA_HD ** -0.5)
    sk = proj(C_SK, C_SV)
    sv = proj(C_SV, C_END)
    if rope:
        cos = cos_ref[...]
        sin = sin_ref[...]
        lane_lo = (lax.broadcasted_iota(I32, cos.shape, 1) % 32) < 16
        for j in range(SWA_WIDTH // LANES):
            sl = slice(j * LANES, (j + 1) * LANES)
            sq_ref[:, sl] = _rope_tile(sq[:, sl], cos, sin, lane_lo).astype(sq_ref.dtype)
        sk = _rope_tile(sk, cos, sin, lane_lo)
    else:
        sq_ref[...] = sq.astype(sq_ref.dtype)
    first = lax.broadcasted_iota(I32, sk.shape, 1) < SWA_HD
    sk_sw = pltpu.roll(sk, SWA_HD, 1)
    sk_ref[:, :LANES] = jnp.where(first, sk, sk_sw).astype(sk_ref.dtype)
    sk_ref[:, LANES:] = jnp.where(first, sk_sw, sk).astype(sk_ref.dtype)
    sv_sw = pltpu.roll(sv, SWA_HD, 1)
    sv_ref[:, 0 * LANES:1 * LANES] = jnp.where(first, sv, 1.0).astype(sv_ref.dtype)
    sv_ref[:, 1 * LANES:2 * LANES] = jnp.where(first, 1.0, sv_sw).astype(sv_ref.dtype)
    sv_ref[:, 2 * LANES:3 * LANES] = jnp.where(first, sv_sw, 1.0).astype(sv_ref.dtype)
    sv_ref[:, 3 * LANES:4 * LANES] = jnp.where(first, 1.0, sv).astype(sv_ref.dtype)


def _inproj_call(x2d, g, mods5, layer, row_of_tile, w_packed, tabs, tm, tiles_per_seq):
    n, d = x2d.shape
    nt = n // tm
    rope = tabs is not None

    def mod_spec(k):
        return pl.BlockSpec((None, None, None, 1, d), lambda t: (layer, row_of_tile(t), k, 0, 0))

    in_specs = [
        pl.BlockSpec((tm, d), lambda t: (t, 0)),
        pl.BlockSpec((1, d), lambda t: (0, 0)),
        mod_spec(1), mod_spec(0),
        pl.BlockSpec((d, C_END), lambda t: (0, 0)),
    ]
    args = [x2d, g.reshape(1, d), mods5, mods5, w_packed]
    if rope:
        in_specs += [pl.BlockSpec((tm, LANES), lambda t: (t % tiles_per_seq, 0))] * 2
        args += [tabs[0], tabs[1]]
    widths = (2 * GLA_KW, GLA_WIDTH, GLA_WIDTH, LANES, SWA_WIDTH, 2 * SWA_KVW, SV_LANES)
    dtypes = (BF16, BF16, BF16, F32, BF16, BF16, BF16)
    return pl.pallas_call(
        functools.partial(_inproj_kernel, rope=rope),
        grid=(nt,),
        in_specs=in_specs,
        out_specs=[pl.BlockSpec((tm, w), lambda t: (t, 0)) for w in widths],
        out_shape=[jax.ShapeDtypeStruct((n, w), dt) for w, dt in zip(widths, dtypes)],
        compiler_params=_cparams(("arbitrary",)),
        name="inproj",
    )(*args)


def _log_sigmoid(x):
    return jnp.minimum(x, 0.0) - jnp.log(1.0 + jnp.exp(-jnp.abs(x)))


def _gla_prepare(q_ref, k_ref, v_ref, z_ref, up_ref, bias_ref, tri_ref, scr):
    qin_s, km0_s, km1_s, kout_s, vm0_s, vm1_s, dec_s = scr[:7]
    tl = q_ref.shape[0]
    x = jnp.dot(z_ref[...], up_ref[...], preferred_element_type=F32, precision=HIGHEST) + bias_ref[...]
    la = _log_sigmoid(x) * (1.0 / GLA_TAU)
    hi = la.astype(BF16)
    lo = (la - hi.astype(F32)).astype(BF16)
    sums = jnp.dot(tri_ref[...], jnp.concatenate([hi, lo], axis=1), preferred_element_type=F32)
    b = sums[:tl, :LANES] + sums[:tl, LANES:]
    btot = sums[tl:, :LANES] + sums[tl:, LANES:]
    q = q_ref[...].astype(F32)
    k = k_ref[...].astype(F32)
    qin_s[...] = (q * jnp.exp(b)).astype(BF16)
    k_in = k * jnp.exp(-b)
    head0 = lax.broadcasted_iota(I32, (tl, LANES), 1) < GLA_DK
    km0_s[...] = jnp.where(head0, k_in, 0.0).astype(BF16)
    km1_s[...] = jnp.where(head0, 0.0, k_in).astype(BF16)
    kout_s[...] = (k * jnp.exp(btot - b)).astype(BF16)
    dec_s[...] = jnp.exp(btot)
    vf = v_ref[...].astype(F32)
    vhead0 = lax.broadcasted_iota(I32, (tl, 2 * GLA_DV), 1) < GLA_DV
    vm0_s[...] = jnp.where(vhead0, vf, 0.0).astype(BF16)
    vm1_s[...] = jnp.where(vhead0, 0.0, vf).astype(BF16)


def _gla_increment(v_ref, scr, r0):
    kout_s, u_s = scr[3], scr[7]
    c = GLA_CHUNK
    rows = pl.ds(r0, c)
    u_t = lax.dot_general(v_ref[rows, :], kout_s[rows, :], (((0,), (0,)), ((), ())),
                          preferred_element_type=F32)
    srow = lax.broadcasted_iota(I32, (2 * GLA_DV, LANES), 0) // GLA_DV
    scol = lax.broadcasted_iota(I32, (2 * GLA_DV, LANES), 1) // GLA_DK
    u_s[r0 // c] = jnp.where(srow == scol, u_t, 0.0)


def _gla_states(scr, s_ref, chunk_order):
    dec_s, u_s, sprev_s = scr[6], scr[7], scr[8]
    s_t = s_ref[...]
    for cidx in chunk_order:
        sprev_s[cidx] = s_t.astype(BF16)
        r0 = cidx * GLA_CHUNK
        s_t = s_t * dec_s[r0:r0 + 1, :] + u_s[cidx]
    s_ref[...] = s_t


def _gla_output(o_ref, scr, r0, fwd):
    qin_s, km0_s, km1_s, _, vm0_s, vm1_s, _, _, sprev_s = scr
    c = GLA_CHUNK
    rows = pl.ds(r0, c)
    q_in = qin_s[rows, :]
    kst = jnp.concatenate([km0_s[rows, :], km1_s[rows, :]], axis=0)
    a = lax.dot_general(q_in, kst, (((1,), (1,)), ((), ())), preferred_element_type=F32)
    r2 = lax.broadcasted_iota(I32, (c, 2 * c), 0)
    c2 = lax.broadcasted_iota(I32, (c, 2 * c), 1) % c
    keep = (c2 <= r2) if fwd else (c2 >= r2)
    a = jnp.where(keep, a, 0.0).astype(BF16)
    vbd = jnp.concatenate([vm0_s[rows, :], vm1_s[rows, :]], axis=0)
    o = jnp.dot(a, vbd, preferred_element_type=F32)
    o = o + lax.dot_general(q_in, sprev_s[r0 // c], (((1,), (1,)), ((), ())), preferred_element_type=F32)
    o_ref[rows, :] = o.astype(o_ref.dtype)


def _gla_kernel(qf_ref, kf_ref, vf_ref, zf_ref, qb_ref, kb_ref, vb_ref, zb_ref,
                upf_ref, upb_ref, bf_ref, bb_ref, trif_ref, trib_ref, s0_ref,
                of_ref, ob_ref, sfin_ref, sf_scr, sb_scr, *scr, nchunk):
    i = pl.program_id(2)
    nt = pl.num_programs(2)
    scr_f, scr_b = scr[:len(scr) // 2], scr[len(scr) // 2:]

    @pl.when(i == 0)
    def _():
        sf_scr[...] = s0_ref[0]
        sb_scr[...] = s0_ref[1]

    _gla_prepare(qf_ref, kf_ref, vf_ref, zf_ref, upf_ref, bf_ref, trif_ref, scr_f)
    _gla_prepare(qb_ref, kb_ref, vb_ref, zb_ref, upb_ref, bb_ref, trib_ref, scr_b)
    for cidx in range(nchunk):
        _gla_increment(vf_ref, scr_f, cidx * GLA_CHUNK)
        _gla_increment(vb_ref, scr_b, cidx * GLA_CHUNK)
    _gla_states(scr_f, sf_scr, range(nchunk))
    _gla_states(scr_b, sb_scr, range(nchunk - 1, -1, -1))
    for cidx in range(nchunk):
        _gla_output(of_ref, scr_f, cidx * GLA_CHUNK, True)
        _gla_output(ob_ref, scr_b, cidx * GLA_CHUNK, False)

    @pl.when(i == nt - 1)
    def _():
        sfin_ref[0] = sf_scr[...]
        sfin_ref[1] = sb_scr[...]


def _gla_call(qk, v, z, upf_pad, upb_pad, bias_f, bias_b, s0, batch, seq, tl):
    n = qk.shape[0]
    nt = seq // tl
    npair = GLA_HEADS // 2

    def fwd_row(b, p, i):
        return b * nt + i

    def bwd_row(b, p, i):
        return b * nt + (nt - 1 - i)

    def specs(row):
        return [
            pl.BlockSpec((tl, LANES), lambda b, p, i: (row(b, p, i), p)),
            pl.BlockSpec((tl, LANES), lambda b, p, i: (row(b, p, i), npair + p)),
            pl.BlockSpec((tl, 2 * GLA_DV), lambda b, p, i: (row(b, p, i), p)),
            pl.BlockSpec((tl, LANES), lambda b, p, i: (row(b, p, i), 0)),
        ]

    in_specs = specs(fwd_row) + specs(bwd_row) + [
        pl.BlockSpec((None, LANES, LANES), lambda b, p, i: (p, 0, 0)),
        pl.BlockSpec((None, LANES, LANES), lambda b, p, i: (p, 0, 0)),
        pl.BlockSpec((None, 1, LANES), lambda b, p, i: (p, 0, 0)),
        pl.BlockSpec((None, 1, LANES), lambda b, p, i: (p, 0, 0)),
        pl.BlockSpec((None, 2 * tl, tl), lambda b, p, i: (0, 0, 0)),
        pl.BlockSpec((None, 2 * tl, tl), lambda b, p, i: (1, 0, 0)),
        pl.BlockSpec((None, None, 2, 2 * GLA_DV, LANES), lambda b, p, i: (b, p, 0, 0, 0)),
    ]
    ri = jnp.arange(tl)[:, None]
    ci = jnp.arange(tl)[None, :]
    same = (ri // GLA_CHUNK) == (ci // GLA_CHUNK)
    tri = jnp.stack([jnp.concatenate([same & (ci <= ri), same], axis=0),
                     jnp.concatenate([same & (ci >= ri), same], axis=0)]).astype(BF16)
    dir_scratch = [
        pltpu.VMEM((tl, LANES), BF16), pltpu.VMEM((tl, LANES), BF16), pltpu.VMEM((tl, LANES), BF16),
        pltpu.VMEM((tl, LANES), BF16), pltpu.VMEM((tl, 2 * GLA_DV), BF16), pltpu.VMEM((tl, 2 * GLA_DV), BF16),
        pltpu.VMEM((tl, LANES), F32),
        pltpu.VMEM((tl // GLA_CHUNK, 2 * GLA_DV, LANES), F32),
        pltpu.VMEM((tl // GLA_CHUNK, 2 * GLA_DV, LANES), BF16),
    ]
    out_specs = [
        pl.BlockSpec((tl, 2 * GLA_DV), lambda b, p, i: (fwd_row(b, p, i), p)),
        pl.BlockSpec((tl, 2 * GLA_DV), lambda b, p, i: (bwd_row(b, p, i), p)),
        pl.BlockSpec((None, None, 2, 2 * GLA_DV, LANES), lambda b, p, i: (b, p, 0, 0, 0)),
    ]
    out_shape = [
        jax.ShapeDtypeStruct((n, GLA_WIDTH), BF16),
        jax.ShapeDtypeStruct((n, GLA_WIDTH), BF16),
        jax.ShapeDtypeStruct((batch, npair, 2, 2 * GLA_DV, LANES), F32),
    ]
    return pl.pallas_call(
        functools.partial(_gla_kernel, nchunk=tl // GLA_CHUNK),
        grid=(batch, npair, nt),
        in_specs=in_specs,
        out_specs=out_specs,
        out_shape=out_shape,
        scratch_shapes=[pltpu.VMEM((2 * GLA_DV, LANES), F32), pltpu.VMEM((2 * GLA_DV, LANES), F32)]
        + dir_scratch + dir_scratch,
        compiler_params=_cparams(("arbitrary", "arbitrary", "arbitrary")),
        name="gla",
    )(qk, qk, v, z, qk, qk, v, z, upf_pad, upb_pad, bias_f, bias_b, tri, tri, s0)


def _swa_kernel(*refs, nb, local):
    if local:
        sink_ref, q_ref, kp_ref, kc_ref, kn_ref, vp_ref, vc_ref, vn_ref, kx_ref, vx_ref, o_ref = refs
        kall = jnp.concatenate([kp_ref[...], kc_ref[...], kn_ref[...], kx_ref[...]], axis=0)
        vall = jnp.concatenate([vp_ref[...], vc_ref[...], vn_ref[...], vx_ref[...]], axis=0)
    else:
        sink_ref, q_ref, kx_ref, vx_ref, o_ref = refs
        kall = kx_ref[...]
        vall = vx_ref[...]
    w = ATT_BLOCK
    nk = kall.shape[0]
    r2 = lax.broadcasted_iota(I32, (2 * w, w), 0) % w
    c2 = lax.broadcasted_iota(I32, (2 * w, w), 1)
    if local:
        i = pl.program_id(1)
        bias_prev = jnp.where((c2 >= r2) & (i > 0), 0.0, NEG_BIG)
        bias_next = jnp.where((c2 <= r2) & (i < nb - 1), 0.0, NEG_BIG)
    first = c2 < SWA_HD
    top = lax.broadcasted_iota(I32, (2 * w, 1), 0) < w
    scores = []
    for g in range(SWA_KV_HEADS):
        kd = kall[:, g * LANES:(g + 1) * LANES]
        qs = jnp.concatenate([q_ref[:, (2 * g) * LANES:(2 * g + 1) * LANES],
                              q_ref[:, (2 * g + 1) * LANES:(2 * g + 2) * LANES]], axis=0).astype(F32)
        for half in range(2):
            qm = (jnp.where(first, qs, 0.0) if half == 0 else jnp.where(first, 0.0, qs)).astype(BF16)
            s = lax.dot_general(qm, kd, (((1,), (1,)), ((), ())), preferred_element_type=F32)
            if local:
                s = jnp.concatenate([s[:, :w] + bias_prev, s[:, w:2 * w], s[:, 2 * w:3 * w] + bias_next,
                                     s[:, 3 * w:]], axis=1)
            scores.append(s)
    for g in range(SWA_KV_HEADS):
        outs = []
        for half in range(2):
            s = scores[2 * g + half]
            sk = jnp.where(top, sink_ref[4 * g + half], sink_ref[4 * g + 2 + half])
            m = jnp.maximum(jnp.max(s, axis=-1, keepdims=True), sk)
            p = jnp.exp((s - m).astype(BF16))
            va = vall[:, (2 * g + half) * LANES:(2 * g + half + 1) * LANES]
            acc = jnp.dot(p, va, preferred_element_type=F32)
            den = pltpu.roll(acc, SWA_HD, 1) + jnp.exp(sk - m)
            outs.append(acc / den)
        out = jnp.where(first, outs[0], outs[1])
        o_ref[:, (2 * g) * LANES:(2 * g + 1) * LANES] = out[:w].astype(o_ref.dtype)
        o_ref[:, (2 * g + 1) * LANES:(2 * g + 2) * LANES] = out[w:].astype(o_ref.dtype)


def _swa_call(sq, skd, svd, kcd, vcd, sink, batch, seq, lc):
    n = sq.shape[0]
    w = ATT_BLOCK
    nb = seq // w
    kvw = 2 * SWA_KVW

    def blk(off):
        return lambda b, i: (b * nb + jnp.clip(i + off, 0, nb - 1), 0)

    def kv_specs(width):
        return [pl.BlockSpec((w, width), blk(-1)), pl.BlockSpec((w, width), blk(0)), pl.BlockSpec((w, width), blk(1))]

    def ctx_spec(width):
        return pl.BlockSpec((lc, width), lambda b, i: (b, 0))

    return pl.pallas_call(
        functools.partial(_swa_kernel, nb=nb, local=True),
        grid=(batch, nb),
        in_specs=[pl.BlockSpec(memory_space=pltpu.SMEM), pl.BlockSpec((w, SWA_WIDTH), blk(0))]
        + kv_specs(kvw) + kv_specs(SV_LANES) + [ctx_spec(kvw), ctx_spec(SV_LANES)],
        out_specs=pl.BlockSpec((w, SWA_WIDTH), blk(0)),
        out_shape=jax.ShapeDtypeStruct((n, SWA_WIDTH), BF16),
        compiler_params=_cparams(("arbitrary", "arbitrary")),
        name="swa",
    )(sink, sq, skd, skd, skd, svd, svd, svd, kcd, vcd)


def _swa_ctx_call(sq, kcd, vcd, sink, batch, lc):
    n = sq.shape[0]
    w = ATT_BLOCK
    nb = lc // w
    kvw = 2 * SWA_KVW
    return pl.pallas_call(
        functools.partial(_swa_kernel, nb=nb, local=False),
        grid=(batch, nb),
        in_specs=[pl.BlockSpec(memory_space=pltpu.SMEM),
                  pl.BlockSpec((w, SWA_WIDTH), lambda b, i: (b * nb + i, 0)),
                  pl.BlockSpec((lc, kvw), lambda b, i: (b, 0)),
                  pl.BlockSpec((lc, SV_LANES), lambda b, i: (b, 0))],
        out_specs=pl.BlockSpec((w, SWA_WIDTH), lambda b, i: (b * nb + i, 0)),
        out_shape=jax.ShapeDtypeStruct((n, SWA_WIDTH), BF16),
        compiler_params=_cparams(("arbitrary", "arbitrary")),
        name="swa_ctx",
    )(sink, sq, kcd, vcd)


def _outproj_kernel(of_ref, ob_ref, gate_ref, gn_ref, swa_ref, w_ref, x_ref, g1_ref, n2_ref, sc_ref, sh_ref,
                    wr_ref, xo_ref, h2_ref, lg_ref):
    o = of_ref[...].astype(F32) + ob_ref[...].astype(F32)
    parts = []
    for h in range(GLA_HEADS):
        oh = o[:, h * GLA_DV:(h + 1) * GLA_DV]
        ms = jnp.mean(oh * oh, axis=-1, keepdims=True)
        parts.append(oh * lax.rsqrt(ms + NORM_EPS))
    on = jnp.concatenate(parts, axis=-1) * gn_ref[...]
    gate = gate_ref[...].astype(F32)
    gla = on * (gate * jax.nn.sigmoid(gate))
    mix = jnp.concatenate([gla.astype(BF16), swa_ref[...]], axis=-1)
    y = jnp.dot(mix, w_ref[...], preferred_element_type=F32)
    xo = x_ref[...] + g1_ref[...] * y
    xo_ref[...] = xo
    ms = jnp.mean(xo * xo, axis=-1, keepdims=True)
    h2 = (xo * lax.rsqrt(ms + NORM_EPS)) * n2_ref[...]
    h2 = h2 * (1.0 + sc_ref[...]) + sh_ref[...]
    h2_ref[...] = _pack_bf16_pairs(h2)
    hi = h2.astype(BF16)
    lo = (h2 - hi.astype(F32)).astype(BF16)
    both = jnp.dot(hi, wr_ref[...], preferred_element_type=F32)
    lg = both[:, :LANES] + both[:, LANES:] + jnp.dot(lo, wr_ref[:, :LANES], preferred_element_type=F32)
    lg_ref[...] = jnp.transpose(lg)[:N_EXPERTS, :]


def _outproj_call(o_f, o_b, gate, gn, swa, w_out_b, x2d, mods5, layer, row_of_tile, n2, wr_cat, tm):
    n, d = x2d.shape
    nt = n // tm

    def mod_spec(k):
        return pl.BlockSpec((None, None, None, 1, d), lambda t: (layer, row_of_tile(t), k, 0, 0))

    return pl.pallas_call(
        _outproj_kernel,
        grid=(nt,),
        in_specs=[
            pl.BlockSpec((tm, GLA_WIDTH), lambda t: (t, 0)),
            pl.BlockSpec((tm, GLA_WIDTH), lambda t: (t, 0)),
            pl.BlockSpec((tm, GLA_WIDTH), lambda t: (t, 0)),
            pl.BlockSpec((1, GLA_WIDTH), lambda t: (0, 0)),
            pl.BlockSpec((tm, SWA_WIDTH), lambda t: (t, 0)),
            pl.BlockSpec((d, d), lambda t: (0, 0)),
            pl.BlockSpec((tm, d), lambda t: (t, 0)),
            mod_spec(2),
            pl.BlockSpec((1, d), lambda t: (0, 0)),
            mod_spec(4), mod_spec(3),
            pl.BlockSpec((d, 2 * LANES), lambda t: (0, 0)),
        ],
        out_specs=[
            pl.BlockSpec((tm, d), lambda t: (t, 0)),
            pl.BlockSpec((tm, d // 2), lambda t: (t, 0)),
            pl.BlockSpec((N_EXPERTS, tm), lambda t: (0, t)),
        ],
        out_shape=[
            jax.ShapeDtypeStruct((n, d), F32),
            jax.ShapeDtypeStruct((n, d // 2), U32),
            jax.ShapeDtypeStruct((N_EXPERTS, n), F32),
        ],
        compiler_params=_cparams(("arbitrary",)),
        name="outproj",
    )(o_f, o_b, gate, gn.reshape(1, GLA_WIDTH), swa, w_out_b, x2d, mods5, n2.reshape(1, d), mods5, mods5, wr_cat)


def _first_index(vals, target):
    idx = jnp.full(target.shape, len(vals) - 1, I32)
    for i in range(len(vals) - 2, -1, -1):
        idx = jnp.where(vals[i] == target, i, idx)
    return idx


def _route_kernel(lg_ref, br_ref, io_ref, wo_ref, cnt_ref, base_scr, start_scr, *, tn):
    phase = pl.program_id(0)
    t = pl.program_id(1)

    @pl.when((phase == 0) & (t == 0))
    def _():
        base_scr[...] = jnp.zeros_like(base_scr)
        start_scr[...] = jnp.zeros_like(start_scr)

    @pl.when((phase == 1) & (t == 0))
    def _():
        counts = base_scr[...]
        cnt_ref[...] = counts.astype(I32)
        padded = jnp.floor((counts + (MOE_BLOCK - 1)) * (1.0 / MOE_BLOCK)) * MOE_BLOCK
        run = jnp.zeros((1, LANES), F32)
        for e in range(N_EXPERTS):
            start_scr[e:e + 1, :] = run
            run = run + padded[e:e + 1, :]
        base_scr[...] = jnp.zeros_like(base_scr)

    s = jax.nn.sigmoid(lg_ref[...])
    sb = s + br_ref[...]
    rows_s = [s[e:e + 1, :] for e in range(N_EXPERTS)]
    rows_b = [sb[e:e + 1, :] for e in range(N_EXPERTS)]
    gscore, gi1, gi2 = [], [], []
    epg = EXPERTS_PER_GROUP
    for g in range(N_GROUPS):
        a = rows_b[g * epg:(g + 1) * epg]
        m1 = functools.reduce(jnp.maximum, a)
        i1 = _first_index(a, m1)
        rest = [jnp.where(i1 == i, -jnp.inf, a[i]) for i in range(epg)]
        m2 = functools.reduce(jnp.maximum, rest)
        i2 = _first_index(rest, m2)
        gscore.append(m1 + m2)
        gi1.append(i1)
        gi2.append(i2)
    gm = functools.reduce(jnp.maximum, gscore)
    gsel = _first_index(gscore, gm)
    i1 = gi1[N_GROUPS - 1]
    i2 = gi2[N_GROUPS - 1]
    for g in range(N_GROUPS - 2, -1, -1):
        i1 = jnp.where(gsel == g, gi1[g], i1)
        i2 = jnp.where(gsel == g, gi2[g], i2)
    idx0 = gsel * epg + i1
    idx1 = gsel * epg + i2
    s0 = jnp.zeros_like(rows_s[0])
    s1 = jnp.zeros_like(rows_s[0])
    for e in range(N_EXPERTS):
        s0 = jnp.where(idx0 == e, rows_s[e], s0)
        s1 = jnp.where(idx1 == e, rows_s[e], s1)
    tot = s0 + s1
    w0 = s0 / tot
    w1 = s1 / tot

    eidx = lax.broadcasted_iota(I32, (N_EXPERTS, tn), 0)
    oh0 = eidx == idx0
    oh1 = eidx == idx1
    oh = jnp.where(oh0 | oh1, 1.0, 0.0)
    rr = lax.broadcasted_iota(I32, (tn, tn), 0)
    cc = lax.broadcasted_iota(I32, (tn, tn), 1)
    upper = jnp.where(rr < cc, 1.0, 0.0).astype(BF16)
    before = jnp.dot(oh.astype(BF16), upper, preferred_element_type=F32)
    pos = before + base_scr[:, 0:1] + start_scr[:, 0:1]
    dest0 = jnp.sum(jnp.where(oh0, pos, 0.0), axis=0, keepdims=True)
    dest1 = jnp.sum(jnp.where(oh1, pos, 0.0), axis=0, keepdims=True)
    base_scr[...] = base_scr[...] + jnp.sum(oh, axis=1, keepdims=True)
    zi = jnp.zeros((SUBLANES - 4, tn), I32)
    io_ref[...] = jnp.concatenate([idx0, idx1, dest0.astype(I32), dest1.astype(I32), zi], axis=0)
    wo_ref[...] = jnp.concatenate([w0, w1, jnp.zeros((SUBLANES - 2, tn), F32)], axis=0)


def _route_call(logits_t, b_router, tn):
    ne, t_all = logits_t.shape
    tok_out = pl.BlockSpec((SUBLANES, tn), lambda p, t: (0, t * p))
    return pl.pallas_call(
        functools.partial(_route_kernel, tn=tn),
        grid=(2, t_all // tn),
        in_specs=[pl.BlockSpec((ne, tn), lambda p, t: (0, t)), pl.BlockSpec((ne, 1), lambda p, t: (0, 0))],
        out_specs=[tok_out, tok_out, pl.BlockSpec((ne, LANES), lambda p, t: (0, 0))],
        out_shape=[
            jax.ShapeDtypeStruct((SUBLANES, t_all), I32),
            jax.ShapeDtypeStruct((SUBLANES, t_all), F32),
            jax.ShapeDtypeStruct((ne, LANES), I32),
        ],
        scratch_shapes=[pltpu.VMEM((ne, LANES), F32), pltpu.VMEM((ne, LANES), F32)],
        compiler_params=_cparams(("arbitrary", "arbitrary")),
        name="route",
    )(logits_t, b_router.reshape(ne, 1))


def _dispatch_kernel(dest_ref, zblk_ref, *refs, tiles, tile, t_all):
    srcs = refs[:len(tiles)]
    xb_hbm, zbuf, sem = refs[len(tiles):]
    i = pl.program_id(0)
    m = MOE_BLOCK

    def zero_copy(j):
        start = pl.multiple_of(zblk_ref[j] * m, m)
        return pltpu.make_async_copy(zbuf, xb_hbm.at[pl.ds(start, m), :], sem)

    @pl.when(i == 0)
    def _():
        zbuf[...] = jnp.zeros_like(zbuf)
        for j in range(2 * N_EXPERTS):
            @pl.when(zblk_ref[j] >= 0)
            def _():
                zero_copy(j).start()
        for j in range(2 * N_EXPERTS):
            @pl.when(zblk_ref[j] >= 0)
            def _():
                zero_copy(j).wait()

    def scatter_rows(h_ref, tok0):
        def row_copy(t, k):
            d = dest_ref[k * t_all + tok0 + t]
            return pltpu.make_async_copy(h_ref.at[pl.ds(t, 1), :], xb_hbm.at[pl.ds(d, 1), :], sem)

        def issue(t, carry):
            row_copy(t, 0).start()
            row_copy(t, 1).start()
            return carry

        lax.fori_loop(0, tile, issue, 0, unroll=8)

        def drain(t, carry):
            row_copy(t, 0).wait()
            row_copy(t, 1).wait()
            return carry

        lax.fori_loop(0, tile, drain, 0, unroll=8)

    first = 0
    for src, ntile in zip(srcs, tiles):
        @pl.when((i >= first) & (i < first + ntile))
        def _(src=src, first=first):
            scatter_rows(src, i * tile)
        first += ntile


def _dispatch_call(dest_flat, zblk, sources, p_rows, t_all, tile):
    d = sources[0].shape[1]
    tiles = tuple(s.shape[0] // tile for s in sources)
    firsts = tuple(sum(tiles[:k]) for k in range(len(tiles)))

    def src_spec(first, ntile):
        return pl.BlockSpec((tile, d), lambda i, dr, zb: (jnp.clip(i - first, 0, ntile - 1), 0))

    grid_spec = pltpu.PrefetchScalarGridSpec(
        num_scalar_prefetch=2,
        grid=(sum(tiles),),
        in_specs=[src_spec(f, n) for f, n in zip(firsts, tiles)],
        out_specs=pl.BlockSpec(memory_space=pl.ANY),
        scratch_shapes=[pltpu.VMEM((MOE_BLOCK, d), sources[0].dtype), pltpu.SemaphoreType.DMA],
    )
    return pl.pallas_call(
        functools.partial(_dispatch_kernel, tiles=tiles, tile=tile, t_all=t_all),
        grid_spec=grid_spec,
        out_shape=jax.ShapeDtypeStruct((p_rows, d), sources[0].dtype),
        compiler_params=_cparams(("arbitrary",)),
        name="dispatch",
    )(dest_flat, zblk, *sources)


def _expert_kernel(be_ref, nu_ref, x_ref, wg_ref, wu_ref, wd_ref, o_ref, wg_b, wu_b, wd_b):
    j = pl.program_id(0)

    @pl.when((j == 0) | (be_ref[j] != be_ref[jnp.maximum(j - 1, 0)]))
    def _():
        wg_b[...] = wg_ref[...].astype(BF16)
        wu_b[...] = wu_ref[...].astype(BF16)
        wd_b[...] = wd_ref[...].astype(BF16)

    @pl.when(j < nu_ref[0])
    def _():
        x = _unpack_bf16_pairs(x_ref[...])
        a = jnp.dot(x, wg_b[...], preferred_element_type=F32)
        u = jnp.dot(x, wu_b[...], preferred_element_type=F32)
        hmid = (a * jax.nn.sigmoid(a)) * u
        y = jnp.dot(hmid.astype(BF16), wd_b[...], preferred_element_type=F32)
        o_ref[...] = _pack_bf16_pairs(y)

    @pl.when(j >= nu_ref[0])
    def _():
        o_ref[...] = jnp.zeros_like(o_ref)


def _expert_call(block_e, nused, xb, wg, wu, wd, layer):
    p_rows, dh = xb.shape
    d = 2 * dh
    m = MOE_BLOCK
    nb = p_rows // m
    de = wg.shape[-1]

    def xmap(j, be, nu):
        return (jnp.minimum(j, nu[0] - 1), 0)

    def wmap(j, be, nu):
        return (layer, be[j], 0, 0)

    grid_spec = pltpu.PrefetchScalarGridSpec(
        num_scalar_prefetch=2,
        grid=(nb,),
        in_specs=[
            pl.BlockSpec((m, dh), xmap),
            pl.BlockSpec((None, None, d, de), wmap),
            pl.BlockSpec((None, None, d, de), wmap),
            pl.BlockSpec((None, None, de, d), wmap),
        ],
        out_specs=pl.BlockSpec((m, dh), lambda j, be, nu: (j, 0)),
        scratch_shapes=[pltpu.VMEM((d, de), BF16), pltpu.VMEM((d, de), BF16), pltpu.VMEM((de, d), BF16)],
    )
    return pl.pallas_call(
        _expert_kernel,
        grid_spec=grid_spec,
        out_shape=jax.ShapeDtypeStruct((p_rows, dh), U32),
        compiler_params=_cparams(("arbitrary",)),
        name="experts",
    )(block_e, nused, xb, wg, wu, wd)


def _combine_kernel(*refs, tm, t_off, t_all, final):
    if final:
        dest_ref, yb_hbm, x_ref, w_ref, g2_ref, fn_ref, o_ref, gbuf, sem = refs
    else:
        dest_ref, yb_hbm, x_ref, w_ref, g2_ref, o_ref, gbuf, sem = refs
    i = pl.program_id(0)
    base = i * tm

    def row_copy(t, k):
        d = dest_ref[k * t_all + t_off + base + t]
        return pltpu.make_async_copy(yb_hbm.at[pl.ds(d, 1), :], gbuf.at[k, pl.ds(t, 1), :], sem)

    def issue(t, carry):
        row_copy(t, 0).start()
        row_copy(t, 1).start()
        return carry

    lax.fori_loop(0, tm, issue, 0, unroll=8)
    for k in range(TOP_K):
        pltpu.make_async_copy(yb_hbm.at[pl.ds(0, tm), :], gbuf.at[k], sem).wait()
    w = w_ref[...]
    y = (w[:, 0:1] * _unpack_bf16_pairs(gbuf[0]).astype(F32)
         + w[:, 1:2] * _unpack_bf16_pairs(gbuf[1]).astype(F32))
    xo = x_ref[...] + g2_ref[...] * y
    if final:
        ms = jnp.mean(xo * xo, axis=-1, keepdims=True)
        xo = (xo * lax.rsqrt(ms + NORM_EPS)) * fn_ref[...]
    o_ref[...] = xo


def _combine_call(dest_flat, yb, x2d, w_tok, mods5, layer, row_of_tile, final_g, t_off, t_all, tm):
    n, d = x2d.shape
    final = final_g is not None
    in_specs = [
        pl.BlockSpec(memory_space=pl.ANY),
        pl.BlockSpec((tm, d), lambda t, dr: (t, 0)),
        pl.BlockSpec((tm, TOP_K), lambda t, dr: (t, 0)),
        pl.BlockSpec((None, None, None, 1, d), lambda t, dr: (layer, row_of_tile(t), 5, 0, 0)),
    ]
    args = [yb, x2d, w_tok, mods5]
    if final:
        in_specs.append(pl.BlockSpec((1, d), lambda t, dr: (0, 0)))
        args.append(final_g.reshape(1, d))
    grid_spec = pltpu.PrefetchScalarGridSpec(
        num_scalar_prefetch=1,
        grid=(n // tm,),
        in_specs=in_specs,
        out_specs=pl.BlockSpec((tm, d), lambda t, dr: (t, 0)),
        scratch_shapes=[pltpu.VMEM((TOP_K, tm, d // 2), U32), pltpu.SemaphoreType.DMA],
    )
    return pl.pallas_call(
        functools.partial(_combine_kernel, tm=tm, t_off=t_off, t_all=t_all, final=final),
        grid_spec=grid_spec,
        out_shape=jax.ShapeDtypeStruct((n, d), F32),
        compiler_params=_cparams(("arbitrary",)),
        name="combine",
    )(dest_flat, *args)


def _pack_w_in(w):
    d = w.shape[0]
    pts, acc = [], 0
    for s in IN_SIZES[:-1]:
        acc += s
        pts.append(acc)
    wq, wk, wv, wg, wzf, wzb, wsq, wsk, wsv = jnp.split(w, pts, axis=-1)
    zpad = jnp.zeros((d, LANES - 2 * GLA_GATE_RANK), w.dtype)
    return jnp.concatenate([wq, wk, wv, wg, wzf, wzb, zpad, wsq, wsk, wsv], axis=-1).astype(BF16)


def _pad_up(up, row0):
    out = jnp.zeros((GLA_HEADS // 2, LANES, LANES), F32)
    return out.at[:, row0:row0 + GLA_GATE_RANK, :].set(up.reshape(GLA_GATE_RANK, GLA_HEADS // 2, LANES).transpose(1, 0, 2))


def _rope_tables(seq):
    rows = seq // GRID_W
    row = jnp.repeat(jnp.arange(rows, dtype=F32), GRID_W)
    col = jnp.tile(jnp.arange(GRID_W, dtype=F32), rows)
    n_freq = ROPE_AXIS_DIM // 2
    inv = ROPE_THETA ** (-(jnp.arange(n_freq, dtype=F32) * 2.0 / ROPE_AXIS_DIM))
    ang_r = row[:, None] * inv[None, :]
    ang_c = col[:, None] * inv[None, :]
    cr, sr, cc, sc = jnp.cos(ang_r), jnp.sin(ang_r), jnp.cos(ang_c), jnp.sin(ang_c)
    cos64 = jnp.concatenate([cr, cr, cc, cc], axis=-1)
    sin64 = jnp.concatenate([-sr, sr, -sc, sc], axis=-1)
    reps = LANES // SWA_HD
    return jnp.tile(cos64, (1, reps)), jnp.tile(sin64, (1, reps))


def _tile(n, pref):
    t = pref
    while n % t:
        t //= 2
    return t


def kernel(x, c, ctx, c_ctx, w_ada, b_ada, norm1, norm2, w_in, gla_up_f, gla_bias_f, gla_up_b, gla_bias_b,
           gla_norm, swa_sink, w_out, w_router, b_router, w_gate, w_up, w_down, final_norm):
    batch, seq, d = x.shape
    lc = ctx.shape[1]
    depth = w_ada.shape[0]
    n_l = batch * seq
    n_c = batch * lc
    npair = GLA_HEADS // 2
    assert batch + 1 <= SUBLANES

    tm_l = _tile(seq, 512)
    tm_c = _tile(lc, 256)
    tl_l = _tile(seq, 512)
    tl_c = _tile(lc, 512)

    rows = jnp.zeros((SUBLANES, d), F32).at[:batch].set(c).at[batch].set(c_ctx)
    mods = _ada_call(rows, w_ada, b_ada)
    mods5 = mods.reshape(depth, SUBLANES, 6, 1, d)

    tabs = _rope_tables(seq)
    wr_pad = jnp.zeros((d, LANES), F32).at[:, :N_EXPERTS].set(w_router)
    wr_hi = wr_pad.astype(BF16)
    wr_lo = (wr_pad - wr_hi.astype(F32)).astype(BF16)
    wr_cat = jnp.concatenate([wr_hi, wr_lo], axis=-1)

    def lat_row(tm):
        return lambda t: t // (seq // tm)

    def ctx_row(tm):
        return lambda t: batch

    xl = x.reshape(n_l, d)
    xc = ctx.reshape(n_c, d)
    for i in range(depth):
        last = i == depth - 1
        w_packed = _pack_w_in(w_in[i])
        upf = _pad_up(gla_up_f[i], 0)
        upb = _pad_up(gla_up_b[i], GLA_GATE_RANK)
        bias_f = gla_bias_f[i].reshape(npair, 1, LANES)
        bias_b = gla_bias_b[i].reshape(npair, 1, LANES)
        w_out_b = w_out[i].astype(BF16)

        c_qk, c_v, c_g, c_z, c_sq, c_sk, c_sv = _inproj_call(
            xc, norm1[i], mods5, i, ctx_row(tm_c), w_packed, None, tm_c, lc // tm_c)
        l_qk, l_v, l_g, l_z, l_sq, l_sk, l_sv = _inproj_call(
            xl, norm1[i], mods5, i, lat_row(tm_l), w_packed, tabs, tm_l, seq // tm_l)

        s_zero = jnp.zeros((batch, npair, 2, 2 * GLA_DV, LANES), F32)
        oc_f, oc_b, s_ctx = _gla_call(c_qk, c_v, c_z, upf, upb, bias_f, bias_b, s_zero, batch, lc, tl_c)
        ol_f, ol_b, _ = _gla_call(l_qk, l_v, l_z, upf, upb, bias_f, bias_b, s_ctx, batch, seq, tl_l)

        swa_l = _swa_call(l_sq, l_sk, l_sv, c_sk, c_sv, swa_sink[i], batch, seq, lc)
        xl, h2l, lg_l = _outproj_call(ol_f, ol_b, l_g, gla_norm[i], swa_l, w_out_b, xl, mods5, i, lat_row(tm_l),
                                      norm2[i], wr_cat, tm_l)
        if last:
            logits_t = lg_l
            t_all = n_l
        else:
            swa_c = _swa_ctx_call(c_sq, c_sk, c_sv, swa_sink[i], batch, lc)
            xc, h2c, lg_c = _outproj_call(oc_f, oc_b, c_g, gla_norm[i], swa_c, w_out_b, xc, mods5, i, ctx_row(tm_c),
                                          norm2[i], wr_cat, tm_c)
            logits_t = jnp.concatenate([lg_l, lg_c], axis=1)
            t_all = n_l + n_c

        io, wo, cnt = _route_call(logits_t, b_router, _tile(t_all, 512))
        counts = cnt[:, 0]
        m = MOE_BLOCK
        padded = ((counts + m - 1) // m) * m
        pend = jnp.cumsum(padded)
        nb = -(-(t_all * TOP_K) // m) + N_EXPERTS
        p_rows = nb * m
        dest_flat = io[2:4].reshape(-1)
        nused = (pend[-1:] // m).astype(I32)
        tail = nused + jnp.arange(N_EXPERTS, dtype=I32)
        zblk = jnp.concatenate([jnp.where(padded > 0, pend // m - 1, -1),
                                jnp.where(tail < nb, tail, -1)]).astype(I32)
        blk_row = jnp.minimum(jnp.arange(nb, dtype=I32), nused - 1) * m
        block_e = jnp.sum((pend[None, :] <= blk_row[:, None]).astype(I32), axis=1)
        w_tok = wo[0:2].T

        sources = (h2l,) if last else (h2l, h2c)
        xb = _dispatch_call(dest_flat, zblk, sources, p_rows, t_all, _tile(n_c, _tile(n_l, 512)))
        yb = _expert_call(block_e, nused, xb, w_gate, w_up, w_down, i)
        tm_y = _tile(seq, 256)
        xl_new = _combine_call(dest_flat, yb, xl, w_tok[:n_l], mods5, i, lat_row(tm_y),
                               final_norm if last else None, 0, t_all, tm_y)
        if not last:
            xc = _combine_call(dest_flat, yb, xc, w_tok[n_l:], mods5, i, ctx_row(0), None, n_l, t_all, _tile(lc, 256))
        xl = xl_new
    return xl.reshape(batch, seq, d)
```

```python
import functools

import jax
import jax.numpy as jnp
from jax import lax
from jax.experimental import pallas as pl
from jax.experimental.pallas import tpu as pltpu

F32 = jnp.float32
BF16 = jnp.bfloat16
I32 = jnp.int32
U32 = jnp.uint32
HIGHEST = lax.Precision.HIGHEST

GRID_W = 64
NORM_EPS = 1e-6
GLA_HEADS = 4
GLA_DK = 64
GLA_DV = 128
GLA_KW = GLA_HEADS * GLA_DK
GLA_WIDTH = GLA_HEADS * GLA_DV
GLA_GATE_RANK = 16
GLA_TAU = 16.0
GLA_CHUNK = 64
SWA_HEADS = 8
SWA_KV_HEADS = 2
SWA_HD = 64
SWA_WIDTH = SWA_HEADS * SWA_HD
SWA_KVW = SWA_KV_HEADS * SWA_HD
WINDOW = 128
ATT_BLOCK = 128
ROPE_THETA = 10000.0
ROPE_AXIS_DIM = SWA_HD // 2
N_EXPERTS = 16
N_GROUPS = 4
EXPERTS_PER_GROUP = N_EXPERTS // N_GROUPS
TOP_K = 2
D_EXPERT = 512
MOE_BLOCK = 256
IN_SIZES = (GLA_KW, GLA_KW, GLA_WIDTH, GLA_WIDTH, GLA_GATE_RANK, GLA_GATE_RANK, SWA_WIDTH, SWA_KVW, SWA_KVW)

LANES = 128
SUBLANES = 8
VMEM_LIMIT = 48 * 1024 * 1024

C_QK = 0
C_V = 512
C_G = 1024
C_Z = 1536
C_SQ = 1664
C_SK = 2176
C_SV = 2304
C_END = 2432
SV_LANES = 4 * LANES
NEG_BIG = -1e30


def _cparams(sem):
    return pltpu.CompilerParams(dimension_semantics=sem, vmem_limit_bytes=VMEM_LIMIT)


def _pack_bf16_pairs(x):
    n = x.shape[1] // 2
    lo = lax.bitcast_convert_type(x[:, :n].astype(BF16).astype(F32), U32)
    hi = lax.bitcast_convert_type(x[:, n:].astype(BF16).astype(F32), U32)
    return (lo >> 16) | hi


def _unpack_bf16_pairs(w):
    lo = lax.bitcast_convert_type(w << 16, F32)
    hi = lax.bitcast_convert_type(w & jnp.uint32(0xFFFF0000), F32)
    return jnp.concatenate([lo.astype(BF16), hi.astype(BF16)], axis=-1)


def _ada_kernel(a_ref, w_ref, b_ref, o_ref):
    a = a_ref[...]
    act = a * jax.nn.sigmoid(a)
    o_ref[...] = jnp.dot(act, w_ref[...], preferred_element_type=F32, precision=HIGHEST) + b_ref[...]


def _ada_call(rows, w_ada, b_ada):
    depth, d, n6 = w_ada.shape
    tn = 1536
    return pl.pallas_call(
        _ada_kernel,
        grid=(depth, n6 // tn),
        in_specs=[
            pl.BlockSpec((SUBLANES, d), lambda l, j: (0, 0)),
            pl.BlockSpec((None, d, tn), lambda l, j: (l, 0, j)),
            pl.BlockSpec((None, 1, tn), lambda l, j: (l, 0, j)),
        ],
        out_specs=pl.BlockSpec((None, SUBLANES, tn), lambda l, j: (l, 0, j)),
        out_shape=jax.ShapeDtypeStruct((depth, SUBLANES, n6), F32),
        compiler_params=_cparams(("arbitrary", "arbitrary")),
        name="adaln",
    )(rows, w_ada, b_ada.reshape(depth, 1, n6))


def _rope_tile(xj, cos, sin, lane_lo):
    partner = jnp.where(lane_lo, pltpu.roll(xj, LANES - 16, 1), pltpu.roll(xj, 16, 1))
    return xj * cos + partner * sin


def _inproj_kernel(*refs, rope):
    if rope:
        (x_ref, g_ref, sc_ref, sh_ref, w_ref, cos_ref, sin_ref,
         qk_ref, v_ref, gate_ref, z_ref, sq_ref, sk_ref, sv_ref) = refs
    else:
        (x_ref, g_ref, sc_ref, sh_ref, w_ref,
         qk_ref, v_ref, gate_ref, z_ref, sq_ref, sk_ref, sv_ref) = refs
    x = x_ref[...]
    ms = jnp.mean(x * x, axis=-1, keepdims=True)
    h = (x * lax.rsqrt(ms + NORM_EPS)) * g_ref[...]
    h = h * (1.0 + sc_ref[...]) + sh_ref[...]
    hb = h.astype(BF16)

    def proj(a, b):
        return jnp.dot(hb, w_ref[:, a:b], preferred_element_type=F32)

    qk = proj(C_QK, C_V)
    qk_ref[:, :GLA_KW] = (qk[:, :GLA_KW] * (GLA_DK ** -0.5)).astype(qk_ref.dtype)
    qk_ref[:, GLA_KW:] = qk[:, GLA_KW:].astype(qk_ref.dtype)
    v_ref[...] = proj(C_V, C_G).astype(v_ref.dtype)
    gate_ref[...] = proj(C_G, C_Z).astype(gate_ref.dtype)
    z_ref[...] = proj(C_Z, C_SQ)
    sq = proj(C_SQ, C_SK) * (SWA_HD ** -0.5)
    sk = proj(C_SK, C_SV)
    sv = proj(C_SV, C_END)
    if rope:
        cos = cos_ref[...]
        sin = sin_ref[...]
        lane_lo = (lax.broadcasted_iota(I32, cos.shape, 1) % 32) < 16
        for j in range(SWA_WIDTH // LANES):
            sl = slice(j * LANES, (j + 1) * LANES)
            sq_ref[:, sl] = _rope_tile(sq[:, sl], cos, sin, lane_lo).astype(sq_ref.dtype)
        sk = _rope_tile(sk, cos, sin, lane_lo)
    else:
        sq_ref[...] = sq.astype(sq_ref.dtype)
    first = lax.broadcasted_iota(I32, sk.shape, 1) < SWA_HD
    sk_sw = pltpu.roll(sk, SWA_HD, 1)
    sk_ref[:, :LANES] = jnp.where(first, sk, sk_sw).astype(sk_ref.dtype)
    sk_ref[:, LANES:] = jnp.where(first, sk_sw, sk).astype(sk_ref.dtype)
    sv_sw = pltpu.roll(sv, SWA_HD, 1)
    sv_ref[:, 0 * LANES:1 * LANES] = jnp.where(first, sv, 1.0).astype(sv_ref.dtype)
    sv_ref[:, 1 * LANES:2 * LANES] = jnp.where(first, 1.0, sv_sw).astype(sv_ref.dtype)
    sv_ref[:, 2 * LANES:3 * LANES] = jnp.where(first, sv_sw, 1.0).astype(sv_ref.dtype)
    sv_ref[:, 3 * LANES:4 * LANES] = jnp.where(first, 1.0, sv).astype(sv_ref.dtype)


def _inproj_call(x2d, g, mods5, layer, row_of_tile, w_packed, tabs, tm, tiles_per_seq):
    n, d = x2d.shape
    nt = n // tm
    rope = tabs is not None

    def mod_spec(k):
        return pl.BlockSpec((None, None, None, 1, d), lambda t: (layer, row_of_tile(t), k, 0, 0))

    in_specs = [
        pl.BlockSpec((tm, d), lambda t: (t, 0)),
        pl.BlockSpec((1, d), lambda t: (0, 0)),
        mod_spec(1), mod_spec(0),
        pl.BlockSpec((d, C_END), lambda t: (0, 0)),
    ]
    args = [x2d, g.reshape(1, d), mods5, mods5, w_packed]
    if rope:
        in_specs += [pl.BlockSpec((tm, LANES), lambda t: (t % tiles_per_seq, 0))] * 2
        args += [tabs[0], tabs[1]]
    widths = (2 * GLA_KW, GLA_WIDTH, GLA_WIDTH, LANES, SWA_WIDTH, 2 * SWA_KVW, SV_LANES)
    dtypes = (BF16, BF16, BF16, F32, BF16, BF16, BF16)
    return pl.pallas_call(
        functools.partial(_inproj_kernel, rope=rope),
        grid=(nt,),
        in_specs=in_specs,
        out_specs=[pl.BlockSpec((tm, w), lambda t: (t, 0)) for w in widths],
        out_shape=[jax.ShapeDtypeStruct((n, w), dt) for w, dt in zip(widths, dtypes)],
        compiler_params=_cparams(("arbitrary",)),
        name="inproj",
    )(*args)


def _log_sigmoid(x):
    return jnp.minimum(x, 0.0) - jnp.log(1.0 + jnp.exp(-jnp.abs(x)))


def _gla_prepare(q_ref, k_ref, v_ref, z_ref, up_ref, bias_ref, tri_ref, scr):
    qin_s, km0_s, km1_s, kout_s, vm0_s, vm1_s, dec_s = scr[:7]
    tl = q_ref.shape[0]
    x = jnp.dot(z_ref[...], up_ref[...], preferred_element_type=F32, precision=HIGHEST) + bias_ref[...]
    la = _log_sigmoid(x) * (1.0 / GLA_TAU)
    hi = la.astype(BF16)
    lo = (la - hi.astype(F32)).astype(BF16)
    sums = jnp.dot(tri_ref[...], jnp.concatenate([hi, lo], axis=1), preferred_element_type=F32)
    b = sums[:tl, :LANES] + sums[:tl, LANES:]
    btot = sums[tl:, :LANES] + sums[tl:, LANES:]
    q = q_ref[...].astype(F32)
    k = k_ref[...].astype(F32)
    qin_s[...] = (q * jnp.exp(b)).astype(BF16)
    k_in = k * jnp.exp(-b)
    head0 = lax.broadcasted_iota(I32, (tl, LANES), 1) < GLA_DK
    km0_s[...] = jnp.where(head0, k_in, 0.0).astype(BF16)
    km1_s[...] = jnp.where(head0, 0.0, k_in).astype(BF16)
    kout_s[...] = (k * jnp.exp(btot - b)).astype(BF16)
    dec_s[...] = jnp.exp(btot)
    vf = v_ref[...].astype(F32)
    vhead0 = lax.broadcasted_iota(I32, (tl, 2 * GLA_DV), 1) < GLA_DV
    vm0_s[...] = jnp.where(vhead0, vf, 0.0).astype(BF16)
    vm1_s[...] = jnp.where(vhead0, 0.0, vf).astype(BF16)


def _gla_increment(v_ref, scr, r0):
    kout_s, u_s = scr[3], scr[7]
    c = GLA_CHUNK
    rows = pl.ds(r0, c)
    u_t = lax.dot_general(v_ref[rows, :], kout_s[rows, :], (((0,), (0,)), ((), ())),
                          preferred_element_type=F32)
    srow = lax.broadcasted_iota(I32, (2 * GLA_DV, LANES), 0) // GLA_DV
    scol = lax.broadcasted_iota(I32, (2 * GLA_DV, LANES), 1) // GLA_DK
    u_s[r0 // c] = jnp.where(srow == scol, u_t, 0.0)


def _gla_states(scr, s_ref, chunk_order):
    dec_s, u_s, sprev_s = scr[6], scr[7], scr[8]
    s_t = s_ref[...]
    for cidx in chunk_order:
        sprev_s[cidx] = s_t.astype(BF16)
        r0 = cidx * GLA_CHUNK
        s_t = s_t * dec_s[r0:r0 + 1, :] + u_s[cidx]
    s_ref[...] = s_t


def _gla_output(o_ref, scr, r0, fwd):
    qin_s, km0_s, km1_s, _, vm0_s, vm1_s, _, _, sprev_s = scr
    c = GLA_CHUNK
    rows = pl.ds(r0, c)
    q_in = qin_s[rows, :]
    kst = jnp.concatenate([km0_s[rows, :], km1_s[rows, :]], axis=0)
    a = lax.dot_general(q_in, kst, (((1,), (1,)), ((), ())), preferred_element_type=F32)
    r2 = lax.broadcasted_iota(I32, (c, 2 * c), 0)
    c2 = lax.broadcasted_iota(I32, (c, 2 * c), 1) % c
    keep = (c2 <= r2) if fwd else (c2 >= r2)
    a = jnp.where(keep, a, 0.0).astype(BF16)
    vbd = jnp.concatenate([vm0_s[rows, :], vm1_s[rows, :]], axis=0)
    o = jnp.dot(a, vbd, preferred_element_type=F32)
    o = o + lax.dot_general(q_in, sprev_s[r0 // c], (((1,), (1,)), ((), ())), preferred_element_type=F32)
    o_ref[rows, :] = o.astype(o_ref.dtype)


def _gla_kernel(qf_ref, kf_ref, vf_ref, zf_ref, qb_ref, kb_ref, vb_ref, zb_ref,
                upf_ref, upb_ref, bf_ref, bb_ref, trif_ref, trib_ref, s0_ref,
                of_ref, ob_ref, sfin_ref, sf_scr, sb_scr, *scr, nchunk):
    i = pl.program_id(2)
    nt = pl.num_programs(2)
    scr_f, scr_b = scr[:len(scr) // 2], scr[len(scr) // 2:]

    @pl.when(i == 0)
    def _():
        sf_scr[...] = s0_ref[0]
        sb_scr[...] = s0_ref[1]

    _gla_prepare(qf_ref, kf_ref, vf_ref, zf_ref, upf_ref, bf_ref, trif_ref, scr_f)
    _gla_prepare(qb_ref, kb_ref, vb_ref, zb_ref, upb_ref, bb_ref, trib_ref, scr_b)
    for cidx in range(nchunk):
        _gla_increment(vf_ref, scr_f, cidx * GLA_CHUNK)
        _gla_increment(vb_ref, scr_b, cidx * GLA_CHUNK)
    _gla_states(scr_f, sf_scr, range(nchunk))
    _gla_states(scr_b, sb_scr, range(nchunk - 1, -1, -1))
    for cidx in range(nchunk):
        _gla_output(of_ref, scr_f, cidx * GLA_CHUNK, True)
        _gla_output(ob_ref, scr_b, cidx * GLA_CHUNK, False)

    @pl.when(i == nt - 1)
    def _():
        sfin_ref[0] = sf_scr[...]
        sfin_ref[1] = sb_scr[...]


def _gla_call(qk, v, z, upf_pad, upb_pad, bias_f, bias_b, s0, batch, seq, tl):
    n = qk.shape[0]
    nt = seq // tl
    npair = GLA_HEADS // 2

    def fwd_row(b, p, i):
        return b * nt + i

    def bwd_row(b, p, i):
        return b * nt + (nt - 1 - i)

    def specs(row):
        return [
            pl.BlockSpec((tl, LANES), lambda b, p, i: (row(b, p, i), p)),
            pl.BlockSpec((tl, LANES), lambda b, p, i: (row(b, p, i), npair + p)),
            pl.BlockSpec((tl, 2 * GLA_DV), lambda b, p, i: (row(b, p, i), p)),
            pl.BlockSpec((tl, LANES), lambda b, p, i: (row(b, p, i), 0)),
        ]

    in_specs = specs(fwd_row) + specs(bwd_row) + [
        pl.BlockSpec((None, LANES, LANES), lambda b, p, i: (p, 0, 0)),
        pl.BlockSpec((None, LANES, LANES), lambda b, p, i: (p, 0, 0)),
        pl.BlockSpec((None, 1, LANES), lambda b, p, i: (p, 0, 0)),
        pl.BlockSpec((None, 1, LANES), lambda b, p, i: (p, 0, 0)),
        pl.BlockSpec((None, 2 * tl, tl), lambda b, p, i: (0, 0, 0)),
        pl.BlockSpec((None, 2 * tl, tl), lambda b, p, i: (1, 0, 0)),
        pl.BlockSpec((None, None, 2, 2 * GLA_DV, LANES), lambda b, p, i: (b, p, 0, 0, 0)),
    ]
    ri = jnp.arange(tl)[:, None]
    ci = jnp.arange(tl)[None, :]
    same = (ri // GLA_CHUNK) == (ci // GLA_CHUNK)
    tri = jnp.stack([jnp.concatenate([same & (ci <= ri), same], axis=0),
                     jnp.concatenate([same & (ci >= ri), same], axis=0)]).astype(BF16)
    dir_scratch = [
        pltpu.VMEM((tl, LANES), BF16), pltpu.VMEM((tl, LANES), BF16), pltpu.VMEM((tl, LANES), BF16),
        pltpu.VMEM((tl, LANES), BF16), pltpu.VMEM((tl, 2 * GLA_DV), BF16), pltpu.VMEM((tl, 2 * GLA_DV), BF16),
        pltpu.VMEM((tl, LANES), F32),
        pltpu.VMEM((tl // GLA_CHUNK, 2 * GLA_DV, LANES), F32),
        pltpu.VMEM((tl // GLA_CHUNK, 2 * GLA_DV, LANES), BF16),
    ]
    out_specs = [
        pl.BlockSpec((tl, 2 * GLA_DV), lambda b, p, i: (fwd_row(b, p, i), p)),
        pl.BlockSpec((tl, 2 * GLA_DV), lambda b, p, i: (bwd_row(b, p, i), p)),
        pl.BlockSpec((None, None, 2, 2 * GLA_DV, LANES), lambda b, p, i: (b, p, 0, 0, 0)),
    ]
    out_shape = [
        jax.ShapeDtypeStruct((n, GLA_WIDTH), BF16),
        jax.ShapeDtypeStruct((n, GLA_WIDTH), BF16),
        jax.ShapeDtypeStruct((batch, npair, 2, 2 * GLA_DV, LANES), F32),
    ]
    return pl.pallas_call(
        functools.partial(_gla_kernel, nchunk=tl // GLA_CHUNK),
        grid=(batch, npair, nt),
        in_specs=in_specs,
        out_specs=out_specs,
        out_shape=out_shape,
        scratch_shapes=[pltpu.VMEM((2 * GLA_DV, LANES), F32), pltpu.VMEM((2 * GLA_DV, LANES), F32)]
        + dir_scratch + dir_scratch,
        compiler_params=_cparams(("arbitrary", "arbitrary", "arbitrary")),
        name="gla",
    )(qk, qk, v, z, qk, qk, v, z, upf_pad, upb_pad, bias_f, bias_b, tri, tri, s0)


def _swa_kernel(*refs, nb, local):
    if local:
        sink_ref, q_ref, kp_ref, kc_ref, kn_ref, vp_ref, vc_ref, vn_ref, kx_ref, vx_ref, o_ref = refs
        kall = jnp.concatenate([kp_ref[...], kc_ref[...], kn_ref[...], kx_ref[...]], axis=0)
        vall = jnp.concatenate([vp_ref[...], vc_ref[...], vn_ref[...], vx_ref[...]], axis=0)
    else:
        sink_ref, q_ref, kx_ref, vx_ref, o_ref = refs
        kall = kx_ref[...]
        vall = vx_ref[...]
    w = ATT_BLOCK
    nk = kall.shape[0]
    r2 = lax.broadcasted_iota(I32, (2 * w, w), 0) % w
    c2 = lax.broadcasted_iota(I32, (2 * w, w), 1)
    if local:
        i = pl.program_id(1)
        bias_prev = jnp.where((c2 >= r2) & (i > 0), 0.0, NEG_BIG)
        bias_next = jnp.where((c2 <= r2) & (i < nb - 1), 0.0, NEG_BIG)
    first = c2 < SWA_HD
    top = lax.broadcasted_iota(I32, (2 * w, 1), 0) < w
    scores = []
    for g in range(SWA_KV_HEADS):
        kd = kall[:, g * LANES:(g + 1) * LANES]
        qs = jnp.concatenate([q_ref[:, (2 * g) * LANES:(2 * g + 1) * LANES],
                              q_ref[:, (2 * g + 1) * LANES:(2 * g + 2) * LANES]], axis=0).astype(F32)
        for half in range(2):
            qm = (jnp.where(first, qs, 0.0) if half == 0 else jnp.where(first, 0.0, qs)).astype(BF16)
            s = lax.dot_general(qm, kd, (((1,), (1,)), ((), ())), preferred_element_type=F32)
            if local:
                s = jnp.concatenate([s[:, :w] + bias_prev, s[:, w:2 * w], s[:, 2 * w:3 * w] + bias_next,
                                     s[:, 3 * w:]], axis=1)
            scores.append(s)
    for g in range(SWA_KV_HEADS):
        outs = []
        for half in range(2):
            s = scores[2 * g + half]
            sk = jnp.where(top, sink_ref[4 * g + half], sink_ref[4 * g + 2 + half])
            m = jnp.maximum(jnp.max(s, axis=-1, keepdims=True), sk)
            p = jnp.exp((s - m).astype(BF16))
            va = vall[:, (2 * g + half) * LANES:(2 * g + half + 1) * LANES]
            acc = jnp.dot(p, va, preferred_element_type=F32)
            den = pltpu.roll(acc, SWA_HD, 1) + jnp.exp(sk - m)
            outs.append(acc / den)
        out = jnp.where(first, outs[0], outs[1])
        o_ref[:, (2 * g) * LANES:(2 * g + 1) * LANES] = out[:w].astype(o_ref.dtype)
        o_ref[:, (2 * g + 1) * LANES:(2 * g + 2) * LANES] = out[w:].astype(o_ref.dtype)


def _swa_call(sq, skd, svd, kcd, vcd, sink, batch, seq, lc):
    n = sq.shape[0]
    w = ATT_BLOCK
    nb = seq // w
    kvw = 2 * SWA_KVW

    def blk(off):
        return lambda b, i: (b * nb + jnp.clip(i + off, 0, nb - 1), 0)

    def kv_specs(width):
        return [pl.BlockSpec((w, width), blk(-1)), pl.BlockSpec((w, width), blk(0)), pl.BlockSpec((w, width), blk(1))]

    def ctx_spec(width):
        return pl.BlockSpec((lc, width), lambda b, i: (b, 0))

    return pl.pallas_call(
        functools.partial(_swa_kernel, nb=nb, local=True),
        grid=(batch, nb),
        in_specs=[pl.BlockSpec(memory_space=pltpu.SMEM), pl.BlockSpec((w, SWA_WIDTH), blk(0))]
        + kv_specs(kvw) + kv_specs(SV_LANES) + [ctx_spec(kvw), ctx_spec(SV_LANES)],
        out_specs=pl.BlockSpec((w, SWA_WIDTH), blk(0)),
        out_shape=jax.ShapeDtypeStruct((n, SWA_WIDTH), BF16),
        compiler_params=_cparams(("arbitrary", "arbitrary")),
        name="swa",
    )(sink, sq, skd, skd, skd, svd, svd, svd, kcd, vcd)


def _swa_ctx_call(sq, kcd, vcd, sink, batch, lc):
    n = sq.shape[0]
    w = ATT_BLOCK
    nb = lc // w
    kvw = 2 * SWA_KVW
    return pl.pallas_call(
        functools.partial(_swa_kernel, nb=nb, local=False),
        grid=(batch, nb),
        in_specs=[pl.BlockSpec(memory_space=pltpu.SMEM),
                  pl.BlockSpec((w, SWA_WIDTH), lambda b, i: (b * nb + i, 0)),
                  pl.BlockSpec((lc, kvw), lambda b, i: (b, 0)),
                  pl.BlockSpec((lc, SV_LANES), lambda b, i: (b, 0))],
        out_specs=pl.BlockSpec((w, SWA_WIDTH), lambda b, i: (b * nb + i, 0)),
        out_shape=jax.ShapeDtypeStruct((n, SWA_WIDTH), BF16),
        compiler_params=_cparams(("arbitrary", "arbitrary")),
        name="swa_ctx",
    )(sink, sq, kcd, vcd)


def _outproj_kernel(of_ref, ob_ref, gate_ref, gn_ref, swa_ref, w_ref, x_ref, g1_ref, n2_ref, sc_ref, sh_ref,
                    wr_ref, xo_ref, h2_ref, lg_ref):
    o = of_ref[...].astype(F32) + ob_ref[...].astype(F32)
    parts = []
    for h in range(GLA_HEADS):
        oh = o[:, h * GLA_DV:(h + 1) * GLA_DV]
        ms = jnp.mean(oh * oh, axis=-1, keepdims=True)
        parts.append(oh * lax.rsqrt(ms + NORM_EPS))
    on = jnp.concatenate(parts, axis=-1) * gn_ref[...]
    gate = gate_ref[...].astype(F32)
    gla = on * (gate * jax.nn.sigmoid(gate))
    mix = jnp.concatenate([gla.astype(BF16), swa_ref[...]], axis=-1)
    y = jnp.dot(mix, w_ref[...], preferred_element_type=F32)
    xo = x_ref[...] + g1_ref[...] * y
    xo_ref[...] = xo
    ms = jnp.mean(xo * xo, axis=-1, keepdims=True)
    h2 = (xo * lax.rsqrt(ms + NORM_EPS)) * n2_ref[...]
    h2 = h2 * (1.0 + sc_ref[...]) + sh_ref[...]
    h2_ref[...] = _pack_bf16_pairs(h2)
    hi = h2.astype(BF16)
    lo = (h2 - hi.astype(F32)).astype(BF16)
    both = jnp.dot(hi, wr_ref[...], preferred_element_type=F32)
    lg = both[:, :LANES] + both[:, LANES:] + jnp.dot(lo, wr_ref[:, :LANES], preferred_element_type=F32)
    lg_ref[...] = jnp.transpose(lg)[:N_EXPERTS, :]


def _outproj_call(o_f, o_b, gate, gn, swa, w_out_b, x2d, mods5, layer, row_of_tile, n2, wr_cat, tm):
    n, d = x2d.shape
    nt = n // tm

    def mod_spec(k):
        return pl.BlockSpec((None, None, None, 1, d), lambda t: (layer, row_of_tile(t), k, 0, 0))

    return pl.pallas_call(
        _outproj_kernel,
        grid=(nt,),
        in_specs=[
            pl.BlockSpec((tm, GLA_WIDTH), lambda t: (t, 0)),
            pl.BlockSpec((tm, GLA_WIDTH), lambda t: (t, 0)),
            pl.BlockSpec((tm, GLA_WIDTH), lambda t: (t, 0)),
            pl.BlockSpec((1, GLA_WIDTH), lambda t: (0, 0)),
            pl.BlockSpec((tm, SWA_WIDTH), lambda t: (t, 0)),
            pl.BlockSpec((d, d), lambda t: (0, 0)),
            pl.BlockSpec((tm, d), lambda t: (t, 0)),
            mod_spec(2),
            pl.BlockSpec((1, d), lambda t: (0, 0)),
            mod_spec(4), mod_spec(3),
            pl.BlockSpec((d, 2 * LANES), lambda t: (0, 0)),
        ],
        out_specs=[
            pl.BlockSpec((tm, d), lambda t: (t, 0)),
            pl.BlockSpec((tm, d // 2), lambda t: (t, 0)),
            pl.BlockSpec((N_EXPERTS, tm), lambda t: (0, t)),
        ],
        out_shape=[
            jax.ShapeDtypeStruct((n, d), F32),
            jax.ShapeDtypeStruct((n, d // 2), U32),
            jax.ShapeDtypeStruct((N_EXPERTS, n), F32),
        ],
        compiler_params=_cparams(("arbitrary",)),
        name="outproj",
    )(o_f, o_b, gate, gn.reshape(1, GLA_WIDTH), swa, w_out_b, x2d, mods5, n2.reshape(1, d), mods5, mods5, wr_cat)


def _first_index(vals, target):
    idx = jnp.full(target.shape, len(vals) - 1, I32)
    for i in range(len(vals) - 2, -1, -1):
        idx = jnp.where(vals[i] == target, i, idx)
    return idx


def _route_kernel(lg_ref, br_ref, io_ref, wo_ref, seg_ref, pad_ref, blk_ref, cnt_scr, seg_scr, pad_scr, *, tn):
    phase = pl.program_id(0)
    t = pl.program_id(1)
    tile_lane = lax.broadcasted_iota(I32, (N_EXPERTS, LANES), 1)

    @pl.when((phase == 0) & (t == 0))
    def _():
        cnt_scr[...] = jnp.zeros_like(cnt_scr)
        seg_scr[...] = jnp.zeros_like(seg_scr)
        pad_scr[...] = jnp.zeros_like(pad_scr)

    @pl.when((phase == 1) & (t == 0))
    def _():
        pad8 = jnp.floor((cnt_scr[...] + (SUBLANES - 1)) * (1.0 / SUBLANES)) * SUBLANES
        rows_e = jnp.sum(pad8, axis=1, keepdims=True)
        blocks = jnp.floor((rows_e + (MOE_BLOCK - 1)) * (1.0 / MOE_BLOCK)) * MOE_BLOCK
        r128 = lax.broadcasted_iota(I32, (LANES, LANES), 0)
        c128 = lax.broadcasted_iota(I32, (LANES, LANES), 1)
        before_tile = jnp.where(r128 < c128, 1.0, 0.0).astype(BF16)
        seg = jnp.dot((pad8 * (1.0 / SUBLANES)).astype(BF16), before_tile,
                      preferred_element_type=F32) * SUBLANES
        run = jnp.zeros((1, 1), F32)
        for e in range(N_EXPERTS):
            seg_scr[e:e + 1, :] = seg[e:e + 1, :] + run
            run = run + blocks[e:e + 1, :]
        pad_scr[...] = pad8
        seg_ref[...] = seg_scr[...].astype(I32)
        pad_ref[...] = pad8.astype(I32)
        blk_ref[...] = jnp.broadcast_to(blocks, blk_ref.shape).astype(I32)

    s = jax.nn.sigmoid(lg_ref[...])
    sb = s + br_ref[...]
    rows_s = [s[e:e + 1, :] for e in range(N_EXPERTS)]
    rows_b = [sb[e:e + 1, :] for e in range(N_EXPERTS)]
    gscore, gi1, gi2 = [], [], []
    epg = EXPERTS_PER_GROUP
    for g in range(N_GROUPS):
        a = rows_b[g * epg:(g + 1) * epg]
        m1 = functools.reduce(jnp.maximum, a)
        i1 = _first_index(a, m1)
        rest = [jnp.where(i1 == i, -jnp.inf, a[i]) for i in range(epg)]
        m2 = functools.reduce(jnp.maximum, rest)
        i2 = _first_index(rest, m2)
        gscore.append(m1 + m2)
        gi1.append(i1)
        gi2.append(i2)
    gm = functools.reduce(jnp.maximum, gscore)
    gsel = _first_index(gscore, gm)
    i1 = gi1[N_GROUPS - 1]
    i2 = gi2[N_GROUPS - 1]
    for g in range(N_GROUPS - 2, -1, -1):
        i1 = jnp.where(gsel == g, gi1[g], i1)
        i2 = jnp.where(gsel == g, gi2[g], i2)
    idx0 = gsel * epg + i1
    idx1 = gsel * epg + i2
    s0 = jnp.zeros_like(rows_s[0])
    s1 = jnp.zeros_like(rows_s[0])
    for e in range(N_EXPERTS):
        s0 = jnp.where(idx0 == e, rows_s[e], s0)
        s1 = jnp.where(idx1 == e, rows_s[e], s1)
    tot = s0 + s1
    w0 = s0 / tot
    w1 = s1 / tot

    eidx = lax.broadcasted_iota(I32, (N_EXPERTS, tn), 0)
    oh0 = eidx == idx0
    oh1 = eidx == idx1
    oh = jnp.where(oh0 | oh1, 1.0, 0.0)
    rr = lax.broadcasted_iota(I32, (tn, tn), 0)
    cc = lax.broadcasted_iota(I32, (tn, tn), 1)
    upper = jnp.where(rr < cc, 1.0, 0.0).astype(BF16)
    before = jnp.dot(oh.astype(BF16), upper, preferred_element_type=F32)
    @pl.when(phase == 0)
    def _():
        cnt_scr[...] = cnt_scr[...] + jnp.where(tile_lane == t, jnp.sum(oh, axis=1, keepdims=True), 0.0)

    pad_col = jnp.sum(jnp.where(tile_lane == t, pad_scr[...], 0.0), axis=1, keepdims=True)
    run = jnp.zeros((1, 1), F32)
    offs = []
    for e in range(N_EXPERTS):
        offs.append(run)
        run = run + pad_col[e:e + 1, :]
    pos = before + jnp.concatenate(offs, axis=0)
    spos0 = jnp.sum(jnp.where(oh0, pos, 0.0), axis=0, keepdims=True)
    spos1 = jnp.sum(jnp.where(oh1, pos, 0.0), axis=0, keepdims=True)
    zi = jnp.zeros((SUBLANES - 4, tn), I32)
    io_ref[...] = jnp.concatenate([idx0, idx1, spos0.astype(I32), spos1.astype(I32), zi], axis=0)
    wo_ref[...] = jnp.concatenate([w0, w1, jnp.zeros((SUBLANES - 2, tn), F32)], axis=0)


def _route_call(logits_t, b_router, tn):
    ne, t_all = logits_t.shape
    assert t_all // tn <= LANES
    tok_out = pl.BlockSpec((SUBLANES, tn), lambda p, t: (0, t * p))
    tab_out = pl.BlockSpec((ne, LANES), lambda p, t: (0, 0))
    tab = jax.ShapeDtypeStruct((ne, LANES), I32)
    return pl.pallas_call(
        functools.partial(_route_kernel, tn=tn),
        grid=(2, t_all // tn),
        in_specs=[pl.BlockSpec((ne, tn), lambda p, t: (0, t)), pl.BlockSpec((ne, 1), lambda p, t: (0, 0))],
        out_specs=[tok_out, tok_out, tab_out, tab_out, tab_out],
        out_shape=[
            jax.ShapeDtypeStruct((SUBLANES, t_all), I32),
            jax.ShapeDtypeStruct((SUBLANES, t_all), F32),
            tab, tab, tab,
        ],
        scratch_shapes=[pltpu.VMEM((ne, LANES), F32)] * 3,
        compiler_params=_cparams(("arbitrary", "arbitrary")),
        name="route",
    )(logits_t, b_router.reshape(ne, 1))


SEG_SIZES = (512, 256, 128, 64, 32, 16, 8)
ROUTE_TILE = 512
STAGE_ROWS = TOP_K * ROUTE_TILE + LANES


def _segment_copies(seg_ref, pad_ref, t, stage, hbm, sem, to_hbm, wait):
    loc = 0
    for e in range(N_EXPERTS):
        n = pad_ref[e * LANES + t]
        start = seg_ref[e * LANES + t]
        for size in SEG_SIZES:
            off = n & (-2 * size)

            @pl.when((n & size) != 0)
            def _(off=off, size=size, loc=loc, start=start):
                s_rows = pl.ds(pl.multiple_of(loc + off, SUBLANES), size)
                h_rows = pl.ds(pl.multiple_of(start + off, SUBLANES), size)
                src, dst = (stage.at[s_rows, :], hbm.at[h_rows, :])
                if not to_hbm:
                    src, dst = dst, src
                cp = pltpu.make_async_copy(src, dst, sem)
                if wait:
                    cp.wait()
                else:
                    cp.start()
        loc = loc + n


def _sort_matrix(io_ref, rows):
    r = lax.broadcasted_iota(I32, (rows, io_ref.shape[1]), 0)
    return r == io_ref[2:3, :], r == io_ref[3:4, :]


def _dispatch_kernel(seg_ref, pad_ref, zblk_ref, *refs, tiles):
    srcs = refs[:len(tiles)]
    io_ref, wo_ref, xb_hbm, stage, zbuf, sem = refs[len(tiles):]
    i = pl.program_id(0)
    m = MOE_BLOCK

    def zero_copy(j):
        start = pl.multiple_of(zblk_ref[j] * m, m)
        return pltpu.make_async_copy(zbuf, xb_hbm.at[pl.ds(start, m), :], sem)

    @pl.when(i == 0)
    def _():
        zbuf[...] = jnp.zeros_like(zbuf)
        for j in range(zblk_ref.shape[0]):
            @pl.when(zblk_ref[j] >= 0)
            def _():
                zero_copy(j).start()
        for j in range(zblk_ref.shape[0]):
            @pl.when(zblk_ref[j] >= 0)
            def _():
                zero_copy(j).wait()

    hp = srcs[0][...]
    if len(srcs) == 2:
        hp = jnp.where(i < tiles[0], hp, srcs[1][...])
    x = _unpack_bf16_pairs(hp)
    m0, m1 = _sort_matrix(io_ref, STAGE_ROWS)
    sort = jnp.where(m0 | m1, 1.0, 0.0).astype(BF16)
    xs = jnp.dot(sort, x, preferred_element_type=F32)
    dh = x.shape[1] // 2
    stage[:, :dh] = _pack_bf16_pairs(xs)
    ws = jnp.sum(jnp.where(m0, wo_ref[0:1, :], 0.0) + jnp.where(m1, wo_ref[1:2, :], 0.0), axis=1, keepdims=True)
    stage[:, dh:] = jnp.broadcast_to(lax.bitcast_convert_type(ws, U32), (STAGE_ROWS, LANES))
    _segment_copies(seg_ref, pad_ref, i, stage, xb_hbm, sem, to_hbm=True, wait=False)
    _segment_copies(seg_ref, pad_ref, i, stage, xb_hbm, sem, to_hbm=True, wait=True)


def _dispatch_call(seg, pad, zblk, io, wo, sources, p_rows):
    tile = ROUTE_TILE
    dh = sources[0].shape[1]
    tiles = tuple(s.shape[0] // tile for s in sources)
    firsts = tuple(sum(tiles[:k]) for k in range(len(tiles)))

    def src_spec(first, ntile):
        return pl.BlockSpec((tile, dh), lambda i, *_: (jnp.clip(i - first, 0, ntile - 1), 0))

    tok_spec = pl.BlockSpec((SUBLANES, tile), lambda i, *_: (0, i))
    grid_spec = pltpu.PrefetchScalarGridSpec(
        num_scalar_prefetch=3,
        grid=(sum(tiles),),
        in_specs=[src_spec(f, n) for f, n in zip(firsts, tiles)] + [tok_spec, tok_spec],
        out_specs=pl.BlockSpec(memory_space=pl.ANY),
        scratch_shapes=[pltpu.VMEM((STAGE_ROWS, dh + LANES), U32), pltpu.VMEM((MOE_BLOCK, dh + LANES), U32),
                        pltpu.SemaphoreType.DMA],
    )
    return pl.pallas_call(
        functools.partial(_dispatch_kernel, tiles=tiles),
        grid_spec=grid_spec,
        out_shape=jax.ShapeDtypeStruct((p_rows, dh + LANES), U32),
        compiler_params=_cparams(("arbitrary",)),
        name="dispatch",
    )(seg, pad, zblk, *sources, io, wo)


def _expert_kernel(be_ref, nu_ref, x_ref, wg_ref, wu_ref, wd_ref, o_ref, wg_b, wu_b, wd_b):
    j = pl.program_id(0)

    @pl.when((j == 0) | (be_ref[j] != be_ref[jnp.maximum(j - 1, 0)]))
    def _():
        wg_b[...] = wg_ref[...].astype(BF16)
        wu_b[...] = wu_ref[...].astype(BF16)
        wd_b[...] = wd_ref[...].astype(BF16)

    @pl.when(j < nu_ref[0])
    def _():
        dh = o_ref.shape[1]
        x = _unpack_bf16_pairs(x_ref[:, :dh])
        row_w = lax.bitcast_convert_type(x_ref[:, dh:dh + 1], F32)
        a = jnp.dot(x, wg_b[...], preferred_element_type=F32)
        u = jnp.dot(x, wu_b[...], preferred_element_type=F32)
        hmid = (a * jax.nn.sigmoid(a)) * u
        y = jnp.dot(hmid.astype(BF16), wd_b[...], preferred_element_type=F32)
        o_ref[...] = _pack_bf16_pairs(y * row_w)

    @pl.when(j >= nu_ref[0])
    def _():
        o_ref[...] = jnp.zeros_like(o_ref)


def _expert_call(block_e, nused, xb, wg, wu, wd, layer):
    p_rows = xb.shape[0]
    dh = xb.shape[1] - LANES
    d = 2 * dh
    m = MOE_BLOCK
    nb = p_rows // m
    de = wg.shape[-1]

    def xmap(j, be, nu):
        return (jnp.minimum(j, nu[0] - 1), 0)

    def wmap(j, be, nu):
        return (layer, be[j], 0, 0)

    grid_spec = pltpu.PrefetchScalarGridSpec(
        num_scalar_prefetch=2,
        grid=(nb,),
        in_specs=[
            pl.BlockSpec((m, dh + LANES), xmap),
            pl.BlockSpec((None, None, d, de), wmap),
            pl.BlockSpec((None, None, d, de), wmap),
            pl.BlockSpec((None, None, de, d), wmap),
        ],
        out_specs=pl.BlockSpec((m, dh), lambda j, be, nu: (j, 0)),
        scratch_shapes=[pltpu.VMEM((d, de), BF16), pltpu.VMEM((d, de), BF16), pltpu.VMEM((de, d), BF16)],
    )
    return pl.pallas_call(
        _expert_kernel,
        grid_spec=grid_spec,
        out_shape=jax.ShapeDtypeStruct((p_rows, dh), U32),
        compiler_params=_cparams(("arbitrary",)),
        name="experts",
    )(block_e, nused, xb, wg, wu, wd)


def _combine_kernel(*refs, tile0, final):
    if final:
        seg_ref, pad_ref, yb_hbm, io_ref, x_ref, g2_ref, fn_ref, o_ref, stage, sem = refs
    else:
        seg_ref, pad_ref, yb_hbm, io_ref, x_ref, g2_ref, o_ref, stage, sem = refs
    i = pl.program_id(0)

    @pl.when(i == 0)
    def _():
        stage[...] = jnp.zeros_like(stage)

    _segment_copies(seg_ref, pad_ref, tile0 + i, stage, yb_hbm, sem, to_hbm=False, wait=False)
    _segment_copies(seg_ref, pad_ref, tile0 + i, stage, yb_hbm, sem, to_hbm=False, wait=True)
    rows = _unpack_bf16_pairs(stage[...])
    m0, m1 = _sort_matrix(io_ref, STAGE_ROWS)
    pick = jnp.where(m0 | m1, 1.0, 0.0).astype(BF16)
    y = lax.dot_general(pick, rows, (((0,), (0,)), ((), ())), preferred_element_type=F32)
    xo = x_ref[...] + g2_ref[...] * y
    if final:
        ms = jnp.mean(xo * xo, axis=-1, keepdims=True)
        xo = (xo * lax.rsqrt(ms + NORM_EPS)) * fn_ref[...]
    o_ref[...] = xo


def _combine_call(seg, pad, yb, io, x2d, mods5, layer, row_of_tile, final_g, tile0):
    n, d = x2d.shape
    tm = ROUTE_TILE
    final = final_g is not None
    in_specs = [
        pl.BlockSpec(memory_space=pl.ANY),
        pl.BlockSpec((SUBLANES, tm), lambda t, *_: (0, tile0 + t)),
        pl.BlockSpec((tm, d), lambda t, *_: (t, 0)),
        pl.BlockSpec((None, None, None, 1, d), lambda t, *_: (layer, row_of_tile(t), 5, 0, 0)),
    ]
    args = [yb, io, x2d, mods5]
    if final:
        in_specs.append(pl.BlockSpec((1, d), lambda t, *_: (0, 0)))
        args.append(final_g.reshape(1, d))
    grid_spec = pltpu.PrefetchScalarGridSpec(
        num_scalar_prefetch=2,
        grid=(n // tm,),
        in_specs=in_specs,
        out_specs=pl.BlockSpec((tm, d), lambda t, *_: (t, 0)),
        scratch_shapes=[pltpu.VMEM((STAGE_ROWS, d // 2), U32), pltpu.SemaphoreType.DMA],
    )
    return pl.pallas_call(
        functools.partial(_combine_kernel, tile0=tile0, final=final),
        grid_spec=grid_spec,
        out_shape=jax.ShapeDtypeStruct((n, d), F32),
        compiler_params=_cparams(("arbitrary",)),
        name="combine",
    )(seg, pad, *args)


def _pack_w_in(w):
    d = w.shape[0]
    pts, acc = [], 0
    for s in IN_SIZES[:-1]:
        acc += s
        pts.append(acc)
    wq, wk, wv, wg, wzf, wzb, wsq, wsk, wsv = jnp.split(w, pts, axis=-1)
    zpad = jnp.zeros((d, LANES - 2 * GLA_GATE_RANK), w.dtype)
    return jnp.concatenate([wq, wk, wv, wg, wzf, wzb, zpad, wsq, wsk, wsv], axis=-1).astype(BF16)


def _pad_up(up, row0):
    out = jnp.zeros((GLA_HEADS // 2, LANES, LANES), F32)
    return out.at[:, row0:row0 + GLA_GATE_RANK, :].set(up.reshape(GLA_GATE_RANK, GLA_HEADS // 2, LANES).transpose(1, 0, 2))


def _rope_tables(seq):
    rows = seq // GRID_W
    row = jnp.repeat(jnp.arange(rows, dtype=F32), GRID_W)
    col = jnp.tile(jnp.arange(GRID_W, dtype=F32), rows)
    n_freq = ROPE_AXIS_DIM // 2
    inv = ROPE_THETA ** (-(jnp.arange(n_freq, dtype=F32) * 2.0 / ROPE_AXIS_DIM))
    ang_r = row[:, None] * inv[None, :]
    ang_c = col[:, None] * inv[None, :]
    cr, sr, cc, sc = jnp.cos(ang_r), jnp.sin(ang_r), jnp.cos(ang_c), jnp.sin(ang_c)
    cos64 = jnp.concatenate([cr, cr, cc, cc], axis=-1)
    sin64 = jnp.concatenate([-sr, sr, -sc, sc], axis=-1)
    reps = LANES // SWA_HD
    return jnp.tile(cos64, (1, reps)), jnp.tile(sin64, (1, reps))


def _tile(n, pref):
    t = pref
    while n % t:
        t //= 2
    return t


def kernel(x, c, ctx, c_ctx, w_ada, b_ada, norm1, norm2, w_in, gla_up_f, gla_bias_f, gla_up_b, gla_bias_b,
           gla_norm, swa_sink, w_out, w_router, b_router, w_gate, w_up, w_down, final_norm):
    batch, seq, d = x.shape
    lc = ctx.shape[1]
    depth = w_ada.shape[0]
    n_l = batch * seq
    n_c = batch * lc
    npair = GLA_HEADS // 2
    assert batch + 1 <= SUBLANES
    assert n_l % ROUTE_TILE == 0 and n_c % ROUTE_TILE == 0

    tm_l = _tile(seq, 512)
    tm_c = _tile(lc, 256)
    tl_l = _tile(seq, 512)
    tl_c = _tile(lc, 512)

    rows = jnp.zeros((SUBLANES, d), F32).at[:batch].set(c).at[batch].set(c_ctx)
    mods = _ada_call(rows, w_ada, b_ada)
    mods5 = mods.reshape(depth, SUBLANES, 6, 1, d)

    tabs = _rope_tables(seq)
    wr_pad = jnp.zeros((d, LANES), F32).at[:, :N_EXPERTS].set(w_router)
    wr_hi = wr_pad.astype(BF16)
    wr_lo = (wr_pad - wr_hi.astype(F32)).astype(BF16)
    wr_cat = jnp.concatenate([wr_hi, wr_lo], axis=-1)

    def lat_row(tm):
        return lambda t: t // (seq // tm)

    def ctx_row(tm):
        return lambda t: batch

    xl = x.reshape(n_l, d)
    xc = ctx.reshape(n_c, d)
    for i in range(depth):
        last = i == depth - 1
        w_packed = _pack_w_in(w_in[i])
        upf = _pad_up(gla_up_f[i], 0)
        upb = _pad_up(gla_up_b[i], GLA_GATE_RANK)
        bias_f = gla_bias_f[i].reshape(npair, 1, LANES)
        bias_b = gla_bias_b[i].reshape(npair, 1, LANES)
        w_out_b = w_out[i].astype(BF16)

        c_qk, c_v, c_g, c_z, c_sq, c_sk, c_sv = _inproj_call(
            xc, norm1[i], mods5, i, ctx_row(tm_c), w_packed, None, tm_c, lc // tm_c)
        l_qk, l_v, l_g, l_z, l_sq, l_sk, l_sv = _inproj_call(
            xl, norm1[i], mods5, i, lat_row(tm_l), w_packed, tabs, tm_l, seq // tm_l)

        s_zero = jnp.zeros((batch, npair, 2, 2 * GLA_DV, LANES), F32)
        oc_f, oc_b, s_ctx = _gla_call(c_qk, c_v, c_z, upf, upb, bias_f, bias_b, s_zero, batch, lc, tl_c)
        ol_f, ol_b, _ = _gla_call(l_qk, l_v, l_z, upf, upb, bias_f, bias_b, s_ctx, batch, seq, tl_l)

        swa_l = _swa_call(l_sq, l_sk, l_sv, c_sk, c_sv, swa_sink[i], batch, seq, lc)
        xl, h2l, lg_l = _outproj_call(ol_f, ol_b, l_g, gla_norm[i], swa_l, w_out_b, xl, mods5, i, lat_row(tm_l),
                                      norm2[i], wr_cat, tm_l)
        if last:
            logits_t = lg_l
            t_all = n_l
        else:
            swa_c = _swa_ctx_call(c_sq, c_sk, c_sv, swa_sink[i], batch, lc)
            xc, h2c, lg_c = _outproj_call(oc_f, oc_b, c_g, gla_norm[i], swa_c, w_out_b, xc, mods5, i, ctx_row(tm_c),
                                          norm2[i], wr_cat, tm_c)
            logits_t = jnp.concatenate([lg_l, lg_c], axis=1)
            t_all = n_l + n_c

        io, wo, seg, pad, blk = _route_call(logits_t, b_router, ROUTE_TILE)
        m = MOE_BLOCK
        padded = blk[:, 0]
        pend = jnp.cumsum(padded)
        ntile = t_all // ROUTE_TILE
        nb = -(-(t_all * TOP_K + ntile * N_EXPERTS * (SUBLANES - 1)) // m) + N_EXPERTS
        p_rows = nb * m
        seg = seg.reshape(-1)
        pad = pad.reshape(-1)
        nused = (pend[-1:] // m).astype(I32)
        tail = nused + jnp.arange(nb - (t_all * TOP_K) // m, dtype=I32)
        zblk = jnp.concatenate([jnp.where(padded > 0, pend // m - 1, -1),
                                jnp.where(tail < nb, tail, -1)]).astype(I32)
        blk_row = jnp.minimum(jnp.arange(nb, dtype=I32), nused - 1) * m
        block_e = jnp.sum((pend[None, :] <= blk_row[:, None]).astype(I32), axis=1)

        sources = (h2l,) if last else (h2l, h2c)
        xb = _dispatch_call(seg, pad, zblk, io, wo, sources, p_rows)
        yb = _expert_call(block_e, nused, xb, w_gate, w_up, w_down, i)
        xl_new = _combine_call(seg, pad, yb, io, xl, mods5, i, lat_row(ROUTE_TILE),
                               final_norm if last else None, 0)
        if not last:
            xc = _combine_call(seg, pad, yb, io, xc, mods5, i, ctx_row(0), None, n_l // ROUTE_TILE)
        xl = xl_new
    return xl.reshape(batch, seq, d)
```

```python
import functools

import jax
import jax.numpy as jnp
from jax import lax
from jax.experimental import pallas as pl
from jax.experimental.pallas import tpu as pltpu

F32 = jnp.float32
BF16 = jnp.bfloat16
I32 = jnp.int32
U32 = jnp.uint32
HIGHEST = lax.Precision.HIGHEST

GRID_W = 64
NORM_EPS = 1e-6
GLA_HEADS = 4
GLA_DK = 64
GLA_DV = 128
GLA_KW = GLA_HEADS * GLA_DK
GLA_WIDTH = GLA_HEADS * GLA_DV
GLA_GATE_RANK = 16
GLA_TAU = 16.0
GLA_CHUNK = 64
SWA_HEADS = 8
SWA_KV_HEADS = 2
SWA_HD = 64
SWA_WIDTH = SWA_HEADS * SWA_HD
SWA_KVW = SWA_KV_HEADS * SWA_HD
WINDOW = 128
ATT_BLOCK = 128
ROPE_THETA = 10000.0
ROPE_AXIS_DIM = SWA_HD // 2
N_EXPERTS = 16
N_GROUPS = 4
EXPERTS_PER_GROUP = N_EXPERTS // N_GROUPS
TOP_K = 2
D_EXPERT = 512
MOE_BLOCK = 256
IN_SIZES = (GLA_KW, GLA_KW, GLA_WIDTH, GLA_WIDTH, GLA_GATE_RANK, GLA_GATE_RANK, SWA_WIDTH, SWA_KVW, SWA_KVW)

LANES = 128
SUBLANES = 8
VMEM_LIMIT = 48 * 1024 * 1024

C_QK = 0
C_V = 512
C_G = 1024
C_Z = 1536
C_SQ = 1664
C_SK = 2176
C_SV = 2304
C_END = 2432
SV_LANES = 4 * LANES
NEG_BIG = -1e30


def _cparams(sem):
    return pltpu.CompilerParams(dimension_semantics=sem, vmem_limit_bytes=VMEM_LIMIT)


def _pack_bf16_pairs(x):
    n = x.shape[1] // 2
    lo = lax.bitcast_convert_type(x[:, :n].astype(BF16).astype(F32), U32)
    hi = lax.bitcast_convert_type(x[:, n:].astype(BF16).astype(F32), U32)
    return (lo >> 16) | hi


def _unpack_bf16_pairs(w):
    lo = lax.bitcast_convert_type(w << 16, F32)
    hi = lax.bitcast_convert_type(w & jnp.uint32(0xFFFF0000), F32)
    return jnp.concatenate([lo.astype(BF16), hi.astype(BF16)], axis=-1)


def _ada_kernel(a_ref, w_ref, b_ref, o_ref):
    a = a_ref[...]
    act = a * jax.nn.sigmoid(a)
    o_ref[...] = jnp.dot(act, w_ref[...], preferred_element_type=F32, precision=HIGHEST) + b_ref[...]


def _ada_call(rows, w_ada, b_ada):
    depth, d, n6 = w_ada.shape
    tn = 1536
    return pl.pallas_call(
        _ada_kernel,
        grid=(depth, n6 // tn),
        in_specs=[
            pl.BlockSpec((SUBLANES, d), lambda l, j: (0, 0)),
            pl.BlockSpec((None, d, tn), lambda l, j: (l, 0, j)),
            pl.BlockSpec((None, 1, tn), lambda l, j: (l, 0, j)),
        ],
        out_specs=pl.BlockSpec((None, SUBLANES, tn), lambda l, j: (l, 0, j)),
        out_shape=jax.ShapeDtypeStruct((depth, SUBLANES, n6), F32),
        compiler_params=_cparams(("arbitrary", "arbitrary")),
        name="adaln",
    )(rows, w_ada, b_ada.reshape(depth, 1, n6))


def _rope_tile(xj, cos, sin, lane_lo):
    partner = jnp.where(lane_lo, pltpu.roll(xj, LANES - 16, 1), pltpu.roll(xj, 16, 1))
    return xj * cos + partner * sin


def _inproj_kernel(*refs, rope):
    if rope:
        (x_ref, g_ref, sc_ref, sh_ref, w_ref, cos_ref, sin_ref,
         qk_ref, v_ref, gate_ref, z_ref, sq_ref, sk_ref, sv_ref) = refs
    else:
        (x_ref, g_ref, sc_ref, sh_ref, w_ref,
         qk_ref, v_ref, gate_ref, z_ref, sq_ref, sk_ref, sv_ref) = refs
    x = x_ref[...]
    ms = jnp.mean(x * x, axis=-1, keepdims=True)
    h = (x * lax.rsqrt(ms + NORM_EPS)) * g_ref[...]
    h = h * (1.0 + sc_ref[...]) + sh_ref[...]
    hb = h.astype(BF16)

    def proj(a, b):
        return jnp.dot(hb, w_ref[:, a:b], preferred_element_type=F32)

    qk = proj(C_QK, C_V)
    qk_ref[:, :GLA_KW] = (qk[:, :GLA_KW] * (GLA_DK ** -0.5)).astype(qk_ref.dtype)
    qk_ref[:, GLA_KW:] = qk[:, GLA_KW:].astype(qk_ref.dtype)
    v_ref[...] = proj(C_V, C_G).astype(v_ref.dtype)
    gate_ref[...] = proj(C_G, C_Z).astype(gate_ref.dtype)
    z_ref[...] = proj(C_Z, C_SQ)
    sq = proj(C_SQ, C_SK) * (SWA_HD ** -0.5)
    sk = proj(C_SK, C_SV)
    sv = proj(C_SV, C_END)
    if rope:
        cos = cos_ref[...]
        sin = sin_ref[...]
        lane_lo = (lax.broadcasted_iota(I32, cos.shape, 1) % 32) < 16
        for j in range(SWA_WIDTH // LANES):
            sl = slice(j * LANES, (j + 1) * LANES)
            sq_ref[:, sl] = _rope_tile(sq[:, sl], cos, sin, lane_lo).astype(sq_ref.dtype)
        sk = _rope_tile(sk, cos, sin, lane_lo)
    else:
        sq_ref[...] = sq.astype(sq_ref.dtype)
    first = lax.broadcasted_iota(I32, sk.shape, 1) < SWA_HD
    sk_sw = pltpu.roll(sk, SWA_HD, 1)
    sk_ref[:, :LANES] = jnp.where(first, sk, sk_sw).astype(sk_ref.dtype)
    sk_ref[:, LANES:] = jnp.where(first, sk_sw, sk).astype(sk_ref.dtype)
    sv_sw = pltpu.roll(sv, SWA_HD, 1)
    sv_ref[:, 0 * LANES:1 * LANES] = jnp.where(first, sv, 1.0).astype(sv_ref.dtype)
    sv_ref[:, 1 * LANES:2 * LANES] = jnp.where(first, 1.0, sv_sw).astype(sv_ref.dtype)
    sv_ref[:, 2 * LANES:3 * LANES] = jnp.where(first, sv_sw, 1.0).astype(sv_ref.dtype)
    sv_ref[:, 3 * LANES:4 * LANES] = jnp.where(first, 1.0, sv).astype(sv_ref.dtype)


def _inproj_call(x2d, g, mods5, layer, row_of_tile, w_packed, tabs, tm, tiles_per_seq):
    n, d = x2d.shape
    nt = n // tm
    rope = tabs is not None

    def mod_spec(k):
        return pl.BlockSpec((None, None, None, 1, d), lambda t: (layer, row_of_tile(t), k, 0, 0))

    in_specs = [
        pl.BlockSpec((tm, d), lambda t: (t, 0)),
        pl.BlockSpec((1, d), lambda t: (0, 0)),
        mod_spec(1), mod_spec(0),
        pl.BlockSpec((d, C_END), lambda t: (0, 0)),
    ]
    args = [x2d, g.reshape(1, d), mods5, mods5, w_packed]
    if rope:
        in_specs += [pl.BlockSpec((tm, LANES), lambda t: (t % tiles_per_seq, 0))] * 2
        args += [tabs[0], tabs[1]]
    widths = (2 * GLA_KW, GLA_WIDTH, GLA_WIDTH, LANES, SWA_WIDTH, 2 * SWA_KVW, SV_LANES)
    dtypes = (BF16, BF16, BF16, F32, BF16, BF16, BF16)
    return pl.pallas_call(
        functools.partial(_inproj_kernel, rope=rope),
        grid=(nt,),
        in_specs=in_specs,
        out_specs=[pl.BlockSpec((tm, w), lambda t: (t, 0)) for w in widths],
        out_shape=[jax.ShapeDtypeStruct((n, w), dt) for w, dt in zip(widths, dtypes)],
        compiler_params=_cparams(("arbitrary",)),
        name="inproj",
    )(*args)


def _log_sigmoid(x):
    return jnp.minimum(x, 0.0) - jnp.log(1.0 + jnp.exp(-jnp.abs(x)))


def _gla_prepare(q_ref, k_ref, v_ref, z_ref, up_ref, bias_ref, tri_ref, scr):
    qin_s, km0_s, km1_s, kout_s, vm0_s, vm1_s, dec_s = scr[:7]
    tl = q_ref.shape[0]
    x = jnp.dot(z_ref[...], up_ref[...], preferred_element_type=F32, precision=HIGHEST) + bias_ref[...]
    la = _log_sigmoid(x) * (1.0 / GLA_TAU)
    hi = la.astype(BF16)
    lo = (la - hi.astype(F32)).astype(BF16)
    sums = jnp.dot(tri_ref[...], jnp.concatenate([hi, lo], axis=1), preferred_element_type=F32)
    b = sums[:tl, :LANES] + sums[:tl, LANES:]
    btot = sums[tl:, :LANES] + sums[tl:, LANES:]
    q = q_ref[...].astype(F32)
    k = k_ref[...].astype(F32)
    qin_s[...] = (q * jnp.exp(b)).astype(BF16)
    k_in = k * jnp.exp(-b)
    head0 = lax.broadcasted_iota(I32, (tl, LANES), 1) < GLA_DK
    km0_s[...] = jnp.where(head0, k_in, 0.0).astype(BF16)
    km1_s[...] = jnp.where(head0, 0.0, k_in).astype(BF16)
    kout_s[...] = (k * jnp.exp(btot - b)).astype(BF16)
    dec_s[...] = jnp.exp(btot)
    vf = v_ref[...].astype(F32)
    vhead0 = lax.broadcasted_iota(I32, (tl, 2 * GLA_DV), 1) < GLA_DV
    vm0_s[...] = jnp.where(vhead0, vf, 0.0).astype(BF16)
    vm1_s[...] = jnp.where(vhead0, 0.0, vf).astype(BF16)


def _gla_increment(v_ref, scr, r0):
    kout_s, u_s = scr[3], scr[7]
    c = GLA_CHUNK
    rows = pl.ds(r0, c)
    u_t = lax.dot_general(v_ref[rows, :], kout_s[rows, :], (((0,), (0,)), ((), ())),
                          preferred_element_type=F32)
    srow = lax.broadcasted_iota(I32, (2 * GLA_DV, LANES), 0) // GLA_DV
    scol = lax.broadcasted_iota(I32, (2 * GLA_DV, LANES), 1) // GLA_DK
    u_s[r0 // c] = jnp.where(srow == scol, u_t, 0.0)


def _gla_states(scr, s_ref, chunk_order):
    dec_s, u_s, sprev_s = scr[6], scr[7], scr[8]
    s_t = s_ref[...]
    for cidx in chunk_order:
        sprev_s[cidx] = s_t.astype(BF16)
        r0 = cidx * GLA_CHUNK
        s_t = s_t * dec_s[r0:r0 + 1, :] + u_s[cidx]
    s_ref[...] = s_t


def _gla_output(o_ref, scr, r0, fwd):
    qin_s, km0_s, km1_s, _, vm0_s, vm1_s, _, _, sprev_s = scr
    c = GLA_CHUNK
    rows = pl.ds(r0, c)
    q_in = qin_s[rows, :]
    kst = jnp.concatenate([km0_s[rows, :], km1_s[rows, :]], axis=0)
    a = lax.dot_general(q_in, kst, (((1,), (1,)), ((), ())), preferred_element_type=F32)
    r2 = lax.broadcasted_iota(I32, (c, 2 * c), 0)
    c2 = lax.broadcasted_iota(I32, (c, 2 * c), 1) % c
    keep = (c2 <= r2) if fwd else (c2 >= r2)
    a = jnp.where(keep, a, 0.0).astype(BF16)
    vbd = jnp.concatenate([vm0_s[rows, :], vm1_s[rows, :]], axis=0)
    o = jnp.dot(a, vbd, preferred_element_type=F32)
    o = o + lax.dot_general(q_in, sprev_s[r0 // c], (((1,), (1,)), ((), ())), preferred_element_type=F32)
    o_ref[rows, :] = o.astype(o_ref.dtype)


def _gla_kernel(qf_ref, kf_ref, vf_ref, zf_ref, qb_ref, kb_ref, vb_ref, zb_ref,
                upf_ref, upb_ref, bf_ref, bb_ref, trif_ref, trib_ref, s0_ref,
                of_ref, ob_ref, sfin_ref, sf_scr, sb_scr, *scr, nchunk):
    i = pl.program_id(2)
    nt = pl.num_programs(2)
    scr_f, scr_b = scr[:len(scr) // 2], scr[len(scr) // 2:]

    @pl.when(i == 0)
    def _():
        sf_scr[...] = s0_ref[0]
        sb_scr[...] = s0_ref[1]

    _gla_prepare(qf_ref, kf_ref, vf_ref, zf_ref, upf_ref, bf_ref, trif_ref, scr_f)
    _gla_prepare(qb_ref, kb_ref, vb_ref, zb_ref, upb_ref, bb_ref, trib_ref, scr_b)
    for cidx in range(nchunk):
        _gla_increment(vf_ref, scr_f, cidx * GLA_CHUNK)
        _gla_increment(vb_ref, scr_b, cidx * GLA_CHUNK)
    _gla_states(scr_f, sf_scr, range(nchunk))
    _gla_states(scr_b, sb_scr, range(nchunk - 1, -1, -1))
    for cidx in range(nchunk):
        _gla_output(of_ref, scr_f, cidx * GLA_CHUNK, True)
        _gla_output(ob_ref, scr_b, cidx * GLA_CHUNK, False)

    @pl.when(i == nt - 1)
    def _():
        sfin_ref[0] = sf_scr[...]
        sfin_ref[1] = sb_scr[...]


def _gla_call(qk, v, z, upf_pad, upb_pad, bias_f, bias_b, s0, batch, seq, tl):
    n = qk.shape[0]
    nt = seq // tl
    npair = GLA_HEADS // 2

    def fwd_row(b, p, i):
        return b * nt + i

    def bwd_row(b, p, i):
        return b * nt + (nt - 1 - i)

    def specs(row):
        return [
            pl.BlockSpec((tl, LANES), lambda b, p, i: (row(b, p, i), p)),
            pl.BlockSpec((tl, LANES), lambda b, p, i: (row(b, p, i), npair + p)),
            pl.BlockSpec((tl, 2 * GLA_DV), lambda b, p, i: (row(b, p, i), p)),
            pl.BlockSpec((tl, LANES), lambda b, p, i: (row(b, p, i), 0)),
        ]

    in_specs = specs(fwd_row) + specs(bwd_row) + [
        pl.BlockSpec((None, LANES, LANES), lambda b, p, i: (p, 0, 0)),
        pl.BlockSpec((None, LANES, LANES), lambda b, p, i: (p, 0, 0)),
        pl.BlockSpec((None, 1, LANES), lambda b, p, i: (p, 0, 0)),
        pl.BlockSpec((None, 1, LANES), lambda b, p, i: (p, 0, 0)),
        pl.BlockSpec((None, 2 * tl, tl), lambda b, p, i: (0, 0, 0)),
        pl.BlockSpec((None, 2 * tl, tl), lambda b, p, i: (1, 0, 0)),
        pl.BlockSpec((None, None, 2, 2 * GLA_DV, LANES), lambda b, p, i: (b, p, 0, 0, 0)),
    ]
    ri = jnp.arange(tl)[:, None]
    ci = jnp.arange(tl)[None, :]
    same = (ri // GLA_CHUNK) == (ci // GLA_CHUNK)
    tri = jnp.stack([jnp.concatenate([same & (ci <= ri), same], axis=0),
                     jnp.concatenate([same & (ci >= ri), same], axis=0)]).astype(BF16)
    dir_scratch = [
        pltpu.VMEM((tl, LANES), BF16), pltpu.VMEM((tl, LANES), BF16), pltpu.VMEM((tl, LANES), BF16),
        pltpu.VMEM((tl, LANES), BF16), pltpu.VMEM((tl, 2 * GLA_DV), BF16), pltpu.VMEM((tl, 2 * GLA_DV), BF16),
        pltpu.VMEM((tl, LANES), F32),
        pltpu.VMEM((tl // GLA_CHUNK, 2 * GLA_DV, LANES), F32),
        pltpu.VMEM((tl // GLA_CHUNK, 2 * GLA_DV, LANES), BF16),
    ]
    out_specs = [
        pl.BlockSpec((tl, 2 * GLA_DV), lambda b, p, i: (fwd_row(b, p, i), p)),
        pl.BlockSpec((tl, 2 * GLA_DV), lambda b, p, i: (bwd_row(b, p, i), p)),
        pl.BlockSpec((None, None, 2, 2 * GLA_DV, LANES), lambda b, p, i: (b, p, 0, 0, 0)),
    ]
    out_shape = [
        jax.ShapeDtypeStruct((n, GLA_WIDTH), BF16),
        jax.ShapeDtypeStruct((n, GLA_WIDTH), BF16),
        jax.ShapeDtypeStruct((batch, npair, 2, 2 * GLA_DV, LANES), F32),
    ]
    return pl.pallas_call(
        functools.partial(_gla_kernel, nchunk=tl // GLA_CHUNK),
        grid=(batch, npair, nt),
        in_specs=in_specs,
        out_specs=out_specs,
        out_shape=out_shape,
        scratch_shapes=[pltpu.VMEM((2 * GLA_DV, LANES), F32), pltpu.VMEM((2 * GLA_DV, LANES), F32)]
        + dir_scratch + dir_scratch,
        compiler_params=_cparams(("arbitrary", "arbitrary", "arbitrary")),
        name="gla",
    )(qk, qk, v, z, qk, qk, v, z, upf_pad, upb_pad, bias_f, bias_b, tri, tri, s0)


def _swa_kernel(*refs, nb, local):
    if local:
        sink_ref, q_ref, kp_ref, kc_ref, kn_ref, vp_ref, vc_ref, vn_ref, kx_ref, vx_ref, o_ref = refs
        kall = jnp.concatenate([kp_ref[...], kc_ref[...], kn_ref[...], kx_ref[...]], axis=0)
        vall = jnp.concatenate([vp_ref[...], vc_ref[...], vn_ref[...], vx_ref[...]], axis=0)
    else:
        sink_ref, q_ref, kx_ref, vx_ref, o_ref = refs
        kall = kx_ref[...]
        vall = vx_ref[...]
    w = ATT_BLOCK
    nk = kall.shape[0]
    r2 = lax.broadcasted_iota(I32, (2 * w, w), 0) % w
    c2 = lax.broadcasted_iota(I32, (2 * w, w), 1)
    if local:
        i = pl.program_id(1)
        bias_prev = jnp.where((c2 >= r2) & (i > 0), 0.0, NEG_BIG)
        bias_next = jnp.where((c2 <= r2) & (i < nb - 1), 0.0, NEG_BIG)
    first = c2 < SWA_HD
    top = lax.broadcasted_iota(I32, (2 * w, 1), 0) < w
    scores = []
    for g in range(SWA_KV_HEADS):
        kd = kall[:, g * LANES:(g + 1) * LANES]
        qs = jnp.concatenate([q_ref[:, (2 * g) * LANES:(2 * g + 1) * LANES],
                              q_ref[:, (2 * g + 1) * LANES:(2 * g + 2) * LANES]], axis=0).astype(F32)
        for half in range(2):
            qm = (jnp.where(first, qs, 0.0) if half == 0 else jnp.where(first, 0.0, qs)).astype(BF16)
            s = lax.dot_general(qm, kd, (((1,), (1,)), ((), ())), preferred_element_type=F32)
            if local:
                s = jnp.concatenate([s[:, :w] + bias_prev, s[:, w:2 * w], s[:, 2 * w:3 * w] + bias_next,
                                     s[:, 3 * w:]], axis=1)
            scores.append(s)
    for g in range(SWA_KV_HEADS):
        outs = []
        for half in range(2):
            s = scores[2 * g + half]
            sk = jnp.where(top, sink_ref[4 * g + half], sink_ref[4 * g + 2 + half])
            m = jnp.maximum(jnp.max(s, axis=-1, keepdims=True), sk)
            p = jnp.exp((s - m).astype(BF16))
            va = vall[:, (2 * g + half) * LANES:(2 * g + half + 1) * LANES]
            acc = jnp.dot(p, va, preferred_element_type=F32)
            den = pltpu.roll(acc, SWA_HD, 1) + jnp.exp(sk - m)
            outs.append(acc / den)
        out = jnp.where(first, outs[0], outs[1])
        o_ref[:, (2 * g) * LANES:(2 * g + 1) * LANES] = out[:w].astype(o_ref.dtype)
        o_ref[:, (2 * g + 1) * LANES:(2 * g + 2) * LANES] = out[w:].astype(o_ref.dtype)


def _swa_call(sq, skd, svd, kcd, vcd, sink, batch, seq, lc):
    n = sq.shape[0]
    w = ATT_BLOCK
    nb = seq // w
    kvw = 2 * SWA_KVW

    def blk(off):
        return lambda b, i: (b * nb + jnp.clip(i + off, 0, nb - 1), 0)

    def kv_specs(width):
        return [pl.BlockSpec((w, width), blk(-1)), pl.BlockSpec((w, width), blk(0)), pl.BlockSpec((w, width), blk(1))]

    def ctx_spec(width):
        return pl.BlockSpec((lc, width), lambda b, i: (b, 0))

    return pl.pallas_call(
        functools.partial(_swa_kernel, nb=nb, local=True),
        grid=(batch, nb),
        in_specs=[pl.BlockSpec(memory_space=pltpu.SMEM), pl.BlockSpec((w, SWA_WIDTH), blk(0))]
        + kv_specs(kvw) + kv_specs(SV_LANES) + [ctx_spec(kvw), ctx_spec(SV_LANES)],
        out_specs=pl.BlockSpec((w, SWA_WIDTH), blk(0)),
        out_shape=jax.ShapeDtypeStruct((n, SWA_WIDTH), BF16),
        compiler_params=_cparams(("arbitrary", "arbitrary")),
        name="swa",
    )(sink, sq, skd, skd, skd, svd, svd, svd, kcd, vcd)


def _swa_ctx_call(sq, kcd, vcd, sink, batch, lc):
    n = sq.shape[0]
    w = ATT_BLOCK
    nb = lc // w
    kvw = 2 * SWA_KVW
    return pl.pallas_call(
        functools.partial(_swa_kernel, nb=nb, local=False),
        grid=(batch, nb),
        in_specs=[pl.BlockSpec(memory_space=pltpu.SMEM),
                  pl.BlockSpec((w, SWA_WIDTH), lambda b, i: (b * nb + i, 0)),
                  pl.BlockSpec((lc, kvw), lambda b, i: (b, 0)),
                  pl.BlockSpec((lc, SV_LANES), lambda b, i: (b, 0))],
        out_specs=pl.BlockSpec((w, SWA_WIDTH), lambda b, i: (b * nb + i, 0)),
        out_shape=jax.ShapeDtypeStruct((n, SWA_WIDTH), BF16),
        compiler_params=_cparams(("arbitrary", "arbitrary")),
        name="swa_ctx",
    )(sink, sq, kcd, vcd)


def _outproj_kernel(of_ref, ob_ref, gate_ref, gn_ref, swa_ref, w_ref, x_ref, g1_ref, n2_ref, sc_ref, sh_ref,
                    wr_ref, xo_ref, h2_ref, lg_ref):
    o = of_ref[...].astype(F32) + ob_ref[...].astype(F32)
    parts = []
    for h in range(GLA_HEADS):
        oh = o[:, h * GLA_DV:(h + 1) * GLA_DV]
        ms = jnp.mean(oh * oh, axis=-1, keepdims=True)
        parts.append(oh * lax.rsqrt(ms + NORM_EPS))
    on = jnp.concatenate(parts, axis=-1) * gn_ref[...]
    gate = gate_ref[...].astype(F32)
    gla = on * (gate * jax.nn.sigmoid(gate))
    mix = jnp.concatenate([gla.astype(BF16), swa_ref[...]], axis=-1)
    y = jnp.dot(mix, w_ref[...], preferred_element_type=F32)
    xo = x_ref[...] + g1_ref[...] * y
    xo_ref[...] = xo
    ms = jnp.mean(xo * xo, axis=-1, keepdims=True)
    h2 = (xo * lax.rsqrt(ms + NORM_EPS)) * n2_ref[...]
    h2 = h2 * (1.0 + sc_ref[...]) + sh_ref[...]
    h2_ref[...] = _pack_bf16_pairs(h2)
    hi = h2.astype(BF16)
    lo = (h2 - hi.astype(F32)).astype(BF16)
    both = jnp.dot(hi, wr_ref[...], preferred_element_type=F32)
    lg = both[:, :LANES] + both[:, LANES:] + jnp.dot(lo, wr_ref[:, :LANES], preferred_element_type=F32)
    lg_ref[...] = jnp.transpose(lg)[:N_EXPERTS, :]


def _outproj_call(o_f, o_b, gate, gn, swa, w_out_b, x2d, mods5, layer, row_of_tile, n2, wr_cat, tm):
    n, d = x2d.shape
    nt = n // tm

    def mod_spec(k):
        return pl.BlockSpec((None, None, None, 1, d), lambda t: (layer, row_of_tile(t), k, 0, 0))

    return pl.pallas_call(
        _outproj_kernel,
        grid=(nt,),
        in_specs=[
            pl.BlockSpec((tm, GLA_WIDTH), lambda t: (t, 0)),
            pl.BlockSpec((tm, GLA_WIDTH), lambda t: (t, 0)),
            pl.BlockSpec((tm, GLA_WIDTH), lambda t: (t, 0)),
            pl.BlockSpec((1, GLA_WIDTH), lambda t: (0, 0)),
            pl.BlockSpec((tm, SWA_WIDTH), lambda t: (t, 0)),
            pl.BlockSpec((d, d), lambda t: (0, 0)),
            pl.BlockSpec((tm, d), lambda t: (t, 0)),
            mod_spec(2),
            pl.BlockSpec((1, d), lambda t: (0, 0)),
            mod_spec(4), mod_spec(3),
            pl.BlockSpec((d, 2 * LANES), lambda t: (0, 0)),
        ],
        out_specs=[
            pl.BlockSpec((tm, d), lambda t: (t, 0)),
            pl.BlockSpec((tm, d // 2), lambda t: (t, 0)),
            pl.BlockSpec((N_EXPERTS, tm), lambda t: (0, t)),
        ],
        out_shape=[
            jax.ShapeDtypeStruct((n, d), F32),
            jax.ShapeDtypeStruct((n, d // 2), U32),
            jax.ShapeDtypeStruct((N_EXPERTS, n), F32),
        ],
        compiler_params=_cparams(("arbitrary",)),
        name="outproj",
    )(o_f, o_b, gate, gn.reshape(1, GLA_WIDTH), swa, w_out_b, x2d, mods5, n2.reshape(1, d), mods5, mods5, wr_cat)


def _first_index(vals, target):
    idx = jnp.full(target.shape, len(vals) - 1, I32)
    for i in range(len(vals) - 2, -1, -1):
        idx = jnp.where(vals[i] == target, i, idx)
    return idx


def _route_kernel(lg_ref, br_ref, io_ref, wo_ref, seg_ref, pad_ref, blk_ref, cnt_scr, seg_scr, pad_scr, *, tn):
    phase = pl.program_id(0)
    t = pl.program_id(1)
    tile_lane = lax.broadcasted_iota(I32, (N_EXPERTS, LANES), 1)

    @pl.when((phase == 0) & (t == 0))
    def _():
        cnt_scr[...] = jnp.zeros_like(cnt_scr)
        seg_scr[...] = jnp.zeros_like(seg_scr)
        pad_scr[...] = jnp.zeros_like(pad_scr)

    @pl.when((phase == 1) & (t == 0))
    def _():
        pad8 = jnp.floor((cnt_scr[...] + (SUBLANES - 1)) * (1.0 / SUBLANES)) * SUBLANES
        rows_e = jnp.sum(pad8, axis=1, keepdims=True)
        blocks = jnp.floor((rows_e + (MOE_BLOCK - 1)) * (1.0 / MOE_BLOCK)) * MOE_BLOCK
        r128 = lax.broadcasted_iota(I32, (LANES, LANES), 0)
        c128 = lax.broadcasted_iota(I32, (LANES, LANES), 1)
        before_tile = jnp.where(r128 < c128, 1.0, 0.0).astype(BF16)
        seg = jnp.dot((pad8 * (1.0 / SUBLANES)).astype(BF16), before_tile,
                      preferred_element_type=F32) * SUBLANES
        run = jnp.zeros((1, 1), F32)
        for e in range(N_EXPERTS):
            seg_scr[e:e + 1, :] = seg[e:e + 1, :] + run
            run = run + blocks[e:e + 1, :]
        pad_scr[...] = pad8
        seg_ref[...] = seg_scr[...].astype(I32)
        pad_ref[...] = pad8.astype(I32)
        blk_ref[...] = jnp.broadcast_to(blocks, blk_ref.shape).astype(I32)

    s = jax.nn.sigmoid(lg_ref[...])
    sb = s + br_ref[...]
    rows_s = [s[e:e + 1, :] for e in range(N_EXPERTS)]
    rows_b = [sb[e:e + 1, :] for e in range(N_EXPERTS)]
    gscore, gi1, gi2 = [], [], []
    epg = EXPERTS_PER_GROUP
    for g in range(N_GROUPS):
        a = rows_b[g * epg:(g + 1) * epg]
        m1 = functools.reduce(jnp.maximum, a)
        i1 = _first_index(a, m1)
        rest = [jnp.where(i1 == i, -jnp.inf, a[i]) for i in range(epg)]
        m2 = functools.reduce(jnp.maximum, rest)
        i2 = _first_index(rest, m2)
        gscore.append(m1 + m2)
        gi1.append(i1)
        gi2.append(i2)
    gm = functools.reduce(jnp.maximum, gscore)
    gsel = _first_index(gscore, gm)
    i1 = gi1[N_GROUPS - 1]
    i2 = gi2[N_GROUPS - 1]
    for g in range(N_GROUPS - 2, -1, -1):
        i1 = jnp.where(gsel == g, gi1[g], i1)
        i2 = jnp.where(gsel == g, gi2[g], i2)
    idx0 = gsel * epg + i1
    idx1 = gsel * epg + i2
    s0 = jnp.zeros_like(rows_s[0])
    s1 = jnp.zeros_like(rows_s[0])
    for e in range(N_EXPERTS):
        s0 = jnp.where(idx0 == e, rows_s[e], s0)
        s1 = jnp.where(idx1 == e, rows_s[e], s1)
    tot = s0 + s1
    w0 = s0 / tot
    w1 = s1 / tot

    eidx = lax.broadcasted_iota(I32, (N_EXPERTS, tn), 0)
    oh0 = eidx == idx0
    oh1 = eidx == idx1
    oh = jnp.where(oh0 | oh1, 1.0, 0.0)
    rr = lax.broadcasted_iota(I32, (tn, tn), 0)
    cc = lax.broadcasted_iota(I32, (tn, tn), 1)
    upper = jnp.where(rr < cc, 1.0, 0.0).astype(BF16)
    before = jnp.dot(oh.astype(BF16), upper, preferred_element_type=F32)
    @pl.when(phase == 0)
    def _():
        cnt_scr[...] = cnt_scr[...] + jnp.where(tile_lane == t, jnp.sum(oh, axis=1, keepdims=True), 0.0)

    pad_col = jnp.sum(jnp.where(tile_lane == t, pad_scr[...], 0.0), axis=1, keepdims=True)
    run = jnp.zeros((1, 1), F32)
    offs = []
    for e in range(N_EXPERTS):
        offs.append(run)
        run = run + pad_col[e:e + 1, :]
    pos = before + jnp.concatenate(offs, axis=0)
    spos0 = jnp.sum(jnp.where(oh0, pos, 0.0), axis=0, keepdims=True)
    spos1 = jnp.sum(jnp.where(oh1, pos, 0.0), axis=0, keepdims=True)
    zi = jnp.zeros((SUBLANES - 4, tn), I32)
    io_ref[...] = jnp.concatenate([idx0, idx1, spos0.astype(I32), spos1.astype(I32), zi], axis=0)
    wo_ref[...] = jnp.concatenate([w0, w1, jnp.zeros((SUBLANES - 2, tn), F32)], axis=0)


def _route_call(logits_t, b_router, tn):
    ne, t_all = logits_t.shape
    assert t_all // tn <= LANES
    tok_out = pl.BlockSpec((SUBLANES, tn), lambda p, t: (0, t * p))
    tab_out = pl.BlockSpec((ne, LANES), lambda p, t: (0, 0))
    tab = jax.ShapeDtypeStruct((ne, LANES), I32)
    return pl.pallas_call(
        functools.partial(_route_kernel, tn=tn),
        grid=(2, t_all // tn),
        in_specs=[pl.BlockSpec((ne, tn), lambda p, t: (0, t)), pl.BlockSpec((ne, 1), lambda p, t: (0, 0))],
        out_specs=[tok_out, tok_out, tab_out, tab_out, tab_out],
        out_shape=[
            jax.ShapeDtypeStruct((SUBLANES, t_all), I32),
            jax.ShapeDtypeStruct((SUBLANES, t_all), F32),
            tab, tab, tab,
        ],
        scratch_shapes=[pltpu.VMEM((ne, LANES), F32)] * 3,
        compiler_params=_cparams(("arbitrary", "arbitrary")),
        name="route",
    )(logits_t, b_router.reshape(ne, 1))


SEG_SIZES = (512, 256, 128, 64, 32, 16, 8)
ROUTE_TILE = 512
STAGE_ROWS = TOP_K * ROUTE_TILE + LANES


def _segment_copies(seg_ref, pad_ref, t, stage, hbm, sem, to_hbm, wait):
    loc = 0
    for e in range(N_EXPERTS):
        n = pad_ref[e * LANES + t]
        start = seg_ref[e * LANES + t]
        for size in SEG_SIZES:
            off = n & (-2 * size)

            @pl.when((n & size) != 0)
            def _(off=off, size=size, loc=loc, start=start):
                s_rows = pl.ds(pl.multiple_of(loc + off, SUBLANES), size)
                h_rows = pl.ds(pl.multiple_of(start + off, SUBLANES), size)
                src, dst = (stage.at[s_rows, :], hbm.at[h_rows, :])
                if not to_hbm:
                    src, dst = dst, src
                cp = pltpu.make_async_copy(src, dst, sem)
                if wait:
                    cp.wait()
                else:
                    cp.start()
        loc = loc + n


def _sort_matrix(io_ref, rows):
    r = lax.broadcasted_iota(I32, (rows, io_ref.shape[1]), 0)
    return r == io_ref[2:3, :], r == io_ref[3:4, :]


def _dispatch_kernel(seg_ref, pad_ref, zblk_ref, *refs, tiles):
    srcs = refs[:len(tiles)]
    io_ref, wo_ref, xb_hbm, stage, zbuf, sem = refs[len(tiles):]
    i = pl.program_id(0)
    m = MOE_BLOCK

    def zero_copy(j):
        start = pl.multiple_of(zblk_ref[j] * m, m)
        return pltpu.make_async_copy(zbuf, xb_hbm.at[pl.ds(start, m), :], sem.at[0])

    @pl.when(i == 0)
    def _():
        zbuf[...] = jnp.zeros_like(zbuf)
        for j in range(zblk_ref.shape[0]):
            @pl.when(zblk_ref[j] >= 0)
            def _():
                zero_copy(j).start()
        for j in range(zblk_ref.shape[0]):
            @pl.when(zblk_ref[j] >= 0)
            def _():
                zero_copy(j).wait()

    hp = srcs[0][...]
    if len(srcs) == 2:
        hp = jnp.where(i < tiles[0], hp, srcs[1][...])
    x = _unpack_bf16_pairs(hp)
    m0, m1 = _sort_matrix(io_ref, STAGE_ROWS)
    sort = jnp.where(m0 | m1, 1.0, 0.0).astype(BF16)
    xs = jnp.dot(sort, x, preferred_element_type=F32)
    dh = x.shape[1] // 2
    slot = i % 2
    cur = stage.at[slot]
    cur[:, :dh] = _pack_bf16_pairs(xs)
    ws = jnp.sum(jnp.where(m0, wo_ref[0:1, :], 0.0) + jnp.where(m1, wo_ref[1:2, :], 0.0), axis=1, keepdims=True)
    cur[:, dh:] = jnp.broadcast_to(lax.bitcast_convert_type(ws, U32), (STAGE_ROWS, LANES))
    _segment_copies(seg_ref, pad_ref, i, cur, xb_hbm, sem.at[slot], to_hbm=True, wait=False)

    @pl.when(i > 0)
    def _():
        _segment_copies(seg_ref, pad_ref, i - 1, stage.at[1 - slot], xb_hbm, sem.at[1 - slot], to_hbm=True, wait=True)

    @pl.when(i == pl.num_programs(0) - 1)
    def _():
        _segment_copies(seg_ref, pad_ref, i, cur, xb_hbm, sem.at[slot], to_hbm=True, wait=True)


def _dispatch_call(seg, pad, zblk, io, wo, sources, p_rows):
    tile = ROUTE_TILE
    dh = sources[0].shape[1]
    tiles = tuple(s.shape[0] // tile for s in sources)
    firsts = tuple(sum(tiles[:k]) for k in range(len(tiles)))

    def src_spec(first, ntile):
        return pl.BlockSpec((tile, dh), lambda i, *_: (jnp.clip(i - first, 0, ntile - 1), 0))

    tok_spec = pl.BlockSpec((SUBLANES, tile), lambda i, *_: (0, i))
    grid_spec = pltpu.PrefetchScalarGridSpec(
        num_scalar_prefetch=3,
        grid=(sum(tiles),),
        in_specs=[src_spec(f, n) for f, n in zip(firsts, tiles)] + [tok_spec, tok_spec],
        out_specs=pl.BlockSpec(memory_space=pl.ANY),
        scratch_shapes=[pltpu.VMEM((2, STAGE_ROWS, dh + LANES), U32), pltpu.VMEM((MOE_BLOCK, dh + LANES), U32),
                        pltpu.SemaphoreType.DMA((2,))],
    )
    return pl.pallas_call(
        functools.partial(_dispatch_kernel, tiles=tiles),
        grid_spec=grid_spec,
        out_shape=jax.ShapeDtypeStruct((p_rows, dh + LANES), U32),
        compiler_params=_cparams(("arbitrary",)),
        name="dispatch",
    )(seg, pad, zblk, *sources, io, wo)


def _expert_kernel(be_ref, nu_ref, x_ref, wg_ref, wu_ref, wd_ref, o_ref, wg_b, wu_b, wd_b):
    j = pl.program_id(0)

    @pl.when((j == 0) | (be_ref[j] != be_ref[jnp.maximum(j - 1, 0)]))
    def _():
        wg_b[...] = wg_ref[...].astype(BF16)
        wu_b[...] = wu_ref[...].astype(BF16)
        wd_b[...] = wd_ref[...].astype(BF16)

    @pl.when(j < nu_ref[0])
    def _():
        dh = o_ref.shape[1]
        half = o_ref.shape[0] // 2
        gates = []
        for r in range(2):
            x = _unpack_bf16_pairs(x_ref[r * half:(r + 1) * half, :dh])
            gates.append((jnp.dot(x, wg_b[...], preferred_element_type=F32),
                          jnp.dot(x, wu_b[...], preferred_element_type=F32)))
        for r in range(2):
            a, u = gates[r]
            hmid = (a * jax.nn.sigmoid(a)) * u
            y = jnp.dot(hmid.astype(BF16), wd_b[...], preferred_element_type=F32)
            row_w = lax.bitcast_convert_type(x_ref[r * half:(r + 1) * half, dh:dh + 1], F32)
            o_ref[r * half:(r + 1) * half, :] = _pack_bf16_pairs(y * row_w)

    @pl.when(j >= nu_ref[0])
    def _():
        o_ref[...] = jnp.zeros_like(o_ref)


def _expert_call(block_e, nused, xb, wg, wu, wd, layer):
    p_rows = xb.shape[0]
    dh = xb.shape[1] - LANES
    d = 2 * dh
    m = MOE_BLOCK
    nb = p_rows // m
    de = wg.shape[-1]

    def xmap(j, be, nu):
        return (jnp.minimum(j, nu[0] - 1), 0)

    def wmap(j, be, nu):
        return (layer, be[j], 0, 0)

    grid_spec = pltpu.PrefetchScalarGridSpec(
        num_scalar_prefetch=2,
        grid=(nb,),
        in_specs=[
            pl.BlockSpec((m, dh + LANES), xmap),
            pl.BlockSpec((None, None, d, de), wmap),
            pl.BlockSpec((None, None, d, de), wmap),
            pl.BlockSpec((None, None, de, d), wmap),
        ],
        out_specs=pl.BlockSpec((m, dh), lambda j, be, nu: (j, 0)),
        scratch_shapes=[pltpu.VMEM((d, de), BF16), pltpu.VMEM((d, de), BF16), pltpu.VMEM((de, d), BF16)],
    )
    return pl.pallas_call(
        _expert_kernel,
        grid_spec=grid_spec,
        out_shape=jax.ShapeDtypeStruct((p_rows, dh), U32),
        compiler_params=_cparams(("arbitrary",)),
        name="experts",
    )(block_e, nused, xb, wg, wu, wd)


def _combine_kernel(*refs, tile0, final):
    if final:
        seg_ref, pad_ref, yb_hbm, io_ref, x_ref, g2_ref, fn_ref, o_ref, stage, sem = refs
    else:
        seg_ref, pad_ref, yb_hbm, io_ref, x_ref, g2_ref, o_ref, stage, sem = refs
    i = pl.program_id(0)

    slot = i % 2

    @pl.when(i == 0)
    def _():
        stage[...] = jnp.zeros_like(stage)
        _segment_copies(seg_ref, pad_ref, tile0, stage.at[0], yb_hbm, sem.at[0], to_hbm=False, wait=False)

    @pl.when(i + 1 < pl.num_programs(0))
    def _():
        _segment_copies(seg_ref, pad_ref, tile0 + i + 1, stage.at[1 - slot], yb_hbm, sem.at[1 - slot],
                        to_hbm=False, wait=False)

    _segment_copies(seg_ref, pad_ref, tile0 + i, stage.at[slot], yb_hbm, sem.at[slot], to_hbm=False, wait=True)
    rows = _unpack_bf16_pairs(stage[slot])
    m0, m1 = _sort_matrix(io_ref, STAGE_ROWS)
    pick = jnp.where(m0 | m1, 1.0, 0.0).astype(BF16)
    y = lax.dot_general(pick, rows, (((0,), (0,)), ((), ())), preferred_element_type=F32)
    xo = x_ref[...] + g2_ref[...] * y
    if final:
        ms = jnp.mean(xo * xo, axis=-1, keepdims=True)
        xo = (xo * lax.rsqrt(ms + NORM_EPS)) * fn_ref[...]
    o_ref[...] = xo


def _combine_call(seg, pad, yb, io, x2d, mods5, layer, row_of_tile, final_g, tile0):
    n, d = x2d.shape
    tm = ROUTE_TILE
    final = final_g is not None
    in_specs = [
        pl.BlockSpec(memory_space=pl.ANY),
        pl.BlockSpec((SUBLANES, tm), lambda t, *_: (0, tile0 + t)),
        pl.BlockSpec((tm, d), lambda t, *_: (t, 0)),
        pl.BlockSpec((None, None, None, 1, d), lambda t, *_: (layer, row_of_tile(t), 5, 0, 0)),
    ]
    args = [yb, io, x2d, mods5]
    if final:
        in_specs.append(pl.BlockSpec((1, d), lambda t, *_: (0, 0)))
        args.append(final_g.reshape(1, d))
    grid_spec = pltpu.PrefetchScalarGridSpec(
        num_scalar_prefetch=2,
        grid=(n // tm,),
        in_specs=in_specs,
        out_specs=pl.BlockSpec((tm, d), lambda t, *_: (t, 0)),
        scratch_shapes=[pltpu.VMEM((2, STAGE_ROWS, d // 2), U32), pltpu.SemaphoreType.DMA((2,))],
    )
    return pl.pallas_call(
        functools.partial(_combine_kernel, tile0=tile0, final=final),
        grid_spec=grid_spec,
        out_shape=jax.ShapeDtypeStruct((n, d), F32),
        compiler_params=_cparams(("arbitrary",)),
        name="combine",
    )(seg, pad, *args)


def _pack_w_in(w):
    d = w.shape[0]
    pts, acc = [], 0
    for s in IN_SIZES[:-1]:
        acc += s
        pts.append(acc)
    wq, wk, wv, wg, wzf, wzb, wsq, wsk, wsv = jnp.split(w, pts, axis=-1)
    zpad = jnp.zeros((d, LANES - 2 * GLA_GATE_RANK), w.dtype)
    return jnp.concatenate([wq, wk, wv, wg, wzf, wzb, zpad, wsq, wsk, wsv], axis=-1).astype(BF16)


def _pad_up(up, row0):
    out = jnp.zeros((GLA_HEADS // 2, LANES, LANES), F32)
    return out.at[:, row0:row0 + GLA_GATE_RANK, :].set(up.reshape(GLA_GATE_RANK, GLA_HEADS // 2, LANES).transpose(1, 0, 2))


def _rope_tables(seq):
    rows = seq // GRID_W
    row = jnp.repeat(jnp.arange(rows, dtype=F32), GRID_W)
    col = jnp.tile(jnp.arange(GRID_W, dtype=F32), rows)
    n_freq = ROPE_AXIS_DIM // 2
    inv = ROPE_THETA ** (-(jnp.arange(n_freq, dtype=F32) * 2.0 / ROPE_AXIS_DIM))
    ang_r = row[:, None] * inv[None, :]
    ang_c = col[:, None] * inv[None, :]
    cr, sr, cc, sc = jnp.cos(ang_r), jnp.sin(ang_r), jnp.cos(ang_c), jnp.sin(ang_c)
    cos64 = jnp.concatenate([cr, cr, cc, cc], axis=-1)
    sin64 = jnp.concatenate([-sr, sr, -sc, sc], axis=-1)
    reps = LANES // SWA_HD
    return jnp.tile(cos64, (1, reps)), jnp.tile(sin64, (1, reps))


def _tile(n, pref):
    t = pref
    while n % t:
        t //= 2
    return t


def kernel(x, c, ctx, c_ctx, w_ada, b_ada, norm1, norm2, w_in, gla_up_f, gla_bias_f, gla_up_b, gla_bias_b,
           gla_norm, swa_sink, w_out, w_router, b_router, w_gate, w_up, w_down, final_norm):
    batch, seq, d = x.shape
    lc = ctx.shape[1]
    depth = w_ada.shape[0]
    n_l = batch * seq
    n_c = batch * lc
    npair = GLA_HEADS // 2
    assert batch + 1 <= SUBLANES
    assert n_l % ROUTE_TILE == 0 and n_c % ROUTE_TILE == 0

    tm_l = _tile(seq, 512)
    tm_c = _tile(lc, 256)
    tl_l = _tile(seq, 512)
    tl_c = _tile(lc, 512)

    rows = jnp.zeros((SUBLANES, d), F32).at[:batch].set(c).at[batch].set(c_ctx)
    mods = _ada_call(rows, w_ada, b_ada)
    mods5 = mods.reshape(depth, SUBLANES, 6, 1, d)

    tabs = _rope_tables(seq)
    wr_pad = jnp.zeros((d, LANES), F32).at[:, :N_EXPERTS].set(w_router)
    wr_hi = wr_pad.astype(BF16)
    wr_lo = (wr_pad - wr_hi.astype(F32)).astype(BF16)
    wr_cat = jnp.concatenate([wr_hi, wr_lo], axis=-1)

    def lat_row(tm):
        return lambda t: t // (seq // tm)

    def ctx_row(tm):
        return lambda t: batch

    xl = x.reshape(n_l, d)
    xc = ctx.reshape(n_c, d)
    for i in range(depth):
        last = i == depth - 1
        w_packed = _pack_w_in(w_in[i])
        upf = _pad_up(gla_up_f[i], 0)
        upb = _pad_up(gla_up_b[i], GLA_GATE_RANK)
        bias_f = gla_bias_f[i].reshape(npair, 1, LANES)
        bias_b = gla_bias_b[i].reshape(npair, 1, LANES)
        w_out_b = w_out[i].astype(BF16)

        c_qk, c_v, c_g, c_z, c_sq, c_sk, c_sv = _inproj_call(
            xc, norm1[i], mods5, i, ctx_row(tm_c), w_packed, None, tm_c, lc // tm_c)
        l_qk, l_v, l_g, l_z, l_sq, l_sk, l_sv = _inproj_call(
            xl, norm1[i], mods5, i, lat_row(tm_l), w_packed, tabs, tm_l, seq // tm_l)

        s_zero = jnp.zeros((batch, npair, 2, 2 * GLA_DV, LANES), F32)
        oc_f, oc_b, s_ctx = _gla_call(c_qk, c_v, c_z, upf, upb, bias_f, bias_b, s_zero, batch, lc, tl_c)
        ol_f, ol_b, _ = _gla_call(l_qk, l_v, l_z, upf, upb, bias_f, bias_b, s_ctx, batch, seq, tl_l)

        swa_l = _swa_call(l_sq, l_sk, l_sv, c_sk, c_sv, swa_sink[i], batch, seq, lc)
        xl, h2l, lg_l = _outproj_call(ol_f, ol_b, l_g, gla_norm[i], swa_l, w_out_b, xl, mods5, i, lat_row(tm_l),
                                      norm2[i], wr_cat, tm_l)
        if last:
            logits_t = lg_l
            t_all = n_l
        else:
            swa_c = _swa_ctx_call(c_sq, c_sk, c_sv, swa_sink[i], batch, lc)
            xc, h2c, lg_c = _outproj_call(oc_f, oc_b, c_g, gla_norm[i], swa_c, w_out_b, xc, mods5, i, ctx_row(tm_c),
                                          norm2[i], wr_cat, tm_c)
            logits_t = jnp.concatenate([lg_l, lg_c], axis=1)
            t_all = n_l + n_c

        io, wo, seg, pad, blk = _route_call(logits_t, b_router, ROUTE_TILE)
        m = MOE_BLOCK
        padded = blk[:, 0]
        pend = jnp.cumsum(padded)
        ntile = t_all // ROUTE_TILE
        nb = -(-(t_all * TOP_K + ntile * N_EXPERTS * (SUBLANES - 1)) // m) + N_EXPERTS
        p_rows = nb * m
        seg = seg.reshape(-1)
        pad = pad.reshape(-1)
        nused = (pend[-1:] // m).astype(I32)
        tail = nused + jnp.arange(nb - (t_all * TOP_K) // m, dtype=I32)
        zblk = jnp.concatenate([jnp.where(padded > 0, pend // m - 1, -1),
                                jnp.where(tail < nb, tail, -1)]).astype(I32)
        blk_row = jnp.minimum(jnp.arange(nb, dtype=I32), nused - 1) * m
        block_e = jnp.sum((pend[None, :] <= blk_row[:, None]).astype(I32), axis=1)

        sources = (h2l,) if last else (h2l, h2c)
        xb = _dispatch_call(seg, pad, zblk, io, wo, sources, p_rows)
        yb = _expert_call(block_e, nused, xb, w_gate, w_up, w_down, i)
        xl_new = _combine_call(seg, pad, yb, io, xl, mods5, i, lat_row(ROUTE_TILE),
                               final_norm if last else None, 0)
        if not last:
            xc = _combine_call(seg, pad, yb, io, xc, mods5, i, ctx_row(0), None, n_l // ROUTE_TILE)
        xl = xl_new
    return xl.reshape(batch, seq, d)
```

```python
import functools

import jax
import jax.numpy as jnp
from jax import lax
from jax.experimental import pallas as pl
from jax.experimental.pallas import tpu as pltpu

F32 = jnp.float32
BF16 = jnp.bfloat16
I32 = jnp.int32
U32 = jnp.uint32
HIGHEST = lax.Precision.HIGHEST

GRID_W = 64
NORM_EPS = 1e-6
GLA_HEADS = 4
GLA_DK = 64
GLA_DV = 128
GLA_KW = GLA_HEADS * GLA_DK
GLA_WIDTH = GLA_HEADS * GLA_DV
GLA_GATE_RANK = 16
GLA_TAU = 16.0
GLA_CHUNK = 64
SWA_HEADS = 8
SWA_KV_HEADS = 2
SWA_HD = 64
SWA_WIDTH = SWA_HEADS * SWA_HD
SWA_KVW = SWA_KV_HEADS * SWA_HD
WINDOW = 128
ATT_BLOCK = 128
ROPE_THETA = 10000.0
ROPE_AXIS_DIM = SWA_HD // 2
N_EXPERTS = 16
N_GROUPS = 4
EXPERTS_PER_GROUP = N_EXPERTS // N_GROUPS
TOP_K = 2
D_EXPERT = 512
MOE_BLOCK = 512
IN_SIZES = (GLA_KW, GLA_KW, GLA_WIDTH, GLA_WIDTH, GLA_GATE_RANK, GLA_GATE_RANK, SWA_WIDTH, SWA_KVW, SWA_KVW)

LANES = 128
SUBLANES = 8
VMEM_LIMIT = 48 * 1024 * 1024

C_QK = 0
C_V = 512
C_G = 1024
C_Z = 1536
C_SQ = 1664
C_SK = 2176
C_SV = 2304
C_END = 2432
SV_LANES = 4 * LANES
NEG_BIG = -1e30


def _cparams(sem):
    return pltpu.CompilerParams(dimension_semantics=sem, vmem_limit_bytes=VMEM_LIMIT)


def _pack_bf16_pairs(x):
    n = x.shape[1] // 2
    lo = lax.bitcast_convert_type(x[:, :n].astype(BF16).astype(F32), U32)
    hi = lax.bitcast_convert_type(x[:, n:].astype(BF16).astype(F32), U32)
    return (lo >> 16) | hi


def _unpack_bf16_pairs(w):
    lo = lax.bitcast_convert_type(w << 16, F32)
    hi = lax.bitcast_convert_type(w & jnp.uint32(0xFFFF0000), F32)
    return jnp.concatenate([lo.astype(BF16), hi.astype(BF16)], axis=-1)


def _ada_kernel(a_ref, w_ref, b_ref, o_ref):
    a = a_ref[...]
    act = a * jax.nn.sigmoid(a)
    o_ref[...] = jnp.dot(act, w_ref[...], preferred_element_type=F32, precision=HIGHEST) + b_ref[...]


def _ada_call(rows, w_ada, b_ada):
    depth, d, n6 = w_ada.shape
    tn = 1536
    return pl.pallas_call(
        _ada_kernel,
        grid=(depth, n6 // tn),
        in_specs=[
            pl.BlockSpec((SUBLANES, d), lambda l, j: (0, 0)),
            pl.BlockSpec((None, d, tn), lambda l, j: (l, 0, j)),
            pl.BlockSpec((None, 1, tn), lambda l, j: (l, 0, j)),
        ],
        out_specs=pl.BlockSpec((None, SUBLANES, tn), lambda l, j: (l, 0, j)),
        out_shape=jax.ShapeDtypeStruct((depth, SUBLANES, n6), F32),
        compiler_params=_cparams(("arbitrary", "arbitrary")),
        name="adaln",
    )(rows, w_ada, b_ada.reshape(depth, 1, n6))


def _rope_tile(xj, cos, sin, lane_lo):
    partner = jnp.where(lane_lo, pltpu.roll(xj, LANES - 16, 1), pltpu.roll(xj, 16, 1))
    return xj * cos + partner * sin


def _inproj_kernel(*refs, rope):
    if rope:
        (x_ref, g_ref, sc_ref, sh_ref, w_ref, cos_ref, sin_ref,
         qk_ref, v_ref, gate_ref, z_ref, sq_ref, sk_ref, sv_ref) = refs
    else:
        (x_ref, g_ref, sc_ref, sh_ref, w_ref,
         qk_ref, v_ref, gate_ref, z_ref, sq_ref, sk_ref, sv_ref) = refs
    x = x_ref[...]
    ms = jnp.mean(x * x, axis=-1, keepdims=True)
    h = (x * lax.rsqrt(ms + NORM_EPS)) * g_ref[...]
    h = h * (1.0 + sc_ref[...]) + sh_ref[...]
    hb = h.astype(BF16)

    def proj(a, b):
        return jnp.dot(hb, w_ref[:, a:b], preferred_element_type=F32)

    qk = proj(C_QK, C_V)
    qk_ref[:, :GLA_KW] = (qk[:, :GLA_KW] * (GLA_DK ** -0.5)).astype(qk_ref.dtype)
    qk_ref[:, GLA_KW:] = qk[:, GLA_KW:].astype(qk_ref.dtype)
    v_ref[...] = proj(C_V, C_G).astype(v_ref.dtype)
    gate_ref[...] = proj(C_G, C_Z).astype(gate_ref.dtype)
    z_ref[...] = proj(C_Z, C_SQ)
    sq = proj(C_SQ, C_SK) * (SWA_HD ** -0.5)
    sk = proj(C_SK, C_SV)
    sv = proj(C_SV, C_END)
    if rope:
        cos = cos_ref[...]
        sin = sin_ref[...]
        lane_lo = (lax.broadcasted_iota(I32, cos.shape, 1) % 32) < 16
        for j in range(SWA_WIDTH // LANES):
            sl = slice(j * LANES, (j + 1) * LANES)
            sq_ref[:, sl] = _rope_tile(sq[:, sl], cos, sin, lane_lo).astype(sq_ref.dtype)
        sk = _rope_tile(sk, cos, sin, lane_lo)
    else:
        sq_ref[...] = sq.astype(sq_ref.dtype)
    first = lax.broadcasted_iota(I32, sk.shape, 1) < SWA_HD
    sk_sw = pltpu.roll(sk, SWA_HD, 1)
    sk_ref[:, :LANES] = jnp.where(first, sk, sk_sw).astype(sk_ref.dtype)
    sk_ref[:, LANES:] = jnp.where(first, sk_sw, sk).astype(sk_ref.dtype)
    sv_sw = pltpu.roll(sv, SWA_HD, 1)
    sv_ref[:, 0 * LANES:1 * LANES] = jnp.where(first, sv, 1.0).astype(sv_ref.dtype)
    sv_ref[:, 1 * LANES:2 * LANES] = jnp.where(first, 1.0, sv_sw).astype(sv_ref.dtype)
    sv_ref[:, 2 * LANES:3 * LANES] = jnp.where(first, sv_sw, 1.0).astype(sv_ref.dtype)
    sv_ref[:, 3 * LANES:4 * LANES] = jnp.where(first, 1.0, sv).astype(sv_ref.dtype)


def _inproj_call(x2d, g, mods5, layer, row_of_tile, w_packed, tabs, tm, tiles_per_seq):
    n, d = x2d.shape
    nt = n // tm
    rope = tabs is not None

    def mod_spec(k):
        return pl.BlockSpec((None, None, None, 1, d), lambda t: (layer, row_of_tile(t), k, 0, 0))

    in_specs = [
        pl.BlockSpec((tm, d), lambda t: (t, 0)),
        pl.BlockSpec((1, d), lambda t: (0, 0)),
        mod_spec(1), mod_spec(0),
        pl.BlockSpec((d, C_END), lambda t: (0, 0)),
    ]
    args = [x2d, g.reshape(1, d), mods5, mods5, w_packed]
    if rope:
        in_specs += [pl.BlockSpec((tm, LANES), lambda t: (t % tiles_per_seq, 0))] * 2
        args += [tabs[0], tabs[1]]
    widths = (2 * GLA_KW, GLA_WIDTH, GLA_WIDTH, LANES, SWA_WIDTH, 2 * SWA_KVW, SV_LANES)
    dtypes = (BF16, BF16, BF16, F32, BF16, BF16, BF16)
    return pl.pallas_call(
        functools.partial(_inproj_kernel, rope=rope),
        grid=(nt,),
        in_specs=in_specs,
        out_specs=[pl.BlockSpec((tm, w), lambda t: (t, 0)) for w in widths],
        out_shape=[jax.ShapeDtypeStruct((n, w), dt) for w, dt in zip(widths, dtypes)],
        compiler_params=_cparams(("arbitrary",)),
        name="inproj",
    )(*args)


def _log_sigmoid(x):
    return jnp.minimum(x, 0.0) - jnp.log(1.0 + jnp.exp(-jnp.abs(x)))


def _gla_prepare(q_ref, k_ref, v_ref, z_ref, up_ref, bias_ref, tri_ref, scr, fwd):
    qin_s, km0_s, km1_s, kout_s, vm0_s, vm1_s, dec_s = scr[:7]
    tl = q_ref.shape[0]
    c = GLA_CHUNK
    z = z_ref[...]
    z_hi = z.astype(BF16).astype(F32)
    zc = z_hi + pltpu.roll(z - z_hi, 2 * GLA_GATE_RANK, 1) + pltpu.roll(z_hi, 4 * GLA_GATE_RANK, 1)
    x = jnp.dot(zc.astype(BF16), up_ref[...], preferred_element_type=F32) + bias_ref[...]
    la = _log_sigmoid(x) * (1.0 / GLA_TAU)
    hi = la.astype(BF16)
    lo = (la - hi.astype(F32)).astype(BF16)
    sums = jnp.dot(tri_ref[...], jnp.concatenate([hi, lo], axis=1), preferred_element_type=F32)
    b = sums[:, :LANES] + sums[:, LANES:]
    b3 = b.reshape(tl // c, c, LANES)
    edge = b3[:, c - 1:c, :] if fwd else b3[:, 0:1, :]
    btot = jnp.broadcast_to(edge, b3.shape).reshape(tl, LANES)
    q = q_ref[...].astype(F32)
    k = k_ref[...].astype(F32)
    qin_s[...] = (q * jnp.exp(b)).astype(BF16)
    k_in = k * jnp.exp(-b)
    head0 = lax.broadcasted_iota(I32, (tl, LANES), 1) < GLA_DK
    km0_s[...] = jnp.where(head0, k_in, 0.0).astype(BF16)
    km1_s[...] = jnp.where(head0, 0.0, k_in).astype(BF16)
    kout_s[...] = (k * jnp.exp(btot - b)).astype(BF16)
    dec_s[...] = jnp.exp(btot)
    vf = v_ref[...].astype(F32)
    vhead0 = lax.broadcasted_iota(I32, (tl, 2 * GLA_DV), 1) < GLA_DV
    vm0_s[...] = jnp.where(vhead0, vf, 0.0).astype(BF16)
    vm1_s[...] = jnp.where(vhead0, 0.0, vf).astype(BF16)


def _gla_increment(v_ref, scr, r0):
    kout_s, u_s = scr[3], scr[7]
    c = GLA_CHUNK
    rows = pl.ds(r0, c)
    u_t = lax.dot_general(v_ref[rows, :], kout_s[rows, :], (((0,), (0,)), ((), ())),
                          preferred_element_type=F32)
    srow = lax.broadcasted_iota(I32, (2 * GLA_DV, LANES), 0) // GLA_DV
    scol = lax.broadcasted_iota(I32, (2 * GLA_DV, LANES), 1) // GLA_DK
    u_s[r0 // c] = jnp.where(srow == scol, u_t, 0.0)


def _gla_states(scr, s_ref, chunk_order):
    dec_s, u_s, sprev_s = scr[6], scr[7], scr[8]
    s_t = s_ref[...]
    for cidx in chunk_order:
        sprev_s[cidx] = s_t.astype(BF16)
        r0 = cidx * GLA_CHUNK
        s_t = s_t * dec_s[r0:r0 + 1, :] + u_s[cidx]
    s_ref[...] = s_t


def _gla_output(o_ref, scr, r0, fwd):
    qin_s, km0_s, km1_s, _, vm0_s, vm1_s, _, _, sprev_s = scr
    c = GLA_CHUNK
    rows = pl.ds(r0, c)
    q_in = qin_s[rows, :]
    kst = jnp.concatenate([km0_s[rows, :], km1_s[rows, :]], axis=0)
    a = lax.dot_general(q_in, kst, (((1,), (1,)), ((), ())), preferred_element_type=F32)
    r2 = lax.broadcasted_iota(I32, (c, 2 * c), 0)
    c2 = lax.broadcasted_iota(I32, (c, 2 * c), 1) % c
    keep = (c2 <= r2) if fwd else (c2 >= r2)
    a = jnp.where(keep, a, 0.0).astype(BF16)
    vbd = jnp.concatenate([vm0_s[rows, :], vm1_s[rows, :]], axis=0)
    o = jnp.dot(a, vbd, preferred_element_type=F32)
    o = o + lax.dot_general(q_in, sprev_s[r0 // c], (((1,), (1,)), ((), ())), preferred_element_type=F32)
    o_ref[rows, :] = o.astype(o_ref.dtype)


def _gla_kernel(qf_ref, kf_ref, vf_ref, zf_ref, qb_ref, kb_ref, vb_ref, zb_ref,
                upf_ref, upb_ref, bf_ref, bb_ref, trif_ref, trib_ref, s0_ref,
                of_ref, ob_ref, sfin_ref, sf_scr, sb_scr, *scr, nchunk):
    i = pl.program_id(2)
    nt = pl.num_programs(2)
    scr_f, scr_b = scr[:len(scr) // 2], scr[len(scr) // 2:]

    @pl.when(i == 0)
    def _():
        sf_scr[...] = s0_ref[0]
        sb_scr[...] = s0_ref[1]

    _gla_prepare(qf_ref, kf_ref, vf_ref, zf_ref, upf_ref, bf_ref, trif_ref, scr_f, True)
    _gla_prepare(qb_ref, kb_ref, vb_ref, zb_ref, upb_ref, bb_ref, trib_ref, scr_b, False)
    for cidx in range(nchunk):
        _gla_increment(vf_ref, scr_f, cidx * GLA_CHUNK)
        _gla_increment(vb_ref, scr_b, cidx * GLA_CHUNK)
    _gla_states(scr_f, sf_scr, range(nchunk))
    _gla_states(scr_b, sb_scr, range(nchunk - 1, -1, -1))
    for cidx in range(nchunk):
        _gla_output(of_ref, scr_f, cidx * GLA_CHUNK, True)
        _gla_output(ob_ref, scr_b, cidx * GLA_CHUNK, False)

    @pl.when(i == nt - 1)
    def _():
        sfin_ref[0] = sf_scr[...]
        sfin_ref[1] = sb_scr[...]


def _gla_call(qk, v, z, upf_pad, upb_pad, bias_f, bias_b, s0, batch, seq, tl):
    n = qk.shape[0]
    nt = seq // tl
    npair = GLA_HEADS // 2

    def fwd_row(b, p, i):
        return b * nt + i

    def bwd_row(b, p, i):
        return b * nt + (nt - 1 - i)

    def specs(row):
        return [
            pl.BlockSpec((tl, LANES), lambda b, p, i: (row(b, p, i), p)),
            pl.BlockSpec((tl, LANES), lambda b, p, i: (row(b, p, i), npair + p)),
            pl.BlockSpec((tl, 2 * GLA_DV), lambda b, p, i: (row(b, p, i), p)),
            pl.BlockSpec((tl, LANES), lambda b, p, i: (row(b, p, i), 0)),
        ]

    in_specs = specs(fwd_row) + specs(bwd_row) + [
        pl.BlockSpec((None, LANES, LANES), lambda b, p, i: (p, 0, 0)),
        pl.BlockSpec((None, LANES, LANES), lambda b, p, i: (p, 0, 0)),
        pl.BlockSpec((None, 1, LANES), lambda b, p, i: (p, 0, 0)),
        pl.BlockSpec((None, 1, LANES), lambda b, p, i: (p, 0, 0)),
        pl.BlockSpec((None, tl, tl), lambda b, p, i: (0, 0, 0)),
        pl.BlockSpec((None, tl, tl), lambda b, p, i: (1, 0, 0)),
        pl.BlockSpec((None, None, 2, 2 * GLA_DV, LANES), lambda b, p, i: (b, p, 0, 0, 0)),
    ]
    ri = jnp.arange(tl)[:, None]
    ci = jnp.arange(tl)[None, :]
    same = (ri // GLA_CHUNK) == (ci // GLA_CHUNK)
    tri = jnp.stack([same & (ci <= ri), same & (ci >= ri)]).astype(BF16)
    dir_scratch = [
        pltpu.VMEM((tl, LANES), BF16), pltpu.VMEM((tl, LANES), BF16), pltpu.VMEM((tl, LANES), BF16),
        pltpu.VMEM((tl, LANES), BF16), pltpu.VMEM((tl, 2 * GLA_DV), BF16), pltpu.VMEM((tl, 2 * GLA_DV), BF16),
        pltpu.VMEM((tl, LANES), F32),
        pltpu.VMEM((tl // GLA_CHUNK, 2 * GLA_DV, LANES), F32),
        pltpu.VMEM((tl // GLA_CHUNK, 2 * GLA_DV, LANES), BF16),
    ]
    out_specs = [
        pl.BlockSpec((tl, 2 * GLA_DV), lambda b, p, i: (fwd_row(b, p, i), p)),
        pl.BlockSpec((tl, 2 * GLA_DV), lambda b, p, i: (bwd_row(b, p, i), p)),
        pl.BlockSpec((None, None, 2, 2 * GLA_DV, LANES), lambda b, p, i: (b, p, 0, 0, 0)),
    ]
    out_shape = [
        jax.ShapeDtypeStruct((n, GLA_WIDTH), BF16),
        jax.ShapeDtypeStruct((n, GLA_WIDTH), BF16),
        jax.ShapeDtypeStruct((batch, npair, 2, 2 * GLA_DV, LANES), F32),
    ]
    return pl.pallas_call(
        functools.partial(_gla_kernel, nchunk=tl // GLA_CHUNK),
        grid=(batch, npair, nt),
        in_specs=in_specs,
        out_specs=out_specs,
        out_shape=out_shape,
        scratch_shapes=[pltpu.VMEM((2 * GLA_DV, LANES), F32), pltpu.VMEM((2 * GLA_DV, LANES), F32)]
        + dir_scratch + dir_scratch,
        compiler_params=_cparams(("arbitrary", "arbitrary", "arbitrary")),
        name="gla",
    )(qk, qk, v, z, qk, qk, v, z, upf_pad, upb_pad, bias_f, bias_b, tri, tri, s0)


def _swa_block(sink_ref, q_ref, o_ref, row0, kall, vall, prev_ok, next_ok):
    w = ATT_BLOCK
    local = prev_ok is not None
    qrows = pl.ds(row0, w)
    r2 = lax.broadcasted_iota(I32, (2 * w, w), 0) % w
    c2 = lax.broadcasted_iota(I32, (2 * w, w), 1)
    if local:
        bias_prev = jnp.where((c2 >= r2) & prev_ok, 0.0, NEG_BIG)
        bias_next = jnp.where((c2 <= r2) & next_ok, 0.0, NEG_BIG)
    first = c2 < SWA_HD
    top = lax.broadcasted_iota(I32, (2 * w, 1), 0) < w
    scores = []
    for g in range(SWA_KV_HEADS):
        kd = kall[:, g * LANES:(g + 1) * LANES]
        qs = jnp.concatenate([q_ref[qrows, (2 * g) * LANES:(2 * g + 1) * LANES],
                              q_ref[qrows, (2 * g + 1) * LANES:(2 * g + 2) * LANES]], axis=0).astype(F32)
        for half in range(2):
            qm = (jnp.where(first, qs, 0.0) if half == 0 else jnp.where(first, 0.0, qs)).astype(BF16)
            s = lax.dot_general(qm, kd, (((1,), (1,)), ((), ())), preferred_element_type=F32)
            if local:
                s = jnp.concatenate([s[:, :w] + bias_prev, s[:, w:2 * w], s[:, 2 * w:3 * w] + bias_next,
                                     s[:, 3 * w:]], axis=1)
            scores.append(s)
    for g in range(SWA_KV_HEADS):
        outs = []
        for half in range(2):
            s = scores[2 * g + half]
            sk = jnp.where(top, sink_ref[4 * g + half], sink_ref[4 * g + 2 + half])
            m = jnp.maximum(jnp.max(s, axis=-1, keepdims=True), sk)
            p = jnp.exp((s - m).astype(BF16))
            va = vall[:, (2 * g + half) * LANES:(2 * g + half + 1) * LANES]
            acc = jnp.dot(p, va, preferred_element_type=F32)
            den = pltpu.roll(acc, SWA_HD, 1) + jnp.exp(sk - m)
            outs.append(acc / den)
        out = jnp.where(first, outs[0], outs[1])
        o_ref[qrows, (2 * g) * LANES:(2 * g + 1) * LANES] = out[:w].astype(o_ref.dtype)
        o_ref[qrows, (2 * g + 1) * LANES:(2 * g + 2) * LANES] = out[w:].astype(o_ref.dtype)


def _swa_kernel(*refs, npair, local):
    w = ATT_BLOCK
    if not local:
        sink_ref, q_ref, kx_ref, vx_ref, o_ref = refs
        _swa_block(sink_ref, q_ref, o_ref, 0, kx_ref[...], vx_ref[...], None, None)
        return
    sink_ref, q_ref, kp_ref, kc_ref, kn_ref, vp_ref, vc_ref, vn_ref, kx_ref, vx_ref, o_ref = refs
    i = pl.program_id(1)
    k_blocks = [kp_ref[...], kc_ref[:w, :], kc_ref[w:, :], kn_ref[...]]
    v_blocks = [vp_ref[...], vc_ref[:w, :], vc_ref[w:, :], vn_ref[...]]
    exists = [i > 0, True, True, i < npair - 1]
    for sub in range(2):
        kall = jnp.concatenate(k_blocks[sub:sub + 3] + [kx_ref[...]], axis=0)
        vall = jnp.concatenate(v_blocks[sub:sub + 3] + [vx_ref[...]], axis=0)
        _swa_block(sink_ref, q_ref, o_ref, sub * w, kall, vall, exists[sub], exists[sub + 2])


def _swa_call(sq, skd, svd, kcd, vcd, sink, batch, seq, lc):
    n = sq.shape[0]
    w = ATT_BLOCK
    nb = seq // w
    npair = nb // 2
    kvw = 2 * SWA_KVW

    def pair(b, i):
        return (b * npair + i, 0)

    def before(b, i):
        return (b * nb + jnp.maximum(2 * i - 1, 0), 0)

    def after(b, i):
        return (b * nb + jnp.minimum(2 * i + 2, nb - 1), 0)

    def kv_specs(width):
        return [pl.BlockSpec((w, width), before), pl.BlockSpec((2 * w, width), pair), pl.BlockSpec((w, width), after)]

    def ctx_spec(width):
        return pl.BlockSpec((lc, width), lambda b, i: (b, 0))

    return pl.pallas_call(
        functools.partial(_swa_kernel, npair=npair, local=True),
        grid=(batch, npair),
        in_specs=[pl.BlockSpec(memory_space=pltpu.SMEM), pl.BlockSpec((2 * w, SWA_WIDTH), pair)]
        + kv_specs(kvw) + kv_specs(SV_LANES) + [ctx_spec(kvw), ctx_spec(SV_LANES)],
        out_specs=pl.BlockSpec((2 * w, SWA_WIDTH), pair),
        out_shape=jax.ShapeDtypeStruct((n, SWA_WIDTH), BF16),
        compiler_params=_cparams(("arbitrary", "arbitrary")),
        name="swa",
    )(sink, sq, skd, skd, skd, svd, svd, svd, kcd, vcd)


def _swa_ctx_call(sq, kcd, vcd, sink, batch, lc):
    n = sq.shape[0]
    w = ATT_BLOCK
    nb = lc // w
    kvw = 2 * SWA_KVW
    return pl.pallas_call(
        functools.partial(_swa_kernel, npair=0, local=False),
        grid=(batch, nb),
        in_specs=[pl.BlockSpec(memory_space=pltpu.SMEM),
                  pl.BlockSpec((w, SWA_WIDTH), lambda b, i: (b * nb + i, 0)),
                  pl.BlockSpec((lc, kvw), lambda b, i: (b, 0)),
                  pl.BlockSpec((lc, SV_LANES), lambda b, i: (b, 0))],
        out_specs=pl.BlockSpec((w, SWA_WIDTH), lambda b, i: (b * nb + i, 0)),
        out_shape=jax.ShapeDtypeStruct((n, SWA_WIDTH), BF16),
        compiler_params=_cparams(("arbitrary", "arbitrary")),
        name="swa_ctx",
    )(sink, sq, kcd, vcd)


def _outproj_kernel(of_ref, ob_ref, gate_ref, gn_ref, swa_ref, w_ref, x_ref, g1_ref, n2_ref, sc_ref, sh_ref,
                    wr_ref, xo_ref, h2_ref, lg_ref):
    o = of_ref[...].astype(F32) + ob_ref[...].astype(F32)
    parts = []
    for h in range(GLA_HEADS):
        oh = o[:, h * GLA_DV:(h + 1) * GLA_DV]
        ms = jnp.mean(oh * oh, axis=-1, keepdims=True)
        parts.append(oh * lax.rsqrt(ms + NORM_EPS))
    on = jnp.concatenate(parts, axis=-1) * gn_ref[...]
    gate = gate_ref[...].astype(F32)
    gla = on * (gate * jax.nn.sigmoid(gate))
    mix = jnp.concatenate([gla.astype(BF16), swa_ref[...]], axis=-1)
    y = jnp.dot(mix, w_ref[...], preferred_element_type=F32)
    xo = x_ref[...] + g1_ref[...] * y
    xo_ref[...] = xo
    ms = jnp.mean(xo * xo, axis=-1, keepdims=True)
    h2 = (xo * lax.rsqrt(ms + NORM_EPS)) * n2_ref[...]
    h2 = h2 * (1.0 + sc_ref[...]) + sh_ref[...]
    h2_ref[...] = _pack_bf16_pairs(h2)
    hi = h2.astype(BF16)
    lo = (h2 - hi.astype(F32)).astype(BF16)
    both = jnp.dot(hi, wr_ref[...], preferred_element_type=F32)
    lg = both[:, :LANES] + both[:, LANES:] + jnp.dot(lo, wr_ref[:, :LANES], preferred_element_type=F32)
    lg_ref[...] = jnp.transpose(lg)[:N_EXPERTS, :]


def _outproj_call(o_f, o_b, gate, gn, swa, w_out_b, x2d, mods5, layer, row_of_tile, n2, wr_cat, tm):
    n, d = x2d.shape
    nt = n // tm

    def mod_spec(k):
        return pl.BlockSpec((None, None, None, 1, d), lambda t: (layer, row_of_tile(t), k, 0, 0))

    return pl.pallas_call(
        _outproj_kernel,
        grid=(nt,),
        in_specs=[
            pl.BlockSpec((tm, GLA_WIDTH), lambda t: (t, 0)),
            pl.BlockSpec((tm, GLA_WIDTH), lambda t: (t, 0)),
            pl.BlockSpec((tm, GLA_WIDTH), lambda t: (t, 0)),
            pl.BlockSpec((1, GLA_WIDTH), lambda t: (0, 0)),
            pl.BlockSpec((tm, SWA_WIDTH), lambda t: (t, 0)),
            pl.BlockSpec((d, d), lambda t: (0, 0)),
            pl.BlockSpec((tm, d), lambda t: (t, 0)),
            mod_spec(2),
            pl.BlockSpec((1, d), lambda t: (0, 0)),
            mod_spec(4), mod_spec(3),
            pl.BlockSpec((d, 2 * LANES), lambda t: (0, 0)),
        ],
        out_specs=[
            pl.BlockSpec((tm, d), lambda t: (t, 0)),
            pl.BlockSpec((tm, d // 2), lambda t: (t, 0)),
            pl.BlockSpec((N_EXPERTS, tm), lambda t: (0, t)),
        ],
        out_shape=[
            jax.ShapeDtypeStruct((n, d), F32),
            jax.ShapeDtypeStruct((n, d // 2), U32),
            jax.ShapeDtypeStruct((N_EXPERTS, n), F32),
        ],
        compiler_params=_cparams(("arbitrary",)),
        name="outproj",
    )(o_f, o_b, gate, gn.reshape(1, GLA_WIDTH), swa, w_out_b, x2d, mods5, n2.reshape(1, d), mods5, mods5, wr_cat)


def _first_index(vals, target):
    idx = jnp.full(target.shape, len(vals) - 1, I32)
    for i in range(len(vals) - 2, -1, -1):
        idx = jnp.where(vals[i] == target, i, idx)
    return idx


def _route_kernel(lg_ref, br_ref, io_ref, wo_ref, seg_ref, pad_ref, blk_ref, cnt_scr, seg_scr, pad_scr, *, tn):
    phase = pl.program_id(0)
    t = pl.program_id(1)
    tile_lane = lax.broadcasted_iota(I32, (N_EXPERTS, LANES), 1)

    @pl.when((phase == 0) & (t == 0))
    def _():
        cnt_scr[...] = jnp.zeros_like(cnt_scr)
        seg_scr[...] = jnp.zeros_like(seg_scr)
        pad_scr[...] = jnp.zeros_like(pad_scr)

    @pl.when((phase == 1) & (t == 0))
    def _():
        pad8 = jnp.floor((cnt_scr[...] + (SUBLANES - 1)) * (1.0 / SUBLANES)) * SUBLANES
        rows_e = jnp.sum(pad8, axis=1, keepdims=True)
        blocks = jnp.floor((rows_e + (MOE_BLOCK - 1)) * (1.0 / MOE_BLOCK)) * MOE_BLOCK
        r128 = lax.broadcasted_iota(I32, (LANES, LANES), 0)
        c128 = lax.broadcasted_iota(I32, (LANES, LANES), 1)
        before_tile = jnp.where(r128 < c128, 1.0, 0.0).astype(BF16)
        seg = jnp.dot((pad8 * (1.0 / SUBLANES)).astype(BF16), before_tile,
                      preferred_element_type=F32) * SUBLANES
        run = jnp.zeros((1, 1), F32)
        for e in range(N_EXPERTS):
            seg_scr[e:e + 1, :] = seg[e:e + 1, :] + run
            run = run + blocks[e:e + 1, :]
        pad_scr[...] = pad8
        seg_ref[...] = seg_scr[...].astype(I32)
        pad_ref[...] = pad8.astype(I32)
        blk_ref[...] = jnp.broadcast_to(blocks, blk_ref.shape).astype(I32)

    s = jax.nn.sigmoid(lg_ref[...])
    sb = s + br_ref[...]
    rows_s = [s[e:e + 1, :] for e in range(N_EXPERTS)]
    rows_b = [sb[e:e + 1, :] for e in range(N_EXPERTS)]
    gscore, gi1, gi2 = [], [], []
    epg = EXPERTS_PER_GROUP
    for g in range(N_GROUPS):
        a = rows_b[g * epg:(g + 1) * epg]
        m1 = functools.reduce(jnp.maximum, a)
        i1 = _first_index(a, m1)
        rest = [jnp.where(i1 == i, -jnp.inf, a[i]) for i in range(epg)]
        m2 = functools.reduce(jnp.maximum, rest)
        i2 = _first_index(rest, m2)
        gscore.append(m1 + m2)
        gi1.append(i1)
        gi2.append(i2)
    gm = functools.reduce(jnp.maximum, gscore)
    gsel = _first_index(gscore, gm)
    i1 = gi1[N_GROUPS - 1]
    i2 = gi2[N_GROUPS - 1]
    for g in range(N_GROUPS - 2, -1, -1):
        i1 = jnp.where(gsel == g, gi1[g], i1)
        i2 = jnp.where(gsel == g, gi2[g], i2)
    idx0 = gsel * epg + i1
    idx1 = gsel * epg + i2
    s0 = jnp.zeros_like(rows_s[0])
    s1 = jnp.zeros_like(rows_s[0])
    for e in range(N_EXPERTS):
        s0 = jnp.where(idx0 == e, rows_s[e], s0)
        s1 = jnp.where(idx1 == e, rows_s[e], s1)
    tot = s0 + s1
    w0 = s0 / tot
    w1 = s1 / tot

    eidx = lax.broadcasted_iota(I32, (N_EXPERTS, tn), 0)
    oh0 = eidx == idx0
    oh1 = eidx == idx1
    oh = jnp.where(oh0 | oh1, 1.0, 0.0)
    rr = lax.broadcasted_iota(I32, (tn, tn), 0)
    cc = lax.broadcasted_iota(I32, (tn, tn), 1)
    upper = jnp.where(rr < cc, 1.0, 0.0).astype(BF16)
    before = jnp.dot(oh.astype(BF16), upper, preferred_element_type=F32)
    @pl.when(phase == 0)
    def _():
        cnt_scr[...] = cnt_scr[...] + jnp.where(tile_lane == t, jnp.sum(oh, axis=1, keepdims=True), 0.0)

    pad_col = jnp.sum(jnp.where(tile_lane == t, pad_scr[...], 0.0), axis=1, keepdims=True)
    run = jnp.zeros((1, 1), F32)
    offs = []
    for e in range(N_EXPERTS):
        offs.append(run)
        run = run + pad_col[e:e + 1, :]
    pos = before + jnp.concatenate(offs, axis=0)
    spos0 = jnp.sum(jnp.where(oh0, pos, 0.0), axis=0, keepdims=True)
    spos1 = jnp.sum(jnp.where(oh1, pos, 0.0), axis=0, keepdims=True)
    zi = jnp.zeros((SUBLANES - 4, tn), I32)
    io_ref[...] = jnp.concatenate([idx0, idx1, spos0.astype(I32), spos1.astype(I32), zi], axis=0)
    wo_ref[...] = jnp.concatenate([w0, w1, jnp.zeros((SUBLANES - 2, tn), F32)], axis=0)


def _route_call(logits_t, b_router, tn):
    ne, t_all = logits_t.shape
    assert t_all // tn <= LANES
    tok_out = pl.BlockSpec((SUBLANES, tn), lambda p, t: (0, t * p))
    tab_out = pl.BlockSpec((ne, LANES), lambda p, t: (0, 0))
    tab = jax.ShapeDtypeStruct((ne, LANES), I32)
    return pl.pallas_call(
        functools.partial(_route_kernel, tn=tn),
        grid=(2, t_all // tn),
        in_specs=[pl.BlockSpec((ne, tn), lambda p, t: (0, t)), pl.BlockSpec((ne, 1), lambda p, t: (0, 0))],
        out_specs=[tok_out, tok_out, tab_out, tab_out, tab_out],
        out_shape=[
            jax.ShapeDtypeStruct((SUBLANES, t_all), I32),
            jax.ShapeDtypeStruct((SUBLANES, t_all), F32),
            tab, tab, tab,
        ],
        scratch_shapes=[pltpu.VMEM((ne, LANES), F32)] * 3,
        compiler_params=_cparams(("arbitrary", "arbitrary")),
        name="route",
    )(logits_t, b_router.reshape(ne, 1))


SEG_SIZES = (512, 256, 128, 64, 32, 16, 8)
ROUTE_TILE = 512
STAGE_ROWS = TOP_K * ROUTE_TILE + LANES


def _segment_copies(seg_ref, pad_ref, t, stage, hbm, sem, to_hbm, wait):
    loc = 0
    for e in range(N_EXPERTS):
        n = pad_ref[e * LANES + t]
        start = seg_ref[e * LANES + t]
        for size in SEG_SIZES:
            off = n & (-2 * size)

            @pl.when((n & size) != 0)
            def _(off=off, size=size, loc=loc, start=start):
                s_rows = pl.ds(pl.multiple_of(loc + off, SUBLANES), size)
                h_rows = pl.ds(pl.multiple_of(start + off, SUBLANES), size)
                src, dst = (stage.at[s_rows, :], hbm.at[h_rows, :])
                if not to_hbm:
                    src, dst = dst, src
                cp = pltpu.make_async_copy(src, dst, sem)
                if wait:
                    cp.wait()
                else:
                    cp.start()
        loc = loc + n


def _sort_matrix(io_ref, rows):
    r = lax.broadcasted_iota(I32, (rows, io_ref.shape[1]), 0)
    return r == io_ref[2:3, :], r == io_ref[3:4, :]


def _dispatch_kernel(seg_ref, pad_ref, zblk_ref, *refs, tiles):
    srcs = refs[:len(tiles)]
    io_ref, wo_ref, xb_hbm, stage, zbuf, sem = refs[len(tiles):]
    i = pl.program_id(0)
    m = MOE_BLOCK

    def zero_copy(j):
        start = pl.multiple_of(zblk_ref[j] * m, m)
        return pltpu.make_async_copy(zbuf, xb_hbm.at[pl.ds(start, m), :], sem.at[0])

    @pl.when(i == 0)
    def _():
        zbuf[...] = jnp.zeros_like(zbuf)
        for j in range(zblk_ref.shape[0]):
            @pl.when(zblk_ref[j] >= 0)
            def _():
                zero_copy(j).start()
        for j in range(zblk_ref.shape[0]):
            @pl.when(zblk_ref[j] >= 0)
            def _():
                zero_copy(j).wait()

    hp = srcs[0][...]
    if len(srcs) == 2:
        hp = jnp.where(i < tiles[0], hp, srcs[1][...])
    x = _unpack_bf16_pairs(hp)
    m0, m1 = _sort_matrix(io_ref, STAGE_ROWS)
    sort = jnp.where(m0 | m1, 1.0, 0.0).astype(BF16)
    xs = jnp.dot(sort, x, preferred_element_type=F32)
    dh = x.shape[1] // 2
    slot = i % 2
    cur = stage.at[slot]
    cur[:, :dh] = _pack_bf16_pairs(xs)
    ws = jnp.sum(jnp.where(m0, wo_ref[0:1, :], 0.0) + jnp.where(m1, wo_ref[1:2, :], 0.0), axis=1, keepdims=True)
    cur[:, dh:] = jnp.broadcast_to(lax.bitcast_convert_type(ws, U32), (STAGE_ROWS, LANES))
    _segment_copies(seg_ref, pad_ref, i, cur, xb_hbm, sem.at[slot], to_hbm=True, wait=False)

    @pl.when(i > 0)
    def _():
        _segment_copies(seg_ref, pad_ref, i - 1, stage.at[1 - slot], xb_hbm, sem.at[1 - slot], to_hbm=True, wait=True)

    @pl.when(i == pl.num_programs(0) - 1)
    def _():
        _segment_copies(seg_ref, pad_ref, i, cur, xb_hbm, sem.at[slot], to_hbm=True, wait=True)


def _dispatch_call(seg, pad, zblk, io, wo, sources, p_rows):
    tile = ROUTE_TILE
    dh = sources[0].shape[1]
    tiles = tuple(s.shape[0] // tile for s in sources)
    firsts = tuple(sum(tiles[:k]) for k in range(len(tiles)))

    def src_spec(first, ntile):
        return pl.BlockSpec((tile, dh), lambda i, *_: (jnp.clip(i - first, 0, ntile - 1), 0))

    tok_spec = pl.BlockSpec((SUBLANES, tile), lambda i, *_: (0, i))
    grid_spec = pltpu.PrefetchScalarGridSpec(
        num_scalar_prefetch=3,
        grid=(sum(tiles),),
        in_specs=[src_spec(f, n) for f, n in zip(firsts, tiles)] + [tok_spec, tok_spec],
        out_specs=pl.BlockSpec(memory_space=pl.ANY),
        scratch_shapes=[pltpu.VMEM((2, STAGE_ROWS, dh + LANES), U32), pltpu.VMEM((MOE_BLOCK, dh + LANES), U32),
                        pltpu.SemaphoreType.DMA((2,))],
    )
    return pl.pallas_call(
        functools.partial(_dispatch_kernel, tiles=tiles),
        grid_spec=grid_spec,
        out_shape=jax.ShapeDtypeStruct((p_rows, dh + LANES), U32),
        compiler_params=_cparams(("arbitrary",)),
        name="dispatch",
    )(seg, pad, zblk, *sources, io, wo)


def _expert_kernel(be_ref, nu_ref, x_ref, wg_ref, wu_ref, wd_ref, o_ref, wg_b, wu_b, wd_b):
    j = pl.program_id(0)

    @pl.when((j == 0) | (be_ref[j] != be_ref[jnp.maximum(j - 1, 0)]))
    def _():
        wg_b[...] = wg_ref[...].astype(BF16)
        wu_b[...] = wu_ref[...].astype(BF16)
        wd_b[...] = wd_ref[...].astype(BF16)

    @pl.when(j < nu_ref[0])
    def _():
        dh = o_ref.shape[1]
        half = o_ref.shape[0] // 2
        gates = []
        for r in range(2):
            x = _unpack_bf16_pairs(x_ref[r * half:(r + 1) * half, :dh])
            gates.append((jnp.dot(x, wg_b[...], preferred_element_type=F32),
                          jnp.dot(x, wu_b[...], preferred_element_type=F32)))
        for r in range(2):
            a, u = gates[r]
            hmid = (a * jax.nn.sigmoid(a)) * u
            y = jnp.dot(hmid.astype(BF16), wd_b[...], preferred_element_type=F32)
            row_w = lax.bitcast_convert_type(x_ref[r * half:(r + 1) * half, dh:dh + 1], F32)
            o_ref[r * half:(r + 1) * half, :] = _pack_bf16_pairs(y * row_w)

    @pl.when(j >= nu_ref[0])
    def _():
        o_ref[...] = jnp.zeros_like(o_ref)


def _expert_call(block_e, nused, xb, wg, wu, wd, layer):
    p_rows = xb.shape[0]
    dh = xb.shape[1] - LANES
    d = 2 * dh
    m = MOE_BLOCK
    nb = p_rows // m
    de = wg.shape[-1]

    def xmap(j, be, nu):
        return (jnp.minimum(j, nu[0] - 1), 0)

    def wmap(j, be, nu):
        return (layer, be[j], 0, 0)

    grid_spec = pltpu.PrefetchScalarGridSpec(
        num_scalar_prefetch=2,
        grid=(nb,),
        in_specs=[
            pl.BlockSpec((m, dh + LANES), xmap),
            pl.BlockSpec((None, None, d, de), wmap),
            pl.BlockSpec((None, None, d, de), wmap),
            pl.BlockSpec((None, None, de, d), wmap),
        ],
        out_specs=pl.BlockSpec((m, dh), lambda j, be, nu: (j, 0)),
        scratch_shapes=[pltpu.VMEM((d, de), BF16), pltpu.VMEM((d, de), BF16), pltpu.VMEM((de, d), BF16)],
    )
    return pl.pallas_call(
        _expert_kernel,
        grid_spec=grid_spec,
        out_shape=jax.ShapeDtypeStruct((p_rows, dh), U32),
        compiler_params=_cparams(("arbitrary",)),
        name="experts",
    )(block_e, nused, xb, wg, wu, wd)


def _combine_kernel(*refs, tile0, final):
    if final:
        seg_ref, pad_ref, yb_hbm, io_ref, x_ref, g2_ref, fn_ref, o_ref, stage, sem = refs
    else:
        seg_ref, pad_ref, yb_hbm, io_ref, x_ref, g2_ref, o_ref, stage, sem = refs
    i = pl.program_id(0)

    slot = i % 2

    @pl.when(i == 0)
    def _():
        stage[...] = jnp.zeros_like(stage)
        _segment_copies(seg_ref, pad_ref, tile0, stage.at[0], yb_hbm, sem.at[0], to_hbm=False, wait=False)

    @pl.when(i + 1 < pl.num_programs(0))
    def _():
        _segment_copies(seg_ref, pad_ref, tile0 + i + 1, stage.at[1 - slot], yb_hbm, sem.at[1 - slot],
                        to_hbm=False, wait=False)

    _segment_copies(seg_ref, pad_ref, tile0 + i, stage.at[slot], yb_hbm, sem.at[slot], to_hbm=False, wait=True)
    rows = _unpack_bf16_pairs(stage[slot])
    m0, m1 = _sort_matrix(io_ref, STAGE_ROWS)
    pick = jnp.where(m0 | m1, 1.0, 0.0).astype(BF16)
    y = lax.dot_general(pick, rows, (((0,), (0,)), ((), ())), preferred_element_type=F32)
    xo = x_ref[...] + g2_ref[...] * y
    if final:
        ms = jnp.mean(xo * xo, axis=-1, keepdims=True)
        xo = (xo * lax.rsqrt(ms + NORM_EPS)) * fn_ref[...]
    o_ref[...] = xo


def _combine_call(seg, pad, yb, io, x2d, mods5, layer, row_of_tile, final_g, tile0):
    n, d = x2d.shape
    tm = ROUTE_TILE
    final = final_g is not None
    in_specs = [
        pl.BlockSpec(memory_space=pl.ANY),
        pl.BlockSpec((SUBLANES, tm), lambda t, *_: (0, tile0 + t)),
        pl.BlockSpec((tm, d), lambda t, *_: (t, 0)),
        pl.BlockSpec((None, None, None, 1, d), lambda t, *_: (layer, row_of_tile(t), 5, 0, 0)),
    ]
    args = [yb, io, x2d, mods5]
    if final:
        in_specs.append(pl.BlockSpec((1, d), lambda t, *_: (0, 0)))
        args.append(final_g.reshape(1, d))
    grid_spec = pltpu.PrefetchScalarGridSpec(
        num_scalar_prefetch=2,
        grid=(n // tm,),
        in_specs=in_specs,
        out_specs=pl.BlockSpec((tm, d), lambda t, *_: (t, 0)),
        scratch_shapes=[pltpu.VMEM((2, STAGE_ROWS, d // 2), U32), pltpu.SemaphoreType.DMA((2,))],
    )
    return pl.pallas_call(
        functools.partial(_combine_kernel, tile0=tile0, final=final),
        grid_spec=grid_spec,
        out_shape=jax.ShapeDtypeStruct((n, d), F32),
        compiler_params=_cparams(("arbitrary",)),
        name="combine",
    )(seg, pad, *args)


def _pack_w_in(w):
    d = w.shape[0]
    pts, acc = [], 0
    for s in IN_SIZES[:-1]:
        acc += s
        pts.append(acc)
    wq, wk, wv, wg, wzf, wzb, wsq, wsk, wsv = jnp.split(w, pts, axis=-1)
    zpad = jnp.zeros((d, LANES - 2 * GLA_GATE_RANK), w.dtype)
    return jnp.concatenate([wq, wk, wv, wg, wzf, wzb, zpad, wsq, wsk, wsv], axis=-1).astype(BF16)


def _pad_up(up, row0):
    up = up.reshape(GLA_GATE_RANK, GLA_HEADS // 2, LANES).transpose(1, 0, 2)
    hi = up.astype(BF16)
    lo = (up - hi.astype(F32)).astype(BF16)
    out = jnp.zeros((GLA_HEADS // 2, LANES, LANES), BF16)
    for group, part in enumerate((hi, hi, lo)):
        r = group * 2 * GLA_GATE_RANK + row0
        out = out.at[:, r:r + GLA_GATE_RANK, :].set(part)
    return out


def _rope_tables(seq):
    rows = seq // GRID_W
    row = jnp.repeat(jnp.arange(rows, dtype=F32), GRID_W)
    col = jnp.tile(jnp.arange(GRID_W, dtype=F32), rows)
    n_freq = ROPE_AXIS_DIM // 2
    inv = ROPE_THETA ** (-(jnp.arange(n_freq, dtype=F32) * 2.0 / ROPE_AXIS_DIM))
    ang_r = row[:, None] * inv[None, :]
    ang_c = col[:, None] * inv[None, :]
    cr, sr, cc, sc = jnp.cos(ang_r), jnp.sin(ang_r), jnp.cos(ang_c), jnp.sin(ang_c)
    cos64 = jnp.concatenate([cr, cr, cc, cc], axis=-1)
    sin64 = jnp.concatenate([-sr, sr, -sc, sc], axis=-1)
    reps = LANES // SWA_HD
    return jnp.tile(cos64, (1, reps)), jnp.tile(sin64, (1, reps))


def _tile(n, pref):
    t = pref
    while n % t:
        t //= 2
    return t


def kernel(x, c, ctx, c_ctx, w_ada, b_ada, norm1, norm2, w_in, gla_up_f, gla_bias_f, gla_up_b, gla_bias_b,
           gla_norm, swa_sink, w_out, w_router, b_router, w_gate, w_up, w_down, final_norm):
    batch, seq, d = x.shape
    lc = ctx.shape[1]
    depth = w_ada.shape[0]
    n_l = batch * seq
    n_c = batch * lc
    npair = GLA_HEADS // 2
    assert batch + 1 <= SUBLANES
    assert n_l % ROUTE_TILE == 0 and n_c % ROUTE_TILE == 0

    tm_l = _tile(seq, 512)
    tm_c = _tile(lc, 256)
    tl_l = _tile(seq, 512)
    tl_c = _tile(lc, 512)

    rows = jnp.zeros((SUBLANES, d), F32).at[:batch].set(c).at[batch].set(c_ctx)
    mods = _ada_call(rows, w_ada, b_ada)
    mods5 = mods.reshape(depth, SUBLANES, 6, 1, d)

    tabs = _rope_tables(seq)
    wr_pad = jnp.zeros((d, LANES), F32).at[:, :N_EXPERTS].set(w_router)
    wr_hi = wr_pad.astype(BF16)
    wr_lo = (wr_pad - wr_hi.astype(F32)).astype(BF16)
    wr_cat = jnp.concatenate([wr_hi, wr_lo], axis=-1)

    def lat_row(tm):
        return lambda t: t // (seq // tm)

    def ctx_row(tm):
        return lambda t: batch

    xl = x.reshape(n_l, d)
    xc = ctx.reshape(n_c, d)
    for i in range(depth):
        last = i == depth - 1
        w_packed = _pack_w_in(w_in[i])
        upf = _pad_up(gla_up_f[i], 0)
        upb = _pad_up(gla_up_b[i], GLA_GATE_RANK)
        bias_f = gla_bias_f[i].reshape(npair, 1, LANES)
        bias_b = gla_bias_b[i].reshape(npair, 1, LANES)
        w_out_b = w_out[i].astype(BF16)

        c_qk, c_v, c_g, c_z, c_sq, c_sk, c_sv = _inproj_call(
            xc, norm1[i], mods5, i, ctx_row(tm_c), w_packed, None, tm_c, lc // tm_c)
        l_qk, l_v, l_g, l_z, l_sq, l_sk, l_sv = _inproj_call(
            xl, norm1[i], mods5, i, lat_row(tm_l), w_packed, tabs, tm_l, seq // tm_l)

        s_zero = jnp.zeros((batch, npair, 2, 2 * GLA_DV, LANES), F32)
        oc_f, oc_b, s_ctx = _gla_call(c_qk, c_v, c_z, upf, upb, bias_f, bias_b, s_zero, batch, lc, tl_c)
        ol_f, ol_b, _ = _gla_call(l_qk, l_v, l_z, upf, upb, bias_f, bias_b, s_ctx, batch, seq, tl_l)

        swa_l = _swa_call(l_sq, l_sk, l_sv, c_sk, c_sv, swa_sink[i], batch, seq, lc)
        xl, h2l, lg_l = _outproj_call(ol_f, ol_b, l_g, gla_norm[i], swa_l, w_out_b, xl, mods5, i, lat_row(tm_l),
                                      norm2[i], wr_cat, tm_l)
        if last:
            logits_t = lg_l
            t_all = n_l
        else:
            swa_c = _swa_ctx_call(c_sq, c_sk, c_sv, swa_sink[i], batch, lc)
            xc, h2c, lg_c = _outproj_call(oc_f, oc_b, c_g, gla_norm[i], swa_c, w_out_b, xc, mods5, i, ctx_row(tm_c),
                                          norm2[i], wr_cat, tm_c)
            logits_t = jnp.concatenate([lg_l, lg_c], axis=1)
            t_all = n_l + n_c

        io, wo, seg, pad, blk = _route_call(logits_t, b_router, ROUTE_TILE)
        m = MOE_BLOCK
        padded = blk[:, 0]
        pend = jnp.cumsum(padded)
        ntile = t_all // ROUTE_TILE
        nb = -(-(t_all * TOP_K + ntile * N_EXPERTS * (SUBLANES - 1)) // m) + N_EXPERTS
        p_rows = nb * m
        seg = seg.reshape(-1)
        pad = pad.reshape(-1)
        nused = (pend[-1:] // m).astype(I32)
        tail = nused + jnp.arange(nb - (t_all * TOP_K) // m, dtype=I32)
        zblk = jnp.concatenate([jnp.where(padded > 0, pend // m - 1, -1),
                                jnp.where(tail < nb, tail, -1)]).astype(I32)
        blk_row = jnp.minimum(jnp.arange(nb, dtype=I32), nused - 1) * m
        block_e = jnp.sum((pend[None, :] <= blk_row[:, None]).astype(I32), axis=1)

        sources = (h2l,) if last else (h2l, h2c)
        xb = _dispatch_call(seg, pad, zblk, io, wo, sources, p_rows)
        yb = _expert_call(block_e, nused, xb, w_gate, w_up, w_down, i)
        xl_new = _combine_call(seg, pad, yb, io, xl, mods5, i, lat_row(ROUTE_TILE),
                               final_norm if last else None, 0)
        if not last:
            xc = _combine_call(seg, pad, yb, io, xc, mods5, i, ctx_row(0), None, n_l // ROUTE_TILE)
        xl = xl_new
    return xl.reshape(batch, seq, d)
```

```python
import functools

import jax
import jax.numpy as jnp
from jax import lax
from jax.experimental import pallas as pl
from jax.experimental.pallas import tpu as pltpu

F32 = jnp.float32
BF16 = jnp.bfloat16
I32 = jnp.int32
U32 = jnp.uint32
HIGHEST = lax.Precision.HIGHEST

GRID_W = 64
NORM_EPS = 1e-6
GLA_HEADS = 4
GLA_DK = 64
GLA_DV = 128
GLA_KW = GLA_HEADS * GLA_DK
GLA_WIDTH = GLA_HEADS * GLA_DV
GLA_GATE_RANK = 16
GLA_TAU = 16.0
GLA_CHUNK = 64
SWA_HEADS = 8
SWA_KV_HEADS = 2
SWA_HD = 64
SWA_WIDTH = SWA_HEADS * SWA_HD
SWA_KVW = SWA_KV_HEADS * SWA_HD
WINDOW = 128
ATT_BLOCK = 128
ROPE_THETA = 10000.0
ROPE_AXIS_DIM = SWA_HD // 2
N_EXPERTS = 16
N_GROUPS = 4
EXPERTS_PER_GROUP = N_EXPERTS // N_GROUPS
TOP_K = 2
D_EXPERT = 512
MOE_BLOCK = 512
IN_SIZES = (GLA_KW, GLA_KW, GLA_WIDTH, GLA_WIDTH, GLA_GATE_RANK, GLA_GATE_RANK, SWA_WIDTH, SWA_KVW, SWA_KVW)

LANES = 128
SUBLANES = 8
VMEM_LIMIT = 48 * 1024 * 1024

C_QK = 0
C_V = 512
C_G = 1024
C_Z = 1536
C_SQ = 1664
C_SK = 2176
C_SV = 2304
C_END = 2432
SV_LANES = 4 * LANES
NEG_BIG = -1e30


def _cparams(sem):
    return pltpu.CompilerParams(dimension_semantics=sem, vmem_limit_bytes=VMEM_LIMIT)


def _pack_bf16_pairs(x):
    n = x.shape[1] // 2
    lo = lax.bitcast_convert_type(x[:, :n].astype(BF16).astype(F32), U32)
    hi = lax.bitcast_convert_type(x[:, n:].astype(BF16).astype(F32), U32)
    return (lo >> 16) | hi


def _unpack_bf16_pairs(w):
    lo = lax.bitcast_convert_type(w << 16, F32)
    hi = lax.bitcast_convert_type(w & jnp.uint32(0xFFFF0000), F32)
    return jnp.concatenate([lo.astype(BF16), hi.astype(BF16)], axis=-1)


def _ada_kernel(a_ref, w_ref, b_ref, o_ref):
    a = a_ref[...]
    act = a * jax.nn.sigmoid(a)
    o_ref[...] = jnp.dot(act, w_ref[...], preferred_element_type=F32, precision=HIGHEST) + b_ref[...]


def _ada_call(rows, w_ada, b_ada):
    depth, d, n6 = w_ada.shape
    tn = 1536
    return pl.pallas_call(
        _ada_kernel,
        grid=(depth, n6 // tn),
        in_specs=[
            pl.BlockSpec((SUBLANES, d), lambda l, j: (0, 0)),
            pl.BlockSpec((None, d, tn), lambda l, j: (l, 0, j)),
            pl.BlockSpec((None, 1, tn), lambda l, j: (l, 0, j)),
        ],
        out_specs=pl.BlockSpec((None, SUBLANES, tn), lambda l, j: (l, 0, j)),
        out_shape=jax.ShapeDtypeStruct((depth, SUBLANES, n6), F32),
        compiler_params=_cparams(("arbitrary", "arbitrary")),
        name="adaln",
    )(rows, w_ada, b_ada.reshape(depth, 1, n6))


def _rope_tile(xj, cos, sin, lane_lo):
    partner = jnp.where(lane_lo, pltpu.roll(xj, LANES - 16, 1), pltpu.roll(xj, 16, 1))
    return xj * cos + partner * sin


def _inproj_kernel(*refs, rope):
    if rope:
        (x_ref, g_ref, sc_ref, sh_ref, w_ref, cos_ref, sin_ref,
         qk_ref, v_ref, gate_ref, z_ref, sq_ref, sk_ref, sv_ref) = refs
    else:
        (x_ref, g_ref, sc_ref, sh_ref, w_ref,
         qk_ref, v_ref, gate_ref, z_ref, sq_ref, sk_ref, sv_ref) = refs
    x = x_ref[...]
    ms = jnp.mean(x * x, axis=-1, keepdims=True)
    h = (x * lax.rsqrt(ms + NORM_EPS)) * g_ref[...]
    h = h * (1.0 + sc_ref[...]) + sh_ref[...]
    hb = h.astype(BF16)

    def proj(a, b):
        return jnp.dot(hb, w_ref[:, a:b], preferred_element_type=F32)

    qk = proj(C_QK, C_V)
    qk_ref[:, :GLA_KW] = (qk[:, :GLA_KW] * (GLA_DK ** -0.5)).astype(qk_ref.dtype)
    qk_ref[:, GLA_KW:] = qk[:, GLA_KW:].astype(qk_ref.dtype)
    v_ref[...] = proj(C_V, C_G).astype(v_ref.dtype)
    gate_ref[...] = proj(C_G, C_Z).astype(gate_ref.dtype)
    z_ref[...] = proj(C_Z, C_SQ)
    sq = proj(C_SQ, C_SK) * (SWA_HD ** -0.5)
    sk = proj(C_SK, C_SV)
    sv = proj(C_SV, C_END)
    if rope:
        cos = cos_ref[...]
        sin = sin_ref[...]
        lane_lo = (lax.broadcasted_iota(I32, cos.shape, 1) % 32) < 16
        for j in range(SWA_WIDTH // LANES):
            sl = slice(j * LANES, (j + 1) * LANES)
            sq_ref[:, sl] = _rope_tile(sq[:, sl], cos, sin, lane_lo).astype(sq_ref.dtype)
        sk = _rope_tile(sk, cos, sin, lane_lo)
    else:
        sq_ref[...] = sq.astype(sq_ref.dtype)
    first = lax.broadcasted_iota(I32, sk.shape, 1) < SWA_HD
    sk_sw = pltpu.roll(sk, SWA_HD, 1)
    sk_ref[:, :LANES] = jnp.where(first, sk, sk_sw).astype(sk_ref.dtype)
    sk_ref[:, LANES:] = jnp.where(first, sk_sw, sk).astype(sk_ref.dtype)
    sv_sw = pltpu.roll(sv, SWA_HD, 1)
    sv_ref[:, 0 * LANES:1 * LANES] = jnp.where(first, sv, 1.0).astype(sv_ref.dtype)
    sv_ref[:, 1 * LANES:2 * LANES] = jnp.where(first, 1.0, sv_sw).astype(sv_ref.dtype)
    sv_ref[:, 2 * LANES:3 * LANES] = jnp.where(first, sv_sw, 1.0).astype(sv_ref.dtype)
    sv_ref[:, 3 * LANES:4 * LANES] = jnp.where(first, 1.0, sv).astype(sv_ref.dtype)


def _inproj_call(x2d, g, mods5, layer, row_of_tile, w_packed, tabs, tm, tiles_per_seq):
    n, d = x2d.shape
    nt = n // tm
    rope = tabs is not None

    def mod_spec(k):
        return pl.BlockSpec((None, None, None, 1, d), lambda t: (layer, row_of_tile(t), k, 0, 0))

    in_specs = [
        pl.BlockSpec((tm, d), lambda t: (t, 0)),
        pl.BlockSpec((1, d), lambda t: (0, 0)),
        mod_spec(1), mod_spec(0),
        pl.BlockSpec((d, C_END), lambda t: (0, 0)),
    ]
    args = [x2d, g.reshape(1, d), mods5, mods5, w_packed]
    if rope:
        in_specs += [pl.BlockSpec((tm, LANES), lambda t: (t % tiles_per_seq, 0))] * 2
        args += [tabs[0], tabs[1]]
    widths = (2 * GLA_KW, GLA_WIDTH, GLA_WIDTH, LANES, SWA_WIDTH, 2 * SWA_KVW, SV_LANES)
    dtypes = (BF16, BF16, BF16, F32, BF16, BF16, BF16)
    return pl.pallas_call(
        functools.partial(_inproj_kernel, rope=rope),
        grid=(nt,),
        in_specs=in_specs,
        out_specs=[pl.BlockSpec((tm, w), lambda t: (t, 0)) for w in widths],
        out_shape=[jax.ShapeDtypeStruct((n, w), dt) for w, dt in zip(widths, dtypes)],
        compiler_params=_cparams(("arbitrary",)),
        name="inproj",
    )(*args)


def _log_sigmoid(x):
    return jnp.minimum(x, 0.0) - jnp.log(1.0 + jnp.exp(-jnp.abs(x)))


def _gla_prepare(q_ref, k_ref, v_ref, z_ref, up_ref, bias_ref, tri_ref, scr, fwd):
    qin_s, km0_s, km1_s, kout_s, vm0_s, vm1_s, dec_s = scr[:7]
    qbd_s = scr[9]
    tl = q_ref.shape[0]
    c = GLA_CHUNK
    z = z_ref[...]
    z_hi = z.astype(BF16).astype(F32)
    zc = z_hi + pltpu.roll(z - z_hi, 2 * GLA_GATE_RANK, 1) + pltpu.roll(z_hi, 4 * GLA_GATE_RANK, 1)
    x = jnp.dot(zc.astype(BF16), up_ref[...], preferred_element_type=F32) + bias_ref[...]
    la = _log_sigmoid(x) * (1.0 / GLA_TAU)
    hi = la.astype(BF16)
    lo = (la - hi.astype(F32)).astype(BF16)
    sums = jnp.dot(tri_ref[...], jnp.concatenate([hi, lo], axis=1), preferred_element_type=F32)
    b = sums[:, :LANES] + sums[:, LANES:]
    b3 = b.reshape(tl // c, c, LANES)
    edge = b3[:, c - 1:c, :] if fwd else b3[:, 0:1, :]
    btot = jnp.broadcast_to(edge, b3.shape).reshape(tl, LANES)
    q = q_ref[...].astype(F32)
    k = k_ref[...].astype(F32)
    first = (lax.broadcasted_iota(I32, (tl, LANES), 0) % (2 * c)) < c
    q_in = q * jnp.exp(b)
    qin_s[...] = q_in.astype(BF16)
    qbd_s[...] = jnp.concatenate([jnp.where(first, q_in, 0.0), jnp.where(first, 0.0, q_in)], axis=1).astype(BF16)
    k_in = k * jnp.exp(-b)
    head0 = lax.broadcasted_iota(I32, (tl, LANES), 1) < GLA_DK
    km0_s[...] = jnp.where(head0, k_in, 0.0).astype(BF16)
    km1_s[...] = jnp.where(head0, 0.0, k_in).astype(BF16)
    k_out = k * jnp.exp(btot - b)
    kout_s[...] = jnp.concatenate([jnp.where(first, k_out, 0.0), jnp.where(first, 0.0, k_out)], axis=1).astype(BF16)
    dec_s[...] = jnp.exp(btot)
    vf = v_ref[...].astype(F32)
    vhead0 = lax.broadcasted_iota(I32, (tl, 2 * GLA_DV), 1) < GLA_DV
    vm0_s[...] = jnp.where(vhead0, vf, 0.0).astype(BF16)
    vm1_s[...] = jnp.where(vhead0, 0.0, vf).astype(BF16)


def _gla_increment(v_ref, scr, pair):
    kout_s, u_s = scr[3], scr[7]
    rows = pl.ds(pair * 2 * GLA_CHUNK, 2 * GLA_CHUNK)
    u_t = lax.dot_general(v_ref[rows, :], kout_s[rows, :], (((0,), (0,)), ((), ())),
                          preferred_element_type=F32)
    srow = lax.broadcasted_iota(I32, (2 * GLA_DV, 2 * LANES), 0) // GLA_DV
    scol = (lax.broadcasted_iota(I32, (2 * GLA_DV, 2 * LANES), 1) % LANES) // GLA_DK
    u_s[pair] = jnp.where(srow == scol, u_t, 0.0)


def _gla_states(scr, s_ref, chunk_order):
    dec_s, u_s, sprev_s = scr[6], scr[7], scr[8]
    s_t = s_ref[...]
    for cidx in chunk_order:
        pair, lanes = cidx // 2, pl.ds((cidx % 2) * LANES, LANES)
        sprev_s[pair, :, lanes] = s_t.astype(BF16)
        r0 = cidx * GLA_CHUNK
        s_t = s_t * dec_s[r0:r0 + 1, :] + u_s[pair, :, lanes]
    s_ref[...] = s_t


def _gla_output(o_ref, scr, pair, fwd):
    qin_s, km0_s, km1_s, _, vm0_s, vm1_s, _, _, sprev_s, qbd_s = scr
    c = GLA_CHUNK
    r0 = pair * 2 * c
    rows = pl.ds(r0, 2 * c)
    ca, cb = pl.ds(r0, c), pl.ds(r0 + c, c)
    kst = jnp.concatenate([km0_s[ca, :], km1_s[ca, :], km0_s[cb, :], km1_s[cb, :]], axis=0)
    a = lax.dot_general(qin_s[rows, :], kst, (((1,), (1,)), ((), ())), preferred_element_type=F32)
    ri = lax.broadcasted_iota(I32, (2 * c, 4 * c), 0)
    ci = lax.broadcasted_iota(I32, (2 * c, 4 * c), 1)
    same_chunk = (ri // c) == (ci // (2 * c))
    keep = same_chunk & ((ci % c <= ri % c) if fwd else (ci % c >= ri % c))
    a = jnp.where(keep, a, 0.0).astype(BF16)
    vbd = jnp.concatenate([vm0_s[ca, :], vm1_s[ca, :], vm0_s[cb, :], vm1_s[cb, :]], axis=0)
    o = jnp.dot(a, vbd, preferred_element_type=F32)
    o = o + lax.dot_general(qbd_s[rows, :], sprev_s[pair], (((1,), (1,)), ((), ())), preferred_element_type=F32)
    o_ref[rows, :] = o.astype(o_ref.dtype)


def _gla_kernel(qf_ref, kf_ref, vf_ref, zf_ref, qb_ref, kb_ref, vb_ref, zb_ref,
                upf_ref, upb_ref, bf_ref, bb_ref, trif_ref, trib_ref, s0_ref,
                of_ref, ob_ref, sfin_ref, sf_scr, sb_scr, *scr, nchunk):
    i = pl.program_id(2)
    nt = pl.num_programs(2)
    scr_f, scr_b = scr[:len(scr) // 2], scr[len(scr) // 2:]

    @pl.when(i == 0)
    def _():
        sf_scr[...] = s0_ref[0]
        sb_scr[...] = s0_ref[1]

    _gla_prepare(qf_ref, kf_ref, vf_ref, zf_ref, upf_ref, bf_ref, trif_ref, scr_f, True)
    _gla_prepare(qb_ref, kb_ref, vb_ref, zb_ref, upb_ref, bb_ref, trib_ref, scr_b, False)
    for pair in range(nchunk // 2):
        _gla_increment(vf_ref, scr_f, pair)
        _gla_increment(vb_ref, scr_b, pair)
    _gla_states(scr_f, sf_scr, range(nchunk))
    _gla_states(scr_b, sb_scr, range(nchunk - 1, -1, -1))
    for pair in range(nchunk // 2):
        _gla_output(of_ref, scr_f, pair, True)
        _gla_output(ob_ref, scr_b, pair, False)

    @pl.when(i == nt - 1)
    def _():
        sfin_ref[0] = sf_scr[...]
        sfin_ref[1] = sb_scr[...]


def _gla_call(qk, v, z, upf_pad, upb_pad, bias_f, bias_b, s0, batch, seq, tl):
    n = qk.shape[0]
    nt = seq // tl
    npair = GLA_HEADS // 2

    def fwd_row(b, p, i):
        return b * nt + i

    def bwd_row(b, p, i):
        return b * nt + (nt - 1 - i)

    def specs(row):
        return [
            pl.BlockSpec((tl, LANES), lambda b, p, i: (row(b, p, i), p)),
            pl.BlockSpec((tl, LANES), lambda b, p, i: (row(b, p, i), npair + p)),
            pl.BlockSpec((tl, 2 * GLA_DV), lambda b, p, i: (row(b, p, i), p)),
            pl.BlockSpec((tl, LANES), lambda b, p, i: (row(b, p, i), 0)),
        ]

    in_specs = specs(fwd_row) + specs(bwd_row) + [
        pl.BlockSpec((None, LANES, LANES), lambda b, p, i: (p, 0, 0)),
        pl.BlockSpec((None, LANES, LANES), lambda b, p, i: (p, 0, 0)),
        pl.BlockSpec((None, 1, LANES), lambda b, p, i: (p, 0, 0)),
        pl.BlockSpec((None, 1, LANES), lambda b, p, i: (p, 0, 0)),
        pl.BlockSpec((None, tl, tl), lambda b, p, i: (0, 0, 0)),
        pl.BlockSpec((None, tl, tl), lambda b, p, i: (1, 0, 0)),
        pl.BlockSpec((None, None, 2, 2 * GLA_DV, LANES), lambda b, p, i: (b, p, 0, 0, 0)),
    ]
    ri = jnp.arange(tl)[:, None]
    ci = jnp.arange(tl)[None, :]
    same = (ri // GLA_CHUNK) == (ci // GLA_CHUNK)
    tri = jnp.stack([same & (ci <= ri), same & (ci >= ri)]).astype(BF16)
    npairs = tl // (2 * GLA_CHUNK)
    dir_scratch = [
        pltpu.VMEM((tl, LANES), BF16),
        pltpu.VMEM((tl, LANES), BF16),
        pltpu.VMEM((tl, LANES), BF16),
        pltpu.VMEM((tl, 2 * LANES), BF16),
        pltpu.VMEM((tl, 2 * GLA_DV), BF16),
        pltpu.VMEM((tl, 2 * GLA_DV), BF16),
        pltpu.VMEM((tl, LANES), F32),
        pltpu.VMEM((npairs, 2 * GLA_DV, 2 * LANES), F32),
        pltpu.VMEM((npairs, 2 * GLA_DV, 2 * LANES), BF16),
        pltpu.VMEM((tl, 2 * LANES), BF16),
    ]
    out_specs = [
        pl.BlockSpec((tl, 2 * GLA_DV), lambda b, p, i: (fwd_row(b, p, i), p)),
        pl.BlockSpec((tl, 2 * GLA_DV), lambda b, p, i: (bwd_row(b, p, i), p)),
        pl.BlockSpec((None, None, 2, 2 * GLA_DV, LANES), lambda b, p, i: (b, p, 0, 0, 0)),
    ]
    out_shape = [
        jax.ShapeDtypeStruct((n, GLA_WIDTH), BF16),
        jax.ShapeDtypeStruct((n, GLA_WIDTH), BF16),
        jax.ShapeDtypeStruct((batch, npair, 2, 2 * GLA_DV, LANES), F32),
    ]
    return pl.pallas_call(
        functools.partial(_gla_kernel, nchunk=tl // GLA_CHUNK),
        grid=(batch, npair, nt),
        in_specs=in_specs,
        out_specs=out_specs,
        out_shape=out_shape,
        scratch_shapes=[pltpu.VMEM((2 * GLA_DV, LANES), F32), pltpu.VMEM((2 * GLA_DV, LANES), F32)]
        + dir_scratch + dir_scratch,
        compiler_params=_cparams(("arbitrary", "arbitrary", "arbitrary")),
        name="gla",
    )(qk, qk, v, z, qk, qk, v, z, upf_pad, upb_pad, bias_f, bias_b, tri, tri, s0)


def _swa_block(sink_ref, q_ref, o_ref, row0, kall, vall, prev_ok, next_ok):
    w = ATT_BLOCK
    local = prev_ok is not None
    qrows = pl.ds(row0, w)
    r2 = lax.broadcasted_iota(I32, (2 * w, w), 0) % w
    c2 = lax.broadcasted_iota(I32, (2 * w, w), 1)
    if local:
        bias_prev = jnp.where((c2 >= r2) & prev_ok, 0.0, NEG_BIG)
        bias_next = jnp.where((c2 <= r2) & next_ok, 0.0, NEG_BIG)
    first = c2 < SWA_HD
    top = lax.broadcasted_iota(I32, (2 * w, 1), 0) < w
    scores = []
    for g in range(SWA_KV_HEADS):
        kd = kall[:, g * LANES:(g + 1) * LANES]
        qs = jnp.concatenate([q_ref[qrows, (2 * g) * LANES:(2 * g + 1) * LANES],
                              q_ref[qrows, (2 * g + 1) * LANES:(2 * g + 2) * LANES]], axis=0).astype(F32)
        for half in range(2):
            qm = (jnp.where(first, qs, 0.0) if half == 0 else jnp.where(first, 0.0, qs)).astype(BF16)
            s = lax.dot_general(qm, kd, (((1,), (1,)), ((), ())), preferred_element_type=F32)
            if local:
                s = jnp.concatenate([s[:, :w] + bias_prev, s[:, w:2 * w], s[:, 2 * w:3 * w] + bias_next,
                                     s[:, 3 * w:]], axis=1)
            scores.append(s)
    for g in range(SWA_KV_HEADS):
        outs = []
        for half in range(2):
            s = scores[2 * g + half]
            sk = jnp.where(top, sink_ref[4 * g + half], sink_ref[4 * g + 2 + half])
            m = jnp.maximum(jnp.max(s, axis=-1, keepdims=True), sk)
            p = jnp.exp((s - m).astype(BF16))
            va = vall[:, (2 * g + half) * LANES:(2 * g + half + 1) * LANES]
            acc = jnp.dot(p, va, preferred_element_type=F32)
            den = pltpu.roll(acc, SWA_HD, 1) + jnp.exp(sk - m)
            outs.append(acc / den)
        out = jnp.where(first, outs[0], outs[1])
        o_ref[qrows, (2 * g) * LANES:(2 * g + 1) * LANES] = out[:w].astype(o_ref.dtype)
        o_ref[qrows, (2 * g + 1) * LANES:(2 * g + 2) * LANES] = out[w:].astype(o_ref.dtype)


def _swa_kernel(*refs, npair, local):
    w = ATT_BLOCK
    if not local:
        sink_ref, q_ref, kx_ref, vx_ref, o_ref = refs
        _swa_block(sink_ref, q_ref, o_ref, 0, kx_ref[...], vx_ref[...], None, None)
        return
    sink_ref, q_ref, kp_ref, kc_ref, kn_ref, vp_ref, vc_ref, vn_ref, kx_ref, vx_ref, o_ref = refs
    i = pl.program_id(1)
    k_blocks = [kp_ref[...], kc_ref[:w, :], kc_ref[w:, :], kn_ref[...]]
    v_blocks = [vp_ref[...], vc_ref[:w, :], vc_ref[w:, :], vn_ref[...]]
    exists = [i > 0, True, True, i < npair - 1]
    for sub in range(2):
        kall = jnp.concatenate(k_blocks[sub:sub + 3] + [kx_ref[...]], axis=0)
        vall = jnp.concatenate(v_blocks[sub:sub + 3] + [vx_ref[...]], axis=0)
        _swa_block(sink_ref, q_ref, o_ref, sub * w, kall, vall, exists[sub], exists[sub + 2])


def _swa_call(sq, skd, svd, kcd, vcd, sink, batch, seq, lc):
    n = sq.shape[0]
    w = ATT_BLOCK
    nb = seq // w
    npair = nb // 2
    kvw = 2 * SWA_KVW

    def pair(b, i):
        return (b * npair + i, 0)

    def before(b, i):
        return (b * nb + jnp.maximum(2 * i - 1, 0), 0)

    def after(b, i):
        return (b * nb + jnp.minimum(2 * i + 2, nb - 1), 0)

    def kv_specs(width):
        return [pl.BlockSpec((w, width), before), pl.BlockSpec((2 * w, width), pair), pl.BlockSpec((w, width), after)]

    def ctx_spec(width):
        return pl.BlockSpec((lc, width), lambda b, i: (b, 0))

    return pl.pallas_call(
        functools.partial(_swa_kernel, npair=npair, local=True),
        grid=(batch, npair),
        in_specs=[pl.BlockSpec(memory_space=pltpu.SMEM), pl.BlockSpec((2 * w, SWA_WIDTH), pair)]
        + kv_specs(kvw) + kv_specs(SV_LANES) + [ctx_spec(kvw), ctx_spec(SV_LANES)],
        out_specs=pl.BlockSpec((2 * w, SWA_WIDTH), pair),
        out_shape=jax.ShapeDtypeStruct((n, SWA_WIDTH), BF16),
        compiler_params=_cparams(("arbitrary", "arbitrary")),
        name="swa",
    )(sink, sq, skd, skd, skd, svd, svd, svd, kcd, vcd)


def _swa_ctx_call(sq, kcd, vcd, sink, batch, lc):
    n = sq.shape[0]
    w = ATT_BLOCK
    nb = lc // w
    kvw = 2 * SWA_KVW
    return pl.pallas_call(
        functools.partial(_swa_kernel, npair=0, local=False),
        grid=(batch, nb),
        in_specs=[pl.BlockSpec(memory_space=pltpu.SMEM),
                  pl.BlockSpec((w, SWA_WIDTH), lambda b, i: (b * nb + i, 0)),
                  pl.BlockSpec((lc, kvw), lambda b, i: (b, 0)),
                  pl.BlockSpec((lc, SV_LANES), lambda b, i: (b, 0))],
        out_specs=pl.BlockSpec((w, SWA_WIDTH), lambda b, i: (b * nb + i, 0)),
        out_shape=jax.ShapeDtypeStruct((n, SWA_WIDTH), BF16),
        compiler_params=_cparams(("arbitrary", "arbitrary")),
        name="swa_ctx",
    )(sink, sq, kcd, vcd)


def _outproj_kernel(of_ref, ob_ref, gate_ref, gn_ref, swa_ref, w_ref, x_ref, g1_ref, n2_ref, sc_ref, sh_ref,
                    wr_ref, xo_ref, h2_ref, lg_ref):
    o = of_ref[...].astype(F32) + ob_ref[...].astype(F32)
    parts = []
    for h in range(GLA_HEADS):
        oh = o[:, h * GLA_DV:(h + 1) * GLA_DV]
        ms = jnp.mean(oh * oh, axis=-1, keepdims=True)
        parts.append(oh * lax.rsqrt(ms + NORM_EPS))
    on = jnp.concatenate(parts, axis=-1) * gn_ref[...]
    gate = gate_ref[...].astype(F32)
    gla = on * (gate * jax.nn.sigmoid(gate))
    mix = jnp.concatenate([gla.astype(BF16), swa_ref[...]], axis=-1)
    y = jnp.dot(mix, w_ref[...], preferred_element_type=F32)
    xo = x_ref[...] + g1_ref[...] * y
    xo_ref[...] = xo
    ms = jnp.mean(xo * xo, axis=-1, keepdims=True)
    h2 = (xo * lax.rsqrt(ms + NORM_EPS)) * n2_ref[...]
    h2 = h2 * (1.0 + sc_ref[...]) + sh_ref[...]
    h2_ref[...] = _pack_bf16_pairs(h2)
    hi = h2.astype(BF16)
    lo = (h2 - hi.astype(F32)).astype(BF16)
    both = jnp.dot(hi, wr_ref[...], preferred_element_type=F32)
    lg = both[:, :LANES] + both[:, LANES:] + jnp.dot(lo, wr_ref[:, :LANES], preferred_element_type=F32)
    lg_ref[...] = jnp.transpose(lg)[:N_EXPERTS, :]


def _outproj_call(o_f, o_b, gate, gn, swa, w_out_b, x2d, mods5, layer, row_of_tile, n2, wr_cat, tm):
    n, d = x2d.shape
    nt = n // tm

    def mod_spec(k):
        return pl.BlockSpec((None, None, None, 1, d), lambda t: (layer, row_of_tile(t), k, 0, 0))

    return pl.pallas_call(
        _outproj_kernel,
        grid=(nt,),
        in_specs=[
            pl.BlockSpec((tm, GLA_WIDTH), lambda t: (t, 0)),
            pl.BlockSpec((tm, GLA_WIDTH), lambda t: (t, 0)),
            pl.BlockSpec((tm, GLA_WIDTH), lambda t: (t, 0)),
            pl.BlockSpec((1, GLA_WIDTH), lambda t: (0, 0)),
            pl.BlockSpec((tm, SWA_WIDTH), lambda t: (t, 0)),
            pl.BlockSpec((d, d), lambda t: (0, 0)),
            pl.BlockSpec((tm, d), lambda t: (t, 0)),
            mod_spec(2),
            pl.BlockSpec((1, d), lambda t: (0, 0)),
            mod_spec(4), mod_spec(3),
            pl.BlockSpec((d, 2 * LANES), lambda t: (0, 0)),
        ],
        out_specs=[
            pl.BlockSpec((tm, d), lambda t: (t, 0)),
            pl.BlockSpec((tm, d // 2), lambda t: (t, 0)),
            pl.BlockSpec((N_EXPERTS, tm), lambda t: (0, t)),
        ],
        out_shape=[
            jax.ShapeDtypeStruct((n, d), F32),
            jax.ShapeDtypeStruct((n, d // 2), U32),
            jax.ShapeDtypeStruct((N_EXPERTS, n), F32),
        ],
        compiler_params=_cparams(("arbitrary",)),
        name="outproj",
    )(o_f, o_b, gate, gn.reshape(1, GLA_WIDTH), swa, w_out_b, x2d, mods5, n2.reshape(1, d), mods5, mods5, wr_cat)


def _first_index(vals, target):
    idx = jnp.full(target.shape, len(vals) - 1, I32)
    for i in range(len(vals) - 2, -1, -1):
        idx = jnp.where(vals[i] == target, i, idx)
    return idx


def _route_kernel(lg_ref, br_ref, io_ref, wo_ref, seg_ref, pad_ref, blk_ref, cnt_scr, seg_scr, pad_scr, *, tn):
    phase = pl.program_id(0)
    t = pl.program_id(1)
    tile_lane = lax.broadcasted_iota(I32, (N_EXPERTS, LANES), 1)

    @pl.when((phase == 0) & (t == 0))
    def _():
        cnt_scr[...] = jnp.zeros_like(cnt_scr)
        seg_scr[...] = jnp.zeros_like(seg_scr)
        pad_scr[...] = jnp.zeros_like(pad_scr)

    @pl.when((phase == 1) & (t == 0))
    def _():
        pad8 = jnp.floor((cnt_scr[...] + (SUBLANES - 1)) * (1.0 / SUBLANES)) * SUBLANES
        rows_e = jnp.sum(pad8, axis=1, keepdims=True)
        blocks = jnp.floor((rows_e + (MOE_BLOCK - 1)) * (1.0 / MOE_BLOCK)) * MOE_BLOCK
        r128 = lax.broadcasted_iota(I32, (LANES, LANES), 0)
        c128 = lax.broadcasted_iota(I32, (LANES, LANES), 1)
        before_tile = jnp.where(r128 < c128, 1.0, 0.0).astype(BF16)
        seg = jnp.dot((pad8 * (1.0 / SUBLANES)).astype(BF16), before_tile,
                      preferred_element_type=F32) * SUBLANES
        run = jnp.zeros((1, 1), F32)
        for e in range(N_EXPERTS):
            seg_scr[e:e + 1, :] = seg[e:e + 1, :] + run
            run = run + blocks[e:e + 1, :]
        pad_scr[...] = pad8
        seg_ref[...] = seg_scr[...].astype(I32)
        pad_ref[...] = pad8.astype(I32)
        blk_ref[...] = jnp.broadcast_to(blocks, blk_ref.shape).astype(I32)

    s = jax.nn.sigmoid(lg_ref[...])
    sb = s + br_ref[...]
    rows_s = [s[e:e + 1, :] for e in range(N_EXPERTS)]
    rows_b = [sb[e:e + 1, :] for e in range(N_EXPERTS)]
    gscore, gi1, gi2 = [], [], []
    epg = EXPERTS_PER_GROUP
    for g in range(N_GROUPS):
        a = rows_b[g * epg:(g + 1) * epg]
        m1 = functools.reduce(jnp.maximum, a)
        i1 = _first_index(a, m1)
        rest = [jnp.where(i1 == i, -jnp.inf, a[i]) for i in range(epg)]
        m2 = functools.reduce(jnp.maximum, rest)
        i2 = _first_index(rest, m2)
        gscore.append(m1 + m2)
        gi1.append(i1)
        gi2.append(i2)
    gm = functools.reduce(jnp.maximum, gscore)
    gsel = _first_index(gscore, gm)
    i1 = gi1[N_GROUPS - 1]
    i2 = gi2[N_GROUPS - 1]
    for g in range(N_GROUPS - 2, -1, -1):
        i1 = jnp.where(gsel == g, gi1[g], i1)
        i2 = jnp.where(gsel == g, gi2[g], i2)
    idx0 = gsel * epg + i1
    idx1 = gsel * epg + i2
    s0 = jnp.zeros_like(rows_s[0])
    s1 = jnp.zeros_like(rows_s[0])
    for e in range(N_EXPERTS):
        s0 = jnp.where(idx0 == e, rows_s[e], s0)
        s1 = jnp.where(idx1 == e, rows_s[e], s1)
    tot = s0 + s1
    w0 = s0 / tot
    w1 = s1 / tot

    eidx = lax.broadcasted_iota(I32, (N_EXPERTS, tn), 0)
    oh0 = eidx == idx0
    oh1 = eidx == idx1
    oh = jnp.where(oh0 | oh1, 1.0, 0.0)
    rr = lax.broadcasted_iota(I32, (tn, tn), 0)
    cc = lax.broadcasted_iota(I32, (tn, tn), 1)
    upper = jnp.where(rr < cc, 1.0, 0.0).astype(BF16)
    before = jnp.dot(oh.astype(BF16), upper, preferred_element_type=F32)
    @pl.when(phase == 0)
    def _():
        cnt_scr[...] = cnt_scr[...] + jnp.where(tile_lane == t, jnp.sum(oh, axis=1, keepdims=True), 0.0)

    pad_col = jnp.sum(jnp.where(tile_lane == t, pad_scr[...], 0.0), axis=1, keepdims=True)
    run = jnp.zeros((1, 1), F32)
    offs = []
    for e in range(N_EXPERTS):
        offs.append(run)
        run = run + pad_col[e:e + 1, :]
    pos = before + jnp.concatenate(offs, axis=0)
    spos0 = jnp.sum(jnp.where(oh0, pos, 0.0), axis=0, keepdims=True)
    spos1 = jnp.sum(jnp.where(oh1, pos, 0.0), axis=0, keepdims=True)
    zi = jnp.zeros((SUBLANES - 4, tn), I32)
    io_ref[...] = jnp.concatenate([idx0, idx1, spos0.astype(I32), spos1.astype(I32), zi], axis=0)
    wo_ref[...] = jnp.concatenate([w0, w1, jnp.zeros((SUBLANES - 2, tn), F32)], axis=0)


def _route_call(logits_t, b_router, tn):
    ne, t_all = logits_t.shape
    assert t_all // tn <= LANES
    tok_out = pl.BlockSpec((SUBLANES, tn), lambda p, t: (0, t * p))
    tab_out = pl.BlockSpec((ne, LANES), lambda p, t: (0, 0))
    tab = jax.ShapeDtypeStruct((ne, LANES), I32)
    return pl.pallas_call(
        functools.partial(_route_kernel, tn=tn),
        grid=(2, t_all // tn),
        in_specs=[pl.BlockSpec((ne, tn), lambda p, t: (0, t)), pl.BlockSpec((ne, 1), lambda p, t: (0, 0))],
        out_specs=[tok_out, tok_out, tab_out, tab_out, tab_out],
        out_shape=[
            jax.ShapeDtypeStruct((SUBLANES, t_all), I32),
            jax.ShapeDtypeStruct((SUBLANES, t_all), F32),
            tab, tab, tab,
        ],
        scratch_shapes=[pltpu.VMEM((ne, LANES), F32)] * 3,
        compiler_params=_cparams(("arbitrary", "arbitrary")),
        name="route",
    )(logits_t, b_router.reshape(ne, 1))


SEG_SIZES = (512, 256, 128, 64, 32, 16, 8)
ROUTE_TILE = 512
STAGE_ROWS = TOP_K * ROUTE_TILE + LANES


def _segment_copies(seg_ref, pad_ref, t, stage, hbm, sem, to_hbm, wait):
    if wait:
        total = functools.reduce(lambda a, b: a + b, [pad_ref[e * LANES + t] for e in range(N_EXPERTS)])
        for size in (2 * SEG_SIZES[0],) + SEG_SIZES:
            @pl.when((total & size) != 0)
            def _(size=size):
                src, dst = stage.at[pl.ds(0, size), :], hbm.at[pl.ds(0, size), :]
                if not to_hbm:
                    src, dst = dst, src
                pltpu.make_async_copy(src, dst, sem).wait()
        return
    loc = 0
    for e in range(N_EXPERTS):
        n = pad_ref[e * LANES + t]
        start = seg_ref[e * LANES + t]
        for size in SEG_SIZES:
            off = n & (-2 * size)

            @pl.when((n & size) != 0)
            def _(off=off, size=size, loc=loc, start=start):
                s_rows = pl.ds(pl.multiple_of(loc + off, SUBLANES), size)
                h_rows = pl.ds(pl.multiple_of(start + off, SUBLANES), size)
                src, dst = (stage.at[s_rows, :], hbm.at[h_rows, :])
                if not to_hbm:
                    src, dst = dst, src
                pltpu.make_async_copy(src, dst, sem).start()
        loc = loc + n


def _sort_matrix(io_ref, rows):
    r = lax.broadcasted_iota(I32, (rows, io_ref.shape[1]), 0)
    return r == io_ref[2:3, :], r == io_ref[3:4, :]


def _dispatch_kernel(seg_ref, pad_ref, zblk_ref, *refs, tiles):
    srcs = refs[:len(tiles)]
    io_ref, wo_ref, xb_hbm, stage, zbuf, sem = refs[len(tiles):]
    i = pl.program_id(0)
    m = MOE_BLOCK

    def zero_copy(j):
        start = pl.multiple_of(zblk_ref[j] * m, m)
        return pltpu.make_async_copy(zbuf, xb_hbm.at[pl.ds(start, m), :], sem.at[0])

    @pl.when(i == 0)
    def _():
        zbuf[...] = jnp.zeros_like(zbuf)
        for j in range(zblk_ref.shape[0]):
            @pl.when(zblk_ref[j] >= 0)
            def _():
                zero_copy(j).start()
        for j in range(zblk_ref.shape[0]):
            @pl.when(zblk_ref[j] >= 0)
            def _():
                zero_copy(j).wait()

    hp = srcs[0][...]
    if len(srcs) == 2:
        hp = jnp.where(i < tiles[0], hp, srcs[1][...])
    x = _unpack_bf16_pairs(hp)
    m0, m1 = _sort_matrix(io_ref, STAGE_ROWS)
    sort = jnp.where(m0 | m1, 1.0, 0.0).astype(BF16)
    xs = jnp.dot(sort, x, preferred_element_type=F32)
    dh = x.shape[1] // 2
    slot = i % 2
    cur = stage.at[slot]
    cur[:, :dh] = _pack_bf16_pairs(xs)
    ws = jnp.sum(jnp.where(m0, wo_ref[0:1, :], 0.0) + jnp.where(m1, wo_ref[1:2, :], 0.0), axis=1, keepdims=True)
    cur[:, dh:] = jnp.broadcast_to(lax.bitcast_convert_type(ws, U32), (STAGE_ROWS, LANES))
    _segment_copies(seg_ref, pad_ref, i, cur, xb_hbm, sem.at[slot], to_hbm=True, wait=False)

    @pl.when(i > 0)
    def _():
        _segment_copies(seg_ref, pad_ref, i - 1, stage.at[1 - slot], xb_hbm, sem.at[1 - slot], to_hbm=True, wait=True)

    @pl.when(i == pl.num_programs(0) - 1)
    def _():
        _segment_copies(seg_ref, pad_ref, i, cur, xb_hbm, sem.at[slot], to_hbm=True, wait=True)


def _dispatch_call(seg, pad, zblk, io, wo, sources, p_rows):
    tile = ROUTE_TILE
    dh = sources[0].shape[1]
    tiles = tuple(s.shape[0] // tile for s in sources)
    firsts = tuple(sum(tiles[:k]) for k in range(len(tiles)))

    def src_spec(first, ntile):
        return pl.BlockSpec((tile, dh), lambda i, *_: (jnp.clip(i - first, 0, ntile - 1), 0))

    tok_spec = pl.BlockSpec((SUBLANES, tile), lambda i, *_: (0, i))
    grid_spec = pltpu.PrefetchScalarGridSpec(
        num_scalar_prefetch=3,
        grid=(sum(tiles),),
        in_specs=[src_spec(f, n) for f, n in zip(firsts, tiles)] + [tok_spec, tok_spec],
        out_specs=pl.BlockSpec(memory_space=pl.ANY),
        scratch_shapes=[pltpu.VMEM((2, STAGE_ROWS, dh + LANES), U32), pltpu.VMEM((MOE_BLOCK, dh + LANES), U32),
                        pltpu.SemaphoreType.DMA((2,))],
    )
    return pl.pallas_call(
        functools.partial(_dispatch_kernel, tiles=tiles),
        grid_spec=grid_spec,
        out_shape=jax.ShapeDtypeStruct((p_rows, dh + LANES), U32),
        compiler_params=_cparams(("arbitrary",)),
        name="dispatch",
    )(seg, pad, zblk, *sources, io, wo)


def _expert_kernel(be_ref, nu_ref, x_ref, wg_ref, wu_ref, wd_ref, o_ref, wg_b, wu_b, wd_b):
    j = pl.program_id(0)

    @pl.when((j == 0) | (be_ref[j] != be_ref[jnp.maximum(j - 1, 0)]))
    def _():
        wg_b[...] = wg_ref[...].astype(BF16)
        wu_b[...] = wu_ref[...].astype(BF16)
        wd_b[...] = wd_ref[...].astype(BF16)

    @pl.when(j < nu_ref[0])
    def _():
        dh = o_ref.shape[1]
        half = o_ref.shape[0] // 2
        gates = []
        for r in range(2):
            x = _unpack_bf16_pairs(x_ref[r * half:(r + 1) * half, :dh])
            gates.append((jnp.dot(x, wg_b[...], preferred_element_type=F32),
                          jnp.dot(x, wu_b[...], preferred_element_type=F32)))
        for r in range(2):
            a, u = gates[r]
            hmid = (a * jax.nn.sigmoid(a)) * u
            y = jnp.dot(hmid.astype(BF16), wd_b[...], preferred_element_type=F32)
            row_w = lax.bitcast_convert_type(x_ref[r * half:(r + 1) * half, dh:dh + 1], F32)
            o_ref[r * half:(r + 1) * half, :] = _pack_bf16_pairs(y * row_w)

    @pl.when(j >= nu_ref[0])
    def _():
        o_ref[...] = jnp.zeros_like(o_ref)


def _expert_call(block_e, nused, xb, wg, wu, wd, layer):
    p_rows = xb.shape[0]
    dh = xb.shape[1] - LANES
    d = 2 * dh
    m = MOE_BLOCK
    nb = p_rows // m
    de = wg.shape[-1]

    def xmap(j, be, nu):
        return (jnp.minimum(j, nu[0] - 1), 0)

    def wmap(j, be, nu):
        return (layer, be[j], 0, 0)

    grid_spec = pltpu.PrefetchScalarGridSpec(
        num_scalar_prefetch=2,
        grid=(nb,),
        in_specs=[
            pl.BlockSpec((m, dh + LANES), xmap),
            pl.BlockSpec((None, None, d, de), wmap),
            pl.BlockSpec((None, None, d, de), wmap),
            pl.BlockSpec((None, None, de, d), wmap),
        ],
        out_specs=pl.BlockSpec((m, dh), lambda j, be, nu: (j, 0)),
        scratch_shapes=[pltpu.VMEM((d, de), BF16), pltpu.VMEM((d, de), BF16), pltpu.VMEM((de, d), BF16)],
    )
    return pl.pallas_call(
        _expert_kernel,
        grid_spec=grid_spec,
        out_shape=jax.ShapeDtypeStruct((p_rows, dh), U32),
        compiler_params=_cparams(("arbitrary",)),
        name="experts",
    )(block_e, nused, xb, wg, wu, wd)


def _combine_kernel(*refs, tile0, final):
    if final:
        seg_ref, pad_ref, yb_hbm, io_ref, x_ref, g2_ref, fn_ref, o_ref, stage, sem = refs
    else:
        seg_ref, pad_ref, yb_hbm, io_ref, x_ref, g2_ref, o_ref, stage, sem = refs
    i = pl.program_id(0)

    slot = i % 2

    @pl.when(i == 0)
    def _():
        stage[...] = jnp.zeros_like(stage)
        _segment_copies(seg_ref, pad_ref, tile0, stage.at[0], yb_hbm, sem.at[0], to_hbm=False, wait=False)

    @pl.when(i + 1 < pl.num_programs(0))
    def _():
        _segment_copies(seg_ref, pad_ref, tile0 + i + 1, stage.at[1 - slot], yb_hbm, sem.at[1 - slot],
                        to_hbm=False, wait=False)

    _segment_copies(seg_ref, pad_ref, tile0 + i, stage.at[slot], yb_hbm, sem.at[slot], to_hbm=False, wait=True)
    rows = _unpack_bf16_pairs(stage[slot])
    m0, m1 = _sort_matrix(io_ref, STAGE_ROWS)
    pick = jnp.where(m0 | m1, 1.0, 0.0).astype(BF16)
    y = lax.dot_general(pick, rows, (((0,), (0,)), ((), ())), preferred_element_type=F32)
    xo = x_ref[...] + g2_ref[...] * y
    if final:
        ms = jnp.mean(xo * xo, axis=-1, keepdims=True)
        xo = (xo * lax.rsqrt(ms + NORM_EPS)) * fn_ref[...]
    o_ref[...] = xo


def _combine_call(seg, pad, yb, io, x2d, mods5, layer, row_of_tile, final_g, tile0):
    n, d = x2d.shape
    tm = ROUTE_TILE
    final = final_g is not None
    in_specs = [
        pl.BlockSpec(memory_space=pl.ANY),
        pl.BlockSpec((SUBLANES, tm), lambda t, *_: (0, tile0 + t)),
        pl.BlockSpec((tm, d), lambda t, *_: (t, 0)),
        pl.BlockSpec((None, None, None, 1, d), lambda t, *_: (layer, row_of_tile(t), 5, 0, 0)),
    ]
    args = [yb, io, x2d, mods5]
    if final:
        in_specs.append(pl.BlockSpec((1, d), lambda t, *_: (0, 0)))
        args.append(final_g.reshape(1, d))
    grid_spec = pltpu.PrefetchScalarGridSpec(
        num_scalar_prefetch=2,
        grid=(n // tm,),
        in_specs=in_specs,
        out_specs=pl.BlockSpec((tm, d), lambda t, *_: (t, 0)),
        scratch_shapes=[pltpu.VMEM((2, STAGE_ROWS, d // 2), U32), pltpu.SemaphoreType.DMA((2,))],
    )
    return pl.pallas_call(
        functools.partial(_combine_kernel, tile0=tile0, final=final),
        grid_spec=grid_spec,
        out_shape=jax.ShapeDtypeStruct((n, d), F32),
        compiler_params=_cparams(("arbitrary",)),
        name="combine",
    )(seg, pad, *args)


def _pack_w_in(w):
    d = w.shape[0]
    pts, acc = [], 0
    for s in IN_SIZES[:-1]:
        acc += s
        pts.append(acc)
    wq, wk, wv, wg, wzf, wzb, wsq, wsk, wsv = jnp.split(w, pts, axis=-1)
    zpad = jnp.zeros((d, LANES - 2 * GLA_GATE_RANK), w.dtype)
    return jnp.concatenate([wq, wk, wv, wg, wzf, wzb, zpad, wsq, wsk, wsv], axis=-1).astype(BF16)


def _pad_up(up, row0):
    up = up.reshape(GLA_GATE_RANK, GLA_HEADS // 2, LANES).transpose(1, 0, 2)
    hi = up.astype(BF16)
    lo = (up - hi.astype(F32)).astype(BF16)
    out = jnp.zeros((GLA_HEADS // 2, LANES, LANES), BF16)
    for group, part in enumerate((hi, hi, lo)):
        r = group * 2 * GLA_GATE_RANK + row0
        out = out.at[:, r:r + GLA_GATE_RANK, :].set(part)
    return out


def _rope_tables(seq):
    rows = seq // GRID_W
    row = jnp.repeat(jnp.arange(rows, dtype=F32), GRID_W)
    col = jnp.tile(jnp.arange(GRID_W, dtype=F32), rows)
    n_freq = ROPE_AXIS_DIM // 2
    inv = ROPE_THETA ** (-(jnp.arange(n_freq, dtype=F32) * 2.0 / ROPE_AXIS_DIM))
    ang_r = row[:, None] * inv[None, :]
    ang_c = col[:, None] * inv[None, :]
    cr, sr, cc, sc = jnp.cos(ang_r), jnp.sin(ang_r), jnp.cos(ang_c), jnp.sin(ang_c)
    cos64 = jnp.concatenate([cr, cr, cc, cc], axis=-1)
    sin64 = jnp.concatenate([-sr, sr, -sc, sc], axis=-1)
    reps = LANES // SWA_HD
    return jnp.tile(cos64, (1, reps)), jnp.tile(sin64, (1, reps))


def _tile(n, pref):
    t = pref
    while n % t:
        t //= 2
    return t


def kernel(x, c, ctx, c_ctx, w_ada, b_ada, norm1, norm2, w_in, gla_up_f, gla_bias_f, gla_up_b, gla_bias_b,
           gla_norm, swa_sink, w_out, w_router, b_router, w_gate, w_up, w_down, final_norm):
    batch, seq, d = x.shape
    lc = ctx.shape[1]
    depth = w_ada.shape[0]
    n_l = batch * seq
    n_c = batch * lc
    npair = GLA_HEADS // 2
    assert batch + 1 <= SUBLANES
    assert n_l % ROUTE_TILE == 0 and n_c % ROUTE_TILE == 0

    tm_l = _tile(seq, 512)
    tm_in = _tile(seq, 1024)
    tm_c = _tile(lc, 256)
    tl_l = _tile(seq, 512)
    tl_c = _tile(lc, 512)

    rows = jnp.zeros((SUBLANES, d), F32).at[:batch].set(c).at[batch].set(c_ctx)
    mods = _ada_call(rows, w_ada, b_ada)
    mods5 = mods.reshape(depth, SUBLANES, 6, 1, d)

    tabs = _rope_tables(seq)
    wr_pad = jnp.zeros((d, LANES), F32).at[:, :N_EXPERTS].set(w_router)
    wr_hi = wr_pad.astype(BF16)
    wr_lo = (wr_pad - wr_hi.astype(F32)).astype(BF16)
    wr_cat = jnp.concatenate([wr_hi, wr_lo], axis=-1)

    def lat_row(tm):
        return lambda t: t // (seq // tm)

    def ctx_row(tm):
        return lambda t: batch

    xl = x.reshape(n_l, d)
    xc = ctx.reshape(n_c, d)
    for i in range(depth):
        last = i == depth - 1
        w_packed = _pack_w_in(w_in[i])
        upf = _pad_up(gla_up_f[i], 0)
        upb = _pad_up(gla_up_b[i], GLA_GATE_RANK)
        bias_f = gla_bias_f[i].reshape(npair, 1, LANES)
        bias_b = gla_bias_b[i].reshape(npair, 1, LANES)
        w_out_b = w_out[i].astype(BF16)

        c_qk, c_v, c_g, c_z, c_sq, c_sk, c_sv = _inproj_call(
            xc, norm1[i], mods5, i, ctx_row(tm_c), w_packed, None, tm_c, lc // tm_c)
        l_qk, l_v, l_g, l_z, l_sq, l_sk, l_sv = _inproj_call(
            xl, norm1[i], mods5, i, lat_row(tm_in), w_packed, tabs, tm_in, seq // tm_in)

        s_zero = jnp.zeros((batch, npair, 2, 2 * GLA_DV, LANES), F32)
        oc_f, oc_b, s_ctx = _gla_call(c_qk, c_v, c_z, upf, upb, bias_f, bias_b, s_zero, batch, lc, tl_c)
        ol_f, ol_b, _ = _gla_call(l_qk, l_v, l_z, upf, upb, bias_f, bias_b, s_ctx, batch, seq, tl_l)

        swa_l = _swa_call(l_sq, l_sk, l_sv, c_sk, c_sv, swa_sink[i], batch, seq, lc)
        xl, h2l, lg_l = _outproj_call(ol_f, ol_b, l_g, gla_norm[i], swa_l, w_out_b, xl, mods5, i, lat_row(tm_l),
                                      norm2[i], wr_cat, tm_l)
        if last:
            logits_t = lg_l
            t_all = n_l
        else:
            swa_c = _swa_ctx_call(c_sq, c_sk, c_sv, swa_sink[i], batch, lc)
            xc, h2c, lg_c = _outproj_call(oc_f, oc_b, c_g, gla_norm[i], swa_c, w_out_b, xc, mods5, i, ctx_row(tm_c),
                                          norm2[i], wr_cat, tm_c)
            logits_t = jnp.concatenate([lg_l, lg_c], axis=1)
            t_all = n_l + n_c

        io, wo, seg, pad, blk = _route_call(logits_t, b_router, ROUTE_TILE)
        m = MOE_BLOCK
        padded = blk[:, 0]
        pend = jnp.cumsum(padded)
        ntile = t_all // ROUTE_TILE
        nb = -(-(t_all * TOP_K + ntile * N_EXPERTS * (SUBLANES - 1)) // m) + N_EXPERTS
        p_rows = nb * m
        seg = seg.reshape(-1)
        pad = pad.reshape(-1)
        nused = (pend[-1:] // m).astype(I32)
        tail = nused + jnp.arange(nb - (t_all * TOP_K) // m, dtype=I32)
        zblk = jnp.concatenate([jnp.where(padded > 0, pend // m - 1, -1),
                                jnp.where(tail < nb, tail, -1)]).astype(I32)
        blk_row = jnp.minimum(jnp.arange(nb, dtype=I32), nused - 1) * m
        block_e = jnp.sum((pend[None, :] <= blk_row[:, None]).astype(I32), axis=1)

        sources = (h2l,) if last else (h2l, h2c)
        xb = _dispatch_call(seg, pad, zblk, io, wo, sources, p_rows)
        yb = _expert_call(block_e, nused, xb, w_gate, w_up, w_down, i)
        xl_new = _combine_call(seg, pad, yb, io, xl, mods5, i, lat_row(ROUTE_TILE),
                               final_norm if last else None, 0)
        if not last:
            xc = _combine_call(seg, pad, yb, io, xc, mods5, i, ctx_row(0), None, n_l // ROUTE_TILE)
        xl = xl_new
    return xl.reshape(batch, seq, d)
```

```python
import functools

import jax
import jax.numpy as jnp
from jax import lax
from jax.experimental import pallas as pl
from jax.experimental.pallas import tpu as pltpu

F32 = jnp.float32
BF16 = jnp.bfloat16
I32 = jnp.int32
U32 = jnp.uint32
HIGHEST = lax.Precision.HIGHEST

GRID_W = 64
NORM_EPS = 1e-6
GLA_HEADS = 4
GLA_DK = 64
GLA_DV = 128
GLA_KW = GLA_HEADS * GLA_DK
GLA_WIDTH = GLA_HEADS * GLA_DV
GLA_GATE_RANK = 16
GLA_TAU = 16.0
GLA_CHUNK = 64
SWA_HEADS = 8
SWA_KV_HEADS = 2
SWA_HD = 64
SWA_WIDTH = SWA_HEADS * SWA_HD
SWA_KVW = SWA_KV_HEADS * SWA_HD
WINDOW = 128
ATT_BLOCK = 128
SWA_BLOCKS_PER_STEP = 4
ROPE_THETA = 10000.0
ROPE_AXIS_DIM = SWA_HD // 2
N_EXPERTS = 16
N_GROUPS = 4
EXPERTS_PER_GROUP = N_EXPERTS // N_GROUPS
TOP_K = 2
D_EXPERT = 512
MOE_BLOCK = 512
IN_SIZES = (GLA_KW, GLA_KW, GLA_WIDTH, GLA_WIDTH, GLA_GATE_RANK, GLA_GATE_RANK, SWA_WIDTH, SWA_KVW, SWA_KVW)

LANES = 128
SUBLANES = 8
VMEM_LIMIT = 48 * 1024 * 1024

C_QK = 0
C_V = 512
C_G = 1024
C_Z = 1536
C_SQ = 1664
C_SK = 2176
C_SV = 2304
C_END = 2432
SV_LANES = 4 * LANES
NEG_BIG = -1e30


def _cparams(sem):
    return pltpu.CompilerParams(dimension_semantics=sem, vmem_limit_bytes=VMEM_LIMIT)


def _pack_bf16_pairs(x):
    n = x.shape[1] // 2
    lo = lax.bitcast_convert_type(x[:, :n].astype(BF16).astype(F32), U32)
    hi = lax.bitcast_convert_type(x[:, n:].astype(BF16).astype(F32), U32)
    return (lo >> 16) | hi


def _unpack_bf16_pairs(w):
    lo = lax.bitcast_convert_type(w << 16, F32)
    hi = lax.bitcast_convert_type(w & jnp.uint32(0xFFFF0000), F32)
    return jnp.concatenate([lo.astype(BF16), hi.astype(BF16)], axis=-1)


def _ada_kernel(a_ref, w_ref, b_ref, o_ref):
    a = a_ref[...]
    act = a * jax.nn.sigmoid(a)
    o_ref[...] = jnp.dot(act, w_ref[...], preferred_element_type=F32, precision=HIGHEST) + b_ref[...]


def _ada_call(rows, w_ada, b_ada):
    depth, d, n6 = w_ada.shape
    tn = 1536
    return pl.pallas_call(
        _ada_kernel,
        grid=(depth, n6 // tn),
        in_specs=[
            pl.BlockSpec((SUBLANES, d), lambda l, j: (0, 0)),
            pl.BlockSpec((None, d, tn), lambda l, j: (l, 0, j)),
            pl.BlockSpec((None, 1, tn), lambda l, j: (l, 0, j)),
        ],
        out_specs=pl.BlockSpec((None, SUBLANES, tn), lambda l, j: (l, 0, j)),
        out_shape=jax.ShapeDtypeStruct((depth, SUBLANES, n6), F32),
        compiler_params=_cparams(("arbitrary", "arbitrary")),
        name="adaln",
    )(rows, w_ada, b_ada.reshape(depth, 1, n6))


def _rope_tile(xj, cos, sin, lane_lo):
    partner = jnp.where(lane_lo, pltpu.roll(xj, LANES - 16, 1), pltpu.roll(xj, 16, 1))
    return xj * cos + partner * sin


def _inproj_kernel(*refs, rope):
    if rope:
        (x_ref, g_ref, sc_ref, sh_ref, w_ref, cos_ref, sin_ref,
         qk_ref, v_ref, gate_ref, z_ref, sq_ref, sk_ref, sv_ref) = refs
    else:
        (x_ref, g_ref, sc_ref, sh_ref, w_ref,
         qk_ref, v_ref, gate_ref, z_ref, sq_ref, sk_ref, sv_ref) = refs
    half = x_ref.shape[0] // 2
    scale = g_ref[...] * (1.0 + sc_ref[...])
    normed = []
    for r in range(2):
        x = x_ref[r * half:(r + 1) * half, :]
        ms = jnp.mean(x * x, axis=-1, keepdims=True)
        normed.append(((x * lax.rsqrt(ms + NORM_EPS)) * scale + sh_ref[...]).astype(BF16))

    for r in range(2):
        rows = slice(r * half, (r + 1) * half)
        hb = normed[r]

        def proj(a, b):
            return jnp.dot(hb, w_ref[:, a:b], preferred_element_type=F32)

        qk = proj(C_QK, C_V)
        qk_ref[rows, :GLA_KW] = (qk[:, :GLA_KW] * (GLA_DK ** -0.5)).astype(qk_ref.dtype)
        qk_ref[rows, GLA_KW:] = qk[:, GLA_KW:].astype(qk_ref.dtype)
        v_ref[rows, :] = proj(C_V, C_G).astype(v_ref.dtype)
        gate_ref[rows, :] = proj(C_G, C_Z).astype(gate_ref.dtype)
        z_ref[rows, :] = proj(C_Z, C_SQ)
        sq = proj(C_SQ, C_SK) * (SWA_HD ** -0.5)
        sk = proj(C_SK, C_SV)
        sv = proj(C_SV, C_END)
        if rope:
            cos = cos_ref[rows, :]
            sin = sin_ref[rows, :]
            lane_lo = (lax.broadcasted_iota(I32, cos.shape, 1) % 32) < 16
            for j in range(SWA_WIDTH // LANES):
                sl = slice(j * LANES, (j + 1) * LANES)
                sq_ref[rows, sl] = _rope_tile(sq[:, sl], cos, sin, lane_lo).astype(sq_ref.dtype)
            sk = _rope_tile(sk, cos, sin, lane_lo)
        else:
            sq_ref[rows, :] = sq.astype(sq_ref.dtype)
        first = lax.broadcasted_iota(I32, sk.shape, 1) < SWA_HD
        sk_sw = pltpu.roll(sk, SWA_HD, 1)
        sk_ref[rows, :LANES] = jnp.where(first, sk, sk_sw).astype(sk_ref.dtype)
        sk_ref[rows, LANES:] = jnp.where(first, sk_sw, sk).astype(sk_ref.dtype)
        sv_sw = pltpu.roll(sv, SWA_HD, 1)
        sv_ref[rows, 0 * LANES:1 * LANES] = jnp.where(first, sv, 1.0).astype(sv_ref.dtype)
        sv_ref[rows, 1 * LANES:2 * LANES] = jnp.where(first, 1.0, sv_sw).astype(sv_ref.dtype)
        sv_ref[rows, 2 * LANES:3 * LANES] = jnp.where(first, sv_sw, 1.0).astype(sv_ref.dtype)
        sv_ref[rows, 3 * LANES:4 * LANES] = jnp.where(first, 1.0, sv).astype(sv_ref.dtype)


def _inproj_call(x2d, g, mods5, layer, row_of_tile, w_packed, tabs, tm, tiles_per_seq):
    n, d = x2d.shape
    nt = n // tm
    rope = tabs is not None

    def mod_spec(k):
        return pl.BlockSpec((None, None, None, 1, d), lambda t: (layer, row_of_tile(t), k, 0, 0))

    in_specs = [
        pl.BlockSpec((tm, d), lambda t: (t, 0)),
        pl.BlockSpec((1, d), lambda t: (0, 0)),
        mod_spec(1), mod_spec(0),
        pl.BlockSpec((d, C_END), lambda t: (0, 0)),
    ]
    args = [x2d, g.reshape(1, d), mods5, mods5, w_packed]
    if rope:
        in_specs += [pl.BlockSpec((tm, LANES), lambda t: (t % tiles_per_seq, 0))] * 2
        args += [tabs[0], tabs[1]]
    widths = (2 * GLA_KW, GLA_WIDTH, GLA_WIDTH, LANES, SWA_WIDTH, 2 * SWA_KVW, SV_LANES)
    dtypes = (BF16, BF16, BF16, F32, BF16, BF16, BF16)
    return pl.pallas_call(
        functools.partial(_inproj_kernel, rope=rope),
        grid=(nt,),
        in_specs=in_specs,
        out_specs=[pl.BlockSpec((tm, w), lambda t: (t, 0)) for w in widths],
        out_shape=[jax.ShapeDtypeStruct((n, w), dt) for w, dt in zip(widths, dtypes)],
        compiler_params=_cparams(("arbitrary",)),
        name="inproj",
    )(*args)


def _log_sigmoid(x):
    return jnp.minimum(x, 0.0) - jnp.log(1.0 + jnp.exp(-jnp.abs(x)))


def _gla_prepare(q_ref, k_ref, v_ref, z_ref, up_ref, bias_ref, tri_ref, scr, fwd):
    qin_s, km0_s, km1_s, kout_s, vm0_s, vm1_s, dec_s = scr[:7]
    qbd_s = scr[9]
    tl = q_ref.shape[0]
    c = GLA_CHUNK
    z = z_ref[...]
    z_hi = z.astype(BF16).astype(F32)
    zc = z_hi + pltpu.roll(z - z_hi, 2 * GLA_GATE_RANK, 1) + pltpu.roll(z_hi, 4 * GLA_GATE_RANK, 1)
    x = jnp.dot(zc.astype(BF16), up_ref[...], preferred_element_type=F32) + bias_ref[...]
    la = _log_sigmoid(x) * (1.0 / GLA_TAU)
    hi = la.astype(BF16)
    lo = (la - hi.astype(F32)).astype(BF16)
    sums = jnp.dot(tri_ref[...], jnp.concatenate([hi, lo], axis=1), preferred_element_type=F32)
    b = sums[:, :LANES] + sums[:, LANES:]
    b3 = b.reshape(tl // c, c, LANES)
    edge = b3[:, c - 1:c, :] if fwd else b3[:, 0:1, :]
    btot = jnp.broadcast_to(edge, b3.shape).reshape(tl, LANES)
    q = q_ref[...].astype(F32)
    k = k_ref[...].astype(F32)
    first = (lax.broadcasted_iota(I32, (tl, LANES), 0) % (2 * c)) < c
    q_in = q * jnp.exp(b)
    qin_s[...] = q_in.astype(BF16)
    qbd_s[...] = jnp.concatenate([jnp.where(first, q_in, 0.0), jnp.where(first, 0.0, q_in)], axis=1).astype(BF16)
    k_in = k * jnp.exp(-b)
    head0 = lax.broadcasted_iota(I32, (tl, LANES), 1) < GLA_DK
    km0_s[...] = jnp.where(head0, k_in, 0.0).astype(BF16)
    km1_s[...] = jnp.where(head0, 0.0, k_in).astype(BF16)
    k_out = k * jnp.exp(btot - b)
    kout_s[...] = jnp.concatenate([jnp.where(first, k_out, 0.0), jnp.where(first, 0.0, k_out)], axis=1).astype(BF16)
    dec_s[...] = jnp.exp(btot)
    vf = v_ref[...].astype(F32)
    vhead0 = lax.broadcasted_iota(I32, (tl, 2 * GLA_DV), 1) < GLA_DV
    vm0_s[...] = jnp.where(vhead0, vf, 0.0).astype(BF16)
    vm1_s[...] = jnp.where(vhead0, 0.0, vf).astype(BF16)


def _gla_increment(v_ref, scr, pair):
    kout_s, u_s = scr[3], scr[7]
    rows = pl.ds(pair * 2 * GLA_CHUNK, 2 * GLA_CHUNK)
    u_t = lax.dot_general(v_ref[rows, :], kout_s[rows, :], (((0,), (0,)), ((), ())),
                          preferred_element_type=F32)
    srow = lax.broadcasted_iota(I32, (2 * GLA_DV, 2 * LANES), 0) // GLA_DV
    scol = (lax.broadcasted_iota(I32, (2 * GLA_DV, 2 * LANES), 1) % LANES) // GLA_DK
    u_s[pair] = jnp.where(srow == scol, u_t, 0.0)


def _gla_states(scr, s_ref, chunk_order):
    dec_s, u_s, sprev_s = scr[6], scr[7], scr[8]
    s_t = s_ref[...]
    for cidx in chunk_order:
        pair, lanes = cidx // 2, pl.ds((cidx % 2) * LANES, LANES)
        sprev_s[pair, :, lanes] = s_t.astype(BF16)
        r0 = cidx * GLA_CHUNK
        s_t = s_t * dec_s[r0:r0 + 1, :] + u_s[pair, :, lanes]
    s_ref[...] = s_t


def _gla_output(o_ref, scr, pair, fwd):
    qin_s, km0_s, km1_s, _, vm0_s, vm1_s, _, _, sprev_s, qbd_s = scr
    c = GLA_CHUNK
    r0 = pair * 2 * c
    rows = pl.ds(r0, 2 * c)
    ca, cb = pl.ds(r0, c), pl.ds(r0 + c, c)
    kst = jnp.concatenate([km0_s[ca, :], km1_s[ca, :], km0_s[cb, :], km1_s[cb, :]], axis=0)
    a = lax.dot_general(qin_s[rows, :], kst, (((1,), (1,)), ((), ())), preferred_element_type=F32)
    ri = lax.broadcasted_iota(I32, (2 * c, 4 * c), 0)
    ci = lax.broadcasted_iota(I32, (2 * c, 4 * c), 1)
    same_chunk = (ri // c) == (ci // (2 * c))
    keep = same_chunk & ((ci % c <= ri % c) if fwd else (ci % c >= ri % c))
    a = jnp.where(keep, a, 0.0).astype(BF16)
    vbd = jnp.concatenate([vm0_s[ca, :], vm1_s[ca, :], vm0_s[cb, :], vm1_s[cb, :]], axis=0)
    o = jnp.dot(a, vbd, preferred_element_type=F32)
    o = o + lax.dot_general(qbd_s[rows, :], sprev_s[pair], (((1,), (1,)), ((), ())), preferred_element_type=F32)
    o_ref[rows, :] = o.astype(o_ref.dtype)


def _gla_kernel(qf_ref, kf_ref, vf_ref, zf_ref, qb_ref, kb_ref, vb_ref, zb_ref,
                upf_ref, upb_ref, bf_ref, bb_ref, trif_ref, trib_ref, s0_ref,
                of_ref, ob_ref, sfin_ref, sf_scr, sb_scr, *scr, nchunk):
    i = pl.program_id(2)
    nt = pl.num_programs(2)
    scr_f, scr_b = scr[:len(scr) // 2], scr[len(scr) // 2:]

    @pl.when(i == 0)
    def _():
        sf_scr[...] = s0_ref[0]
        sb_scr[...] = s0_ref[1]

    _gla_prepare(qf_ref, kf_ref, vf_ref, zf_ref, upf_ref, bf_ref, trif_ref, scr_f, True)
    _gla_prepare(qb_ref, kb_ref, vb_ref, zb_ref, upb_ref, bb_ref, trib_ref, scr_b, False)
    for pair in range(nchunk // 2):
        _gla_increment(vf_ref, scr_f, pair)
        _gla_increment(vb_ref, scr_b, pair)
    _gla_states(scr_f, sf_scr, range(nchunk))
    _gla_states(scr_b, sb_scr, range(nchunk - 1, -1, -1))
    for pair in range(nchunk // 2):
        _gla_output(of_ref, scr_f, pair, True)
        _gla_output(ob_ref, scr_b, pair, False)

    @pl.when(i == nt - 1)
    def _():
        sfin_ref[0] = sf_scr[...]
        sfin_ref[1] = sb_scr[...]


def _gla_call(qk, v, z, upf_pad, upb_pad, bias_f, bias_b, s0, batch, seq, tl):
    n = qk.shape[0]
    nt = seq // tl
    npair = GLA_HEADS // 2

    def fwd_row(b, p, i):
        return b * nt + i

    def bwd_row(b, p, i):
        return b * nt + (nt - 1 - i)

    def specs(row):
        return [
            pl.BlockSpec((tl, LANES), lambda b, p, i: (row(b, p, i), p)),
            pl.BlockSpec((tl, LANES), lambda b, p, i: (row(b, p, i), npair + p)),
            pl.BlockSpec((tl, 2 * GLA_DV), lambda b, p, i: (row(b, p, i), p)),
            pl.BlockSpec((tl, LANES), lambda b, p, i: (row(b, p, i), 0)),
        ]

    in_specs = specs(fwd_row) + specs(bwd_row) + [
        pl.BlockSpec((None, LANES, LANES), lambda b, p, i: (p, 0, 0)),
        pl.BlockSpec((None, LANES, LANES), lambda b, p, i: (p, 0, 0)),
        pl.BlockSpec((None, 1, LANES), lambda b, p, i: (p, 0, 0)),
        pl.BlockSpec((None, 1, LANES), lambda b, p, i: (p, 0, 0)),
        pl.BlockSpec((None, tl, tl), lambda b, p, i: (0, 0, 0)),
        pl.BlockSpec((None, tl, tl), lambda b, p, i: (1, 0, 0)),
        pl.BlockSpec((None, None, 2, 2 * GLA_DV, LANES), lambda b, p, i: (b, p, 0, 0, 0)),
    ]
    ri = jnp.arange(tl)[:, None]
    ci = jnp.arange(tl)[None, :]
    same = (ri // GLA_CHUNK) == (ci // GLA_CHUNK)
    tri = jnp.stack([same & (ci <= ri), same & (ci >= ri)]).astype(BF16)
    npairs = tl // (2 * GLA_CHUNK)
    dir_scratch = [
        pltpu.VMEM((tl, LANES), BF16),
        pltpu.VMEM((tl, LANES), BF16),
        pltpu.VMEM((tl, LANES), BF16),
        pltpu.VMEM((tl, 2 * LANES), BF16),
        pltpu.VMEM((tl, 2 * GLA_DV), BF16),
        pltpu.VMEM((tl, 2 * GLA_DV), BF16),
        pltpu.VMEM((tl, LANES), F32),
        pltpu.VMEM((npairs, 2 * GLA_DV, 2 * LANES), F32),
        pltpu.VMEM((npairs, 2 * GLA_DV, 2 * LANES), BF16),
        pltpu.VMEM((tl, 2 * LANES), BF16),
    ]
    out_specs = [
        pl.BlockSpec((tl, 2 * GLA_DV), lambda b, p, i: (fwd_row(b, p, i), p)),
        pl.BlockSpec((tl, 2 * GLA_DV), lambda b, p, i: (bwd_row(b, p, i), p)),
        pl.BlockSpec((None, None, 2, 2 * GLA_DV, LANES), lambda b, p, i: (b, p, 0, 0, 0)),
    ]
    out_shape = [
        jax.ShapeDtypeStruct((n, GLA_WIDTH), BF16),
        jax.ShapeDtypeStruct((n, GLA_WIDTH), BF16),
        jax.ShapeDtypeStruct((batch, npair, 2, 2 * GLA_DV, LANES), F32),
    ]
    return pl.pallas_call(
        functools.partial(_gla_kernel, nchunk=tl // GLA_CHUNK),
        grid=(batch, npair, nt),
        in_specs=in_specs,
        out_specs=out_specs,
        out_shape=out_shape,
        scratch_shapes=[pltpu.VMEM((2 * GLA_DV, LANES), F32), pltpu.VMEM((2 * GLA_DV, LANES), F32)]
        + dir_scratch + dir_scratch,
        compiler_params=_cparams(("arbitrary", "arbitrary", "arbitrary")),
        name="gla",
    )(qk, qk, v, z, qk, qk, v, z, upf_pad, upb_pad, bias_f, bias_b, tri, tri, s0)


def _swa_block(sink_ref, q_ref, o_ref, row0, kall, vall, prev_ok, next_ok):
    w = ATT_BLOCK
    local = prev_ok is not None
    qrows = pl.ds(row0, w)
    r2 = lax.broadcasted_iota(I32, (2 * w, w), 0) % w
    c2 = lax.broadcasted_iota(I32, (2 * w, w), 1)
    if local:
        bias_prev = jnp.where((c2 >= r2) & prev_ok, 0.0, NEG_BIG)
        bias_next = jnp.where((c2 <= r2) & next_ok, 0.0, NEG_BIG)
    first = c2 < SWA_HD
    top = lax.broadcasted_iota(I32, (2 * w, 1), 0) < w
    scores = []
    for g in range(SWA_KV_HEADS):
        kd = kall[:, g * LANES:(g + 1) * LANES]
        qs = jnp.concatenate([q_ref[qrows, (2 * g) * LANES:(2 * g + 1) * LANES],
                              q_ref[qrows, (2 * g + 1) * LANES:(2 * g + 2) * LANES]], axis=0).astype(F32)
        for half in range(2):
            qm = (jnp.where(first, qs, 0.0) if half == 0 else jnp.where(first, 0.0, qs)).astype(BF16)
            s = lax.dot_general(qm, kd, (((1,), (1,)), ((), ())), preferred_element_type=F32)
            if local:
                s = jnp.concatenate([s[:, :w] + bias_prev, s[:, w:2 * w], s[:, 2 * w:3 * w] + bias_next,
                                     s[:, 3 * w:]], axis=1)
            scores.append(s)
    for g in range(SWA_KV_HEADS):
        outs = []
        for half in range(2):
            s = scores[2 * g + half]
            sk = jnp.where(top, sink_ref[4 * g + half], sink_ref[4 * g + 2 + half])
            m = jnp.maximum(jnp.max(s, axis=-1, keepdims=True), sk)
            p = jnp.exp((s - m).astype(BF16))
            va = vall[:, (2 * g + half) * LANES:(2 * g + half + 1) * LANES]
            acc = jnp.dot(p, va, preferred_element_type=F32)
            den = pltpu.roll(acc, SWA_HD, 1) + jnp.exp(sk - m)
            outs.append(acc / den)
        out = jnp.where(first, outs[0], outs[1])
        o_ref[qrows, (2 * g) * LANES:(2 * g + 1) * LANES] = out[:w].astype(o_ref.dtype)
        o_ref[qrows, (2 * g + 1) * LANES:(2 * g + 2) * LANES] = out[w:].astype(o_ref.dtype)


def _swa_kernel(*refs, nstep, local):
    w = ATT_BLOCK
    if not local:
        sink_ref, q_ref, kx_ref, vx_ref, o_ref = refs
        _swa_block(sink_ref, q_ref, o_ref, 0, kx_ref[...], vx_ref[...], None, None)
        return
    sink_ref, q_ref, kp_ref, kc_ref, kn_ref, vp_ref, vc_ref, vn_ref, kx_ref, vx_ref, o_ref = refs
    i = pl.program_id(1)
    nblk = q_ref.shape[0] // w
    k_blocks = [kp_ref[...]] + [kc_ref[j * w:(j + 1) * w, :] for j in range(nblk)] + [kn_ref[...]]
    v_blocks = [vp_ref[...]] + [vc_ref[j * w:(j + 1) * w, :] for j in range(nblk)] + [vn_ref[...]]
    exists = [i > 0] + [True] * nblk + [i < nstep - 1]
    for sub in range(nblk):
        kall = jnp.concatenate(k_blocks[sub:sub + 3] + [kx_ref[...]], axis=0)
        vall = jnp.concatenate(v_blocks[sub:sub + 3] + [vx_ref[...]], axis=0)
        _swa_block(sink_ref, q_ref, o_ref, sub * w, kall, vall, exists[sub], exists[sub + 2])


def _swa_call(sq, skd, svd, kcd, vcd, sink, batch, seq, lc):
    n = sq.shape[0]
    w = ATT_BLOCK
    nb = seq // w
    nblk = SWA_BLOCKS_PER_STEP if nb % SWA_BLOCKS_PER_STEP == 0 else 1
    nstep = nb // nblk
    kvw = 2 * SWA_KVW

    def pair(b, i):
        return (b * nstep + i, 0)

    def before(b, i):
        return (b * nb + jnp.maximum(nblk * i - 1, 0), 0)

    def after(b, i):
        return (b * nb + jnp.minimum(nblk * (i + 1), nb - 1), 0)

    def kv_specs(width):
        return [pl.BlockSpec((w, width), before), pl.BlockSpec((nblk * w, width), pair),
                pl.BlockSpec((w, width), after)]

    def ctx_spec(width):
        return pl.BlockSpec((lc, width), lambda b, i: (b, 0))

    return pl.pallas_call(
        functools.partial(_swa_kernel, nstep=nstep, local=True),
        grid=(batch, nstep),
        in_specs=[pl.BlockSpec(memory_space=pltpu.SMEM), pl.BlockSpec((nblk * w, SWA_WIDTH), pair)]
        + kv_specs(kvw) + kv_specs(SV_LANES) + [ctx_spec(kvw), ctx_spec(SV_LANES)],
        out_specs=pl.BlockSpec((nblk * w, SWA_WIDTH), pair),
        out_shape=jax.ShapeDtypeStruct((n, SWA_WIDTH), BF16),
        compiler_params=_cparams(("arbitrary", "arbitrary")),
        name="swa",
    )(sink, sq, skd, skd, skd, svd, svd, svd, kcd, vcd)


def _swa_ctx_call(sq, kcd, vcd, sink, batch, lc):
    n = sq.shape[0]
    w = ATT_BLOCK
    nb = lc // w
    kvw = 2 * SWA_KVW
    return pl.pallas_call(
        functools.partial(_swa_kernel, nstep=0, local=False),
        grid=(batch, nb),
        in_specs=[pl.BlockSpec(memory_space=pltpu.SMEM),
                  pl.BlockSpec((w, SWA_WIDTH), lambda b, i: (b * nb + i, 0)),
                  pl.BlockSpec((lc, kvw), lambda b, i: (b, 0)),
                  pl.BlockSpec((lc, SV_LANES), lambda b, i: (b, 0))],
        out_specs=pl.BlockSpec((w, SWA_WIDTH), lambda b, i: (b * nb + i, 0)),
        out_shape=jax.ShapeDtypeStruct((n, SWA_WIDTH), BF16),
        compiler_params=_cparams(("arbitrary", "arbitrary")),
        name="swa_ctx",
    )(sink, sq, kcd, vcd)


def _outproj_kernel(of_ref, ob_ref, gate_ref, gn_ref, swa_ref, w_ref, x_ref, g1_ref, n2_ref, sc_ref, sh_ref,
                    wr_ref, xo_ref, h2_ref, lg_ref):
    o = of_ref[...].astype(F32) + ob_ref[...].astype(F32)
    parts = []
    for h in range(GLA_HEADS):
        oh = o[:, h * GLA_DV:(h + 1) * GLA_DV]
        ms = jnp.mean(oh * oh, axis=-1, keepdims=True)
        parts.append(oh * lax.rsqrt(ms + NORM_EPS))
    on = jnp.concatenate(parts, axis=-1) * gn_ref[...]
    gate = gate_ref[...].astype(F32)
    gla = on * (gate * jax.nn.sigmoid(gate))
    mix = jnp.concatenate([gla.astype(BF16), swa_ref[...]], axis=-1)
    y = jnp.dot(mix, w_ref[...], preferred_element_type=F32)
    xo = x_ref[...] + g1_ref[...] * y
    xo_ref[...] = xo
    ms = jnp.mean(xo * xo, axis=-1, keepdims=True)
    h2 = (xo * lax.rsqrt(ms + NORM_EPS)) * n2_ref[...]
    h2 = h2 * (1.0 + sc_ref[...]) + sh_ref[...]
    h2_ref[...] = _pack_bf16_pairs(h2)
    hi = h2.astype(BF16)
    lo = (h2 - hi.astype(F32)).astype(BF16)
    both = jnp.dot(hi, wr_ref[...], preferred_element_type=F32)
    lg = both[:, :LANES] + both[:, LANES:] + jnp.dot(lo, wr_ref[:, :LANES], preferred_element_type=F32)
    lg_ref[...] = jnp.transpose(lg)[:N_EXPERTS, :]


def _outproj_call(o_f, o_b, gate, gn, swa, w_out_b, x2d, mods5, layer, row_of_tile, n2, wr_cat, tm):
    n, d = x2d.shape
    nt = n // tm

    def mod_spec(k):
        return pl.BlockSpec((None, None, None, 1, d), lambda t: (layer, row_of_tile(t), k, 0, 0))

    return pl.pallas_call(
        _outproj_kernel,
        grid=(nt,),
        in_specs=[
            pl.BlockSpec((tm, GLA_WIDTH), lambda t: (t, 0)),
            pl.BlockSpec((tm, GLA_WIDTH), lambda t: (t, 0)),
            pl.BlockSpec((tm, GLA_WIDTH), lambda t: (t, 0)),
            pl.BlockSpec((1, GLA_WIDTH), lambda t: (0, 0)),
            pl.BlockSpec((tm, SWA_WIDTH), lambda t: (t, 0)),
            pl.BlockSpec((d, d), lambda t: (0, 0)),
            pl.BlockSpec((tm, d), lambda t: (t, 0)),
            mod_spec(2),
            pl.BlockSpec((1, d), lambda t: (0, 0)),
            mod_spec(4), mod_spec(3),
            pl.BlockSpec((d, 2 * LANES), lambda t: (0, 0)),
        ],
        out_specs=[
            pl.BlockSpec((tm, d), lambda t: (t, 0)),
            pl.BlockSpec((tm, d // 2), lambda t: (t, 0)),
            pl.BlockSpec((N_EXPERTS, tm), lambda t: (0, t)),
        ],
        out_shape=[
            jax.ShapeDtypeStruct((n, d), F32),
            jax.ShapeDtypeStruct((n, d // 2), U32),
            jax.ShapeDtypeStruct((N_EXPERTS, n), F32),
        ],
        compiler_params=_cparams(("arbitrary",)),
        name="outproj",
    )(o_f, o_b, gate, gn.reshape(1, GLA_WIDTH), swa, w_out_b, x2d, mods5, n2.reshape(1, d), mods5, mods5, wr_cat)


def _first_index(vals, target):
    idx = jnp.full(target.shape, len(vals) - 1, I32)
    for i in range(len(vals) - 2, -1, -1):
        idx = jnp.where(vals[i] == target, i, idx)
    return idx


def _route_kernel(lg_ref, br_ref, io_ref, wo_ref, seg_ref, pad_ref, blk_ref, cnt_scr, seg_scr, pad_scr, *, tn):
    phase = pl.program_id(0)
    step = pl.program_id(1)
    group = lg_ref.shape[1] // tn

    @pl.when((phase == 0) & (step == 0))
    def _():
        cnt_scr[...] = jnp.zeros_like(cnt_scr)
        seg_scr[...] = jnp.zeros_like(seg_scr)
        pad_scr[...] = jnp.zeros_like(pad_scr)

    @pl.when((phase == 1) & (step == 0))
    def _():
        pad8 = jnp.floor((cnt_scr[...] + (SUBLANES - 1)) * (1.0 / SUBLANES)) * SUBLANES
        rows_e = jnp.sum(pad8, axis=1, keepdims=True)
        blocks = jnp.floor((rows_e + (MOE_BLOCK - 1)) * (1.0 / MOE_BLOCK)) * MOE_BLOCK
        r128 = lax.broadcasted_iota(I32, (LANES, LANES), 0)
        c128 = lax.broadcasted_iota(I32, (LANES, LANES), 1)
        before_tile = jnp.where(r128 < c128, 1.0, 0.0).astype(BF16)
        seg = jnp.dot((pad8 * (1.0 / SUBLANES)).astype(BF16), before_tile,
                      preferred_element_type=F32) * SUBLANES
        run = jnp.zeros((1, 1), F32)
        for e in range(N_EXPERTS):
            seg_scr[e:e + 1, :] = seg[e:e + 1, :] + run
            run = run + blocks[e:e + 1, :]
        pad_scr[...] = pad8
        seg_ref[...] = seg_scr[...].astype(I32)
        pad_ref[...] = pad8.astype(I32)
        blk_ref[...] = jnp.broadcast_to(blocks, blk_ref.shape).astype(I32)

    for g in range(group):
        lanes = slice(g * tn, (g + 1) * tn)
        _route_tile(lg_ref[:, lanes], br_ref, io_ref, wo_ref, lanes, step * group + g, phase, cnt_scr, pad_scr)


def _route_tile(lg, br_ref, io_ref, wo_ref, lanes, t, phase, cnt_scr, pad_scr):
    tn = lg.shape[1]
    tile_lane = lax.broadcasted_iota(I32, (N_EXPERTS, LANES), 1)
    s = jax.nn.sigmoid(lg)
    sb = s + br_ref[...]
    rows_s = [s[e:e + 1, :] for e in range(N_EXPERTS)]
    rows_b = [sb[e:e + 1, :] for e in range(N_EXPERTS)]
    gscore, gi1, gi2 = [], [], []
    epg = EXPERTS_PER_GROUP
    for g in range(N_GROUPS):
        a = rows_b[g * epg:(g + 1) * epg]
        m1 = functools.reduce(jnp.maximum, a)
        i1 = _first_index(a, m1)
        rest = [jnp.where(i1 == i, -jnp.inf, a[i]) for i in range(epg)]
        m2 = functools.reduce(jnp.maximum, rest)
        i2 = _first_index(rest, m2)
        gscore.append(m1 + m2)
        gi1.append(i1)
        gi2.append(i2)
    gm = functools.reduce(jnp.maximum, gscore)
    gsel = _first_index(gscore, gm)
    i1 = gi1[N_GROUPS - 1]
    i2 = gi2[N_GROUPS - 1]
    for g in range(N_GROUPS - 2, -1, -1):
        i1 = jnp.where(gsel == g, gi1[g], i1)
        i2 = jnp.where(gsel == g, gi2[g], i2)
    idx0 = gsel * epg + i1
    idx1 = gsel * epg + i2
    s0 = jnp.zeros_like(rows_s[0])
    s1 = jnp.zeros_like(rows_s[0])
    for e in range(N_EXPERTS):
        s0 = jnp.where(idx0 == e, rows_s[e], s0)
        s1 = jnp.where(idx1 == e, rows_s[e], s1)
    tot = s0 + s1
    w0 = s0 / tot
    w1 = s1 / tot

    eidx = lax.broadcasted_iota(I32, (N_EXPERTS, tn), 0)
    oh0 = eidx == idx0
    oh1 = eidx == idx1
    oh = jnp.where(oh0 | oh1, 1.0, 0.0)
    rr = lax.broadcasted_iota(I32, (tn, tn), 0)
    cc = lax.broadcasted_iota(I32, (tn, tn), 1)
    upper = jnp.where(rr < cc, 1.0, 0.0).astype(BF16)
    before = jnp.dot(oh.astype(BF16), upper, preferred_element_type=F32)

    @pl.when(phase == 0)
    def _():
        cnt_scr[...] = cnt_scr[...] + jnp.where(tile_lane == t, jnp.sum(oh, axis=1, keepdims=True), 0.0)

    pad_col = jnp.sum(jnp.where(tile_lane == t, pad_scr[...], 0.0), axis=1, keepdims=True)
    run = jnp.zeros((1, 1), F32)
    offs = []
    for e in range(N_EXPERTS):
        offs.append(run)
        run = run + pad_col[e:e + 1, :]
    pos = before + jnp.concatenate(offs, axis=0)
    spos0 = jnp.sum(jnp.where(oh0, pos, 0.0), axis=0, keepdims=True)
    spos1 = jnp.sum(jnp.where(oh1, pos, 0.0), axis=0, keepdims=True)
    zi = jnp.zeros((SUBLANES - 4, tn), I32)
    io_ref[:, lanes] = jnp.concatenate([idx0, idx1, spos0.astype(I32), spos1.astype(I32), zi], axis=0)
    wo_ref[:, lanes] = jnp.concatenate([w0, w1, jnp.zeros((SUBLANES - 2, tn), F32)], axis=0)


def _route_call(logits_t, b_router, tn):
    ne, t_all = logits_t.shape
    ntile = t_all // tn
    assert ntile <= LANES
    group = max(g for g in range(1, ROUTE_TILES_PER_STEP + 1) if ntile % g == 0)
    tw = group * tn
    tok_out = pl.BlockSpec((SUBLANES, tw), lambda p, t: (0, t * p))
    tab_out = pl.BlockSpec((ne, LANES), lambda p, t: (0, 0))
    tab = jax.ShapeDtypeStruct((ne, LANES), I32)
    return pl.pallas_call(
        functools.partial(_route_kernel, tn=tn),
        grid=(2, ntile // group),
        in_specs=[pl.BlockSpec((ne, tw), lambda p, t: (0, t)), pl.BlockSpec((ne, 1), lambda p, t: (0, 0))],
        out_specs=[tok_out, tok_out, tab_out, tab_out, tab_out],
        out_shape=[
            jax.ShapeDtypeStruct((SUBLANES, t_all), I32),
            jax.ShapeDtypeStruct((SUBLANES, t_all), F32),
            tab, tab, tab,
        ],
        scratch_shapes=[pltpu.VMEM((ne, LANES), F32)] * 3,
        compiler_params=_cparams(("arbitrary", "arbitrary")),
        name="route",
    )(logits_t, b_router.reshape(ne, 1))


SEG_SIZES = (512, 256, 128, 64, 32, 16, 8)
ROUTE_TILE = 512
ROUTE_TILES_PER_STEP = 4
STAGE_ROWS = TOP_K * ROUTE_TILE + LANES


def _segment_copies(seg_ref, pad_ref, t, stage, hbm, sem, to_hbm, wait):
    if wait:
        total = functools.reduce(lambda a, b: a + b, [pad_ref[e * LANES + t] for e in range(N_EXPERTS)])
        for size in (2 * SEG_SIZES[0],) + SEG_SIZES:
            @pl.when((total & size) != 0)
            def _(size=size):
                src, dst = stage.at[pl.ds(0, size), :], hbm.at[pl.ds(0, size), :]
                if not to_hbm:
                    src, dst = dst, src
                pltpu.make_async_copy(src, dst, sem).wait()
        return
    loc = 0
    for e in range(N_EXPERTS):
        n = pad_ref[e * LANES + t]
        start = seg_ref[e * LANES + t]
        for size in SEG_SIZES:
            off = n & (-2 * size)

            @pl.when((n & size) != 0)
            def _(off=off, size=size, loc=loc, start=start):
                s_rows = pl.ds(pl.multiple_of(loc + off, SUBLANES), size)
                h_rows = pl.ds(pl.multiple_of(start + off, SUBLANES), size)
                src, dst = (stage.at[s_rows, :], hbm.at[h_rows, :])
                if not to_hbm:
                    src, dst = dst, src
                pltpu.make_async_copy(src, dst, sem).start()
        loc = loc + n


def _sort_matrix(io_ref, rows):
    r = lax.broadcasted_iota(I32, (rows, io_ref.shape[1]), 0)
    return r == io_ref[2:3, :], r == io_ref[3:4, :]


def _dispatch_kernel(seg_ref, pad_ref, zblk_ref, *refs, tiles):
    srcs = refs[:len(tiles)]
    io_ref, wo_ref, xb_hbm, stage, zbuf, sem = refs[len(tiles):]
    i = pl.program_id(0)
    m = MOE_BLOCK

    def zero_copy(j):
        start = pl.multiple_of(zblk_ref[j] * m, m)
        return pltpu.make_async_copy(zbuf, xb_hbm.at[pl.ds(start, m), :], sem.at[0])

    @pl.when(i == 0)
    def _():
        zbuf[...] = jnp.zeros_like(zbuf)
        for j in range(zblk_ref.shape[0]):
            @pl.when(zblk_ref[j] >= 0)
            def _():
                zero_copy(j).start()
        for j in range(zblk_ref.shape[0]):
            @pl.when(zblk_ref[j] >= 0)
            def _():
                zero_copy(j).wait()

    hp = srcs[0][...]
    if len(srcs) == 2:
        hp = jnp.where(i < tiles[0], hp, srcs[1][...])
    x = _unpack_bf16_pairs(hp)
    m0, m1 = _sort_matrix(io_ref, STAGE_ROWS)
    sort = jnp.where(m0 | m1, 1.0, 0.0).astype(BF16)
    xs = jnp.dot(sort, x, preferred_element_type=F32)
    dh = x.shape[1] // 2
    slot = i % 2
    cur = stage.at[slot]
    cur[:, :dh] = _pack_bf16_pairs(xs)
    ws = jnp.sum(jnp.where(m0, wo_ref[0:1, :], 0.0) + jnp.where(m1, wo_ref[1:2, :], 0.0), axis=1, keepdims=True)
    cur[:, dh:] = jnp.broadcast_to(lax.bitcast_convert_type(ws, U32), (STAGE_ROWS, LANES))
    _segment_copies(seg_ref, pad_ref, i, cur, xb_hbm, sem.at[slot], to_hbm=True, wait=False)

    @pl.when(i > 0)
    def _():
        _segment_copies(seg_ref, pad_ref, i - 1, stage.at[1 - slot], xb_hbm, sem.at[1 - slot], to_hbm=True, wait=True)

    @pl.when(i == pl.num_programs(0) - 1)
    def _():
        _segment_copies(seg_ref, pad_ref, i, cur, xb_hbm, sem.at[slot], to_hbm=True, wait=True)


def _dispatch_call(seg, pad, zblk, io, wo, sources, p_rows):
    tile = ROUTE_TILE
    dh = sources[0].shape[1]
    tiles = tuple(s.shape[0] // tile for s in sources)
    firsts = tuple(sum(tiles[:k]) for k in range(len(tiles)))

    def src_spec(first, ntile):
        return pl.BlockSpec((tile, dh), lambda i, *_: (jnp.clip(i - first, 0, ntile - 1), 0))

    tok_spec = pl.BlockSpec((SUBLANES, tile), lambda i, *_: (0, i))
    grid_spec = pltpu.PrefetchScalarGridSpec(
        num_scalar_prefetch=3,
        grid=(sum(tiles),),
        in_specs=[src_spec(f, n) for f, n in zip(firsts, tiles)] + [tok_spec, tok_spec],
        out_specs=pl.BlockSpec(memory_space=pl.ANY),
        scratch_shapes=[pltpu.VMEM((2, STAGE_ROWS, dh + LANES), U32), pltpu.VMEM((MOE_BLOCK, dh + LANES), U32),
                        pltpu.SemaphoreType.DMA((2,))],
    )
    return pl.pallas_call(
        functools.partial(_dispatch_kernel, tiles=tiles),
        grid_spec=grid_spec,
        out_shape=jax.ShapeDtypeStruct((p_rows, dh + LANES), U32),
        compiler_params=_cparams(("arbitrary",)),
        name="dispatch",
    )(seg, pad, zblk, *sources, io, wo)


def _expert_kernel(be_ref, nu_ref, va_ref, x_ref, wg_ref, wu_ref, wd_ref, o_ref, wg_b, wu_b, wd_b):
    j = pl.program_id(0)
    dh = o_ref.shape[1]
    half = o_ref.shape[0] // 2
    valid = va_ref[j]

    @pl.when((j == 0) | (be_ref[j] != be_ref[jnp.maximum(j - 1, 0)]))
    def _():
        wg_b[...] = wg_ref[...].astype(BF16)
        wu_b[...] = wu_ref[...].astype(BF16)
        wd_b[...] = wd_ref[...].astype(BF16)

    def ffn_halves(halves):
        gates = []
        for r in halves:
            x = _unpack_bf16_pairs(x_ref[r * half:(r + 1) * half, :dh])
            gates.append((jnp.dot(x, wg_b[...], preferred_element_type=F32),
                          jnp.dot(x, wu_b[...], preferred_element_type=F32)))
        for r, (a, u) in zip(halves, gates):
            hmid = (a * jax.nn.sigmoid(a)) * u
            y = jnp.dot(hmid.astype(BF16), wd_b[...], preferred_element_type=F32)
            row_w = lax.bitcast_convert_type(x_ref[r * half:(r + 1) * half, dh:dh + 1], F32)
            o_ref[r * half:(r + 1) * half, :] = _pack_bf16_pairs(y * row_w)

    @pl.when(valid > half)
    def _():
        ffn_halves((0, 1))

    @pl.when((valid > 0) & (valid <= half))
    def _():
        ffn_halves((0,))
        o_ref[half:, :] = jnp.zeros((half, dh), o_ref.dtype)

    @pl.when(valid == 0)
    def _():
        o_ref[...] = jnp.zeros_like(o_ref)


def _expert_call(block_e, nused, valid, xb, wg, wu, wd, layer):
    p_rows = xb.shape[0]
    dh = xb.shape[1] - LANES
    d = 2 * dh
    m = MOE_BLOCK
    nb = p_rows // m
    de = wg.shape[-1]

    def xmap(j, be, nu, va):
        return (jnp.minimum(j, nu[0] - 1), 0)

    def wmap(j, be, nu, va):
        return (layer, be[j], 0, 0)

    grid_spec = pltpu.PrefetchScalarGridSpec(
        num_scalar_prefetch=3,
        grid=(nb,),
        in_specs=[
            pl.BlockSpec((m, dh + LANES), xmap),
            pl.BlockSpec((None, None, d, de), wmap),
            pl.BlockSpec((None, None, d, de), wmap),
            pl.BlockSpec((None, None, de, d), wmap),
        ],
        out_specs=pl.BlockSpec((m, dh), lambda j, be, nu, va: (j, 0)),
        scratch_shapes=[pltpu.VMEM((d, de), BF16), pltpu.VMEM((d, de), BF16), pltpu.VMEM((de, d), BF16)],
    )
    return pl.pallas_call(
        _expert_kernel,
        grid_spec=grid_spec,
        out_shape=jax.ShapeDtypeStruct((p_rows, dh), U32),
        compiler_params=_cparams(("arbitrary",)),
        name="experts",
    )(block_e, nused, valid, xb, wg, wu, wd)


def _combine_kernel(*refs, tile0, final):
    if final:
        seg_ref, pad_ref, yb_hbm, io_ref, x_ref, g2_ref, fn_ref, o_ref, stage, sem = refs
    else:
        seg_ref, pad_ref, yb_hbm, io_ref, x_ref, g2_ref, o_ref, stage, sem = refs
    i = pl.program_id(0)

    slot = i % 2

    @pl.when(i == 0)
    def _():
        stage[...] = jnp.zeros_like(stage)
        _segment_copies(seg_ref, pad_ref, tile0, stage.at[0], yb_hbm, sem.at[0], to_hbm=False, wait=False)

    @pl.when(i + 1 < pl.num_programs(0))
    def _():
        _segment_copies(seg_ref, pad_ref, tile0 + i + 1, stage.at[1 - slot], yb_hbm, sem.at[1 - slot],
                        to_hbm=False, wait=False)

    _segment_copies(seg_ref, pad_ref, tile0 + i, stage.at[slot], yb_hbm, sem.at[slot], to_hbm=False, wait=True)
    rows = _unpack_bf16_pairs(stage[slot])
    m0, m1 = _sort_matrix(io_ref, STAGE_ROWS)
    pick = jnp.where(m0 | m1, 1.0, 0.0).astype(BF16)
    y = lax.dot_general(pick, rows, (((0,), (0,)), ((), ())), preferred_element_type=F32)
    xo = x_ref[...] + g2_ref[...] * y
    if final:
        ms = jnp.mean(xo * xo, axis=-1, keepdims=True)
        xo = (xo * lax.rsqrt(ms + NORM_EPS)) * fn_ref[...]
    o_ref[...] = xo


def _combine_call(seg, pad, yb, io, x2d, mods5, layer, row_of_tile, final_g, tile0):
    n, d = x2d.shape
    tm = ROUTE_TILE
    final = final_g is not None
    in_specs = [
        pl.BlockSpec(memory_space=pl.ANY),
        pl.BlockSpec((SUBLANES, tm), lambda t, *_: (0, tile0 + t)),
        pl.BlockSpec((tm, d), lambda t, *_: (t, 0)),
        pl.BlockSpec((None, None, None, 1, d), lambda t, *_: (layer, row_of_tile(t), 5, 0, 0)),
    ]
    args = [yb, io, x2d, mods5]
    if final:
        in_specs.append(pl.BlockSpec((1, d), lambda t, *_: (0, 0)))
        args.append(final_g.reshape(1, d))
    grid_spec = pltpu.PrefetchScalarGridSpec(
        num_scalar_prefetch=2,
        grid=(n // tm,),
        in_specs=in_specs,
        out_specs=pl.BlockSpec((tm, d), lambda t, *_: (t, 0)),
        scratch_shapes=[pltpu.VMEM((2, STAGE_ROWS, d // 2), U32), pltpu.SemaphoreType.DMA((2,))],
    )
    return pl.pallas_call(
        functools.partial(_combine_kernel, tile0=tile0, final=final),
        grid_spec=grid_spec,
        out_shape=jax.ShapeDtypeStruct((n, d), F32),
        compiler_params=_cparams(("arbitrary",)),
        name="combine",
    )(seg, pad, *args)


def _pack_w_in(w):
    d = w.shape[0]
    pts, acc = [], 0
    for s in IN_SIZES[:-1]:
        acc += s
        pts.append(acc)
    wq, wk, wv, wg, wzf, wzb, wsq, wsk, wsv = jnp.split(w, pts, axis=-1)
    zpad = jnp.zeros((d, LANES - 2 * GLA_GATE_RANK), w.dtype)
    return jnp.concatenate([wq, wk, wv, wg, wzf, wzb, zpad, wsq, wsk, wsv], axis=-1).astype(BF16)


def _pad_up(up, row0):
    up = up.reshape(GLA_GATE_RANK, GLA_HEADS // 2, LANES).transpose(1, 0, 2)
    hi = up.astype(BF16)
    lo = (up - hi.astype(F32)).astype(BF16)
    out = jnp.zeros((GLA_HEADS // 2, LANES, LANES), BF16)
    for group, part in enumerate((hi, hi, lo)):
        r = group * 2 * GLA_GATE_RANK + row0
        out = out.at[:, r:r + GLA_GATE_RANK, :].set(part)
    return out


def _rope_tables(seq):
    rows = seq // GRID_W
    row = jnp.repeat(jnp.arange(rows, dtype=F32), GRID_W)
    col = jnp.tile(jnp.arange(GRID_W, dtype=F32), rows)
    n_freq = ROPE_AXIS_DIM // 2
    inv = ROPE_THETA ** (-(jnp.arange(n_freq, dtype=F32) * 2.0 / ROPE_AXIS_DIM))
    ang_r = row[:, None] * inv[None, :]
    ang_c = col[:, None] * inv[None, :]
    cr, sr, cc, sc = jnp.cos(ang_r), jnp.sin(ang_r), jnp.cos(ang_c), jnp.sin(ang_c)
    cos64 = jnp.concatenate([cr, cr, cc, cc], axis=-1)
    sin64 = jnp.concatenate([-sr, sr, -sc, sc], axis=-1)
    reps = LANES // SWA_HD
    return jnp.tile(cos64, (1, reps)), jnp.tile(sin64, (1, reps))


def _tile(n, pref):
    t = pref
    while n % t:
        t //= 2
    return t


def kernel(x, c, ctx, c_ctx, w_ada, b_ada, norm1, norm2, w_in, gla_up_f, gla_bias_f, gla_up_b, gla_bias_b,
           gla_norm, swa_sink, w_out, w_router, b_router, w_gate, w_up, w_down, final_norm):
    batch, seq, d = x.shape
    lc = ctx.shape[1]
    depth = w_ada.shape[0]
    n_l = batch * seq
    n_c = batch * lc
    npair = GLA_HEADS // 2
    assert batch + 1 <= SUBLANES
    assert n_l % ROUTE_TILE == 0 and n_c % ROUTE_TILE == 0

    tm_l = _tile(seq, 512)
    tm_in = _tile(seq, 1024)
    tm_c = _tile(lc, 256)
    tl_l = _tile(seq, 512)
    tl_c = _tile(lc, 512)

    rows = jnp.zeros((SUBLANES, d), F32).at[:batch].set(c).at[batch].set(c_ctx)
    mods = _ada_call(rows, w_ada, b_ada)
    mods5 = mods.reshape(depth, SUBLANES, 6, 1, d)

    tabs = _rope_tables(seq)
    wr_pad = jnp.zeros((d, LANES), F32).at[:, :N_EXPERTS].set(w_router)
    wr_hi = wr_pad.astype(BF16)
    wr_lo = (wr_pad - wr_hi.astype(F32)).astype(BF16)
    wr_cat = jnp.concatenate([wr_hi, wr_lo], axis=-1)

    def lat_row(tm):
        return lambda t: t // (seq // tm)

    def ctx_row(tm):
        return lambda t: batch

    xl = x.reshape(n_l, d)
    xc = ctx.reshape(n_c, d)
    for i in range(depth):
        last = i == depth - 1
        w_packed = _pack_w_in(w_in[i])
        upf = _pad_up(gla_up_f[i], 0)
        upb = _pad_up(gla_up_b[i], GLA_GATE_RANK)
        bias_f = gla_bias_f[i].reshape(npair, 1, LANES)
        bias_b = gla_bias_b[i].reshape(npair, 1, LANES)
        w_out_b = w_out[i].astype(BF16)

        c_qk, c_v, c_g, c_z, c_sq, c_sk, c_sv = _inproj_call(
            xc, norm1[i], mods5, i, ctx_row(tm_c), w_packed, None, tm_c, lc // tm_c)
        l_qk, l_v, l_g, l_z, l_sq, l_sk, l_sv = _inproj_call(
            xl, norm1[i], mods5, i, lat_row(tm_in), w_packed, tabs, tm_in, seq // tm_in)

        s_zero = jnp.zeros((batch, npair, 2, 2 * GLA_DV, LANES), F32)
        oc_f, oc_b, s_ctx = _gla_call(c_qk, c_v, c_z, upf, upb, bias_f, bias_b, s_zero, batch, lc, tl_c)
        ol_f, ol_b, _ = _gla_call(l_qk, l_v, l_z, upf, upb, bias_f, bias_b, s_ctx, batch, seq, tl_l)

        swa_l = _swa_call(l_sq, l_sk, l_sv, c_sk, c_sv, swa_sink[i], batch, seq, lc)
        xl, h2l, lg_l = _outproj_call(ol_f, ol_b, l_g, gla_norm[i], swa_l, w_out_b, xl, mods5, i, lat_row(tm_l),
                                      norm2[i], wr_cat, tm_l)
        if last:
            logits_t = lg_l
            t_all = n_l
        else:
            swa_c = _swa_ctx_call(c_sq, c_sk, c_sv, swa_sink[i], batch, lc)
            xc, h2c, lg_c = _outproj_call(oc_f, oc_b, c_g, gla_norm[i], swa_c, w_out_b, xc, mods5, i, ctx_row(tm_c),
                                          norm2[i], wr_cat, tm_c)
            logits_t = jnp.concatenate([lg_l, lg_c], axis=1)
            t_all = n_l + n_c

        io, wo, seg, pad, blk = _route_call(logits_t, b_router, ROUTE_TILE)
        m = MOE_BLOCK
        padded = blk[:, 0]
        pend = jnp.cumsum(padded)
        ntile = t_all // ROUTE_TILE
        nb = -(-(t_all * TOP_K + ntile * N_EXPERTS * (SUBLANES - 1)) // m) + N_EXPERTS
        p_rows = nb * m
        seg = seg.reshape(-1)
        pad = pad.reshape(-1)
        nused = (pend[-1:] // m).astype(I32)
        tail = nused + jnp.arange(nb - (t_all * TOP_K) // m, dtype=I32)
        zblk = jnp.concatenate([jnp.where(padded > 0, pend // m - 1, -1),
                                jnp.where(tail < nb, tail, -1)]).astype(I32)
        blk_row = jnp.minimum(jnp.arange(nb, dtype=I32), nused - 1) * m
        block_e = jnp.sum((pend[None, :] <= blk_row[:, None]).astype(I32), axis=1)
        rows_e = jnp.sum(pad.reshape(N_EXPERTS, LANES), axis=1)
        own = (block_e[:, None] == jnp.arange(N_EXPERTS, dtype=I32)[None, :]).astype(I32)
        seg_end = jnp.sum(own * (pend - padded + rows_e)[None, :], axis=1)
        blk_idx = jnp.arange(nb, dtype=I32)
        valid = jnp.where(blk_idx < nused, jnp.clip(seg_end - blk_idx * m, 0, m), 0).astype(I32)

        sources = (h2l,) if last else (h2l, h2c)
        xb = _dispatch_call(seg, pad, zblk, io, wo, sources, p_rows)
        yb = _expert_call(block_e, nused, valid, xb, w_gate, w_up, w_down, i)
        xl_new = _combine_call(seg, pad, yb, io, xl, mods5, i, lat_row(ROUTE_TILE),
                               final_norm if last else None, 0)
        if not last:
            xc = _combine_call(seg, pad, yb, io, xc, mods5, i, ctx_row(0), None, n_l // ROUTE_TILE)
        xl = xl_new
    return xl.reshape(batch, seq, d)
```

```python
import functools

import jax
import jax.numpy as jnp
from jax import lax
from jax.experimental import pallas as pl
from jax.experimental.pallas import tpu as pltpu

F32 = jnp.float32
BF16 = jnp.bfloat16
I32 = jnp.int32
U32 = jnp.uint32
HIGHEST = lax.Precision.HIGHEST

GRID_W = 64
NORM_EPS = 1e-6
GLA_HEADS = 4
GLA_DK = 64
GLA_DV = 128
GLA_KW = GLA_HEADS * GLA_DK
GLA_WIDTH = GLA_HEADS * GLA_DV
GLA_GATE_RANK = 16
GLA_TAU = 16.0
GLA_CHUNK = 64
SWA_HEADS = 8
SWA_KV_HEADS = 2
SWA_HD = 64
SWA_WIDTH = SWA_HEADS * SWA_HD
SWA_KVW = SWA_KV_HEADS * SWA_HD
WINDOW = 128
ATT_BLOCK = 128
ROPE_THETA = 10000.0
ROPE_AXIS_DIM = SWA_HD // 2
ROPE_HALF = ROPE_AXIS_DIM // 2
N_EXPERTS = 16
N_GROUPS = 4
EXPERTS_PER_GROUP = N_EXPERTS // N_GROUPS
TOP_K = 2
D_EXPERT = 512
IN_SIZES = (GLA_KW, GLA_KW, GLA_WIDTH, GLA_WIDTH, GLA_GATE_RANK, GLA_GATE_RANK, SWA_WIDTH, SWA_KVW, SWA_KVW)

LANES = 128
SUBLANES = 8
VMEM_LIMIT = 48 * 1024 * 1024

ROW_TILE = 512
GLA_TILE = 1024
INPROJ_TILE = 1024
CTX_TILE = 256
ADA_COLS = 1536
SWA_BLOCKS_PER_STEP = 4
MOE_BLOCK = 512
ROUTE_TILE = 512
ROUTE_TILES_PER_STEP = 4
SEG_SIZES = (512, 256, 128, 64, 32, 16, 8)
STAGE_ROWS = TOP_K * ROUTE_TILE + LANES

C_QK = 0
C_V = 512
C_G = 1024
C_Z = 1536
C_SQ = 1664
C_SK = 2176
C_SV = 2304
C_END = 2432
SV_LANES = 4 * LANES
NEG_BIG = -1e30


def _cparams(sem):
    return pltpu.CompilerParams(dimension_semantics=sem, vmem_limit_bytes=VMEM_LIMIT)


def _pack_bf16_pairs(x):
    return _pack_rounded_pairs(x.astype(BF16).astype(F32))


def _pack_rounded_pairs(xr):
    n = xr.shape[1] // 2
    lo = lax.bitcast_convert_type(xr[:, :n], U32)
    hi = lax.bitcast_convert_type(xr[:, n:], U32)
    return (lo >> 16) | hi


def _unpack_bf16_pairs(w):
    lo = lax.bitcast_convert_type(w << 16, F32)
    hi = lax.bitcast_convert_type(w & jnp.uint32(0xFFFF0000), F32)
    return jnp.concatenate([lo.astype(BF16), hi.astype(BF16)], axis=-1)


def _ada_kernel(a_ref, w_ref, b_ref, o_ref):
    a = a_ref[...]
    act = a * jax.nn.sigmoid(a)
    o_ref[...] = jnp.dot(act, w_ref[...], preferred_element_type=F32, precision=HIGHEST) + b_ref[...]


def _ada_call(rows, w_ada, b_ada):
    depth, d, n6 = w_ada.shape
    tn = ADA_COLS
    return pl.pallas_call(
        _ada_kernel,
        grid=(depth, n6 // tn),
        in_specs=[
            pl.BlockSpec((SUBLANES, d), lambda l, j: (0, 0)),
            pl.BlockSpec((None, d, tn), lambda l, j: (l, 0, j)),
            pl.BlockSpec((None, 1, tn), lambda l, j: (l, 0, j)),
        ],
        out_specs=pl.BlockSpec((None, SUBLANES, tn), lambda l, j: (l, 0, j)),
        out_shape=jax.ShapeDtypeStruct((depth, SUBLANES, n6), F32),
        compiler_params=_cparams(("arbitrary", "arbitrary")),
        name="adaln",
    )(rows, w_ada, b_ada.reshape(depth, 1, n6))


def _rope_tile(xj, cos, sin, lane_lo):
    partner = jnp.where(lane_lo, pltpu.roll(xj, LANES - ROPE_HALF, 1), pltpu.roll(xj, ROPE_HALF, 1))
    return xj * cos + partner * sin


def _inproj_kernel(*refs, rope):
    if rope:
        (x_ref, g_ref, sc_ref, sh_ref, w_ref, cos_ref, sin_ref,
         qk_ref, v_ref, gate_ref, z_ref, sq_ref, sk_ref, sv_ref) = refs
    else:
        (x_ref, g_ref, sc_ref, sh_ref, w_ref,
         qk_ref, v_ref, gate_ref, z_ref, sq_ref, sk_ref, sv_ref) = refs
    half = x_ref.shape[0] // 2
    scale = g_ref[...] * (1.0 + sc_ref[...])
    normed = []
    for r in range(2):
        x = x_ref[r * half:(r + 1) * half, :]
        ms = jnp.mean(x * x, axis=-1, keepdims=True)
        normed.append(((x * lax.rsqrt(ms + NORM_EPS)) * scale + sh_ref[...]).astype(BF16))

    for r in range(2):
        rows = slice(r * half, (r + 1) * half)
        hb = normed[r]

        def proj(a, b):
            return jnp.dot(hb, w_ref[:, a:b], preferred_element_type=F32)

        qk = proj(C_QK, C_V)
        qk_ref[rows, :GLA_KW] = (qk[:, :GLA_KW] * (GLA_DK ** -0.5)).astype(qk_ref.dtype)
        qk_ref[rows, GLA_KW:] = qk[:, GLA_KW:].astype(qk_ref.dtype)
        v_ref[rows, :] = proj(C_V, C_G).astype(v_ref.dtype)
        gate_ref[rows, :] = proj(C_G, C_Z).astype(gate_ref.dtype)
        z_ref[rows, :] = proj(C_Z, C_SQ)
        sq = proj(C_SQ, C_SK) * (SWA_HD ** -0.5)
        sk = proj(C_SK, C_SV)
        sv = proj(C_SV, C_END)
        if rope:
            cos = cos_ref[rows, :]
            sin = sin_ref[rows, :]
            lane_lo = (lax.broadcasted_iota(I32, cos.shape, 1) % ROPE_AXIS_DIM) < ROPE_HALF
            for j in range(SWA_WIDTH // LANES):
                sl = slice(j * LANES, (j + 1) * LANES)
                sq_ref[rows, sl] = _rope_tile(sq[:, sl], cos, sin, lane_lo).astype(sq_ref.dtype)
            sk = _rope_tile(sk, cos, sin, lane_lo)
        else:
            sq_ref[rows, :] = sq.astype(sq_ref.dtype)
        first = lax.broadcasted_iota(I32, sk.shape, 1) < SWA_HD
        sk_sw = pltpu.roll(sk, SWA_HD, 1)
        sk_ref[rows, :LANES] = jnp.where(first, sk, sk_sw).astype(sk_ref.dtype)
        sk_ref[rows, LANES:] = jnp.where(first, sk_sw, sk).astype(sk_ref.dtype)
        sv_sw = pltpu.roll(sv, SWA_HD, 1)
        sv_ref[rows, 0 * LANES:1 * LANES] = jnp.where(first, sv, 1.0).astype(sv_ref.dtype)
        sv_ref[rows, 1 * LANES:2 * LANES] = jnp.where(first, 1.0, sv_sw).astype(sv_ref.dtype)
        sv_ref[rows, 2 * LANES:3 * LANES] = jnp.where(first, sv_sw, 1.0).astype(sv_ref.dtype)
        sv_ref[rows, 3 * LANES:4 * LANES] = jnp.where(first, 1.0, sv).astype(sv_ref.dtype)


def _inproj_call(x2d, g, mods5, layer, row_of_tile, w_packed, tabs, tm, tiles_per_seq):
    n, d = x2d.shape
    nt = n // tm
    rope = tabs is not None

    def mod_spec(k):
        return pl.BlockSpec((None, None, None, 1, d), lambda t: (layer, row_of_tile(t), k, 0, 0))

    in_specs = [
        pl.BlockSpec((tm, d), lambda t: (t, 0)),
        pl.BlockSpec((1, d), lambda t: (0, 0)),
        mod_spec(1), mod_spec(0),
        pl.BlockSpec((d, C_END), lambda t: (0, 0)),
    ]
    args = [x2d, g.reshape(1, d), mods5, mods5, w_packed]
    if rope:
        in_specs += [pl.BlockSpec((tm, LANES), lambda t: (t % tiles_per_seq, 0))] * 2
        args += [tabs[0], tabs[1]]
    widths = (2 * GLA_KW, GLA_WIDTH, GLA_WIDTH, LANES, SWA_WIDTH, 2 * SWA_KVW, SV_LANES)
    dtypes = (BF16, BF16, BF16, F32, BF16, BF16, BF16)
    return pl.pallas_call(
        functools.partial(_inproj_kernel, rope=rope),
        grid=(nt,),
        in_specs=in_specs,
        out_specs=[pl.BlockSpec((tm, w), lambda t: (t, 0)) for w in widths],
        out_shape=[jax.ShapeDtypeStruct((n, w), dt) for w, dt in zip(widths, dtypes)],
        compiler_params=_cparams(("arbitrary",)),
        name="inproj",
    )(*args)


def _log_sigmoid(x):
    return jnp.minimum(x, 0.0) - jnp.log(1.0 + jnp.exp(-jnp.abs(x)))


def _gla_prepare(q_ref, k_ref, v_ref, z_ref, up_ref, bias_ref, tri_ref, scr, fwd):
    qin_s, km0_s, km1_s, kout_s, vm0_s, vm1_s, dec_s = scr[:7]
    qbd_s = scr[9]
    tl = q_ref.shape[0]
    c = GLA_CHUNK
    z = z_ref[...]
    z_hi = z.astype(BF16).astype(F32)
    zc = z_hi + pltpu.roll(z - z_hi, 2 * GLA_GATE_RANK, 1) + pltpu.roll(z_hi, 4 * GLA_GATE_RANK, 1)
    x = jnp.dot(zc.astype(BF16), up_ref[...], preferred_element_type=F32) + bias_ref[...]
    la = _log_sigmoid(x) * (1.0 / GLA_TAU)
    hi = la.astype(BF16)
    lo = (la - hi.astype(F32)).astype(BF16)
    sums = jnp.dot(tri_ref[...], jnp.concatenate([hi, lo], axis=1), preferred_element_type=F32)
    b = sums[:, :LANES] + sums[:, LANES:]
    b3 = b.reshape(tl // c, c, LANES)
    edge = b3[:, c - 1:c, :] if fwd else b3[:, 0:1, :]
    btot = jnp.broadcast_to(edge, b3.shape).reshape(tl, LANES)
    q = q_ref[...].astype(F32)
    k = k_ref[...].astype(F32)
    first = (lax.broadcasted_iota(I32, (tl, LANES), 0) % (2 * c)) < c
    q_in = q * jnp.exp(b)
    qin_s[...] = q_in.astype(BF16)
    qbd_s[...] = jnp.concatenate([jnp.where(first, q_in, 0.0), jnp.where(first, 0.0, q_in)], axis=1).astype(BF16)
    k_in = k * jnp.exp(-b)
    head0 = lax.broadcasted_iota(I32, (tl, LANES), 1) < GLA_DK
    km0_s[...] = jnp.where(head0, k_in, 0.0).astype(BF16)
    km1_s[...] = jnp.where(head0, 0.0, k_in).astype(BF16)
    k_out = k * jnp.exp(btot - b)
    kout_s[...] = jnp.concatenate([jnp.where(first, k_out, 0.0), jnp.where(first, 0.0, k_out)], axis=1).astype(BF16)
    dec_s[...] = jnp.exp(btot)
    vf = v_ref[...].astype(F32)
    vhead0 = lax.broadcasted_iota(I32, (tl, 2 * GLA_DV), 1) < GLA_DV
    vm0_s[...] = jnp.where(vhead0, vf, 0.0).astype(BF16)
    vm1_s[...] = jnp.where(vhead0, 0.0, vf).astype(BF16)


def _gla_increment(v_ref, scr, pair):
    kout_s, u_s = scr[3], scr[7]
    rows = pl.ds(pair * 2 * GLA_CHUNK, 2 * GLA_CHUNK)
    u_t = lax.dot_general(v_ref[rows, :], kout_s[rows, :], (((0,), (0,)), ((), ())),
                          preferred_element_type=F32)
    srow = lax.broadcasted_iota(I32, (2 * GLA_DV, 2 * LANES), 0) // GLA_DV
    scol = (lax.broadcasted_iota(I32, (2 * GLA_DV, 2 * LANES), 1) % LANES) // GLA_DK
    u_s[pair] = jnp.where(srow == scol, u_t, 0.0)


def _gla_states(scr, s_ref, chunk_order):
    dec_s, u_s, sprev_s = scr[6], scr[7], scr[8]
    s_t = s_ref[...]
    for cidx in chunk_order:
        pair, lanes = cidx // 2, pl.ds((cidx % 2) * LANES, LANES)
        sprev_s[pair, :, lanes] = s_t.astype(BF16)
        r0 = cidx * GLA_CHUNK
        s_t = s_t * dec_s[r0:r0 + 1, :] + u_s[pair, :, lanes]
    s_ref[...] = s_t


def _gla_output(o_ref, scr, pair, fwd):
    qin_s, km0_s, km1_s, _, vm0_s, vm1_s, _, _, sprev_s, qbd_s = scr
    c = GLA_CHUNK
    r0 = pair * 2 * c
    rows = pl.ds(r0, 2 * c)
    ca, cb = pl.ds(r0, c), pl.ds(r0 + c, c)
    kst = jnp.concatenate([km0_s[ca, :], km1_s[ca, :], km0_s[cb, :], km1_s[cb, :]], axis=0)
    a = lax.dot_general(qin_s[rows, :], kst, (((1,), (1,)), ((), ())), preferred_element_type=F32)
    ri = lax.broadcasted_iota(I32, (2 * c, 4 * c), 0)
    ci = lax.broadcasted_iota(I32, (2 * c, 4 * c), 1)
    same_chunk = (ri // c) == (ci // (2 * c))
    keep = same_chunk & ((ci % c <= ri % c) if fwd else (ci % c >= ri % c))
    a = jnp.where(keep, a, 0.0).astype(BF16)
    vbd = jnp.concatenate([vm0_s[ca, :], vm1_s[ca, :], vm0_s[cb, :], vm1_s[cb, :]], axis=0)
    o = jnp.dot(a, vbd, preferred_element_type=F32)
    o = o + lax.dot_general(qbd_s[rows, :], sprev_s[pair], (((1,), (1,)), ((), ())), preferred_element_type=F32)
    o_ref[rows, :] = o.astype(o_ref.dtype)


def _gla_kernel(qf_ref, kf_ref, vf_ref, zf_ref, qb_ref, kb_ref, vb_ref, zb_ref,
                upf_ref, upb_ref, bf_ref, bb_ref, trif_ref, trib_ref, s0_ref,
                of_ref, ob_ref, sfin_ref, sf_scr, sb_scr, *scr, nchunk):
    i = pl.program_id(2)
    nt = pl.num_programs(2)
    scr_f, scr_b = scr[:len(scr) // 2], scr[len(scr) // 2:]

    @pl.when(i == 0)
    def _():
        sf_scr[...] = s0_ref[0]
        sb_scr[...] = s0_ref[1]

    _gla_prepare(qf_ref, kf_ref, vf_ref, zf_ref, upf_ref, bf_ref, trif_ref, scr_f, True)
    _gla_prepare(qb_ref, kb_ref, vb_ref, zb_ref, upb_ref, bb_ref, trib_ref, scr_b, False)
    for pair in range(nchunk // 2):
        _gla_increment(vf_ref, scr_f, pair)
        _gla_increment(vb_ref, scr_b, pair)
    _gla_states(scr_f, sf_scr, range(nchunk))
    _gla_states(scr_b, sb_scr, range(nchunk - 1, -1, -1))
    for pair in range(nchunk // 2):
        _gla_output(of_ref, scr_f, pair, True)
        _gla_output(ob_ref, scr_b, pair, False)

    @pl.when(i == nt - 1)
    def _():
        sfin_ref[0] = sf_scr[...]
        sfin_ref[1] = sb_scr[...]


def _gla_call(qk, v, z, upf_pad, upb_pad, bias_f, bias_b, s0, batch, seq, tl):
    n = qk.shape[0]
    nt = seq // tl
    npair = GLA_HEADS // 2

    def fwd_row(b, p, i):
        return b * nt + i

    def bwd_row(b, p, i):
        return b * nt + (nt - 1 - i)

    def specs(row):
        return [
            pl.BlockSpec((tl, LANES), lambda b, p, i: (row(b, p, i), p)),
            pl.BlockSpec((tl, LANES), lambda b, p, i: (row(b, p, i), npair + p)),
            pl.BlockSpec((tl, 2 * GLA_DV), lambda b, p, i: (row(b, p, i), p)),
            pl.BlockSpec((tl, LANES), lambda b, p, i: (row(b, p, i), 0)),
        ]

    in_specs = specs(fwd_row) + specs(bwd_row) + [
        pl.BlockSpec((None, LANES, LANES), lambda b, p, i: (p, 0, 0)),
        pl.BlockSpec((None, LANES, LANES), lambda b, p, i: (p, 0, 0)),
        pl.BlockSpec((None, 1, LANES), lambda b, p, i: (p, 0, 0)),
        pl.BlockSpec((None, 1, LANES), lambda b, p, i: (p, 0, 0)),
        pl.BlockSpec((None, tl, tl), lambda b, p, i: (0, 0, 0)),
        pl.BlockSpec((None, tl, tl), lambda b, p, i: (1, 0, 0)),
        pl.BlockSpec((None, None, 2, 2 * GLA_DV, LANES), lambda b, p, i: (b, p, 0, 0, 0)),
    ]
    ri = jnp.arange(tl)[:, None]
    ci = jnp.arange(tl)[None, :]
    same = (ri // GLA_CHUNK) == (ci // GLA_CHUNK)
    tri = jnp.stack([same & (ci <= ri), same & (ci >= ri)]).astype(BF16)
    npairs = tl // (2 * GLA_CHUNK)
    dir_scratch = [
        pltpu.VMEM((tl, LANES), BF16),
        pltpu.VMEM((tl, LANES), BF16),
        pltpu.VMEM((tl, LANES), BF16),
        pltpu.VMEM((tl, 2 * LANES), BF16),
        pltpu.VMEM((tl, 2 * GLA_DV), BF16),
        pltpu.VMEM((tl, 2 * GLA_DV), BF16),
        pltpu.VMEM((tl, LANES), F32),
        pltpu.VMEM((npairs, 2 * GLA_DV, 2 * LANES), F32),
        pltpu.VMEM((npairs, 2 * GLA_DV, 2 * LANES), BF16),
        pltpu.VMEM((tl, 2 * LANES), BF16),
    ]
    out_specs = [
        pl.BlockSpec((tl, 2 * GLA_DV), lambda b, p, i: (fwd_row(b, p, i), p)),
        pl.BlockSpec((tl, 2 * GLA_DV), lambda b, p, i: (bwd_row(b, p, i), p)),
        pl.BlockSpec((None, None, 2, 2 * GLA_DV, LANES), lambda b, p, i: (b, p, 0, 0, 0)),
    ]
    out_shape = [
        jax.ShapeDtypeStruct((n, GLA_WIDTH), BF16),
        jax.ShapeDtypeStruct((n, GLA_WIDTH), BF16),
        jax.ShapeDtypeStruct((batch, npair, 2, 2 * GLA_DV, LANES), F32),
    ]
    return pl.pallas_call(
        functools.partial(_gla_kernel, nchunk=tl // GLA_CHUNK),
        grid=(batch, npair, nt),
        in_specs=in_specs,
        out_specs=out_specs,
        out_shape=out_shape,
        scratch_shapes=[pltpu.VMEM((2 * GLA_DV, LANES), F32), pltpu.VMEM((2 * GLA_DV, LANES), F32)]
        + dir_scratch + dir_scratch,
        compiler_params=_cparams(("arbitrary", "arbitrary", "arbitrary")),
        name="gla",
    )(qk, qk, v, z, qk, qk, v, z, upf_pad, upb_pad, bias_f, bias_b, tri, tri, s0)


def _swa_block(sink_ref, q_ref, o_ref, row0, kall, vall, prev_ok, next_ok):
    w = ATT_BLOCK
    local = prev_ok is not None
    qrows = pl.ds(row0, w)
    r2 = lax.broadcasted_iota(I32, (2 * w, w), 0) % w
    c2 = lax.broadcasted_iota(I32, (2 * w, w), 1)
    if local:
        bias_prev = jnp.where((c2 >= r2) & prev_ok, 0.0, NEG_BIG)
        bias_next = jnp.where((c2 <= r2) & next_ok, 0.0, NEG_BIG)
    first = c2 < SWA_HD
    top = lax.broadcasted_iota(I32, (2 * w, 1), 0) < w
    scores = []
    for g in range(SWA_KV_HEADS):
        kd = kall[:, g * LANES:(g + 1) * LANES]
        qs = jnp.concatenate([q_ref[qrows, (2 * g) * LANES:(2 * g + 1) * LANES],
                              q_ref[qrows, (2 * g + 1) * LANES:(2 * g + 2) * LANES]], axis=0).astype(F32)
        for half in range(2):
            qm = (jnp.where(first, qs, 0.0) if half == 0 else jnp.where(first, 0.0, qs)).astype(BF16)
            s = lax.dot_general(qm, kd, (((1,), (1,)), ((), ())), preferred_element_type=F32)
            if local:
                s = jnp.concatenate([s[:, :w] + bias_prev, s[:, w:2 * w], s[:, 2 * w:3 * w] + bias_next,
                                     s[:, 3 * w:]], axis=1)
            scores.append(s)
    for g in range(SWA_KV_HEADS):
        outs = []
        for half in range(2):
            s = scores[2 * g + half]
            sk = jnp.where(top, sink_ref[4 * g + half], sink_ref[4 * g + 2 + half])
            m = jnp.maximum(jnp.max(s, axis=-1, keepdims=True), sk)
            p = jnp.exp((s - m).astype(BF16))
            va = vall[:, (2 * g + half) * LANES:(2 * g + half + 1) * LANES]
            acc = jnp.dot(p, va, preferred_element_type=F32)
            den = pltpu.roll(acc, SWA_HD, 1) + jnp.exp(sk - m)
            outs.append(acc / den)
        out = jnp.where(first, outs[0], outs[1])
        o_ref[qrows, (2 * g) * LANES:(2 * g + 1) * LANES] = out[:w].astype(o_ref.dtype)
        o_ref[qrows, (2 * g + 1) * LANES:(2 * g + 2) * LANES] = out[w:].astype(o_ref.dtype)


def _swa_kernel(*refs, nstep, local):
    w = ATT_BLOCK
    if not local:
        sink_ref, q_ref, kx_ref, vx_ref, o_ref = refs
        _swa_block(sink_ref, q_ref, o_ref, 0, kx_ref[...], vx_ref[...], None, None)
        return
    sink_ref, q_ref, kp_ref, kc_ref, kn_ref, vp_ref, vc_ref, vn_ref, kx_ref, vx_ref, o_ref = refs
    i = pl.program_id(1)
    nblk = q_ref.shape[0] // w
    k_blocks = [kp_ref[...]] + [kc_ref[j * w:(j + 1) * w, :] for j in range(nblk)] + [kn_ref[...]]
    v_blocks = [vp_ref[...]] + [vc_ref[j * w:(j + 1) * w, :] for j in range(nblk)] + [vn_ref[...]]
    exists = [i > 0] + [True] * nblk + [i < nstep - 1]
    for sub in range(nblk):
        kall = jnp.concatenate(k_blocks[sub:sub + 3] + [kx_ref[...]], axis=0)
        vall = jnp.concatenate(v_blocks[sub:sub + 3] + [vx_ref[...]], axis=0)
        _swa_block(sink_ref, q_ref, o_ref, sub * w, kall, vall, exists[sub], exists[sub + 2])


def _swa_call(sq, skd, svd, kcd, vcd, sink, batch, seq, lc):
    n = sq.shape[0]
    w = ATT_BLOCK
    nb = seq // w
    nblk = SWA_BLOCKS_PER_STEP if nb % SWA_BLOCKS_PER_STEP == 0 else 1
    nstep = nb // nblk
    kvw = 2 * SWA_KVW

    def pair(b, i):
        return (b * nstep + i, 0)

    def before(b, i):
        return (b * nb + jnp.maximum(nblk * i - 1, 0), 0)

    def after(b, i):
        return (b * nb + jnp.minimum(nblk * (i + 1), nb - 1), 0)

    def kv_specs(width):
        return [pl.BlockSpec((w, width), before), pl.BlockSpec((nblk * w, width), pair),
                pl.BlockSpec((w, width), after)]

    def ctx_spec(width):
        return pl.BlockSpec((lc, width), lambda b, i: (b, 0))

    return pl.pallas_call(
        functools.partial(_swa_kernel, nstep=nstep, local=True),
        grid=(batch, nstep),
        in_specs=[pl.BlockSpec(memory_space=pltpu.SMEM), pl.BlockSpec((nblk * w, SWA_WIDTH), pair)]
        + kv_specs(kvw) + kv_specs(SV_LANES) + [ctx_spec(kvw), ctx_spec(SV_LANES)],
        out_specs=pl.BlockSpec((nblk * w, SWA_WIDTH), pair),
        out_shape=jax.ShapeDtypeStruct((n, SWA_WIDTH), BF16),
        compiler_params=_cparams(("arbitrary", "arbitrary")),
        name="swa",
    )(sink, sq, skd, skd, skd, svd, svd, svd, kcd, vcd)


def _swa_ctx_call(sq, kcd, vcd, sink, batch, lc):
    n = sq.shape[0]
    w = ATT_BLOCK
    nb = lc // w
    kvw = 2 * SWA_KVW
    return pl.pallas_call(
        functools.partial(_swa_kernel, nstep=0, local=False),
        grid=(batch, nb),
        in_specs=[pl.BlockSpec(memory_space=pltpu.SMEM),
                  pl.BlockSpec((w, SWA_WIDTH), lambda b, i: (b * nb + i, 0)),
                  pl.BlockSpec((lc, kvw), lambda b, i: (b, 0)),
                  pl.BlockSpec((lc, SV_LANES), lambda b, i: (b, 0))],
        out_specs=pl.BlockSpec((w, SWA_WIDTH), lambda b, i: (b * nb + i, 0)),
        out_shape=jax.ShapeDtypeStruct((n, SWA_WIDTH), BF16),
        compiler_params=_cparams(("arbitrary", "arbitrary")),
        name="swa_ctx",
    )(sink, sq, kcd, vcd)


def _outproj_kernel(of_ref, ob_ref, gate_ref, gn_ref, swa_ref, w_ref, x_ref, g1_ref, n2_ref, sc_ref, sh_ref,
                    wr_ref, xo_ref, h2_ref, lg_ref):
    o = of_ref[...].astype(F32) + ob_ref[...].astype(F32)
    parts = []
    for h in range(GLA_HEADS):
        oh = o[:, h * GLA_DV:(h + 1) * GLA_DV]
        ms = jnp.mean(oh * oh, axis=-1, keepdims=True)
        parts.append(oh * lax.rsqrt(ms + NORM_EPS))
    on = jnp.concatenate(parts, axis=-1) * gn_ref[...]
    gate = gate_ref[...].astype(F32)
    gla = on * (gate * jax.nn.sigmoid(gate))
    mix = jnp.concatenate([gla.astype(BF16), swa_ref[...]], axis=-1)
    y = jnp.dot(mix, w_ref[...], preferred_element_type=F32)
    xo = x_ref[...] + g1_ref[...] * y
    xo_ref[...] = xo
    ms = jnp.mean(xo * xo, axis=-1, keepdims=True)
    h2 = (xo * lax.rsqrt(ms + NORM_EPS)) * (n2_ref[...] * (1.0 + sc_ref[...])) + sh_ref[...]
    hi = h2.astype(BF16)
    hi_f = hi.astype(F32)
    h2_ref[...] = _pack_rounded_pairs(hi_f)
    lo = (h2 - hi_f).astype(BF16)
    both = jnp.dot(hi, wr_ref[...], preferred_element_type=F32)
    lg = both[:, :LANES] + both[:, LANES:] + jnp.dot(lo, wr_ref[:, :LANES], preferred_element_type=F32)
    lg_ref[...] = jnp.transpose(lg)[:N_EXPERTS, :]


def _outproj_call(o_f, o_b, gate, gn, swa, w_out_b, x2d, mods5, layer, row_of_tile, n2, wr_cat, tm):
    n, d = x2d.shape
    nt = n // tm

    def mod_spec(k):
        return pl.BlockSpec((None, None, None, 1, d), lambda t: (layer, row_of_tile(t), k, 0, 0))

    return pl.pallas_call(
        _outproj_kernel,
        grid=(nt,),
        in_specs=[
            pl.BlockSpec((tm, GLA_WIDTH), lambda t: (t, 0)),
            pl.BlockSpec((tm, GLA_WIDTH), lambda t: (t, 0)),
            pl.BlockSpec((tm, GLA_WIDTH), lambda t: (t, 0)),
            pl.BlockSpec((1, GLA_WIDTH), lambda t: (0, 0)),
            pl.BlockSpec((tm, SWA_WIDTH), lambda t: (t, 0)),
            pl.BlockSpec((d, d), lambda t: (0, 0)),
            pl.BlockSpec((tm, d), lambda t: (t, 0)),
            mod_spec(2),
            pl.BlockSpec((1, d), lambda t: (0, 0)),
            mod_spec(4), mod_spec(3),
            pl.BlockSpec((d, 2 * LANES), lambda t: (0, 0)),
        ],
        out_specs=[
            pl.BlockSpec((tm, d), lambda t: (t, 0)),
            pl.BlockSpec((tm, d // 2), lambda t: (t, 0)),
            pl.BlockSpec((N_EXPERTS, tm), lambda t: (0, t)),
        ],
        out_shape=[
            jax.ShapeDtypeStruct((n, d), F32),
            jax.ShapeDtypeStruct((n, d // 2), U32),
            jax.ShapeDtypeStruct((N_EXPERTS, n), F32),
        ],
        compiler_params=_cparams(("arbitrary",)),
        name="outproj",
    )(o_f, o_b, gate, gn.reshape(1, GLA_WIDTH), swa, w_out_b, x2d, mods5, n2.reshape(1, d), mods5, mods5, wr_cat)


def _first_index(vals, target):
    idx = jnp.full(target.shape, len(vals) - 1, I32)
    for i in range(len(vals) - 2, -1, -1):
        idx = jnp.where(vals[i] == target, i, idx)
    return idx


def _route_kernel(lg_ref, br_ref, io_ref, wo_ref, seg_ref, pad_ref, blk_ref, cnt_scr, seg_scr, pad_scr, *, tn):
    phase = pl.program_id(0)
    step = pl.program_id(1)
    group = lg_ref.shape[1] // tn

    @pl.when((phase == 0) & (step == 0))
    def _():
        cnt_scr[...] = jnp.zeros_like(cnt_scr)
        seg_scr[...] = jnp.zeros_like(seg_scr)
        pad_scr[...] = jnp.zeros_like(pad_scr)

    @pl.when((phase == 1) & (step == 0))
    def _():
        pad8 = jnp.floor((cnt_scr[...] + (SUBLANES - 1)) * (1.0 / SUBLANES)) * SUBLANES
        rows_e = jnp.sum(pad8, axis=1, keepdims=True)
        blocks = jnp.floor((rows_e + (MOE_BLOCK - 1)) * (1.0 / MOE_BLOCK)) * MOE_BLOCK
        r128 = lax.broadcasted_iota(I32, (LANES, LANES), 0)
        c128 = lax.broadcasted_iota(I32, (LANES, LANES), 1)
        before_tile = jnp.where(r128 < c128, 1.0, 0.0).astype(BF16)
        seg = jnp.dot((pad8 * (1.0 / SUBLANES)).astype(BF16), before_tile,
                      preferred_element_type=F32) * SUBLANES
        run = jnp.zeros((1, 1), F32)
        for e in range(N_EXPERTS):
            seg_scr[e:e + 1, :] = seg[e:e + 1, :] + run
            run = run + blocks[e:e + 1, :]
        pad_scr[...] = pad8
        seg_ref[...] = seg_scr[...].astype(I32)
        pad_ref[...] = pad8.astype(I32)
        blk_ref[...] = jnp.broadcast_to(blocks, blk_ref.shape).astype(I32)

    for g in range(group):
        lanes = slice(g * tn, (g + 1) * tn)
        _route_tile(lg_ref[:, lanes], br_ref, io_ref, wo_ref, lanes, step * group + g, phase, cnt_scr, pad_scr)


def _route_tile(lg, br_ref, io_ref, wo_ref, lanes, t, phase, cnt_scr, pad_scr):
    tn = lg.shape[1]
    tile_lane = lax.broadcasted_iota(I32, (N_EXPERTS, LANES), 1)
    s = jax.nn.sigmoid(lg)
    sb = s + br_ref[...]
    rows_s = [s[e:e + 1, :] for e in range(N_EXPERTS)]
    rows_b = [sb[e:e + 1, :] for e in range(N_EXPERTS)]
    gscore, gi1, gi2 = [], [], []
    epg = EXPERTS_PER_GROUP
    for g in range(N_GROUPS):
        a = rows_b[g * epg:(g + 1) * epg]
        m1 = functools.reduce(jnp.maximum, a)
        i1 = _first_index(a, m1)
        rest = [jnp.where(i1 == i, -jnp.inf, a[i]) for i in range(epg)]
        m2 = functools.reduce(jnp.maximum, rest)
        i2 = _first_index(rest, m2)
        gscore.append(m1 + m2)
        gi1.append(i1)
        gi2.append(i2)
    gm = functools.reduce(jnp.maximum, gscore)
    gsel = _first_index(gscore, gm)
    i1 = gi1[N_GROUPS - 1]
    i2 = gi2[N_GROUPS - 1]
    for g in range(N_GROUPS - 2, -1, -1):
        i1 = jnp.where(gsel == g, gi1[g], i1)
        i2 = jnp.where(gsel == g, gi2[g], i2)
    idx0 = gsel * epg + i1
    idx1 = gsel * epg + i2
    s0 = jnp.zeros_like(rows_s[0])
    s1 = jnp.zeros_like(rows_s[0])
    for e in range(N_EXPERTS):
        s0 = jnp.where(idx0 == e, rows_s[e], s0)
        s1 = jnp.where(idx1 == e, rows_s[e], s1)
    tot = s0 + s1
    w0 = s0 / tot
    w1 = s1 / tot

    eidx = lax.broadcasted_iota(I32, (N_EXPERTS, tn), 0)
    oh0 = eidx == idx0
    oh1 = eidx == idx1
    oh = jnp.where(oh0 | oh1, 1.0, 0.0)
    rr = lax.broadcasted_iota(I32, (tn, tn), 0)
    cc = lax.broadcasted_iota(I32, (tn, tn), 1)
    upper = jnp.where(rr < cc, 1.0, 0.0).astype(BF16)
    before = jnp.dot(oh.astype(BF16), upper, preferred_element_type=F32)

    @pl.when(phase == 0)
    def _():
        cnt_scr[...] = cnt_scr[...] + jnp.where(tile_lane == t, jnp.sum(oh, axis=1, keepdims=True), 0.0)

    pad_col = jnp.sum(jnp.where(tile_lane == t, pad_scr[...], 0.0), axis=1, keepdims=True)
    run = jnp.zeros((1, 1), F32)
    offs = []
    for e in range(N_EXPERTS):
        offs.append(run)
        run = run + pad_col[e:e + 1, :]
    pos = before + jnp.concatenate(offs, axis=0)
    spos0 = jnp.sum(jnp.where(oh0, pos, 0.0), axis=0, keepdims=True)
    spos1 = jnp.sum(jnp.where(oh1, pos, 0.0), axis=0, keepdims=True)
    zi = jnp.zeros((SUBLANES - 4, tn), I32)
    io_ref[:, lanes] = jnp.concatenate([idx0, idx1, spos0.astype(I32), spos1.astype(I32), zi], axis=0)
    wo_ref[:, lanes] = jnp.concatenate([w0, w1, jnp.zeros((SUBLANES - 2, tn), F32)], axis=0)


def _route_call(logits_t, b_router, tn):
    ne, t_all = logits_t.shape
    ntile = t_all // tn
    assert ntile <= LANES
    group = max(g for g in range(1, ROUTE_TILES_PER_STEP + 1) if ntile % g == 0)
    tw = group * tn
    tok_out = pl.BlockSpec((SUBLANES, tw), lambda p, t: (0, t * p))
    tab_out = pl.BlockSpec((ne, LANES), lambda p, t: (0, 0))
    tab = jax.ShapeDtypeStruct((ne, LANES), I32)
    return pl.pallas_call(
        functools.partial(_route_kernel, tn=tn),
        grid=(2, ntile // group),
        in_specs=[pl.BlockSpec((ne, tw), lambda p, t: (0, t)), pl.BlockSpec((ne, 1), lambda p, t: (0, 0))],
        out_specs=[tok_out, tok_out, tab_out, tab_out, tab_out],
        out_shape=[
            jax.ShapeDtypeStruct((SUBLANES, t_all), I32),
            jax.ShapeDtypeStruct((SUBLANES, t_all), F32),
            tab, tab, tab,
        ],
        scratch_shapes=[pltpu.VMEM((ne, LANES), F32)] * 3,
        compiler_params=_cparams(("arbitrary", "arbitrary")),
        name="route",
    )(logits_t, b_router.reshape(ne, 1))


def _segment_copies(seg_ref, pad_ref, t, stage, hbm, sem, to_hbm, wait):
    if wait:
        total = functools.reduce(lambda a, b: a + b, [pad_ref[e * LANES + t] for e in range(N_EXPERTS)])
        for size in (2 * SEG_SIZES[0],) + SEG_SIZES:
            @pl.when((total & size) != 0)
            def _(size=size):
                src, dst = stage.at[pl.ds(0, size), :], hbm.at[pl.ds(0, size), :]
                if not to_hbm:
                    src, dst = dst, src
                pltpu.make_async_copy(src, dst, sem).wait()
        return
    loc = 0
    for e in range(N_EXPERTS):
        n = pad_ref[e * LANES + t]
        start = seg_ref[e * LANES + t]
        for size in SEG_SIZES:
            off = n & (-2 * size)

            @pl.when((n & size) != 0)
            def _(off=off, size=size, loc=loc, start=start):
                s_rows = pl.ds(pl.multiple_of(loc + off, SUBLANES), size)
                h_rows = pl.ds(pl.multiple_of(start + off, SUBLANES), size)
                src, dst = (stage.at[s_rows, :], hbm.at[h_rows, :])
                if not to_hbm:
                    src, dst = dst, src
                pltpu.make_async_copy(src, dst, sem).start()
        loc = loc + n


def _sort_matrix(io_ref, rows):
    r = lax.broadcasted_iota(I32, (rows, io_ref.shape[1]), 0)
    return r == io_ref[2:3, :], r == io_ref[3:4, :]


def _dispatch_kernel(seg_ref, pad_ref, zblk_ref, *refs, tiles):
    srcs = refs[:len(tiles)]
    io_ref, wo_ref, xb_hbm, stage, zbuf, sem = refs[len(tiles):]
    i = pl.program_id(0)
    m = MOE_BLOCK

    def zero_copy(j):
        start = pl.multiple_of(zblk_ref[j] * m, m)
        return pltpu.make_async_copy(zbuf, xb_hbm.at[pl.ds(start, m), :], sem.at[0])

    @pl.when(i == 0)
    def _():
        zbuf[...] = jnp.zeros_like(zbuf)
        for j in range(zblk_ref.shape[0]):
            @pl.when(zblk_ref[j] >= 0)
            def _():
                zero_copy(j).start()
        for j in range(zblk_ref.shape[0]):
            @pl.when(zblk_ref[j] >= 0)
            def _():
                zero_copy(j).wait()

    hp = srcs[0][...]
    if len(srcs) == 2:
        hp = jnp.where(i < tiles[0], hp, srcs[1][...])
    x = _unpack_bf16_pairs(hp)
    m0, m1 = _sort_matrix(io_ref, STAGE_ROWS)
    sort = jnp.where(m0 | m1, 1.0, 0.0).astype(BF16)
    xs = jnp.dot(sort, x, preferred_element_type=F32)
    dh = x.shape[1] // 2
    slot = i % 2
    cur = stage.at[slot]
    cur[:, :dh] = _pack_bf16_pairs(xs)
    ws = jnp.sum(jnp.where(m0, wo_ref[0:1, :], 0.0) + jnp.where(m1, wo_ref[1:2, :], 0.0), axis=1, keepdims=True)
    cur[:, dh:] = jnp.broadcast_to(lax.bitcast_convert_type(ws, U32), (STAGE_ROWS, LANES))
    _segment_copies(seg_ref, pad_ref, i, cur, xb_hbm, sem.at[slot], to_hbm=True, wait=False)

    @pl.when(i > 0)
    def _():
        _segment_copies(seg_ref, pad_ref, i - 1, stage.at[1 - slot], xb_hbm, sem.at[1 - slot], to_hbm=True, wait=True)

    @pl.when(i == pl.num_programs(0) - 1)
    def _():
        _segment_copies(seg_ref, pad_ref, i, cur, xb_hbm, sem.at[slot], to_hbm=True, wait=True)


def _dispatch_call(seg, pad, zblk, io, wo, sources, p_rows):
    tile = ROUTE_TILE
    dh = sources[0].shape[1]
    tiles = tuple(s.shape[0] // tile for s in sources)
    firsts = tuple(sum(tiles[:k]) for k in range(len(tiles)))

    def src_spec(first, ntile):
        return pl.BlockSpec((tile, dh), lambda i, *_: (jnp.clip(i - first, 0, ntile - 1), 0))

    tok_spec = pl.BlockSpec((SUBLANES, tile), lambda i, *_: (0, i))
    grid_spec = pltpu.PrefetchScalarGridSpec(
        num_scalar_prefetch=3,
        grid=(sum(tiles),),
        in_specs=[src_spec(f, n) for f, n in zip(firsts, tiles)] + [tok_spec, tok_spec],
        out_specs=pl.BlockSpec(memory_space=pl.ANY),
        scratch_shapes=[pltpu.VMEM((2, STAGE_ROWS, dh + LANES), U32), pltpu.VMEM((MOE_BLOCK, dh + LANES), U32),
                        pltpu.SemaphoreType.DMA((2,))],
    )
    return pl.pallas_call(
        functools.partial(_dispatch_kernel, tiles=tiles),
        grid_spec=grid_spec,
        out_shape=jax.ShapeDtypeStruct((p_rows, dh + LANES), U32),
        compiler_params=_cparams(("arbitrary",)),
        name="dispatch",
    )(seg, pad, zblk, *sources, io, wo)


def _expert_kernel(be_ref, nu_ref, va_ref, ne_ref, x_ref, wg_hbm, wu_hbm, wd_hbm, o_ref,
                   wg_b, wu_b, wd_b, wg_f, wu_f, wd_f, switch_ref, sem, *, layer):
    j = pl.program_id(0)
    dh = o_ref.shape[1]
    half = o_ref.shape[0] // 2
    valid = va_ref[j]
    e = be_ref[j]

    def weight_copies(expert, slot):
        return [pltpu.make_async_copy(hbm.at[layer, expert], buf.at[slot], sem.at[slot, k])
                for k, (hbm, buf) in enumerate(((wg_hbm, wg_f), (wu_hbm, wu_f), (wd_hbm, wd_f)))]

    @pl.when(j == 0)
    def _():
        switch_ref[0] = 0
        for cp in weight_copies(e, 0):
            cp.start()

    @pl.when((j == 0) | (e != be_ref[jnp.maximum(j - 1, 0)]))
    def _():
        slot = switch_ref[0] % 2
        for cp in weight_copies(e, slot):
            cp.wait()
        wg_b[...] = wg_f[slot].astype(BF16)
        wu_b[...] = wu_f[slot].astype(BF16)
        wd_b[...] = wd_f[slot].astype(BF16)
        nxt = ne_ref[e]

        @pl.when(nxt >= 0)
        def _():
            for cp in weight_copies(nxt, 1 - slot):
                cp.start()
        switch_ref[0] = switch_ref[0] + 1

    def ffn_halves(halves):
        gates = []
        for r in halves:
            x = _unpack_bf16_pairs(x_ref[r * half:(r + 1) * half, :dh])
            gates.append((jnp.dot(x, wg_b[...], preferred_element_type=F32),
                          jnp.dot(x, wu_b[...], preferred_element_type=F32)))
        for r, (a, u) in zip(halves, gates):
            hmid = (a * jax.nn.sigmoid(a)) * u
            y = jnp.dot(hmid.astype(BF16), wd_b[...], preferred_element_type=F32)
            row_w = lax.bitcast_convert_type(x_ref[r * half:(r + 1) * half, dh:dh + 1], F32)
            o_ref[r * half:(r + 1) * half, :] = _pack_bf16_pairs(y * row_w)

    @pl.when(valid > half)
    def _():
        ffn_halves((0, 1))

    @pl.when((valid > 0) & (valid <= half))
    def _():
        ffn_halves((0,))
        o_ref[half:, :] = jnp.zeros((half, dh), o_ref.dtype)

    @pl.when(valid == 0)
    def _():
        o_ref[...] = jnp.zeros_like(o_ref)


def _expert_call(block_e, nused, valid, next_e, xb, wg, wu, wd, layer):
    p_rows = xb.shape[0]
    dh = xb.shape[1] - LANES
    d = 2 * dh
    m = MOE_BLOCK
    nb = p_rows // m
    de = wg.shape[-1]

    def xmap(j, be, nu, va, ne):
        return (jnp.minimum(j, nu[0] - 1), 0)

    any_spec = pl.BlockSpec(memory_space=pl.ANY)
    grid_spec = pltpu.PrefetchScalarGridSpec(
        num_scalar_prefetch=4,
        grid=(nb,),
        in_specs=[pl.BlockSpec((m, dh + LANES), xmap), any_spec, any_spec, any_spec],
        out_specs=pl.BlockSpec((m, dh), lambda j, be, nu, va, ne: (j, 0)),
        scratch_shapes=[
            pltpu.VMEM((d, de), BF16), pltpu.VMEM((d, de), BF16), pltpu.VMEM((de, d), BF16),
            pltpu.VMEM((2, d, de), F32), pltpu.VMEM((2, d, de), F32), pltpu.VMEM((2, de, d), F32),
            pltpu.SMEM((1,), I32), pltpu.SemaphoreType.DMA((2, 3)),
        ],
    )
    return pl.pallas_call(
        functools.partial(_expert_kernel, layer=layer),
        grid_spec=grid_spec,
        out_shape=jax.ShapeDtypeStruct((p_rows, dh), U32),
        compiler_params=_cparams(("arbitrary",)),
        name="experts",
    )(block_e, nused, valid, next_e, xb, wg, wu, wd)


def _combine_kernel(*refs, tile0, final):
    if final:
        seg_ref, pad_ref, yb_hbm, io_ref, x_ref, g2_ref, fn_ref, o_ref, stage, sem = refs
    else:
        seg_ref, pad_ref, yb_hbm, io_ref, x_ref, g2_ref, o_ref, stage, sem = refs
    i = pl.program_id(0)

    slot = i % 2

    @pl.when(i == 0)
    def _():
        stage[...] = jnp.zeros_like(stage)
        _segment_copies(seg_ref, pad_ref, tile0, stage.at[0], yb_hbm, sem.at[0], to_hbm=False, wait=False)

    @pl.when(i + 1 < pl.num_programs(0))
    def _():
        _segment_copies(seg_ref, pad_ref, tile0 + i + 1, stage.at[1 - slot], yb_hbm, sem.at[1 - slot],
                        to_hbm=False, wait=False)

    _segment_copies(seg_ref, pad_ref, tile0 + i, stage.at[slot], yb_hbm, sem.at[slot], to_hbm=False, wait=True)
    rows = _unpack_bf16_pairs(stage[slot])
    m0, m1 = _sort_matrix(io_ref, STAGE_ROWS)
    pick = jnp.where(m0 | m1, 1.0, 0.0).astype(BF16)
    y = lax.dot_general(pick, rows, (((0,), (0,)), ((), ())), preferred_element_type=F32)
    xo = x_ref[...] + g2_ref[...] * y
    if final:
        ms = jnp.mean(xo * xo, axis=-1, keepdims=True)
        xo = (xo * lax.rsqrt(ms + NORM_EPS)) * fn_ref[...]
    o_ref[...] = xo


def _combine_call(seg, pad, yb, io, x2d, mods5, layer, row_of_tile, final_g, tile0):
    n, d = x2d.shape
    tm = ROUTE_TILE
    final = final_g is not None
    in_specs = [
        pl.BlockSpec(memory_space=pl.ANY),
        pl.BlockSpec((SUBLANES, tm), lambda t, *_: (0, tile0 + t)),
        pl.BlockSpec((tm, d), lambda t, *_: (t, 0)),
        pl.BlockSpec((None, None, None, 1, d), lambda t, *_: (layer, row_of_tile(t), 5, 0, 0)),
    ]
    args = [yb, io, x2d, mods5]
    if final:
        in_specs.append(pl.BlockSpec((1, d), lambda t, *_: (0, 0)))
        args.append(final_g.reshape(1, d))
    grid_spec = pltpu.PrefetchScalarGridSpec(
        num_scalar_prefetch=2,
        grid=(n // tm,),
        in_specs=in_specs,
        out_specs=pl.BlockSpec((tm, d), lambda t, *_: (t, 0)),
        scratch_shapes=[pltpu.VMEM((2, STAGE_ROWS, d // 2), U32), pltpu.SemaphoreType.DMA((2,))],
    )
    return pl.pallas_call(
        functools.partial(_combine_kernel, tile0=tile0, final=final),
        grid_spec=grid_spec,
        out_shape=jax.ShapeDtypeStruct((n, d), F32),
        compiler_params=_cparams(("arbitrary",)),
        name="combine",
    )(seg, pad, *args)


def _pack_w_in(w):
    d = w.shape[0]
    pts, acc = [], 0
    for s in IN_SIZES[:-1]:
        acc += s
        pts.append(acc)
    wq, wk, wv, wg, wzf, wzb, wsq, wsk, wsv = jnp.split(w, pts, axis=-1)
    zpad = jnp.zeros((d, LANES - 2 * GLA_GATE_RANK), w.dtype)
    return jnp.concatenate([wq, wk, wv, wg, wzf, wzb, zpad, wsq, wsk, wsv], axis=-1).astype(BF16)


def _pad_up(up, row0):
    up = up.reshape(GLA_GATE_RANK, GLA_HEADS // 2, LANES).transpose(1, 0, 2)
    hi = up.astype(BF16)
    lo = (up - hi.astype(F32)).astype(BF16)
    out = jnp.zeros((GLA_HEADS // 2, LANES, LANES), BF16)
    for group, part in enumerate((hi, hi, lo)):
        r = group * 2 * GLA_GATE_RANK + row0
        out = out.at[:, r:r + GLA_GATE_RANK, :].set(part)
    return out


def _rope_tables(seq):
    rows = seq // GRID_W
    row = jnp.repeat(jnp.arange(rows, dtype=F32), GRID_W)
    col = jnp.tile(jnp.arange(GRID_W, dtype=F32), rows)
    n_freq = ROPE_AXIS_DIM // 2
    inv = ROPE_THETA ** (-(jnp.arange(n_freq, dtype=F32) * 2.0 / ROPE_AXIS_DIM))
    ang_r = row[:, None] * inv[None, :]
    ang_c = col[:, None] * inv[None, :]
    cr, sr, cc, sc = jnp.cos(ang_r), jnp.sin(ang_r), jnp.cos(ang_c), jnp.sin(ang_c)
    cos64 = jnp.concatenate([cr, cr, cc, cc], axis=-1)
    sin64 = jnp.concatenate([-sr, sr, -sc, sc], axis=-1)
    reps = LANES // SWA_HD
    return jnp.tile(cos64, (1, reps)), jnp.tile(sin64, (1, reps))


def _tile(n, pref):
    t = pref
    while n % t:
        t //= 2
    return t


def kernel(x, c, ctx, c_ctx, w_ada, b_ada, norm1, norm2, w_in, gla_up_f, gla_bias_f, gla_up_b, gla_bias_b,
           gla_norm, swa_sink, w_out, w_router, b_router, w_gate, w_up, w_down, final_norm):
    batch, seq, d = x.shape
    lc = ctx.shape[1]
    depth = w_ada.shape[0]
    n_l = batch * seq
    n_c = batch * lc
    npair = GLA_HEADS // 2
    assert batch + 1 <= SUBLANES
    assert n_l % ROUTE_TILE == 0 and n_c % ROUTE_TILE == 0

    tm_l = _tile(seq, ROW_TILE)
    tm_in = _tile(seq, INPROJ_TILE)
    tm_c = _tile(lc, CTX_TILE)
    tl_l = _tile(seq, GLA_TILE)
    tl_c = _tile(lc, ROW_TILE)

    rows = jnp.zeros((SUBLANES, d), F32).at[:batch].set(c).at[batch].set(c_ctx)
    mods = _ada_call(rows, w_ada, b_ada)
    mods5 = mods.reshape(depth, SUBLANES, 6, 1, d)

    tabs = _rope_tables(seq)
    wr_pad = jnp.zeros((d, LANES), F32).at[:, :N_EXPERTS].set(w_router)
    wr_hi = wr_pad.astype(BF16)
    wr_lo = (wr_pad - wr_hi.astype(F32)).astype(BF16)
    wr_cat = jnp.concatenate([wr_hi, wr_lo], axis=-1)

    def lat_row(tm):
        return lambda t: t // (seq // tm)

    def ctx_row(tm):
        return lambda t: batch

    xl = x.reshape(n_l, d)
    xc = ctx.reshape(n_c, d)
    for i in range(depth):
        last = i == depth - 1
        w_packed = _pack_w_in(w_in[i])
        upf = _pad_up(gla_up_f[i], 0)
        upb = _pad_up(gla_up_b[i], GLA_GATE_RANK)
        bias_f = gla_bias_f[i].reshape(npair, 1, LANES)
        bias_b = gla_bias_b[i].reshape(npair, 1, LANES)
        w_out_b = w_out[i].astype(BF16)

        c_qk, c_v, c_g, c_z, c_sq, c_sk, c_sv = _inproj_call(
            xc, norm1[i], mods5, i, ctx_row(tm_c), w_packed, None, tm_c, lc // tm_c)
        l_qk, l_v, l_g, l_z, l_sq, l_sk, l_sv = _inproj_call(
            xl, norm1[i], mods5, i, lat_row(tm_in), w_packed, tabs, tm_in, seq // tm_in)

        s_zero = jnp.zeros((batch, npair, 2, 2 * GLA_DV, LANES), F32)
        oc_f, oc_b, s_ctx = _gla_call(c_qk, c_v, c_z, upf, upb, bias_f, bias_b, s_zero, batch, lc, tl_c)
        ol_f, ol_b, _ = _gla_call(l_qk, l_v, l_z, upf, upb, bias_f, bias_b, s_ctx, batch, seq, tl_l)

        swa_l = _swa_call(l_sq, l_sk, l_sv, c_sk, c_sv, swa_sink[i], batch, seq, lc)
        xl, h2l, lg_l = _outproj_call(ol_f, ol_b, l_g, gla_norm[i], swa_l, w_out_b, xl, mods5, i, lat_row(tm_l),
                                      norm2[i], wr_cat, tm_l)
        if last:
            logits_t = lg_l
            t_all = n_l
        else:
            swa_c = _swa_ctx_call(c_sq, c_sk, c_sv, swa_sink[i], batch, lc)
            xc, h2c, lg_c = _outproj_call(oc_f, oc_b, c_g, gla_norm[i], swa_c, w_out_b, xc, mods5, i, ctx_row(tm_c),
                                          norm2[i], wr_cat, tm_c)
            logits_t = jnp.concatenate([lg_l, lg_c], axis=1)
            t_all = n_l + n_c

        io, wo, seg, pad, blk = _route_call(logits_t, b_router, ROUTE_TILE)
        m = MOE_BLOCK
        padded = blk[:, 0]
        pend = jnp.cumsum(padded)
        ntile = t_all // ROUTE_TILE
        nb = -(-(t_all * TOP_K + ntile * N_EXPERTS * (SUBLANES - 1)) // m) + N_EXPERTS
        p_rows = nb * m
        seg = seg.reshape(-1)
        pad = pad.reshape(-1)
        nused = (pend[-1:] // m).astype(I32)
        tail = nused + jnp.arange(nb - (t_all * TOP_K) // m, dtype=I32)
        zblk = jnp.concatenate([jnp.where(padded > 0, pend // m - 1, -1),
                                jnp.where(tail < nb, tail, -1)]).astype(I32)
        blk_row = jnp.minimum(jnp.arange(nb, dtype=I32), nused - 1) * m
        block_e = jnp.sum((pend[None, :] <= blk_row[:, None]).astype(I32), axis=1)
        rows_e = jnp.sum(pad.reshape(N_EXPERTS, LANES), axis=1)
        own = (block_e[:, None] == jnp.arange(N_EXPERTS, dtype=I32)[None, :]).astype(I32)
        seg_end = jnp.sum(own * (pend - padded + rows_e)[None, :], axis=1)
        blk_idx = jnp.arange(nb, dtype=I32)
        valid = jnp.where(blk_idx < nused, jnp.clip(seg_end - blk_idx * m, 0, m), 0).astype(I32)

        sources = (h2l,) if last else (h2l, h2c)
        xb = _dispatch_call(seg, pad, zblk, io, wo, sources, p_rows)
        eid = jnp.arange(N_EXPERTS, dtype=I32)
        later = (eid[None, :] > eid[:, None]) & (padded[None, :] > 0)
        next_e = jnp.min(jnp.where(later, eid[None, :], N_EXPERTS), axis=1)
        next_e = jnp.where(next_e < N_EXPERTS, next_e, -1).astype(I32)
        yb = _expert_call(block_e, nused, valid, next_e, xb, w_gate, w_up, w_down, i)
        xl_new = _combine_call(seg, pad, yb, io, xl, mods5, i, lat_row(ROUTE_TILE),
                               final_norm if last else None, 0)
        if not last:
            xc = _combine_call(seg, pad, yb, io, xc, mods5, i, ctx_row(0), None, n_l // ROUTE_TILE)
        xl = xl_new
    return xl.reshape(batch, seq, d)
```

```python
import functools

import jax
import jax.numpy as jnp
from jax import lax
from jax.experimental import pallas as pl
from jax.experimental.pallas import tpu as pltpu

F32 = jnp.float32
BF16 = jnp.bfloat16
I32 = jnp.int32
U32 = jnp.uint32
HIGHEST = lax.Precision.HIGHEST

GRID_W = 64
NORM_EPS = 1e-6
GLA_HEADS = 4
GLA_DK = 64
GLA_DV = 128
GLA_KW = GLA_HEADS * GLA_DK
GLA_WIDTH = GLA_HEADS * GLA_DV
GLA_GATE_RANK = 16
GLA_TAU = 16.0
GLA_CHUNK = 64
SWA_HEADS = 8
SWA_KV_HEADS = 2
SWA_HD = 64
SWA_WIDTH = SWA_HEADS * SWA_HD
SWA_KVW = SWA_KV_HEADS * SWA_HD
WINDOW = 128
ATT_BLOCK = 128
ROPE_THETA = 10000.0
ROPE_AXIS_DIM = SWA_HD // 2
ROPE_HALF = ROPE_AXIS_DIM // 2
N_EXPERTS = 16
N_GROUPS = 4
EXPERTS_PER_GROUP = N_EXPERTS // N_GROUPS
TOP_K = 2
D_EXPERT = 512
IN_SIZES = (GLA_KW, GLA_KW, GLA_WIDTH, GLA_WIDTH, GLA_GATE_RANK, GLA_GATE_RANK, SWA_WIDTH, SWA_KVW, SWA_KVW)

LANES = 128
SUBLANES = 8
VMEM_LIMIT = 48 * 1024 * 1024

ROW_TILE = 512
GLA_TILE = 1024
INPROJ_TILE = 1024
CTX_TILE = 256
ADA_COLS = 1536
SWA_BLOCKS_PER_STEP = 4
MOE_BLOCK = 512
ROUTE_TILE = 512
ROUTE_TILES_PER_STEP = 4
SEG_SIZES = (512, 256, 128, 64, 32, 16, 8)
STAGE_ROWS = TOP_K * ROUTE_TILE + LANES

C_QK = 0
C_V = 512
C_G = 1024
C_Z = 1536
C_SQ = 1664
C_SK = 2176
C_SV = 2304
C_END = 2432
SV_LANES = 4 * LANES
NEG_BIG = -1e30


def _cparams(sem):
    return pltpu.CompilerParams(dimension_semantics=sem, vmem_limit_bytes=VMEM_LIMIT)


def _pack_bf16_pairs(x):
    return _pack_rounded_pairs(x.astype(BF16).astype(F32))


def _pack_rounded_pairs(xr):
    n = xr.shape[1] // 2
    lo = lax.bitcast_convert_type(xr[:, :n], U32)
    hi = lax.bitcast_convert_type(xr[:, n:], U32)
    return (lo >> 16) | hi


def _unpack_bf16_pairs(w):
    lo = lax.bitcast_convert_type(w << 16, F32)
    hi = lax.bitcast_convert_type(w & jnp.uint32(0xFFFF0000), F32)
    return jnp.concatenate([lo.astype(BF16), hi.astype(BF16)], axis=-1)


def _ada_kernel(a_ref, w_ref, b_ref, o_ref):
    a = a_ref[...]
    act = a * jax.nn.sigmoid(a)
    o_ref[...] = jnp.dot(act, w_ref[...], preferred_element_type=F32, precision=HIGHEST) + b_ref[...]


def _ada_call(rows, w_ada, b_ada):
    depth, d, n6 = w_ada.shape
    tn = ADA_COLS
    return pl.pallas_call(
        _ada_kernel,
        grid=(depth, n6 // tn),
        in_specs=[
            pl.BlockSpec((SUBLANES, d), lambda l, j: (0, 0)),
            pl.BlockSpec((None, d, tn), lambda l, j: (l, 0, j)),
            pl.BlockSpec((None, 1, tn), lambda l, j: (l, 0, j)),
        ],
        out_specs=pl.BlockSpec((None, SUBLANES, tn), lambda l, j: (l, 0, j)),
        out_shape=jax.ShapeDtypeStruct((depth, SUBLANES, n6), F32),
        compiler_params=_cparams(("arbitrary", "arbitrary")),
        name="adaln",
    )(rows, w_ada, b_ada.reshape(depth, 1, n6))


def _rope_tile(xj, cos, sin, lane_lo):
    partner = jnp.where(lane_lo, pltpu.roll(xj, LANES - ROPE_HALF, 1), pltpu.roll(xj, ROPE_HALF, 1))
    return xj * cos + partner * sin


def _inproj_kernel(*refs, rope):
    if rope:
        (x_ref, g_ref, sc_ref, sh_ref, w_ref, cos_ref, sin_ref,
         qk_ref, v_ref, gate_ref, z_ref, sq_ref, sk_ref, sv_ref) = refs
    else:
        (x_ref, g_ref, sc_ref, sh_ref, w_ref,
         qk_ref, v_ref, gate_ref, z_ref, sq_ref, sk_ref, sv_ref) = refs
    half = x_ref.shape[0] // 2
    scale = g_ref[...] * (1.0 + sc_ref[...])
    normed = []
    for r in range(2):
        x = x_ref[r * half:(r + 1) * half, :]
        ms = jnp.mean(x * x, axis=-1, keepdims=True)
        normed.append(((x * lax.rsqrt(ms + NORM_EPS)) * scale + sh_ref[...]).astype(BF16))

    for r in range(2):
        rows = slice(r * half, (r + 1) * half)
        hb = normed[r]

        def proj(a, b):
            return jnp.dot(hb, w_ref[:, a:b], preferred_element_type=F32)

        qk = proj(C_QK, C_V)
        qk_ref[rows, :GLA_KW] = (qk[:, :GLA_KW] * (GLA_DK ** -0.5)).astype(qk_ref.dtype)
        qk_ref[rows, GLA_KW:] = qk[:, GLA_KW:].astype(qk_ref.dtype)
        v_ref[rows, :] = proj(C_V, C_G).astype(v_ref.dtype)
        gate_ref[rows, :] = proj(C_G, C_Z).astype(gate_ref.dtype)
        z_ref[rows, :] = proj(C_Z, C_SQ)
        sq = proj(C_SQ, C_SK) * (SWA_HD ** -0.5)
        sk = proj(C_SK, C_SV)
        sv = proj(C_SV, C_END)
        if rope:
            cos = cos_ref[rows, :]
            sin = sin_ref[rows, :]
            lane_lo = (lax.broadcasted_iota(I32, cos.shape, 1) % ROPE_AXIS_DIM) < ROPE_HALF
            for j in range(SWA_WIDTH // LANES):
                sl = slice(j * LANES, (j + 1) * LANES)
                sq_ref[rows, sl] = _rope_tile(sq[:, sl], cos, sin, lane_lo).astype(sq_ref.dtype)
            sk = _rope_tile(sk, cos, sin, lane_lo)
        else:
            sq_ref[rows, :] = sq.astype(sq_ref.dtype)
        first = lax.broadcasted_iota(I32, sk.shape, 1) < SWA_HD
        sk_sw = pltpu.roll(sk, SWA_HD, 1)
        sk_ref[rows, :LANES] = jnp.where(first, sk, sk_sw).astype(sk_ref.dtype)
        sk_ref[rows, LANES:] = jnp.where(first, sk_sw, sk).astype(sk_ref.dtype)
        sv_sw = pltpu.roll(sv, SWA_HD, 1)
        sv_ref[rows, 0 * LANES:1 * LANES] = jnp.where(first, sv, 1.0).astype(sv_ref.dtype)
        sv_ref[rows, 1 * LANES:2 * LANES] = jnp.where(first, 1.0, sv_sw).astype(sv_ref.dtype)
        sv_ref[rows, 2 * LANES:3 * LANES] = jnp.where(first, sv_sw, 1.0).astype(sv_ref.dtype)
        sv_ref[rows, 3 * LANES:4 * LANES] = jnp.where(first, 1.0, sv).astype(sv_ref.dtype)


def _inproj_call(x2d, g, mods5, layer, row_of_tile, w_packed, tabs, tm, tiles_per_seq):
    n, d = x2d.shape
    nt = n // tm
    rope = tabs is not None

    def mod_spec(k):
        return pl.BlockSpec((None, None, None, 1, d), lambda t: (layer, row_of_tile(t), k, 0, 0))

    in_specs = [
        pl.BlockSpec((tm, d), lambda t: (t, 0)),
        pl.BlockSpec((1, d), lambda t: (0, 0)),
        mod_spec(1), mod_spec(0),
        pl.BlockSpec((None, d, C_END), lambda t: (layer, 0, 0)),
    ]
    args = [x2d, g.reshape(1, d), mods5, mods5, w_packed]
    if rope:
        in_specs += [pl.BlockSpec((tm, LANES), lambda t: (t % tiles_per_seq, 0))] * 2
        args += [tabs[0], tabs[1]]
    widths = (2 * GLA_KW, GLA_WIDTH, GLA_WIDTH, LANES, SWA_WIDTH, 2 * SWA_KVW, SV_LANES)
    dtypes = (BF16, BF16, BF16, F32, BF16, BF16, BF16)
    return pl.pallas_call(
        functools.partial(_inproj_kernel, rope=rope),
        grid=(nt,),
        in_specs=in_specs,
        out_specs=[pl.BlockSpec((tm, w), lambda t: (t, 0)) for w in widths],
        out_shape=[jax.ShapeDtypeStruct((n, w), dt) for w, dt in zip(widths, dtypes)],
        compiler_params=_cparams(("arbitrary",)),
        name="inproj",
    )(*args)


def _log_sigmoid(x):
    return jnp.minimum(x, 0.0) - jnp.log(1.0 + jnp.exp(-jnp.abs(x)))


def _gla_prepare(q_ref, k_ref, v_ref, z_ref, up_ref, bias_ref, tri_ref, scr, fwd):
    qin_s, km0_s, km1_s, kout_s, vm0_s, vm1_s, dec_s = scr[:7]
    qbd_s = scr[9]
    tl = q_ref.shape[0]
    c = GLA_CHUNK
    z = z_ref[...]
    z_hi = z.astype(BF16).astype(F32)
    zc = z_hi + pltpu.roll(z - z_hi, 2 * GLA_GATE_RANK, 1) + pltpu.roll(z_hi, 4 * GLA_GATE_RANK, 1)
    x = jnp.dot(zc.astype(BF16), up_ref[...], preferred_element_type=F32) + bias_ref[...]
    la = _log_sigmoid(x) * (1.0 / GLA_TAU)
    hi = la.astype(BF16)
    lo = (la - hi.astype(F32)).astype(BF16)
    sums = jnp.dot(tri_ref[...], jnp.concatenate([hi, lo], axis=1), preferred_element_type=F32)
    b = sums[:, :LANES] + sums[:, LANES:]
    b3 = b.reshape(tl // c, c, LANES)
    edge = b3[:, c - 1:c, :] if fwd else b3[:, 0:1, :]
    btot = jnp.broadcast_to(edge, b3.shape).reshape(tl, LANES)
    q = q_ref[...].astype(F32)
    k = k_ref[...].astype(F32)
    first = (lax.broadcasted_iota(I32, (tl, LANES), 0) % (2 * c)) < c
    q_in = q * jnp.exp(b)
    qin_s[...] = q_in.astype(BF16)
    qbd_s[...] = jnp.concatenate([jnp.where(first, q_in, 0.0), jnp.where(first, 0.0, q_in)], axis=1).astype(BF16)
    k_in = k * jnp.exp(-b)
    head0 = lax.broadcasted_iota(I32, (tl, LANES), 1) < GLA_DK
    km0_s[...] = jnp.where(head0, k_in, 0.0).astype(BF16)
    km1_s[...] = jnp.where(head0, 0.0, k_in).astype(BF16)
    k_out = k * jnp.exp(btot - b)
    kout_s[...] = jnp.concatenate([jnp.where(first, k_out, 0.0), jnp.where(first, 0.0, k_out)], axis=1).astype(BF16)
    dec_s[...] = jnp.exp(btot)
    vf = v_ref[...].astype(F32)
    vhead0 = lax.broadcasted_iota(I32, (tl, 2 * GLA_DV), 1) < GLA_DV
    vm0_s[...] = jnp.where(vhead0, vf, 0.0).astype(BF16)
    vm1_s[...] = jnp.where(vhead0, 0.0, vf).astype(BF16)


def _gla_increment(v_ref, scr, pair):
    kout_s, u_s = scr[3], scr[7]
    rows = pl.ds(pair * 2 * GLA_CHUNK, 2 * GLA_CHUNK)
    u_t = lax.dot_general(v_ref[rows, :], kout_s[rows, :], (((0,), (0,)), ((), ())),
                          preferred_element_type=F32)
    srow = lax.broadcasted_iota(I32, (2 * GLA_DV, 2 * LANES), 0) // GLA_DV
    scol = (lax.broadcasted_iota(I32, (2 * GLA_DV, 2 * LANES), 1) % LANES) // GLA_DK
    u_s[pair] = jnp.where(srow == scol, u_t, 0.0)


def _gla_states(scr, s_ref, chunk_order):
    dec_s, u_s, sprev_s = scr[6], scr[7], scr[8]
    s_t = s_ref[...]
    for cidx in chunk_order:
        pair, lanes = cidx // 2, pl.ds((cidx % 2) * LANES, LANES)
        sprev_s[pair, :, lanes] = s_t.astype(BF16)
        r0 = cidx * GLA_CHUNK
        s_t = s_t * dec_s[r0:r0 + 1, :] + u_s[pair, :, lanes]
    s_ref[...] = s_t


def _gla_output(o_ref, scr, pair, fwd):
    qin_s, km0_s, km1_s, _, vm0_s, vm1_s, _, _, sprev_s, qbd_s = scr
    c = GLA_CHUNK
    r0 = pair * 2 * c
    rows = pl.ds(r0, 2 * c)
    ca, cb = pl.ds(r0, c), pl.ds(r0 + c, c)
    kst = jnp.concatenate([km0_s[ca, :], km1_s[ca, :], km0_s[cb, :], km1_s[cb, :]], axis=0)
    a = lax.dot_general(qin_s[rows, :], kst, (((1,), (1,)), ((), ())), preferred_element_type=F32)
    ri = lax.broadcasted_iota(I32, (2 * c, 4 * c), 0)
    ci = lax.broadcasted_iota(I32, (2 * c, 4 * c), 1)
    same_chunk = (ri // c) == (ci // (2 * c))
    keep = same_chunk & ((ci % c <= ri % c) if fwd else (ci % c >= ri % c))
    a = jnp.where(keep, a, 0.0).astype(BF16)
    vbd = jnp.concatenate([vm0_s[ca, :], vm1_s[ca, :], vm0_s[cb, :], vm1_s[cb, :]], axis=0)
    o = jnp.dot(a, vbd, preferred_element_type=F32)
    o = o + lax.dot_general(qbd_s[rows, :], sprev_s[pair], (((1,), (1,)), ((), ())), preferred_element_type=F32)
    o_ref[rows, :] = o.astype(o_ref.dtype)


def _gla_kernel(qf_ref, kf_ref, vf_ref, zf_ref, qb_ref, kb_ref, vb_ref, zb_ref,
                upf_ref, upb_ref, bf_ref, bb_ref, trif_ref, trib_ref, s0_ref,
                of_ref, ob_ref, sfin_ref, sf_scr, sb_scr, *scr, nchunk):
    i = pl.program_id(2)
    nt = pl.num_programs(2)
    scr_f, scr_b = scr[:len(scr) // 2], scr[len(scr) // 2:]

    @pl.when(i == 0)
    def _():
        sf_scr[...] = s0_ref[0]
        sb_scr[...] = s0_ref[1]

    _gla_prepare(qf_ref, kf_ref, vf_ref, zf_ref, upf_ref, bf_ref, trif_ref, scr_f, True)
    _gla_prepare(qb_ref, kb_ref, vb_ref, zb_ref, upb_ref, bb_ref, trib_ref, scr_b, False)
    for pair in range(nchunk // 2):
        _gla_increment(vf_ref, scr_f, pair)
        _gla_increment(vb_ref, scr_b, pair)
    _gla_states(scr_f, sf_scr, range(nchunk))
    _gla_states(scr_b, sb_scr, range(nchunk - 1, -1, -1))
    for pair in range(nchunk // 2):
        _gla_output(of_ref, scr_f, pair, True)
        _gla_output(ob_ref, scr_b, pair, False)

    @pl.when(i == nt - 1)
    def _():
        sfin_ref[0] = sf_scr[...]
        sfin_ref[1] = sb_scr[...]


def _gla_call(qk, v, z, upf_pad, upb_pad, bias_f, bias_b, s0, batch, seq, tl):
    n = qk.shape[0]
    nt = seq // tl
    npair = GLA_HEADS // 2

    def fwd_row(b, p, i):
        return b * nt + i

    def bwd_row(b, p, i):
        return b * nt + (nt - 1 - i)

    def specs(row):
        return [
            pl.BlockSpec((tl, LANES), lambda b, p, i: (row(b, p, i), p)),
            pl.BlockSpec((tl, LANES), lambda b, p, i: (row(b, p, i), npair + p)),
            pl.BlockSpec((tl, 2 * GLA_DV), lambda b, p, i: (row(b, p, i), p)),
            pl.BlockSpec((tl, LANES), lambda b, p, i: (row(b, p, i), 0)),
        ]

    in_specs = specs(fwd_row) + specs(bwd_row) + [
        pl.BlockSpec((None, LANES, LANES), lambda b, p, i: (p, 0, 0)),
        pl.BlockSpec((None, LANES, LANES), lambda b, p, i: (p, 0, 0)),
        pl.BlockSpec((None, 1, LANES), lambda b, p, i: (p, 0, 0)),
        pl.BlockSpec((None, 1, LANES), lambda b, p, i: (p, 0, 0)),
        pl.BlockSpec((None, tl, tl), lambda b, p, i: (0, 0, 0)),
        pl.BlockSpec((None, tl, tl), lambda b, p, i: (1, 0, 0)),
        pl.BlockSpec((None, None, 2, 2 * GLA_DV, LANES), lambda b, p, i: (b, p, 0, 0, 0)),
    ]
    ri = jnp.arange(tl)[:, None]
    ci = jnp.arange(tl)[None, :]
    same = (ri // GLA_CHUNK) == (ci // GLA_CHUNK)
    tri = jnp.stack([same & (ci <= ri), same & (ci >= ri)]).astype(BF16)
    npairs = tl // (2 * GLA_CHUNK)
    dir_scratch = [
        pltpu.VMEM((tl, LANES), BF16),
        pltpu.VMEM((tl, LANES), BF16),
        pltpu.VMEM((tl, LANES), BF16),
        pltpu.VMEM((tl, 2 * LANES), BF16),
        pltpu.VMEM((tl, 2 * GLA_DV), BF16),
        pltpu.VMEM((tl, 2 * GLA_DV), BF16),
        pltpu.VMEM((tl, LANES), F32),
        pltpu.VMEM((npairs, 2 * GLA_DV, 2 * LANES), F32),
        pltpu.VMEM((npairs, 2 * GLA_DV, 2 * LANES), BF16),
        pltpu.VMEM((tl, 2 * LANES), BF16),
    ]
    out_specs = [
        pl.BlockSpec((tl, 2 * GLA_DV), lambda b, p, i: (fwd_row(b, p, i), p)),
        pl.BlockSpec((tl, 2 * GLA_DV), lambda b, p, i: (bwd_row(b, p, i), p)),
        pl.BlockSpec((None, None, 2, 2 * GLA_DV, LANES), lambda b, p, i: (b, p, 0, 0, 0)),
    ]
    out_shape = [
        jax.ShapeDtypeStruct((n, GLA_WIDTH), BF16),
        jax.ShapeDtypeStruct((n, GLA_WIDTH), BF16),
        jax.ShapeDtypeStruct((batch, npair, 2, 2 * GLA_DV, LANES), F32),
    ]
    return pl.pallas_call(
        functools.partial(_gla_kernel, nchunk=tl // GLA_CHUNK),
        grid=(batch, npair, nt),
        in_specs=in_specs,
        out_specs=out_specs,
        out_shape=out_shape,
        scratch_shapes=[pltpu.VMEM((2 * GLA_DV, LANES), F32), pltpu.VMEM((2 * GLA_DV, LANES), F32)]
        + dir_scratch + dir_scratch,
        compiler_params=_cparams(("arbitrary", "arbitrary", "arbitrary")),
        name="gla",
    )(qk, qk, v, z, qk, qk, v, z, upf_pad, upb_pad, bias_f, bias_b, tri, tri, s0)


def _swa_block(sink_ref, q_ref, o_ref, row0, kall, vall, prev_ok, next_ok):
    w = ATT_BLOCK
    local = prev_ok is not None
    qrows = pl.ds(row0, w)
    r2 = lax.broadcasted_iota(I32, (2 * w, w), 0) % w
    c2 = lax.broadcasted_iota(I32, (2 * w, w), 1)
    if local:
        bias_prev = jnp.where((c2 >= r2) & prev_ok, 0.0, NEG_BIG)
        bias_next = jnp.where((c2 <= r2) & next_ok, 0.0, NEG_BIG)
    first = c2 < SWA_HD
    top = lax.broadcasted_iota(I32, (2 * w, 1), 0) < w
    scores = []
    for g in range(SWA_KV_HEADS):
        kd = kall[:, g * LANES:(g + 1) * LANES]
        qs = jnp.concatenate([q_ref[qrows, (2 * g) * LANES:(2 * g + 1) * LANES],
                              q_ref[qrows, (2 * g + 1) * LANES:(2 * g + 2) * LANES]], axis=0).astype(F32)
        for half in range(2):
            qm = (jnp.where(first, qs, 0.0) if half == 0 else jnp.where(first, 0.0, qs)).astype(BF16)
            s = lax.dot_general(qm, kd, (((1,), (1,)), ((), ())), preferred_element_type=F32)
            if local:
                s = jnp.concatenate([s[:, :w] + bias_prev, s[:, w:2 * w], s[:, 2 * w:3 * w] + bias_next,
                                     s[:, 3 * w:]], axis=1)
            scores.append(s)
    for g in range(SWA_KV_HEADS):
        outs = []
        for half in range(2):
            s = scores[2 * g + half]
            sk = jnp.where(top, sink_ref[4 * g + half], sink_ref[4 * g + 2 + half])
            m = jnp.maximum(jnp.max(s, axis=-1, keepdims=True), sk)
            p = jnp.exp((s - m).astype(BF16))
            va = vall[:, (2 * g + half) * LANES:(2 * g + half + 1) * LANES]
            acc = jnp.dot(p, va, preferred_element_type=F32)
            den = pltpu.roll(acc, SWA_HD, 1) + jnp.exp(sk - m)
            outs.append(acc / den)
        out = jnp.where(first, outs[0], outs[1])
        o_ref[qrows, (2 * g) * LANES:(2 * g + 1) * LANES] = out[:w].astype(o_ref.dtype)
        o_ref[qrows, (2 * g + 1) * LANES:(2 * g + 2) * LANES] = out[w:].astype(o_ref.dtype)


def _swa_kernel(*refs, nstep, local):
    w = ATT_BLOCK
    if not local:
        sink_ref, q_ref, kx_ref, vx_ref, o_ref = refs
        _swa_block(sink_ref, q_ref, o_ref, 0, kx_ref[...], vx_ref[...], None, None)
        return
    sink_ref, q_ref, kp_ref, kc_ref, kn_ref, vp_ref, vc_ref, vn_ref, kx_ref, vx_ref, o_ref = refs
    i = pl.program_id(1)
    nblk = q_ref.shape[0] // w
    k_blocks = [kp_ref[...]] + [kc_ref[j * w:(j + 1) * w, :] for j in range(nblk)] + [kn_ref[...]]
    v_blocks = [vp_ref[...]] + [vc_ref[j * w:(j + 1) * w, :] for j in range(nblk)] + [vn_ref[...]]
    exists = [i > 0] + [True] * nblk + [i < nstep - 1]
    for sub in range(nblk):
        kall = jnp.concatenate(k_blocks[sub:sub + 3] + [kx_ref[...]], axis=0)
        vall = jnp.concatenate(v_blocks[sub:sub + 3] + [vx_ref[...]], axis=0)
        _swa_block(sink_ref, q_ref, o_ref, sub * w, kall, vall, exists[sub], exists[sub + 2])


def _swa_call(sq, skd, svd, kcd, vcd, sink, batch, seq, lc):
    n = sq.shape[0]
    w = ATT_BLOCK
    nb = seq // w
    nblk = SWA_BLOCKS_PER_STEP if nb % SWA_BLOCKS_PER_STEP == 0 else 1
    nstep = nb // nblk
    kvw = 2 * SWA_KVW

    def pair(b, i):
        return (b * nstep + i, 0)

    def before(b, i):
        return (b * nb + jnp.maximum(nblk * i - 1, 0), 0)

    def after(b, i):
        return (b * nb + jnp.minimum(nblk * (i + 1), nb - 1), 0)

    def kv_specs(width):
        return [pl.BlockSpec((w, width), before), pl.BlockSpec((nblk * w, width), pair),
                pl.BlockSpec((w, width), after)]

    def ctx_spec(width):
        return pl.BlockSpec((lc, width), lambda b, i: (b, 0))

    return pl.pallas_call(
        functools.partial(_swa_kernel, nstep=nstep, local=True),
        grid=(batch, nstep),
        in_specs=[pl.BlockSpec(memory_space=pltpu.SMEM), pl.BlockSpec((nblk * w, SWA_WIDTH), pair)]
        + kv_specs(kvw) + kv_specs(SV_LANES) + [ctx_spec(kvw), ctx_spec(SV_LANES)],
        out_specs=pl.BlockSpec((nblk * w, SWA_WIDTH), pair),
        out_shape=jax.ShapeDtypeStruct((n, SWA_WIDTH), BF16),
        compiler_params=_cparams(("arbitrary", "arbitrary")),
        name="swa",
    )(sink, sq, skd, skd, skd, svd, svd, svd, kcd, vcd)


def _swa_ctx_call(sq, kcd, vcd, sink, batch, lc):
    n = sq.shape[0]
    w = ATT_BLOCK
    nb = lc // w
    kvw = 2 * SWA_KVW
    return pl.pallas_call(
        functools.partial(_swa_kernel, nstep=0, local=False),
        grid=(batch, nb),
        in_specs=[pl.BlockSpec(memory_space=pltpu.SMEM),
                  pl.BlockSpec((w, SWA_WIDTH), lambda b, i: (b * nb + i, 0)),
                  pl.BlockSpec((lc, kvw), lambda b, i: (b, 0)),
                  pl.BlockSpec((lc, SV_LANES), lambda b, i: (b, 0))],
        out_specs=pl.BlockSpec((w, SWA_WIDTH), lambda b, i: (b * nb + i, 0)),
        out_shape=jax.ShapeDtypeStruct((n, SWA_WIDTH), BF16),
        compiler_params=_cparams(("arbitrary", "arbitrary")),
        name="swa_ctx",
    )(sink, sq, kcd, vcd)


def _outproj_kernel(of_ref, ob_ref, gate_ref, gn_ref, swa_ref, w_ref, x_ref, g1_ref, n2_ref, sc_ref, sh_ref,
                    wr_ref, xo_ref, h2_ref, lg_ref):
    o = of_ref[...].astype(F32) + ob_ref[...].astype(F32)
    parts = []
    for h in range(GLA_HEADS):
        oh = o[:, h * GLA_DV:(h + 1) * GLA_DV]
        ms = jnp.mean(oh * oh, axis=-1, keepdims=True)
        parts.append(oh * lax.rsqrt(ms + NORM_EPS))
    on = jnp.concatenate(parts, axis=-1) * gn_ref[...]
    gate = gate_ref[...].astype(F32)
    gla = on * (gate * jax.nn.sigmoid(gate))
    mix = jnp.concatenate([gla.astype(BF16), swa_ref[...]], axis=-1)
    y = jnp.dot(mix, w_ref[...], preferred_element_type=F32)
    xo = x_ref[...] + g1_ref[...] * y
    xo_ref[...] = xo
    ms = jnp.mean(xo * xo, axis=-1, keepdims=True)
    h2 = (xo * lax.rsqrt(ms + NORM_EPS)) * (n2_ref[...] * (1.0 + sc_ref[...])) + sh_ref[...]
    hi = h2.astype(BF16)
    hi_f = hi.astype(F32)
    h2_ref[...] = _pack_rounded_pairs(hi_f)
    lo = (h2 - hi_f).astype(BF16)
    both = jnp.dot(hi, wr_ref[...], preferred_element_type=F32)
    lg = both[:, :LANES] + both[:, LANES:] + jnp.dot(lo, wr_ref[:, :LANES], preferred_element_type=F32)
    lg_ref[...] = jnp.transpose(lg)[:N_EXPERTS, :]


def _outproj_call(o_f, o_b, gate, gn, swa, w_out_b, x2d, mods5, layer, row_of_tile, n2, wr_cat, tm):
    n, d = x2d.shape
    nt = n // tm

    def mod_spec(k):
        return pl.BlockSpec((None, None, None, 1, d), lambda t: (layer, row_of_tile(t), k, 0, 0))

    return pl.pallas_call(
        _outproj_kernel,
        grid=(nt,),
        in_specs=[
            pl.BlockSpec((tm, GLA_WIDTH), lambda t: (t, 0)),
            pl.BlockSpec((tm, GLA_WIDTH), lambda t: (t, 0)),
            pl.BlockSpec((tm, GLA_WIDTH), lambda t: (t, 0)),
            pl.BlockSpec((1, GLA_WIDTH), lambda t: (0, 0)),
            pl.BlockSpec((tm, SWA_WIDTH), lambda t: (t, 0)),
            pl.BlockSpec((None, d, d), lambda t: (layer, 0, 0)),
            pl.BlockSpec((tm, d), lambda t: (t, 0)),
            mod_spec(2),
            pl.BlockSpec((1, d), lambda t: (0, 0)),
            mod_spec(4), mod_spec(3),
            pl.BlockSpec((d, 2 * LANES), lambda t: (0, 0)),
        ],
        out_specs=[
            pl.BlockSpec((tm, d), lambda t: (t, 0)),
            pl.BlockSpec((tm, d // 2), lambda t: (t, 0)),
            pl.BlockSpec((N_EXPERTS, tm), lambda t: (0, t)),
        ],
        out_shape=[
            jax.ShapeDtypeStruct((n, d), F32),
            jax.ShapeDtypeStruct((n, d // 2), U32),
            jax.ShapeDtypeStruct((N_EXPERTS, n), F32),
        ],
        compiler_params=_cparams(("arbitrary",)),
        name="outproj",
    )(o_f, o_b, gate, gn.reshape(1, GLA_WIDTH), swa, w_out_b, x2d, mods5, n2.reshape(1, d), mods5, mods5, wr_cat)


def _first_index(vals, target):
    idx = jnp.full(target.shape, len(vals) - 1, I32)
    for i in range(len(vals) - 2, -1, -1):
        idx = jnp.where(vals[i] == target, i, idx)
    return idx


def _route_kernel(lg_ref, br_ref, io_ref, wo_ref, seg_ref, pad_ref, blk_ref, cnt_scr, seg_scr, pad_scr, *, tn):
    phase = pl.program_id(0)
    step = pl.program_id(1)
    group = lg_ref.shape[1] // tn

    @pl.when((phase == 0) & (step == 0))
    def _():
        cnt_scr[...] = jnp.zeros_like(cnt_scr)
        seg_scr[...] = jnp.zeros_like(seg_scr)
        pad_scr[...] = jnp.zeros_like(pad_scr)

    @pl.when((phase == 1) & (step == 0))
    def _():
        pad8 = jnp.floor((cnt_scr[...] + (SUBLANES - 1)) * (1.0 / SUBLANES)) * SUBLANES
        rows_e = jnp.sum(pad8, axis=1, keepdims=True)
        blocks = jnp.floor((rows_e + (MOE_BLOCK - 1)) * (1.0 / MOE_BLOCK)) * MOE_BLOCK
        r128 = lax.broadcasted_iota(I32, (LANES, LANES), 0)
        c128 = lax.broadcasted_iota(I32, (LANES, LANES), 1)
        before_tile = jnp.where(r128 < c128, 1.0, 0.0).astype(BF16)
        seg = jnp.dot((pad8 * (1.0 / SUBLANES)).astype(BF16), before_tile,
                      preferred_element_type=F32) * SUBLANES
        run = jnp.zeros((1, 1), F32)
        for e in range(N_EXPERTS):
            seg_scr[e:e + 1, :] = seg[e:e + 1, :] + run
            run = run + blocks[e:e + 1, :]
        pad_scr[...] = pad8
        seg_ref[...] = seg_scr[...].astype(I32)
        pad_ref[...] = pad8.astype(I32)
        blk_ref[...] = jnp.broadcast_to(blocks, blk_ref.shape).astype(I32)

    for g in range(group):
        lanes = slice(g * tn, (g + 1) * tn)
        _route_tile(lg_ref[:, lanes], br_ref, io_ref, wo_ref, lanes, step * group + g, phase, cnt_scr, pad_scr)


def _route_tile(lg, br_ref, io_ref, wo_ref, lanes, t, phase, cnt_scr, pad_scr):
    tn = lg.shape[1]
    tile_lane = lax.broadcasted_iota(I32, (N_EXPERTS, LANES), 1)
    s = jax.nn.sigmoid(lg)
    sb = s + br_ref[...]
    rows_s = [s[e:e + 1, :] for e in range(N_EXPERTS)]
    rows_b = [sb[e:e + 1, :] for e in range(N_EXPERTS)]
    gscore, gi1, gi2 = [], [], []
    epg = EXPERTS_PER_GROUP
    for g in range(N_GROUPS):
        a = rows_b[g * epg:(g + 1) * epg]
        m1 = functools.reduce(jnp.maximum, a)
        i1 = _first_index(a, m1)
        rest = [jnp.where(i1 == i, -jnp.inf, a[i]) for i in range(epg)]
        m2 = functools.reduce(jnp.maximum, rest)
        i2 = _first_index(rest, m2)
        gscore.append(m1 + m2)
        gi1.append(i1)
        gi2.append(i2)
    gm = functools.reduce(jnp.maximum, gscore)
    gsel = _first_index(gscore, gm)
    i1 = gi1[N_GROUPS - 1]
    i2 = gi2[N_GROUPS - 1]
    for g in range(N_GROUPS - 2, -1, -1):
        i1 = jnp.where(gsel == g, gi1[g], i1)
        i2 = jnp.where(gsel == g, gi2[g], i2)
    idx0 = gsel * epg + i1
    idx1 = gsel * epg + i2
    s0 = jnp.zeros_like(rows_s[0])
    s1 = jnp.zeros_like(rows_s[0])
    for e in range(N_EXPERTS):
        s0 = jnp.where(idx0 == e, rows_s[e], s0)
        s1 = jnp.where(idx1 == e, rows_s[e], s1)
    tot = s0 + s1
    w0 = s0 / tot
    w1 = s1 / tot

    eidx = lax.broadcasted_iota(I32, (N_EXPERTS, tn), 0)
    oh0 = eidx == idx0
    oh1 = eidx == idx1
    oh = jnp.where(oh0 | oh1, 1.0, 0.0)
    rr = lax.broadcasted_iota(I32, (tn, tn), 0)
    cc = lax.broadcasted_iota(I32, (tn, tn), 1)
    upper = jnp.where(rr < cc, 1.0, 0.0).astype(BF16)
    before = jnp.dot(oh.astype(BF16), upper, preferred_element_type=F32)

    @pl.when(phase == 0)
    def _():
        cnt_scr[...] = cnt_scr[...] + jnp.where(tile_lane == t, jnp.sum(oh, axis=1, keepdims=True), 0.0)

    pad_col = jnp.sum(jnp.where(tile_lane == t, pad_scr[...], 0.0), axis=1, keepdims=True)
    run = jnp.zeros((1, 1), F32)
    offs = []
    for e in range(N_EXPERTS):
        offs.append(run)
        run = run + pad_col[e:e + 1, :]
    pos = before + jnp.concatenate(offs, axis=0)
    spos0 = jnp.sum(jnp.where(oh0, pos, 0.0), axis=0, keepdims=True)
    spos1 = jnp.sum(jnp.where(oh1, pos, 0.0), axis=0, keepdims=True)
    zi = jnp.zeros((SUBLANES - 4, tn), I32)
    io_ref[:, lanes] = jnp.concatenate([idx0, idx1, spos0.astype(I32), spos1.astype(I32), zi], axis=0)
    wo_ref[:, lanes] = jnp.concatenate([w0, w1, jnp.zeros((SUBLANES - 2, tn), F32)], axis=0)


def _route_call(logits_t, b_router, tn):
    ne, t_all = logits_t.shape
    ntile = t_all // tn
    assert ntile <= LANES
    group = max(g for g in range(1, ROUTE_TILES_PER_STEP + 1) if ntile % g == 0)
    tw = group * tn
    tok_out = pl.BlockSpec((SUBLANES, tw), lambda p, t: (0, t * p))
    tab_out = pl.BlockSpec((ne, LANES), lambda p, t: (0, 0))
    tab = jax.ShapeDtypeStruct((ne, LANES), I32)
    return pl.pallas_call(
        functools.partial(_route_kernel, tn=tn),
        grid=(2, ntile // group),
        in_specs=[pl.BlockSpec((ne, tw), lambda p, t: (0, t)), pl.BlockSpec((ne, 1), lambda p, t: (0, 0))],
        out_specs=[tok_out, tok_out, tab_out, tab_out, tab_out],
        out_shape=[
            jax.ShapeDtypeStruct((SUBLANES, t_all), I32),
            jax.ShapeDtypeStruct((SUBLANES, t_all), F32),
            tab, tab, tab,
        ],
        scratch_shapes=[pltpu.VMEM((ne, LANES), F32)] * 3,
        compiler_params=_cparams(("arbitrary", "arbitrary")),
        name="route",
    )(logits_t, b_router.reshape(ne, 1))


def _segment_copies(seg_ref, pad_ref, t, stage, hbm, sem, to_hbm, wait):
    if wait:
        total = functools.reduce(lambda a, b: a + b, [pad_ref[e * LANES + t] for e in range(N_EXPERTS)])
        for size in (2 * SEG_SIZES[0],) + SEG_SIZES:
            @pl.when((total & size) != 0)
            def _(size=size):
                src, dst = stage.at[pl.ds(0, size), :], hbm.at[pl.ds(0, size), :]
                if not to_hbm:
                    src, dst = dst, src
                pltpu.make_async_copy(src, dst, sem).wait()
        return
    loc = 0
    for e in range(N_EXPERTS):
        n = pad_ref[e * LANES + t]
        start = seg_ref[e * LANES + t]
        for size in SEG_SIZES:
            off = n & (-2 * size)

            @pl.when((n & size) != 0)
            def _(off=off, size=size, loc=loc, start=start):
                s_rows = pl.ds(pl.multiple_of(loc + off, SUBLANES), size)
                h_rows = pl.ds(pl.multiple_of(start + off, SUBLANES), size)
                src, dst = (stage.at[s_rows, :], hbm.at[h_rows, :])
                if not to_hbm:
                    src, dst = dst, src
                pltpu.make_async_copy(src, dst, sem).start()
        loc = loc + n


def _sort_matrix(io_ref, rows):
    r = lax.broadcasted_iota(I32, (rows, io_ref.shape[1]), 0)
    return r == io_ref[2:3, :], r == io_ref[3:4, :]


def _dispatch_kernel(seg_ref, pad_ref, zblk_ref, *refs, tiles):
    srcs = refs[:len(tiles)]
    io_ref, wo_ref, xb_hbm, stage, zbuf, sem = refs[len(tiles):]
    i = pl.program_id(0)
    m = MOE_BLOCK

    def zero_copy(j):
        start = pl.multiple_of(zblk_ref[j] * m, m)
        return pltpu.make_async_copy(zbuf, xb_hbm.at[pl.ds(start, m), :], sem.at[0])

    @pl.when(i == 0)
    def _():
        zbuf[...] = jnp.zeros_like(zbuf)
        for j in range(zblk_ref.shape[0]):
            @pl.when(zblk_ref[j] >= 0)
            def _():
                zero_copy(j).start()
        for j in range(zblk_ref.shape[0]):
            @pl.when(zblk_ref[j] >= 0)
            def _():
                zero_copy(j).wait()

    hp = srcs[0][...]
    if len(srcs) == 2:
        hp = jnp.where(i < tiles[0], hp, srcs[1][...])
    x = _unpack_bf16_pairs(hp)
    m0, m1 = _sort_matrix(io_ref, STAGE_ROWS)
    sort = jnp.where(m0 | m1, 1.0, 0.0).astype(BF16)
    xs = jnp.dot(sort, x, preferred_element_type=F32)
    dh = x.shape[1] // 2
    slot = i % 2
    cur = stage.at[slot]
    cur[:, :dh] = _pack_bf16_pairs(xs)
    ws = jnp.sum(jnp.where(m0, wo_ref[0:1, :], 0.0) + jnp.where(m1, wo_ref[1:2, :], 0.0), axis=1, keepdims=True)
    cur[:, dh:] = jnp.broadcast_to(lax.bitcast_convert_type(ws, U32), (STAGE_ROWS, LANES))
    _segment_copies(seg_ref, pad_ref, i, cur, xb_hbm, sem.at[slot], to_hbm=True, wait=False)

    @pl.when(i > 0)
    def _():
        _segment_copies(seg_ref, pad_ref, i - 1, stage.at[1 - slot], xb_hbm, sem.at[1 - slot], to_hbm=True, wait=True)

    @pl.when(i == pl.num_programs(0) - 1)
    def _():
        _segment_copies(seg_ref, pad_ref, i, cur, xb_hbm, sem.at[slot], to_hbm=True, wait=True)


def _dispatch_call(seg, pad, zblk, io, wo, sources, p_rows):
    tile = ROUTE_TILE
    dh = sources[0].shape[1]
    tiles = tuple(s.shape[0] // tile for s in sources)
    firsts = tuple(sum(tiles[:k]) for k in range(len(tiles)))

    def src_spec(first, ntile):
        return pl.BlockSpec((tile, dh), lambda i, *_: (jnp.clip(i - first, 0, ntile - 1), 0))

    tok_spec = pl.BlockSpec((SUBLANES, tile), lambda i, *_: (0, i))
    grid_spec = pltpu.PrefetchScalarGridSpec(
        num_scalar_prefetch=3,
        grid=(sum(tiles),),
        in_specs=[src_spec(f, n) for f, n in zip(firsts, tiles)] + [tok_spec, tok_spec],
        out_specs=pl.BlockSpec(memory_space=pl.ANY),
        scratch_shapes=[pltpu.VMEM((2, STAGE_ROWS, dh + LANES), U32), pltpu.VMEM((MOE_BLOCK, dh + LANES), U32),
                        pltpu.SemaphoreType.DMA((2,))],
    )
    return pl.pallas_call(
        functools.partial(_dispatch_kernel, tiles=tiles),
        grid_spec=grid_spec,
        out_shape=jax.ShapeDtypeStruct((p_rows, dh + LANES), U32),
        compiler_params=_cparams(("arbitrary",)),
        name="dispatch",
    )(seg, pad, zblk, *sources, io, wo)


def _expert_kernel(be_ref, nu_ref, va_ref, ne_ref, x_ref, wg_hbm, wu_hbm, wd_hbm, o_ref,
                   wg_b, wu_b, wd_b, wg_f, wu_f, wd_f, switch_ref, sem, *, layer):
    j = pl.program_id(0)
    dh = o_ref.shape[1]
    half = o_ref.shape[0] // 2
    valid = va_ref[j]
    e = be_ref[j]

    def weight_copies(expert, slot):
        return [pltpu.make_async_copy(hbm.at[layer, expert], buf.at[slot], sem.at[slot, k])
                for k, (hbm, buf) in enumerate(((wg_hbm, wg_f), (wu_hbm, wu_f), (wd_hbm, wd_f)))]

    @pl.when(j == 0)
    def _():
        switch_ref[0] = 0
        for cp in weight_copies(e, 0):
            cp.start()

    @pl.when((j == 0) | (e != be_ref[jnp.maximum(j - 1, 0)]))
    def _():
        slot = switch_ref[0] % 2
        for cp in weight_copies(e, slot):
            cp.wait()
        wg_b[...] = wg_f[slot].astype(BF16)
        wu_b[...] = wu_f[slot].astype(BF16)
        wd_b[...] = wd_f[slot].astype(BF16)
        nxt = ne_ref[e]

        @pl.when(nxt >= 0)
        def _():
            for cp in weight_copies(nxt, 1 - slot):
                cp.start()
        switch_ref[0] = switch_ref[0] + 1

    def ffn_halves(halves):
        gates = []
        for r in halves:
            x = _unpack_bf16_pairs(x_ref[r * half:(r + 1) * half, :dh])
            gates.append((jnp.dot(x, wg_b[...], preferred_element_type=F32),
                          jnp.dot(x, wu_b[...], preferred_element_type=F32)))
        for r, (a, u) in zip(halves, gates):
            hmid = (a * jax.nn.sigmoid(a)) * u
            y = jnp.dot(hmid.astype(BF16), wd_b[...], preferred_element_type=F32)
            row_w = lax.bitcast_convert_type(x_ref[r * half:(r + 1) * half, dh:dh + 1], F32)
            o_ref[r * half:(r + 1) * half, :] = _pack_bf16_pairs(y * row_w)

    @pl.when(valid > half)
    def _():
        ffn_halves((0, 1))

    @pl.when((valid > 0) & (valid <= half))
    def _():
        ffn_halves((0,))
        o_ref[half:, :] = jnp.zeros((half, dh), o_ref.dtype)

    @pl.when(valid == 0)
    def _():
        o_ref[...] = jnp.zeros_like(o_ref)


def _expert_call(block_e, nused, valid, next_e, xb, wg, wu, wd, layer):
    p_rows = xb.shape[0]
    dh = xb.shape[1] - LANES
    d = 2 * dh
    m = MOE_BLOCK
    nb = p_rows // m
    de = wg.shape[-1]

    def xmap(j, be, nu, va, ne):
        return (jnp.minimum(j, nu[0] - 1), 0)

    any_spec = pl.BlockSpec(memory_space=pl.ANY)
    grid_spec = pltpu.PrefetchScalarGridSpec(
        num_scalar_prefetch=4,
        grid=(nb,),
        in_specs=[pl.BlockSpec((m, dh + LANES), xmap), any_spec, any_spec, any_spec],
        out_specs=pl.BlockSpec((m, dh), lambda j, be, nu, va, ne: (j, 0)),
        scratch_shapes=[
            pltpu.VMEM((d, de), BF16), pltpu.VMEM((d, de), BF16), pltpu.VMEM((de, d), BF16),
            pltpu.VMEM((2, d, de), F32), pltpu.VMEM((2, d, de), F32), pltpu.VMEM((2, de, d), F32),
            pltpu.SMEM((1,), I32), pltpu.SemaphoreType.DMA((2, 3)),
        ],
    )
    return pl.pallas_call(
        functools.partial(_expert_kernel, layer=layer),
        grid_spec=grid_spec,
        out_shape=jax.ShapeDtypeStruct((p_rows, dh), U32),
        compiler_params=_cparams(("arbitrary",)),
        name="experts",
    )(block_e, nused, valid, next_e, xb, wg, wu, wd)


def _combine_kernel(*refs, tile0, final):
    if final:
        seg_ref, pad_ref, yb_hbm, io_ref, x_ref, g2_ref, fn_ref, o_ref, stage, sem = refs
    else:
        seg_ref, pad_ref, yb_hbm, io_ref, x_ref, g2_ref, o_ref, stage, sem = refs
    i = pl.program_id(0)

    slot = i % 2

    @pl.when(i == 0)
    def _():
        stage[...] = jnp.zeros_like(stage)
        _segment_copies(seg_ref, pad_ref, tile0, stage.at[0], yb_hbm, sem.at[0], to_hbm=False, wait=False)

    @pl.when(i + 1 < pl.num_programs(0))
    def _():
        _segment_copies(seg_ref, pad_ref, tile0 + i + 1, stage.at[1 - slot], yb_hbm, sem.at[1 - slot],
                        to_hbm=False, wait=False)

    _segment_copies(seg_ref, pad_ref, tile0 + i, stage.at[slot], yb_hbm, sem.at[slot], to_hbm=False, wait=True)
    rows = _unpack_bf16_pairs(stage[slot])
    m0, m1 = _sort_matrix(io_ref, STAGE_ROWS)
    pick = jnp.where(m0 | m1, 1.0, 0.0).astype(BF16)
    y = lax.dot_general(pick, rows, (((0,), (0,)), ((), ())), preferred_element_type=F32)
    xo = x_ref[...] + g2_ref[...] * y
    if final:
        ms = jnp.mean(xo * xo, axis=-1, keepdims=True)
        xo = (xo * lax.rsqrt(ms + NORM_EPS)) * fn_ref[...]
    o_ref[...] = xo


def _combine_call(seg, pad, yb, io, x2d, mods5, layer, row_of_tile, final_g, tile0):
    n, d = x2d.shape
    tm = ROUTE_TILE
    final = final_g is not None
    in_specs = [
        pl.BlockSpec(memory_space=pl.ANY),
        pl.BlockSpec((SUBLANES, tm), lambda t, *_: (0, tile0 + t)),
        pl.BlockSpec((tm, d), lambda t, *_: (t, 0)),
        pl.BlockSpec((None, None, None, 1, d), lambda t, *_: (layer, row_of_tile(t), 5, 0, 0)),
    ]
    args = [yb, io, x2d, mods5]
    if final:
        in_specs.append(pl.BlockSpec((1, d), lambda t, *_: (0, 0)))
        args.append(final_g.reshape(1, d))
    grid_spec = pltpu.PrefetchScalarGridSpec(
        num_scalar_prefetch=2,
        grid=(n // tm,),
        in_specs=in_specs,
        out_specs=pl.BlockSpec((tm, d), lambda t, *_: (t, 0)),
        scratch_shapes=[pltpu.VMEM((2, STAGE_ROWS, d // 2), U32), pltpu.SemaphoreType.DMA((2,))],
    )
    return pl.pallas_call(
        functools.partial(_combine_kernel, tile0=tile0, final=final),
        grid_spec=grid_spec,
        out_shape=jax.ShapeDtypeStruct((n, d), F32),
        compiler_params=_cparams(("arbitrary",)),
        name="combine",
    )(seg, pad, *args)


def _pack_w_in(w):
    assert sum(IN_SIZES[:6]) == C_Z + 2 * GLA_GATE_RANK and sum(IN_SIZES) - sum(IN_SIZES[:6]) == C_END - C_SQ
    wb = w.astype(BF16)
    cut = C_Z + 2 * GLA_GATE_RANK
    out = jnp.zeros(w.shape[:-1] + (C_END,), BF16)
    out = lax.dynamic_update_slice(out, wb[..., :cut], (0, 0, 0))
    return lax.dynamic_update_slice(out, wb[..., cut:], (0, 0, C_SQ))


def _pad_up(up, row0):
    up = up.reshape(GLA_GATE_RANK, GLA_HEADS // 2, LANES).transpose(1, 0, 2)
    hi = up.astype(BF16)
    lo = (up - hi.astype(F32)).astype(BF16)
    out = jnp.zeros((GLA_HEADS // 2, LANES, LANES), BF16)
    for group, part in enumerate((hi, hi, lo)):
        r = group * 2 * GLA_GATE_RANK + row0
        out = out.at[:, r:r + GLA_GATE_RANK, :].set(part)
    return out


def _rope_tables(seq):
    t = lax.broadcasted_iota(I32, (seq, LANES), 0)
    dim = lax.broadcasted_iota(I32, (seq, LANES), 1) % SWA_HD
    pos = jnp.where(dim < ROPE_AXIS_DIM, t // GRID_W, t % GRID_W).astype(F32)
    freq = (dim % ROPE_HALF).astype(F32)
    inv = ROPE_THETA ** (-(freq * 2.0 / ROPE_AXIS_DIM))
    ang = pos * inv
    sign = jnp.where(dim % ROPE_AXIS_DIM < ROPE_HALF, -1.0, 1.0)
    return jnp.cos(ang), jnp.sin(ang) * sign


def _tile(n, pref):
    t = pref
    while n % t:
        t //= 2
    return t


def kernel(x, c, ctx, c_ctx, w_ada, b_ada, norm1, norm2, w_in, gla_up_f, gla_bias_f, gla_up_b, gla_bias_b,
           gla_norm, swa_sink, w_out, w_router, b_router, w_gate, w_up, w_down, final_norm):
    batch, seq, d = x.shape
    lc = ctx.shape[1]
    depth = w_ada.shape[0]
    n_l = batch * seq
    n_c = batch * lc
    npair = GLA_HEADS // 2
    assert batch + 1 <= SUBLANES
    assert n_l % ROUTE_TILE == 0 and n_c % ROUTE_TILE == 0

    tm_l = _tile(seq, ROW_TILE)
    tm_in = _tile(seq, INPROJ_TILE)
    tm_c = _tile(lc, CTX_TILE)
    tl_l = _tile(seq, GLA_TILE)
    tl_c = _tile(lc, ROW_TILE)

    rows = jnp.zeros((SUBLANES, d), F32).at[:batch].set(c).at[batch].set(c_ctx)
    mods = _ada_call(rows, w_ada, b_ada)
    mods5 = mods.reshape(depth, SUBLANES, 6, 1, d)

    tabs = _rope_tables(seq)
    w_packed = _pack_w_in(w_in)
    w_out_b = w_out.astype(BF16)
    wr_pad = jnp.zeros((d, LANES), F32).at[:, :N_EXPERTS].set(w_router)
    wr_hi = wr_pad.astype(BF16)
    wr_lo = (wr_pad - wr_hi.astype(F32)).astype(BF16)
    wr_cat = jnp.concatenate([wr_hi, wr_lo], axis=-1)

    def lat_row(tm):
        return lambda t: t // (seq // tm)

    def ctx_row(tm):
        return lambda t: batch

    xl = x.reshape(n_l, d)
    xc = ctx.reshape(n_c, d)
    for i in range(depth):
        last = i == depth - 1
        upf = _pad_up(gla_up_f[i], 0)
        upb = _pad_up(gla_up_b[i], GLA_GATE_RANK)
        bias_f = gla_bias_f[i].reshape(npair, 1, LANES)
        bias_b = gla_bias_b[i].reshape(npair, 1, LANES)

        c_qk, c_v, c_g, c_z, c_sq, c_sk, c_sv = _inproj_call(
            xc, norm1[i], mods5, i, ctx_row(tm_c), w_packed, None, tm_c, lc // tm_c)
        l_qk, l_v, l_g, l_z, l_sq, l_sk, l_sv = _inproj_call(
            xl, norm1[i], mods5, i, lat_row(tm_in), w_packed, tabs, tm_in, seq // tm_in)

        s_zero = jnp.zeros((batch, npair, 2, 2 * GLA_DV, LANES), F32)
        oc_f, oc_b, s_ctx = _gla_call(c_qk, c_v, c_z, upf, upb, bias_f, bias_b, s_zero, batch, lc, tl_c)
        ol_f, ol_b, _ = _gla_call(l_qk, l_v, l_z, upf, upb, bias_f, bias_b, s_ctx, batch, seq, tl_l)

        swa_l = _swa_call(l_sq, l_sk, l_sv, c_sk, c_sv, swa_sink[i], batch, seq, lc)
        xl, h2l, lg_l = _outproj_call(ol_f, ol_b, l_g, gla_norm[i], swa_l, w_out_b, xl, mods5, i, lat_row(tm_l),
                                      norm2[i], wr_cat, tm_l)
        if last:
            logits_t = lg_l
            t_all = n_l
        else:
            swa_c = _swa_ctx_call(c_sq, c_sk, c_sv, swa_sink[i], batch, lc)
            xc, h2c, lg_c = _outproj_call(oc_f, oc_b, c_g, gla_norm[i], swa_c, w_out_b, xc, mods5, i, ctx_row(tm_c),
                                          norm2[i], wr_cat, tm_c)
            logits_t = jnp.concatenate([lg_l, lg_c], axis=1)
            t_all = n_l + n_c

        io, wo, seg, pad, blk = _route_call(logits_t, b_router, ROUTE_TILE)
        m = MOE_BLOCK
        padded = blk[:, 0]
        pend = jnp.cumsum(padded)
        ntile = t_all // ROUTE_TILE
        nb = -(-(t_all * TOP_K + ntile * N_EXPERTS * (SUBLANES - 1)) // m) + N_EXPERTS
        p_rows = nb * m
        seg = seg.reshape(-1)
        pad = pad.reshape(-1)
        nused = (pend[-1:] // m).astype(I32)
        tail = nused + jnp.arange(nb - (t_all * TOP_K) // m, dtype=I32)
        zblk = jnp.concatenate([jnp.where(padded > 0, pend // m - 1, -1),
                                jnp.where(tail < nb, tail, -1)]).astype(I32)
        blk_row = jnp.minimum(jnp.arange(nb, dtype=I32), nused - 1) * m
        block_e = jnp.sum((pend[None, :] <= blk_row[:, None]).astype(I32), axis=1)
        rows_e = jnp.sum(pad.reshape(N_EXPERTS, LANES), axis=1)
        own = (block_e[:, None] == jnp.arange(N_EXPERTS, dtype=I32)[None, :]).astype(I32)
        seg_end = jnp.sum(own * (pend - padded + rows_e)[None, :], axis=1)
        blk_idx = jnp.arange(nb, dtype=I32)
        valid = jnp.where(blk_idx < nused, jnp.clip(seg_end - blk_idx * m, 0, m), 0).astype(I32)

        sources = (h2l,) if last else (h2l, h2c)
        xb = _dispatch_call(seg, pad, zblk, io, wo, sources, p_rows)
        eid = jnp.arange(N_EXPERTS, dtype=I32)
        later = (eid[None, :] > eid[:, None]) & (padded[None, :] > 0)
        next_e = jnp.min(jnp.where(later, eid[None, :], N_EXPERTS), axis=1)
        next_e = jnp.where(next_e < N_EXPERTS, next_e, -1).astype(I32)
        yb = _expert_call(block_e, nused, valid, next_e, xb, w_gate, w_up, w_down, i)
        xl_new = _combine_call(seg, pad, yb, io, xl, mods5, i, lat_row(ROUTE_TILE),
                               final_norm if last else None, 0)
        if not last:
            xc = _combine_call(seg, pad, yb, io, xc, mods5, i, ctx_row(0), None, n_l // ROUTE_TILE)
        xl = xl_new
    return xl.reshape(batch, seq, d)
```

```python
import functools

import jax
import jax.numpy as jnp
from jax import lax
from jax.experimental import pallas as pl
from jax.experimental.pallas import tpu as pltpu

F32 = jnp.float32
BF16 = jnp.bfloat16
I32 = jnp.int32
HIGHEST = lax.Precision.HIGHEST

GRID_W = 64
NORM_EPS = 1e-6
GLA_HEADS = 4
GLA_DK = 64
GLA_DV = 128
GLA_KW = GLA_HEADS * GLA_DK
GLA_WIDTH = GLA_HEADS * GLA_DV
GLA_GATE_RANK = 16
GLA_TAU = 16.0
GLA_CHUNK = 64
SWA_HEADS = 8
SWA_KV_HEADS = 2
SWA_HD = 64
SWA_WIDTH = SWA_HEADS * SWA_HD
SWA_KVW = SWA_KV_HEADS * SWA_HD
WINDOW = 128
ATT_BLOCK = 128
ROPE_THETA = 10000.0
ROPE_AXIS_DIM = SWA_HD // 2
ROPE_HALF = ROPE_AXIS_DIM // 2
N_EXPERTS = 16
N_GROUPS = 4
EXPERTS_PER_GROUP = N_EXPERTS // N_GROUPS
TOP_K = 2
IN_SIZES = (GLA_KW, GLA_KW, GLA_WIDTH, GLA_WIDTH, GLA_GATE_RANK, GLA_GATE_RANK, SWA_WIDTH, SWA_KVW, SWA_KVW)

LANES = 128
SUBLANES = 8
VMEM_LIMIT = 48 * 1024 * 1024

ROW_TILE = 512
GLA_TILE = 1024
INPROJ_TILE = 1024
CTX_TILE = 256
ADA_COLS = 1536
SWA_BLOCKS_PER_STEP = 4
MOE_BLOCK = 512
ROUTE_TILE = 512
ROUTE_TILES_PER_STEP = 4
SEG_ALIGN = 16
SEG_SIZES = (512, 256, 128, 64, 32, 16)
STAGE_ROWS = TOP_K * ROUTE_TILE + 2 * LANES

C_QK = 0
C_V = 512
C_G = 1024
C_Z = 1536
C_SQ = 1664
C_SK = 2176
C_SV = 2304
C_END = 2432
SV_LANES = 4 * LANES
NEG_BIG = -1e30

assert WINDOW == ATT_BLOCK


def _cparams(sem):
    return pltpu.CompilerParams(dimension_semantics=sem, vmem_limit_bytes=VMEM_LIMIT)


def _ada_kernel(a_ref, w_ref, b_ref, o_ref):
    a = a_ref[...]
    act = a * jax.nn.sigmoid(a)
    o_ref[...] = jnp.dot(act, w_ref[...], preferred_element_type=F32, precision=HIGHEST) + b_ref[...]


def _ada_call(rows, w_ada, b_ada):
    depth, d, n6 = w_ada.shape
    tn = ADA_COLS
    return pl.pallas_call(
        _ada_kernel,
        grid=(depth, n6 // tn),
        in_specs=[
            pl.BlockSpec((SUBLANES, d), lambda l, j: (0, 0)),
            pl.BlockSpec((None, d, tn), lambda l, j: (l, 0, j)),
            pl.BlockSpec((None, 1, tn), lambda l, j: (l, 0, j)),
        ],
        out_specs=pl.BlockSpec((None, SUBLANES, tn), lambda l, j: (l, 0, j)),
        out_shape=jax.ShapeDtypeStruct((depth, SUBLANES, n6), F32),
        compiler_params=_cparams(("arbitrary", "arbitrary")),
        name="adaln",
    )(rows, w_ada, b_ada.reshape(depth, 1, n6))


def _rope_tile(xj, cos, sin, lane_lo):
    partner = jnp.where(lane_lo, pltpu.roll(xj, LANES - ROPE_HALF, 1), pltpu.roll(xj, ROPE_HALF, 1))
    return xj * cos + partner * sin


def _inproj_kernel(*refs, rope):
    if rope:
        (x_ref, g_ref, sc_ref, sh_ref, w_ref, cos_ref, sin_ref,
         qk_ref, v_ref, gate_ref, z_ref, sq_ref, sk_ref, sv_ref) = refs
    else:
        (x_ref, g_ref, sc_ref, sh_ref, w_ref,
         qk_ref, v_ref, gate_ref, z_ref, sq_ref, sk_ref, sv_ref) = refs
    half = x_ref.shape[0] // 2
    scale = g_ref[...] * (1.0 + sc_ref[...])
    normed = []
    for r in range(2):
        x = x_ref[r * half:(r + 1) * half, :]
        ms = jnp.mean(x * x, axis=-1, keepdims=True)
        normed.append(((x * lax.rsqrt(ms + NORM_EPS)) * scale + sh_ref[...]).astype(BF16))

    for r in range(2):
        rows = slice(r * half, (r + 1) * half)
        hb = normed[r]

        def proj(a, b):
            return jnp.dot(hb, w_ref[:, a:b], preferred_element_type=F32)

        qk = proj(C_QK, C_V)
        qk_ref[rows, :GLA_KW] = (qk[:, :GLA_KW] * (GLA_DK ** -0.5)).astype(qk_ref.dtype)
        qk_ref[rows, GLA_KW:] = qk[:, GLA_KW:].astype(qk_ref.dtype)
        v_ref[rows, :] = proj(C_V, C_G).astype(v_ref.dtype)
        gate_ref[rows, :] = proj(C_G, C_Z).astype(gate_ref.dtype)
        z_ref[rows, :] = proj(C_Z, C_SQ)
        sq = proj(C_SQ, C_SK) * (SWA_HD ** -0.5)
        sk = proj(C_SK, C_SV)
        sv = proj(C_SV, C_END)
        if rope:
            cos = cos_ref[rows, :]
            sin = sin_ref[rows, :]
            lane_lo = (lax.broadcasted_iota(I32, cos.shape, 1) % ROPE_AXIS_DIM) < ROPE_HALF
            for j in range(SWA_WIDTH // LANES):
                sl = slice(j * LANES, (j + 1) * LANES)
                sq_ref[rows, sl] = _rope_tile(sq[:, sl], cos, sin, lane_lo).astype(sq_ref.dtype)
            sk = _rope_tile(sk, cos, sin, lane_lo)
        else:
            sq_ref[rows, :] = sq.astype(sq_ref.dtype)
        first = lax.broadcasted_iota(I32, sk.shape, 1) < SWA_HD
        sk_sw = pltpu.roll(sk, SWA_HD, 1)
        sk_ref[rows, :LANES] = jnp.where(first, sk, sk_sw).astype(sk_ref.dtype)
        sk_ref[rows, LANES:] = jnp.where(first, sk_sw, sk).astype(sk_ref.dtype)
        sv_sw = pltpu.roll(sv, SWA_HD, 1)
        sv_ref[rows, 0 * LANES:1 * LANES] = jnp.where(first, sv, 1.0).astype(sv_ref.dtype)
        sv_ref[rows, 1 * LANES:2 * LANES] = jnp.where(first, 1.0, sv_sw).astype(sv_ref.dtype)
        sv_ref[rows, 2 * LANES:3 * LANES] = jnp.where(first, sv_sw, 1.0).astype(sv_ref.dtype)
        sv_ref[rows, 3 * LANES:4 * LANES] = jnp.where(first, 1.0, sv).astype(sv_ref.dtype)


def _inproj_call(x2d, g, mods5, layer, row_of_tile, w_packed, tabs, tm, tiles_per_seq):
    n, d = x2d.shape
    nt = n // tm
    rope = tabs is not None

    def mod_spec(k):
        return pl.BlockSpec((None, None, None, 1, d), lambda t: (layer, row_of_tile(t), k, 0, 0))

    in_specs = [
        pl.BlockSpec((tm, d), lambda t: (t, 0)),
        pl.BlockSpec((1, d), lambda t: (0, 0)),
        mod_spec(1), mod_spec(0),
        pl.BlockSpec((None, d, C_END), lambda t: (layer, 0, 0)),
    ]
    args = [x2d, g.reshape(1, d), mods5, mods5, w_packed]
    if rope:
        in_specs += [pl.BlockSpec((tm, LANES), lambda t: (t % tiles_per_seq, 0))] * 2
        args += [tabs[0], tabs[1]]
    widths = (2 * GLA_KW, GLA_WIDTH, GLA_WIDTH, LANES, SWA_WIDTH, 2 * SWA_KVW, SV_LANES)
    dtypes = (BF16, BF16, BF16, F32, BF16, BF16, BF16)
    return pl.pallas_call(
        functools.partial(_inproj_kernel, rope=rope),
        grid=(nt,),
        in_specs=in_specs,
        out_specs=[pl.BlockSpec((tm, w), lambda t: (t, 0)) for w in widths],
        out_shape=[jax.ShapeDtypeStruct((n, w), dt) for w, dt in zip(widths, dtypes)],
        compiler_params=_cparams(("arbitrary",)),
        name="inproj",
    )(*args)


def _log_sigmoid(x):
    return jnp.minimum(x, 0.0) - jnp.log(1.0 + jnp.exp(-jnp.abs(x)))


def _gla_prepare(q_ref, k_ref, v_ref, z_ref, up_ref, bias_ref, tri_ref, scr, fwd):
    qin_s, km0_s, km1_s, kout_s, vm0_s, vm1_s, dec_s = scr[:7]
    qbd_s = scr[9]
    tl = q_ref.shape[0]
    c = GLA_CHUNK
    z = z_ref[...]
    z_hi = z.astype(BF16).astype(F32)
    zc = z_hi + pltpu.roll(z - z_hi, 2 * GLA_GATE_RANK, 1) + pltpu.roll(z_hi, 4 * GLA_GATE_RANK, 1)
    x = jnp.dot(zc.astype(BF16), up_ref[...], preferred_element_type=F32) + bias_ref[...]
    la = _log_sigmoid(x) * (1.0 / GLA_TAU)
    hi = la.astype(BF16)
    lo = (la - hi.astype(F32)).astype(BF16)
    sums = jnp.dot(tri_ref[...], jnp.concatenate([hi, lo], axis=1), preferred_element_type=F32)
    b = sums[:, :LANES] + sums[:, LANES:]
    b3 = b.reshape(tl // c, c, LANES)
    edge = b3[:, c - 1:c, :] if fwd else b3[:, 0:1, :]
    btot = jnp.broadcast_to(edge, b3.shape).reshape(tl, LANES)
    q = q_ref[...].astype(F32)
    k = k_ref[...].astype(F32)
    first = (lax.broadcasted_iota(I32, (tl, LANES), 0) % (2 * c)) < c
    q_in = q * jnp.exp(b)
    qin_s[...] = q_in.astype(BF16)
    qbd_s[...] = jnp.concatenate([jnp.where(first, q_in, 0.0), jnp.where(first, 0.0, q_in)], axis=1).astype(BF16)
    k_in = k * jnp.exp(-b)
    head0 = lax.broadcasted_iota(I32, (tl, LANES), 1) < GLA_DK
    km0_s[...] = jnp.where(head0, k_in, 0.0).astype(BF16)
    km1_s[...] = jnp.where(head0, 0.0, k_in).astype(BF16)
    k_out = k * jnp.exp(btot - b)
    kout_s[...] = jnp.concatenate([jnp.where(first, k_out, 0.0), jnp.where(first, 0.0, k_out)], axis=1).astype(BF16)
    dec_s[...] = jnp.exp(btot)
    vf = v_ref[...].astype(F32)
    vhead0 = lax.broadcasted_iota(I32, (tl, 2 * GLA_DV), 1) < GLA_DV
    vm0_s[...] = jnp.where(vhead0, vf, 0.0).astype(BF16)
    vm1_s[...] = jnp.where(vhead0, 0.0, vf).astype(BF16)


def _gla_increment(v_ref, scr, pair):
    kout_s, u_s = scr[3], scr[7]
    rows = pl.ds(pair * 2 * GLA_CHUNK, 2 * GLA_CHUNK)
    u_t = lax.dot_general(v_ref[rows, :], kout_s[rows, :], (((0,), (0,)), ((), ())),
                          preferred_element_type=F32)
    srow = lax.broadcasted_iota(I32, (2 * GLA_DV, 2 * LANES), 0) // GLA_DV
    scol = (lax.broadcasted_iota(I32, (2 * GLA_DV, 2 * LANES), 1) % LANES) // GLA_DK
    u_s[pair] = jnp.where(srow == scol, u_t, 0.0)


def _gla_states(scr, s_ref, chunk_order):
    dec_s, u_s, sprev_s = scr[6], scr[7], scr[8]
    s_t = s_ref[...]
    for cidx in chunk_order:
        pair, lanes = cidx // 2, pl.ds((cidx % 2) * LANES, LANES)
        sprev_s[pair, :, lanes] = s_t.astype(BF16)
        r0 = cidx * GLA_CHUNK
        s_t = s_t * dec_s[r0:r0 + 1, :] + u_s[pair, :, lanes]
    s_ref[...] = s_t


def _gla_output(o_ref, scr, pair, fwd):
    qin_s, km0_s, km1_s, _, vm0_s, vm1_s, _, _, sprev_s, qbd_s = scr
    c = GLA_CHUNK
    r0 = pair * 2 * c
    rows = pl.ds(r0, 2 * c)
    ca, cb = pl.ds(r0, c), pl.ds(r0 + c, c)
    kst = jnp.concatenate([km0_s[ca, :], km1_s[ca, :], km0_s[cb, :], km1_s[cb, :]], axis=0)
    a = lax.dot_general(qin_s[rows, :], kst, (((1,), (1,)), ((), ())), preferred_element_type=F32)
    ri = lax.broadcasted_iota(I32, (2 * c, 4 * c), 0)
    ci = lax.broadcasted_iota(I32, (2 * c, 4 * c), 1)
    same_chunk = (ri // c) == (ci // (2 * c))
    keep = same_chunk & ((ci % c <= ri % c) if fwd else (ci % c >= ri % c))
    a = jnp.where(keep, a, 0.0).astype(BF16)
    vbd = jnp.concatenate([vm0_s[ca, :], vm1_s[ca, :], vm0_s[cb, :], vm1_s[cb, :]], axis=0)
    o = jnp.dot(a, vbd, preferred_element_type=F32)
    o = o + lax.dot_general(qbd_s[rows, :], sprev_s[pair], (((1,), (1,)), ((), ())), preferred_element_type=F32)
    o_ref[rows, :] = o.astype(o_ref.dtype)


def _gla_kernel(qf_ref, kf_ref, vf_ref, zf_ref, qb_ref, kb_ref, vb_ref, zb_ref,
                upf_ref, upb_ref, bf_ref, bb_ref, trif_ref, trib_ref, s0_ref,
                of_ref, ob_ref, sfin_ref, sf_scr, sb_scr, *scr, nchunk):
    i = pl.program_id(2)
    nt = pl.num_programs(2)
    scr_f, scr_b = scr[:len(scr) // 2], scr[len(scr) // 2:]

    @pl.when(i == 0)
    def _():
        sf_scr[...] = s0_ref[0]
        sb_scr[...] = s0_ref[1]

    _gla_prepare(qf_ref, kf_ref, vf_ref, zf_ref, upf_ref, bf_ref, trif_ref, scr_f, True)
    _gla_prepare(qb_ref, kb_ref, vb_ref, zb_ref, upb_ref, bb_ref, trib_ref, scr_b, False)
    for pair in range(nchunk // 2):
        _gla_increment(vf_ref, scr_f, pair)
        _gla_increment(vb_ref, scr_b, pair)
    _gla_states(scr_f, sf_scr, range(nchunk))
    _gla_states(scr_b, sb_scr, range(nchunk - 1, -1, -1))
    for pair in range(nchunk // 2):
        _gla_output(of_ref, scr_f, pair, True)
        _gla_output(ob_ref, scr_b, pair, False)

    @pl.when(i == nt - 1)
    def _():
        sfin_ref[0] = sf_scr[...]
        sfin_ref[1] = sb_scr[...]


def _gla_call(qk, v, z, upf_pad, upb_pad, bias_f, bias_b, s0, batch, seq, tl):
    n = qk.shape[0]
    nt = seq // tl
    npair = GLA_HEADS // 2

    def fwd_row(b, p, i):
        return b * nt + i

    def bwd_row(b, p, i):
        return b * nt + (nt - 1 - i)

    def specs(row):
        return [
            pl.BlockSpec((tl, LANES), lambda b, p, i: (row(b, p, i), p)),
            pl.BlockSpec((tl, LANES), lambda b, p, i: (row(b, p, i), npair + p)),
            pl.BlockSpec((tl, 2 * GLA_DV), lambda b, p, i: (row(b, p, i), p)),
            pl.BlockSpec((tl, LANES), lambda b, p, i: (row(b, p, i), 0)),
        ]

    in_specs = specs(fwd_row) + specs(bwd_row) + [
        pl.BlockSpec((None, LANES, LANES), lambda b, p, i: (p, 0, 0)),
        pl.BlockSpec((None, LANES, LANES), lambda b, p, i: (p, 0, 0)),
        pl.BlockSpec((None, 1, LANES), lambda b, p, i: (p, 0, 0)),
        pl.BlockSpec((None, 1, LANES), lambda b, p, i: (p, 0, 0)),
        pl.BlockSpec((None, tl, tl), lambda b, p, i: (0, 0, 0)),
        pl.BlockSpec((None, tl, tl), lambda b, p, i: (1, 0, 0)),
        pl.BlockSpec((None, None, 2, 2 * GLA_DV, LANES), lambda b, p, i: (b, p, 0, 0, 0)),
    ]
    ri = jnp.arange(tl)[:, None]
    ci = jnp.arange(tl)[None, :]
    same = (ri // GLA_CHUNK) == (ci // GLA_CHUNK)
    tri = jnp.stack([same & (ci <= ri), same & (ci >= ri)]).astype(BF16)
    npairs = tl // (2 * GLA_CHUNK)
    dir_scratch = [
        pltpu.VMEM((tl, LANES), BF16),
        pltpu.VMEM((tl, LANES), BF16),
        pltpu.VMEM((tl, LANES), BF16),
        pltpu.VMEM((tl, 2 * LANES), BF16),
        pltpu.VMEM((tl, 2 * GLA_DV), BF16),
        pltpu.VMEM((tl, 2 * GLA_DV), BF16),
        pltpu.VMEM((tl, LANES), F32),
        pltpu.VMEM((npairs, 2 * GLA_DV, 2 * LANES), F32),
        pltpu.VMEM((npairs, 2 * GLA_DV, 2 * LANES), BF16),
        pltpu.VMEM((tl, 2 * LANES), BF16),
    ]
    out_specs = [
        pl.BlockSpec((tl, 2 * GLA_DV), lambda b, p, i: (fwd_row(b, p, i), p)),
        pl.BlockSpec((tl, 2 * GLA_DV), lambda b, p, i: (bwd_row(b, p, i), p)),
        pl.BlockSpec((None, None, 2, 2 * GLA_DV, LANES), lambda b, p, i: (b, p, 0, 0, 0)),
    ]
    out_shape = [
        jax.ShapeDtypeStruct((n, GLA_WIDTH), BF16),
        jax.ShapeDtypeStruct((n, GLA_WIDTH), BF16),
        jax.ShapeDtypeStruct((batch, npair, 2, 2 * GLA_DV, LANES), F32),
    ]
    return pl.pallas_call(
        functools.partial(_gla_kernel, nchunk=tl // GLA_CHUNK),
        grid=(batch, npair, nt),
        in_specs=in_specs,
        out_specs=out_specs,
        out_shape=out_shape,
        scratch_shapes=[pltpu.VMEM((2 * GLA_DV, LANES), F32), pltpu.VMEM((2 * GLA_DV, LANES), F32)]
        + dir_scratch + dir_scratch,
        compiler_params=_cparams(("arbitrary", "arbitrary", "arbitrary")),
        name="gla",
    )(qk, qk, v, z, qk, qk, v, z, upf_pad, upb_pad, bias_f, bias_b, tri, tri, s0)


def _swa_block(sink_ref, q_ref, o_ref, row0, kall, vall, prev_ok, next_ok):
    w = ATT_BLOCK
    local = prev_ok is not None
    qrows = pl.ds(row0, w)
    r2 = lax.broadcasted_iota(I32, (2 * w, w), 0) % w
    c2 = lax.broadcasted_iota(I32, (2 * w, w), 1)
    if local:
        bias_prev = jnp.where((c2 >= r2) & prev_ok, 0.0, NEG_BIG)
        bias_next = jnp.where((c2 <= r2) & next_ok, 0.0, NEG_BIG)
    first = c2 < SWA_HD
    top = lax.broadcasted_iota(I32, (2 * w, 1), 0) < w
    scores = []
    for g in range(SWA_KV_HEADS):
        kd = kall[:, g * LANES:(g + 1) * LANES]
        qs = jnp.concatenate([q_ref[qrows, (2 * g) * LANES:(2 * g + 1) * LANES],
                              q_ref[qrows, (2 * g + 1) * LANES:(2 * g + 2) * LANES]], axis=0).astype(F32)
        for half in range(2):
            qm = (jnp.where(first, qs, 0.0) if half == 0 else jnp.where(first, 0.0, qs)).astype(BF16)
            s = lax.dot_general(qm, kd, (((1,), (1,)), ((), ())), preferred_element_type=F32)
            if local:
                s = jnp.concatenate([s[:, :w] + bias_prev, s[:, w:2 * w], s[:, 2 * w:3 * w] + bias_next,
                                     s[:, 3 * w:]], axis=1)
            scores.append(s)
    for g in range(SWA_KV_HEADS):
        outs = []
        for half in range(2):
            s = scores[2 * g + half]
            sk = jnp.where(top, sink_ref[4 * g + half], sink_ref[4 * g + 2 + half])
            m = jnp.maximum(jnp.max(s, axis=-1, keepdims=True), sk)
            p = jnp.exp((s - m).astype(BF16))
            va = vall[:, (2 * g + half) * LANES:(2 * g + half + 1) * LANES]
            acc = jnp.dot(p, va, preferred_element_type=F32)
            den = pltpu.roll(acc, SWA_HD, 1) + jnp.exp(sk - m)
            outs.append(acc / den)
        out = jnp.where(first, outs[0], outs[1])
        o_ref[qrows, (2 * g) * LANES:(2 * g + 1) * LANES] = out[:w].astype(o_ref.dtype)
        o_ref[qrows, (2 * g + 1) * LANES:(2 * g + 2) * LANES] = out[w:].astype(o_ref.dtype)


def _swa_kernel(*refs, nstep, local):
    w = ATT_BLOCK
    if not local:
        sink_ref, q_ref, kx_ref, vx_ref, o_ref = refs
        _swa_block(sink_ref, q_ref, o_ref, 0, kx_ref[...], vx_ref[...], None, None)
        return
    sink_ref, q_ref, kp_ref, kc_ref, kn_ref, vp_ref, vc_ref, vn_ref, kx_ref, vx_ref, o_ref = refs
    i = pl.program_id(1)
    nblk = q_ref.shape[0] // w
    k_blocks = [kp_ref[...]] + [kc_ref[j * w:(j + 1) * w, :] for j in range(nblk)] + [kn_ref[...]]
    v_blocks = [vp_ref[...]] + [vc_ref[j * w:(j + 1) * w, :] for j in range(nblk)] + [vn_ref[...]]
    exists = [i > 0] + [True] * nblk + [i < nstep - 1]
    for sub in range(nblk):
        kall = jnp.concatenate(k_blocks[sub:sub + 3] + [kx_ref[...]], axis=0)
        vall = jnp.concatenate(v_blocks[sub:sub + 3] + [vx_ref[...]], axis=0)
        _swa_block(sink_ref, q_ref, o_ref, sub * w, kall, vall, exists[sub], exists[sub + 2])


def _swa_call(sq, skd, svd, kcd, vcd, sink, batch, seq, lc):
    n = sq.shape[0]
    w = ATT_BLOCK
    nb = seq // w
    nblk = SWA_BLOCKS_PER_STEP if nb % SWA_BLOCKS_PER_STEP == 0 else 1
    nstep = nb // nblk
    kvw = 2 * SWA_KVW

    def pair(b, i):
        return (b * nstep + i, 0)

    def before(b, i):
        return (b * nb + jnp.maximum(nblk * i - 1, 0), 0)

    def after(b, i):
        return (b * nb + jnp.minimum(nblk * (i + 1), nb - 1), 0)

    def kv_specs(width):
        return [pl.BlockSpec((w, width), before), pl.BlockSpec((nblk * w, width), pair),
                pl.BlockSpec((w, width), after)]

    def ctx_spec(width):
        return pl.BlockSpec((lc, width), lambda b, i: (b, 0))

    return pl.pallas_call(
        functools.partial(_swa_kernel, nstep=nstep, local=True),
        grid=(batch, nstep),
        in_specs=[pl.BlockSpec(memory_space=pltpu.SMEM), pl.BlockSpec((nblk * w, SWA_WIDTH), pair)]
        + kv_specs(kvw) + kv_specs(SV_LANES) + [ctx_spec(kvw), ctx_spec(SV_LANES)],
        out_specs=pl.BlockSpec((nblk * w, SWA_WIDTH), pair),
        out_shape=jax.ShapeDtypeStruct((n, SWA_WIDTH), BF16),
        compiler_params=_cparams(("arbitrary", "arbitrary")),
        name="swa",
    )(sink, sq, skd, skd, skd, svd, svd, svd, kcd, vcd)


def _swa_ctx_call(sq, kcd, vcd, sink, batch, lc):
    n = sq.shape[0]
    w = ATT_BLOCK
    nb = lc // w
    kvw = 2 * SWA_KVW
    return pl.pallas_call(
        functools.partial(_swa_kernel, nstep=0, local=False),
        grid=(batch, nb),
        in_specs=[pl.BlockSpec(memory_space=pltpu.SMEM),
                  pl.BlockSpec((w, SWA_WIDTH), lambda b, i: (b * nb + i, 0)),
                  pl.BlockSpec((lc, kvw), lambda b, i: (b, 0)),
                  pl.BlockSpec((lc, SV_LANES), lambda b, i: (b, 0))],
        out_specs=pl.BlockSpec((w, SWA_WIDTH), lambda b, i: (b * nb + i, 0)),
        out_shape=jax.ShapeDtypeStruct((n, SWA_WIDTH), BF16),
        compiler_params=_cparams(("arbitrary", "arbitrary")),
        name="swa_ctx",
    )(sink, sq, kcd, vcd)


def _outproj_kernel(of_ref, ob_ref, gate_ref, gn_ref, swa_ref, w_ref, x_ref, g1_ref, n2_ref, sc_ref, sh_ref,
                    wr_ref, xo_ref, h2_ref, lg_ref):
    o = of_ref[...].astype(F32) + ob_ref[...].astype(F32)
    parts = []
    for h in range(GLA_HEADS):
        oh = o[:, h * GLA_DV:(h + 1) * GLA_DV]
        ms = jnp.mean(oh * oh, axis=-1, keepdims=True)
        parts.append(oh * lax.rsqrt(ms + NORM_EPS))
    on = jnp.concatenate(parts, axis=-1) * gn_ref[...]
    gate = gate_ref[...].astype(F32)
    gla = on * (gate * jax.nn.sigmoid(gate))
    mix = jnp.concatenate([gla.astype(BF16), swa_ref[...]], axis=-1)
    y = jnp.dot(mix, w_ref[...], preferred_element_type=F32)
    xo = x_ref[...] + g1_ref[...] * y
    xo_ref[...] = xo
    ms = jnp.mean(xo * xo, axis=-1, keepdims=True)
    h2 = (xo * lax.rsqrt(ms + NORM_EPS)) * (n2_ref[...] * (1.0 + sc_ref[...])) + sh_ref[...]
    hi = h2.astype(BF16)
    h2_ref[...] = hi
    lo = (h2 - hi.astype(F32)).astype(BF16)
    both = jnp.dot(hi, wr_ref[...], preferred_element_type=F32)
    lg = both[:, :LANES] + both[:, LANES:] + jnp.dot(lo, wr_ref[:, :LANES], preferred_element_type=F32)
    lg_ref[...] = jnp.transpose(lg)[:N_EXPERTS, :]


def _outproj_call(o_f, o_b, gate, gn, swa, w_out_b, x2d, mods5, layer, row_of_tile, n2, wr_cat, tm):
    n, d = x2d.shape
    nt = n // tm

    def mod_spec(k):
        return pl.BlockSpec((None, None, None, 1, d), lambda t: (layer, row_of_tile(t), k, 0, 0))

    return pl.pallas_call(
        _outproj_kernel,
        grid=(nt,),
        in_specs=[
            pl.BlockSpec((tm, GLA_WIDTH), lambda t: (t, 0)),
            pl.BlockSpec((tm, GLA_WIDTH), lambda t: (t, 0)),
            pl.BlockSpec((tm, GLA_WIDTH), lambda t: (t, 0)),
            pl.BlockSpec((1, GLA_WIDTH), lambda t: (0, 0)),
            pl.BlockSpec((tm, SWA_WIDTH), lambda t: (t, 0)),
            pl.BlockSpec((None, d, d), lambda t: (layer, 0, 0)),
            pl.BlockSpec((tm, d), lambda t: (t, 0)),
            mod_spec(2),
            pl.BlockSpec((1, d), lambda t: (0, 0)),
            mod_spec(4), mod_spec(3),
            pl.BlockSpec((d, 2 * LANES), lambda t: (0, 0)),
        ],
        out_specs=[
            pl.BlockSpec((tm, d), lambda t: (t, 0)),
            pl.BlockSpec((tm, d), lambda t: (t, 0)),
            pl.BlockSpec((N_EXPERTS, tm), lambda t: (0, t)),
        ],
        out_shape=[
            jax.ShapeDtypeStruct((n, d), F32),
            jax.ShapeDtypeStruct((n, d), BF16),
            jax.ShapeDtypeStruct((N_EXPERTS, n), F32),
        ],
        compiler_params=_cparams(("arbitrary",)),
        name="outproj",
    )(o_f, o_b, gate, gn.reshape(1, GLA_WIDTH), swa, w_out_b, x2d, mods5, n2.reshape(1, d), mods5, mods5, wr_cat)


def _first_index(vals, target):
    idx = jnp.full(target.shape, len(vals) - 1, I32)
    for i in range(len(vals) - 2, -1, -1):
        idx = jnp.where(vals[i] == target, i, idx)
    return idx


def _route_kernel(lg_ref, br_ref, io_ref, wo_ref, seg_ref, pad_ref, blk_ref, cnt_scr, seg_scr, pad_scr, *, tn):
    phase = pl.program_id(0)
    step = pl.program_id(1)
    group = lg_ref.shape[1] // tn

    @pl.when((phase == 0) & (step == 0))
    def _():
        cnt_scr[...] = jnp.zeros_like(cnt_scr)
        seg_scr[...] = jnp.zeros_like(seg_scr)
        pad_scr[...] = jnp.zeros_like(pad_scr)

    @pl.when((phase == 1) & (step == 0))
    def _():
        seg_len = jnp.floor((cnt_scr[...] + (SEG_ALIGN - 1)) * (1.0 / SEG_ALIGN)) * SEG_ALIGN
        rows_e = jnp.sum(seg_len, axis=1, keepdims=True)
        blocks = jnp.floor((rows_e + (MOE_BLOCK - 1)) * (1.0 / MOE_BLOCK)) * MOE_BLOCK
        r128 = lax.broadcasted_iota(I32, (LANES, LANES), 0)
        c128 = lax.broadcasted_iota(I32, (LANES, LANES), 1)
        before_tile = jnp.where(r128 < c128, 1.0, 0.0).astype(BF16)
        seg = jnp.dot((seg_len * (1.0 / SEG_ALIGN)).astype(BF16), before_tile,
                      preferred_element_type=F32) * SEG_ALIGN
        run = jnp.zeros((1, 1), F32)
        for e in range(N_EXPERTS):
            seg_scr[e:e + 1, :] = seg[e:e + 1, :] + run
            run = run + blocks[e:e + 1, :]
        pad_scr[...] = seg_len
        seg_ref[...] = seg_scr[...].astype(I32)
        pad_ref[...] = seg_len.astype(I32)
        blk_ref[...] = jnp.broadcast_to(blocks, blk_ref.shape).astype(I32)

    for g in range(group):
        lanes = slice(g * tn, (g + 1) * tn)
        _route_tile(lg_ref[:, lanes], br_ref, io_ref, wo_ref, lanes, step * group + g, phase, cnt_scr, pad_scr)


def _route_tile(lg, br_ref, io_ref, wo_ref, lanes, t, phase, cnt_scr, pad_scr):
    tn = lg.shape[1]
    tile_lane = lax.broadcasted_iota(I32, (N_EXPERTS, LANES), 1)
    s = jax.nn.sigmoid(lg)
    sb = s + br_ref[...]
    rows_s = [s[e:e + 1, :] for e in range(N_EXPERTS)]
    rows_b = [sb[e:e + 1, :] for e in range(N_EXPERTS)]
    gscore, gi1, gi2 = [], [], []
    epg = EXPERTS_PER_GROUP
    for g in range(N_GROUPS):
        a = rows_b[g * epg:(g + 1) * epg]
        m1 = functools.reduce(jnp.maximum, a)
        i1 = _first_index(a, m1)
        rest = [jnp.where(i1 == i, -jnp.inf, a[i]) for i in range(epg)]
        m2 = functools.reduce(jnp.maximum, rest)
        i2 = _first_index(rest, m2)
        gscore.append(m1 + m2)
        gi1.append(i1)
        gi2.append(i2)
    gm = functools.reduce(jnp.maximum, gscore)
    gsel = _first_index(gscore, gm)
    i1 = gi1[N_GROUPS - 1]
    i2 = gi2[N_GROUPS - 1]
    for g in range(N_GROUPS - 2, -1, -1):
        i1 = jnp.where(gsel == g, gi1[g], i1)
        i2 = jnp.where(gsel == g, gi2[g], i2)
    idx0 = gsel * epg + i1
    idx1 = gsel * epg + i2
    s0 = jnp.zeros_like(rows_s[0])
    s1 = jnp.zeros_like(rows_s[0])
    for e in range(N_EXPERTS):
        s0 = jnp.where(idx0 == e, rows_s[e], s0)
        s1 = jnp.where(idx1 == e, rows_s[e], s1)
    tot = s0 + s1
    w0 = s0 / tot
    w1 = s1 / tot

    eidx = lax.broadcasted_iota(I32, (N_EXPERTS, tn), 0)
    oh0 = eidx == idx0
    oh1 = eidx == idx1
    oh = jnp.where(oh0 | oh1, 1.0, 0.0)
    rr = lax.broadcasted_iota(I32, (tn, tn), 0)
    cc = lax.broadcasted_iota(I32, (tn, tn), 1)
    upper = jnp.where(rr < cc, 1.0, 0.0).astype(BF16)
    before = jnp.dot(oh.astype(BF16), upper, preferred_element_type=F32)

    @pl.when(phase == 0)
    def _():
        cnt_scr[...] = cnt_scr[...] + jnp.where(tile_lane == t, jnp.sum(oh, axis=1, keepdims=True), 0.0)

    pad_col = jnp.sum(jnp.where(tile_lane == t, pad_scr[...], 0.0), axis=1, keepdims=True)
    run = jnp.zeros((1, 1), F32)
    offs = []
    for e in range(N_EXPERTS):
        offs.append(run)
        run = run + pad_col[e:e + 1, :]
    pos = before + jnp.concatenate(offs, axis=0)
    spos0 = jnp.sum(jnp.where(oh0, pos, 0.0), axis=0, keepdims=True)
    spos1 = jnp.sum(jnp.where(oh1, pos, 0.0), axis=0, keepdims=True)
    zi = jnp.zeros((SUBLANES - 4, tn), I32)
    io_ref[:, lanes] = jnp.concatenate([idx0, idx1, spos0.astype(I32), spos1.astype(I32), zi], axis=0)
    wo_ref[:, lanes] = jnp.concatenate([w0, w1, jnp.zeros((SUBLANES - 2, tn), F32)], axis=0)


def _route_call(logits_t, b_router, tn):
    ne, t_all = logits_t.shape
    ntile = t_all // tn
    assert ntile <= LANES
    group = max(g for g in range(1, ROUTE_TILES_PER_STEP + 1) if ntile % g == 0)
    tw = group * tn
    tok_out = pl.BlockSpec((SUBLANES, tw), lambda p, t: (0, t * p))
    tab_out = pl.BlockSpec((ne, LANES), lambda p, t: (0, 0))
    tab = jax.ShapeDtypeStruct((ne, LANES), I32)
    return pl.pallas_call(
        functools.partial(_route_kernel, tn=tn),
        grid=(2, ntile // group),
        in_specs=[pl.BlockSpec((ne, tw), lambda p, t: (0, t)), pl.BlockSpec((ne, 1), lambda p, t: (0, 0))],
        out_specs=[tok_out, tok_out, tab_out, tab_out, tab_out],
        out_shape=[
            jax.ShapeDtypeStruct((SUBLANES, t_all), I32),
            jax.ShapeDtypeStruct((SUBLANES, t_all), F32),
            tab, tab, tab,
        ],
        scratch_shapes=[pltpu.VMEM((ne, LANES), F32)] * 3,
        compiler_params=_cparams(("arbitrary", "arbitrary")),
        name="route",
    )(logits_t, b_router.reshape(ne, 1))


def _segment_copies(seg_ref, pad_ref, t, stage, hbm, sem, to_hbm, wait):
    if wait:
        total = functools.reduce(lambda a, b: a + b, [pad_ref[e * LANES + t] for e in range(N_EXPERTS)])
        for size in (2 * SEG_SIZES[0],) + SEG_SIZES:
            @pl.when((total & size) != 0)
            def _(size=size):
                src, dst = stage.at[pl.ds(0, size), :], hbm.at[pl.ds(0, size), :]
                if not to_hbm:
                    src, dst = dst, src
                pltpu.make_async_copy(src, dst, sem).wait()
        return
    loc = 0
    for e in range(N_EXPERTS):
        n = pad_ref[e * LANES + t]
        start = seg_ref[e * LANES + t]
        for size in SEG_SIZES:
            off = n & (-2 * size)

            @pl.when((n & size) != 0)
            def _(off=off, size=size, loc=loc, start=start):
                s_rows = pl.ds(pl.multiple_of(loc + off, SEG_ALIGN), size)
                h_rows = pl.ds(pl.multiple_of(start + off, SEG_ALIGN), size)
                src, dst = (stage.at[s_rows, :], hbm.at[h_rows, :])
                if not to_hbm:
                    src, dst = dst, src
                pltpu.make_async_copy(src, dst, sem).start()
        loc = loc + n


def _sort_matrix(io_ref, rows):
    r = lax.broadcasted_iota(I32, (rows, io_ref.shape[1]), 0)
    return r == io_ref[2:3, :], r == io_ref[3:4, :]


def _dispatch_kernel(seg_ref, pad_ref, zblk_ref, *refs, tiles):
    srcs = refs[:len(tiles)]
    io_ref, wo_ref, xb_hbm, stage, zbuf, sem = refs[len(tiles):]
    i = pl.program_id(0)
    m = MOE_BLOCK

    def zero_copy(j):
        start = pl.multiple_of(zblk_ref[j] * m, m)
        return pltpu.make_async_copy(zbuf, xb_hbm.at[pl.ds(start, m), :], sem.at[0])

    @pl.when(i == 0)
    def _():
        zbuf[...] = jnp.zeros_like(zbuf)
        for j in range(zblk_ref.shape[0]):
            @pl.when(zblk_ref[j] >= 0)
            def _():
                zero_copy(j).start()
        for j in range(zblk_ref.shape[0]):
            @pl.when(zblk_ref[j] >= 0)
            def _():
                zero_copy(j).wait()

    slot = i % 2
    cur = stage.at[slot]

    def stage_tile(src_ref):
        x = src_ref[...]
        d = x.shape[1]
        m0, m1 = _sort_matrix(io_ref, STAGE_ROWS)
        sort = jnp.where(m0 | m1, 1.0, 0.0).astype(BF16)
        xs = jnp.dot(sort, x, preferred_element_type=F32)
        cur[:, :d] = xs.astype(BF16)
        ws = jnp.sum(jnp.where(m0, wo_ref[0:1, :], 0.0) + jnp.where(m1, wo_ref[1:2, :], 0.0),
                     axis=1, keepdims=True)
        ws_hi = ws.astype(BF16).astype(F32)
        lane = lax.broadcasted_iota(I32, (STAGE_ROWS, LANES), 1)
        cur[:, d:] = jnp.where(lane == 0, ws_hi, jnp.where(lane == 1, ws - ws_hi, 0.0)).astype(BF16)

    first = 0
    for src, ntile in zip(srcs, tiles):
        if len(srcs) == 1:
            stage_tile(src)
        else:
            pl.when((i >= first) & (i < first + ntile))(functools.partial(stage_tile, src))
        first += ntile
    _segment_copies(seg_ref, pad_ref, i, cur, xb_hbm, sem.at[slot], to_hbm=True, wait=False)

    @pl.when(i > 0)
    def _():
        _segment_copies(seg_ref, pad_ref, i - 1, stage.at[1 - slot], xb_hbm, sem.at[1 - slot], to_hbm=True, wait=True)

    @pl.when(i == pl.num_programs(0) - 1)
    def _():
        _segment_copies(seg_ref, pad_ref, i, cur, xb_hbm, sem.at[slot], to_hbm=True, wait=True)


def _dispatch_call(seg, pad, zblk, io, wo, sources, p_rows):
    tile = ROUTE_TILE
    d = sources[0].shape[1]
    tiles = tuple(s.shape[0] // tile for s in sources)
    firsts = tuple(sum(tiles[:k]) for k in range(len(tiles)))

    def src_spec(first, ntile):
        return pl.BlockSpec((tile, d), lambda i, *_: (jnp.clip(i - first, 0, ntile - 1), 0))

    tok_spec = pl.BlockSpec((SUBLANES, tile), lambda i, *_: (0, i))
    grid_spec = pltpu.PrefetchScalarGridSpec(
        num_scalar_prefetch=3,
        grid=(sum(tiles),),
        in_specs=[src_spec(f, n) for f, n in zip(firsts, tiles)] + [tok_spec, tok_spec],
        out_specs=pl.BlockSpec(memory_space=pl.ANY),
        scratch_shapes=[pltpu.VMEM((2, STAGE_ROWS, d + LANES), BF16), pltpu.VMEM((MOE_BLOCK, d + LANES), BF16),
                        pltpu.SemaphoreType.DMA((2,))],
    )
    return pl.pallas_call(
        functools.partial(_dispatch_kernel, tiles=tiles),
        grid_spec=grid_spec,
        out_shape=jax.ShapeDtypeStruct((p_rows, d + LANES), BF16),
        compiler_params=_cparams(("arbitrary",)),
        name="dispatch",
    )(seg, pad, zblk, *sources, io, wo)


def _expert_kernel(be_ref, nu_ref, va_ref, ne_ref, x_ref, wg_hbm, wu_hbm, wd_hbm, o_ref,
                   wg_b, wu_b, wd_b, wg_f, wu_f, wd_f, switch_ref, sem, *, layer):
    j = pl.program_id(0)
    d = o_ref.shape[1]
    half = o_ref.shape[0] // 2
    valid = va_ref[j]
    e = be_ref[j]

    def weight_copies(expert, slot):
        return [pltpu.make_async_copy(hbm.at[layer, expert], buf.at[slot], sem.at[slot, k])
                for k, (hbm, buf) in enumerate(((wg_hbm, wg_f), (wu_hbm, wu_f), (wd_hbm, wd_f)))]

    @pl.when(j == 0)
    def _():
        switch_ref[0] = 0
        for cp in weight_copies(e, 0):
            cp.start()

    @pl.when((j == 0) | (e != be_ref[jnp.maximum(j - 1, 0)]))
    def _():
        slot = switch_ref[0] % 2
        for cp in weight_copies(e, slot):
            cp.wait()
        wg_b[...] = wg_f[slot].astype(BF16)
        wu_b[...] = wu_f[slot].astype(BF16)
        wd_b[...] = wd_f[slot].astype(BF16)
        nxt = ne_ref[e]

        @pl.when(nxt >= 0)
        def _():
            for cp in weight_copies(nxt, 1 - slot):
                cp.start()
        switch_ref[0] = switch_ref[0] + 1

    def ffn_halves(halves):
        gates = []
        for r in halves:
            x = x_ref[r * half:(r + 1) * half, :d]
            gates.append((jnp.dot(x, wg_b[...], preferred_element_type=F32),
                          jnp.dot(x, wu_b[...], preferred_element_type=F32)))
        for r, (a, u) in zip(halves, gates):
            hmid = (a * jax.nn.sigmoid(a)) * u
            y = jnp.dot(hmid.astype(BF16), wd_b[...], preferred_element_type=F32)
            w_pair = x_ref[r * half:(r + 1) * half, d:].astype(F32)
            row_w = w_pair[:, 0:1] + w_pair[:, 1:2]
            o_ref[r * half:(r + 1) * half, :] = (y * row_w).astype(o_ref.dtype)

    @pl.when(valid > half)
    def _():
        ffn_halves((0, 1))

    @pl.when((valid > 0) & (valid <= half))
    def _():
        ffn_halves((0,))
        o_ref[half:, :] = jnp.zeros((half, d), o_ref.dtype)

    @pl.when(valid == 0)
    def _():
        o_ref[...] = jnp.zeros_like(o_ref)


def _expert_call(block_e, nused, valid, next_e, xb, wg, wu, wd, layer):
    p_rows = xb.shape[0]
    d = xb.shape[1] - LANES
    m = MOE_BLOCK
    nb = p_rows // m
    de = wg.shape[-1]

    def xmap(j, be, nu, va, ne):
        return (jnp.minimum(j, nu[0] - 1), 0)

    any_spec = pl.BlockSpec(memory_space=pl.ANY)
    grid_spec = pltpu.PrefetchScalarGridSpec(
        num_scalar_prefetch=4,
        grid=(nb,),
        in_specs=[pl.BlockSpec((m, d + LANES), xmap), any_spec, any_spec, any_spec],
        out_specs=pl.BlockSpec((m, d), lambda j, be, nu, va, ne: (j, 0)),
        scratch_shapes=[
            pltpu.VMEM((d, de), BF16), pltpu.VMEM((d, de), BF16), pltpu.VMEM((de, d), BF16),
            pltpu.VMEM((2, d, de), F32), pltpu.VMEM((2, d, de), F32), pltpu.VMEM((2, de, d), F32),
            pltpu.SMEM((1,), I32), pltpu.SemaphoreType.DMA((2, 3)),
        ],
    )
    return pl.pallas_call(
        functools.partial(_expert_kernel, layer=layer),
        grid_spec=grid_spec,
        out_shape=jax.ShapeDtypeStruct((p_rows, d), BF16),
        compiler_params=_cparams(("arbitrary",)),
        name="experts",
    )(block_e, nused, valid, next_e, xb, wg, wu, wd)


def _combine_kernel(*refs, tile0, final):
    if final:
        seg_ref, pad_ref, yb_hbm, io_ref, x_ref, g2_ref, fn_ref, o_ref, stage, sem = refs
    else:
        seg_ref, pad_ref, yb_hbm, io_ref, x_ref, g2_ref, o_ref, stage, sem = refs
    i = pl.program_id(0)

    slot = i % 2

    @pl.when(i == 0)
    def _():
        stage[...] = jnp.zeros_like(stage)
        _segment_copies(seg_ref, pad_ref, tile0, stage.at[0], yb_hbm, sem.at[0], to_hbm=False, wait=False)

    @pl.when(i + 1 < pl.num_programs(0))
    def _():
        _segment_copies(seg_ref, pad_ref, tile0 + i + 1, stage.at[1 - slot], yb_hbm, sem.at[1 - slot],
                        to_hbm=False, wait=False)

    _segment_copies(seg_ref, pad_ref, tile0 + i, stage.at[slot], yb_hbm, sem.at[slot], to_hbm=False, wait=True)
    rows = stage[slot]
    m0, m1 = _sort_matrix(io_ref, STAGE_ROWS)
    pick = jnp.where(m0 | m1, 1.0, 0.0).astype(BF16)
    y = lax.dot_general(pick, rows, (((0,), (0,)), ((), ())), preferred_element_type=F32)
    xo = x_ref[...] + g2_ref[...] * y
    if final:
        ms = jnp.mean(xo * xo, axis=-1, keepdims=True)
        xo = (xo * lax.rsqrt(ms + NORM_EPS)) * fn_ref[...]
    o_ref[...] = xo


def _combine_call(seg, pad, yb, io, x2d, mods5, layer, row_of_tile, final_g, tile0):
    n, d = x2d.shape
    tm = ROUTE_TILE
    final = final_g is not None
    in_specs = [
        pl.BlockSpec(memory_space=pl.ANY),
        pl.BlockSpec((SUBLANES, tm), lambda t, *_: (0, tile0 + t)),
        pl.BlockSpec((tm, d), lambda t, *_: (t, 0)),
        pl.BlockSpec((None, None, None, 1, d), lambda t, *_: (layer, row_of_tile(t), 5, 0, 0)),
    ]
    args = [yb, io, x2d, mods5]
    if final:
        in_specs.append(pl.BlockSpec((1, d), lambda t, *_: (0, 0)))
        args.append(final_g.reshape(1, d))
    grid_spec = pltpu.PrefetchScalarGridSpec(
        num_scalar_prefetch=2,
        grid=(n // tm,),
        in_specs=in_specs,
        out_specs=pl.BlockSpec((tm, d), lambda t, *_: (t, 0)),
        scratch_shapes=[pltpu.VMEM((2, STAGE_ROWS, d), BF16), pltpu.SemaphoreType.DMA((2,))],
    )
    return pl.pallas_call(
        functools.partial(_combine_kernel, tile0=tile0, final=final),
        grid_spec=grid_spec,
        out_shape=jax.ShapeDtypeStruct((n, d), F32),
        compiler_params=_cparams(("arbitrary",)),
        name="combine",
    )(seg, pad, *args)


def _pack_w_in(w):
    assert sum(IN_SIZES[:6]) == C_Z + 2 * GLA_GATE_RANK and sum(IN_SIZES) - sum(IN_SIZES[:6]) == C_END - C_SQ
    wb = w.astype(BF16)
    cut = C_Z + 2 * GLA_GATE_RANK
    out = jnp.zeros(w.shape[:-1] + (C_END,), BF16)
    out = lax.dynamic_update_slice(out, wb[..., :cut], (0, 0, 0))
    return lax.dynamic_update_slice(out, wb[..., cut:], (0, 0, C_SQ))


def _pad_up(up, row0):
    up = up.reshape(GLA_GATE_RANK, GLA_HEADS // 2, LANES).transpose(1, 0, 2)
    hi = up.astype(BF16)
    lo = (up - hi.astype(F32)).astype(BF16)
    out = jnp.zeros((GLA_HEADS // 2, LANES, LANES), BF16)
    for group, part in enumerate((hi, hi, lo)):
        r = group * 2 * GLA_GATE_RANK + row0
        out = out.at[:, r:r + GLA_GATE_RANK, :].set(part)
    return out


def _rope_tables(seq):
    rows = seq // GRID_W
    dim = jnp.arange(LANES, dtype=I32) % SWA_HD
    inv = ROPE_THETA ** (-((dim % ROPE_HALF).astype(F32) * 2.0 / ROPE_AXIS_DIM))
    sign = jnp.where(dim % ROPE_AXIS_DIM < ROPE_HALF, -1.0, 1.0)
    by_row = (dim < ROPE_AXIS_DIM)[None, None, :]
    ang_r = (jnp.arange(rows, dtype=F32)[:, None] * inv[None, :])[:, None, :]
    ang_c = (jnp.arange(GRID_W, dtype=F32)[:, None] * inv[None, :])[None, :, :]
    cos = jnp.where(by_row, jnp.cos(ang_r), jnp.cos(ang_c))
    sin = jnp.where(by_row, jnp.sin(ang_r), jnp.sin(ang_c)) * sign
    return cos.reshape(seq, LANES), sin.reshape(seq, LANES)


def _tile(n, pref):
    t = pref
    while n % t:
        t //= 2
    return t


def kernel(x, c, ctx, c_ctx, w_ada, b_ada, norm1, norm2, w_in, gla_up_f, gla_bias_f, gla_up_b, gla_bias_b,
           gla_norm, swa_sink, w_out, w_router, b_router, w_gate, w_up, w_down, final_norm):
    batch, seq, d = x.shape
    lc = ctx.shape[1]
    depth = w_ada.shape[0]
    n_l = batch * seq
    n_c = batch * lc
    npair = GLA_HEADS // 2
    assert batch + 1 <= SUBLANES
    assert n_l % ROUTE_TILE == 0 and n_c % ROUTE_TILE == 0

    tm_l = _tile(seq, ROW_TILE)
    tm_in = _tile(seq, INPROJ_TILE)
    tm_c = _tile(lc, CTX_TILE)
    tl_l = _tile(seq, GLA_TILE)
    tl_c = _tile(lc, ROW_TILE)

    rows = jnp.zeros((SUBLANES, d), F32).at[:batch].set(c).at[batch].set(c_ctx)
    mods = _ada_call(rows, w_ada, b_ada)
    mods5 = mods.reshape(depth, SUBLANES, 6, 1, d)

    tabs = _rope_tables(seq)
    w_packed = _pack_w_in(w_in)
    w_out_b = w_out.astype(BF16)
    wr_pad = jnp.zeros((d, LANES), F32).at[:, :N_EXPERTS].set(w_router)
    wr_hi = wr_pad.astype(BF16)
    wr_lo = (wr_pad - wr_hi.astype(F32)).astype(BF16)
    wr_cat = jnp.concatenate([wr_hi, wr_lo], axis=-1)

    def lat_row(tm):
        return lambda t: t // (seq // tm)

    def ctx_row(tm):
        return lambda t: batch

    xl = x.reshape(n_l, d)
    xc = ctx.reshape(n_c, d)
    for i in range(depth):
        last = i == depth - 1
        upf = _pad_up(gla_up_f[i], 0)
        upb = _pad_up(gla_up_b[i], GLA_GATE_RANK)
        bias_f = gla_bias_f[i].reshape(npair, 1, LANES)
        bias_b = gla_bias_b[i].reshape(npair, 1, LANES)

        c_qk, c_v, c_g, c_z, c_sq, c_sk, c_sv = _inproj_call(
            xc, norm1[i], mods5, i, ctx_row(tm_c), w_packed, None, tm_c, lc // tm_c)
        l_qk, l_v, l_g, l_z, l_sq, l_sk, l_sv = _inproj_call(
            xl, norm1[i], mods5, i, lat_row(tm_in), w_packed, tabs, tm_in, seq // tm_in)

        s_zero = jnp.zeros((batch, npair, 2, 2 * GLA_DV, LANES), F32)
        oc_f, oc_b, s_ctx = _gla_call(c_qk, c_v, c_z, upf, upb, bias_f, bias_b, s_zero, batch, lc, tl_c)
        ol_f, ol_b, _ = _gla_call(l_qk, l_v, l_z, upf, upb, bias_f, bias_b, s_ctx, batch, seq, tl_l)

        swa_l = _swa_call(l_sq, l_sk, l_sv, c_sk, c_sv, swa_sink[i], batch, seq, lc)
        xl, h2l, lg_l = _outproj_call(ol_f, ol_b, l_g, gla_norm[i], swa_l, w_out_b, xl, mods5, i, lat_row(tm_l),
                                      norm2[i], wr_cat, tm_l)
        if last:
            logits_t = lg_l
            t_all = n_l
        else:
            swa_c = _swa_ctx_call(c_sq, c_sk, c_sv, swa_sink[i], batch, lc)
            xc, h2c, lg_c = _outproj_call(oc_f, oc_b, c_g, gla_norm[i], swa_c, w_out_b, xc, mods5, i, ctx_row(tm_c),
                                          norm2[i], wr_cat, tm_c)
            logits_t = jnp.concatenate([lg_l, lg_c], axis=1)
            t_all = n_l + n_c

        io, wo, seg, pad, blk = _route_call(logits_t, b_router, ROUTE_TILE)
        m = MOE_BLOCK
        padded = blk[:, 0]
        pend = jnp.cumsum(padded)
        ntile = t_all // ROUTE_TILE
        nb = -(-(t_all * TOP_K + ntile * N_EXPERTS * (SEG_ALIGN - 1)) // m) + N_EXPERTS
        p_rows = nb * m
        seg = seg.reshape(-1)
        pad = pad.reshape(-1)
        nused = (pend[-1:] // m).astype(I32)
        tail = nused + jnp.arange(nb - (t_all * TOP_K) // m, dtype=I32)
        zblk = jnp.concatenate([jnp.where(padded > 0, pend // m - 1, -1),
                                jnp.where(tail < nb, tail, -1)]).astype(I32)
        blk_row = jnp.minimum(jnp.arange(nb, dtype=I32), nused - 1) * m
        block_e = jnp.sum((pend[None, :] <= blk_row[:, None]).astype(I32), axis=1)
        rows_e = jnp.sum(pad.reshape(N_EXPERTS, LANES), axis=1)
        own = (block_e[:, None] == jnp.arange(N_EXPERTS, dtype=I32)[None, :]).astype(I32)
        seg_end = jnp.sum(own * (pend - padded + rows_e)[None, :], axis=1)
        blk_idx = jnp.arange(nb, dtype=I32)
        valid = jnp.where(blk_idx < nused, jnp.clip(seg_end - blk_idx * m, 0, m), 0).astype(I32)

        sources = (h2l,) if last else (h2l, h2c)
        xb = _dispatch_call(seg, pad, zblk, io, wo, sources, p_rows)
        eid = jnp.arange(N_EXPERTS, dtype=I32)
        later = (eid[None, :] > eid[:, None]) & (padded[None, :] > 0)
        next_e = jnp.min(jnp.where(later, eid[None, :], N_EXPERTS), axis=1)
        next_e = jnp.where(next_e < N_EXPERTS, next_e, -1).astype(I32)
        yb = _expert_call(block_e, nused, valid, next_e, xb, w_gate, w_up, w_down, i)
        xl_new = _combine_call(seg, pad, yb, io, xl, mods5, i, lat_row(ROUTE_TILE),
                               final_norm if last else None, 0)
        if not last:
            xc = _combine_call(seg, pad, yb, io, xc, mods5, i, ctx_row(0), None, n_l // ROUTE_TILE)
        xl = xl_new
    return xl.reshape(batch, seq, d)
```

```python
import functools

import jax
import jax.numpy as jnp
from jax import lax
from jax.experimental import pallas as pl
from jax.experimental.pallas import tpu as pltpu

F32 = jnp.float32
BF16 = jnp.bfloat16
I32 = jnp.int32
U32 = jnp.uint32
HIGHEST = lax.Precision.HIGHEST

GRID_W = 64
NORM_EPS = 1e-6
GLA_HEADS = 4
GLA_DK = 64
GLA_DV = 128
GLA_KW = GLA_HEADS * GLA_DK
GLA_WIDTH = GLA_HEADS * GLA_DV
GLA_GATE_RANK = 16
GLA_TAU = 16.0
GLA_CHUNK = 64
SWA_HEADS = 8
SWA_KV_HEADS = 2
SWA_HD = 64
SWA_WIDTH = SWA_HEADS * SWA_HD
SWA_KVW = SWA_KV_HEADS * SWA_HD
WINDOW = 128
ATT_BLOCK = 128
ROPE_THETA = 10000.0
ROPE_AXIS_DIM = SWA_HD // 2
ROPE_HALF = ROPE_AXIS_DIM // 2
N_EXPERTS = 16
N_GROUPS = 4
EXPERTS_PER_GROUP = N_EXPERTS // N_GROUPS
TOP_K = 2
IN_SIZES = (GLA_KW, GLA_KW, GLA_WIDTH, GLA_WIDTH, GLA_GATE_RANK, GLA_GATE_RANK, SWA_WIDTH, SWA_KVW, SWA_KVW)

LANES = 128
SUBLANES = 8
VMEM_LIMIT = 48 * 1024 * 1024

ROW_TILE = 512
GLA_TILE = 1024
INPROJ_TILE = 1024
CTX_TILE = 256
ADA_COLS = 1536
SWA_BLOCKS_PER_STEP = 4
MOE_BLOCK = 512
ROUTE_TILE = 512
ROUTE_TILES_PER_STEP = 4
SEG_SIZES = (512, 256, 128, 64, 32, 16, 8)
STAGE_ROWS = TOP_K * ROUTE_TILE + LANES

C_QK = 0
C_V = 512
C_G = 1024
C_Z = 1536
C_SQ = 1664
C_SK = 2176
C_SV = 2304
C_END = 2432
SV_LANES = 4 * LANES
NEG_BIG = -1e30

assert WINDOW == ATT_BLOCK


def _cparams(sem):
    return pltpu.CompilerParams(dimension_semantics=sem, vmem_limit_bytes=VMEM_LIMIT)


def _pack_bf16_pairs(x):
    return _pack_rounded_pairs(x.astype(BF16).astype(F32))


def _pack_rounded_pairs(xr):
    n = xr.shape[1] // 2
    lo = lax.bitcast_convert_type(xr[:, :n], U32)
    hi = lax.bitcast_convert_type(xr[:, n:], U32)
    return (lo >> 16) | hi


def _unpack_bf16_pairs(w):
    lo = lax.bitcast_convert_type(w << 16, F32)
    hi = lax.bitcast_convert_type(w & jnp.uint32(0xFFFF0000), F32)
    return jnp.concatenate([lo.astype(BF16), hi.astype(BF16)], axis=-1)


def _ada_kernel(a_ref, w_ref, b_ref, o_ref):
    a = a_ref[...]
    act = a * jax.nn.sigmoid(a)
    o_ref[...] = jnp.dot(act, w_ref[...], preferred_element_type=F32, precision=HIGHEST) + b_ref[...]


def _ada_call(rows, w_ada, b_ada):
    depth, d, n6 = w_ada.shape
    tn = ADA_COLS
    return pl.pallas_call(
        _ada_kernel,
        grid=(depth, n6 // tn),
        in_specs=[
            pl.BlockSpec((SUBLANES, d), lambda l, j: (0, 0)),
            pl.BlockSpec((None, d, tn), lambda l, j: (l, 0, j)),
            pl.BlockSpec((None, 1, tn), lambda l, j: (l, 0, j)),
        ],
        out_specs=pl.BlockSpec((None, SUBLANES, tn), lambda l, j: (l, 0, j)),
        out_shape=jax.ShapeDtypeStruct((depth, SUBLANES, n6), F32),
        compiler_params=_cparams(("arbitrary", "arbitrary")),
        name="adaln",
    )(rows, w_ada, b_ada.reshape(depth, 1, n6))


def _rope_tile(xj, cos, sin, lane_lo):
    partner = jnp.where(lane_lo, pltpu.roll(xj, LANES - ROPE_HALF, 1), pltpu.roll(xj, ROPE_HALF, 1))
    return xj * cos + partner * sin


def _inproj_kernel(*refs, rope):
    if rope:
        (x_ref, g_ref, sc_ref, sh_ref, w_ref, cos_ref, sin_ref,
         qk_ref, v_ref, gate_ref, z_ref, sq_ref, sk_ref, sv_ref) = refs
    else:
        (x_ref, g_ref, sc_ref, sh_ref, w_ref,
         qk_ref, v_ref, gate_ref, z_ref, sq_ref, sk_ref, sv_ref) = refs
    half = x_ref.shape[0] // 2
    scale = g_ref[...] * (1.0 + sc_ref[...])
    normed = []
    for r in range(2):
        x = x_ref[r * half:(r + 1) * half, :]
        ms = jnp.mean(x * x, axis=-1, keepdims=True)
        normed.append(((x * lax.rsqrt(ms + NORM_EPS)) * scale + sh_ref[...]).astype(BF16))

    for r in range(2):
        rows = slice(r * half, (r + 1) * half)
        hb = normed[r]

        def proj(a, b):
            return jnp.dot(hb, w_ref[:, a:b], preferred_element_type=F32)

        qk = proj(C_QK, C_V)
        qk_ref[rows, :GLA_KW] = (qk[:, :GLA_KW] * (GLA_DK ** -0.5)).astype(qk_ref.dtype)
        qk_ref[rows, GLA_KW:] = qk[:, GLA_KW:].astype(qk_ref.dtype)
        v_ref[rows, :] = proj(C_V, C_G).astype(v_ref.dtype)
        gate_ref[rows, :] = proj(C_G, C_Z).astype(gate_ref.dtype)
        z_ref[rows, :] = proj(C_Z, C_SQ)
        sq = proj(C_SQ, C_SK) * (SWA_HD ** -0.5)
        sk = proj(C_SK, C_SV)
        sv = proj(C_SV, C_END)
        if rope:
            cos = cos_ref[rows, :]
            sin = sin_ref[rows, :]
            lane_lo = (lax.broadcasted_iota(I32, cos.shape, 1) % ROPE_AXIS_DIM) < ROPE_HALF
            for j in range(SWA_WIDTH // LANES):
                sl = slice(j * LANES, (j + 1) * LANES)
                sq_ref[rows, sl] = _rope_tile(sq[:, sl], cos, sin, lane_lo).astype(sq_ref.dtype)
            sk = _rope_tile(sk, cos, sin, lane_lo)
        else:
            sq_ref[rows, :] = sq.astype(sq_ref.dtype)
        first = lax.broadcasted_iota(I32, sk.shape, 1) < SWA_HD
        sk_sw = pltpu.roll(sk, SWA_HD, 1)
        sk_ref[rows, :LANES] = jnp.where(first, sk, sk_sw).astype(sk_ref.dtype)
        sk_ref[rows, LANES:] = jnp.where(first, sk_sw, sk).astype(sk_ref.dtype)
        sv_sw = pltpu.roll(sv, SWA_HD, 1)
        sv_ref[rows, 0 * LANES:1 * LANES] = jnp.where(first, sv, 1.0).astype(sv_ref.dtype)
        sv_ref[rows, 1 * LANES:2 * LANES] = jnp.where(first, 1.0, sv_sw).astype(sv_ref.dtype)
        sv_ref[rows, 2 * LANES:3 * LANES] = jnp.where(first, sv_sw, 1.0).astype(sv_ref.dtype)
        sv_ref[rows, 3 * LANES:4 * LANES] = jnp.where(first, 1.0, sv).astype(sv_ref.dtype)


def _inproj_call(x2d, g, mods5, layer, row_of_tile, w_packed, tabs, tm, tiles_per_seq):
    n, d = x2d.shape
    nt = n // tm
    rope = tabs is not None

    def mod_spec(k):
        return pl.BlockSpec((None, None, None, 1, d), lambda t: (layer, row_of_tile(t), k, 0, 0))

    in_specs = [
        pl.BlockSpec((tm, d), lambda t: (t, 0)),
        pl.BlockSpec((1, d), lambda t: (0, 0)),
        mod_spec(1), mod_spec(0),
        pl.BlockSpec((None, d, C_END), lambda t: (layer, 0, 0)),
    ]
    args = [x2d, g.reshape(1, d), mods5, mods5, w_packed]
    if rope:
        in_specs += [pl.BlockSpec((tm, LANES), lambda t: (t % tiles_per_seq, 0))] * 2
        args += [tabs[0], tabs[1]]
    widths = (2 * GLA_KW, GLA_WIDTH, GLA_WIDTH, LANES, SWA_WIDTH, 2 * SWA_KVW, SV_LANES)
    dtypes = (BF16, BF16, BF16, F32, BF16, BF16, BF16)
    return pl.pallas_call(
        functools.partial(_inproj_kernel, rope=rope),
        grid=(nt,),
        in_specs=in_specs,
        out_specs=[pl.BlockSpec((tm, w), lambda t: (t, 0)) for w in widths],
        out_shape=[jax.ShapeDtypeStruct((n, w), dt) for w, dt in zip(widths, dtypes)],
        compiler_params=_cparams(("arbitrary",)),
        name="inproj",
    )(*args)


def _log_sigmoid(x):
    return jnp.minimum(x, 0.0) - jnp.log(1.0 + jnp.exp(-jnp.abs(x)))


def _gla_prepare(q_ref, k_ref, v_ref, z_ref, up_ref, bias_ref, tri_ref, scr, fwd):
    qin_s, km0_s, km1_s, kout_s, vm0_s, vm1_s, dec_s = scr[:7]
    qbd_s = scr[9]
    tl = q_ref.shape[0]
    c = GLA_CHUNK
    z = z_ref[...]
    z_hi = z.astype(BF16).astype(F32)
    zc = z_hi + pltpu.roll(z - z_hi, 2 * GLA_GATE_RANK, 1) + pltpu.roll(z_hi, 4 * GLA_GATE_RANK, 1)
    x = jnp.dot(zc.astype(BF16), up_ref[...], preferred_element_type=F32) + bias_ref[...]
    la = _log_sigmoid(x) * (1.0 / GLA_TAU)
    hi = la.astype(BF16)
    lo = (la - hi.astype(F32)).astype(BF16)
    sums = jnp.dot(tri_ref[...], jnp.concatenate([hi, lo], axis=1), preferred_element_type=F32)
    b = sums[:, :LANES] + sums[:, LANES:]
    b3 = b.reshape(tl // c, c, LANES)
    edge = b3[:, c - 1:c, :] if fwd else b3[:, 0:1, :]
    btot = jnp.broadcast_to(edge, b3.shape).reshape(tl, LANES)
    q = q_ref[...].astype(F32)
    k = k_ref[...].astype(F32)
    first = (lax.broadcasted_iota(I32, (tl, LANES), 0) % (2 * c)) < c
    q_in = q * jnp.exp(b)
    qin_s[...] = q_in.astype(BF16)
    qbd_s[...] = jnp.concatenate([jnp.where(first, q_in, 0.0), jnp.where(first, 0.0, q_in)], axis=1).astype(BF16)
    k_in = k * jnp.exp(-b)
    head0 = lax.broadcasted_iota(I32, (tl, LANES), 1) < GLA_DK
    km0_s[...] = jnp.where(head0, k_in, 0.0).astype(BF16)
    km1_s[...] = jnp.where(head0, 0.0, k_in).astype(BF16)
    k_out = k * jnp.exp(btot - b)
    kout_s[...] = jnp.concatenate([jnp.where(first, k_out, 0.0), jnp.where(first, 0.0, k_out)], axis=1).astype(BF16)
    dec_s[...] = jnp.exp(btot)
    vf = v_ref[...].astype(F32)
    vhead0 = lax.broadcasted_iota(I32, (tl, 2 * GLA_DV), 1) < GLA_DV
    vm0_s[...] = jnp.where(vhead0, vf, 0.0).astype(BF16)
    vm1_s[...] = jnp.where(vhead0, 0.0, vf).astype(BF16)


def _gla_increment(v_ref, scr, pair):
    kout_s, u_s = scr[3], scr[7]
    rows = pl.ds(pair * 2 * GLA_CHUNK, 2 * GLA_CHUNK)
    u_t = lax.dot_general(v_ref[rows, :], kout_s[rows, :], (((0,), (0,)), ((), ())),
                          preferred_element_type=F32)
    srow = lax.broadcasted_iota(I32, (2 * GLA_DV, 2 * LANES), 0) // GLA_DV
    scol = (lax.broadcasted_iota(I32, (2 * GLA_DV, 2 * LANES), 1) % LANES) // GLA_DK
    u_s[pair] = jnp.where(srow == scol, u_t, 0.0)


def _gla_states(scr, s_ref, chunk_order):
    dec_s, u_s, sprev_s = scr[6], scr[7], scr[8]
    s_t = s_ref[...]
    for cidx in chunk_order:
        pair, lanes = cidx // 2, pl.ds((cidx % 2) * LANES, LANES)
        sprev_s[pair, :, lanes] = s_t.astype(BF16)
        r0 = cidx * GLA_CHUNK
        s_t = s_t * dec_s[r0:r0 + 1, :] + u_s[pair, :, lanes]
    s_ref[...] = s_t


def _gla_output(o_ref, scr, pair, fwd):
    qin_s, km0_s, km1_s, _, vm0_s, vm1_s, _, _, sprev_s, qbd_s = scr
    c = GLA_CHUNK
    r0 = pair * 2 * c
    rows = pl.ds(r0, 2 * c)
    ca, cb = pl.ds(r0, c), pl.ds(r0 + c, c)
    kst = jnp.concatenate([km0_s[ca, :], km1_s[ca, :], km0_s[cb, :], km1_s[cb, :]], axis=0)
    a = lax.dot_general(qin_s[rows, :], kst, (((1,), (1,)), ((), ())), preferred_element_type=F32)
    ri = lax.broadcasted_iota(I32, (2 * c, 4 * c), 0)
    ci = lax.broadcasted_iota(I32, (2 * c, 4 * c), 1)
    same_chunk = (ri // c) == (ci // (2 * c))
    keep = same_chunk & ((ci % c <= ri % c) if fwd else (ci % c >= ri % c))
    a = jnp.where(keep, a, 0.0).astype(BF16)
    vbd = jnp.concatenate([vm0_s[ca, :], vm1_s[ca, :], vm0_s[cb, :], vm1_s[cb, :]], axis=0)
    o = jnp.dot(a, vbd, preferred_element_type=F32)
    o = o + lax.dot_general(qbd_s[rows, :], sprev_s[pair], (((1,), (1,)), ((), ())), preferred_element_type=F32)
    o_ref[rows, :] = o.astype(o_ref.dtype)


def _gla_kernel(qf_ref, kf_ref, vf_ref, zf_ref, qb_ref, kb_ref, vb_ref, zb_ref,
                upf_ref, upb_ref, bf_ref, bb_ref, trif_ref, trib_ref, s0_ref,
                of_ref, ob_ref, sfin_ref, sf_scr, sb_scr, *scr, nchunk):
    i = pl.program_id(2)
    nt = pl.num_programs(2)
    scr_f, scr_b = scr[:len(scr) // 2], scr[len(scr) // 2:]

    @pl.when(i == 0)
    def _():
        sf_scr[...] = s0_ref[0]
        sb_scr[...] = s0_ref[1]

    _gla_prepare(qf_ref, kf_ref, vf_ref, zf_ref, upf_ref, bf_ref, trif_ref, scr_f, True)
    _gla_prepare(qb_ref, kb_ref, vb_ref, zb_ref, upb_ref, bb_ref, trib_ref, scr_b, False)
    for pair in range(nchunk // 2):
        _gla_increment(vf_ref, scr_f, pair)
        _gla_increment(vb_ref, scr_b, pair)
    _gla_states(scr_f, sf_scr, range(nchunk))
    _gla_states(scr_b, sb_scr, range(nchunk - 1, -1, -1))
    for pair in range(nchunk // 2):
        _gla_output(of_ref, scr_f, pair, True)
        _gla_output(ob_ref, scr_b, pair, False)

    @pl.when(i == nt - 1)
    def _():
        sfin_ref[0] = sf_scr[...]
        sfin_ref[1] = sb_scr[...]


def _gla_call(qk, v, z, upf_pad, upb_pad, bias_f, bias_b, s0, batch, seq, tl):
    n = qk.shape[0]
    nt = seq // tl
    npair = GLA_HEADS // 2

    def fwd_row(b, p, i):
        return b * nt + i

    def bwd_row(b, p, i):
        return b * nt + (nt - 1 - i)

    def specs(row):
        return [
            pl.BlockSpec((tl, LANES), lambda b, p, i: (row(b, p, i), p)),
            pl.BlockSpec((tl, LANES), lambda b, p, i: (row(b, p, i), npair + p)),
            pl.BlockSpec((tl, 2 * GLA_DV), lambda b, p, i: (row(b, p, i), p)),
            pl.BlockSpec((tl, LANES), lambda b, p, i: (row(b, p, i), 0)),
        ]

    in_specs = specs(fwd_row) + specs(bwd_row) + [
        pl.BlockSpec((None, LANES, LANES), lambda b, p, i: (p, 0, 0)),
        pl.BlockSpec((None, LANES, LANES), lambda b, p, i: (p, 0, 0)),
        pl.BlockSpec((None, 1, LANES), lambda b, p, i: (p, 0, 0)),
        pl.BlockSpec((None, 1, LANES), lambda b, p, i: (p, 0, 0)),
        pl.BlockSpec((None, tl, tl), lambda b, p, i: (0, 0, 0)),
        pl.BlockSpec((None, tl, tl), lambda b, p, i: (1, 0, 0)),
        pl.BlockSpec((None, None, 2, 2 * GLA_DV, LANES), lambda b, p, i: (b, p, 0, 0, 0)),
    ]
    ri = jnp.arange(tl)[:, None]
    ci = jnp.arange(tl)[None, :]
    same = (ri // GLA_CHUNK) == (ci // GLA_CHUNK)
    tri = jnp.stack([same & (ci <= ri), same & (ci >= ri)]).astype(BF16)
    npairs = tl // (2 * GLA_CHUNK)
    dir_scratch = [
        pltpu.VMEM((tl, LANES), BF16),
        pltpu.VMEM((tl, LANES), BF16),
        pltpu.VMEM((tl, LANES), BF16),
        pltpu.VMEM((tl, 2 * LANES), BF16),
        pltpu.VMEM((tl, 2 * GLA_DV), BF16),
        pltpu.VMEM((tl, 2 * GLA_DV), BF16),
        pltpu.VMEM((tl, LANES), F32),
        pltpu.VMEM((npairs, 2 * GLA_DV, 2 * LANES), F32),
        pltpu.VMEM((npairs, 2 * GLA_DV, 2 * LANES), BF16),
        pltpu.VMEM((tl, 2 * LANES), BF16),
    ]
    out_specs = [
        pl.BlockSpec((tl, 2 * GLA_DV), lambda b, p, i: (fwd_row(b, p, i), p)),
        pl.BlockSpec((tl, 2 * GLA_DV), lambda b, p, i: (bwd_row(b, p, i), p)),
        pl.BlockSpec((None, None, 2, 2 * GLA_DV, LANES), lambda b, p, i: (b, p, 0, 0, 0)),
    ]
    out_shape = [
        jax.ShapeDtypeStruct((n, GLA_WIDTH), BF16),
        jax.ShapeDtypeStruct((n, GLA_WIDTH), BF16),
        jax.ShapeDtypeStruct((batch, npair, 2, 2 * GLA_DV, LANES), F32),
    ]
    return pl.pallas_call(
        functools.partial(_gla_kernel, nchunk=tl // GLA_CHUNK),
        grid=(batch, npair, nt),
        in_specs=in_specs,
        out_specs=out_specs,
        out_shape=out_shape,
        scratch_shapes=[pltpu.VMEM((2 * GLA_DV, LANES), F32), pltpu.VMEM((2 * GLA_DV, LANES), F32)]
        + dir_scratch + dir_scratch,
        compiler_params=_cparams(("arbitrary", "arbitrary", "arbitrary")),
        name="gla",
    )(qk, qk, v, z, qk, qk, v, z, upf_pad, upb_pad, bias_f, bias_b, tri, tri, s0)


def _swa_block(sink_ref, q_ref, o_ref, row0, kall, vall, prev_ok, next_ok):
    w = ATT_BLOCK
    local = prev_ok is not None
    qrows = pl.ds(row0, w)
    r2 = lax.broadcasted_iota(I32, (2 * w, w), 0) % w
    c2 = lax.broadcasted_iota(I32, (2 * w, w), 1)
    if local:
        bias_prev = jnp.where((c2 >= r2) & prev_ok, 0.0, NEG_BIG)
        bias_next = jnp.where((c2 <= r2) & next_ok, 0.0, NEG_BIG)
    first = c2 < SWA_HD
    top = lax.broadcasted_iota(I32, (2 * w, 1), 0) < w
    scores = []
    for g in range(SWA_KV_HEADS):
        kd = kall[:, g * LANES:(g + 1) * LANES]
        qs = jnp.concatenate([q_ref[qrows, (2 * g) * LANES:(2 * g + 1) * LANES],
                              q_ref[qrows, (2 * g + 1) * LANES:(2 * g + 2) * LANES]], axis=0).astype(F32)
        for half in range(2):
            qm = (jnp.where(first, qs, 0.0) if half == 0 else jnp.where(first, 0.0, qs)).astype(BF16)
            s = lax.dot_general(qm, kd, (((1,), (1,)), ((), ())), preferred_element_type=F32)
            if local:
                s = jnp.concatenate([s[:, :w] + bias_prev, s[:, w:2 * w], s[:, 2 * w:3 * w] + bias_next,
                                     s[:, 3 * w:]], axis=1)
            scores.append(s)
    for g in range(SWA_KV_HEADS):
        outs = []
        for half in range(2):
            s = scores[2 * g + half]
            sk = jnp.where(top, sink_ref[4 * g + half], sink_ref[4 * g + 2 + half])
            m = jnp.maximum(jnp.max(s, axis=-1, keepdims=True), sk)
            p = jnp.exp((s - m).astype(BF16))
            va = vall[:, (2 * g + half) * LANES:(2 * g + half + 1) * LANES]
            acc = jnp.dot(p, va, preferred_element_type=F32)
            den = pltpu.roll(acc, SWA_HD, 1) + jnp.exp(sk - m)
            outs.append(acc / den)
        out = jnp.where(first, outs[0], outs[1])
        o_ref[qrows, (2 * g) * LANES:(2 * g + 1) * LANES] = out[:w].astype(o_ref.dtype)
        o_ref[qrows, (2 * g + 1) * LANES:(2 * g + 2) * LANES] = out[w:].astype(o_ref.dtype)


def _swa_kernel(*refs, nstep, local):
    w = ATT_BLOCK
    if not local:
        sink_ref, q_ref, kx_ref, vx_ref, o_ref = refs
        _swa_block(sink_ref, q_ref, o_ref, 0, kx_ref[...], vx_ref[...], None, None)
        return
    sink_ref, q_ref, kp_ref, kc_ref, kn_ref, vp_ref, vc_ref, vn_ref, kx_ref, vx_ref, o_ref = refs
    i = pl.program_id(1)
    nblk = q_ref.shape[0] // w
    k_blocks = [kp_ref[...]] + [kc_ref[j * w:(j + 1) * w, :] for j in range(nblk)] + [kn_ref[...]]
    v_blocks = [vp_ref[...]] + [vc_ref[j * w:(j + 1) * w, :] for j in range(nblk)] + [vn_ref[...]]
    exists = [i > 0] + [True] * nblk + [i < nstep - 1]
    for sub in range(nblk):
        kall = jnp.concatenate(k_blocks[sub:sub + 3] + [kx_ref[...]], axis=0)
        vall = jnp.concatenate(v_blocks[sub:sub + 3] + [vx_ref[...]], axis=0)
        _swa_block(sink_ref, q_ref, o_ref, sub * w, kall, vall, exists[sub], exists[sub + 2])


def _swa_call(sq, skd, svd, kcd, vcd, sink, batch, seq, lc):
    n = sq.shape[0]
    w = ATT_BLOCK
    nb = seq // w
    nblk = SWA_BLOCKS_PER_STEP if nb % SWA_BLOCKS_PER_STEP == 0 else 1
    nstep = nb // nblk
    kvw = 2 * SWA_KVW

    def pair(b, i):
        return (b * nstep + i, 0)

    def before(b, i):
        return (b * nb + jnp.maximum(nblk * i - 1, 0), 0)

    def after(b, i):
        return (b * nb + jnp.minimum(nblk * (i + 1), nb - 1), 0)

    def kv_specs(width):
        return [pl.BlockSpec((w, width), before), pl.BlockSpec((nblk * w, width), pair),
                pl.BlockSpec((w, width), after)]

    def ctx_spec(width):
        return pl.BlockSpec((lc, width), lambda b, i: (b, 0))

    return pl.pallas_call(
        functools.partial(_swa_kernel, nstep=nstep, local=True),
        grid=(batch, nstep),
        in_specs=[pl.BlockSpec(memory_space=pltpu.SMEM), pl.BlockSpec((nblk * w, SWA_WIDTH), pair)]
        + kv_specs(kvw) + kv_specs(SV_LANES) + [ctx_spec(kvw), ctx_spec(SV_LANES)],
        out_specs=pl.BlockSpec((nblk * w, SWA_WIDTH), pair),
        out_shape=jax.ShapeDtypeStruct((n, SWA_WIDTH), BF16),
        compiler_params=_cparams(("arbitrary", "arbitrary")),
        name="swa",
    )(sink, sq, skd, skd, skd, svd, svd, svd, kcd, vcd)


def _swa_ctx_call(sq, kcd, vcd, sink, batch, lc):
    n = sq.shape[0]
    w = ATT_BLOCK
    nb = lc // w
    kvw = 2 * SWA_KVW
    return pl.pallas_call(
        functools.partial(_swa_kernel, nstep=0, local=False),
        grid=(batch, nb),
        in_specs=[pl.BlockSpec(memory_space=pltpu.SMEM),
                  pl.BlockSpec((w, SWA_WIDTH), lambda b, i: (b * nb + i, 0)),
                  pl.BlockSpec((lc, kvw), lambda b, i: (b, 0)),
                  pl.BlockSpec((lc, SV_LANES), lambda b, i: (b, 0))],
        out_specs=pl.BlockSpec((w, SWA_WIDTH), lambda b, i: (b * nb + i, 0)),
        out_shape=jax.ShapeDtypeStruct((n, SWA_WIDTH), BF16),
        compiler_params=_cparams(("arbitrary", "arbitrary")),
        name="swa_ctx",
    )(sink, sq, kcd, vcd)


def _outproj_kernel(of_ref, ob_ref, gate_ref, gn_ref, swa_ref, w_ref, x_ref, g1_ref, n2_ref, sc_ref, sh_ref,
                    wr_ref, xo_ref, h2_ref, lg_ref):
    o = of_ref[...].astype(F32) + ob_ref[...].astype(F32)
    parts = []
    for h in range(GLA_HEADS):
        oh = o[:, h * GLA_DV:(h + 1) * GLA_DV]
        ms = jnp.mean(oh * oh, axis=-1, keepdims=True)
        parts.append(oh * lax.rsqrt(ms + NORM_EPS))
    on = jnp.concatenate(parts, axis=-1) * gn_ref[...]
    gate = gate_ref[...].astype(F32)
    gla = on * (gate * jax.nn.sigmoid(gate))
    mix = jnp.concatenate([gla.astype(BF16), swa_ref[...]], axis=-1)
    y = jnp.dot(mix, w_ref[...], preferred_element_type=F32)
    xo = x_ref[...] + g1_ref[...] * y
    xo_ref[...] = xo
    ms = jnp.mean(xo * xo, axis=-1, keepdims=True)
    h2 = (xo * lax.rsqrt(ms + NORM_EPS)) * (n2_ref[...] * (1.0 + sc_ref[...])) + sh_ref[...]
    hi = h2.astype(BF16)
    hi_f = hi.astype(F32)
    h2_ref[...] = _pack_rounded_pairs(hi_f)
    lo = (h2 - hi_f).astype(BF16)
    both = jnp.dot(hi, wr_ref[...], preferred_element_type=F32)
    lg = both[:, :LANES] + both[:, LANES:] + jnp.dot(lo, wr_ref[:, :LANES], preferred_element_type=F32)
    lg_ref[...] = jnp.transpose(lg)[:N_EXPERTS, :]


def _outproj_call(o_f, o_b, gate, gn, swa, w_out_b, x2d, mods5, layer, row_of_tile, n2, wr_cat, tm):
    n, d = x2d.shape
    nt = n // tm

    def mod_spec(k):
        return pl.BlockSpec((None, None, None, 1, d), lambda t: (layer, row_of_tile(t), k, 0, 0))

    return pl.pallas_call(
        _outproj_kernel,
        grid=(nt,),
        in_specs=[
            pl.BlockSpec((tm, GLA_WIDTH), lambda t: (t, 0)),
            pl.BlockSpec((tm, GLA_WIDTH), lambda t: (t, 0)),
            pl.BlockSpec((tm, GLA_WIDTH), lambda t: (t, 0)),
            pl.BlockSpec((1, GLA_WIDTH), lambda t: (0, 0)),
            pl.BlockSpec((tm, SWA_WIDTH), lambda t: (t, 0)),
            pl.BlockSpec((None, d, d), lambda t: (layer, 0, 0)),
            pl.BlockSpec((tm, d), lambda t: (t, 0)),
            mod_spec(2),
            pl.BlockSpec((1, d), lambda t: (0, 0)),
            mod_spec(4), mod_spec(3),
            pl.BlockSpec((d, 2 * LANES), lambda t: (0, 0)),
        ],
        out_specs=[
            pl.BlockSpec((tm, d), lambda t: (t, 0)),
            pl.BlockSpec((tm, d // 2), lambda t: (t, 0)),
            pl.BlockSpec((N_EXPERTS, tm), lambda t: (0, t)),
        ],
        out_shape=[
            jax.ShapeDtypeStruct((n, d), F32),
            jax.ShapeDtypeStruct((n, d // 2), U32),
            jax.ShapeDtypeStruct((N_EXPERTS, n), F32),
        ],
        compiler_params=_cparams(("arbitrary",)),
        name="outproj",
    )(o_f, o_b, gate, gn.reshape(1, GLA_WIDTH), swa, w_out_b, x2d, mods5, n2.reshape(1, d), mods5, mods5, wr_cat)


def _first_index(vals, target):
    idx = jnp.full(target.shape, len(vals) - 1, I32)
    for i in range(len(vals) - 2, -1, -1):
        idx = jnp.where(vals[i] == target, i, idx)
    return idx


def _route_kernel(lg_ref, br_ref, io_ref, wo_ref, seg_ref, pad_ref, blk_ref, cnt_scr, seg_scr, pad_scr, *, tn):
    phase = pl.program_id(0)
    step = pl.program_id(1)
    group = lg_ref.shape[1] // tn

    @pl.when((phase == 0) & (step == 0))
    def _():
        cnt_scr[...] = jnp.zeros_like(cnt_scr)
        seg_scr[...] = jnp.zeros_like(seg_scr)
        pad_scr[...] = jnp.zeros_like(pad_scr)

    @pl.when((phase == 1) & (step == 0))
    def _():
        seg_len = jnp.floor((cnt_scr[...] + (SUBLANES - 1)) * (1.0 / SUBLANES)) * SUBLANES
        rows_e = jnp.sum(seg_len, axis=1, keepdims=True)
        blocks = jnp.floor((rows_e + (MOE_BLOCK - 1)) * (1.0 / MOE_BLOCK)) * MOE_BLOCK
        r128 = lax.broadcasted_iota(I32, (LANES, LANES), 0)
        c128 = lax.broadcasted_iota(I32, (LANES, LANES), 1)
        before_tile = jnp.where(r128 < c128, 1.0, 0.0).astype(BF16)
        seg = jnp.dot((seg_len * (1.0 / SUBLANES)).astype(BF16), before_tile,
                      preferred_element_type=F32) * SUBLANES
        run = jnp.zeros((1, 1), F32)
        for e in range(N_EXPERTS):
            seg_scr[e:e + 1, :] = seg[e:e + 1, :] + run
            run = run + blocks[e:e + 1, :]
        pad_scr[...] = seg_len
        seg_ref[...] = seg_scr[...].astype(I32)
        pad_ref[...] = seg_len.astype(I32)
        blk_ref[...] = jnp.broadcast_to(blocks, blk_ref.shape).astype(I32)

    for g in range(group):
        lanes = slice(g * tn, (g + 1) * tn)
        _route_tile(lg_ref[:, lanes], br_ref, io_ref, wo_ref, lanes, step * group + g, phase, cnt_scr, pad_scr)


def _route_tile(lg, br_ref, io_ref, wo_ref, lanes, t, phase, cnt_scr, pad_scr):
    tn = lg.shape[1]
    tile_lane = lax.broadcasted_iota(I32, (N_EXPERTS, LANES), 1)
    s = jax.nn.sigmoid(lg)
    sb = s + br_ref[...]
    rows_s = [s[e:e + 1, :] for e in range(N_EXPERTS)]
    rows_b = [sb[e:e + 1, :] for e in range(N_EXPERTS)]
    gscore, gi1, gi2 = [], [], []
    epg = EXPERTS_PER_GROUP
    for g in range(N_GROUPS):
        a = rows_b[g * epg:(g + 1) * epg]
        m1 = functools.reduce(jnp.maximum, a)
        i1 = _first_index(a, m1)
        rest = [jnp.where(i1 == i, -jnp.inf, a[i]) for i in range(epg)]
        m2 = functools.reduce(jnp.maximum, rest)
        i2 = _first_index(rest, m2)
        gscore.append(m1 + m2)
        gi1.append(i1)
        gi2.append(i2)
    gm = functools.reduce(jnp.maximum, gscore)
    gsel = _first_index(gscore, gm)
    i1 = gi1[N_GROUPS - 1]
    i2 = gi2[N_GROUPS - 1]
    for g in range(N_GROUPS - 2, -1, -1):
        i1 = jnp.where(gsel == g, gi1[g], i1)
        i2 = jnp.where(gsel == g, gi2[g], i2)
    idx0 = gsel * epg + i1
    idx1 = gsel * epg + i2
    s0 = jnp.zeros_like(rows_s[0])
    s1 = jnp.zeros_like(rows_s[0])
    for e in range(N_EXPERTS):
        s0 = jnp.where(idx0 == e, rows_s[e], s0)
        s1 = jnp.where(idx1 == e, rows_s[e], s1)
    tot = s0 + s1
    w0 = s0 / tot
    w1 = s1 / tot

    eidx = lax.broadcasted_iota(I32, (N_EXPERTS, tn), 0)
    oh0 = eidx == idx0
    oh1 = eidx == idx1
    oh = jnp.where(oh0 | oh1, 1.0, 0.0)
    rr = lax.broadcasted_iota(I32, (tn, tn), 0)
    cc = lax.broadcasted_iota(I32, (tn, tn), 1)
    upper = jnp.where(rr < cc, 1.0, 0.0).astype(BF16)
    before = jnp.dot(oh.astype(BF16), upper, preferred_element_type=F32)

    @pl.when(phase == 0)
    def _():
        cnt_scr[...] = cnt_scr[...] + jnp.where(tile_lane == t, jnp.sum(oh, axis=1, keepdims=True), 0.0)

    pad_col = jnp.sum(jnp.where(tile_lane == t, pad_scr[...], 0.0), axis=1, keepdims=True)
    run = jnp.zeros((1, 1), F32)
    offs = []
    for e in range(N_EXPERTS):
        offs.append(run)
        run = run + pad_col[e:e + 1, :]
    pos = before + jnp.concatenate(offs, axis=0)
    spos0 = jnp.sum(jnp.where(oh0, pos, 0.0), axis=0, keepdims=True)
    spos1 = jnp.sum(jnp.where(oh1, pos, 0.0), axis=0, keepdims=True)
    zi = jnp.zeros((SUBLANES - 4, tn), I32)
    io_ref[:, lanes] = jnp.concatenate([idx0, idx1, spos0.astype(I32), spos1.astype(I32), zi], axis=0)
    wo_ref[:, lanes] = jnp.concatenate([w0, w1, jnp.zeros((SUBLANES - 2, tn), F32)], axis=0)


def _route_call(logits_t, b_router, tn):
    ne, t_all = logits_t.shape
    ntile = t_all // tn
    assert ntile <= LANES
    group = max(g for g in range(1, ROUTE_TILES_PER_STEP + 1) if ntile % g == 0)
    tw = group * tn
    tok_out = pl.BlockSpec((SUBLANES, tw), lambda p, t: (0, t * p))
    tab_out = pl.BlockSpec((ne, LANES), lambda p, t: (0, 0))
    tab = jax.ShapeDtypeStruct((ne, LANES), I32)
    return pl.pallas_call(
        functools.partial(_route_kernel, tn=tn),
        grid=(2, ntile // group),
        in_specs=[pl.BlockSpec((ne, tw), lambda p, t: (0, t)), pl.BlockSpec((ne, 1), lambda p, t: (0, 0))],
        out_specs=[tok_out, tok_out, tab_out, tab_out, tab_out],
        out_shape=[
            jax.ShapeDtypeStruct((SUBLANES, t_all), I32),
            jax.ShapeDtypeStruct((SUBLANES, t_all), F32),
            tab, tab, tab,
        ],
        scratch_shapes=[pltpu.VMEM((ne, LANES), F32)] * 3,
        compiler_params=_cparams(("arbitrary", "arbitrary")),
        name="route",
    )(logits_t, b_router.reshape(ne, 1))


def _segment_copies(seg_ref, pad_ref, t, stage, hbm, sem, to_hbm, wait):
    if wait:
        total = functools.reduce(lambda a, b: a + b, [pad_ref[e * LANES + t] for e in range(N_EXPERTS)])
        for size in (2 * SEG_SIZES[0],) + SEG_SIZES:
            @pl.when((total & size) != 0)
            def _(size=size):
                src, dst = stage.at[pl.ds(0, size), :], hbm.at[pl.ds(0, size), :]
                if not to_hbm:
                    src, dst = dst, src
                pltpu.make_async_copy(src, dst, sem).wait()
        return
    loc = 0
    for e in range(N_EXPERTS):
        n = pad_ref[e * LANES + t]
        start = seg_ref[e * LANES + t]
        for size in SEG_SIZES:
            off = n & (-2 * size)

            @pl.when((n & size) != 0)
            def _(off=off, size=size, loc=loc, start=start):
                s_rows = pl.ds(pl.multiple_of(loc + off, SUBLANES), size)
                h_rows = pl.ds(pl.multiple_of(start + off, SUBLANES), size)
                src, dst = (stage.at[s_rows, :], hbm.at[h_rows, :])
                if not to_hbm:
                    src, dst = dst, src
                pltpu.make_async_copy(src, dst, sem).start()
        loc = loc + n


def _sort_matrix(io_ref, rows):
    r = lax.broadcasted_iota(I32, (rows, io_ref.shape[1]), 0)
    return r == io_ref[2:3, :], r == io_ref[3:4, :]


def _dispatch_kernel(seg_ref, pad_ref, zblk_ref, *refs, tiles):
    srcs = refs[:len(tiles)]
    io_ref, wo_ref, xb_hbm, stage, zbuf, sem = refs[len(tiles):]
    i = pl.program_id(0)
    m = MOE_BLOCK

    def zero_copy(j):
        start = pl.multiple_of(zblk_ref[j] * m, m)
        return pltpu.make_async_copy(zbuf, xb_hbm.at[pl.ds(start, m), :], sem.at[0])

    @pl.when(i == 0)
    def _():
        zbuf[...] = jnp.zeros_like(zbuf)
        for j in range(zblk_ref.shape[0]):
            @pl.when(zblk_ref[j] >= 0)
            def _():
                zero_copy(j).start()
        for j in range(zblk_ref.shape[0]):
            @pl.when(zblk_ref[j] >= 0)
            def _():
                zero_copy(j).wait()

    hp = srcs[0][...]
    if len(srcs) == 2:
        hp = jnp.where(i < tiles[0], hp, srcs[1][...])
    x = _unpack_bf16_pairs(hp)
    m0, m1 = _sort_matrix(io_ref, STAGE_ROWS)
    sort = jnp.where(m0 | m1, 1.0, 0.0).astype(BF16)
    xs = jnp.dot(sort, x, preferred_element_type=F32)
    dh = x.shape[1] // 2
    slot = i % 2
    cur = stage.at[slot]
    cur[:, :dh] = _pack_rounded_pairs(xs)
    ws = jnp.sum(jnp.where(m0, wo_ref[0:1, :], 0.0) + jnp.where(m1, wo_ref[1:2, :], 0.0), axis=1, keepdims=True)
    cur[:, dh:] = jnp.broadcast_to(lax.bitcast_convert_type(ws, U32), (STAGE_ROWS, LANES))
    _segment_copies(seg_ref, pad_ref, i, cur, xb_hbm, sem.at[slot], to_hbm=True, wait=False)

    @pl.when(i > 0)
    def _():
        _segment_copies(seg_ref, pad_ref, i - 1, stage.at[1 - slot], xb_hbm, sem.at[1 - slot], to_hbm=True, wait=True)

    @pl.when(i == pl.num_programs(0) - 1)
    def _():
        _segment_copies(seg_ref, pad_ref, i, cur, xb_hbm, sem.at[slot], to_hbm=True, wait=True)


def _dispatch_call(seg, pad, zblk, io, wo, sources, p_rows):
    tile = ROUTE_TILE
    dh = sources[0].shape[1]
    tiles = tuple(s.shape[0] // tile for s in sources)
    firsts = tuple(sum(tiles[:k]) for k in range(len(tiles)))

    def src_spec(first, ntile):
        return pl.BlockSpec((tile, dh), lambda i, *_: (jnp.clip(i - first, 0, ntile - 1), 0))

    tok_spec = pl.BlockSpec((SUBLANES, tile), lambda i, *_: (0, i))
    grid_spec = pltpu.PrefetchScalarGridSpec(
        num_scalar_prefetch=3,
        grid=(sum(tiles),),
        in_specs=[src_spec(f, n) for f, n in zip(firsts, tiles)] + [tok_spec, tok_spec],
        out_specs=pl.BlockSpec(memory_space=pl.ANY),
        scratch_shapes=[pltpu.VMEM((2, STAGE_ROWS, dh + LANES), U32), pltpu.VMEM((MOE_BLOCK, dh + LANES), U32),
                        pltpu.SemaphoreType.DMA((2,))],
    )
    return pl.pallas_call(
        functools.partial(_dispatch_kernel, tiles=tiles),
        grid_spec=grid_spec,
        out_shape=jax.ShapeDtypeStruct((p_rows, dh + LANES), U32),
        compiler_params=_cparams(("arbitrary",)),
        name="dispatch",
    )(seg, pad, zblk, *sources, io, wo)


def _expert_kernel(be_ref, nu_ref, va_ref, ne_ref, x_ref, wg_hbm, wu_hbm, wd_hbm, o_ref,
                   wg_b, wu_b, wd_b, wg_f, wu_f, wd_f, switch_ref, sem, *, layer):
    j = pl.program_id(0)
    dh = o_ref.shape[1]
    half = o_ref.shape[0] // 2
    valid = va_ref[j]
    e = be_ref[j]

    def weight_copies(expert, slot):
        return [pltpu.make_async_copy(hbm.at[layer, expert], buf.at[slot], sem.at[slot, k])
                for k, (hbm, buf) in enumerate(((wg_hbm, wg_f), (wu_hbm, wu_f), (wd_hbm, wd_f)))]

    @pl.when(j == 0)
    def _():
        switch_ref[0] = 0
        for cp in weight_copies(e, 0):
            cp.start()

    @pl.when((j == 0) | (e != be_ref[jnp.maximum(j - 1, 0)]))
    def _():
        slot = switch_ref[0] % 2
        for cp in weight_copies(e, slot):
            cp.wait()
        wg_b[...] = wg_f[slot].astype(BF16)
        wu_b[...] = wu_f[slot].astype(BF16)
        wd_b[...] = wd_f[slot].astype(BF16)
        nxt = ne_ref[e]

        @pl.when(nxt >= 0)
        def _():
            for cp in weight_copies(nxt, 1 - slot):
                cp.start()
        switch_ref[0] = switch_ref[0] + 1

    def ffn_halves(halves):
        gates = []
        for r in halves:
            x = _unpack_bf16_pairs(x_ref[r * half:(r + 1) * half, :dh])
            gates.append((jnp.dot(x, wg_b[...], preferred_element_type=F32),
                          jnp.dot(x, wu_b[...], preferred_element_type=F32)))
        for r, (a, u) in zip(halves, gates):
            hmid = (a * jax.nn.sigmoid(a)) * u
            y = jnp.dot(hmid.astype(BF16), wd_b[...], preferred_element_type=F32)
            row_w = lax.bitcast_convert_type(x_ref[r * half:(r + 1) * half, dh:dh + 1], F32)
            o_ref[r * half:(r + 1) * half, :] = _pack_bf16_pairs(y * row_w)

    @pl.when(valid > half)
    def _():
        ffn_halves((0, 1))

    @pl.when((valid > 0) & (valid <= half))
    def _():
        ffn_halves((0,))
        o_ref[half:, :] = jnp.zeros((half, dh), o_ref.dtype)

    @pl.when(valid == 0)
    def _():
        o_ref[...] = jnp.zeros_like(o_ref)


def _expert_call(block_e, nused, valid, next_e, xb, wg, wu, wd, layer):
    p_rows = xb.shape[0]
    dh = xb.shape[1] - LANES
    d = 2 * dh
    m = MOE_BLOCK
    nb = p_rows // m
    de = wg.shape[-1]

    def xmap(j, be, nu, va, ne):
        return (jnp.minimum(j, nu[0] - 1), 0)

    any_spec = pl.BlockSpec(memory_space=pl.ANY)
    grid_spec = pltpu.PrefetchScalarGridSpec(
        num_scalar_prefetch=4,
        grid=(nb,),
        in_specs=[pl.BlockSpec((m, dh + LANES), xmap), any_spec, any_spec, any_spec],
        out_specs=pl.BlockSpec((m, dh), lambda j, be, nu, va, ne: (j, 0)),
        scratch_shapes=[
            pltpu.VMEM((d, de), BF16), pltpu.VMEM((d, de), BF16), pltpu.VMEM((de, d), BF16),
            pltpu.VMEM((2, d, de), F32), pltpu.VMEM((2, d, de), F32), pltpu.VMEM((2, de, d), F32),
            pltpu.SMEM((1,), I32), pltpu.SemaphoreType.DMA((2, 3)),
        ],
    )
    return pl.pallas_call(
        functools.partial(_expert_kernel, layer=layer),
        grid_spec=grid_spec,
        out_shape=jax.ShapeDtypeStruct((p_rows, dh), U32),
        compiler_params=_cparams(("arbitrary",)),
        name="experts",
    )(block_e, nused, valid, next_e, xb, wg, wu, wd)


def _combine_kernel(*refs, tile0, final):
    if final:
        seg_ref, pad_ref, yb_hbm, io_ref, x_ref, g2_ref, fn_ref, o_ref, stage, sem = refs
    else:
        seg_ref, pad_ref, yb_hbm, io_ref, x_ref, g2_ref, o_ref, stage, sem = refs
    i = pl.program_id(0)

    slot = i % 2

    @pl.when(i == 0)
    def _():
        stage[...] = jnp.zeros_like(stage)
        _segment_copies(seg_ref, pad_ref, tile0, stage.at[0], yb_hbm, sem.at[0], to_hbm=False, wait=False)

    @pl.when(i + 1 < pl.num_programs(0))
    def _():
        _segment_copies(seg_ref, pad_ref, tile0 + i + 1, stage.at[1 - slot], yb_hbm, sem.at[1 - slot],
                        to_hbm=False, wait=False)

    _segment_copies(seg_ref, pad_ref, tile0 + i, stage.at[slot], yb_hbm, sem.at[slot], to_hbm=False, wait=True)
    rows = _unpack_bf16_pairs(stage[slot])
    m0, m1 = _sort_matrix(io_ref, STAGE_ROWS)
    pick = jnp.where(m0 | m1, 1.0, 0.0).astype(BF16)
    y = lax.dot_general(pick, rows, (((0,), (0,)), ((), ())), preferred_element_type=F32)
    xo = x_ref[...] + g2_ref[...] * y
    if final:
        ms = jnp.mean(xo * xo, axis=-1, keepdims=True)
        xo = (xo * lax.rsqrt(ms + NORM_EPS)) * fn_ref[...]
    o_ref[...] = xo


def _combine_call(seg, pad, yb, io, x2d, mods5, layer, row_of_tile, final_g, tile0):
    n, d = x2d.shape
    tm = ROUTE_TILE
    final = final_g is not None
    in_specs = [
        pl.BlockSpec(memory_space=pl.ANY),
        pl.BlockSpec((SUBLANES, tm), lambda t, *_: (0, tile0 + t)),
        pl.BlockSpec((tm, d), lambda t, *_: (t, 0)),
        pl.BlockSpec((None, None, None, 1, d), lambda t, *_: (layer, row_of_tile(t), 5, 0, 0)),
    ]
    args = [yb, io, x2d, mods5]
    if final:
        in_specs.append(pl.BlockSpec((1, d), lambda t, *_: (0, 0)))
        args.append(final_g.reshape(1, d))
    grid_spec = pltpu.PrefetchScalarGridSpec(
        num_scalar_prefetch=2,
        grid=(n // tm,),
        in_specs=in_specs,
        out_specs=pl.BlockSpec((tm, d), lambda t, *_: (t, 0)),
        scratch_shapes=[pltpu.VMEM((2, STAGE_ROWS, d // 2), U32), pltpu.SemaphoreType.DMA((2,))],
    )
    return pl.pallas_call(
        functools.partial(_combine_kernel, tile0=tile0, final=final),
        grid_spec=grid_spec,
        out_shape=jax.ShapeDtypeStruct((n, d), F32),
        compiler_params=_cparams(("arbitrary",)),
        name="combine",
    )(seg, pad, *args)


def _pack_w_in(w):
    assert sum(IN_SIZES[:6]) == C_Z + 2 * GLA_GATE_RANK and sum(IN_SIZES) - sum(IN_SIZES[:6]) == C_END - C_SQ
    wb = w.astype(BF16)
    cut = C_Z + 2 * GLA_GATE_RANK
    out = jnp.zeros(w.shape[:-1] + (C_END,), BF16)
    out = lax.dynamic_update_slice(out, wb[..., :cut], (0, 0, 0))
    return lax.dynamic_update_slice(out, wb[..., cut:], (0, 0, C_SQ))


def _pad_up(up, row0):
    up = up.reshape(GLA_GATE_RANK, GLA_HEADS // 2, LANES).transpose(1, 0, 2)
    hi = up.astype(BF16)
    lo = (up - hi.astype(F32)).astype(BF16)
    out = jnp.zeros((GLA_HEADS // 2, LANES, LANES), BF16)
    for group, part in enumerate((hi, hi, lo)):
        r = group * 2 * GLA_GATE_RANK + row0
        out = out.at[:, r:r + GLA_GATE_RANK, :].set(part)
    return out


def _rope_tables(seq):
    rows = seq // GRID_W
    dim = jnp.arange(LANES, dtype=I32) % SWA_HD
    inv = ROPE_THETA ** (-((dim % ROPE_HALF).astype(F32) * 2.0 / ROPE_AXIS_DIM))
    sign = jnp.where(dim % ROPE_AXIS_DIM < ROPE_HALF, -1.0, 1.0)
    by_row = (dim < ROPE_AXIS_DIM)[None, None, :]
    ang_r = (jnp.arange(rows, dtype=F32)[:, None] * inv[None, :])[:, None, :]
    ang_c = (jnp.arange(GRID_W, dtype=F32)[:, None] * inv[None, :])[None, :, :]
    cos = jnp.where(by_row, jnp.cos(ang_r), jnp.cos(ang_c))
    sin = jnp.where(by_row, jnp.sin(ang_r), jnp.sin(ang_c)) * sign
    return cos.reshape(seq, LANES), sin.reshape(seq, LANES)


def _tile(n, pref):
    t = pref
    while n % t:
        t //= 2
    return t


def kernel(x, c, ctx, c_ctx, w_ada, b_ada, norm1, norm2, w_in, gla_up_f, gla_bias_f, gla_up_b, gla_bias_b,
           gla_norm, swa_sink, w_out, w_router, b_router, w_gate, w_up, w_down, final_norm):
    batch, seq, d = x.shape
    lc = ctx.shape[1]
    depth = w_ada.shape[0]
    n_l = batch * seq
    n_c = batch * lc
    npair = GLA_HEADS // 2
    assert batch + 1 <= SUBLANES
    assert n_l % ROUTE_TILE == 0 and n_c % ROUTE_TILE == 0

    tm_l = _tile(seq, ROW_TILE)
    tm_in = _tile(seq, INPROJ_TILE)
    tm_c = _tile(lc, CTX_TILE)
    tl_l = _tile(seq, GLA_TILE)
    tl_c = _tile(lc, ROW_TILE)

    rows = jnp.zeros((SUBLANES, d), F32).at[:batch].set(c).at[batch].set(c_ctx)
    mods = _ada_call(rows, w_ada, b_ada)
    mods5 = mods.reshape(depth, SUBLANES, 6, 1, d)

    tabs = _rope_tables(seq)
    w_packed = _pack_w_in(w_in)
    w_out_b = w_out.astype(BF16)
    wr_pad = jnp.zeros((d, LANES), F32).at[:, :N_EXPERTS].set(w_router)
    wr_hi = wr_pad.astype(BF16)
    wr_lo = (wr_pad - wr_hi.astype(F32)).astype(BF16)
    wr_cat = jnp.concatenate([wr_hi, wr_lo], axis=-1)

    def lat_row(tm):
        return lambda t: t // (seq // tm)

    def ctx_row(tm):
        return lambda t: batch

    xl = x.reshape(n_l, d)
    xc = ctx.reshape(n_c, d)
    for i in range(depth):
        last = i == depth - 1
        upf = _pad_up(gla_up_f[i], 0)
        upb = _pad_up(gla_up_b[i], GLA_GATE_RANK)
        bias_f = gla_bias_f[i].reshape(npair, 1, LANES)
        bias_b = gla_bias_b[i].reshape(npair, 1, LANES)

        c_qk, c_v, c_g, c_z, c_sq, c_sk, c_sv = _inproj_call(
            xc, norm1[i], mods5, i, ctx_row(tm_c), w_packed, None, tm_c, lc // tm_c)
        l_qk, l_v, l_g, l_z, l_sq, l_sk, l_sv = _inproj_call(
            xl, norm1[i], mods5, i, lat_row(tm_in), w_packed, tabs, tm_in, seq // tm_in)

        s_zero = jnp.zeros((batch, npair, 2, 2 * GLA_DV, LANES), F32)
        oc_f, oc_b, s_ctx = _gla_call(c_qk, c_v, c_z, upf, upb, bias_f, bias_b, s_zero, batch, lc, tl_c)
        ol_f, ol_b, _ = _gla_call(l_qk, l_v, l_z, upf, upb, bias_f, bias_b, s_ctx, batch, seq, tl_l)

        swa_l = _swa_call(l_sq, l_sk, l_sv, c_sk, c_sv, swa_sink[i], batch, seq, lc)
        xl, h2l, lg_l = _outproj_call(ol_f, ol_b, l_g, gla_norm[i], swa_l, w_out_b, xl, mods5, i, lat_row(tm_l),
                                      norm2[i], wr_cat, tm_l)
        if last:
            logits_t = lg_l
            t_all = n_l
        else:
            swa_c = _swa_ctx_call(c_sq, c_sk, c_sv, swa_sink[i], batch, lc)
            xc, h2c, lg_c = _outproj_call(oc_f, oc_b, c_g, gla_norm[i], swa_c, w_out_b, xc, mods5, i, ctx_row(tm_c),
                                          norm2[i], wr_cat, tm_c)
            logits_t = jnp.concatenate([lg_l, lg_c], axis=1)
            t_all = n_l + n_c

        io, wo, seg, pad, blk = _route_call(logits_t, b_router, ROUTE_TILE)
        m = MOE_BLOCK
        padded = blk[:, 0]
        pend = jnp.cumsum(padded)
        ntile = t_all // ROUTE_TILE
        nb = -(-(t_all * TOP_K + ntile * N_EXPERTS * (SUBLANES - 1)) // m) + N_EXPERTS
        p_rows = nb * m
        seg = seg.reshape(-1)
        pad = pad.reshape(-1)
        nused = (pend[-1:] // m).astype(I32)
        tail = nused + jnp.arange(nb - (t_all * TOP_K) // m, dtype=I32)
        zblk = jnp.concatenate([jnp.where(padded > 0, pend // m - 1, -1),
                                jnp.where(tail < nb, tail, -1)]).astype(I32)
        blk_row = jnp.minimum(jnp.arange(nb, dtype=I32), nused - 1) * m
        block_e = jnp.sum((pend[None, :] <= blk_row[:, None]).astype(I32), axis=1)
        rows_e = jnp.sum(pad.reshape(N_EXPERTS, LANES), axis=1)
        own = (block_e[:, None] == jnp.arange(N_EXPERTS, dtype=I32)[None, :]).astype(I32)
        seg_end = jnp.sum(own * (pend - padded + rows_e)[None, :], axis=1)
        blk_idx = jnp.arange(nb, dtype=I32)
        valid = jnp.where(blk_idx < nused, jnp.clip(seg_end - blk_idx * m, 0, m), 0).astype(I32)

        sources = (h2l,) if last else (h2l, h2c)
        xb = _dispatch_call(seg, pad, zblk, io, wo, sources, p_rows)
        eid = jnp.arange(N_EXPERTS, dtype=I32)
        later = (eid[None, :] > eid[:, None]) & (padded[None, :] > 0)
        next_e = jnp.min(jnp.where(later, eid[None, :], N_EXPERTS), axis=1)
        next_e = jnp.where(next_e < N_EXPERTS, next_e, -1).astype(I32)
        yb = _expert_call(block_e, nused, valid, next_e, xb, w_gate, w_up, w_down, i)
        xl_new = _combine_call(seg, pad, yb, io, xl, mods5, i, lat_row(ROUTE_TILE),
                               final_norm if last else None, 0)
        if not last:
            xc = _combine_call(seg, pad, yb, io, xc, mods5, i, ctx_row(0), None, n_l // ROUTE_TILE)
        xl = xl_new
    return xl.reshape(batch, seq, d)
```

```python
import functools

import jax
import jax.numpy as jnp
from jax import lax
from jax.experimental import pallas as pl
from jax.experimental.pallas import tpu as pltpu

F32 = jnp.float32
BF16 = jnp.bfloat16
I32 = jnp.int32
U32 = jnp.uint32
HIGHEST = lax.Precision.HIGHEST

GRID_W = 64
NORM_EPS = 1e-6
GLA_HEADS = 4
GLA_DK = 64
GLA_DV = 128
GLA_KW = GLA_HEADS * GLA_DK
GLA_WIDTH = GLA_HEADS * GLA_DV
GLA_GATE_RANK = 16
GLA_TAU = 16.0
GLA_CHUNK = 64
SWA_HEADS = 8
SWA_KV_HEADS = 2
SWA_HD = 64
SWA_WIDTH = SWA_HEADS * SWA_HD
SWA_KVW = SWA_KV_HEADS * SWA_HD
WINDOW = 128
ATT_BLOCK = 128
ROPE_THETA = 10000.0
ROPE_AXIS_DIM = SWA_HD // 2
ROPE_HALF = ROPE_AXIS_DIM // 2
N_EXPERTS = 16
N_GROUPS = 4
EXPERTS_PER_GROUP = N_EXPERTS // N_GROUPS
TOP_K = 2
IN_SIZES = (GLA_KW, GLA_KW, GLA_WIDTH, GLA_WIDTH, GLA_GATE_RANK, GLA_GATE_RANK, SWA_WIDTH, SWA_KVW, SWA_KVW)

LANES = 128
SUBLANES = 8
VMEM_LIMIT = 48 * 1024 * 1024

ROW_TILE = 512
GLA_TILE = 1024
INPROJ_TILE = 1024
CTX_TILE = 256
ADA_COLS = 1536
SWA_BLOCKS_PER_STEP = 4
MOE_BLOCK = 512
ROUTE_TILE = 512
ROUTE_TILES_PER_STEP = 4
SEG_SIZES = (512, 256, 128, 64, 32, 16, 8)
STAGE_ROWS = TOP_K * ROUTE_TILE + LANES

C_QK = 0
C_V = 512
C_G = 1024
C_Z = 1536
C_SQ = 1664
C_SK = 2176
C_SV = 2304
C_END = 2432
SV_LANES = 4 * LANES
NEG_BIG = -1e30

assert WINDOW == ATT_BLOCK


def _cparams(sem):
    return pltpu.CompilerParams(dimension_semantics=sem, vmem_limit_bytes=VMEM_LIMIT)


def _pack_bf16_pairs(x):
    return _pack_rounded_pairs(x.astype(BF16).astype(F32))


def _pack_rounded_pairs(xr):
    n = xr.shape[1] // 2
    lo = lax.bitcast_convert_type(xr[:, :n], U32)
    hi = lax.bitcast_convert_type(xr[:, n:], U32)
    return (lo >> 16) | hi


def _unpack_bf16_pairs(w):
    lo = lax.bitcast_convert_type(w << 16, F32)
    hi = lax.bitcast_convert_type(w & jnp.uint32(0xFFFF0000), F32)
    return jnp.concatenate([lo.astype(BF16), hi.astype(BF16)], axis=-1)


def _ada_kernel(a_ref, w_ref, b_ref, o_ref):
    a = a_ref[...]
    act = a * jax.nn.sigmoid(a)
    o_ref[...] = jnp.dot(act, w_ref[...], preferred_element_type=F32, precision=HIGHEST) + b_ref[...]


def _ada_call(rows, w_ada, b_ada):
    depth, d, n6 = w_ada.shape
    tn = ADA_COLS
    return pl.pallas_call(
        _ada_kernel,
        grid=(depth, n6 // tn),
        in_specs=[
            pl.BlockSpec((SUBLANES, d), lambda l, j: (0, 0)),
            pl.BlockSpec((None, d, tn), lambda l, j: (l, 0, j)),
            pl.BlockSpec((None, 1, tn), lambda l, j: (l, 0, j)),
        ],
        out_specs=pl.BlockSpec((None, SUBLANES, tn), lambda l, j: (l, 0, j)),
        out_shape=jax.ShapeDtypeStruct((depth, SUBLANES, n6), F32),
        compiler_params=_cparams(("arbitrary", "arbitrary")),
        name="adaln",
    )(rows, w_ada, b_ada.reshape(depth, 1, n6))


def _rope_tile(xj, cos, sin, lane_lo):
    partner = jnp.where(lane_lo, pltpu.roll(xj, LANES - ROPE_HALF, 1), pltpu.roll(xj, ROPE_HALF, 1))
    return xj * cos + partner * sin


def _inproj_kernel(*refs, rope):
    if rope:
        (x_ref, g_ref, sc_ref, sh_ref, w_ref, cos_ref, sin_ref,
         qk_ref, v_ref, gate_ref, z_ref, sq_ref, sk_ref, sv_ref) = refs
    else:
        (x_ref, g_ref, sc_ref, sh_ref, w_ref,
         qk_ref, v_ref, gate_ref, z_ref, sq_ref, sk_ref, sv_ref) = refs
    half = x_ref.shape[0] // 2
    scale = g_ref[...] * (1.0 + sc_ref[...])
    normed = []
    for r in range(2):
        x = x_ref[r * half:(r + 1) * half, :]
        ms = jnp.mean(x * x, axis=-1, keepdims=True)
        normed.append(((x * lax.rsqrt(ms + NORM_EPS)) * scale + sh_ref[...]).astype(BF16))

    for r in range(2):
        rows = slice(r * half, (r + 1) * half)
        hb = normed[r]

        def proj(a, b):
            return jnp.dot(hb, w_ref[:, a:b], preferred_element_type=F32)

        qk = proj(C_QK, C_V)
        qk_ref[rows, :GLA_KW] = (qk[:, :GLA_KW] * (GLA_DK ** -0.5)).astype(qk_ref.dtype)
        qk_ref[rows, GLA_KW:] = qk[:, GLA_KW:].astype(qk_ref.dtype)
        v_ref[rows, :] = proj(C_V, C_G).astype(v_ref.dtype)
        gate_ref[rows, :] = proj(C_G, C_Z).astype(gate_ref.dtype)
        z_ref[rows, :] = proj(C_Z, C_SQ)
        sq = proj(C_SQ, C_SK) * (SWA_HD ** -0.5)
        sk = proj(C_SK, C_SV)
        sv = proj(C_SV, C_END)
        if rope:
            cos = cos_ref[rows, :]
            sin = sin_ref[rows, :]
            lane_lo = (lax.broadcasted_iota(I32, cos.shape, 1) % ROPE_AXIS_DIM) < ROPE_HALF
            for j in range(SWA_WIDTH // LANES):
                sl = slice(j * LANES, (j + 1) * LANES)
                sq_ref[rows, sl] = _rope_tile(sq[:, sl], cos, sin, lane_lo).astype(sq_ref.dtype)
            sk = _rope_tile(sk, cos, sin, lane_lo)
        else:
            sq_ref[rows, :] = sq.astype(sq_ref.dtype)
        first = lax.broadcasted_iota(I32, sk.shape, 1) < SWA_HD
        sk_sw = pltpu.roll(sk, SWA_HD, 1)
        sk_ref[rows, :LANES] = jnp.where(first, sk, sk_sw).astype(sk_ref.dtype)
        sk_ref[rows, LANES:] = jnp.where(first, sk_sw, sk).astype(sk_ref.dtype)
        sv_sw = pltpu.roll(sv, SWA_HD, 1)
        sv_ref[rows, 0 * LANES:1 * LANES] = jnp.where(first, sv, 1.0).astype(sv_ref.dtype)
        sv_ref[rows, 1 * LANES:2 * LANES] = jnp.where(first, 1.0, sv_sw).astype(sv_ref.dtype)
        sv_ref[rows, 2 * LANES:3 * LANES] = jnp.where(first, sv_sw, 1.0).astype(sv_ref.dtype)
        sv_ref[rows, 3 * LANES:4 * LANES] = jnp.where(first, 1.0, sv).astype(sv_ref.dtype)


def _inproj_call(x2d, g, mods5, layer, row_of_tile, w_packed, tabs, tm, tiles_per_seq):
    n, d = x2d.shape
    nt = n // tm
    rope = tabs is not None

    def mod_spec(k):
        return pl.BlockSpec((None, None, None, 1, d), lambda t: (layer, row_of_tile(t), k, 0, 0))

    in_specs = [
        pl.BlockSpec((tm, d), lambda t: (t, 0)),
        pl.BlockSpec((1, d), lambda t: (0, 0)),
        mod_spec(1), mod_spec(0),
        pl.BlockSpec((None, d, C_END), lambda t: (layer, 0, 0)),
    ]
    args = [x2d, g.reshape(1, d), mods5, mods5, w_packed]
    if rope:
        in_specs += [pl.BlockSpec((tm, LANES), lambda t: (t % tiles_per_seq, 0))] * 2
        args += [tabs[0], tabs[1]]
    widths = (2 * GLA_KW, GLA_WIDTH, GLA_WIDTH, LANES, SWA_WIDTH, 2 * SWA_KVW, SV_LANES)
    dtypes = (BF16, BF16, BF16, F32, BF16, BF16, BF16)
    return pl.pallas_call(
        functools.partial(_inproj_kernel, rope=rope),
        grid=(nt,),
        in_specs=in_specs,
        out_specs=[pl.BlockSpec((tm, w), lambda t: (t, 0)) for w in widths],
        out_shape=[jax.ShapeDtypeStruct((n, w), dt) for w, dt in zip(widths, dtypes)],
        compiler_params=_cparams(("arbitrary",)),
        name="inproj",
    )(*args)


def _log_sigmoid(x):
    return jnp.minimum(x, 0.0) - jnp.log(1.0 + jnp.exp(-jnp.abs(x)))


def _gla_prepare(q_ref, k_ref, v_ref, z_ref, up_ref, bias_ref, tri_ref, scr, fwd):
    qin_s, km0_s, km1_s, kout_s, vm0_s, vm1_s, dec_s = scr[:7]
    qbd_s = scr[9]
    tl = q_ref.shape[0]
    c = GLA_CHUNK
    z = z_ref[...]
    z_hi = z.astype(BF16).astype(F32)
    zc = z_hi + pltpu.roll(z - z_hi, 2 * GLA_GATE_RANK, 1) + pltpu.roll(z_hi, 4 * GLA_GATE_RANK, 1)
    x = jnp.dot(zc.astype(BF16), up_ref[...], preferred_element_type=F32) + bias_ref[...]
    la = _log_sigmoid(x) * (1.0 / GLA_TAU)
    hi = la.astype(BF16)
    lo = (la - hi.astype(F32)).astype(BF16)
    sums = jnp.dot(tri_ref[...], jnp.concatenate([hi, lo], axis=1), preferred_element_type=F32)
    b = sums[:, :LANES] + sums[:, LANES:]
    b3 = b.reshape(tl // c, c, LANES)
    edge = b3[:, c - 1:c, :] if fwd else b3[:, 0:1, :]
    btot = jnp.broadcast_to(edge, b3.shape).reshape(tl, LANES)
    q = q_ref[...].astype(F32)
    k = k_ref[...].astype(F32)
    first = (lax.broadcasted_iota(I32, (tl, LANES), 0) % (2 * c)) < c
    q_in = q * jnp.exp(b)
    qin_s[...] = q_in.astype(BF16)
    qbd_s[...] = jnp.concatenate([jnp.where(first, q_in, 0.0), jnp.where(first, 0.0, q_in)], axis=1).astype(BF16)
    k_in = k * jnp.exp(-b)
    head0 = lax.broadcasted_iota(I32, (tl, LANES), 1) < GLA_DK
    km0_s[...] = jnp.where(head0, k_in, 0.0).astype(BF16)
    km1_s[...] = jnp.where(head0, 0.0, k_in).astype(BF16)
    k_out = k * jnp.exp(btot - b)
    kout_s[...] = jnp.concatenate([jnp.where(first, k_out, 0.0), jnp.where(first, 0.0, k_out)], axis=1).astype(BF16)
    dec_s[...] = jnp.exp(btot)
    vf = v_ref[...].astype(F32)
    vhead0 = lax.broadcasted_iota(I32, (tl, 2 * GLA_DV), 1) < GLA_DV
    vm0_s[...] = jnp.where(vhead0, vf, 0.0).astype(BF16)
    vm1_s[...] = jnp.where(vhead0, 0.0, vf).astype(BF16)


def _gla_increment(v_ref, scr, pair):
    kout_s, u_s = scr[3], scr[7]
    rows = pl.ds(pair * 2 * GLA_CHUNK, 2 * GLA_CHUNK)
    u_t = lax.dot_general(v_ref[rows, :], kout_s[rows, :], (((0,), (0,)), ((), ())),
                          preferred_element_type=F32)
    srow = lax.broadcasted_iota(I32, (2 * GLA_DV, 2 * LANES), 0) // GLA_DV
    scol = (lax.broadcasted_iota(I32, (2 * GLA_DV, 2 * LANES), 1) % LANES) // GLA_DK
    u_s[pair] = jnp.where(srow == scol, u_t, 0.0)


def _gla_states(scr, s_ref, chunk_order):
    dec_s, u_s, sprev_s = scr[6], scr[7], scr[8]
    s_t = s_ref[...]
    for cidx in chunk_order:
        pair, lanes = cidx // 2, pl.ds((cidx % 2) * LANES, LANES)
        sprev_s[pair, :, lanes] = s_t.astype(BF16)
        r0 = cidx * GLA_CHUNK
        s_t = s_t * dec_s[r0:r0 + 1, :] + u_s[pair, :, lanes]
    s_ref[...] = s_t


def _gla_output(o_ref, scr, pair, fwd):
    qin_s, km0_s, km1_s, _, vm0_s, vm1_s, _, _, sprev_s, qbd_s = scr
    c = GLA_CHUNK
    r0 = pair * 2 * c
    rows = pl.ds(r0, 2 * c)
    ca, cb = pl.ds(r0, c), pl.ds(r0 + c, c)
    kst = jnp.concatenate([km0_s[ca, :], km1_s[ca, :], km0_s[cb, :], km1_s[cb, :]], axis=0)
    a = lax.dot_general(qin_s[rows, :], kst, (((1,), (1,)), ((), ())), preferred_element_type=F32)
    ri = lax.broadcasted_iota(I32, (2 * c, 4 * c), 0)
    ci = lax.broadcasted_iota(I32, (2 * c, 4 * c), 1)
    same_chunk = (ri // c) == (ci // (2 * c))
    keep = same_chunk & ((ci % c <= ri % c) if fwd else (ci % c >= ri % c))
    a = jnp.where(keep, a, 0.0).astype(BF16)
    vbd = jnp.concatenate([vm0_s[ca, :], vm1_s[ca, :], vm0_s[cb, :], vm1_s[cb, :]], axis=0)
    o = jnp.dot(a, vbd, preferred_element_type=F32)
    o = o + lax.dot_general(qbd_s[rows, :], sprev_s[pair], (((1,), (1,)), ((), ())), preferred_element_type=F32)
    o_ref[rows, :] = o.astype(o_ref.dtype)


def _gla_kernel(qf_ref, kf_ref, vf_ref, zf_ref, qb_ref, kb_ref, vb_ref, zb_ref,
                upf_ref, upb_ref, bf_ref, bb_ref, trif_ref, trib_ref, s0_ref,
                of_ref, ob_ref, sfin_ref, sf_scr, sb_scr, *scr, nchunk):
    i = pl.program_id(2)
    nt = pl.num_programs(2)
    scr_f, scr_b = scr[:len(scr) // 2], scr[len(scr) // 2:]

    @pl.when(i == 0)
    def _():
        sf_scr[...] = s0_ref[0]
        sb_scr[...] = s0_ref[1]

    _gla_prepare(qf_ref, kf_ref, vf_ref, zf_ref, upf_ref, bf_ref, trif_ref, scr_f, True)
    _gla_prepare(qb_ref, kb_ref, vb_ref, zb_ref, upb_ref, bb_ref, trib_ref, scr_b, False)
    for pair in range(nchunk // 2):
        _gla_increment(vf_ref, scr_f, pair)
        _gla_increment(vb_ref, scr_b, pair)
    _gla_states(scr_f, sf_scr, range(nchunk))
    _gla_states(scr_b, sb_scr, range(nchunk - 1, -1, -1))
    for pair in range(nchunk // 2):
        _gla_output(of_ref, scr_f, pair, True)
        _gla_output(ob_ref, scr_b, pair, False)

    @pl.when(i == nt - 1)
    def _():
        sfin_ref[0] = sf_scr[...]
        sfin_ref[1] = sb_scr[...]


def _gla_call(qk, v, z, upf_pad, upb_pad, bias_f, bias_b, s0, batch, seq, tl):
    n = qk.shape[0]
    nt = seq // tl
    npair = GLA_HEADS // 2

    def fwd_row(b, p, i):
        return b * nt + i

    def bwd_row(b, p, i):
        return b * nt + (nt - 1 - i)

    def specs(row):
        return [
            pl.BlockSpec((tl, LANES), lambda b, p, i: (row(b, p, i), p)),
            pl.BlockSpec((tl, LANES), lambda b, p, i: (row(b, p, i), npair + p)),
            pl.BlockSpec((tl, 2 * GLA_DV), lambda b, p, i: (row(b, p, i), p)),
            pl.BlockSpec((tl, LANES), lambda b, p, i: (row(b, p, i), 0)),
        ]

    in_specs = specs(fwd_row) + specs(bwd_row) + [
        pl.BlockSpec((None, LANES, LANES), lambda b, p, i: (p, 0, 0)),
        pl.BlockSpec((None, LANES, LANES), lambda b, p, i: (p, 0, 0)),
        pl.BlockSpec((None, 1, LANES), lambda b, p, i: (p, 0, 0)),
        pl.BlockSpec((None, 1, LANES), lambda b, p, i: (p, 0, 0)),
        pl.BlockSpec((None, tl, tl), lambda b, p, i: (0, 0, 0)),
        pl.BlockSpec((None, tl, tl), lambda b, p, i: (1, 0, 0)),
        pl.BlockSpec((None, None, 2, 2 * GLA_DV, LANES), lambda b, p, i: (b, p, 0, 0, 0)),
    ]
    ri = jnp.arange(tl)[:, None]
    ci = jnp.arange(tl)[None, :]
    same = (ri // GLA_CHUNK) == (ci // GLA_CHUNK)
    tri = jnp.stack([same & (ci <= ri), same & (ci >= ri)]).astype(BF16)
    npairs = tl // (2 * GLA_CHUNK)
    dir_scratch = [
        pltpu.VMEM((tl, LANES), BF16),
        pltpu.VMEM((tl, LANES), BF16),
        pltpu.VMEM((tl, LANES), BF16),
        pltpu.VMEM((tl, 2 * LANES), BF16),
        pltpu.VMEM((tl, 2 * GLA_DV), BF16),
        pltpu.VMEM((tl, 2 * GLA_DV), BF16),
        pltpu.VMEM((tl, LANES), F32),
        pltpu.VMEM((npairs, 2 * GLA_DV, 2 * LANES), F32),
        pltpu.VMEM((npairs, 2 * GLA_DV, 2 * LANES), BF16),
        pltpu.VMEM((tl, 2 * LANES), BF16),
    ]
    out_specs = [
        pl.BlockSpec((tl, 2 * GLA_DV), lambda b, p, i: (fwd_row(b, p, i), p)),
        pl.BlockSpec((tl, 2 * GLA_DV), lambda b, p, i: (bwd_row(b, p, i), p)),
        pl.BlockSpec((None, None, 2, 2 * GLA_DV, LANES), lambda b, p, i: (b, p, 0, 0, 0)),
    ]
    out_shape = [
        jax.ShapeDtypeStruct((n, GLA_WIDTH), BF16),
        jax.ShapeDtypeStruct((n, GLA_WIDTH), BF16),
        jax.ShapeDtypeStruct((batch, npair, 2, 2 * GLA_DV, LANES), F32),
    ]
    return pl.pallas_call(
        functools.partial(_gla_kernel, nchunk=tl // GLA_CHUNK),
        grid=(batch, npair, nt),
        in_specs=in_specs,
        out_specs=out_specs,
        out_shape=out_shape,
        scratch_shapes=[pltpu.VMEM((2 * GLA_DV, LANES), F32), pltpu.VMEM((2 * GLA_DV, LANES), F32)]
        + dir_scratch + dir_scratch,
        compiler_params=_cparams(("arbitrary", "arbitrary", "arbitrary")),
        name="gla",
    )(qk, qk, v, z, qk, qk, v, z, upf_pad, upb_pad, bias_f, bias_b, tri, tri, s0)


def _swa_block(sink_ref, q_ref, o_ref, row0, kall, vall, prev_ok, next_ok):
    w = ATT_BLOCK
    local = prev_ok is not None
    qrows = pl.ds(row0, w)
    r2 = lax.broadcasted_iota(I32, (2 * w, w), 0) % w
    c2 = lax.broadcasted_iota(I32, (2 * w, w), 1)
    if local:
        bias_prev = jnp.where((c2 >= r2) & prev_ok, 0.0, NEG_BIG)
        bias_next = jnp.where((c2 <= r2) & next_ok, 0.0, NEG_BIG)
    first = c2 < SWA_HD
    top = lax.broadcasted_iota(I32, (2 * w, 1), 0) < w
    scores = []
    for g in range(SWA_KV_HEADS):
        kd = kall[:, g * LANES:(g + 1) * LANES]
        qs = jnp.concatenate([q_ref[qrows, (2 * g) * LANES:(2 * g + 1) * LANES],
                              q_ref[qrows, (2 * g + 1) * LANES:(2 * g + 2) * LANES]], axis=0).astype(F32)
        for half in range(2):
            qm = (jnp.where(first, qs, 0.0) if half == 0 else jnp.where(first, 0.0, qs)).astype(BF16)
            s = lax.dot_general(qm, kd, (((1,), (1,)), ((), ())), preferred_element_type=F32)
            if local:
                s = jnp.concatenate([s[:, :w] + bias_prev, s[:, w:2 * w], s[:, 2 * w:3 * w] + bias_next,
                                     s[:, 3 * w:]], axis=1)
            scores.append(s)
    for g in range(SWA_KV_HEADS):
        outs = []
        for half in range(2):
            s = scores[2 * g + half]
            sk = jnp.where(top, sink_ref[4 * g + half], sink_ref[4 * g + 2 + half])
            m = jnp.maximum(jnp.max(s, axis=-1, keepdims=True), sk)
            p = jnp.exp((s - m).astype(BF16))
            va = vall[:, (2 * g + half) * LANES:(2 * g + half + 1) * LANES]
            acc = jnp.dot(p, va, preferred_element_type=F32)
            den = pltpu.roll(acc, SWA_HD, 1) + jnp.exp(sk - m)
            outs.append(acc / den)
        out = jnp.where(first, outs[0], outs[1])
        o_ref[qrows, (2 * g) * LANES:(2 * g + 1) * LANES] = out[:w].astype(o_ref.dtype)
        o_ref[qrows, (2 * g + 1) * LANES:(2 * g + 2) * LANES] = out[w:].astype(o_ref.dtype)


def _swa_kernel(*refs, nstep, local):
    w = ATT_BLOCK
    if not local:
        sink_ref, q_ref, kx_ref, vx_ref, o_ref = refs
        _swa_block(sink_ref, q_ref, o_ref, 0, kx_ref[...], vx_ref[...], None, None)
        return
    sink_ref, q_ref, kp_ref, kc_ref, kn_ref, vp_ref, vc_ref, vn_ref, kx_ref, vx_ref, o_ref = refs
    i = pl.program_id(1)
    nblk = q_ref.shape[0] // w
    k_blocks = [kp_ref[...]] + [kc_ref[j * w:(j + 1) * w, :] for j in range(nblk)] + [kn_ref[...]]
    v_blocks = [vp_ref[...]] + [vc_ref[j * w:(j + 1) * w, :] for j in range(nblk)] + [vn_ref[...]]
    exists = [i > 0] + [True] * nblk + [i < nstep - 1]
    for sub in range(nblk):
        kall = jnp.concatenate(k_blocks[sub:sub + 3] + [kx_ref[...]], axis=0)
        vall = jnp.concatenate(v_blocks[sub:sub + 3] + [vx_ref[...]], axis=0)
        _swa_block(sink_ref, q_ref, o_ref, sub * w, kall, vall, exists[sub], exists[sub + 2])


def _swa_call(sq, skd, svd, kcd, vcd, sink, batch, seq, lc):
    n = sq.shape[0]
    w = ATT_BLOCK
    nb = seq // w
    nblk = SWA_BLOCKS_PER_STEP if nb % SWA_BLOCKS_PER_STEP == 0 else 1
    nstep = nb // nblk
    kvw = 2 * SWA_KVW

    def pair(b, i):
        return (b * nstep + i, 0)

    def before(b, i):
        return (b * nb + jnp.maximum(nblk * i - 1, 0), 0)

    def after(b, i):
        return (b * nb + jnp.minimum(nblk * (i + 1), nb - 1), 0)

    def kv_specs(width):
        return [pl.BlockSpec((w, width), before), pl.BlockSpec((nblk * w, width), pair),
                pl.BlockSpec((w, width), after)]

    def ctx_spec(width):
        return pl.BlockSpec((lc, width), lambda b, i: (b, 0))

    return pl.pallas_call(
        functools.partial(_swa_kernel, nstep=nstep, local=True),
        grid=(batch, nstep),
        in_specs=[pl.BlockSpec(memory_space=pltpu.SMEM), pl.BlockSpec((nblk * w, SWA_WIDTH), pair)]
        + kv_specs(kvw) + kv_specs(SV_LANES) + [ctx_spec(kvw), ctx_spec(SV_LANES)],
        out_specs=pl.BlockSpec((nblk * w, SWA_WIDTH), pair),
        out_shape=jax.ShapeDtypeStruct((n, SWA_WIDTH), BF16),
        compiler_params=_cparams(("arbitrary", "arbitrary")),
        name="swa",
    )(sink, sq, skd, skd, skd, svd, svd, svd, kcd, vcd)


def _swa_ctx_call(sq, kcd, vcd, sink, batch, lc):
    n = sq.shape[0]
    w = ATT_BLOCK
    nb = lc // w
    kvw = 2 * SWA_KVW
    return pl.pallas_call(
        functools.partial(_swa_kernel, nstep=0, local=False),
        grid=(batch, nb),
        in_specs=[pl.BlockSpec(memory_space=pltpu.SMEM),
                  pl.BlockSpec((w, SWA_WIDTH), lambda b, i: (b * nb + i, 0)),
                  pl.BlockSpec((lc, kvw), lambda b, i: (b, 0)),
                  pl.BlockSpec((lc, SV_LANES), lambda b, i: (b, 0))],
        out_specs=pl.BlockSpec((w, SWA_WIDTH), lambda b, i: (b * nb + i, 0)),
        out_shape=jax.ShapeDtypeStruct((n, SWA_WIDTH), BF16),
        compiler_params=_cparams(("arbitrary", "arbitrary")),
        name="swa_ctx",
    )(sink, sq, kcd, vcd)


def _outproj_kernel(of_ref, ob_ref, gate_ref, gn_ref, swa_ref, w_ref, x_ref, g1_ref, n2_ref, sc_ref, sh_ref,
                    wr_ref, xo_ref, h2_ref, lg_ref):
    o = of_ref[...].astype(F32) + ob_ref[...].astype(F32)
    parts = []
    for h in range(GLA_HEADS):
        oh = o[:, h * GLA_DV:(h + 1) * GLA_DV]
        ms = jnp.mean(oh * oh, axis=-1, keepdims=True)
        parts.append(oh * lax.rsqrt(ms + NORM_EPS))
    on = jnp.concatenate(parts, axis=-1) * gn_ref[...]
    gate = gate_ref[...].astype(F32)
    gla = on * (gate * jax.nn.sigmoid(gate))
    mix = jnp.concatenate([gla.astype(BF16), swa_ref[...]], axis=-1)
    y = jnp.dot(mix, w_ref[...], preferred_element_type=F32)
    xo = x_ref[...] + g1_ref[...] * y
    xo_ref[...] = xo
    ms = jnp.mean(xo * xo, axis=-1, keepdims=True)
    h2 = (xo * lax.rsqrt(ms + NORM_EPS)) * (n2_ref[...] * (1.0 + sc_ref[...])) + sh_ref[...]
    hi = h2.astype(BF16)
    hi_f = hi.astype(F32)
    h2_ref[...] = _pack_rounded_pairs(hi_f)
    lo = (h2 - hi_f).astype(BF16)
    both = jnp.dot(hi, wr_ref[...], preferred_element_type=F32)
    lg = both[:, :LANES] + both[:, LANES:] + jnp.dot(lo, wr_ref[:, :LANES], preferred_element_type=F32)
    lg_ref[...] = jnp.transpose(lg)[:N_EXPERTS, :]


def _outproj_call(o_f, o_b, gate, gn, swa, w_out_b, x2d, mods5, layer, row_of_tile, n2, wr_cat, tm):
    n, d = x2d.shape
    nt = n // tm

    def mod_spec(k):
        return pl.BlockSpec((None, None, None, 1, d), lambda t: (layer, row_of_tile(t), k, 0, 0))

    return pl.pallas_call(
        _outproj_kernel,
        grid=(nt,),
        in_specs=[
            pl.BlockSpec((tm, GLA_WIDTH), lambda t: (t, 0)),
            pl.BlockSpec((tm, GLA_WIDTH), lambda t: (t, 0)),
            pl.BlockSpec((tm, GLA_WIDTH), lambda t: (t, 0)),
            pl.BlockSpec((1, GLA_WIDTH), lambda t: (0, 0)),
            pl.BlockSpec((tm, SWA_WIDTH), lambda t: (t, 0)),
            pl.BlockSpec((None, d, d), lambda t: (layer, 0, 0)),
            pl.BlockSpec((tm, d), lambda t: (t, 0)),
            mod_spec(2),
            pl.BlockSpec((1, d), lambda t: (0, 0)),
            mod_spec(4), mod_spec(3),
            pl.BlockSpec((d, 2 * LANES), lambda t: (0, 0)),
        ],
        out_specs=[
            pl.BlockSpec((tm, d), lambda t: (t, 0)),
            pl.BlockSpec((tm, d // 2), lambda t: (t, 0)),
            pl.BlockSpec((N_EXPERTS, tm), lambda t: (0, t)),
        ],
        out_shape=[
            jax.ShapeDtypeStruct((n, d), F32),
            jax.ShapeDtypeStruct((n, d // 2), U32),
            jax.ShapeDtypeStruct((N_EXPERTS, n), F32),
        ],
        compiler_params=_cparams(("arbitrary",)),
        name="outproj",
    )(o_f, o_b, gate, gn.reshape(1, GLA_WIDTH), swa, w_out_b, x2d, mods5, n2.reshape(1, d), mods5, mods5, wr_cat)


def _first_index(vals, target):
    idx = jnp.full(target.shape, len(vals) - 1, I32)
    for i in range(len(vals) - 2, -1, -1):
        idx = jnp.where(vals[i] == target, i, idx)
    return idx


def _route_kernel(lg_ref, br_ref, io_ref, wo_ref, seg_ref, pad_ref, blk_ref, cnt_scr, seg_scr, pad_scr, *, tn):
    phase = pl.program_id(0)
    step = pl.program_id(1)
    group = lg_ref.shape[1] // tn

    @pl.when((phase == 0) & (step == 0))
    def _():
        cnt_scr[...] = jnp.zeros_like(cnt_scr)
        seg_scr[...] = jnp.zeros_like(seg_scr)
        pad_scr[...] = jnp.zeros_like(pad_scr)

    @pl.when((phase == 1) & (step == 0))
    def _():
        seg_len = jnp.floor((cnt_scr[...] + (SUBLANES - 1)) * (1.0 / SUBLANES)) * SUBLANES
        rows_e = jnp.sum(seg_len, axis=1, keepdims=True)
        blocks = jnp.floor((rows_e + (MOE_BLOCK - 1)) * (1.0 / MOE_BLOCK)) * MOE_BLOCK
        r128 = lax.broadcasted_iota(I32, (LANES, LANES), 0)
        c128 = lax.broadcasted_iota(I32, (LANES, LANES), 1)
        before_tile = jnp.where(r128 < c128, 1.0, 0.0).astype(BF16)
        seg = jnp.dot((seg_len * (1.0 / SUBLANES)).astype(BF16), before_tile,
                      preferred_element_type=F32) * SUBLANES
        run = jnp.zeros((1, 1), F32)
        for e in range(N_EXPERTS):
            seg_scr[e:e + 1, :] = seg[e:e + 1, :] + run
            run = run + blocks[e:e + 1, :]
        pad_scr[...] = seg_len
        seg_ref[...] = seg_scr[...].astype(I32)
        pad_ref[...] = seg_len.astype(I32)
        blk_ref[...] = jnp.broadcast_to(blocks, blk_ref.shape).astype(I32)

    for g in range(group):
        lanes = slice(g * tn, (g + 1) * tn)
        _route_tile(lg_ref[:, lanes], br_ref, io_ref, wo_ref, lanes, step * group + g, phase, cnt_scr, pad_scr)


def _route_tile(lg, br_ref, io_ref, wo_ref, lanes, t, phase, cnt_scr, pad_scr):
    tn = lg.shape[1]
    tile_lane = lax.broadcasted_iota(I32, (N_EXPERTS, LANES), 1)
    s = jax.nn.sigmoid(lg)
    sb = s + br_ref[...]
    rows_s = [s[e:e + 1, :] for e in range(N_EXPERTS)]
    rows_b = [sb[e:e + 1, :] for e in range(N_EXPERTS)]
    gscore, gi1, gi2 = [], [], []
    epg = EXPERTS_PER_GROUP
    for g in range(N_GROUPS):
        a = rows_b[g * epg:(g + 1) * epg]
        m1 = functools.reduce(jnp.maximum, a)
        i1 = _first_index(a, m1)
        rest = [jnp.where(i1 == i, -jnp.inf, a[i]) for i in range(epg)]
        m2 = functools.reduce(jnp.maximum, rest)
        i2 = _first_index(rest, m2)
        gscore.append(m1 + m2)
        gi1.append(i1)
        gi2.append(i2)
    gm = functools.reduce(jnp.maximum, gscore)
    gsel = _first_index(gscore, gm)
    i1 = gi1[N_GROUPS - 1]
    i2 = gi2[N_GROUPS - 1]
    for g in range(N_GROUPS - 2, -1, -1):
        i1 = jnp.where(gsel == g, gi1[g], i1)
        i2 = jnp.where(gsel == g, gi2[g], i2)
    idx0 = gsel * epg + i1
    idx1 = gsel * epg + i2
    s0 = jnp.zeros_like(rows_s[0])
    s1 = jnp.zeros_like(rows_s[0])
    for e in range(N_EXPERTS):
        s0 = jnp.where(idx0 == e, rows_s[e], s0)
        s1 = jnp.where(idx1 == e, rows_s[e], s1)
    tot = s0 + s1
    w0 = s0 / tot
    w1 = s1 / tot

    eidx = lax.broadcasted_iota(I32, (N_EXPERTS, tn), 0)
    oh0 = eidx == idx0
    oh1 = eidx == idx1
    oh = jnp.where(oh0 | oh1, 1.0, 0.0)
    rr = lax.broadcasted_iota(I32, (tn, tn), 0)
    cc = lax.broadcasted_iota(I32, (tn, tn), 1)
    upper = jnp.where(rr < cc, 1.0, 0.0).astype(BF16)
    before = jnp.dot(oh.astype(BF16), upper, preferred_element_type=F32)

    @pl.when(phase == 0)
    def _():
        cnt_scr[...] = cnt_scr[...] + jnp.where(tile_lane == t, jnp.sum(oh, axis=1, keepdims=True), 0.0)

    pad_col = jnp.sum(jnp.where(tile_lane == t, pad_scr[...], 0.0), axis=1, keepdims=True)
    run = jnp.zeros((1, 1), F32)
    offs = []
    for e in range(N_EXPERTS):
        offs.append(run)
        run = run + pad_col[e:e + 1, :]
    pos = before + jnp.concatenate(offs, axis=0)
    spos0 = jnp.sum(jnp.where(oh0, pos, 0.0), axis=0, keepdims=True)
    spos1 = jnp.sum(jnp.where(oh1, pos, 0.0), axis=0, keepdims=True)
    zi = jnp.zeros((SUBLANES - 4, tn), I32)
    io_ref[:, lanes] = jnp.concatenate([idx0, idx1, spos0.astype(I32), spos1.astype(I32), zi], axis=0)
    wo_ref[:, lanes] = jnp.concatenate([w0, w1, jnp.zeros((SUBLANES - 2, tn), F32)], axis=0)


def _route_call(logits_t, b_router, tn):
    ne, t_all = logits_t.shape
    ntile = t_all // tn
    assert ntile <= LANES
    group = max(g for g in range(1, ROUTE_TILES_PER_STEP + 1) if ntile % g == 0)
    tw = group * tn
    tok_out = pl.BlockSpec((SUBLANES, tw), lambda p, t: (0, t * p))
    tab_out = pl.BlockSpec((ne, LANES), lambda p, t: (0, 0))
    tab = jax.ShapeDtypeStruct((ne, LANES), I32)
    return pl.pallas_call(
        functools.partial(_route_kernel, tn=tn),
        grid=(2, ntile // group),
        in_specs=[pl.BlockSpec((ne, tw), lambda p, t: (0, t)), pl.BlockSpec((ne, 1), lambda p, t: (0, 0))],
        out_specs=[tok_out, tok_out, tab_out, tab_out, tab_out],
        out_shape=[
            jax.ShapeDtypeStruct((SUBLANES, t_all), I32),
            jax.ShapeDtypeStruct((SUBLANES, t_all), F32),
            tab, tab, tab,
        ],
        scratch_shapes=[pltpu.VMEM((ne, LANES), F32)] * 3,
        compiler_params=_cparams(("arbitrary", "arbitrary")),
        name="route",
    )(logits_t, b_router.reshape(ne, 1))


def _segment_copies(seg_ref, pad_ref, t, stage, hbm, sem, to_hbm, wait):
    if wait:
        total = functools.reduce(lambda a, b: a + b, [pad_ref[e * LANES + t] for e in range(N_EXPERTS)])
        for size in (2 * SEG_SIZES[0],) + SEG_SIZES:
            @pl.when((total & size) != 0)
            def _(size=size):
                src, dst = stage.at[pl.ds(0, size), :], hbm.at[pl.ds(0, size), :]
                if not to_hbm:
                    src, dst = dst, src
                pltpu.make_async_copy(src, dst, sem).wait()
        return
    loc = 0
    for e in range(N_EXPERTS):
        n = pad_ref[e * LANES + t]
        start = seg_ref[e * LANES + t]
        for size in SEG_SIZES:
            off = n & (-2 * size)

            @pl.when((n & size) != 0)
            def _(off=off, size=size, loc=loc, start=start):
                s_rows = pl.ds(pl.multiple_of(loc + off, SUBLANES), size)
                h_rows = pl.ds(pl.multiple_of(start + off, SUBLANES), size)
                src, dst = (stage.at[s_rows, :], hbm.at[h_rows, :])
                if not to_hbm:
                    src, dst = dst, src
                pltpu.make_async_copy(src, dst, sem).start()
        loc = loc + n


def _sort_matrix(io_ref, rows):
    r = lax.broadcasted_iota(I32, (rows, io_ref.shape[1]), 0)
    return r == io_ref[2:3, :], r == io_ref[3:4, :]


def _dispatch_kernel(seg_ref, pad_ref, zblk_ref, *refs, tiles):
    srcs = refs[:len(tiles)]
    io_ref, wo_ref, xb_hbm, stage, zbuf, sem = refs[len(tiles):]
    i = pl.program_id(0)
    m = MOE_BLOCK

    def zero_copy(j):
        start = pl.multiple_of(zblk_ref[j] * m, m)
        return pltpu.make_async_copy(zbuf, xb_hbm.at[pl.ds(start, m), :], sem.at[0])

    @pl.when(i == 0)
    def _():
        zbuf[...] = jnp.zeros_like(zbuf)
        for j in range(zblk_ref.shape[0]):
            @pl.when(zblk_ref[j] >= 0)
            def _():
                zero_copy(j).start()
        for j in range(zblk_ref.shape[0]):
            @pl.when(zblk_ref[j] >= 0)
            def _():
                zero_copy(j).wait()

    hp = srcs[0][...]
    if len(srcs) == 2:
        hp = jnp.where(i < tiles[0], hp, srcs[1][...])
    x = _unpack_bf16_pairs(hp)
    m0, m1 = _sort_matrix(io_ref, STAGE_ROWS)
    sort = jnp.where(m0 | m1, 1.0, 0.0).astype(BF16)
    xs = jnp.dot(sort, x, preferred_element_type=F32)
    dh = x.shape[1] // 2
    slot = i % 2
    cur = stage.at[slot]
    cur[:, :dh] = _pack_rounded_pairs(xs)
    ws = jnp.sum(jnp.where(m0, wo_ref[0:1, :], 0.0) + jnp.where(m1, wo_ref[1:2, :], 0.0), axis=1, keepdims=True)
    cur[:, dh:] = jnp.broadcast_to(lax.bitcast_convert_type(ws, U32), (STAGE_ROWS, LANES))
    _segment_copies(seg_ref, pad_ref, i, cur, xb_hbm, sem.at[slot], to_hbm=True, wait=False)

    @pl.when(i > 0)
    def _():
        _segment_copies(seg_ref, pad_ref, i - 1, stage.at[1 - slot], xb_hbm, sem.at[1 - slot], to_hbm=True, wait=True)

    @pl.when(i == pl.num_programs(0) - 1)
    def _():
        _segment_copies(seg_ref, pad_ref, i, cur, xb_hbm, sem.at[slot], to_hbm=True, wait=True)


def _dispatch_call(seg, pad, zblk, io, wo, sources, p_rows):
    tile = ROUTE_TILE
    dh = sources[0].shape[1]
    tiles = tuple(s.shape[0] // tile for s in sources)
    firsts = tuple(sum(tiles[:k]) for k in range(len(tiles)))

    def src_spec(first, ntile):
        return pl.BlockSpec((tile, dh), lambda i, *_: (jnp.clip(i - first, 0, ntile - 1), 0))

    tok_spec = pl.BlockSpec((SUBLANES, tile), lambda i, *_: (0, i))
    grid_spec = pltpu.PrefetchScalarGridSpec(
        num_scalar_prefetch=3,
        grid=(sum(tiles),),
        in_specs=[src_spec(f, n) for f, n in zip(firsts, tiles)] + [tok_spec, tok_spec],
        out_specs=pl.BlockSpec(memory_space=pl.ANY),
        scratch_shapes=[pltpu.VMEM((2, STAGE_ROWS, dh + LANES), U32), pltpu.VMEM((MOE_BLOCK, dh + LANES), U32),
                        pltpu.SemaphoreType.DMA((2,))],
    )
    return pl.pallas_call(
        functools.partial(_dispatch_kernel, tiles=tiles),
        grid_spec=grid_spec,
        out_shape=jax.ShapeDtypeStruct((p_rows, dh + LANES), U32),
        compiler_params=_cparams(("arbitrary",)),
        name="dispatch",
    )(seg, pad, zblk, *sources, io, wo)


def _expert_kernel(be_ref, nu_ref, va_ref, ne_ref, x_ref, wg_hbm, wu_hbm, wd_hbm, o_ref,
                   wg_b, wu_b, wd_b, wg_f, wu_f, wd_f, switch_ref, sem, *, layer):
    j = pl.program_id(0)
    dh = o_ref.shape[1]
    half = o_ref.shape[0] // 2
    valid = va_ref[j]
    e = be_ref[j]

    def weight_copies(expert, slot):
        return [pltpu.make_async_copy(hbm.at[layer, expert], buf.at[slot], sem.at[slot, k])
                for k, (hbm, buf) in enumerate(((wg_hbm, wg_f), (wu_hbm, wu_f), (wd_hbm, wd_f)))]

    @pl.when(j == 0)
    def _():
        switch_ref[0] = 0
        for cp in weight_copies(e, 0):
            cp.start()

    @pl.when((j == 0) | (e != be_ref[jnp.maximum(j - 1, 0)]))
    def _():
        slot = switch_ref[0] % 2
        for cp in weight_copies(e, slot):
            cp.wait()
        wg_b[...] = wg_f[slot].astype(BF16)
        wu_b[...] = wu_f[slot].astype(BF16)
        wd_b[...] = wd_f[slot].astype(BF16)
        nxt = ne_ref[e]

        @pl.when(nxt >= 0)
        def _():
            for cp in weight_copies(nxt, 1 - slot):
                cp.start()
        switch_ref[0] = switch_ref[0] + 1

    def ffn_halves(halves):
        gates = []
        for r in halves:
            x = _unpack_bf16_pairs(x_ref[r * half:(r + 1) * half, :dh])
            gates.append((jnp.dot(x, wg_b[...], preferred_element_type=F32),
                          jnp.dot(x, wu_b[...], preferred_element_type=F32)))
        for r, (a, u) in zip(halves, gates):
            hmid = (a * jax.nn.sigmoid(a)) * u
            y = jnp.dot(hmid.astype(BF16), wd_b[...], preferred_element_type=F32)
            row_w = lax.bitcast_convert_type(x_ref[r * half:(r + 1) * half, dh:dh + 1], F32)
            o_ref[r * half:(r + 1) * half, :] = _pack_bf16_pairs(y * row_w)

    @pl.when(valid > half)
    def _():
        ffn_halves((0, 1))

    @pl.when((valid > 0) & (valid <= half))
    def _():
        ffn_halves((0,))
        o_ref[half:, :] = jnp.zeros((half, dh), o_ref.dtype)

    @pl.when(valid == 0)
    def _():
        o_ref[...] = jnp.zeros_like(o_ref)


def _expert_call(block_e, nused, valid, next_e, xb, wg, wu, wd, layer):
    p_rows = xb.shape[0]
    dh = xb.shape[1] - LANES
    d = 2 * dh
    m = MOE_BLOCK
    nb = p_rows // m
    de = wg.shape[-1]

    def xmap(j, be, nu, va, ne):
        return (jnp.minimum(j, nu[0] - 1), 0)

    any_spec = pl.BlockSpec(memory_space=pl.ANY)
    grid_spec = pltpu.PrefetchScalarGridSpec(
        num_scalar_prefetch=4,
        grid=(nb,),
        in_specs=[pl.BlockSpec((m, dh + LANES), xmap), any_spec, any_spec, any_spec],
        out_specs=pl.BlockSpec((m, dh), lambda j, be, nu, va, ne: (j, 0)),
        scratch_shapes=[
            pltpu.VMEM((d, de), BF16), pltpu.VMEM((d, de), BF16), pltpu.VMEM((de, d), BF16),
            pltpu.VMEM((2, d, de), F32), pltpu.VMEM((2, d, de), F32), pltpu.VMEM((2, de, d), F32),
            pltpu.SMEM((1,), I32), pltpu.SemaphoreType.DMA((2, 3)),
        ],
    )
    return pl.pallas_call(
        functools.partial(_expert_kernel, layer=layer),
        grid_spec=grid_spec,
        out_shape=jax.ShapeDtypeStruct((p_rows, dh), U32),
        compiler_params=_cparams(("arbitrary",)),
        name="experts",
    )(block_e, nused, valid, next_e, xb, wg, wu, wd)


def _combine_kernel(*refs, tile0, final):
    if final:
        seg_ref, pad_ref, yb_hbm, io_ref, x_ref, g2_ref, fn_ref, o_ref, stage, sem = refs
    else:
        seg_ref, pad_ref, yb_hbm, io_ref, x_ref, g2_ref, o_ref, stage, sem = refs
    i = pl.program_id(0)

    slot = i % 2

    @pl.when(i == 0)
    def _():
        stage[...] = jnp.zeros_like(stage)
        _segment_copies(seg_ref, pad_ref, tile0, stage.at[0], yb_hbm, sem.at[0], to_hbm=False, wait=False)

    @pl.when(i + 1 < pl.num_programs(0))
    def _():
        _segment_copies(seg_ref, pad_ref, tile0 + i + 1, stage.at[1 - slot], yb_hbm, sem.at[1 - slot],
                        to_hbm=False, wait=False)

    _segment_copies(seg_ref, pad_ref, tile0 + i, stage.at[slot], yb_hbm, sem.at[slot], to_hbm=False, wait=True)
    rows = _unpack_bf16_pairs(stage[slot])
    m0, m1 = _sort_matrix(io_ref, STAGE_ROWS)
    pick = jnp.where(m0 | m1, 1.0, 0.0).astype(BF16)
    y = lax.dot_general(pick, rows, (((0,), (0,)), ((), ())), preferred_element_type=F32)
    xo = x_ref[...] + g2_ref[...] * y
    if final:
        ms = jnp.mean(xo * xo, axis=-1, keepdims=True)
        xo = (xo * lax.rsqrt(ms + NORM_EPS)) * fn_ref[...]
    o_ref[...] = xo


def _combine_call(seg, pad, yb, io, x2d, mods5, layer, row_of_tile, final_g, tile0):
    n, d = x2d.shape
    tm = ROUTE_TILE
    final = final_g is not None
    in_specs = [
        pl.BlockSpec(memory_space=pl.ANY),
        pl.BlockSpec((SUBLANES, tm), lambda t, *_: (0, tile0 + t)),
        pl.BlockSpec((tm, d), lambda t, *_: (t, 0)),
        pl.BlockSpec((None, None, None, 1, d), lambda t, *_: (layer, row_of_tile(t), 5, 0, 0)),
    ]
    args = [yb, io, x2d, mods5]
    if final:
        in_specs.append(pl.BlockSpec((1, d), lambda t, *_: (0, 0)))
        args.append(final_g.reshape(1, d))
    grid_spec = pltpu.PrefetchScalarGridSpec(
        num_scalar_prefetch=2,
        grid=(n // tm,),
        in_specs=in_specs,
        out_specs=pl.BlockSpec((tm, d), lambda t, *_: (t, 0)),
        scratch_shapes=[pltpu.VMEM((2, STAGE_ROWS, d // 2), U32), pltpu.SemaphoreType.DMA((2,))],
    )
    return pl.pallas_call(
        functools.partial(_combine_kernel, tile0=tile0, final=final),
        grid_spec=grid_spec,
        out_shape=jax.ShapeDtypeStruct((n, d), F32),
        compiler_params=_cparams(("arbitrary",)),
        name="combine",
    )(seg, pad, *args)


def _pack_w_in(w):
    assert sum(IN_SIZES[:6]) == C_Z + 2 * GLA_GATE_RANK and sum(IN_SIZES) - sum(IN_SIZES[:6]) == C_END - C_SQ
    cut = C_Z + 2 * GLA_GATE_RANK
    left = jnp.pad(w[..., :cut], ((0, 0), (0, 0), (0, C_END - cut)))
    right = jnp.pad(w[..., cut:], ((0, 0), (0, 0), (C_SQ, 0)))
    return (left + right).astype(BF16)


def _pad_up(up, row0):
    up = up.reshape(GLA_GATE_RANK, GLA_HEADS // 2, LANES).transpose(1, 0, 2)
    hi = up.astype(BF16)
    lo = (up - hi.astype(F32)).astype(BF16)
    out = jnp.zeros((GLA_HEADS // 2, LANES, LANES), BF16)
    for group, part in enumerate((hi, hi, lo)):
        r = group * 2 * GLA_GATE_RANK + row0
        out = out.at[:, r:r + GLA_GATE_RANK, :].set(part)
    return out


def _rope_tables(seq):
    rows = seq // GRID_W
    dim = jnp.arange(LANES, dtype=I32) % SWA_HD
    inv = ROPE_THETA ** (-((dim % ROPE_HALF).astype(F32) * 2.0 / ROPE_AXIS_DIM))
    sign = jnp.where(dim % ROPE_AXIS_DIM < ROPE_HALF, -1.0, 1.0)
    by_row = (dim < ROPE_AXIS_DIM)[None, None, :]
    ang_r = (jnp.arange(rows, dtype=F32)[:, None] * inv[None, :])[:, None, :]
    ang_c = (jnp.arange(GRID_W, dtype=F32)[:, None] * inv[None, :])[None, :, :]
    cos = jnp.where(by_row, jnp.cos(ang_r), jnp.cos(ang_c))
    sin = jnp.where(by_row, jnp.sin(ang_r), jnp.sin(ang_c)) * sign
    return cos.reshape(seq, LANES), sin.reshape(seq, LANES)


def _tile(n, pref):
    t = pref
    while n % t:
        t //= 2
    return t


def kernel(x, c, ctx, c_ctx, w_ada, b_ada, norm1, norm2, w_in, gla_up_f, gla_bias_f, gla_up_b, gla_bias_b,
           gla_norm, swa_sink, w_out, w_router, b_router, w_gate, w_up, w_down, final_norm):
    batch, seq, d = x.shape
    lc = ctx.shape[1]
    depth = w_ada.shape[0]
    n_l = batch * seq
    n_c = batch * lc
    npair = GLA_HEADS // 2
    assert batch + 1 <= SUBLANES
    assert n_l % ROUTE_TILE == 0 and n_c % ROUTE_TILE == 0

    tm_l = _tile(seq, ROW_TILE)
    tm_in = _tile(seq, INPROJ_TILE)
    tm_c = _tile(lc, CTX_TILE)
    tl_l = _tile(seq, GLA_TILE)
    tl_c = _tile(lc, ROW_TILE)

    rows = jnp.zeros((SUBLANES, d), F32).at[:batch].set(c).at[batch].set(c_ctx)
    mods = _ada_call(rows, w_ada, b_ada)
    mods5 = mods.reshape(depth, SUBLANES, 6, 1, d)

    tabs = _rope_tables(seq)
    w_packed = _pack_w_in(w_in)
    w_out_b = w_out.astype(BF16)
    wr_pad = jnp.zeros((d, LANES), F32).at[:, :N_EXPERTS].set(w_router)
    wr_hi = wr_pad.astype(BF16)
    wr_lo = (wr_pad - wr_hi.astype(F32)).astype(BF16)
    wr_cat = jnp.concatenate([wr_hi, wr_lo], axis=-1)

    def lat_row(tm):
        return lambda t: t // (seq // tm)

    def ctx_row(tm):
        return lambda t: batch

    xl = x.reshape(n_l, d)
    xc = ctx.reshape(n_c, d)
    for i in range(depth):
        last = i == depth - 1
        upf = _pad_up(gla_up_f[i], 0)
        upb = _pad_up(gla_up_b[i], GLA_GATE_RANK)
        bias_f = gla_bias_f[i].reshape(npair, 1, LANES)
        bias_b = gla_bias_b[i].reshape(npair, 1, LANES)

        c_qk, c_v, c_g, c_z, c_sq, c_sk, c_sv = _inproj_call(
            xc, norm1[i], mods5, i, ctx_row(tm_c), w_packed, None, tm_c, lc // tm_c)
        l_qk, l_v, l_g, l_z, l_sq, l_sk, l_sv = _inproj_call(
            xl, norm1[i], mods5, i, lat_row(tm_in), w_packed, tabs, tm_in, seq // tm_in)

        s_zero = jnp.zeros((batch, npair, 2, 2 * GLA_DV, LANES), F32)
        oc_f, oc_b, s_ctx = _gla_call(c_qk, c_v, c_z, upf, upb, bias_f, bias_b, s_zero, batch, lc, tl_c)
        ol_f, ol_b, _ = _gla_call(l_qk, l_v, l_z, upf, upb, bias_f, bias_b, s_ctx, batch, seq, tl_l)

        swa_l = _swa_call(l_sq, l_sk, l_sv, c_sk, c_sv, swa_sink[i], batch, seq, lc)
        xl, h2l, lg_l = _outproj_call(ol_f, ol_b, l_g, gla_norm[i], swa_l, w_out_b, xl, mods5, i, lat_row(tm_l),
                                      norm2[i], wr_cat, tm_l)
        if last:
            logits_t = lg_l
            t_all = n_l
        else:
            swa_c = _swa_ctx_call(c_sq, c_sk, c_sv, swa_sink[i], batch, lc)
            xc, h2c, lg_c = _outproj_call(oc_f, oc_b, c_g, gla_norm[i], swa_c, w_out_b, xc, mods5, i, ctx_row(tm_c),
                                          norm2[i], wr_cat, tm_c)
            logits_t = jnp.concatenate([lg_l, lg_c], axis=1)
            t_all = n_l + n_c

        io, wo, seg, pad, blk = _route_call(logits_t, b_router, ROUTE_TILE)
        m = MOE_BLOCK
        padded = blk[:, 0]
        pend = jnp.cumsum(padded)
        ntile = t_all // ROUTE_TILE
        nb = -(-(t_all * TOP_K + ntile * N_EXPERTS * (SUBLANES - 1)) // m) + N_EXPERTS
        p_rows = nb * m
        seg = seg.reshape(-1)
        pad = pad.reshape(-1)
        nused = (pend[-1:] // m).astype(I32)
        tail = nused + jnp.arange(nb - (t_all * TOP_K) // m, dtype=I32)
        zblk = jnp.concatenate([jnp.where(padded > 0, pend // m - 1, -1),
                                jnp.where(tail < nb, tail, -1)]).astype(I32)
        blk_row = jnp.minimum(jnp.arange(nb, dtype=I32), nused - 1) * m
        block_e = jnp.sum((pend[None, :] <= blk_row[:, None]).astype(I32), axis=1)
        rows_e = jnp.sum(pad.reshape(N_EXPERTS, LANES), axis=1)
        own = (block_e[:, None] == jnp.arange(N_EXPERTS, dtype=I32)[None, :]).astype(I32)
        seg_end = jnp.sum(own * (pend - padded + rows_e)[None, :], axis=1)
        blk_idx = jnp.arange(nb, dtype=I32)
        valid = jnp.where(blk_idx < nused, jnp.clip(seg_end - blk_idx * m, 0, m), 0).astype(I32)

        sources = (h2l,) if last else (h2l, h2c)
        xb = _dispatch_call(seg, pad, zblk, io, wo, sources, p_rows)
        eid = jnp.arange(N_EXPERTS, dtype=I32)
        later = (eid[None, :] > eid[:, None]) & (padded[None, :] > 0)
        next_e = jnp.min(jnp.where(later, eid[None, :], N_EXPERTS), axis=1)
        next_e = jnp.where(next_e < N_EXPERTS, next_e, -1).astype(I32)
        yb = _expert_call(block_e, nused, valid, next_e, xb, w_gate, w_up, w_down, i)
        xl_new = _combine_call(seg, pad, yb, io, xl, mods5, i, lat_row(ROUTE_TILE),
                               final_norm if last else None, 0)
        if not last:
            xc = _combine_call(seg, pad, yb, io, xc, mods5, i, ctx_row(0), None, n_l // ROUTE_TILE)
        xl = xl_new
    return xl.reshape(batch, seq, d)
```

```python
import functools

import jax
import jax.numpy as jnp
from jax import lax
from jax.experimental import pallas as pl
from jax.experimental.pallas import tpu as pltpu

F32 = jnp.float32
BF16 = jnp.bfloat16
I32 = jnp.int32
U32 = jnp.uint32
HIGHEST = lax.Precision.HIGHEST

GRID_W = 64
NORM_EPS = 1e-6
GLA_HEADS = 4
GLA_DK = 64
GLA_DV = 128
GLA_KW = GLA_HEADS * GLA_DK
GLA_WIDTH = GLA_HEADS * GLA_DV
GLA_GATE_RANK = 16
GLA_TAU = 16.0
GLA_CHUNK = 64
SWA_HEADS = 8
SWA_KV_HEADS = 2
SWA_HD = 64
SWA_WIDTH = SWA_HEADS * SWA_HD
SWA_KVW = SWA_KV_HEADS * SWA_HD
WINDOW = 128
ATT_BLOCK = 128
ROPE_THETA = 10000.0
ROPE_AXIS_DIM = SWA_HD // 2
ROPE_HALF = ROPE_AXIS_DIM // 2
N_EXPERTS = 16
N_GROUPS = 4
EXPERTS_PER_GROUP = N_EXPERTS // N_GROUPS
TOP_K = 2
IN_SIZES = (GLA_KW, GLA_KW, GLA_WIDTH, GLA_WIDTH, GLA_GATE_RANK, GLA_GATE_RANK, SWA_WIDTH, SWA_KVW, SWA_KVW)

LANES = 128
SUBLANES = 8
VMEM_LIMIT = 48 * 1024 * 1024

ROW_TILE = 512
GLA_TILE = 1024
GLA_SUM_ROWS = 256
INPROJ_TILE = 1024
CTX_TILE = 256
ADA_COLS = 1536
SWA_BLOCKS_PER_STEP = 4
MOE_BLOCK = 512
ROUTE_TILE = 512
ROUTE_TILES_PER_STEP = 4
SEG_SIZES = (512, 256, 128, 64, 32, 16, 8)
STAGE_ROWS = TOP_K * ROUTE_TILE + LANES

C_QK = 0
C_V = 512
C_G = 1024
C_Z = 1536
C_SQ = 1664
C_SK = 2176
C_SV = 2304
C_END = 2432
SV_LANES = 4 * LANES
NEG_BIG = -1e30

assert WINDOW == ATT_BLOCK


def _cparams(sem):
    return pltpu.CompilerParams(dimension_semantics=sem, vmem_limit_bytes=VMEM_LIMIT)


def _pack_bf16_pairs(x):
    return _pack_rounded_pairs(x.astype(BF16).astype(F32))


def _pack_rounded_pairs(xr):
    n = xr.shape[1] // 2
    lo = lax.bitcast_convert_type(xr[:, :n], U32)
    hi = lax.bitcast_convert_type(xr[:, n:], U32)
    return (lo >> 16) | hi


def _unpack_bf16_pairs(w):
    lo = lax.bitcast_convert_type(w << 16, F32)
    hi = lax.bitcast_convert_type(w & jnp.uint32(0xFFFF0000), F32)
    return jnp.concatenate([lo.astype(BF16), hi.astype(BF16)], axis=-1)


def _ada_kernel(a_ref, w_ref, b_ref, o_ref):
    a = a_ref[...]
    act = a * jax.nn.sigmoid(a)
    o_ref[...] = jnp.dot(act, w_ref[...], preferred_element_type=F32, precision=HIGHEST) + b_ref[...]


def _ada_call(rows, w_ada, b_ada):
    depth, d, n6 = w_ada.shape
    tn = ADA_COLS
    return pl.pallas_call(
        _ada_kernel,
        grid=(depth, n6 // tn),
        in_specs=[
            pl.BlockSpec((SUBLANES, d), lambda l, j: (0, 0)),
            pl.BlockSpec((None, d, tn), lambda l, j: (l, 0, j)),
            pl.BlockSpec((None, 1, tn), lambda l, j: (l, 0, j)),
        ],
        out_specs=pl.BlockSpec((None, SUBLANES, tn), lambda l, j: (l, 0, j)),
        out_shape=jax.ShapeDtypeStruct((depth, SUBLANES, n6), F32),
        compiler_params=_cparams(("arbitrary", "arbitrary")),
        name="adaln",
    )(rows, w_ada, b_ada.reshape(depth, 1, n6))


def _rope_tile(xj, cos, sin, lane_lo):
    partner = jnp.where(lane_lo, pltpu.roll(xj, LANES - ROPE_HALF, 1), pltpu.roll(xj, ROPE_HALF, 1))
    return xj * cos + partner * sin


def _inproj_kernel(*refs, rope):
    if rope:
        (x_ref, g_ref, sc_ref, sh_ref, w_ref, cos_ref, sin_ref,
         qk_ref, v_ref, gate_ref, z_ref, sq_ref, sk_ref, sv_ref) = refs
    else:
        (x_ref, g_ref, sc_ref, sh_ref, w_ref,
         qk_ref, v_ref, gate_ref, z_ref, sq_ref, sk_ref, sv_ref) = refs
    half = x_ref.shape[0] // 2
    scale = g_ref[...] * (1.0 + sc_ref[...])
    normed = []
    for r in range(2):
        x = x_ref[r * half:(r + 1) * half, :]
        ms = jnp.mean(x * x, axis=-1, keepdims=True)
        normed.append(((x * lax.rsqrt(ms + NORM_EPS)) * scale + sh_ref[...]).astype(BF16))

    for r in range(2):
        rows = slice(r * half, (r + 1) * half)
        hb = normed[r]

        def proj(a, b):
            return jnp.dot(hb, w_ref[:, a:b], preferred_element_type=F32)

        qk = proj(C_QK, C_V)
        qk_ref[rows, :GLA_KW] = (qk[:, :GLA_KW] * (GLA_DK ** -0.5)).astype(qk_ref.dtype)
        qk_ref[rows, GLA_KW:] = qk[:, GLA_KW:].astype(qk_ref.dtype)
        v_ref[rows, :] = proj(C_V, C_G).astype(v_ref.dtype)
        gate_ref[rows, :] = proj(C_G, C_Z).astype(gate_ref.dtype)
        z_ref[rows, :] = proj(C_Z, C_SQ)
        sq = proj(C_SQ, C_SK) * (SWA_HD ** -0.5)
        sk = proj(C_SK, C_SV)
        sv = proj(C_SV, C_END)
        if rope:
            cos = cos_ref[rows, :]
            sin = sin_ref[rows, :]
            lane_lo = (lax.broadcasted_iota(I32, cos.shape, 1) % ROPE_AXIS_DIM) < ROPE_HALF
            for j in range(SWA_WIDTH // LANES):
                sl = slice(j * LANES, (j + 1) * LANES)
                sq_ref[rows, sl] = _rope_tile(sq[:, sl], cos, sin, lane_lo).astype(sq_ref.dtype)
            sk = _rope_tile(sk, cos, sin, lane_lo)
        else:
            sq_ref[rows, :] = sq.astype(sq_ref.dtype)
        first = lax.broadcasted_iota(I32, sk.shape, 1) < SWA_HD
        sk_sw = pltpu.roll(sk, SWA_HD, 1)
        sk_ref[rows, :LANES] = jnp.where(first, sk, sk_sw).astype(sk_ref.dtype)
        sk_ref[rows, LANES:] = jnp.where(first, sk_sw, sk).astype(sk_ref.dtype)
        sv_sw = pltpu.roll(sv, SWA_HD, 1)
        sv_ref[rows, 0 * LANES:1 * LANES] = jnp.where(first, sv, 1.0).astype(sv_ref.dtype)
        sv_ref[rows, 1 * LANES:2 * LANES] = jnp.where(first, 1.0, sv_sw).astype(sv_ref.dtype)
        sv_ref[rows, 2 * LANES:3 * LANES] = jnp.where(first, sv_sw, 1.0).astype(sv_ref.dtype)
        sv_ref[rows, 3 * LANES:4 * LANES] = jnp.where(first, 1.0, sv).astype(sv_ref.dtype)


def _inproj_call(x2d, g, mods5, layer, row_of_tile, w_packed, tabs, tm, tiles_per_seq):
    n, d = x2d.shape
    nt = n // tm
    rope = tabs is not None

    def mod_spec(k):
        return pl.BlockSpec((None, None, None, 1, d), lambda t: (layer, row_of_tile(t), k, 0, 0))

    in_specs = [
        pl.BlockSpec((tm, d), lambda t: (t, 0)),
        pl.BlockSpec((1, d), lambda t: (0, 0)),
        mod_spec(1), mod_spec(0),
        pl.BlockSpec((None, d, C_END), lambda t: (layer, 0, 0)),
    ]
    args = [x2d, g.reshape(1, d), mods5, mods5, w_packed]
    if rope:
        in_specs += [pl.BlockSpec((tm, LANES), lambda t: (t % tiles_per_seq, 0))] * 2
        args += [tabs[0], tabs[1]]
    widths = (2 * GLA_KW, GLA_WIDTH, GLA_WIDTH, LANES, SWA_WIDTH, 2 * SWA_KVW, SV_LANES)
    dtypes = (BF16, BF16, BF16, F32, BF16, BF16, BF16)
    return pl.pallas_call(
        functools.partial(_inproj_kernel, rope=rope),
        grid=(nt,),
        in_specs=in_specs,
        out_specs=[pl.BlockSpec((tm, w), lambda t: (t, 0)) for w in widths],
        out_shape=[jax.ShapeDtypeStruct((n, w), dt) for w, dt in zip(widths, dtypes)],
        compiler_params=_cparams(("arbitrary",)),
        name="inproj",
    )(*args)


def _log_sigmoid(x):
    return jnp.minimum(x, 0.0) - jnp.log(1.0 + jnp.exp(-jnp.abs(x)))


def _gla_prepare(q_ref, k_ref, v_ref, z_ref, up_ref, bias_ref, tri_ref, scr, fwd):
    qin_s, km0_s, km1_s, kout_s, vm0_s, vm1_s, dec_s = scr[:7]
    qbd_s = scr[9]
    tl = q_ref.shape[0]
    c = GLA_CHUNK
    z = z_ref[...]
    z_hi = z.astype(BF16).astype(F32)
    zc = z_hi + pltpu.roll(z - z_hi, 2 * GLA_GATE_RANK, 1) + pltpu.roll(z_hi, 4 * GLA_GATE_RANK, 1)
    x = jnp.dot(zc.astype(BF16), up_ref[...], preferred_element_type=F32) + bias_ref[...]
    la = _log_sigmoid(x) * (1.0 / GLA_TAU)
    hi = la.astype(BF16)
    lo = (la - hi.astype(F32)).astype(BF16)
    hl = jnp.concatenate([hi, lo], axis=1)
    tg = tri_ref.shape[0]
    sums = jnp.concatenate([jnp.dot(tri_ref[...], hl[r0:r0 + tg, :], preferred_element_type=F32)
                            for r0 in range(0, tl, tg)], axis=0)
    b = sums[:, :LANES] + sums[:, LANES:]
    b3 = b.reshape(tl // c, c, LANES)
    edge = b3[:, c - 1:c, :] if fwd else b3[:, 0:1, :]
    btot = jnp.broadcast_to(edge, b3.shape).reshape(tl, LANES)
    q = q_ref[...].astype(F32)
    k = k_ref[...].astype(F32)
    first = (lax.broadcasted_iota(I32, (tl, LANES), 0) % (2 * c)) < c
    q_in = q * jnp.exp(b)
    qin_s[...] = q_in.astype(BF16)
    qbd_s[...] = jnp.concatenate([jnp.where(first, q_in, 0.0), jnp.where(first, 0.0, q_in)], axis=1).astype(BF16)
    k_in = k * jnp.exp(-b)
    head0 = lax.broadcasted_iota(I32, (tl, LANES), 1) < GLA_DK
    km0_s[...] = jnp.where(head0, k_in, 0.0).astype(BF16)
    km1_s[...] = jnp.where(head0, 0.0, k_in).astype(BF16)
    k_out = k * jnp.exp(btot - b)
    kout_s[...] = jnp.concatenate([jnp.where(first, k_out, 0.0), jnp.where(first, 0.0, k_out)], axis=1).astype(BF16)
    dec_s[...] = jnp.exp(btot)
    vf = v_ref[...].astype(F32)
    vhead0 = lax.broadcasted_iota(I32, (tl, 2 * GLA_DV), 1) < GLA_DV
    vm0_s[...] = jnp.where(vhead0, vf, 0.0).astype(BF16)
    vm1_s[...] = jnp.where(vhead0, 0.0, vf).astype(BF16)


def _gla_increment(v_ref, scr, pair):
    kout_s, u_s = scr[3], scr[7]
    rows = pl.ds(pair * 2 * GLA_CHUNK, 2 * GLA_CHUNK)
    u_t = lax.dot_general(v_ref[rows, :], kout_s[rows, :], (((0,), (0,)), ((), ())),
                          preferred_element_type=F32)
    srow = lax.broadcasted_iota(I32, (2 * GLA_DV, 2 * LANES), 0) // GLA_DV
    scol = (lax.broadcasted_iota(I32, (2 * GLA_DV, 2 * LANES), 1) % LANES) // GLA_DK
    u_s[pair] = jnp.where(srow == scol, u_t, 0.0)


def _gla_states(scr, s_ref, chunk_order):
    dec_s, u_s, sprev_s = scr[6], scr[7], scr[8]
    s_t = s_ref[...]
    for cidx in chunk_order:
        pair, lanes = cidx // 2, pl.ds((cidx % 2) * LANES, LANES)
        sprev_s[pair, :, lanes] = s_t.astype(BF16)
        r0 = cidx * GLA_CHUNK
        s_t = s_t * dec_s[r0:r0 + 1, :] + u_s[pair, :, lanes]
    s_ref[...] = s_t


def _gla_output(o_ref, scr, pair, fwd):
    qin_s, km0_s, km1_s, _, vm0_s, vm1_s, _, _, sprev_s, qbd_s = scr
    c = GLA_CHUNK
    r0 = pair * 2 * c
    rows = pl.ds(r0, 2 * c)
    ca, cb = pl.ds(r0, c), pl.ds(r0 + c, c)
    kst = jnp.concatenate([km0_s[ca, :], km1_s[ca, :], km0_s[cb, :], km1_s[cb, :]], axis=0)
    a = lax.dot_general(qin_s[rows, :], kst, (((1,), (1,)), ((), ())), preferred_element_type=F32)
    ri = lax.broadcasted_iota(I32, (2 * c, 4 * c), 0)
    ci = lax.broadcasted_iota(I32, (2 * c, 4 * c), 1)
    same_chunk = (ri // c) == (ci // (2 * c))
    keep = same_chunk & ((ci % c <= ri % c) if fwd else (ci % c >= ri % c))
    a = jnp.where(keep, a, 0.0).astype(BF16)
    vbd = jnp.concatenate([vm0_s[ca, :], vm1_s[ca, :], vm0_s[cb, :], vm1_s[cb, :]], axis=0)
    o = jnp.dot(a, vbd, preferred_element_type=F32)
    o = o + lax.dot_general(qbd_s[rows, :], sprev_s[pair], (((1,), (1,)), ((), ())), preferred_element_type=F32)
    o_ref[rows, :] = o.astype(o_ref.dtype)


def _gla_kernel(qf_ref, kf_ref, vf_ref, zf_ref, qb_ref, kb_ref, vb_ref, zb_ref,
                upf_ref, upb_ref, bf_ref, bb_ref, trif_ref, trib_ref, s0_ref,
                of_ref, ob_ref, sfin_ref, sf_scr, sb_scr, *scr, nchunk):
    i = pl.program_id(2)
    nt = pl.num_programs(2)
    scr_f, scr_b = scr[:len(scr) // 2], scr[len(scr) // 2:]

    @pl.when(i == 0)
    def _():
        sf_scr[...] = s0_ref[0]
        sb_scr[...] = s0_ref[1]

    _gla_prepare(qf_ref, kf_ref, vf_ref, zf_ref, upf_ref, bf_ref, trif_ref, scr_f, True)
    _gla_prepare(qb_ref, kb_ref, vb_ref, zb_ref, upb_ref, bb_ref, trib_ref, scr_b, False)
    for pair in range(nchunk // 2):
        _gla_increment(vf_ref, scr_f, pair)
        _gla_increment(vb_ref, scr_b, pair)
    _gla_states(scr_f, sf_scr, range(nchunk))
    _gla_states(scr_b, sb_scr, range(nchunk - 1, -1, -1))
    for pair in range(nchunk // 2):
        _gla_output(of_ref, scr_f, pair, True)
        _gla_output(ob_ref, scr_b, pair, False)

    @pl.when(i == nt - 1)
    def _():
        sfin_ref[0] = sf_scr[...]
        sfin_ref[1] = sb_scr[...]


def _gla_call(qk, v, z, upf_pad, upb_pad, bias_f, bias_b, s0, batch, seq, tl):
    n = qk.shape[0]
    nt = seq // tl
    npair = GLA_HEADS // 2

    def fwd_row(b, p, i):
        return b * nt + i

    def bwd_row(b, p, i):
        return b * nt + (nt - 1 - i)

    def specs(row):
        return [
            pl.BlockSpec((tl, LANES), lambda b, p, i: (row(b, p, i), p)),
            pl.BlockSpec((tl, LANES), lambda b, p, i: (row(b, p, i), npair + p)),
            pl.BlockSpec((tl, 2 * GLA_DV), lambda b, p, i: (row(b, p, i), p)),
            pl.BlockSpec((tl, LANES), lambda b, p, i: (row(b, p, i), 0)),
        ]

    in_specs = specs(fwd_row) + specs(bwd_row) + [
        pl.BlockSpec((None, LANES, LANES), lambda b, p, i: (p, 0, 0)),
        pl.BlockSpec((None, LANES, LANES), lambda b, p, i: (p, 0, 0)),
        pl.BlockSpec((None, 1, LANES), lambda b, p, i: (p, 0, 0)),
        pl.BlockSpec((None, 1, LANES), lambda b, p, i: (p, 0, 0)),
        pl.BlockSpec((None, GLA_SUM_ROWS, GLA_SUM_ROWS), lambda b, p, i: (0, 0, 0)),
        pl.BlockSpec((None, GLA_SUM_ROWS, GLA_SUM_ROWS), lambda b, p, i: (1, 0, 0)),
        pl.BlockSpec((None, None, 2, 2 * GLA_DV, LANES), lambda b, p, i: (b, p, 0, 0, 0)),
    ]
    assert tl % GLA_SUM_ROWS == 0
    ri = jnp.arange(GLA_SUM_ROWS)[:, None]
    ci = jnp.arange(GLA_SUM_ROWS)[None, :]
    same = (ri // GLA_CHUNK) == (ci // GLA_CHUNK)
    tri = jnp.stack([same & (ci <= ri), same & (ci >= ri)]).astype(BF16)
    npairs = tl // (2 * GLA_CHUNK)
    dir_scratch = [
        pltpu.VMEM((tl, LANES), BF16),
        pltpu.VMEM((tl, LANES), BF16),
        pltpu.VMEM((tl, LANES), BF16),
        pltpu.VMEM((tl, 2 * LANES), BF16),
        pltpu.VMEM((tl, 2 * GLA_DV), BF16),
        pltpu.VMEM((tl, 2 * GLA_DV), BF16),
        pltpu.VMEM((tl, LANES), F32),
        pltpu.VMEM((npairs, 2 * GLA_DV, 2 * LANES), F32),
        pltpu.VMEM((npairs, 2 * GLA_DV, 2 * LANES), BF16),
        pltpu.VMEM((tl, 2 * LANES), BF16),
    ]
    out_specs = [
        pl.BlockSpec((tl, 2 * GLA_DV), lambda b, p, i: (fwd_row(b, p, i), p)),
        pl.BlockSpec((tl, 2 * GLA_DV), lambda b, p, i: (bwd_row(b, p, i), p)),
        pl.BlockSpec((None, None, 2, 2 * GLA_DV, LANES), lambda b, p, i: (b, p, 0, 0, 0)),
    ]
    out_shape = [
        jax.ShapeDtypeStruct((n, GLA_WIDTH), BF16),
        jax.ShapeDtypeStruct((n, GLA_WIDTH), BF16),
        jax.ShapeDtypeStruct((batch, npair, 2, 2 * GLA_DV, LANES), F32),
    ]
    return pl.pallas_call(
        functools.partial(_gla_kernel, nchunk=tl // GLA_CHUNK),
        grid=(batch, npair, nt),
        in_specs=in_specs,
        out_specs=out_specs,
        out_shape=out_shape,
        scratch_shapes=[pltpu.VMEM((2 * GLA_DV, LANES), F32), pltpu.VMEM((2 * GLA_DV, LANES), F32)]
        + dir_scratch + dir_scratch,
        compiler_params=_cparams(("arbitrary", "arbitrary", "arbitrary")),
        name="gla",
    )(qk, qk, v, z, qk, qk, v, z, upf_pad, upb_pad, bias_f, bias_b, tri, tri, s0)


def _swa_block(sink_ref, q_ref, o_ref, row0, kall, vall, prev_ok, next_ok):
    w = ATT_BLOCK
    local = prev_ok is not None
    qrows = pl.ds(row0, w)
    r2 = lax.broadcasted_iota(I32, (2 * w, w), 0) % w
    c2 = lax.broadcasted_iota(I32, (2 * w, w), 1)
    if local:
        bias_prev = jnp.where((c2 >= r2) & prev_ok, 0.0, NEG_BIG)
        bias_next = jnp.where((c2 <= r2) & next_ok, 0.0, NEG_BIG)
    first = c2 < SWA_HD
    top = lax.broadcasted_iota(I32, (2 * w, 1), 0) < w
    scores = []
    for g in range(SWA_KV_HEADS):
        kd = kall[:, g * LANES:(g + 1) * LANES]
        qs = jnp.concatenate([q_ref[qrows, (2 * g) * LANES:(2 * g + 1) * LANES],
                              q_ref[qrows, (2 * g + 1) * LANES:(2 * g + 2) * LANES]], axis=0).astype(F32)
        for half in range(2):
            qm = (jnp.where(first, qs, 0.0) if half == 0 else jnp.where(first, 0.0, qs)).astype(BF16)
            s = lax.dot_general(qm, kd, (((1,), (1,)), ((), ())), preferred_element_type=F32)
            if local:
                s = jnp.concatenate([s[:, :w] + bias_prev, s[:, w:2 * w], s[:, 2 * w:3 * w] + bias_next,
                                     s[:, 3 * w:]], axis=1)
            scores.append(s)
    for g in range(SWA_KV_HEADS):
        outs = []
        for half in range(2):
            s = scores[2 * g + half]
            sk = jnp.where(top, sink_ref[4 * g + half], sink_ref[4 * g + 2 + half])
            m = jnp.maximum(jnp.max(s, axis=-1, keepdims=True), sk)
            p = jnp.exp((s - m).astype(BF16))
            va = vall[:, (2 * g + half) * LANES:(2 * g + half + 1) * LANES]
            acc = jnp.dot(p, va, preferred_element_type=F32)
            den = pltpu.roll(acc, SWA_HD, 1) + jnp.exp(sk - m)
            outs.append(acc / den)
        out = jnp.where(first, outs[0], outs[1])
        o_ref[qrows, (2 * g) * LANES:(2 * g + 1) * LANES] = out[:w].astype(o_ref.dtype)
        o_ref[qrows, (2 * g + 1) * LANES:(2 * g + 2) * LANES] = out[w:].astype(o_ref.dtype)


def _swa_kernel(*refs, nstep, local):
    w = ATT_BLOCK
    if not local:
        sink_ref, q_ref, kx_ref, vx_ref, o_ref = refs
        _swa_block(sink_ref, q_ref, o_ref, 0, kx_ref[...], vx_ref[...], None, None)
        return
    sink_ref, q_ref, kp_ref, kc_ref, kn_ref, vp_ref, vc_ref, vn_ref, kx_ref, vx_ref, o_ref = refs
    i = pl.program_id(1)
    nblk = q_ref.shape[0] // w
    k_blocks = [kp_ref[...]] + [kc_ref[j * w:(j + 1) * w, :] for j in range(nblk)] + [kn_ref[...]]
    v_blocks = [vp_ref[...]] + [vc_ref[j * w:(j + 1) * w, :] for j in range(nblk)] + [vn_ref[...]]
    exists = [i > 0] + [True] * nblk + [i < nstep - 1]
    for sub in range(nblk):
        kall = jnp.concatenate(k_blocks[sub:sub + 3] + [kx_ref[...]], axis=0)
        vall = jnp.concatenate(v_blocks[sub:sub + 3] + [vx_ref[...]], axis=0)
        _swa_block(sink_ref, q_ref, o_ref, sub * w, kall, vall, exists[sub], exists[sub + 2])


def _swa_call(sq, skd, svd, kcd, vcd, sink, batch, seq, lc):
    n = sq.shape[0]
    w = ATT_BLOCK
    nb = seq // w
    nblk = SWA_BLOCKS_PER_STEP if nb % SWA_BLOCKS_PER_STEP == 0 else 1
    nstep = nb // nblk
    kvw = 2 * SWA_KVW

    def pair(b, i):
        return (b * nstep + i, 0)

    def before(b, i):
        return (b * nb + jnp.maximum(nblk * i - 1, 0), 0)

    def after(b, i):
        return (b * nb + jnp.minimum(nblk * (i + 1), nb - 1), 0)

    def kv_specs(width):
        return [pl.BlockSpec((w, width), before), pl.BlockSpec((nblk * w, width), pair),
                pl.BlockSpec((w, width), after)]

    def ctx_spec(width):
        return pl.BlockSpec((lc, width), lambda b, i: (b, 0))

    return pl.pallas_call(
        functools.partial(_swa_kernel, nstep=nstep, local=True),
        grid=(batch, nstep),
        in_specs=[pl.BlockSpec(memory_space=pltpu.SMEM), pl.BlockSpec((nblk * w, SWA_WIDTH), pair)]
        + kv_specs(kvw) + kv_specs(SV_LANES) + [ctx_spec(kvw), ctx_spec(SV_LANES)],
        out_specs=pl.BlockSpec((nblk * w, SWA_WIDTH), pair),
        out_shape=jax.ShapeDtypeStruct((n, SWA_WIDTH), BF16),
        compiler_params=_cparams(("arbitrary", "arbitrary")),
        name="swa",
    )(sink, sq, skd, skd, skd, svd, svd, svd, kcd, vcd)


def _swa_ctx_call(sq, kcd, vcd, sink, batch, lc):
    n = sq.shape[0]
    w = ATT_BLOCK
    nb = lc // w
    kvw = 2 * SWA_KVW
    return pl.pallas_call(
        functools.partial(_swa_kernel, nstep=0, local=False),
        grid=(batch, nb),
        in_specs=[pl.BlockSpec(memory_space=pltpu.SMEM),
                  pl.BlockSpec((w, SWA_WIDTH), lambda b, i: (b * nb + i, 0)),
                  pl.BlockSpec((lc, kvw), lambda b, i: (b, 0)),
                  pl.BlockSpec((lc, SV_LANES), lambda b, i: (b, 0))],
        out_specs=pl.BlockSpec((w, SWA_WIDTH), lambda b, i: (b * nb + i, 0)),
        out_shape=jax.ShapeDtypeStruct((n, SWA_WIDTH), BF16),
        compiler_params=_cparams(("arbitrary", "arbitrary")),
        name="swa_ctx",
    )(sink, sq, kcd, vcd)


def _outproj_kernel(of_ref, ob_ref, gate_ref, gn_ref, swa_ref, w_ref, x_ref, g1_ref, n2_ref, sc_ref, sh_ref,
                    wr_ref, xo_ref, h2_ref, lg_ref):
    o = of_ref[...].astype(F32) + ob_ref[...].astype(F32)
    parts = []
    for h in range(GLA_HEADS):
        oh = o[:, h * GLA_DV:(h + 1) * GLA_DV]
        ms = jnp.mean(oh * oh, axis=-1, keepdims=True)
        parts.append(oh * lax.rsqrt(ms + NORM_EPS))
    on = jnp.concatenate(parts, axis=-1) * gn_ref[...]
    gate = gate_ref[...].astype(F32)
    gla = on * (gate * jax.nn.sigmoid(gate))
    mix = jnp.concatenate([gla.astype(BF16), swa_ref[...]], axis=-1)
    y = jnp.dot(mix, w_ref[...], preferred_element_type=F32)
    xo = x_ref[...] + g1_ref[...] * y
    xo_ref[...] = xo
    ms = jnp.mean(xo * xo, axis=-1, keepdims=True)
    h2 = (xo * lax.rsqrt(ms + NORM_EPS)) * (n2_ref[...] * (1.0 + sc_ref[...])) + sh_ref[...]
    hi = h2.astype(BF16)
    hi_f = hi.astype(F32)
    h2_ref[...] = _pack_rounded_pairs(hi_f)
    lo = (h2 - hi_f).astype(BF16)
    both = jnp.dot(hi, wr_ref[...], preferred_element_type=F32)
    lg = both[:, :LANES] + both[:, LANES:] + jnp.dot(lo, wr_ref[:, :LANES], preferred_element_type=F32)
    lg_ref[...] = jnp.transpose(lg)[:N_EXPERTS, :]


def _outproj_call(o_f, o_b, gate, gn, swa, w_out_b, x2d, mods5, layer, row_of_tile, n2, wr_cat, tm):
    n, d = x2d.shape
    nt = n // tm

    def mod_spec(k):
        return pl.BlockSpec((None, None, None, 1, d), lambda t: (layer, row_of_tile(t), k, 0, 0))

    return pl.pallas_call(
        _outproj_kernel,
        grid=(nt,),
        in_specs=[
            pl.BlockSpec((tm, GLA_WIDTH), lambda t: (t, 0)),
            pl.BlockSpec((tm, GLA_WIDTH), lambda t: (t, 0)),
            pl.BlockSpec((tm, GLA_WIDTH), lambda t: (t, 0)),
            pl.BlockSpec((1, GLA_WIDTH), lambda t: (0, 0)),
            pl.BlockSpec((tm, SWA_WIDTH), lambda t: (t, 0)),
            pl.BlockSpec((None, d, d), lambda t: (layer, 0, 0)),
            pl.BlockSpec((tm, d), lambda t: (t, 0)),
            mod_spec(2),
            pl.BlockSpec((1, d), lambda t: (0, 0)),
            mod_spec(4), mod_spec(3),
            pl.BlockSpec((d, 2 * LANES), lambda t: (0, 0)),
        ],
        out_specs=[
            pl.BlockSpec((tm, d), lambda t: (t, 0)),
            pl.BlockSpec((tm, d // 2), lambda t: (t, 0)),
            pl.BlockSpec((N_EXPERTS, tm), lambda t: (0, t)),
        ],
        out_shape=[
            jax.ShapeDtypeStruct((n, d), F32),
            jax.ShapeDtypeStruct((n, d // 2), U32),
            jax.ShapeDtypeStruct((N_EXPERTS, n), F32),
        ],
        compiler_params=_cparams(("arbitrary",)),
        name="outproj",
    )(o_f, o_b, gate, gn.reshape(1, GLA_WIDTH), swa, w_out_b, x2d, mods5, n2.reshape(1, d), mods5, mods5, wr_cat)


def _first_index(vals, target):
    idx = jnp.full(target.shape, len(vals) - 1, I32)
    for i in range(len(vals) - 2, -1, -1):
        idx = jnp.where(vals[i] == target, i, idx)
    return idx


def _route_kernel(lg_ref, br_ref, io_ref, wo_ref, seg_ref, pad_ref, blk_ref, cnt_scr, seg_scr, pad_scr, *, tn):
    phase = pl.program_id(0)
    step = pl.program_id(1)
    group = lg_ref.shape[1] // tn

    @pl.when((phase == 0) & (step == 0))
    def _():
        cnt_scr[...] = jnp.zeros_like(cnt_scr)
        seg_scr[...] = jnp.zeros_like(seg_scr)
        pad_scr[...] = jnp.zeros_like(pad_scr)

    @pl.when((phase == 1) & (step == 0))
    def _():
        seg_len = jnp.floor((cnt_scr[...] + (SUBLANES - 1)) * (1.0 / SUBLANES)) * SUBLANES
        rows_e = jnp.sum(seg_len, axis=1, keepdims=True)
        blocks = jnp.floor((rows_e + (MOE_BLOCK - 1)) * (1.0 / MOE_BLOCK)) * MOE_BLOCK
        r128 = lax.broadcasted_iota(I32, (LANES, LANES), 0)
        c128 = lax.broadcasted_iota(I32, (LANES, LANES), 1)
        before_tile = jnp.where(r128 < c128, 1.0, 0.0).astype(BF16)
        seg = jnp.dot((seg_len * (1.0 / SUBLANES)).astype(BF16), before_tile,
                      preferred_element_type=F32) * SUBLANES
        run = jnp.zeros((1, 1), F32)
        for e in range(N_EXPERTS):
            seg_scr[e:e + 1, :] = seg[e:e + 1, :] + run
            run = run + blocks[e:e + 1, :]
        pad_scr[...] = seg_len
        seg_ref[...] = seg_scr[...].astype(I32)
        pad_ref[...] = seg_len.astype(I32)
        blk_ref[...] = jnp.broadcast_to(blocks, blk_ref.shape).astype(I32)

    for g in range(group):
        lanes = slice(g * tn, (g + 1) * tn)
        _route_tile(lg_ref[:, lanes], br_ref, io_ref, wo_ref, lanes, step * group + g, phase, cnt_scr, pad_scr)


def _route_tile(lg, br_ref, io_ref, wo_ref, lanes, t, phase, cnt_scr, pad_scr):
    tn = lg.shape[1]
    tile_lane = lax.broadcasted_iota(I32, (N_EXPERTS, LANES), 1)
    s = jax.nn.sigmoid(lg)
    sb = s + br_ref[...]
    rows_s = [s[e:e + 1, :] for e in range(N_EXPERTS)]
    rows_b = [sb[e:e + 1, :] for e in range(N_EXPERTS)]
    gscore, gi1, gi2 = [], [], []
    epg = EXPERTS_PER_GROUP
    for g in range(N_GROUPS):
        a = rows_b[g * epg:(g + 1) * epg]
        m1 = functools.reduce(jnp.maximum, a)
        i1 = _first_index(a, m1)
        rest = [jnp.where(i1 == i, -jnp.inf, a[i]) for i in range(epg)]
        m2 = functools.reduce(jnp.maximum, rest)
        i2 = _first_index(rest, m2)
        gscore.append(m1 + m2)
        gi1.append(i1)
        gi2.append(i2)
    gm = functools.reduce(jnp.maximum, gscore)
    gsel = _first_index(gscore, gm)
    i1 = gi1[N_GROUPS - 1]
    i2 = gi2[N_GROUPS - 1]
    for g in range(N_GROUPS - 2, -1, -1):
        i1 = jnp.where(gsel == g, gi1[g], i1)
        i2 = jnp.where(gsel == g, gi2[g], i2)
    idx0 = gsel * epg + i1
    idx1 = gsel * epg + i2
    s0 = jnp.zeros_like(rows_s[0])
    s1 = jnp.zeros_like(rows_s[0])
    for e in range(N_EXPERTS):
        s0 = jnp.where(idx0 == e, rows_s[e], s0)
        s1 = jnp.where(idx1 == e, rows_s[e], s1)
    tot = s0 + s1
    w0 = s0 / tot
    w1 = s1 / tot

    eidx = lax.broadcasted_iota(I32, (N_EXPERTS, tn), 0)
    oh0 = eidx == idx0
    oh1 = eidx == idx1
    oh = jnp.where(oh0 | oh1, 1.0, 0.0)

    @pl.when(phase == 0)
    def _():
        cnt_scr[...] = cnt_scr[...] + jnp.where(tile_lane == t, jnp.sum(oh, axis=1, keepdims=True), 0.0)

    @pl.when(phase == 1)
    def _():
        rr = lax.broadcasted_iota(I32, (tn, tn), 0)
        cc = lax.broadcasted_iota(I32, (tn, tn), 1)
        upper = jnp.where(rr < cc, 1.0, 0.0).astype(BF16)
        before = jnp.dot(oh.astype(BF16), upper, preferred_element_type=F32)
        pad_col = jnp.sum(jnp.where(tile_lane == t, pad_scr[...], 0.0), axis=1, keepdims=True)
        run = jnp.zeros((1, 1), F32)
        offs = []
        for e in range(N_EXPERTS):
            offs.append(run)
            run = run + pad_col[e:e + 1, :]
        pos = before + jnp.concatenate(offs, axis=0)
        spos0 = jnp.sum(jnp.where(oh0, pos, 0.0), axis=0, keepdims=True)
        spos1 = jnp.sum(jnp.where(oh1, pos, 0.0), axis=0, keepdims=True)
        zi = jnp.zeros((SUBLANES - 4, tn), I32)
        io_ref[:, lanes] = jnp.concatenate([idx0, idx1, spos0.astype(I32), spos1.astype(I32), zi], axis=0)
        wo_ref[:, lanes] = jnp.concatenate([w0, w1, jnp.zeros((SUBLANES - 2, tn), F32)], axis=0)


def _route_call(logits_t, b_router, tn):
    ne, t_all = logits_t.shape
    ntile = t_all // tn
    assert ntile <= LANES
    group = max(g for g in range(1, ROUTE_TILES_PER_STEP + 1) if ntile % g == 0)
    tw = group * tn
    tok_out = pl.BlockSpec((SUBLANES, tw), lambda p, t: (0, t * p))
    tab_out = pl.BlockSpec((ne, LANES), lambda p, t: (0, 0))
    tab = jax.ShapeDtypeStruct((ne, LANES), I32)
    return pl.pallas_call(
        functools.partial(_route_kernel, tn=tn),
        grid=(2, ntile // group),
        in_specs=[pl.BlockSpec((ne, tw), lambda p, t: (0, t)), pl.BlockSpec((ne, 1), lambda p, t: (0, 0))],
        out_specs=[tok_out, tok_out, tab_out, tab_out, tab_out],
        out_shape=[
            jax.ShapeDtypeStruct((SUBLANES, t_all), I32),
            jax.ShapeDtypeStruct((SUBLANES, t_all), F32),
            tab, tab, tab,
        ],
        scratch_shapes=[pltpu.VMEM((ne, LANES), F32)] * 3,
        compiler_params=_cparams(("arbitrary", "arbitrary")),
        name="route",
    )(logits_t, b_router.reshape(ne, 1))


def _segment_copies(seg_ref, pad_ref, t, stage, hbm, sem, to_hbm, wait):
    if wait:
        total = functools.reduce(lambda a, b: a + b, [pad_ref[e * LANES + t] for e in range(N_EXPERTS)])
        for size in (2 * SEG_SIZES[0],) + SEG_SIZES:
            @pl.when((total & size) != 0)
            def _(size=size):
                src, dst = stage.at[pl.ds(0, size), :], hbm.at[pl.ds(0, size), :]
                if not to_hbm:
                    src, dst = dst, src
                pltpu.make_async_copy(src, dst, sem).wait()
        return
    loc = 0
    for e in range(N_EXPERTS):
        n = pad_ref[e * LANES + t]
        start = seg_ref[e * LANES + t]
        for size in SEG_SIZES:
            off = n & (-2 * size)

            @pl.when((n & size) != 0)
            def _(off=off, size=size, loc=loc, start=start):
                s_rows = pl.ds(pl.multiple_of(loc + off, SUBLANES), size)
                h_rows = pl.ds(pl.multiple_of(start + off, SUBLANES), size)
                src, dst = (stage.at[s_rows, :], hbm.at[h_rows, :])
                if not to_hbm:
                    src, dst = dst, src
                pltpu.make_async_copy(src, dst, sem).start()
        loc = loc + n


def _sort_matrix(io_ref, rows):
    r = lax.broadcasted_iota(I32, (rows, io_ref.shape[1]), 0)
    return r == io_ref[2:3, :], r == io_ref[3:4, :]


def _dispatch_kernel(seg_ref, pad_ref, zblk_ref, *refs, tiles):
    srcs = refs[:len(tiles)]
    io_ref, wo_ref, xb_hbm, stage, zbuf, sem = refs[len(tiles):]
    i = pl.program_id(0)
    m = MOE_BLOCK

    def zero_copy(j):
        start = pl.multiple_of(zblk_ref[j] * m, m)
        return pltpu.make_async_copy(zbuf, xb_hbm.at[pl.ds(start, m), :], sem.at[0])

    @pl.when(i == 0)
    def _():
        zbuf[...] = jnp.zeros_like(zbuf)
        for j in range(zblk_ref.shape[0]):
            @pl.when(zblk_ref[j] >= 0)
            def _():
                zero_copy(j).start()
        for j in range(zblk_ref.shape[0]):
            @pl.when(zblk_ref[j] >= 0)
            def _():
                zero_copy(j).wait()

    hp = srcs[0][...]
    if len(srcs) == 2:
        hp = jnp.where(i < tiles[0], hp, srcs[1][...])
    x = _unpack_bf16_pairs(hp)
    m0, m1 = _sort_matrix(io_ref, STAGE_ROWS)
    sort = jnp.where(m0 | m1, 1.0, 0.0).astype(BF16)
    xs = jnp.dot(sort, x, preferred_element_type=F32)
    dh = x.shape[1] // 2
    slot = i % 2
    cur = stage.at[slot]
    cur[:, :dh] = _pack_rounded_pairs(xs)
    ws = jnp.sum(jnp.where(m0, wo_ref[0:1, :], 0.0) + jnp.where(m1, wo_ref[1:2, :], 0.0), axis=1, keepdims=True)
    cur[:, dh:] = jnp.broadcast_to(lax.bitcast_convert_type(ws, U32), (STAGE_ROWS, LANES))
    _segment_copies(seg_ref, pad_ref, i, cur, xb_hbm, sem.at[slot], to_hbm=True, wait=False)

    @pl.when(i > 0)
    def _():
        _segment_copies(seg_ref, pad_ref, i - 1, stage.at[1 - slot], xb_hbm, sem.at[1 - slot], to_hbm=True, wait=True)

    @pl.when(i == pl.num_programs(0) - 1)
    def _():
        _segment_copies(seg_ref, pad_ref, i, cur, xb_hbm, sem.at[slot], to_hbm=True, wait=True)


def _dispatch_call(seg, pad, zblk, io, wo, sources, p_rows):
    tile = ROUTE_TILE
    dh = sources[0].shape[1]
    tiles = tuple(s.shape[0] // tile for s in sources)
    firsts = tuple(sum(tiles[:k]) for k in range(len(tiles)))

    def src_spec(first, ntile):
        return pl.BlockSpec((tile, dh), lambda i, *_: (jnp.clip(i - first, 0, ntile - 1), 0))

    tok_spec = pl.BlockSpec((SUBLANES, tile), lambda i, *_: (0, i))
    grid_spec = pltpu.PrefetchScalarGridSpec(
        num_scalar_prefetch=3,
        grid=(sum(tiles),),
        in_specs=[src_spec(f, n) for f, n in zip(firsts, tiles)] + [tok_spec, tok_spec],
        out_specs=pl.BlockSpec(memory_space=pl.ANY),
        scratch_shapes=[pltpu.VMEM((2, STAGE_ROWS, dh + LANES), U32), pltpu.VMEM((MOE_BLOCK, dh + LANES), U32),
                        pltpu.SemaphoreType.DMA((2,))],
    )
    return pl.pallas_call(
        functools.partial(_dispatch_kernel, tiles=tiles),
        grid_spec=grid_spec,
        out_shape=jax.ShapeDtypeStruct((p_rows, dh + LANES), U32),
        compiler_params=_cparams(("arbitrary",)),
        name="dispatch",
    )(seg, pad, zblk, *sources, io, wo)


def _expert_kernel(be_ref, nu_ref, va_ref, ne_ref, x_ref, wg_hbm, wu_hbm, wd_hbm, o_ref,
                   wg_b, wu_b, wd_b, wg_f, wu_f, wd_f, switch_ref, sem, *, layer):
    j = pl.program_id(0)
    dh = o_ref.shape[1]
    half = o_ref.shape[0] // 2
    valid = va_ref[j]
    e = be_ref[j]

    def weight_copies(expert, slot):
        return [pltpu.make_async_copy(hbm.at[layer, expert], buf.at[slot], sem.at[slot, k])
                for k, (hbm, buf) in enumerate(((wg_hbm, wg_f), (wu_hbm, wu_f), (wd_hbm, wd_f)))]

    @pl.when(j == 0)
    def _():
        switch_ref[0] = 0
        for cp in weight_copies(e, 0):
            cp.start()

    @pl.when((j == 0) | (e != be_ref[jnp.maximum(j - 1, 0)]))
    def _():
        slot = switch_ref[0] % 2
        for cp in weight_copies(e, slot):
            cp.wait()
        wg_b[...] = wg_f[slot].astype(BF16)
        wu_b[...] = wu_f[slot].astype(BF16)
        wd_b[...] = wd_f[slot].astype(BF16)
        nxt = ne_ref[e]

        @pl.when(nxt >= 0)
        def _():
            for cp in weight_copies(nxt, 1 - slot):
                cp.start()
        switch_ref[0] = switch_ref[0] + 1

    def ffn_halves(halves):
        gates = []
        for r in halves:
            x = _unpack_bf16_pairs(x_ref[r * half:(r + 1) * half, :dh])
            gates.append((jnp.dot(x, wg_b[...], preferred_element_type=F32),
                          jnp.dot(x, wu_b[...], preferred_element_type=F32)))
        for r, (a, u) in zip(halves, gates):
            hmid = (a * jax.nn.sigmoid(a)) * u
            y = jnp.dot(hmid.astype(BF16), wd_b[...], preferred_element_type=F32)
            row_w = lax.bitcast_convert_type(x_ref[r * half:(r + 1) * half, dh:dh + 1], F32)
            o_ref[r * half:(r + 1) * half, :] = _pack_bf16_pairs(y * row_w)

    @pl.when(valid > half)
    def _():
        ffn_halves((0, 1))

    @pl.when((valid > 0) & (valid <= half))
    def _():
        ffn_halves((0,))
        o_ref[half:, :] = jnp.zeros((half, dh), o_ref.dtype)

    @pl.when(valid == 0)
    def _():
        o_ref[...] = jnp.zeros_like(o_ref)


def _expert_call(block_e, nused, valid, next_e, xb, wg, wu, wd, layer):
    p_rows = xb.shape[0]
    dh = xb.shape[1] - LANES
    d = 2 * dh
    m = MOE_BLOCK
    nb = p_rows // m
    de = wg.shape[-1]

    def xmap(j, be, nu, va, ne):
        return (jnp.minimum(j, nu[0] - 1), 0)

    any_spec = pl.BlockSpec(memory_space=pl.ANY)
    grid_spec = pltpu.PrefetchScalarGridSpec(
        num_scalar_prefetch=4,
        grid=(nb,),
        in_specs=[pl.BlockSpec((m, dh + LANES), xmap), any_spec, any_spec, any_spec],
        out_specs=pl.BlockSpec((m, dh), lambda j, be, nu, va, ne: (j, 0)),
        scratch_shapes=[
            pltpu.VMEM((d, de), BF16), pltpu.VMEM((d, de), BF16), pltpu.VMEM((de, d), BF16),
            pltpu.VMEM((2, d, de), F32), pltpu.VMEM((2, d, de), F32), pltpu.VMEM((2, de, d), F32),
            pltpu.SMEM((1,), I32), pltpu.SemaphoreType.DMA((2, 3)),
        ],
    )
    return pl.pallas_call(
        functools.partial(_expert_kernel, layer=layer),
        grid_spec=grid_spec,
        out_shape=jax.ShapeDtypeStruct((p_rows, dh), U32),
        compiler_params=_cparams(("arbitrary",)),
        name="experts",
    )(block_e, nused, valid, next_e, xb, wg, wu, wd)


def _combine_kernel(*refs, tile0, final):
    if final:
        seg_ref, pad_ref, yb_hbm, io_ref, x_ref, g2_ref, fn_ref, o_ref, stage, sem = refs
    else:
        seg_ref, pad_ref, yb_hbm, io_ref, x_ref, g2_ref, o_ref, stage, sem = refs
    i = pl.program_id(0)

    slot = i % 2

    @pl.when(i == 0)
    def _():
        stage[...] = jnp.zeros_like(stage)
        _segment_copies(seg_ref, pad_ref, tile0, stage.at[0], yb_hbm, sem.at[0], to_hbm=False, wait=False)

    @pl.when(i + 1 < pl.num_programs(0))
    def _():
        _segment_copies(seg_ref, pad_ref, tile0 + i + 1, stage.at[1 - slot], yb_hbm, sem.at[1 - slot],
                        to_hbm=False, wait=False)

    _segment_copies(seg_ref, pad_ref, tile0 + i, stage.at[slot], yb_hbm, sem.at[slot], to_hbm=False, wait=True)
    rows = _unpack_bf16_pairs(stage[slot])
    m0, m1 = _sort_matrix(io_ref, STAGE_ROWS)
    pick = jnp.where(m0 | m1, 1.0, 0.0).astype(BF16)
    y = lax.dot_general(pick, rows, (((0,), (0,)), ((), ())), preferred_element_type=F32)
    xo = x_ref[...] + g2_ref[...] * y
    if final:
        ms = jnp.mean(xo * xo, axis=-1, keepdims=True)
        xo = (xo * lax.rsqrt(ms + NORM_EPS)) * fn_ref[...]
    o_ref[...] = xo


def _combine_call(seg, pad, yb, io, x2d, mods5, layer, row_of_tile, final_g, tile0):
    n, d = x2d.shape
    tm = ROUTE_TILE
    final = final_g is not None
    in_specs = [
        pl.BlockSpec(memory_space=pl.ANY),
        pl.BlockSpec((SUBLANES, tm), lambda t, *_: (0, tile0 + t)),
        pl.BlockSpec((tm, d), lambda t, *_: (t, 0)),
        pl.BlockSpec((None, None, None, 1, d), lambda t, *_: (layer, row_of_tile(t), 5, 0, 0)),
    ]
    args = [yb, io, x2d, mods5]
    if final:
        in_specs.append(pl.BlockSpec((1, d), lambda t, *_: (0, 0)))
        args.append(final_g.reshape(1, d))
    grid_spec = pltpu.PrefetchScalarGridSpec(
        num_scalar_prefetch=2,
        grid=(n // tm,),
        in_specs=in_specs,
        out_specs=pl.BlockSpec((tm, d), lambda t, *_: (t, 0)),
        scratch_shapes=[pltpu.VMEM((2, STAGE_ROWS, d // 2), U32), pltpu.SemaphoreType.DMA((2,))],
    )
    return pl.pallas_call(
        functools.partial(_combine_kernel, tile0=tile0, final=final),
        grid_spec=grid_spec,
        out_shape=jax.ShapeDtypeStruct((n, d), F32),
        compiler_params=_cparams(("arbitrary",)),
        name="combine",
    )(seg, pad, *args)


def _pack_w_in(w):
    assert sum(IN_SIZES[:6]) == C_Z + 2 * GLA_GATE_RANK and sum(IN_SIZES) - sum(IN_SIZES[:6]) == C_END - C_SQ
    wb = w.astype(BF16)
    cut = C_Z + 2 * GLA_GATE_RANK
    out = jnp.zeros(w.shape[:-1] + (C_END,), BF16)
    out = lax.dynamic_update_slice(out, wb[..., :cut], (0, 0, 0))
    return lax.dynamic_update_slice(out, wb[..., cut:], (0, 0, C_SQ))


def _pad_up(up, row0):
    up = up.reshape(GLA_GATE_RANK, GLA_HEADS // 2, LANES).transpose(1, 0, 2)
    hi = up.astype(BF16)
    lo = (up - hi.astype(F32)).astype(BF16)
    out = jnp.zeros((GLA_HEADS // 2, LANES, LANES), BF16)
    for group, part in enumerate((hi, hi, lo)):
        r = group * 2 * GLA_GATE_RANK + row0
        out = out.at[:, r:r + GLA_GATE_RANK, :].set(part)
    return out


def _rope_tables(seq):
    rows = seq // GRID_W
    dim = jnp.arange(LANES, dtype=I32) % SWA_HD
    inv = ROPE_THETA ** (-((dim % ROPE_HALF).astype(F32) * 2.0 / ROPE_AXIS_DIM))
    sign = jnp.where(dim % ROPE_AXIS_DIM < ROPE_HALF, -1.0, 1.0)
    by_row = (dim < ROPE_AXIS_DIM)[None, None, :]
    ang_r = (jnp.arange(rows, dtype=F32)[:, None] * inv[None, :])[:, None, :]
    ang_c = (jnp.arange(GRID_W, dtype=F32)[:, None] * inv[None, :])[None, :, :]
    cos = jnp.where(by_row, jnp.cos(ang_r), jnp.cos(ang_c))
    sin = jnp.where(by_row, jnp.sin(ang_r), jnp.sin(ang_c)) * sign
    return cos.reshape(seq, LANES), sin.reshape(seq, LANES)


def _tile(n, pref):
    t = pref
    while n % t:
        t //= 2
    return t


def kernel(x, c, ctx, c_ctx, w_ada, b_ada, norm1, norm2, w_in, gla_up_f, gla_bias_f, gla_up_b, gla_bias_b,
           gla_norm, swa_sink, w_out, w_router, b_router, w_gate, w_up, w_down, final_norm):
    batch, seq, d = x.shape
    lc = ctx.shape[1]
    depth = w_ada.shape[0]
    n_l = batch * seq
    n_c = batch * lc
    npair = GLA_HEADS // 2
    assert batch + 1 <= SUBLANES
    assert n_l % ROUTE_TILE == 0 and n_c % ROUTE_TILE == 0

    tm_l = _tile(seq, ROW_TILE)
    tm_in = _tile(seq, INPROJ_TILE)
    tm_c = _tile(lc, CTX_TILE)
    tl_l = _tile(seq, GLA_TILE)
    tl_c = _tile(lc, ROW_TILE)

    rows = jnp.zeros((SUBLANES, d), F32).at[:batch].set(c).at[batch].set(c_ctx)
    mods = _ada_call(rows, w_ada, b_ada)
    mods5 = mods.reshape(depth, SUBLANES, 6, 1, d)

    tabs = _rope_tables(seq)
    w_packed = _pack_w_in(w_in)
    w_out_b = w_out.astype(BF16)
    wr_pad = jnp.zeros((d, LANES), F32).at[:, :N_EXPERTS].set(w_router)
    wr_hi = wr_pad.astype(BF16)
    wr_lo = (wr_pad - wr_hi.astype(F32)).astype(BF16)
    wr_cat = jnp.concatenate([wr_hi, wr_lo], axis=-1)

    def lat_row(tm):
        return lambda t: t // (seq // tm)

    def ctx_row(tm):
        return lambda t: batch

    xl = x.reshape(n_l, d)
    xc = ctx.reshape(n_c, d)
    for i in range(depth):
        last = i == depth - 1
        upf = _pad_up(gla_up_f[i], 0)
        upb = _pad_up(gla_up_b[i], GLA_GATE_RANK)
        bias_f = gla_bias_f[i].reshape(npair, 1, LANES)
        bias_b = gla_bias_b[i].reshape(npair, 1, LANES)

        c_qk, c_v, c_g, c_z, c_sq, c_sk, c_sv = _inproj_call(
            xc, norm1[i], mods5, i, ctx_row(tm_c), w_packed, None, tm_c, lc // tm_c)
        l_qk, l_v, l_g, l_z, l_sq, l_sk, l_sv = _inproj_call(
            xl, norm1[i], mods5, i, lat_row(tm_in), w_packed, tabs, tm_in, seq // tm_in)

        s_zero = jnp.zeros((batch, npair, 2, 2 * GLA_DV, LANES), F32)
        oc_f, oc_b, s_ctx = _gla_call(c_qk, c_v, c_z, upf, upb, bias_f, bias_b, s_zero, batch, lc, tl_c)
        ol_f, ol_b, _ = _gla_call(l_qk, l_v, l_z, upf, upb, bias_f, bias_b, s_ctx, batch, seq, tl_l)

        swa_l = _swa_call(l_sq, l_sk, l_sv, c_sk, c_sv, swa_sink[i], batch, seq, lc)
        xl, h2l, lg_l = _outproj_call(ol_f, ol_b, l_g, gla_norm[i], swa_l, w_out_b, xl, mods5, i, lat_row(tm_l),
                                      norm2[i], wr_cat, tm_l)
        if last:
            logits_t = lg_l
            t_all = n_l
        else:
            swa_c = _swa_ctx_call(c_sq, c_sk, c_sv, swa_sink[i], batch, lc)
            xc, h2c, lg_c = _outproj_call(oc_f, oc_b, c_g, gla_norm[i], swa_c, w_out_b, xc, mods5, i, ctx_row(tm_c),
                                          norm2[i], wr_cat, tm_c)
            logits_t = jnp.concatenate([lg_l, lg_c], axis=1)
            t_all = n_l + n_c

        io, wo, seg, pad, blk = _route_call(logits_t, b_router, ROUTE_TILE)
        m = MOE_BLOCK
        padded = blk[:, 0]
        pend = jnp.cumsum(padded)
        ntile = t_all // ROUTE_TILE
        nb = -(-(t_all * TOP_K + ntile * N_EXPERTS * (SUBLANES - 1)) // m) + N_EXPERTS
        p_rows = nb * m
        seg = seg.reshape(-1)
        pad = pad.reshape(-1)
        nused = (pend[-1:] // m).astype(I32)
        tail = nused + jnp.arange(nb - (t_all * TOP_K) // m, dtype=I32)
        zblk = jnp.concatenate([jnp.where(padded > 0, pend // m - 1, -1),
                                jnp.where(tail < nb, tail, -1)]).astype(I32)
        blk_row = jnp.minimum(jnp.arange(nb, dtype=I32), nused - 1) * m
        block_e = jnp.sum((pend[None, :] <= blk_row[:, None]).astype(I32), axis=1)
        rows_e = jnp.sum(pad.reshape(N_EXPERTS, LANES), axis=1)
        own = (block_e[:, None] == jnp.arange(N_EXPERTS, dtype=I32)[None, :]).astype(I32)
        seg_end = jnp.sum(own * (pend - padded + rows_e)[None, :], axis=1)
        blk_idx = jnp.arange(nb, dtype=I32)
        valid = jnp.where(blk_idx < nused, jnp.clip(seg_end - blk_idx * m, 0, m), 0).astype(I32)

        sources = (h2l,) if last else (h2l, h2c)
        xb = _dispatch_call(seg, pad, zblk, io, wo, sources, p_rows)
        eid = jnp.arange(N_EXPERTS, dtype=I32)
        later = (eid[None, :] > eid[:, None]) & (padded[None, :] > 0)
        next_e = jnp.min(jnp.where(later, eid[None, :], N_EXPERTS), axis=1)
        next_e = jnp.where(next_e < N_EXPERTS, next_e, -1).astype(I32)
        yb = _expert_call(block_e, nused, valid, next_e, xb, w_gate, w_up, w_down, i)
        xl_new = _combine_call(seg, pad, yb, io, xl, mods5, i, lat_row(ROUTE_TILE),
                               final_norm if last else None, 0)
        if not last:
            xc = _combine_call(seg, pad, yb, io, xc, mods5, i, ctx_row(0), None, n_l // ROUTE_TILE)
        xl = xl_new
    return xl.reshape(batch, seq, d)
```

```python
import functools

import jax
import jax.numpy as jnp
from jax import lax
from jax.experimental import pallas as pl
from jax.experimental.pallas import tpu as pltpu

F32 = jnp.float32
BF16 = jnp.bfloat16
I32 = jnp.int32
U32 = jnp.uint32

GRID_W = 64
NORM_EPS = 1e-6
GLA_HEADS = 4
GLA_DK = 64
GLA_DV = 128
GLA_KW = GLA_HEADS * GLA_DK
GLA_WIDTH = GLA_HEADS * GLA_DV
GLA_GATE_RANK = 16
GLA_TAU = 16.0
GLA_CHUNK = 64
SWA_HEADS = 8
SWA_KV_HEADS = 2
SWA_HD = 64
SWA_WIDTH = SWA_HEADS * SWA_HD
SWA_KVW = SWA_KV_HEADS * SWA_HD
WINDOW = 128
ATT_BLOCK = 128
ROPE_THETA = 10000.0
ROPE_AXIS_DIM = SWA_HD // 2
ROPE_HALF = ROPE_AXIS_DIM // 2
N_EXPERTS = 16
N_GROUPS = 4
EXPERTS_PER_GROUP = N_EXPERTS // N_GROUPS
TOP_K = 2
IN_SIZES = (GLA_KW, GLA_KW, GLA_WIDTH, GLA_WIDTH, GLA_GATE_RANK, GLA_GATE_RANK, SWA_WIDTH, SWA_KVW, SWA_KVW)

LANES = 128
SUBLANES = 8
VMEM_LIMIT = 48 * 1024 * 1024

ROW_TILE = 512
GLA_TILE = 1024
GLA_SUM_ROWS = 256
INPROJ_TILE = 1024
CTX_TILE = 256
ADA_COLS = 1536
SWA_BLOCKS_PER_STEP = 4
MOE_BLOCK = 512
ROUTE_TILE = 512
ROUTE_TILES_PER_STEP = 4
SEG_SIZES = (512, 256, 128, 64, 32, 16, 8)
STAGE_ROWS = TOP_K * ROUTE_TILE + LANES

C_QK = 0
C_V = 512
C_G = 1024
C_Z = 1536
C_SQ = 1664
C_SK = 2176
C_SV = 2304
C_END = 2432
SV_LANES = 4 * LANES
NEG_BIG = -1e30

assert WINDOW == ATT_BLOCK


def _cparams(sem):
    return pltpu.CompilerParams(dimension_semantics=sem, vmem_limit_bytes=VMEM_LIMIT)


def _pack_bf16_pairs(x):
    return _pack_rounded_pairs(x.astype(BF16).astype(F32))


def _pack_rounded_pairs(xr):
    n = xr.shape[1] // 2
    lo = lax.bitcast_convert_type(xr[:, :n], U32)
    hi = lax.bitcast_convert_type(xr[:, n:], U32)
    return (lo >> 16) | hi


def _unpack_bf16_pairs(w):
    lo = lax.bitcast_convert_type(w << 16, F32)
    hi = lax.bitcast_convert_type(w & jnp.uint32(0xFFFF0000), F32)
    return jnp.concatenate([lo.astype(BF16), hi.astype(BF16)], axis=-1)


def _ada_kernel(a_ref, w_ref, b_ref, o_ref):
    a = a_ref[...]
    act = a * jax.nn.sigmoid(a)
    a_hi = act.astype(BF16)
    a_lo = (act - a_hi.astype(F32)).astype(BF16)
    w = w_ref[...]
    w_hi = w.astype(BF16)
    w_lo = (w - w_hi.astype(F32)).astype(BF16)
    t = jnp.dot(jnp.concatenate([a_hi, a_lo], axis=0), w_hi, preferred_element_type=F32)
    n = a.shape[0]
    o_ref[...] = t[:n] + t[n:] + jnp.dot(a_hi, w_lo, preferred_element_type=F32) + b_ref[...]


def _ada_call(rows, w_ada, b_ada):
    depth, d, n6 = w_ada.shape
    tn = ADA_COLS
    return pl.pallas_call(
        _ada_kernel,
        grid=(depth, n6 // tn),
        in_specs=[
            pl.BlockSpec((SUBLANES, d), lambda l, j: (0, 0)),
            pl.BlockSpec((None, d, tn), lambda l, j: (l, 0, j)),
            pl.BlockSpec((None, 1, tn), lambda l, j: (l, 0, j)),
        ],
        out_specs=pl.BlockSpec((None, SUBLANES, tn), lambda l, j: (l, 0, j)),
        out_shape=jax.ShapeDtypeStruct((depth, SUBLANES, n6), F32),
        compiler_params=_cparams(("arbitrary", "arbitrary")),
        name="adaln",
    )(rows, w_ada, b_ada.reshape(depth, 1, n6))


def _rope_tile(xj, cos, sin, lane_lo):
    partner = jnp.where(lane_lo, pltpu.roll(xj, LANES - ROPE_HALF, 1), pltpu.roll(xj, ROPE_HALF, 1))
    return xj * cos + partner * sin


def _inproj_kernel(*refs, rope):
    if rope:
        (x_ref, g_ref, sc_ref, sh_ref, w_ref, cos_ref, sin_ref,
         qk_ref, v_ref, gate_ref, z_ref, sq_ref, sk_ref, sv_ref) = refs
    else:
        (x_ref, g_ref, sc_ref, sh_ref, w_ref,
         qk_ref, v_ref, gate_ref, z_ref, sq_ref, sk_ref, sv_ref) = refs
    half = x_ref.shape[0] // 2
    scale = g_ref[...] * (1.0 + sc_ref[...])
    normed = []
    for r in range(2):
        x = x_ref[r * half:(r + 1) * half, :]
        ms = jnp.mean(x * x, axis=-1, keepdims=True)
        normed.append(((x * lax.rsqrt(ms + NORM_EPS)) * scale + sh_ref[...]).astype(BF16))

    for r in range(2):
        rows = slice(r * half, (r + 1) * half)
        hb = normed[r]

        def proj(a, b):
            return jnp.dot(hb, w_ref[:, a:b], preferred_element_type=F32)

        qk = proj(C_QK, C_V)
        qk_ref[rows, :GLA_KW] = (qk[:, :GLA_KW] * (GLA_DK ** -0.5)).astype(qk_ref.dtype)
        qk_ref[rows, GLA_KW:] = qk[:, GLA_KW:].astype(qk_ref.dtype)
        v_ref[rows, :] = proj(C_V, C_G).astype(v_ref.dtype)
        gate_ref[rows, :] = proj(C_G, C_Z).astype(gate_ref.dtype)
        z_ref[rows, :] = proj(C_Z, C_SQ)
        sq = proj(C_SQ, C_SK) * (SWA_HD ** -0.5)
        sk = proj(C_SK, C_SV)
        sv = proj(C_SV, C_END)
        if rope:
            cos = cos_ref[rows, :]
            sin = sin_ref[rows, :]
            lane_lo = (lax.broadcasted_iota(I32, cos.shape, 1) % ROPE_AXIS_DIM) < ROPE_HALF
            for j in range(SWA_WIDTH // LANES):
                sl = slice(j * LANES, (j + 1) * LANES)
                sq_ref[rows, sl] = _rope_tile(sq[:, sl], cos, sin, lane_lo).astype(sq_ref.dtype)
            sk = _rope_tile(sk, cos, sin, lane_lo)
        else:
            sq_ref[rows, :] = sq.astype(sq_ref.dtype)
        first = lax.broadcasted_iota(I32, sk.shape, 1) < SWA_HD
        sk_sw = pltpu.roll(sk, SWA_HD, 1)
        sk_ref[rows, :LANES] = jnp.where(first, sk, sk_sw).astype(sk_ref.dtype)
        sk_ref[rows, LANES:] = jnp.where(first, sk_sw, sk).astype(sk_ref.dtype)
        sv_sw = pltpu.roll(sv, SWA_HD, 1)
        sv_ref[rows, 0 * LANES:1 * LANES] = jnp.where(first, sv, 1.0).astype(sv_ref.dtype)
        sv_ref[rows, 1 * LANES:2 * LANES] = jnp.where(first, 1.0, sv_sw).astype(sv_ref.dtype)
        sv_ref[rows, 2 * LANES:3 * LANES] = jnp.where(first, sv_sw, 1.0).astype(sv_ref.dtype)
        sv_ref[rows, 3 * LANES:4 * LANES] = jnp.where(first, 1.0, sv).astype(sv_ref.dtype)


def _inproj_call(x2d, g, mods5, layer, row_of_tile, w_packed, tabs, tm, tiles_per_seq):
    n, d = x2d.shape
    nt = n // tm
    rope = tabs is not None

    def mod_spec(k):
        return pl.BlockSpec((None, None, None, 1, d), lambda t: (layer, row_of_tile(t), k, 0, 0))

    in_specs = [
        pl.BlockSpec((tm, d), lambda t: (t, 0)),
        pl.BlockSpec((1, d), lambda t: (0, 0)),
        mod_spec(1), mod_spec(0),
        pl.BlockSpec((None, d, C_END), lambda t: (layer, 0, 0)),
    ]
    args = [x2d, g.reshape(1, d), mods5, mods5, w_packed]
    if rope:
        in_specs += [pl.BlockSpec((tm, LANES), lambda t: (t % tiles_per_seq, 0))] * 2
        args += [tabs[0], tabs[1]]
    widths = (2 * GLA_KW, GLA_WIDTH, GLA_WIDTH, LANES, SWA_WIDTH, 2 * SWA_KVW, SV_LANES)
    dtypes = (BF16, BF16, BF16, F32, BF16, BF16, BF16)
    return pl.pallas_call(
        functools.partial(_inproj_kernel, rope=rope),
        grid=(nt,),
        in_specs=in_specs,
        out_specs=[pl.BlockSpec((tm, w), lambda t: (t, 0)) for w in widths],
        out_shape=[jax.ShapeDtypeStruct((n, w), dt) for w, dt in zip(widths, dtypes)],
        compiler_params=_cparams(("arbitrary",)),
        name="inproj",
    )(*args)


def _log_sigmoid(x):
    return jnp.minimum(x, 0.0) - jnp.log(1.0 + jnp.exp(-jnp.abs(x)))


def _gla_prepare(q_ref, k_ref, v_ref, z_ref, up_ref, bias_ref, tri_ref, scr, fwd):
    qin_s, km0_s, km1_s, kout_s, vm0_s, vm1_s, dec_s = scr[:7]
    qbd_s = scr[9]
    tl = q_ref.shape[0]
    c = GLA_CHUNK
    z = z_ref[...]
    z_hi = z.astype(BF16).astype(F32)
    zc = z_hi + pltpu.roll(z - z_hi, 2 * GLA_GATE_RANK, 1) + pltpu.roll(z_hi, 4 * GLA_GATE_RANK, 1)
    x = jnp.dot(zc.astype(BF16), up_ref[...], preferred_element_type=F32) + bias_ref[...]
    la = _log_sigmoid(x) * (1.0 / GLA_TAU)
    hi = la.astype(BF16)
    lo = (la - hi.astype(F32)).astype(BF16)
    hl = jnp.concatenate([hi, lo], axis=1)
    tg = tri_ref.shape[0]
    sums = jnp.concatenate([jnp.dot(tri_ref[...], hl[r0:r0 + tg, :], preferred_element_type=F32)
                            for r0 in range(0, tl, tg)], axis=0)
    b = sums[:, :LANES] + sums[:, LANES:]
    b3 = b.reshape(tl // c, c, LANES)
    edge = b3[:, c - 1:c, :] if fwd else b3[:, 0:1, :]
    btot = jnp.broadcast_to(edge, b3.shape).reshape(tl, LANES)
    q = q_ref[...].astype(F32)
    k = k_ref[...].astype(F32)
    first = (lax.broadcasted_iota(I32, (tl, LANES), 0) % (2 * c)) < c
    q_in = q * jnp.exp(b)
    qin_s[...] = q_in.astype(BF16)
    qbd_s[...] = jnp.concatenate([jnp.where(first, q_in, 0.0), jnp.where(first, 0.0, q_in)], axis=1).astype(BF16)
    k_in = k * jnp.exp(-b)
    head0 = lax.broadcasted_iota(I32, (tl, LANES), 1) < GLA_DK
    km0_s[...] = jnp.where(head0, k_in, 0.0).astype(BF16)
    km1_s[...] = jnp.where(head0, 0.0, k_in).astype(BF16)
    k_out = k * jnp.exp(btot - b)
    kout_s[...] = jnp.concatenate([jnp.where(first, k_out, 0.0), jnp.where(first, 0.0, k_out)], axis=1).astype(BF16)
    dec_s[...] = jnp.exp(btot)
    vf = v_ref[...].astype(F32)
    vhead0 = lax.broadcasted_iota(I32, (tl, 2 * GLA_DV), 1) < GLA_DV
    vm0_s[...] = jnp.where(vhead0, vf, 0.0).astype(BF16)
    vm1_s[...] = jnp.where(vhead0, 0.0, vf).astype(BF16)


def _gla_increment(v_ref, scr, pair):
    kout_s, u_s = scr[3], scr[7]
    rows = pl.ds(pair * 2 * GLA_CHUNK, 2 * GLA_CHUNK)
    u_t = lax.dot_general(v_ref[rows, :], kout_s[rows, :], (((0,), (0,)), ((), ())),
                          preferred_element_type=F32)
    srow = lax.broadcasted_iota(I32, (2 * GLA_DV, 2 * LANES), 0) // GLA_DV
    scol = (lax.broadcasted_iota(I32, (2 * GLA_DV, 2 * LANES), 1) % LANES) // GLA_DK
    u_s[pair] = jnp.where(srow == scol, u_t, 0.0)


def _gla_states(scr, s_ref, chunk_order):
    dec_s, u_s, sprev_s = scr[6], scr[7], scr[8]
    s_t = s_ref[...]
    for cidx in chunk_order:
        pair, lanes = cidx // 2, pl.ds((cidx % 2) * LANES, LANES)
        sprev_s[pair, :, lanes] = s_t.astype(BF16)
        r0 = cidx * GLA_CHUNK
        s_t = s_t * dec_s[r0:r0 + 1, :] + u_s[pair, :, lanes]
    s_ref[...] = s_t


def _gla_output(o_ref, scr, pair, fwd):
    qin_s, km0_s, km1_s, _, vm0_s, vm1_s, _, _, sprev_s, qbd_s = scr
    c = GLA_CHUNK
    r0 = pair * 2 * c
    rows = pl.ds(r0, 2 * c)
    ca, cb = pl.ds(r0, c), pl.ds(r0 + c, c)
    kst = jnp.concatenate([km0_s[ca, :], km1_s[ca, :], km0_s[cb, :], km1_s[cb, :]], axis=0)
    a = lax.dot_general(qin_s[rows, :], kst, (((1,), (1,)), ((), ())), preferred_element_type=F32)
    ri = lax.broadcasted_iota(I32, (2 * c, 4 * c), 0)
    ci = lax.broadcasted_iota(I32, (2 * c, 4 * c), 1)
    same_chunk = (ri // c) == (ci // (2 * c))
    keep = same_chunk & ((ci % c <= ri % c) if fwd else (ci % c >= ri % c))
    a = jnp.where(keep, a, 0.0).astype(BF16)
    vbd = jnp.concatenate([vm0_s[ca, :], vm1_s[ca, :], vm0_s[cb, :], vm1_s[cb, :]], axis=0)
    o = jnp.dot(a, vbd, preferred_element_type=F32)
    o = o + lax.dot_general(qbd_s[rows, :], sprev_s[pair], (((1,), (1,)), ((), ())), preferred_element_type=F32)
    o_ref[rows, :] = o.astype(o_ref.dtype)


def _gla_kernel(qf_ref, kf_ref, vf_ref, zf_ref, qb_ref, kb_ref, vb_ref, zb_ref,
                upf_ref, upb_ref, bf_ref, bb_ref, trif_ref, trib_ref, s0_ref,
                of_ref, ob_ref, sfin_ref, sf_scr, sb_scr, *scr, nchunk):
    i = pl.program_id(2)
    nt = pl.num_programs(2)
    scr_f, scr_b = scr[:len(scr) // 2], scr[len(scr) // 2:]

    @pl.when(i == 0)
    def _():
        sf_scr[...] = s0_ref[0]
        sb_scr[...] = s0_ref[1]

    _gla_prepare(qf_ref, kf_ref, vf_ref, zf_ref, upf_ref, bf_ref, trif_ref, scr_f, True)
    _gla_prepare(qb_ref, kb_ref, vb_ref, zb_ref, upb_ref, bb_ref, trib_ref, scr_b, False)
    for pair in range(nchunk // 2):
        _gla_increment(vf_ref, scr_f, pair)
        _gla_increment(vb_ref, scr_b, pair)
    _gla_states(scr_f, sf_scr, range(nchunk))
    _gla_states(scr_b, sb_scr, range(nchunk - 1, -1, -1))
    for pair in range(nchunk // 2):
        _gla_output(of_ref, scr_f, pair, True)
        _gla_output(ob_ref, scr_b, pair, False)

    @pl.when(i == nt - 1)
    def _():
        sfin_ref[0] = sf_scr[...]
        sfin_ref[1] = sb_scr[...]


def _gla_call(qk, v, z, upf_pad, upb_pad, bias_f, bias_b, s0, batch, seq, tl):
    n = qk.shape[0]
    nt = seq // tl
    npair = GLA_HEADS // 2

    def fwd_row(b, p, i):
        return b * nt + i

    def bwd_row(b, p, i):
        return b * nt + (nt - 1 - i)

    def specs(row):
        return [
            pl.BlockSpec((tl, LANES), lambda b, p, i: (row(b, p, i), p)),
            pl.BlockSpec((tl, LANES), lambda b, p, i: (row(b, p, i), npair + p)),
            pl.BlockSpec((tl, 2 * GLA_DV), lambda b, p, i: (row(b, p, i), p)),
            pl.BlockSpec((tl, LANES), lambda b, p, i: (row(b, p, i), 0)),
        ]

    in_specs = specs(fwd_row) + specs(bwd_row) + [
        pl.BlockSpec((None, LANES, LANES), lambda b, p, i: (p, 0, 0)),
        pl.BlockSpec((None, LANES, LANES), lambda b, p, i: (p, 0, 0)),
        pl.BlockSpec((None, 1, LANES), lambda b, p, i: (p, 0, 0)),
        pl.BlockSpec((None, 1, LANES), lambda b, p, i: (p, 0, 0)),
        pl.BlockSpec((None, GLA_SUM_ROWS, GLA_SUM_ROWS), lambda b, p, i: (0, 0, 0)),
        pl.BlockSpec((None, GLA_SUM_ROWS, GLA_SUM_ROWS), lambda b, p, i: (1, 0, 0)),
        pl.BlockSpec((None, None, 2, 2 * GLA_DV, LANES), lambda b, p, i: (b, p, 0, 0, 0)),
    ]
    assert tl % GLA_SUM_ROWS == 0
    ri = jnp.arange(GLA_SUM_ROWS)[:, None]
    ci = jnp.arange(GLA_SUM_ROWS)[None, :]
    same = (ri // GLA_CHUNK) == (ci // GLA_CHUNK)
    tri = jnp.stack([same & (ci <= ri), same & (ci >= ri)]).astype(BF16)
    npairs = tl // (2 * GLA_CHUNK)
    dir_scratch = [
        pltpu.VMEM((tl, LANES), BF16),
        pltpu.VMEM((tl, LANES), BF16),
        pltpu.VMEM((tl, LANES), BF16),
        pltpu.VMEM((tl, 2 * LANES), BF16),
        pltpu.VMEM((tl, 2 * GLA_DV), BF16),
        pltpu.VMEM((tl, 2 * GLA_DV), BF16),
        pltpu.VMEM((tl, LANES), F32),
        pltpu.VMEM((npairs, 2 * GLA_DV, 2 * LANES), F32),
        pltpu.VMEM((npairs, 2 * GLA_DV, 2 * LANES), BF16),
        pltpu.VMEM((tl, 2 * LANES), BF16),
    ]
    out_specs = [
        pl.BlockSpec((tl, 2 * GLA_DV), lambda b, p, i: (fwd_row(b, p, i), p)),
        pl.BlockSpec((tl, 2 * GLA_DV), lambda b, p, i: (bwd_row(b, p, i), p)),
        pl.BlockSpec((None, None, 2, 2 * GLA_DV, LANES), lambda b, p, i: (b, p, 0, 0, 0)),
    ]
    out_shape = [
        jax.ShapeDtypeStruct((n, GLA_WIDTH), BF16),
        jax.ShapeDtypeStruct((n, GLA_WIDTH), BF16),
        jax.ShapeDtypeStruct((batch, npair, 2, 2 * GLA_DV, LANES), F32),
    ]
    return pl.pallas_call(
        functools.partial(_gla_kernel, nchunk=tl // GLA_CHUNK),
        grid=(batch, npair, nt),
        in_specs=in_specs,
        out_specs=out_specs,
        out_shape=out_shape,
        scratch_shapes=[pltpu.VMEM((2 * GLA_DV, LANES), F32), pltpu.VMEM((2 * GLA_DV, LANES), F32)]
        + dir_scratch + dir_scratch,
        compiler_params=_cparams(("arbitrary", "arbitrary", "arbitrary")),
        name="gla",
    )(qk, qk, v, z, qk, qk, v, z, upf_pad, upb_pad, bias_f, bias_b, tri, tri, s0)


def _swa_block(sink_ref, q_ref, o_ref, row0, kall, vall, prev_ok, next_ok):
    w = ATT_BLOCK
    local = prev_ok is not None
    qrows = pl.ds(row0, w)
    r2 = lax.broadcasted_iota(I32, (2 * w, w), 0) % w
    c2 = lax.broadcasted_iota(I32, (2 * w, w), 1)
    if local:
        bias_prev = jnp.where((c2 >= r2) & prev_ok, 0.0, NEG_BIG)
        bias_next = jnp.where((c2 <= r2) & next_ok, 0.0, NEG_BIG)
    first = c2 < SWA_HD
    top = lax.broadcasted_iota(I32, (2 * w, 1), 0) < w
    scores = []
    for g in range(SWA_KV_HEADS):
        kd = kall[:, g * LANES:(g + 1) * LANES]
        qs = jnp.concatenate([q_ref[qrows, (2 * g) * LANES:(2 * g + 1) * LANES],
                              q_ref[qrows, (2 * g + 1) * LANES:(2 * g + 2) * LANES]], axis=0).astype(F32)
        for half in range(2):
            qm = (jnp.where(first, qs, 0.0) if half == 0 else jnp.where(first, 0.0, qs)).astype(BF16)
            s = lax.dot_general(qm, kd, (((1,), (1,)), ((), ())), preferred_element_type=F32)
            if local:
                s = jnp.concatenate([s[:, :w] + bias_prev, s[:, w:2 * w], s[:, 2 * w:3 * w] + bias_next,
                                     s[:, 3 * w:]], axis=1)
            scores.append(s)
    for g in range(SWA_KV_HEADS):
        outs = []
        for half in range(2):
            s = scores[2 * g + half]
            sk = jnp.where(top, sink_ref[4 * g + half], sink_ref[4 * g + 2 + half])
            m = jnp.maximum(jnp.max(s, axis=-1, keepdims=True), sk)
            p = jnp.exp((s - m).astype(BF16))
            va = vall[:, (2 * g + half) * LANES:(2 * g + half + 1) * LANES]
            acc = jnp.dot(p, va, preferred_element_type=F32)
            den = pltpu.roll(acc, SWA_HD, 1) + jnp.exp(sk - m)
            outs.append(acc / den)
        out = jnp.where(first, outs[0], outs[1])
        o_ref[qrows, (2 * g) * LANES:(2 * g + 1) * LANES] = out[:w].astype(o_ref.dtype)
        o_ref[qrows, (2 * g + 1) * LANES:(2 * g + 2) * LANES] = out[w:].astype(o_ref.dtype)


def _swa_kernel(*refs, nstep, local):
    w = ATT_BLOCK
    if not local:
        sink_ref, q_ref, kx_ref, vx_ref, o_ref = refs
        _swa_block(sink_ref, q_ref, o_ref, 0, kx_ref[...], vx_ref[...], None, None)
        return
    sink_ref, q_ref, kp_ref, kc_ref, kn_ref, vp_ref, vc_ref, vn_ref, kx_ref, vx_ref, o_ref = refs
    i = pl.program_id(1)
    nblk = q_ref.shape[0] // w
    k_blocks = [kp_ref[...]] + [kc_ref[j * w:(j + 1) * w, :] for j in range(nblk)] + [kn_ref[...]]
    v_blocks = [vp_ref[...]] + [vc_ref[j * w:(j + 1) * w, :] for j in range(nblk)] + [vn_ref[...]]
    exists = [i > 0] + [True] * nblk + [i < nstep - 1]
    for sub in range(nblk):
        kall = jnp.concatenate(k_blocks[sub:sub + 3] + [kx_ref[...]], axis=0)
        vall = jnp.concatenate(v_blocks[sub:sub + 3] + [vx_ref[...]], axis=0)
        _swa_block(sink_ref, q_ref, o_ref, sub * w, kall, vall, exists[sub], exists[sub + 2])


def _swa_call(sq, skd, svd, kcd, vcd, sink, batch, seq, lc):
    n = sq.shape[0]
    w = ATT_BLOCK
    nb = seq // w
    nblk = SWA_BLOCKS_PER_STEP if nb % SWA_BLOCKS_PER_STEP == 0 else 1
    nstep = nb // nblk
    kvw = 2 * SWA_KVW

    def pair(b, i):
        return (b * nstep + i, 0)

    def before(b, i):
        return (b * nb + jnp.maximum(nblk * i - 1, 0), 0)

    def after(b, i):
        return (b * nb + jnp.minimum(nblk * (i + 1), nb - 1), 0)

    def kv_specs(width):
        return [pl.BlockSpec((w, width), before), pl.BlockSpec((nblk * w, width), pair),
                pl.BlockSpec((w, width), after)]

    def ctx_spec(width):
        return pl.BlockSpec((lc, width), lambda b, i: (b, 0))

    return pl.pallas_call(
        functools.partial(_swa_kernel, nstep=nstep, local=True),
        grid=(batch, nstep),
        in_specs=[pl.BlockSpec(memory_space=pltpu.SMEM), pl.BlockSpec((nblk * w, SWA_WIDTH), pair)]
        + kv_specs(kvw) + kv_specs(SV_LANES) + [ctx_spec(kvw), ctx_spec(SV_LANES)],
        out_specs=pl.BlockSpec((nblk * w, SWA_WIDTH), pair),
        out_shape=jax.ShapeDtypeStruct((n, SWA_WIDTH), BF16),
        compiler_params=_cparams(("arbitrary", "arbitrary")),
        name="swa",
    )(sink, sq, skd, skd, skd, svd, svd, svd, kcd, vcd)


def _swa_ctx_call(sq, kcd, vcd, sink, batch, lc):
    n = sq.shape[0]
    w = ATT_BLOCK
    nb = lc // w
    kvw = 2 * SWA_KVW
    return pl.pallas_call(
        functools.partial(_swa_kernel, nstep=0, local=False),
        grid=(batch, nb),
        in_specs=[pl.BlockSpec(memory_space=pltpu.SMEM),
                  pl.BlockSpec((w, SWA_WIDTH), lambda b, i: (b * nb + i, 0)),
                  pl.BlockSpec((lc, kvw), lambda b, i: (b, 0)),
                  pl.BlockSpec((lc, SV_LANES), lambda b, i: (b, 0))],
        out_specs=pl.BlockSpec((w, SWA_WIDTH), lambda b, i: (b * nb + i, 0)),
        out_shape=jax.ShapeDtypeStruct((n, SWA_WIDTH), BF16),
        compiler_params=_cparams(("arbitrary", "arbitrary")),
        name="swa_ctx",
    )(sink, sq, kcd, vcd)


def _outproj_kernel(of_ref, ob_ref, gate_ref, gn_ref, swa_ref, w_ref, x_ref, g1_ref, n2_ref, sc_ref, sh_ref,
                    wr_ref, xo_ref, h2_ref, lg_ref):
    o = of_ref[...].astype(F32) + ob_ref[...].astype(F32)
    parts = []
    for h in range(GLA_HEADS):
        oh = o[:, h * GLA_DV:(h + 1) * GLA_DV]
        ms = jnp.mean(oh * oh, axis=-1, keepdims=True)
        parts.append(oh * lax.rsqrt(ms + NORM_EPS))
    on = jnp.concatenate(parts, axis=-1) * gn_ref[...]
    gate = gate_ref[...].astype(F32)
    gla = on * (gate * jax.nn.sigmoid(gate))
    mix = jnp.concatenate([gla.astype(BF16), swa_ref[...]], axis=-1)
    y = jnp.dot(mix, w_ref[...], preferred_element_type=F32)
    xo = x_ref[...] + g1_ref[...] * y
    xo_ref[...] = xo
    ms = jnp.mean(xo * xo, axis=-1, keepdims=True)
    h2 = (xo * lax.rsqrt(ms + NORM_EPS)) * (n2_ref[...] * (1.0 + sc_ref[...])) + sh_ref[...]
    hi = h2.astype(BF16)
    hi_f = hi.astype(F32)
    h2_ref[...] = _pack_rounded_pairs(hi_f)
    lo = (h2 - hi_f).astype(BF16)
    both = jnp.dot(hi, wr_ref[...], preferred_element_type=F32)
    lg = both[:, :LANES] + both[:, LANES:] + jnp.dot(lo, wr_ref[:, :LANES], preferred_element_type=F32)
    lg_ref[...] = jnp.transpose(lg)[:N_EXPERTS, :]


def _outproj_call(o_f, o_b, gate, gn, swa, w_out_b, x2d, mods5, layer, row_of_tile, n2, wr_cat, tm):
    n, d = x2d.shape
    nt = n // tm

    def mod_spec(k):
        return pl.BlockSpec((None, None, None, 1, d), lambda t: (layer, row_of_tile(t), k, 0, 0))

    return pl.pallas_call(
        _outproj_kernel,
        grid=(nt,),
        in_specs=[
            pl.BlockSpec((tm, GLA_WIDTH), lambda t: (t, 0)),
            pl.BlockSpec((tm, GLA_WIDTH), lambda t: (t, 0)),
            pl.BlockSpec((tm, GLA_WIDTH), lambda t: (t, 0)),
            pl.BlockSpec((1, GLA_WIDTH), lambda t: (0, 0)),
            pl.BlockSpec((tm, SWA_WIDTH), lambda t: (t, 0)),
            pl.BlockSpec((None, d, d), lambda t: (layer, 0, 0)),
            pl.BlockSpec((tm, d), lambda t: (t, 0)),
            mod_spec(2),
            pl.BlockSpec((1, d), lambda t: (0, 0)),
            mod_spec(4), mod_spec(3),
            pl.BlockSpec((d, 2 * LANES), lambda t: (0, 0)),
        ],
        out_specs=[
            pl.BlockSpec((tm, d), lambda t: (t, 0)),
            pl.BlockSpec((tm, d // 2), lambda t: (t, 0)),
            pl.BlockSpec((N_EXPERTS, tm), lambda t: (0, t)),
        ],
        out_shape=[
            jax.ShapeDtypeStruct((n, d), F32),
            jax.ShapeDtypeStruct((n, d // 2), U32),
            jax.ShapeDtypeStruct((N_EXPERTS, n), F32),
        ],
        compiler_params=_cparams(("arbitrary",)),
        name="outproj",
    )(o_f, o_b, gate, gn.reshape(1, GLA_WIDTH), swa, w_out_b, x2d, mods5, n2.reshape(1, d), mods5, mods5, wr_cat)


def _first_index(vals, target):
    idx = jnp.full(target.shape, len(vals) - 1, I32)
    for i in range(len(vals) - 2, -1, -1):
        idx = jnp.where(vals[i] == target, i, idx)
    return idx


def _route_kernel(lg_ref, br_ref, io_ref, wo_ref, seg_ref, pad_ref, blk_ref, cnt_scr, seg_scr, pad_scr, *, tn):
    phase = pl.program_id(0)
    step = pl.program_id(1)
    group = lg_ref.shape[1] // tn

    @pl.when((phase == 0) & (step == 0))
    def _():
        cnt_scr[...] = jnp.zeros_like(cnt_scr)
        seg_scr[...] = jnp.zeros_like(seg_scr)
        pad_scr[...] = jnp.zeros_like(pad_scr)

    @pl.when((phase == 1) & (step == 0))
    def _():
        seg_len = jnp.floor((cnt_scr[...] + (SUBLANES - 1)) * (1.0 / SUBLANES)) * SUBLANES
        rows_e = jnp.sum(seg_len, axis=1, keepdims=True)
        blocks = jnp.floor((rows_e + (MOE_BLOCK - 1)) * (1.0 / MOE_BLOCK)) * MOE_BLOCK
        r128 = lax.broadcasted_iota(I32, (LANES, LANES), 0)
        c128 = lax.broadcasted_iota(I32, (LANES, LANES), 1)
        before_tile = jnp.where(r128 < c128, 1.0, 0.0).astype(BF16)
        seg = jnp.dot((seg_len * (1.0 / SUBLANES)).astype(BF16), before_tile,
                      preferred_element_type=F32) * SUBLANES
        run = jnp.zeros((1, 1), F32)
        for e in range(N_EXPERTS):
            seg_scr[e:e + 1, :] = seg[e:e + 1, :] + run
            run = run + blocks[e:e + 1, :]
        pad_scr[...] = seg_len
        seg_ref[...] = seg_scr[...].astype(I32)
        pad_ref[...] = seg_len.astype(I32)
        blk_ref[...] = jnp.broadcast_to(blocks, blk_ref.shape).astype(I32)

    for g in range(group):
        lanes = slice(g * tn, (g + 1) * tn)
        _route_tile(lg_ref[:, lanes], br_ref, io_ref, wo_ref, lanes, step * group + g, phase, cnt_scr, pad_scr)


def _route_tile(lg, br_ref, io_ref, wo_ref, lanes, t, phase, cnt_scr, pad_scr):
    tn = lg.shape[1]
    tile_lane = lax.broadcasted_iota(I32, (N_EXPERTS, LANES), 1)
    s = jax.nn.sigmoid(lg)
    sb = s + br_ref[...]
    rows_s = [s[e:e + 1, :] for e in range(N_EXPERTS)]
    rows_b = [sb[e:e + 1, :] for e in range(N_EXPERTS)]
    gscore, gi1, gi2 = [], [], []
    epg = EXPERTS_PER_GROUP
    for g in range(N_GROUPS):
        a = rows_b[g * epg:(g + 1) * epg]
        m1 = functools.reduce(jnp.maximum, a)
        i1 = _first_index(a, m1)
        rest = [jnp.where(i1 == i, -jnp.inf, a[i]) for i in range(epg)]
        m2 = functools.reduce(jnp.maximum, rest)
        i2 = _first_index(rest, m2)
        gscore.append(m1 + m2)
        gi1.append(i1)
        gi2.append(i2)
    gm = functools.reduce(jnp.maximum, gscore)
    gsel = _first_index(gscore, gm)
    i1 = gi1[N_GROUPS - 1]
    i2 = gi2[N_GROUPS - 1]
    for g in range(N_GROUPS - 2, -1, -1):
        i1 = jnp.where(gsel == g, gi1[g], i1)
        i2 = jnp.where(gsel == g, gi2[g], i2)
    idx0 = gsel * epg + i1
    idx1 = gsel * epg + i2
    s0 = jnp.zeros_like(rows_s[0])
    s1 = jnp.zeros_like(rows_s[0])
    for e in range(N_EXPERTS):
        s0 = jnp.where(idx0 == e, rows_s[e], s0)
        s1 = jnp.where(idx1 == e, rows_s[e], s1)
    tot = s0 + s1
    w0 = s0 / tot
    w1 = s1 / tot

    eidx = lax.broadcasted_iota(I32, (N_EXPERTS, tn), 0)
    oh0 = eidx == idx0
    oh1 = eidx == idx1
    oh = jnp.where(oh0 | oh1, 1.0, 0.0)

    @pl.when(phase == 0)
    def _():
        cnt_scr[...] = cnt_scr[...] + jnp.where(tile_lane == t, jnp.sum(oh, axis=1, keepdims=True), 0.0)

    @pl.when(phase == 1)
    def _():
        rr = lax.broadcasted_iota(I32, (tn, tn), 0)
        cc = lax.broadcasted_iota(I32, (tn, tn), 1)
        upper = jnp.where(rr < cc, 1.0, 0.0).astype(BF16)
        before = jnp.dot(oh.astype(BF16), upper, preferred_element_type=F32)
        pad_col = jnp.sum(jnp.where(tile_lane == t, pad_scr[...], 0.0), axis=1, keepdims=True)
        run = jnp.zeros((1, 1), F32)
        offs = []
        for e in range(N_EXPERTS):
            offs.append(run)
            run = run + pad_col[e:e + 1, :]
        pos = before + jnp.concatenate(offs, axis=0)
        spos0 = jnp.sum(jnp.where(oh0, pos, 0.0), axis=0, keepdims=True)
        spos1 = jnp.sum(jnp.where(oh1, pos, 0.0), axis=0, keepdims=True)
        zi = jnp.zeros((SUBLANES - 4, tn), I32)
        io_ref[:, lanes] = jnp.concatenate([idx0, idx1, spos0.astype(I32), spos1.astype(I32), zi], axis=0)
        wo_ref[:, lanes] = jnp.concatenate([w0, w1, jnp.zeros((SUBLANES - 2, tn), F32)], axis=0)


def _route_call(logits_t, b_router, tn):
    ne, t_all = logits_t.shape
    ntile = t_all // tn
    assert ntile <= LANES
    group = max(g for g in range(1, ROUTE_TILES_PER_STEP + 1) if ntile % g == 0)
    tw = group * tn
    tok_out = pl.BlockSpec((SUBLANES, tw), lambda p, t: (0, t * p))
    tab_out = pl.BlockSpec((ne, LANES), lambda p, t: (0, 0))
    tab = jax.ShapeDtypeStruct((ne, LANES), I32)
    return pl.pallas_call(
        functools.partial(_route_kernel, tn=tn),
        grid=(2, ntile // group),
        in_specs=[pl.BlockSpec((ne, tw), lambda p, t: (0, t)), pl.BlockSpec((ne, 1), lambda p, t: (0, 0))],
        out_specs=[tok_out, tok_out, tab_out, tab_out, tab_out],
        out_shape=[
            jax.ShapeDtypeStruct((SUBLANES, t_all), I32),
            jax.ShapeDtypeStruct((SUBLANES, t_all), F32),
            tab, tab, tab,
        ],
        scratch_shapes=[pltpu.VMEM((ne, LANES), F32)] * 3,
        compiler_params=_cparams(("arbitrary", "arbitrary")),
        name="route",
    )(logits_t, b_router.reshape(ne, 1))


def _segment_copies(seg_ref, pad_ref, t, stage, hbm, sem, to_hbm, wait):
    if wait:
        total = functools.reduce(lambda a, b: a + b, [pad_ref[e * LANES + t] for e in range(N_EXPERTS)])
        for size in (2 * SEG_SIZES[0],) + SEG_SIZES:
            @pl.when((total & size) != 0)
            def _(size=size):
                src, dst = stage.at[pl.ds(0, size), :], hbm.at[pl.ds(0, size), :]
                if not to_hbm:
                    src, dst = dst, src
                pltpu.make_async_copy(src, dst, sem).wait()
        return
    loc = 0
    for e in range(N_EXPERTS):
        n = pad_ref[e * LANES + t]
        start = seg_ref[e * LANES + t]
        for size in SEG_SIZES:
            off = n & (-2 * size)

            @pl.when((n & size) != 0)
            def _(off=off, size=size, loc=loc, start=start):
                s_rows = pl.ds(pl.multiple_of(loc + off, SUBLANES), size)
                h_rows = pl.ds(pl.multiple_of(start + off, SUBLANES), size)
                src, dst = (stage.at[s_rows, :], hbm.at[h_rows, :])
                if not to_hbm:
                    src, dst = dst, src
                pltpu.make_async_copy(src, dst, sem).start()
        loc = loc + n


def _sort_matrix(io_ref, rows):
    r = lax.broadcasted_iota(I32, (rows, io_ref.shape[1]), 0)
    return r == io_ref[2:3, :], r == io_ref[3:4, :]


def _dispatch_kernel(seg_ref, pad_ref, zblk_ref, *refs, tiles):
    srcs = refs[:len(tiles)]
    io_ref, wo_ref, xb_hbm, stage, zbuf, sem = refs[len(tiles):]
    i = pl.program_id(0)
    m = MOE_BLOCK

    def zero_copy(j):
        start = pl.multiple_of(zblk_ref[j] * m, m)
        return pltpu.make_async_copy(zbuf, xb_hbm.at[pl.ds(start, m), :], sem.at[0])

    @pl.when(i == 0)
    def _():
        zbuf[...] = jnp.zeros_like(zbuf)
        for j in range(zblk_ref.shape[0]):
            @pl.when(zblk_ref[j] >= 0)
            def _():
                zero_copy(j).start()
        for j in range(zblk_ref.shape[0]):
            @pl.when(zblk_ref[j] >= 0)
            def _():
                zero_copy(j).wait()

    hp = srcs[0][...]
    if len(srcs) == 2:
        hp = jnp.where(i < tiles[0], hp, srcs[1][...])
    x = _unpack_bf16_pairs(hp)
    m0, m1 = _sort_matrix(io_ref, STAGE_ROWS)
    sort = jnp.where(m0 | m1, 1.0, 0.0).astype(BF16)
    xs = jnp.dot(sort, x, preferred_element_type=F32)
    dh = x.shape[1] // 2
    slot = i % 2
    cur = stage.at[slot]
    cur[:, :dh] = _pack_rounded_pairs(xs)
    ws = jnp.sum(jnp.where(m0, wo_ref[0:1, :], 0.0) + jnp.where(m1, wo_ref[1:2, :], 0.0), axis=1, keepdims=True)
    cur[:, dh:] = jnp.broadcast_to(lax.bitcast_convert_type(ws, U32), (STAGE_ROWS, LANES))
    _segment_copies(seg_ref, pad_ref, i, cur, xb_hbm, sem.at[slot], to_hbm=True, wait=False)

    @pl.when(i > 0)
    def _():
        _segment_copies(seg_ref, pad_ref, i - 1, stage.at[1 - slot], xb_hbm, sem.at[1 - slot], to_hbm=True, wait=True)

    @pl.when(i == pl.num_programs(0) - 1)
    def _():
        _segment_copies(seg_ref, pad_ref, i, cur, xb_hbm, sem.at[slot], to_hbm=True, wait=True)


def _dispatch_call(seg, pad, zblk, io, wo, sources, p_rows):
    tile = ROUTE_TILE
    dh = sources[0].shape[1]
    tiles = tuple(s.shape[0] // tile for s in sources)
    firsts = tuple(sum(tiles[:k]) for k in range(len(tiles)))

    def src_spec(first, ntile):
        return pl.BlockSpec((tile, dh), lambda i, *_: (jnp.clip(i - first, 0, ntile - 1), 0))

    tok_spec = pl.BlockSpec((SUBLANES, tile), lambda i, *_: (0, i))
    grid_spec = pltpu.PrefetchScalarGridSpec(
        num_scalar_prefetch=3,
        grid=(sum(tiles),),
        in_specs=[src_spec(f, n) for f, n in zip(firsts, tiles)] + [tok_spec, tok_spec],
        out_specs=pl.BlockSpec(memory_space=pl.ANY),
        scratch_shapes=[pltpu.VMEM((2, STAGE_ROWS, dh + LANES), U32), pltpu.VMEM((MOE_BLOCK, dh + LANES), U32),
                        pltpu.SemaphoreType.DMA((2,))],
    )
    return pl.pallas_call(
        functools.partial(_dispatch_kernel, tiles=tiles),
        grid_spec=grid_spec,
        out_shape=jax.ShapeDtypeStruct((p_rows, dh + LANES), U32),
        compiler_params=_cparams(("arbitrary",)),
        name="dispatch",
    )(seg, pad, zblk, *sources, io, wo)


def _expert_kernel(be_ref, nu_ref, va_ref, ne_ref, x_ref, wg_hbm, wu_hbm, wd_hbm, o_ref,
                   wg_b, wu_b, wd_b, wg_f, wu_f, wd_f, switch_ref, sem, *, layer):
    j = pl.program_id(0)
    dh = o_ref.shape[1]
    half = o_ref.shape[0] // 2
    valid = va_ref[j]
    e = be_ref[j]

    def weight_copies(expert, slot):
        return [pltpu.make_async_copy(hbm.at[layer, expert], buf.at[slot], sem.at[slot, k])
                for k, (hbm, buf) in enumerate(((wg_hbm, wg_f), (wu_hbm, wu_f), (wd_hbm, wd_f)))]

    @pl.when(j == 0)
    def _():
        switch_ref[0] = 0
        for cp in weight_copies(e, 0):
            cp.start()

    @pl.when((j == 0) | (e != be_ref[jnp.maximum(j - 1, 0)]))
    def _():
        slot = switch_ref[0] % 2
        for cp in weight_copies(e, slot):
            cp.wait()
        wg_b[...] = wg_f[slot].astype(BF16)
        wu_b[...] = wu_f[slot].astype(BF16)
        wd_b[...] = wd_f[slot].astype(BF16)
        nxt = ne_ref[e]

        @pl.when(nxt >= 0)
        def _():
            for cp in weight_copies(nxt, 1 - slot):
                cp.start()
        switch_ref[0] = switch_ref[0] + 1

    def ffn_halves(halves):
        gates = []
        for r in halves:
            x = _unpack_bf16_pairs(x_ref[r * half:(r + 1) * half, :dh])
            gates.append((jnp.dot(x, wg_b[...], preferred_element_type=F32),
                          jnp.dot(x, wu_b[...], preferred_element_type=F32)))
        for r, (a, u) in zip(halves, gates):
            hmid = (a * jax.nn.sigmoid(a)) * u
            y = jnp.dot(hmid.astype(BF16), wd_b[...], preferred_element_type=F32)
            row_w = lax.bitcast_convert_type(x_ref[r * half:(r + 1) * half, dh:dh + 1], F32)
            o_ref[r * half:(r + 1) * half, :] = _pack_bf16_pairs(y * row_w)

    @pl.when(valid > half)
    def _():
        ffn_halves((0, 1))

    @pl.when((valid > 0) & (valid <= half))
    def _():
        ffn_halves((0,))
        o_ref[half:, :] = jnp.zeros((half, dh), o_ref.dtype)

    @pl.when(valid == 0)
    def _():
        o_ref[...] = jnp.zeros_like(o_ref)


def _expert_call(block_e, nused, valid, next_e, xb, wg, wu, wd, layer):
    p_rows = xb.shape[0]
    dh = xb.shape[1] - LANES
    d = 2 * dh
    m = MOE_BLOCK
    nb = p_rows // m
    de = wg.shape[-1]

    def xmap(j, be, nu, va, ne):
        return (jnp.minimum(j, nu[0] - 1), 0)

    any_spec = pl.BlockSpec(memory_space=pl.ANY)
    grid_spec = pltpu.PrefetchScalarGridSpec(
        num_scalar_prefetch=4,
        grid=(nb,),
        in_specs=[pl.BlockSpec((m, dh + LANES), xmap), any_spec, any_spec, any_spec],
        out_specs=pl.BlockSpec((m, dh), lambda j, be, nu, va, ne: (j, 0)),
        scratch_shapes=[
            pltpu.VMEM((d, de), BF16), pltpu.VMEM((d, de), BF16), pltpu.VMEM((de, d), BF16),
            pltpu.VMEM((2, d, de), F32), pltpu.VMEM((2, d, de), F32), pltpu.VMEM((2, de, d), F32),
            pltpu.SMEM((1,), I32), pltpu.SemaphoreType.DMA((2, 3)),
        ],
    )
    return pl.pallas_call(
        functools.partial(_expert_kernel, layer=layer),
        grid_spec=grid_spec,
        out_shape=jax.ShapeDtypeStruct((p_rows, dh), U32),
        compiler_params=_cparams(("arbitrary",)),
        name="experts",
    )(block_e, nused, valid, next_e, xb, wg, wu, wd)


def _combine_kernel(*refs, tile0, final):
    if final:
        seg_ref, pad_ref, yb_hbm, io_ref, x_ref, g2_ref, fn_ref, o_ref, stage, sem = refs
    else:
        seg_ref, pad_ref, yb_hbm, io_ref, x_ref, g2_ref, o_ref, stage, sem = refs
    i = pl.program_id(0)

    slot = i % 2

    @pl.when(i == 0)
    def _():
        stage[...] = jnp.zeros_like(stage)
        _segment_copies(seg_ref, pad_ref, tile0, stage.at[0], yb_hbm, sem.at[0], to_hbm=False, wait=False)

    @pl.when(i + 1 < pl.num_programs(0))
    def _():
        _segment_copies(seg_ref, pad_ref, tile0 + i + 1, stage.at[1 - slot], yb_hbm, sem.at[1 - slot],
                        to_hbm=False, wait=False)

    _segment_copies(seg_ref, pad_ref, tile0 + i, stage.at[slot], yb_hbm, sem.at[slot], to_hbm=False, wait=True)
    rows = _unpack_bf16_pairs(stage[slot])
    m0, m1 = _sort_matrix(io_ref, STAGE_ROWS)
    pick = jnp.where(m0 | m1, 1.0, 0.0).astype(BF16)
    y = lax.dot_general(pick, rows, (((0,), (0,)), ((), ())), preferred_element_type=F32)
    xo = x_ref[...] + g2_ref[...] * y
    if final:
        ms = jnp.mean(xo * xo, axis=-1, keepdims=True)
        xo = (xo * lax.rsqrt(ms + NORM_EPS)) * fn_ref[...]
    o_ref[...] = xo


def _combine_call(seg, pad, yb, io, x2d, mods5, layer, row_of_tile, final_g, tile0):
    n, d = x2d.shape
    tm = ROUTE_TILE
    final = final_g is not None
    in_specs = [
        pl.BlockSpec(memory_space=pl.ANY),
        pl.BlockSpec((SUBLANES, tm), lambda t, *_: (0, tile0 + t)),
        pl.BlockSpec((tm, d), lambda t, *_: (t, 0)),
        pl.BlockSpec((None, None, None, 1, d), lambda t, *_: (layer, row_of_tile(t), 5, 0, 0)),
    ]
    args = [yb, io, x2d, mods5]
    if final:
        in_specs.append(pl.BlockSpec((1, d), lambda t, *_: (0, 0)))
        args.append(final_g.reshape(1, d))
    grid_spec = pltpu.PrefetchScalarGridSpec(
        num_scalar_prefetch=2,
        grid=(n // tm,),
        in_specs=in_specs,
        out_specs=pl.BlockSpec((tm, d), lambda t, *_: (t, 0)),
        scratch_shapes=[pltpu.VMEM((2, STAGE_ROWS, d // 2), U32), pltpu.SemaphoreType.DMA((2,))],
    )
    return pl.pallas_call(
        functools.partial(_combine_kernel, tile0=tile0, final=final),
        grid_spec=grid_spec,
        out_shape=jax.ShapeDtypeStruct((n, d), F32),
        compiler_params=_cparams(("arbitrary",)),
        name="combine",
    )(seg, pad, *args)


def _pack_w_kernel(w_ref, o_ref):
    cut = C_Z + 2 * GLA_GATE_RANK
    o_ref[:, :C_Z] = w_ref[:, :C_Z].astype(BF16)
    gate_tile = w_ref[:, C_Z:C_Z + LANES]
    lane = lax.broadcasted_iota(I32, gate_tile.shape, 1)
    o_ref[:, C_Z:C_SQ] = jnp.where(lane < 2 * GLA_GATE_RANK, gate_tile, 0.0).astype(BF16)
    o_ref[:, C_SQ:] = w_ref[:, cut:].astype(BF16)


def _pack_w_in(w):
    assert sum(IN_SIZES[:6]) == C_Z + 2 * GLA_GATE_RANK and sum(IN_SIZES) - sum(IN_SIZES[:6]) == C_END - C_SQ
    depth, d, cols = w.shape
    rows = ROW_TILE // 2
    return pl.pallas_call(
        _pack_w_kernel,
        grid=(depth, d // rows),
        in_specs=[pl.BlockSpec((None, rows, cols), lambda l, r: (l, r, 0))],
        out_specs=pl.BlockSpec((None, rows, C_END), lambda l, r: (l, r, 0)),
        out_shape=jax.ShapeDtypeStruct((depth, d, C_END), BF16),
        compiler_params=_cparams(("arbitrary", "arbitrary")),
        name="pack_w_in",
    )(w)


def _pad_up(up, row0):
    up = up.reshape(GLA_GATE_RANK, GLA_HEADS // 2, LANES).transpose(1, 0, 2)
    hi = up.astype(BF16)
    lo = (up - hi.astype(F32)).astype(BF16)
    out = jnp.zeros((GLA_HEADS // 2, LANES, LANES), BF16)
    for group, part in enumerate((hi, hi, lo)):
        r = group * 2 * GLA_GATE_RANK + row0
        out = out.at[:, r:r + GLA_GATE_RANK, :].set(part)
    return out


def _rope_tables(seq):
    rows = seq // GRID_W
    dim = jnp.arange(LANES, dtype=I32) % SWA_HD
    inv = ROPE_THETA ** (-((dim % ROPE_HALF).astype(F32) * 2.0 / ROPE_AXIS_DIM))
    sign = jnp.where(dim % ROPE_AXIS_DIM < ROPE_HALF, -1.0, 1.0)
    by_row = (dim < ROPE_AXIS_DIM)[None, None, :]
    ang_r = (jnp.arange(rows, dtype=F32)[:, None] * inv[None, :])[:, None, :]
    ang_c = (jnp.arange(GRID_W, dtype=F32)[:, None] * inv[None, :])[None, :, :]
    cos = jnp.where(by_row, jnp.cos(ang_r), jnp.cos(ang_c))
    sin = jnp.where(by_row, jnp.sin(ang_r), jnp.sin(ang_c)) * sign
    return cos.reshape(seq, LANES), sin.reshape(seq, LANES)


def _tile(n, pref):
    t = pref
    while n % t:
        t //= 2
    return t


def kernel(x, c, ctx, c_ctx, w_ada, b_ada, norm1, norm2, w_in, gla_up_f, gla_bias_f, gla_up_b, gla_bias_b,
           gla_norm, swa_sink, w_out, w_router, b_router, w_gate, w_up, w_down, final_norm):
    batch, seq, d = x.shape
    lc = ctx.shape[1]
    depth = w_ada.shape[0]
    n_l = batch * seq
    n_c = batch * lc
    npair = GLA_HEADS // 2
    assert batch + 1 <= SUBLANES
    assert n_l % ROUTE_TILE == 0 and n_c % ROUTE_TILE == 0

    tm_l = _tile(seq, ROW_TILE)
    tm_in = _tile(seq, INPROJ_TILE)
    tm_c = _tile(lc, CTX_TILE)
    tl_l = _tile(seq, GLA_TILE)
    tl_c = _tile(lc, ROW_TILE)

    rows = jnp.zeros((SUBLANES, d), F32).at[:batch].set(c).at[batch].set(c_ctx)
    mods = _ada_call(rows, w_ada, b_ada)
    mods5 = mods.reshape(depth, SUBLANES, 6, 1, d)

    tabs = _rope_tables(seq)
    w_packed = _pack_w_in(w_in)
    w_out_b = w_out.astype(BF16)
    wr_pad = jnp.zeros((d, LANES), F32).at[:, :N_EXPERTS].set(w_router)
    wr_hi = wr_pad.astype(BF16)
    wr_lo = (wr_pad - wr_hi.astype(F32)).astype(BF16)
    wr_cat = jnp.concatenate([wr_hi, wr_lo], axis=-1)

    def lat_row(tm):
        return lambda t: t // (seq // tm)

    def ctx_row(tm):
        return lambda t: batch

    xl = x.reshape(n_l, d)
    xc = ctx.reshape(n_c, d)
    for i in range(depth):
        last = i == depth - 1
        upf = _pad_up(gla_up_f[i], 0)
        upb = _pad_up(gla_up_b[i], GLA_GATE_RANK)
        bias_f = gla_bias_f[i].reshape(npair, 1, LANES)
        bias_b = gla_bias_b[i].reshape(npair, 1, LANES)

        c_qk, c_v, c_g, c_z, c_sq, c_sk, c_sv = _inproj_call(
            xc, norm1[i], mods5, i, ctx_row(tm_c), w_packed, None, tm_c, lc // tm_c)
        l_qk, l_v, l_g, l_z, l_sq, l_sk, l_sv = _inproj_call(
            xl, norm1[i], mods5, i, lat_row(tm_in), w_packed, tabs, tm_in, seq // tm_in)

        s_zero = jnp.zeros((batch, npair, 2, 2 * GLA_DV, LANES), F32)
        oc_f, oc_b, s_ctx = _gla_call(c_qk, c_v, c_z, upf, upb, bias_f, bias_b, s_zero, batch, lc, tl_c)
        ol_f, ol_b, _ = _gla_call(l_qk, l_v, l_z, upf, upb, bias_f, bias_b, s_ctx, batch, seq, tl_l)

        swa_l = _swa_call(l_sq, l_sk, l_sv, c_sk, c_sv, swa_sink[i], batch, seq, lc)
        xl, h2l, lg_l = _outproj_call(ol_f, ol_b, l_g, gla_norm[i], swa_l, w_out_b, xl, mods5, i, lat_row(tm_l),
                                      norm2[i], wr_cat, tm_l)
        if last:
            logits_t = lg_l
            t_all = n_l
        else:
            swa_c = _swa_ctx_call(c_sq, c_sk, c_sv, swa_sink[i], batch, lc)
            xc, h2c, lg_c = _outproj_call(oc_f, oc_b, c_g, gla_norm[i], swa_c, w_out_b, xc, mods5, i, ctx_row(tm_c),
                                          norm2[i], wr_cat, tm_c)
            logits_t = jnp.concatenate([lg_l, lg_c], axis=1)
            t_all = n_l + n_c

        io, wo, seg, pad, blk = _route_call(logits_t, b_router, ROUTE_TILE)
        m = MOE_BLOCK
        padded = blk[:, 0]
        pend = jnp.cumsum(padded)
        ntile = t_all // ROUTE_TILE
        nb = -(-(t_all * TOP_K + ntile * N_EXPERTS * (SUBLANES - 1)) // m) + N_EXPERTS
        p_rows = nb * m
        seg = seg.reshape(-1)
        pad = pad.reshape(-1)
        nused = (pend[-1:] // m).astype(I32)
        tail = nused + jnp.arange(nb - (t_all * TOP_K) // m, dtype=I32)
        zblk = jnp.concatenate([jnp.where(padded > 0, pend // m - 1, -1),
                                jnp.where(tail < nb, tail, -1)]).astype(I32)
        blk_row = jnp.minimum(jnp.arange(nb, dtype=I32), nused - 1) * m
        block_e = jnp.sum((pend[None, :] <= blk_row[:, None]).astype(I32), axis=1)
        rows_e = jnp.sum(pad.reshape(N_EXPERTS, LANES), axis=1)
        own = (block_e[:, None] == jnp.arange(N_EXPERTS, dtype=I32)[None, :]).astype(I32)
        seg_end = jnp.sum(own * (pend - padded + rows_e)[None, :], axis=1)
        blk_idx = jnp.arange(nb, dtype=I32)
        valid = jnp.where(blk_idx < nused, jnp.clip(seg_end - blk_idx * m, 0, m), 0).astype(I32)

        sources = (h2l,) if last else (h2l, h2c)
        xb = _dispatch_call(seg, pad, zblk, io, wo, sources, p_rows)
        eid = jnp.arange(N_EXPERTS, dtype=I32)
        later = (eid[None, :] > eid[:, None]) & (padded[None, :] > 0)
        next_e = jnp.min(jnp.where(later, eid[None, :], N_EXPERTS), axis=1)
        next_e = jnp.where(next_e < N_EXPERTS, next_e, -1).astype(I32)
        yb = _expert_call(block_e, nused, valid, next_e, xb, w_gate, w_up, w_down, i)
        xl_new = _combine_call(seg, pad, yb, io, xl, mods5, i, lat_row(ROUTE_TILE),
                               final_norm if last else None, 0)
        if not last:
            xc = _combine_call(seg, pad, yb, io, xc, mods5, i, ctx_row(0), None, n_l // ROUTE_TILE)
        xl = xl_new
    return xl.reshape(batch, seq, d)
```

```python
import functools

import jax
import jax.numpy as jnp
from jax import lax
from jax.experimental import pallas as pl
from jax.experimental.pallas import tpu as pltpu

F32 = jnp.float32
BF16 = jnp.bfloat16
I32 = jnp.int32
U32 = jnp.uint32

GRID_W = 64
NORM_EPS = 1e-6
GLA_HEADS = 4
GLA_DK = 64
GLA_DV = 128
GLA_KW = GLA_HEADS * GLA_DK
GLA_WIDTH = GLA_HEADS * GLA_DV
GLA_GATE_RANK = 16
GLA_TAU = 16.0
GLA_CHUNK = 64
SWA_HEADS = 8
SWA_KV_HEADS = 2
SWA_HD = 64
SWA_WIDTH = SWA_HEADS * SWA_HD
SWA_KVW = SWA_KV_HEADS * SWA_HD
WINDOW = 128
ATT_BLOCK = 128
ROPE_THETA = 10000.0
ROPE_AXIS_DIM = SWA_HD // 2
ROPE_HALF = ROPE_AXIS_DIM // 2
N_EXPERTS = 16
N_GROUPS = 4
EXPERTS_PER_GROUP = N_EXPERTS // N_GROUPS
TOP_K = 2
IN_SIZES = (GLA_KW, GLA_KW, GLA_WIDTH, GLA_WIDTH, GLA_GATE_RANK, GLA_GATE_RANK, SWA_WIDTH, SWA_KVW, SWA_KVW)

LANES = 128
SUBLANES = 8
VMEM_LIMIT = 48 * 1024 * 1024

ROW_TILE = 512
GLA_TILE = 1024
GLA_SUM_ROWS = 256
INPROJ_TILE = 1024
CTX_TILE = 256
ADA_COLS = 1536
SWA_BLOCKS_PER_STEP = 4
MOE_BLOCK = 512
ROUTE_TILE = 512
ROUTE_TILES_PER_STEP = 4
SEG_SIZES = (512, 256, 128, 64, 32, 16, 8)
STAGE_ROWS = TOP_K * ROUTE_TILE + LANES

C_QK = 0
C_V = 512
C_G = 1024
C_Z = 1536
C_SQ = 1664
C_SK = 2176
C_SV = 2304
C_END = 2432
SV_LANES = 4 * LANES
NEG_BIG = -1e30
LOG2E = 1.4426950408889634

assert WINDOW == ATT_BLOCK


def _cparams(sem):
    return pltpu.CompilerParams(dimension_semantics=sem, vmem_limit_bytes=VMEM_LIMIT)


def _pack_bf16_pairs(x):
    return _pack_rounded_pairs(x.astype(BF16).astype(F32))


def _pack_rounded_pairs(xr):
    n = xr.shape[1] // 2
    lo = lax.bitcast_convert_type(xr[:, :n], U32)
    hi = lax.bitcast_convert_type(xr[:, n:], U32)
    return (lo >> 16) | hi


def _unpack_bf16_pairs(w):
    lo = lax.bitcast_convert_type(w << 16, F32)
    hi = lax.bitcast_convert_type(w & jnp.uint32(0xFFFF0000), F32)
    return jnp.concatenate([lo.astype(BF16), hi.astype(BF16)], axis=-1)


def _ada_kernel(a_ref, w_ref, b_ref, o_ref):
    a = a_ref[...]
    act = a * jax.nn.sigmoid(a)
    a_hi = act.astype(BF16)
    a_lo = (act - a_hi.astype(F32)).astype(BF16)
    w = w_ref[...]
    w_hi = w.astype(BF16)
    w_lo = (w - w_hi.astype(F32)).astype(BF16)
    t = jnp.dot(jnp.concatenate([a_hi, a_lo], axis=0), w_hi, preferred_element_type=F32)
    n = a.shape[0]
    o_ref[...] = t[:n] + t[n:] + jnp.dot(a_hi, w_lo, preferred_element_type=F32) + b_ref[...]


def _ada_call(rows, w_ada, b_ada):
    depth, d, n6 = w_ada.shape
    tn = ADA_COLS
    return pl.pallas_call(
        _ada_kernel,
        grid=(depth, n6 // tn),
        in_specs=[
            pl.BlockSpec((SUBLANES, d), lambda l, j: (0, 0)),
            pl.BlockSpec((None, d, tn), lambda l, j: (l, 0, j)),
            pl.BlockSpec((None, 1, tn), lambda l, j: (l, 0, j)),
        ],
        out_specs=pl.BlockSpec((None, SUBLANES, tn), lambda l, j: (l, 0, j)),
        out_shape=jax.ShapeDtypeStruct((depth, SUBLANES, n6), F32),
        compiler_params=_cparams(("arbitrary", "arbitrary")),
        name="adaln",
    )(rows, w_ada, b_ada.reshape(depth, 1, n6))


def _rope_tile(xj, cos, sin, lane_lo):
    partner = jnp.where(lane_lo, pltpu.roll(xj, LANES - ROPE_HALF, 1), pltpu.roll(xj, ROPE_HALF, 1))
    return xj * cos + partner * sin


def _inproj_kernel(*refs, rope):
    if rope:
        (x_ref, g_ref, sc_ref, sh_ref, w_ref, cos_ref, sin_ref,
         qk_ref, v_ref, gate_ref, z_ref, sq_ref, sk_ref, sv_ref) = refs
    else:
        (x_ref, g_ref, sc_ref, sh_ref, w_ref,
         qk_ref, v_ref, gate_ref, z_ref, sq_ref, sk_ref, sv_ref) = refs
    half = x_ref.shape[0] // 2
    scale = g_ref[...] * (1.0 + sc_ref[...])
    normed = []
    for r in range(2):
        x = x_ref[r * half:(r + 1) * half, :]
        ms = jnp.mean(x * x, axis=-1, keepdims=True)
        normed.append(((x * lax.rsqrt(ms + NORM_EPS)) * scale + sh_ref[...]).astype(BF16))

    for r in range(2):
        rows = slice(r * half, (r + 1) * half)
        hb = normed[r]

        def proj(a, b):
            return jnp.dot(hb, w_ref[:, a:b], preferred_element_type=F32)

        qk = proj(C_QK, C_V)
        qk_ref[rows, :GLA_KW] = (qk[:, :GLA_KW] * (GLA_DK ** -0.5)).astype(qk_ref.dtype)
        qk_ref[rows, GLA_KW:] = qk[:, GLA_KW:].astype(qk_ref.dtype)
        v_ref[rows, :] = proj(C_V, C_G).astype(v_ref.dtype)
        gate_ref[rows, :] = proj(C_G, C_Z).astype(gate_ref.dtype)
        z_ref[rows, :] = proj(C_Z, C_SQ)
        sq = proj(C_SQ, C_SK) * (SWA_HD ** -0.5 * LOG2E)
        sk = proj(C_SK, C_SV)
        sv = proj(C_SV, C_END)
        if rope:
            cos = cos_ref[rows, :]
            sin = sin_ref[rows, :]
            lane_lo = (lax.broadcasted_iota(I32, cos.shape, 1) % ROPE_AXIS_DIM) < ROPE_HALF
            for j in range(SWA_WIDTH // LANES):
                sl = slice(j * LANES, (j + 1) * LANES)
                sq_ref[rows, sl] = _rope_tile(sq[:, sl], cos, sin, lane_lo).astype(sq_ref.dtype)
            sk = _rope_tile(sk, cos, sin, lane_lo)
        else:
            sq_ref[rows, :] = sq.astype(sq_ref.dtype)
        first = lax.broadcasted_iota(I32, sk.shape, 1) < SWA_HD
        sk_sw = pltpu.roll(sk, SWA_HD, 1)
        sk_ref[rows, :LANES] = jnp.where(first, sk, sk_sw).astype(sk_ref.dtype)
        sk_ref[rows, LANES:] = jnp.where(first, sk_sw, sk).astype(sk_ref.dtype)
        sv_sw = pltpu.roll(sv, SWA_HD, 1)
        sv_ref[rows, 0 * LANES:1 * LANES] = jnp.where(first, sv, 1.0).astype(sv_ref.dtype)
        sv_ref[rows, 1 * LANES:2 * LANES] = jnp.where(first, 1.0, sv_sw).astype(sv_ref.dtype)
        sv_ref[rows, 2 * LANES:3 * LANES] = jnp.where(first, sv_sw, 1.0).astype(sv_ref.dtype)
        sv_ref[rows, 3 * LANES:4 * LANES] = jnp.where(first, 1.0, sv).astype(sv_ref.dtype)


def _inproj_call(x2d, g, mods5, layer, row_of_tile, w_packed, tabs, tm, tiles_per_seq):
    n, d = x2d.shape
    nt = n // tm
    rope = tabs is not None

    def mod_spec(k):
        return pl.BlockSpec((None, None, None, 1, d), lambda t: (layer, row_of_tile(t), k, 0, 0))

    in_specs = [
        pl.BlockSpec((tm, d), lambda t: (t, 0)),
        pl.BlockSpec((1, d), lambda t: (0, 0)),
        mod_spec(1), mod_spec(0),
        pl.BlockSpec((None, d, C_END), lambda t: (layer, 0, 0)),
    ]
    args = [x2d, g.reshape(1, d), mods5, mods5, w_packed]
    if rope:
        in_specs += [pl.BlockSpec((tm, LANES), lambda t: (t % tiles_per_seq, 0))] * 2
        args += [tabs[0], tabs[1]]
    widths = (2 * GLA_KW, GLA_WIDTH, GLA_WIDTH, LANES, SWA_WIDTH, 2 * SWA_KVW, SV_LANES)
    dtypes = (BF16, BF16, BF16, F32, BF16, BF16, BF16)
    return pl.pallas_call(
        functools.partial(_inproj_kernel, rope=rope),
        grid=(nt,),
        in_specs=in_specs,
        out_specs=[pl.BlockSpec((tm, w), lambda t: (t, 0)) for w in widths],
        out_shape=[jax.ShapeDtypeStruct((n, w), dt) for w, dt in zip(widths, dtypes)],
        compiler_params=_cparams(("arbitrary",)),
        name="inproj",
    )(*args)


def _log_sigmoid(x):
    return jnp.minimum(x, 0.0) - jnp.log(1.0 + jnp.exp(-jnp.abs(x)))


def _gla_prepare(q_ref, k_ref, v_ref, z_ref, up_ref, bias_ref, tri_ref, scr, fwd):
    qin_s, km0_s, km1_s, kout_s, vm0_s, vm1_s, dec_s = scr[:7]
    qbd_s = scr[9]
    tl = q_ref.shape[0]
    c = GLA_CHUNK
    z = z_ref[...]
    z_hi = z.astype(BF16).astype(F32)
    zc = z_hi + pltpu.roll(z - z_hi, 2 * GLA_GATE_RANK, 1) + pltpu.roll(z_hi, 4 * GLA_GATE_RANK, 1)
    x = jnp.dot(zc.astype(BF16), up_ref[...], preferred_element_type=F32) + bias_ref[...]
    la = _log_sigmoid(x) * (LOG2E / GLA_TAU)
    hi = la.astype(BF16)
    lo = (la - hi.astype(F32)).astype(BF16)
    hl = jnp.concatenate([hi, lo], axis=1)
    tg = tri_ref.shape[0]
    sums = jnp.concatenate([jnp.dot(tri_ref[...], hl[r0:r0 + tg, :], preferred_element_type=F32)
                            for r0 in range(0, tl, tg)], axis=0)
    b = sums[:, :LANES] + sums[:, LANES:]
    b3 = b.reshape(tl // c, c, LANES)
    edge = b3[:, c - 1:c, :] if fwd else b3[:, 0:1, :]
    btot = jnp.broadcast_to(edge, b3.shape).reshape(tl, LANES)
    q = q_ref[...].astype(F32)
    k = k_ref[...].astype(F32)
    first = (lax.broadcasted_iota(I32, (tl, LANES), 0) % (2 * c)) < c
    q_in = q * jnp.exp2(b)
    qin_s[...] = q_in.astype(BF16)
    qbd_s[...] = jnp.concatenate([jnp.where(first, q_in, 0.0), jnp.where(first, 0.0, q_in)], axis=1).astype(BF16)
    k_in = k * jnp.exp2(-b)
    head0 = lax.broadcasted_iota(I32, (tl, LANES), 1) < GLA_DK
    km0_s[...] = jnp.where(head0, k_in, 0.0).astype(BF16)
    km1_s[...] = jnp.where(head0, 0.0, k_in).astype(BF16)
    k_out = k * jnp.exp2(btot - b)
    kout_s[...] = jnp.concatenate([jnp.where(first, k_out, 0.0), jnp.where(first, 0.0, k_out)], axis=1).astype(BF16)
    dec_s[...] = jnp.exp2(btot)
    vf = v_ref[...].astype(F32)
    vhead0 = lax.broadcasted_iota(I32, (tl, 2 * GLA_DV), 1) < GLA_DV
    vm0_s[...] = jnp.where(vhead0, vf, 0.0).astype(BF16)
    vm1_s[...] = jnp.where(vhead0, 0.0, vf).astype(BF16)


def _gla_increment(v_ref, scr, pair):
    kout_s, u_s = scr[3], scr[7]
    rows = pl.ds(pair * 2 * GLA_CHUNK, 2 * GLA_CHUNK)
    u_t = lax.dot_general(v_ref[rows, :], kout_s[rows, :], (((0,), (0,)), ((), ())),
                          preferred_element_type=F32)
    srow = lax.broadcasted_iota(I32, (2 * GLA_DV, 2 * LANES), 0) // GLA_DV
    scol = (lax.broadcasted_iota(I32, (2 * GLA_DV, 2 * LANES), 1) % LANES) // GLA_DK
    u_s[pair] = jnp.where(srow == scol, u_t, 0.0)


def _gla_states(scr, s_ref, chunk_order):
    dec_s, u_s, sprev_s = scr[6], scr[7], scr[8]
    s_t = s_ref[...]
    for cidx in chunk_order:
        pair, lanes = cidx // 2, pl.ds((cidx % 2) * LANES, LANES)
        sprev_s[pair, :, lanes] = s_t.astype(BF16)
        r0 = cidx * GLA_CHUNK
        s_t = s_t * dec_s[r0:r0 + 1, :] + u_s[pair, :, lanes]
    s_ref[...] = s_t


def _gla_output(o_ref, scr, pair, fwd):
    qin_s, km0_s, km1_s, _, vm0_s, vm1_s, _, _, sprev_s, qbd_s = scr
    c = GLA_CHUNK
    r0 = pair * 2 * c
    rows = pl.ds(r0, 2 * c)
    ca, cb = pl.ds(r0, c), pl.ds(r0 + c, c)
    kst = jnp.concatenate([km0_s[ca, :], km1_s[ca, :], km0_s[cb, :], km1_s[cb, :]], axis=0)
    a = lax.dot_general(qin_s[rows, :], kst, (((1,), (1,)), ((), ())), preferred_element_type=F32)
    ri = lax.broadcasted_iota(I32, (2 * c, 4 * c), 0)
    ci = lax.broadcasted_iota(I32, (2 * c, 4 * c), 1)
    same_chunk = (ri // c) == (ci // (2 * c))
    keep = same_chunk & ((ci % c <= ri % c) if fwd else (ci % c >= ri % c))
    a = jnp.where(keep, a, 0.0).astype(BF16)
    vbd = jnp.concatenate([vm0_s[ca, :], vm1_s[ca, :], vm0_s[cb, :], vm1_s[cb, :]], axis=0)
    o = jnp.dot(a, vbd, preferred_element_type=F32)
    o = o + lax.dot_general(qbd_s[rows, :], sprev_s[pair], (((1,), (1,)), ((), ())), preferred_element_type=F32)
    o_ref[rows, :] = o.astype(o_ref.dtype)


def _gla_kernel(qf_ref, kf_ref, vf_ref, zf_ref, qb_ref, kb_ref, vb_ref, zb_ref,
                upf_ref, upb_ref, bf_ref, bb_ref, trif_ref, trib_ref, s0_ref,
                of_ref, ob_ref, sfin_ref, sf_scr, sb_scr, *scr, nchunk):
    i = pl.program_id(2)
    nt = pl.num_programs(2)
    scr_f, scr_b = scr[:len(scr) // 2], scr[len(scr) // 2:]

    @pl.when(i == 0)
    def _():
        sf_scr[...] = s0_ref[0]
        sb_scr[...] = s0_ref[1]

    _gla_prepare(qf_ref, kf_ref, vf_ref, zf_ref, upf_ref, bf_ref, trif_ref, scr_f, True)
    _gla_prepare(qb_ref, kb_ref, vb_ref, zb_ref, upb_ref, bb_ref, trib_ref, scr_b, False)
    for pair in range(nchunk // 2):
        _gla_increment(vf_ref, scr_f, pair)
        _gla_increment(vb_ref, scr_b, pair)
    _gla_states(scr_f, sf_scr, range(nchunk))
    _gla_states(scr_b, sb_scr, range(nchunk - 1, -1, -1))
    for pair in range(nchunk // 2):
        _gla_output(of_ref, scr_f, pair, True)
        _gla_output(ob_ref, scr_b, pair, False)

    @pl.when(i == nt - 1)
    def _():
        sfin_ref[0] = sf_scr[...]
        sfin_ref[1] = sb_scr[...]


def _gla_call(qk, v, z, upf_pad, upb_pad, bias_f, bias_b, s0, batch, seq, tl):
    n = qk.shape[0]
    nt = seq // tl
    npair = GLA_HEADS // 2

    def fwd_row(b, p, i):
        return b * nt + i

    def bwd_row(b, p, i):
        return b * nt + (nt - 1 - i)

    def specs(row):
        return [
            pl.BlockSpec((tl, LANES), lambda b, p, i: (row(b, p, i), p)),
            pl.BlockSpec((tl, LANES), lambda b, p, i: (row(b, p, i), npair + p)),
            pl.BlockSpec((tl, 2 * GLA_DV), lambda b, p, i: (row(b, p, i), p)),
            pl.BlockSpec((tl, LANES), lambda b, p, i: (row(b, p, i), 0)),
        ]

    in_specs = specs(fwd_row) + specs(bwd_row) + [
        pl.BlockSpec((None, LANES, LANES), lambda b, p, i: (p, 0, 0)),
        pl.BlockSpec((None, LANES, LANES), lambda b, p, i: (p, 0, 0)),
        pl.BlockSpec((None, 1, LANES), lambda b, p, i: (p, 0, 0)),
        pl.BlockSpec((None, 1, LANES), lambda b, p, i: (p, 0, 0)),
        pl.BlockSpec((None, GLA_SUM_ROWS, GLA_SUM_ROWS), lambda b, p, i: (0, 0, 0)),
        pl.BlockSpec((None, GLA_SUM_ROWS, GLA_SUM_ROWS), lambda b, p, i: (1, 0, 0)),
        pl.BlockSpec((None, None, 2, 2 * GLA_DV, LANES), lambda b, p, i: (b, p, 0, 0, 0)),
    ]
    assert tl % GLA_SUM_ROWS == 0
    ri = jnp.arange(GLA_SUM_ROWS)[:, None]
    ci = jnp.arange(GLA_SUM_ROWS)[None, :]
    same = (ri // GLA_CHUNK) == (ci // GLA_CHUNK)
    tri = jnp.stack([same & (ci <= ri), same & (ci >= ri)]).astype(BF16)
    npairs = tl // (2 * GLA_CHUNK)
    dir_scratch = [
        pltpu.VMEM((tl, LANES), BF16),
        pltpu.VMEM((tl, LANES), BF16),
        pltpu.VMEM((tl, LANES), BF16),
        pltpu.VMEM((tl, 2 * LANES), BF16),
        pltpu.VMEM((tl, 2 * GLA_DV), BF16),
        pltpu.VMEM((tl, 2 * GLA_DV), BF16),
        pltpu.VMEM((tl, LANES), F32),
        pltpu.VMEM((npairs, 2 * GLA_DV, 2 * LANES), F32),
        pltpu.VMEM((npairs, 2 * GLA_DV, 2 * LANES), BF16),
        pltpu.VMEM((tl, 2 * LANES), BF16),
    ]
    out_specs = [
        pl.BlockSpec((tl, 2 * GLA_DV), lambda b, p, i: (fwd_row(b, p, i), p)),
        pl.BlockSpec((tl, 2 * GLA_DV), lambda b, p, i: (bwd_row(b, p, i), p)),
        pl.BlockSpec((None, None, 2, 2 * GLA_DV, LANES), lambda b, p, i: (b, p, 0, 0, 0)),
    ]
    out_shape = [
        jax.ShapeDtypeStruct((n, GLA_WIDTH), BF16),
        jax.ShapeDtypeStruct((n, GLA_WIDTH), BF16),
        jax.ShapeDtypeStruct((batch, npair, 2, 2 * GLA_DV, LANES), F32),
    ]
    return pl.pallas_call(
        functools.partial(_gla_kernel, nchunk=tl // GLA_CHUNK),
        grid=(batch, npair, nt),
        in_specs=in_specs,
        out_specs=out_specs,
        out_shape=out_shape,
        scratch_shapes=[pltpu.VMEM((2 * GLA_DV, LANES), F32), pltpu.VMEM((2 * GLA_DV, LANES), F32)]
        + dir_scratch + dir_scratch,
        compiler_params=_cparams(("arbitrary", "arbitrary", "arbitrary")),
        name="gla",
    )(qk, qk, v, z, qk, qk, v, z, upf_pad, upb_pad, bias_f, bias_b, tri, tri, s0)


def _swa_block(sink_ref, q_ref, o_ref, row0, kall, vall, prev_ok, next_ok):
    w = ATT_BLOCK
    local = prev_ok is not None
    qrows = pl.ds(row0, w)
    r2 = lax.broadcasted_iota(I32, (2 * w, w), 0) % w
    c2 = lax.broadcasted_iota(I32, (2 * w, w), 1)
    if local:
        bias_prev = jnp.where((c2 >= r2) & prev_ok, 0.0, NEG_BIG)
        bias_next = jnp.where((c2 <= r2) & next_ok, 0.0, NEG_BIG)
    first = c2 < SWA_HD
    top = lax.broadcasted_iota(I32, (2 * w, 1), 0) < w
    scores = []
    for g in range(SWA_KV_HEADS):
        kd = kall[:, g * LANES:(g + 1) * LANES]
        qs = jnp.concatenate([q_ref[qrows, (2 * g) * LANES:(2 * g + 1) * LANES],
                              q_ref[qrows, (2 * g + 1) * LANES:(2 * g + 2) * LANES]], axis=0).astype(F32)
        for half in range(2):
            qm = (jnp.where(first, qs, 0.0) if half == 0 else jnp.where(first, 0.0, qs)).astype(BF16)
            s = lax.dot_general(qm, kd, (((1,), (1,)), ((), ())), preferred_element_type=F32)
            if local:
                s = jnp.concatenate([s[:, :w] + bias_prev, s[:, w:2 * w], s[:, 2 * w:3 * w] + bias_next,
                                     s[:, 3 * w:]], axis=1)
            scores.append(s)
    for g in range(SWA_KV_HEADS):
        outs = []
        for half in range(2):
            s = scores[2 * g + half]
            sk = jnp.where(top, sink_ref[4 * g + half], sink_ref[4 * g + 2 + half]) * LOG2E
            m = jnp.maximum(jnp.max(s, axis=-1, keepdims=True), sk)
            p = jnp.exp2((s - m).astype(BF16))
            va = vall[:, (2 * g + half) * LANES:(2 * g + half + 1) * LANES]
            acc = jnp.dot(p, va, preferred_element_type=F32)
            den = pltpu.roll(acc, SWA_HD, 1) + jnp.exp2(sk - m)
            outs.append(acc / den)
        out = jnp.where(first, outs[0], outs[1])
        o_ref[qrows, (2 * g) * LANES:(2 * g + 1) * LANES] = out[:w].astype(o_ref.dtype)
        o_ref[qrows, (2 * g + 1) * LANES:(2 * g + 2) * LANES] = out[w:].astype(o_ref.dtype)


def _swa_kernel(*refs, nstep, local):
    w = ATT_BLOCK
    if not local:
        sink_ref, q_ref, kx_ref, vx_ref, o_ref = refs
        _swa_block(sink_ref, q_ref, o_ref, 0, kx_ref[...], vx_ref[...], None, None)
        return
    sink_ref, q_ref, kp_ref, kc_ref, kn_ref, vp_ref, vc_ref, vn_ref, kx_ref, vx_ref, o_ref = refs
    i = pl.program_id(1)
    nblk = q_ref.shape[0] // w
    k_blocks = [kp_ref[...]] + [kc_ref[j * w:(j + 1) * w, :] for j in range(nblk)] + [kn_ref[...]]
    v_blocks = [vp_ref[...]] + [vc_ref[j * w:(j + 1) * w, :] for j in range(nblk)] + [vn_ref[...]]
    exists = [i > 0] + [True] * nblk + [i < nstep - 1]
    for sub in range(nblk):
        kall = jnp.concatenate(k_blocks[sub:sub + 3] + [kx_ref[...]], axis=0)
        vall = jnp.concatenate(v_blocks[sub:sub + 3] + [vx_ref[...]], axis=0)
        _swa_block(sink_ref, q_ref, o_ref, sub * w, kall, vall, exists[sub], exists[sub + 2])


def _swa_call(sq, skd, svd, kcd, vcd, sink, batch, seq, lc):
    n = sq.shape[0]
    w = ATT_BLOCK
    nb = seq // w
    nblk = SWA_BLOCKS_PER_STEP if nb % SWA_BLOCKS_PER_STEP == 0 else 1
    nstep = nb // nblk
    kvw = 2 * SWA_KVW

    def pair(b, i):
        return (b * nstep + i, 0)

    def before(b, i):
        return (b * nb + jnp.maximum(nblk * i - 1, 0), 0)

    def after(b, i):
        return (b * nb + jnp.minimum(nblk * (i + 1), nb - 1), 0)

    def kv_specs(width):
        return [pl.BlockSpec((w, width), before), pl.BlockSpec((nblk * w, width), pair),
                pl.BlockSpec((w, width), after)]

    def ctx_spec(width):
        return pl.BlockSpec((lc, width), lambda b, i: (b, 0))

    return pl.pallas_call(
        functools.partial(_swa_kernel, nstep=nstep, local=True),
        grid=(batch, nstep),
        in_specs=[pl.BlockSpec(memory_space=pltpu.SMEM), pl.BlockSpec((nblk * w, SWA_WIDTH), pair)]
        + kv_specs(kvw) + kv_specs(SV_LANES) + [ctx_spec(kvw), ctx_spec(SV_LANES)],
        out_specs=pl.BlockSpec((nblk * w, SWA_WIDTH), pair),
        out_shape=jax.ShapeDtypeStruct((n, SWA_WIDTH), BF16),
        compiler_params=_cparams(("arbitrary", "arbitrary")),
        name="swa",
    )(sink, sq, skd, skd, skd, svd, svd, svd, kcd, vcd)


def _swa_ctx_call(sq, kcd, vcd, sink, batch, lc):
    n = sq.shape[0]
    w = ATT_BLOCK
    nb = lc // w
    kvw = 2 * SWA_KVW
    return pl.pallas_call(
        functools.partial(_swa_kernel, nstep=0, local=False),
        grid=(batch, nb),
        in_specs=[pl.BlockSpec(memory_space=pltpu.SMEM),
                  pl.BlockSpec((w, SWA_WIDTH), lambda b, i: (b * nb + i, 0)),
                  pl.BlockSpec((lc, kvw), lambda b, i: (b, 0)),
                  pl.BlockSpec((lc, SV_LANES), lambda b, i: (b, 0))],
        out_specs=pl.BlockSpec((w, SWA_WIDTH), lambda b, i: (b * nb + i, 0)),
        out_shape=jax.ShapeDtypeStruct((n, SWA_WIDTH), BF16),
        compiler_params=_cparams(("arbitrary", "arbitrary")),
        name="swa_ctx",
    )(sink, sq, kcd, vcd)


def _outproj_kernel(of_ref, ob_ref, gate_ref, gn_ref, swa_ref, w_ref, x_ref, g1_ref, n2_ref, sc_ref, sh_ref,
                    wr_ref, xo_ref, h2_ref, lg_ref):
    o = of_ref[...].astype(F32) + ob_ref[...].astype(F32)
    parts = []
    for h in range(GLA_HEADS):
        oh = o[:, h * GLA_DV:(h + 1) * GLA_DV]
        ms = jnp.mean(oh * oh, axis=-1, keepdims=True)
        parts.append(oh * lax.rsqrt(ms + NORM_EPS))
    on = jnp.concatenate(parts, axis=-1) * gn_ref[...]
    gate = gate_ref[...].astype(F32)
    gla = on * (gate * jax.nn.sigmoid(gate))
    mix = jnp.concatenate([gla.astype(BF16), swa_ref[...]], axis=-1)
    y = jnp.dot(mix, w_ref[...], preferred_element_type=F32)
    xo = x_ref[...] + g1_ref[...] * y
    xo_ref[...] = xo
    ms = jnp.mean(xo * xo, axis=-1, keepdims=True)
    h2 = (xo * lax.rsqrt(ms + NORM_EPS)) * (n2_ref[...] * (1.0 + sc_ref[...])) + sh_ref[...]
    hi = h2.astype(BF16)
    hi_f = hi.astype(F32)
    h2_ref[...] = _pack_rounded_pairs(hi_f)
    lo = (h2 - hi_f).astype(BF16)
    both = jnp.dot(hi, wr_ref[...], preferred_element_type=F32)
    lg = both[:, :LANES] + both[:, LANES:] + jnp.dot(lo, wr_ref[:, :LANES], preferred_element_type=F32)
    lg_ref[...] = jnp.transpose(lg)[:N_EXPERTS, :]


def _outproj_call(o_f, o_b, gate, gn, swa, w_out_b, x2d, mods5, layer, row_of_tile, n2, wr_cat, tm):
    n, d = x2d.shape
    nt = n // tm

    def mod_spec(k):
        return pl.BlockSpec((None, None, None, 1, d), lambda t: (layer, row_of_tile(t), k, 0, 0))

    return pl.pallas_call(
        _outproj_kernel,
        grid=(nt,),
        in_specs=[
            pl.BlockSpec((tm, GLA_WIDTH), lambda t: (t, 0)),
            pl.BlockSpec((tm, GLA_WIDTH), lambda t: (t, 0)),
            pl.BlockSpec((tm, GLA_WIDTH), lambda t: (t, 0)),
            pl.BlockSpec((1, GLA_WIDTH), lambda t: (0, 0)),
            pl.BlockSpec((tm, SWA_WIDTH), lambda t: (t, 0)),
            pl.BlockSpec((None, d, d), lambda t: (layer, 0, 0)),
            pl.BlockSpec((tm, d), lambda t: (t, 0)),
            mod_spec(2),
            pl.BlockSpec((1, d), lambda t: (0, 0)),
            mod_spec(4), mod_spec(3),
            pl.BlockSpec((d, 2 * LANES), lambda t: (0, 0)),
        ],
        out_specs=[
            pl.BlockSpec((tm, d), lambda t: (t, 0)),
            pl.BlockSpec((tm, d // 2), lambda t: (t, 0)),
            pl.BlockSpec((N_EXPERTS, tm), lambda t: (0, t)),
        ],
        out_shape=[
            jax.ShapeDtypeStruct((n, d), F32),
            jax.ShapeDtypeStruct((n, d // 2), U32),
            jax.ShapeDtypeStruct((N_EXPERTS, n), F32),
        ],
        compiler_params=_cparams(("arbitrary",)),
        name="outproj",
    )(o_f, o_b, gate, gn.reshape(1, GLA_WIDTH), swa, w_out_b, x2d, mods5, n2.reshape(1, d), mods5, mods5, wr_cat)


def _first_index(vals, target):
    idx = jnp.full(target.shape, len(vals) - 1, I32)
    for i in range(len(vals) - 2, -1, -1):
        idx = jnp.where(vals[i] == target, i, idx)
    return idx


def _route_kernel(lg_ref, br_ref, io_ref, wo_ref, seg_ref, pad_ref, blk_ref, cnt_scr, seg_scr, pad_scr, *, tn):
    phase = pl.program_id(0)
    step = pl.program_id(1)
    group = lg_ref.shape[1] // tn

    @pl.when((phase == 0) & (step == 0))
    def _():
        cnt_scr[...] = jnp.zeros_like(cnt_scr)
        seg_scr[...] = jnp.zeros_like(seg_scr)
        pad_scr[...] = jnp.zeros_like(pad_scr)

    @pl.when((phase == 1) & (step == 0))
    def _():
        seg_len = jnp.floor((cnt_scr[...] + (SUBLANES - 1)) * (1.0 / SUBLANES)) * SUBLANES
        rows_e = jnp.sum(seg_len, axis=1, keepdims=True)
        blocks = jnp.floor((rows_e + (MOE_BLOCK - 1)) * (1.0 / MOE_BLOCK)) * MOE_BLOCK
        r128 = lax.broadcasted_iota(I32, (LANES, LANES), 0)
        c128 = lax.broadcasted_iota(I32, (LANES, LANES), 1)
        before_tile = jnp.where(r128 < c128, 1.0, 0.0).astype(BF16)
        seg = jnp.dot((seg_len * (1.0 / SUBLANES)).astype(BF16), before_tile,
                      preferred_element_type=F32) * SUBLANES
        run = jnp.zeros((1, 1), F32)
        for e in range(N_EXPERTS):
            seg_scr[e:e + 1, :] = seg[e:e + 1, :] + run
            run = run + blocks[e:e + 1, :]
        pad_scr[...] = seg_len
        seg_ref[...] = seg_scr[...].astype(I32)
        pad_ref[...] = seg_len.astype(I32)
        blk_ref[...] = jnp.broadcast_to(blocks, blk_ref.shape).astype(I32)

    for g in range(group):
        lanes = slice(g * tn, (g + 1) * tn)
        _route_tile(lg_ref[:, lanes], br_ref, io_ref, wo_ref, lanes, step * group + g, phase, cnt_scr, pad_scr)


def _route_tile(lg, br_ref, io_ref, wo_ref, lanes, t, phase, cnt_scr, pad_scr):
    tn = lg.shape[1]
    tile_lane = lax.broadcasted_iota(I32, (N_EXPERTS, LANES), 1)
    s = jax.nn.sigmoid(lg)
    sb = s + br_ref[...]
    rows_s = [s[e:e + 1, :] for e in range(N_EXPERTS)]
    rows_b = [sb[e:e + 1, :] for e in range(N_EXPERTS)]
    gscore, gi1, gi2 = [], [], []
    epg = EXPERTS_PER_GROUP
    for g in range(N_GROUPS):
        a = rows_b[g * epg:(g + 1) * epg]
        m1 = functools.reduce(jnp.maximum, a)
        i1 = _first_index(a, m1)
        rest = [jnp.where(i1 == i, -jnp.inf, a[i]) for i in range(epg)]
        m2 = functools.reduce(jnp.maximum, rest)
        i2 = _first_index(rest, m2)
        gscore.append(m1 + m2)
        gi1.append(i1)
        gi2.append(i2)
    gm = functools.reduce(jnp.maximum, gscore)
    gsel = _first_index(gscore, gm)
    i1 = gi1[N_GROUPS - 1]
    i2 = gi2[N_GROUPS - 1]
    for g in range(N_GROUPS - 2, -1, -1):
        i1 = jnp.where(gsel == g, gi1[g], i1)
        i2 = jnp.where(gsel == g, gi2[g], i2)
    idx0 = gsel * epg + i1
    idx1 = gsel * epg + i2
    s0 = jnp.zeros_like(rows_s[0])
    s1 = jnp.zeros_like(rows_s[0])
    for e in range(N_EXPERTS):
        s0 = jnp.where(idx0 == e, rows_s[e], s0)
        s1 = jnp.where(idx1 == e, rows_s[e], s1)
    tot = s0 + s1
    w0 = s0 / tot
    w1 = s1 / tot

    eidx = lax.broadcasted_iota(I32, (N_EXPERTS, tn), 0)
    oh0 = eidx == idx0
    oh1 = eidx == idx1
    oh = jnp.where(oh0 | oh1, 1.0, 0.0)

    @pl.when(phase == 0)
    def _():
        cnt_scr[...] = cnt_scr[...] + jnp.where(tile_lane == t, jnp.sum(oh, axis=1, keepdims=True), 0.0)

    @pl.when(phase == 1)
    def _():
        rr = lax.broadcasted_iota(I32, (tn, tn), 0)
        cc = lax.broadcasted_iota(I32, (tn, tn), 1)
        upper = jnp.where(rr < cc, 1.0, 0.0).astype(BF16)
        before = jnp.dot(oh.astype(BF16), upper, preferred_element_type=F32)
        pad_col = jnp.sum(jnp.where(tile_lane == t, pad_scr[...], 0.0), axis=1, keepdims=True)
        run = jnp.zeros((1, 1), F32)
        offs = []
        for e in range(N_EXPERTS):
            offs.append(run)
            run = run + pad_col[e:e + 1, :]
        pos = before + jnp.concatenate(offs, axis=0)
        spos0 = jnp.sum(jnp.where(oh0, pos, 0.0), axis=0, keepdims=True)
        spos1 = jnp.sum(jnp.where(oh1, pos, 0.0), axis=0, keepdims=True)
        zi = jnp.zeros((SUBLANES - 4, tn), I32)
        io_ref[:, lanes] = jnp.concatenate([idx0, idx1, spos0.astype(I32), spos1.astype(I32), zi], axis=0)
        wo_ref[:, lanes] = jnp.concatenate([w0, w1, jnp.zeros((SUBLANES - 2, tn), F32)], axis=0)


def _route_call(logits_t, b_router, tn):
    ne, t_all = logits_t.shape
    ntile = t_all // tn
    assert ntile <= LANES
    group = max(g for g in range(1, ROUTE_TILES_PER_STEP + 1) if ntile % g == 0)
    tw = group * tn
    tok_out = pl.BlockSpec((SUBLANES, tw), lambda p, t: (0, t * p))
    tab_out = pl.BlockSpec((ne, LANES), lambda p, t: (0, 0))
    tab = jax.ShapeDtypeStruct((ne, LANES), I32)
    return pl.pallas_call(
        functools.partial(_route_kernel, tn=tn),
        grid=(2, ntile // group),
        in_specs=[pl.BlockSpec((ne, tw), lambda p, t: (0, t)), pl.BlockSpec((ne, 1), lambda p, t: (0, 0))],
        out_specs=[tok_out, tok_out, tab_out, tab_out, tab_out],
        out_shape=[
            jax.ShapeDtypeStruct((SUBLANES, t_all), I32),
            jax.ShapeDtypeStruct((SUBLANES, t_all), F32),
            tab, tab, tab,
        ],
        scratch_shapes=[pltpu.VMEM((ne, LANES), F32)] * 3,
        compiler_params=_cparams(("arbitrary", "arbitrary")),
        name="route",
    )(logits_t, b_router.reshape(ne, 1))


def _segment_copies(seg_ref, pad_ref, t, stage, hbm, sem, to_hbm, wait):
    if wait:
        total = functools.reduce(lambda a, b: a + b, [pad_ref[e * LANES + t] for e in range(N_EXPERTS)])
        for size in (2 * SEG_SIZES[0],) + SEG_SIZES:
            @pl.when((total & size) != 0)
            def _(size=size):
                src, dst = stage.at[pl.ds(0, size), :], hbm.at[pl.ds(0, size), :]
                if not to_hbm:
                    src, dst = dst, src
                pltpu.make_async_copy(src, dst, sem).wait()
        return
    loc = 0
    for e in range(N_EXPERTS):
        n = pad_ref[e * LANES + t]
        start = seg_ref[e * LANES + t]
        for size in SEG_SIZES:
            off = n & (-2 * size)

            @pl.when((n & size) != 0)
            def _(off=off, size=size, loc=loc, start=start):
                s_rows = pl.ds(pl.multiple_of(loc + off, SUBLANES), size)
                h_rows = pl.ds(pl.multiple_of(start + off, SUBLANES), size)
                src, dst = (stage.at[s_rows, :], hbm.at[h_rows, :])
                if not to_hbm:
                    src, dst = dst, src
                pltpu.make_async_copy(src, dst, sem).start()
        loc = loc + n


def _sort_matrix(io_ref, rows):
    r = lax.broadcasted_iota(I32, (rows, io_ref.shape[1]), 0)
    return r == io_ref[2:3, :], r == io_ref[3:4, :]


def _dispatch_kernel(seg_ref, pad_ref, zblk_ref, *refs, tiles):
    srcs = refs[:len(tiles)]
    io_ref, wo_ref, xb_hbm, stage, zbuf, sem = refs[len(tiles):]
    i = pl.program_id(0)
    m = MOE_BLOCK

    def zero_copy(j):
        start = pl.multiple_of(zblk_ref[j] * m, m)
        return pltpu.make_async_copy(zbuf, xb_hbm.at[pl.ds(start, m), :], sem.at[0])

    @pl.when(i == 0)
    def _():
        zbuf[...] = jnp.zeros_like(zbuf)
        for j in range(zblk_ref.shape[0]):
            @pl.when(zblk_ref[j] >= 0)
            def _():
                zero_copy(j).start()
        for j in range(zblk_ref.shape[0]):
            @pl.when(zblk_ref[j] >= 0)
            def _():
                zero_copy(j).wait()

    hp = srcs[0][...]
    if len(srcs) == 2:
        hp = jnp.where(i < tiles[0], hp, srcs[1][...])
    x = _unpack_bf16_pairs(hp)
    m0, m1 = _sort_matrix(io_ref, STAGE_ROWS)
    sort = jnp.where(m0 | m1, 1.0, 0.0).astype(BF16)
    xs = jnp.dot(sort, x, preferred_element_type=F32)
    dh = x.shape[1] // 2
    slot = i % 2
    cur = stage.at[slot]
    cur[:, :dh] = _pack_rounded_pairs(xs)
    ws = jnp.sum(jnp.where(m0, wo_ref[0:1, :], 0.0) + jnp.where(m1, wo_ref[1:2, :], 0.0), axis=1, keepdims=True)
    cur[:, dh:] = jnp.broadcast_to(lax.bitcast_convert_type(ws, U32), (STAGE_ROWS, LANES))
    _segment_copies(seg_ref, pad_ref, i, cur, xb_hbm, sem.at[slot], to_hbm=True, wait=False)

    @pl.when(i > 0)
    def _():
        _segment_copies(seg_ref, pad_ref, i - 1, stage.at[1 - slot], xb_hbm, sem.at[1 - slot], to_hbm=True, wait=True)

    @pl.when(i == pl.num_programs(0) - 1)
    def _():
        _segment_copies(seg_ref, pad_ref, i, cur, xb_hbm, sem.at[slot], to_hbm=True, wait=True)


def _dispatch_call(seg, pad, zblk, io, wo, sources, p_rows):
    tile = ROUTE_TILE
    dh = sources[0].shape[1]
    tiles = tuple(s.shape[0] // tile for s in sources)
    firsts = tuple(sum(tiles[:k]) for k in range(len(tiles)))

    def src_spec(first, ntile):
        return pl.BlockSpec((tile, dh), lambda i, *_: (jnp.clip(i - first, 0, ntile - 1), 0))

    tok_spec = pl.BlockSpec((SUBLANES, tile), lambda i, *_: (0, i))
    grid_spec = pltpu.PrefetchScalarGridSpec(
        num_scalar_prefetch=3,
        grid=(sum(tiles),),
        in_specs=[src_spec(f, n) for f, n in zip(firsts, tiles)] + [tok_spec, tok_spec],
        out_specs=pl.BlockSpec(memory_space=pl.ANY),
        scratch_shapes=[pltpu.VMEM((2, STAGE_ROWS, dh + LANES), U32), pltpu.VMEM((MOE_BLOCK, dh + LANES), U32),
                        pltpu.SemaphoreType.DMA((2,))],
    )
    return pl.pallas_call(
        functools.partial(_dispatch_kernel, tiles=tiles),
        grid_spec=grid_spec,
        out_shape=jax.ShapeDtypeStruct((p_rows, dh + LANES), U32),
        compiler_params=_cparams(("arbitrary",)),
        name="dispatch",
    )(seg, pad, zblk, *sources, io, wo)


def _expert_kernel(be_ref, nu_ref, va_ref, ne_ref, x_ref, wg_hbm, wu_hbm, wd_hbm, o_ref,
                   wg_b, wu_b, wd_b, wg_f, wu_f, wd_f, switch_ref, sem, *, layer):
    j = pl.program_id(0)
    dh = o_ref.shape[1]
    half = o_ref.shape[0] // 2
    valid = va_ref[j]
    e = be_ref[j]

    def weight_copies(expert, slot):
        return [pltpu.make_async_copy(hbm.at[layer, expert], buf.at[slot], sem.at[slot, k])
                for k, (hbm, buf) in enumerate(((wg_hbm, wg_f), (wu_hbm, wu_f), (wd_hbm, wd_f)))]

    @pl.when(j == 0)
    def _():
        switch_ref[0] = 0
        for cp in weight_copies(e, 0):
            cp.start()

    @pl.when((j == 0) | (e != be_ref[jnp.maximum(j - 1, 0)]))
    def _():
        slot = switch_ref[0] % 2
        for cp in weight_copies(e, slot):
            cp.wait()
        wg_b[...] = wg_f[slot].astype(BF16)
        wu_b[...] = wu_f[slot].astype(BF16)
        wd_b[...] = wd_f[slot].astype(BF16)
        nxt = ne_ref[e]

        @pl.when(nxt >= 0)
        def _():
            for cp in weight_copies(nxt, 1 - slot):
                cp.start()
        switch_ref[0] = switch_ref[0] + 1

    def ffn_halves(halves):
        gates = []
        for r in halves:
            x = _unpack_bf16_pairs(x_ref[r * half:(r + 1) * half, :dh])
            gates.append((jnp.dot(x, wg_b[...], preferred_element_type=F32),
                          jnp.dot(x, wu_b[...], preferred_element_type=F32)))
        for r, (a, u) in zip(halves, gates):
            hmid = (a * jax.nn.sigmoid(a)) * u
            y = jnp.dot(hmid.astype(BF16), wd_b[...], preferred_element_type=F32)
            row_w = lax.bitcast_convert_type(x_ref[r * half:(r + 1) * half, dh:dh + 1], F32)
            o_ref[r * half:(r + 1) * half, :] = _pack_bf16_pairs(y * row_w)

    @pl.when(valid > half)
    def _():
        ffn_halves((0, 1))

    @pl.when((valid > 0) & (valid <= half))
    def _():
        ffn_halves((0,))
        o_ref[half:, :] = jnp.zeros((half, dh), o_ref.dtype)

    @pl.when(valid == 0)
    def _():
        o_ref[...] = jnp.zeros_like(o_ref)


def _expert_call(block_e, nused, valid, next_e, xb, wg, wu, wd, layer):
    p_rows = xb.shape[0]
    dh = xb.shape[1] - LANES
    d = 2 * dh
    m = MOE_BLOCK
    nb = p_rows // m
    de = wg.shape[-1]

    def xmap(j, be, nu, va, ne):
        return (jnp.minimum(j, nu[0] - 1), 0)

    any_spec = pl.BlockSpec(memory_space=pl.ANY)
    grid_spec = pltpu.PrefetchScalarGridSpec(
        num_scalar_prefetch=4,
        grid=(nb,),
        in_specs=[pl.BlockSpec((m, dh + LANES), xmap), any_spec, any_spec, any_spec],
        out_specs=pl.BlockSpec((m, dh), lambda j, be, nu, va, ne: (j, 0)),
        scratch_shapes=[
            pltpu.VMEM((d, de), BF16), pltpu.VMEM((d, de), BF16), pltpu.VMEM((de, d), BF16),
            pltpu.VMEM((2, d, de), F32), pltpu.VMEM((2, d, de), F32), pltpu.VMEM((2, de, d), F32),
            pltpu.SMEM((1,), I32), pltpu.SemaphoreType.DMA((2, 3)),
        ],
    )
    return pl.pallas_call(
        functools.partial(_expert_kernel, layer=layer),
        grid_spec=grid_spec,
        out_shape=jax.ShapeDtypeStruct((p_rows, dh), U32),
        compiler_params=_cparams(("arbitrary",)),
        name="experts",
    )(block_e, nused, valid, next_e, xb, wg, wu, wd)


def _combine_kernel(*refs, tile0, final):
    if final:
        seg_ref, pad_ref, yb_hbm, io_ref, x_ref, g2_ref, fn_ref, o_ref, stage, sem = refs
    else:
        seg_ref, pad_ref, yb_hbm, io_ref, x_ref, g2_ref, o_ref, stage, sem = refs
    i = pl.program_id(0)

    slot = i % 2

    @pl.when(i == 0)
    def _():
        stage[...] = jnp.zeros_like(stage)
        _segment_copies(seg_ref, pad_ref, tile0, stage.at[0], yb_hbm, sem.at[0], to_hbm=False, wait=False)

    @pl.when(i + 1 < pl.num_programs(0))
    def _():
        _segment_copies(seg_ref, pad_ref, tile0 + i + 1, stage.at[1 - slot], yb_hbm, sem.at[1 - slot],
                        to_hbm=False, wait=False)

    _segment_copies(seg_ref, pad_ref, tile0 + i, stage.at[slot], yb_hbm, sem.at[slot], to_hbm=False, wait=True)
    rows = _unpack_bf16_pairs(stage[slot])
    m0, m1 = _sort_matrix(io_ref, STAGE_ROWS)
    pick = jnp.where(m0 | m1, 1.0, 0.0).astype(BF16)
    y = lax.dot_general(pick, rows, (((0,), (0,)), ((), ())), preferred_element_type=F32)
    xo = x_ref[...] + g2_ref[...] * y
    if final:
        ms = jnp.mean(xo * xo, axis=-1, keepdims=True)
        xo = (xo * lax.rsqrt(ms + NORM_EPS)) * fn_ref[...]
    o_ref[...] = xo


def _combine_call(seg, pad, yb, io, x2d, mods5, layer, row_of_tile, final_g, tile0):
    n, d = x2d.shape
    tm = ROUTE_TILE
    final = final_g is not None
    in_specs = [
        pl.BlockSpec(memory_space=pl.ANY),
        pl.BlockSpec((SUBLANES, tm), lambda t, *_: (0, tile0 + t)),
        pl.BlockSpec((tm, d), lambda t, *_: (t, 0)),
        pl.BlockSpec((None, None, None, 1, d), lambda t, *_: (layer, row_of_tile(t), 5, 0, 0)),
    ]
    args = [yb, io, x2d, mods5]
    if final:
        in_specs.append(pl.BlockSpec((1, d), lambda t, *_: (0, 0)))
        args.append(final_g.reshape(1, d))
    grid_spec = pltpu.PrefetchScalarGridSpec(
        num_scalar_prefetch=2,
        grid=(n // tm,),
        in_specs=in_specs,
        out_specs=pl.BlockSpec((tm, d), lambda t, *_: (t, 0)),
        scratch_shapes=[pltpu.VMEM((2, STAGE_ROWS, d // 2), U32), pltpu.SemaphoreType.DMA((2,))],
    )
    return pl.pallas_call(
        functools.partial(_combine_kernel, tile0=tile0, final=final),
        grid_spec=grid_spec,
        out_shape=jax.ShapeDtypeStruct((n, d), F32),
        compiler_params=_cparams(("arbitrary",)),
        name="combine",
    )(seg, pad, *args)


def _pack_w_kernel(w_ref, o_ref):
    cut = C_Z + 2 * GLA_GATE_RANK
    o_ref[:, :C_Z] = w_ref[:, :C_Z].astype(BF16)
    gate_tile = w_ref[:, C_Z:C_Z + LANES]
    lane = lax.broadcasted_iota(I32, gate_tile.shape, 1)
    o_ref[:, C_Z:C_SQ] = jnp.where(lane < 2 * GLA_GATE_RANK, gate_tile, 0.0).astype(BF16)
    o_ref[:, C_SQ:] = w_ref[:, cut:].astype(BF16)


def _pack_w_in(w):
    assert sum(IN_SIZES[:6]) == C_Z + 2 * GLA_GATE_RANK and sum(IN_SIZES) - sum(IN_SIZES[:6]) == C_END - C_SQ
    depth, d, cols = w.shape
    rows = ROW_TILE // 2
    return pl.pallas_call(
        _pack_w_kernel,
        grid=(depth, d // rows),
        in_specs=[pl.BlockSpec((None, rows, cols), lambda l, r: (l, r, 0))],
        out_specs=pl.BlockSpec((None, rows, C_END), lambda l, r: (l, r, 0)),
        out_shape=jax.ShapeDtypeStruct((depth, d, C_END), BF16),
        compiler_params=_cparams(("arbitrary", "arbitrary")),
        name="pack_w_in",
    )(w)


def _pad_up(up, row0):
    up = up.reshape(GLA_GATE_RANK, GLA_HEADS // 2, LANES).transpose(1, 0, 2)
    hi = up.astype(BF16)
    lo = (up - hi.astype(F32)).astype(BF16)
    out = jnp.zeros((GLA_HEADS // 2, LANES, LANES), BF16)
    for group, part in enumerate((hi, hi, lo)):
        r = group * 2 * GLA_GATE_RANK + row0
        out = out.at[:, r:r + GLA_GATE_RANK, :].set(part)
    return out


def _rope_tables(seq):
    rows = seq // GRID_W
    dim = jnp.arange(LANES, dtype=I32) % SWA_HD
    inv = ROPE_THETA ** (-((dim % ROPE_HALF).astype(F32) * 2.0 / ROPE_AXIS_DIM))
    sign = jnp.where(dim % ROPE_AXIS_DIM < ROPE_HALF, -1.0, 1.0)
    by_row = (dim < ROPE_AXIS_DIM)[None, None, :]
    ang_r = (jnp.arange(rows, dtype=F32)[:, None] * inv[None, :])[:, None, :]
    ang_c = (jnp.arange(GRID_W, dtype=F32)[:, None] * inv[None, :])[None, :, :]
    cos = jnp.where(by_row, jnp.cos(ang_r), jnp.cos(ang_c))
    sin = jnp.where(by_row, jnp.sin(ang_r), jnp.sin(ang_c)) * sign
    return cos.reshape(seq, LANES), sin.reshape(seq, LANES)


def _tile(n, pref):
    t = pref
    while n % t:
        t //= 2
    return t


def kernel(x, c, ctx, c_ctx, w_ada, b_ada, norm1, norm2, w_in, gla_up_f, gla_bias_f, gla_up_b, gla_bias_b,
           gla_norm, swa_sink, w_out, w_router, b_router, w_gate, w_up, w_down, final_norm):
    batch, seq, d = x.shape
    lc = ctx.shape[1]
    depth = w_ada.shape[0]
    n_l = batch * seq
    n_c = batch * lc
    npair = GLA_HEADS // 2
    assert batch + 1 <= SUBLANES
    assert n_l % ROUTE_TILE == 0 and n_c % ROUTE_TILE == 0

    tm_l = _tile(seq, ROW_TILE)
    tm_in = _tile(seq, INPROJ_TILE)
    tm_c = _tile(lc, CTX_TILE)
    tl_l = _tile(seq, GLA_TILE)
    tl_c = _tile(lc, ROW_TILE)

    rows = jnp.zeros((SUBLANES, d), F32).at[:batch].set(c).at[batch].set(c_ctx)
    mods = _ada_call(rows, w_ada, b_ada)
    mods5 = mods.reshape(depth, SUBLANES, 6, 1, d)

    tabs = _rope_tables(seq)
    w_packed = _pack_w_in(w_in)
    w_out_b = w_out.astype(BF16)
    wr_pad = jnp.zeros((d, LANES), F32).at[:, :N_EXPERTS].set(w_router)
    wr_hi = wr_pad.astype(BF16)
    wr_lo = (wr_pad - wr_hi.astype(F32)).astype(BF16)
    wr_cat = jnp.concatenate([wr_hi, wr_lo], axis=-1)

    def lat_row(tm):
        return lambda t: t // (seq // tm)

    def ctx_row(tm):
        return lambda t: batch

    xl = x.reshape(n_l, d)
    xc = ctx.reshape(n_c, d)
    for i in range(depth):
        last = i == depth - 1
        upf = _pad_up(gla_up_f[i], 0)
        upb = _pad_up(gla_up_b[i], GLA_GATE_RANK)
        bias_f = gla_bias_f[i].reshape(npair, 1, LANES)
        bias_b = gla_bias_b[i].reshape(npair, 1, LANES)

        c_qk, c_v, c_g, c_z, c_sq, c_sk, c_sv = _inproj_call(
            xc, norm1[i], mods5, i, ctx_row(tm_c), w_packed, None, tm_c, lc // tm_c)
        l_qk, l_v, l_g, l_z, l_sq, l_sk, l_sv = _inproj_call(
            xl, norm1[i], mods5, i, lat_row(tm_in), w_packed, tabs, tm_in, seq // tm_in)

        s_zero = jnp.zeros((batch, npair, 2, 2 * GLA_DV, LANES), F32)
        oc_f, oc_b, s_ctx = _gla_call(c_qk, c_v, c_z, upf, upb, bias_f, bias_b, s_zero, batch, lc, tl_c)
        ol_f, ol_b, _ = _gla_call(l_qk, l_v, l_z, upf, upb, bias_f, bias_b, s_ctx, batch, seq, tl_l)

        swa_l = _swa_call(l_sq, l_sk, l_sv, c_sk, c_sv, swa_sink[i], batch, seq, lc)
        xl, h2l, lg_l = _outproj_call(ol_f, ol_b, l_g, gla_norm[i], swa_l, w_out_b, xl, mods5, i, lat_row(tm_l),
                                      norm2[i], wr_cat, tm_l)
        if last:
            logits_t = lg_l
            t_all = n_l
        else:
            swa_c = _swa_ctx_call(c_sq, c_sk, c_sv, swa_sink[i], batch, lc)
            xc, h2c, lg_c = _outproj_call(oc_f, oc_b, c_g, gla_norm[i], swa_c, w_out_b, xc, mods5, i, ctx_row(tm_c),
                                          norm2[i], wr_cat, tm_c)
            logits_t = jnp.concatenate([lg_l, lg_c], axis=1)
            t_all = n_l + n_c

        io, wo, seg, pad, blk = _route_call(logits_t, b_router, ROUTE_TILE)
        m = MOE_BLOCK
        padded = blk[:, 0]
        pend = jnp.cumsum(padded)
        ntile = t_all // ROUTE_TILE
        nb = -(-(t_all * TOP_K + ntile * N_EXPERTS * (SUBLANES - 1)) // m) + N_EXPERTS
        p_rows = nb * m
        seg = seg.reshape(-1)
        pad = pad.reshape(-1)
        nused = (pend[-1:] // m).astype(I32)
        tail = nused + jnp.arange(nb - (t_all * TOP_K) // m, dtype=I32)
        zblk = jnp.concatenate([jnp.where(padded > 0, pend // m - 1, -1),
                                jnp.where(tail < nb, tail, -1)]).astype(I32)
        blk_row = jnp.minimum(jnp.arange(nb, dtype=I32), nused - 1) * m
        block_e = jnp.sum((pend[None, :] <= blk_row[:, None]).astype(I32), axis=1)
        rows_e = jnp.sum(pad.reshape(N_EXPERTS, LANES), axis=1)
        own = (block_e[:, None] == jnp.arange(N_EXPERTS, dtype=I32)[None, :]).astype(I32)
        seg_end = jnp.sum(own * (pend - padded + rows_e)[None, :], axis=1)
        blk_idx = jnp.arange(nb, dtype=I32)
        valid = jnp.where(blk_idx < nused, jnp.clip(seg_end - blk_idx * m, 0, m), 0).astype(I32)

        sources = (h2l,) if last else (h2l, h2c)
        xb = _dispatch_call(seg, pad, zblk, io, wo, sources, p_rows)
        eid = jnp.arange(N_EXPERTS, dtype=I32)
        later = (eid[None, :] > eid[:, None]) & (padded[None, :] > 0)
        next_e = jnp.min(jnp.where(later, eid[None, :], N_EXPERTS), axis=1)
        next_e = jnp.where(next_e < N_EXPERTS, next_e, -1).astype(I32)
        yb = _expert_call(block_e, nused, valid, next_e, xb, w_gate, w_up, w_down, i)
        xl_new = _combine_call(seg, pad, yb, io, xl, mods5, i, lat_row(ROUTE_TILE),
                               final_norm if last else None, 0)
        if not last:
            xc = _combine_call(seg, pad, yb, io, xc, mods5, i, ctx_row(0), None, n_l // ROUTE_TILE)
        xl = xl_new
    return xl.reshape(batch, seq, d)
```

```python
import functools

import jax
import jax.numpy as jnp
from jax import lax
from jax.experimental import pallas as pl
from jax.experimental.pallas import tpu as pltpu

F32 = jnp.float32
BF16 = jnp.bfloat16
I32 = jnp.int32
U32 = jnp.uint32

GRID_W = 64
NORM_EPS = 1e-6
GLA_HEADS = 4
GLA_DK = 64
GLA_DV = 128
GLA_KW = GLA_HEADS * GLA_DK
GLA_WIDTH = GLA_HEADS * GLA_DV
GLA_GATE_RANK = 16
GLA_TAU = 16.0
GLA_CHUNK = 64
SWA_HEADS = 8
SWA_KV_HEADS = 2
SWA_HD = 64
SWA_WIDTH = SWA_HEADS * SWA_HD
SWA_KVW = SWA_KV_HEADS * SWA_HD
WINDOW = 128
ATT_BLOCK = 128
ROPE_THETA = 10000.0
ROPE_AXIS_DIM = SWA_HD // 2
ROPE_HALF = ROPE_AXIS_DIM // 2
N_EXPERTS = 16
N_GROUPS = 4
EXPERTS_PER_GROUP = N_EXPERTS // N_GROUPS
TOP_K = 2
IN_SIZES = (GLA_KW, GLA_KW, GLA_WIDTH, GLA_WIDTH, GLA_GATE_RANK, GLA_GATE_RANK, SWA_WIDTH, SWA_KVW, SWA_KVW)

LANES = 128
SUBLANES = 8
VMEM_LIMIT = 48 * 1024 * 1024

ROW_TILE = 512
GLA_TILE = 1024
GLA_SUM_ROWS = 256
INPROJ_TILE = 1024
CTX_TILE = 256
ADA_COLS = 1536
SWA_BLOCKS_PER_STEP = 8
MOE_BLOCK = 512
ROUTE_TILE = 512
ROUTE_TILES_PER_STEP = 11
SEG_SIZES = (512, 256, 128, 64, 32, 16, 8)
STAGE_ROWS = TOP_K * ROUTE_TILE + LANES

C_QK = 0
C_V = 512
C_G = 1024
C_Z = 1536
C_SQ = 1664
C_SK = 2176
C_SV = 2304
C_END = 2432
SV_LANES = 4 * LANES
NEG_BIG = -1e30
LOG2E = 1.4426950408889634

assert WINDOW == ATT_BLOCK


def _cparams(sem):
    return pltpu.CompilerParams(dimension_semantics=sem, vmem_limit_bytes=VMEM_LIMIT)


def _pack_bf16_pairs(x):
    return _pack_rounded_pairs(x.astype(BF16).astype(F32))


def _pack_rounded_pairs(xr):
    n = xr.shape[1] // 2
    lo = lax.bitcast_convert_type(xr[:, :n], U32)
    hi = lax.bitcast_convert_type(xr[:, n:], U32)
    return (lo >> 16) | hi


def _unpack_bf16_pairs(w):
    lo = lax.bitcast_convert_type(w << 16, F32)
    hi = lax.bitcast_convert_type(w & jnp.uint32(0xFFFF0000), F32)
    return jnp.concatenate([lo.astype(BF16), hi.astype(BF16)], axis=-1)


def _ada_kernel(a_ref, w_ref, b_ref, o_ref):
    a = a_ref[...]
    act = a * jax.nn.sigmoid(a)
    a_hi = act.astype(BF16)
    a_lo = (act - a_hi.astype(F32)).astype(BF16)
    w = w_ref[...]
    w_hi = w.astype(BF16)
    w_lo = (w - w_hi.astype(F32)).astype(BF16)
    t = jnp.dot(jnp.concatenate([a_hi, a_lo], axis=0), w_hi, preferred_element_type=F32)
    n = a.shape[0]
    o_ref[...] = t[:n] + t[n:] + jnp.dot(a_hi, w_lo, preferred_element_type=F32) + b_ref[...]


def _ada_call(rows, w_ada, b_ada):
    depth, d, n6 = w_ada.shape
    tn = ADA_COLS
    return pl.pallas_call(
        _ada_kernel,
        grid=(depth, n6 // tn),
        in_specs=[
            pl.BlockSpec((SUBLANES, d), lambda l, j: (0, 0)),
            pl.BlockSpec((None, d, tn), lambda l, j: (l, 0, j)),
            pl.BlockSpec((None, 1, tn), lambda l, j: (l, 0, j)),
        ],
        out_specs=pl.BlockSpec((None, SUBLANES, tn), lambda l, j: (l, 0, j)),
        out_shape=jax.ShapeDtypeStruct((depth, SUBLANES, n6), F32),
        compiler_params=_cparams(("arbitrary", "arbitrary")),
        name="adaln",
    )(rows, w_ada, b_ada.reshape(depth, 1, n6))


def _rope_tile(xj, cos, sin, lane_lo):
    partner = jnp.where(lane_lo, pltpu.roll(xj, LANES - ROPE_HALF, 1), pltpu.roll(xj, ROPE_HALF, 1))
    return xj * cos + partner * sin


def _inproj_kernel(*refs, rope):
    if rope:
        (x_ref, g_ref, sc_ref, sh_ref, w_ref, cos_ref, sin_ref,
         qk_ref, v_ref, gate_ref, z_ref, sq_ref, sk_ref, sv_ref) = refs
    else:
        (x_ref, g_ref, sc_ref, sh_ref, w_ref,
         qk_ref, v_ref, gate_ref, z_ref, sq_ref, sk_ref, sv_ref) = refs
    half = x_ref.shape[0] // 2
    scale = g_ref[...] * (1.0 + sc_ref[...])
    normed = []
    for r in range(2):
        x = x_ref[r * half:(r + 1) * half, :]
        ms = jnp.mean(x * x, axis=-1, keepdims=True)
        normed.append(((x * lax.rsqrt(ms + NORM_EPS)) * scale + sh_ref[...]).astype(BF16))

    for r in range(2):
        rows = slice(r * half, (r + 1) * half)
        hb = normed[r]

        def proj(a, b):
            return jnp.dot(hb, w_ref[:, a:b], preferred_element_type=F32)

        qk = proj(C_QK, C_V)
        qk_ref[rows, :GLA_KW] = (qk[:, :GLA_KW] * (GLA_DK ** -0.5)).astype(qk_ref.dtype)
        qk_ref[rows, GLA_KW:] = qk[:, GLA_KW:].astype(qk_ref.dtype)
        v_ref[rows, :] = proj(C_V, C_G).astype(v_ref.dtype)
        gate_ref[rows, :] = proj(C_G, C_Z).astype(gate_ref.dtype)
        z_ref[rows, :] = proj(C_Z, C_SQ)
        sq = proj(C_SQ, C_SK) * (SWA_HD ** -0.5 * LOG2E)
        sk = proj(C_SK, C_SV)
        sv = proj(C_SV, C_END)
        if rope:
            cos = cos_ref[rows, :]
            sin = sin_ref[rows, :]
            lane_lo = (lax.broadcasted_iota(I32, cos.shape, 1) % ROPE_AXIS_DIM) < ROPE_HALF
            for j in range(SWA_WIDTH // LANES):
                sl = slice(j * LANES, (j + 1) * LANES)
                sq_ref[rows, sl] = _rope_tile(sq[:, sl], cos, sin, lane_lo).astype(sq_ref.dtype)
            sk = _rope_tile(sk, cos, sin, lane_lo)
        else:
            sq_ref[rows, :] = sq.astype(sq_ref.dtype)
        first = lax.broadcasted_iota(I32, sk.shape, 1) < SWA_HD
        sk_sw = pltpu.roll(sk, SWA_HD, 1)
        sk_ref[rows, :LANES] = jnp.where(first, sk, sk_sw).astype(sk_ref.dtype)
        sk_ref[rows, LANES:] = jnp.where(first, sk_sw, sk).astype(sk_ref.dtype)
        sv_sw = pltpu.roll(sv, SWA_HD, 1)
        sv_ref[rows, 0 * LANES:1 * LANES] = jnp.where(first, sv, 1.0).astype(sv_ref.dtype)
        sv_ref[rows, 1 * LANES:2 * LANES] = jnp.where(first, 1.0, sv_sw).astype(sv_ref.dtype)
        sv_ref[rows, 2 * LANES:3 * LANES] = jnp.where(first, sv_sw, 1.0).astype(sv_ref.dtype)
        sv_ref[rows, 3 * LANES:4 * LANES] = jnp.where(first, 1.0, sv).astype(sv_ref.dtype)


def _inproj_call(x2d, g, mods5, layer, row_of_tile, w_packed, tabs, tm, tiles_per_seq):
    n, d = x2d.shape
    nt = n // tm
    rope = tabs is not None

    def mod_spec(k):
        return pl.BlockSpec((None, None, None, 1, d), lambda t: (layer, row_of_tile(t), k, 0, 0))

    in_specs = [
        pl.BlockSpec((tm, d), lambda t: (t, 0)),
        pl.BlockSpec((1, d), lambda t: (0, 0)),
        mod_spec(1), mod_spec(0),
        pl.BlockSpec((None, d, C_END), lambda t: (layer, 0, 0)),
    ]
    args = [x2d, g.reshape(1, d), mods5, mods5, w_packed]
    if rope:
        in_specs += [pl.BlockSpec((tm, LANES), lambda t: (t % tiles_per_seq, 0))] * 2
        args += [tabs[0], tabs[1]]
    widths = (2 * GLA_KW, GLA_WIDTH, GLA_WIDTH, LANES, SWA_WIDTH, 2 * SWA_KVW, SV_LANES)
    dtypes = (BF16, BF16, BF16, F32, BF16, BF16, BF16)
    return pl.pallas_call(
        functools.partial(_inproj_kernel, rope=rope),
        grid=(nt,),
        in_specs=in_specs,
        out_specs=[pl.BlockSpec((tm, w), lambda t: (t, 0)) for w in widths],
        out_shape=[jax.ShapeDtypeStruct((n, w), dt) for w, dt in zip(widths, dtypes)],
        compiler_params=_cparams(("arbitrary",)),
        name="inproj",
    )(*args)


def _log_sigmoid(x):
    return jnp.minimum(x, 0.0) - jnp.log(1.0 + jnp.exp(-jnp.abs(x)))


def _gla_prepare(q_ref, k_ref, v_ref, z_ref, up_ref, bias_ref, tri_ref, scr, fwd):
    qin_s, km0_s, km1_s, kout_s, vm0_s, vm1_s, dec_s = scr[:7]
    qbd_s = scr[9]
    tl = q_ref.shape[0]
    c = GLA_CHUNK
    z = z_ref[...]
    z_hi = z.astype(BF16).astype(F32)
    zc = z_hi + pltpu.roll(z - z_hi, 2 * GLA_GATE_RANK, 1) + pltpu.roll(z_hi, 4 * GLA_GATE_RANK, 1)
    x = jnp.dot(zc.astype(BF16), up_ref[...], preferred_element_type=F32) + bias_ref[...]
    la = _log_sigmoid(x) * (LOG2E / GLA_TAU)
    hi = la.astype(BF16)
    lo = (la - hi.astype(F32)).astype(BF16)
    hl = jnp.concatenate([hi, lo], axis=1)
    tg = tri_ref.shape[0]
    sums = jnp.concatenate([jnp.dot(tri_ref[...], hl[r0:r0 + tg, :], preferred_element_type=F32)
                            for r0 in range(0, tl, tg)], axis=0)
    b = sums[:, :LANES] + sums[:, LANES:]
    b3 = b.reshape(tl // c, c, LANES)
    edge = b3[:, c - 1:c, :] if fwd else b3[:, 0:1, :]
    btot = jnp.broadcast_to(edge, b3.shape).reshape(tl, LANES)
    q = q_ref[...].astype(F32)
    k = k_ref[...].astype(F32)
    first = (lax.broadcasted_iota(I32, (tl, LANES), 0) % (2 * c)) < c
    q_in = q * jnp.exp2(b)
    qin_s[...] = q_in.astype(BF16)
    qbd_s[...] = jnp.concatenate([jnp.where(first, q_in, 0.0), jnp.where(first, 0.0, q_in)], axis=1).astype(BF16)
    k_in = k * jnp.exp2(-b)
    head0 = lax.broadcasted_iota(I32, (tl, LANES), 1) < GLA_DK
    km0_s[...] = jnp.where(head0, k_in, 0.0).astype(BF16)
    km1_s[...] = jnp.where(head0, 0.0, k_in).astype(BF16)
    k_out = k * jnp.exp2(btot - b)
    kout_s[...] = jnp.concatenate([jnp.where(first, k_out, 0.0), jnp.where(first, 0.0, k_out)], axis=1).astype(BF16)
    dec_s[...] = jnp.exp2(btot)
    vf = v_ref[...].astype(F32)
    vhead0 = lax.broadcasted_iota(I32, (tl, 2 * GLA_DV), 1) < GLA_DV
    vm0_s[...] = jnp.where(vhead0, vf, 0.0).astype(BF16)
    vm1_s[...] = jnp.where(vhead0, 0.0, vf).astype(BF16)


def _gla_increment(v_ref, scr, pair):
    kout_s, u_s = scr[3], scr[7]
    rows = pl.ds(pair * 2 * GLA_CHUNK, 2 * GLA_CHUNK)
    u_t = lax.dot_general(v_ref[rows, :], kout_s[rows, :], (((0,), (0,)), ((), ())),
                          preferred_element_type=F32)
    srow = lax.broadcasted_iota(I32, (2 * GLA_DV, 2 * LANES), 0) // GLA_DV
    scol = (lax.broadcasted_iota(I32, (2 * GLA_DV, 2 * LANES), 1) % LANES) // GLA_DK
    u_s[pair] = jnp.where(srow == scol, u_t, 0.0)


def _gla_states(scr, s_ref, chunk_order):
    dec_s, u_s, sprev_s = scr[6], scr[7], scr[8]
    s_t = s_ref[...]
    for cidx in chunk_order:
        pair, lanes = cidx // 2, pl.ds((cidx % 2) * LANES, LANES)
        sprev_s[pair, :, lanes] = s_t.astype(BF16)
        r0 = cidx * GLA_CHUNK
        s_t = s_t * dec_s[r0:r0 + 1, :] + u_s[pair, :, lanes]
    s_ref[...] = s_t


def _gla_output(o_ref, scr, pair, fwd):
    qin_s, km0_s, km1_s, _, vm0_s, vm1_s, _, _, sprev_s, qbd_s = scr
    c = GLA_CHUNK
    r0 = pair * 2 * c
    rows = pl.ds(r0, 2 * c)
    ca, cb = pl.ds(r0, c), pl.ds(r0 + c, c)
    kst = jnp.concatenate([km0_s[ca, :], km1_s[ca, :], km0_s[cb, :], km1_s[cb, :]], axis=0)
    a = lax.dot_general(qin_s[rows, :], kst, (((1,), (1,)), ((), ())), preferred_element_type=F32)
    ri = lax.broadcasted_iota(I32, (2 * c, 4 * c), 0)
    ci = lax.broadcasted_iota(I32, (2 * c, 4 * c), 1)
    same_chunk = (ri // c) == (ci // (2 * c))
    keep = same_chunk & ((ci % c <= ri % c) if fwd else (ci % c >= ri % c))
    a = jnp.where(keep, a, 0.0).astype(BF16)
    vbd = jnp.concatenate([vm0_s[ca, :], vm1_s[ca, :], vm0_s[cb, :], vm1_s[cb, :]], axis=0)
    o = jnp.dot(a, vbd, preferred_element_type=F32)
    o = o + lax.dot_general(qbd_s[rows, :], sprev_s[pair], (((1,), (1,)), ((), ())), preferred_element_type=F32)
    o_ref[rows, :] = o.astype(o_ref.dtype)


def _gla_kernel(qf_ref, kf_ref, vf_ref, zf_ref, qb_ref, kb_ref, vb_ref, zb_ref,
                upf_ref, upb_ref, bf_ref, bb_ref, trif_ref, trib_ref, s0_ref,
                of_ref, ob_ref, sfin_ref, sf_scr, sb_scr, *scr, nchunk):
    i = pl.program_id(2)
    nt = pl.num_programs(2)
    scr_f, scr_b = scr[:len(scr) // 2], scr[len(scr) // 2:]

    @pl.when(i == 0)
    def _():
        sf_scr[...] = s0_ref[0]
        sb_scr[...] = s0_ref[1]

    _gla_prepare(qf_ref, kf_ref, vf_ref, zf_ref, upf_ref, bf_ref, trif_ref, scr_f, True)
    _gla_prepare(qb_ref, kb_ref, vb_ref, zb_ref, upb_ref, bb_ref, trib_ref, scr_b, False)
    for pair in range(nchunk // 2):
        _gla_increment(vf_ref, scr_f, pair)
        _gla_increment(vb_ref, scr_b, pair)
    _gla_states(scr_f, sf_scr, range(nchunk))
    _gla_states(scr_b, sb_scr, range(nchunk - 1, -1, -1))
    for pair in range(nchunk // 2):
        _gla_output(of_ref, scr_f, pair, True)
        _gla_output(ob_ref, scr_b, pair, False)

    @pl.when(i == nt - 1)
    def _():
        sfin_ref[0] = sf_scr[...]
        sfin_ref[1] = sb_scr[...]


def _gla_call(qk, v, z, upf_pad, upb_pad, bias_f, bias_b, s0, batch, seq, tl):
    n = qk.shape[0]
    nt = seq // tl
    npair = GLA_HEADS // 2

    def fwd_row(b, p, i):
        return b * nt + i

    def bwd_row(b, p, i):
        return b * nt + (nt - 1 - i)

    def specs(row):
        return [
            pl.BlockSpec((tl, LANES), lambda b, p, i: (row(b, p, i), p)),
            pl.BlockSpec((tl, LANES), lambda b, p, i: (row(b, p, i), npair + p)),
            pl.BlockSpec((tl, 2 * GLA_DV), lambda b, p, i: (row(b, p, i), p)),
            pl.BlockSpec((tl, LANES), lambda b, p, i: (row(b, p, i), 0)),
        ]

    in_specs = specs(fwd_row) + specs(bwd_row) + [
        pl.BlockSpec((None, LANES, LANES), lambda b, p, i: (p, 0, 0)),
        pl.BlockSpec((None, LANES, LANES), lambda b, p, i: (p, 0, 0)),
        pl.BlockSpec((None, 1, LANES), lambda b, p, i: (p, 0, 0)),
        pl.BlockSpec((None, 1, LANES), lambda b, p, i: (p, 0, 0)),
        pl.BlockSpec((None, GLA_SUM_ROWS, GLA_SUM_ROWS), lambda b, p, i: (0, 0, 0)),
        pl.BlockSpec((None, GLA_SUM_ROWS, GLA_SUM_ROWS), lambda b, p, i: (1, 0, 0)),
        pl.BlockSpec((None, None, 2, 2 * GLA_DV, LANES), lambda b, p, i: (b, p, 0, 0, 0)),
    ]
    assert tl % GLA_SUM_ROWS == 0
    ri = jnp.arange(GLA_SUM_ROWS)[:, None]
    ci = jnp.arange(GLA_SUM_ROWS)[None, :]
    same = (ri // GLA_CHUNK) == (ci // GLA_CHUNK)
    tri = jnp.stack([same & (ci <= ri), same & (ci >= ri)]).astype(BF16)
    npairs = tl // (2 * GLA_CHUNK)
    dir_scratch = [
        pltpu.VMEM((tl, LANES), BF16),
        pltpu.VMEM((tl, LANES), BF16),
        pltpu.VMEM((tl, LANES), BF16),
        pltpu.VMEM((tl, 2 * LANES), BF16),
        pltpu.VMEM((tl, 2 * GLA_DV), BF16),
        pltpu.VMEM((tl, 2 * GLA_DV), BF16),
        pltpu.VMEM((tl, LANES), F32),
        pltpu.VMEM((npairs, 2 * GLA_DV, 2 * LANES), F32),
        pltpu.VMEM((npairs, 2 * GLA_DV, 2 * LANES), BF16),
        pltpu.VMEM((tl, 2 * LANES), BF16),
    ]
    out_specs = [
        pl.BlockSpec((tl, 2 * GLA_DV), lambda b, p, i: (fwd_row(b, p, i), p)),
        pl.BlockSpec((tl, 2 * GLA_DV), lambda b, p, i: (bwd_row(b, p, i), p)),
        pl.BlockSpec((None, None, 2, 2 * GLA_DV, LANES), lambda b, p, i: (b, p, 0, 0, 0)),
    ]
    out_shape = [
        jax.ShapeDtypeStruct((n, GLA_WIDTH), BF16),
        jax.ShapeDtypeStruct((n, GLA_WIDTH), BF16),
        jax.ShapeDtypeStruct((batch, npair, 2, 2 * GLA_DV, LANES), F32),
    ]
    return pl.pallas_call(
        functools.partial(_gla_kernel, nchunk=tl // GLA_CHUNK),
        grid=(batch, npair, nt),
        in_specs=in_specs,
        out_specs=out_specs,
        out_shape=out_shape,
        scratch_shapes=[pltpu.VMEM((2 * GLA_DV, LANES), F32), pltpu.VMEM((2 * GLA_DV, LANES), F32)]
        + dir_scratch + dir_scratch,
        compiler_params=_cparams(("arbitrary", "arbitrary", "arbitrary")),
        name="gla",
    )(qk, qk, v, z, qk, qk, v, z, upf_pad, upb_pad, bias_f, bias_b, tri, tri, s0)


def _swa_block(sink_ref, q_ref, o_ref, row0, kall, vall, prev_ok, next_ok):
    w = ATT_BLOCK
    local = prev_ok is not None
    qrows = pl.ds(row0, w)
    r2 = lax.broadcasted_iota(I32, (2 * w, w), 0) % w
    c2 = lax.broadcasted_iota(I32, (2 * w, w), 1)
    if local:
        bias_prev = jnp.where((c2 >= r2) & prev_ok, 0.0, NEG_BIG)
        bias_next = jnp.where((c2 <= r2) & next_ok, 0.0, NEG_BIG)
    first = c2 < SWA_HD
    top = lax.broadcasted_iota(I32, (2 * w, 1), 0) < w
    scores = []
    for g in range(SWA_KV_HEADS):
        kd = kall[:, g * LANES:(g + 1) * LANES]
        qs = jnp.concatenate([q_ref[qrows, (2 * g) * LANES:(2 * g + 1) * LANES],
                              q_ref[qrows, (2 * g + 1) * LANES:(2 * g + 2) * LANES]], axis=0).astype(F32)
        for half in range(2):
            qm = (jnp.where(first, qs, 0.0) if half == 0 else jnp.where(first, 0.0, qs)).astype(BF16)
            s = lax.dot_general(qm, kd, (((1,), (1,)), ((), ())), preferred_element_type=F32)
            if local:
                s = jnp.concatenate([s[:, :w] + bias_prev, s[:, w:2 * w], s[:, 2 * w:3 * w] + bias_next,
                                     s[:, 3 * w:]], axis=1)
            scores.append(s)
    for g in range(SWA_KV_HEADS):
        outs = []
        for half in range(2):
            s = scores[2 * g + half]
            sk = jnp.where(top, sink_ref[4 * g + half], sink_ref[4 * g + 2 + half]) * LOG2E
            m = jnp.maximum(jnp.max(s, axis=-1, keepdims=True), sk)
            p = jnp.exp2((s - m).astype(BF16))
            va = vall[:, (2 * g + half) * LANES:(2 * g + half + 1) * LANES]
            acc = jnp.dot(p, va, preferred_element_type=F32)
            den = pltpu.roll(acc, SWA_HD, 1) + jnp.exp2(sk - m)
            outs.append(acc / den)
        out = jnp.where(first, outs[0], outs[1])
        o_ref[qrows, (2 * g) * LANES:(2 * g + 1) * LANES] = out[:w].astype(o_ref.dtype)
        o_ref[qrows, (2 * g + 1) * LANES:(2 * g + 2) * LANES] = out[w:].astype(o_ref.dtype)


def _swa_kernel(*refs, nstep, local):
    w = ATT_BLOCK
    if not local:
        sink_ref, q_ref, kx_ref, vx_ref, o_ref = refs
        _swa_block(sink_ref, q_ref, o_ref, 0, kx_ref[...], vx_ref[...], None, None)
        return
    sink_ref, q_ref, kp_ref, kc_ref, kn_ref, vp_ref, vc_ref, vn_ref, kx_ref, vx_ref, o_ref = refs
    i = pl.program_id(1)
    nblk = q_ref.shape[0] // w
    k_blocks = [kp_ref[...]] + [kc_ref[j * w:(j + 1) * w, :] for j in range(nblk)] + [kn_ref[...]]
    v_blocks = [vp_ref[...]] + [vc_ref[j * w:(j + 1) * w, :] for j in range(nblk)] + [vn_ref[...]]
    exists = [i > 0] + [True] * nblk + [i < nstep - 1]
    for sub in range(nblk):
        kall = jnp.concatenate(k_blocks[sub:sub + 3] + [kx_ref[...]], axis=0)
        vall = jnp.concatenate(v_blocks[sub:sub + 3] + [vx_ref[...]], axis=0)
        _swa_block(sink_ref, q_ref, o_ref, sub * w, kall, vall, exists[sub], exists[sub + 2])


def _swa_call(sq, skd, svd, kcd, vcd, sink, batch, seq, lc):
    n = sq.shape[0]
    w = ATT_BLOCK
    nb = seq // w
    nblk = SWA_BLOCKS_PER_STEP if nb % SWA_BLOCKS_PER_STEP == 0 else 1
    nstep = nb // nblk
    kvw = 2 * SWA_KVW

    def pair(b, i):
        return (b * nstep + i, 0)

    def before(b, i):
        return (b * nb + jnp.maximum(nblk * i - 1, 0), 0)

    def after(b, i):
        return (b * nb + jnp.minimum(nblk * (i + 1), nb - 1), 0)

    def kv_specs(width):
        return [pl.BlockSpec((w, width), before), pl.BlockSpec((nblk * w, width), pair),
                pl.BlockSpec((w, width), after)]

    def ctx_spec(width):
        return pl.BlockSpec((lc, width), lambda b, i: (b, 0))

    return pl.pallas_call(
        functools.partial(_swa_kernel, nstep=nstep, local=True),
        grid=(batch, nstep),
        in_specs=[pl.BlockSpec(memory_space=pltpu.SMEM), pl.BlockSpec((nblk * w, SWA_WIDTH), pair)]
        + kv_specs(kvw) + kv_specs(SV_LANES) + [ctx_spec(kvw), ctx_spec(SV_LANES)],
        out_specs=pl.BlockSpec((nblk * w, SWA_WIDTH), pair),
        out_shape=jax.ShapeDtypeStruct((n, SWA_WIDTH), BF16),
        compiler_params=_cparams(("arbitrary", "arbitrary")),
        name="swa",
    )(sink, sq, skd, skd, skd, svd, svd, svd, kcd, vcd)


def _swa_ctx_call(sq, kcd, vcd, sink, batch, lc):
    n = sq.shape[0]
    w = ATT_BLOCK
    nb = lc // w
    kvw = 2 * SWA_KVW
    return pl.pallas_call(
        functools.partial(_swa_kernel, nstep=0, local=False),
        grid=(batch, nb),
        in_specs=[pl.BlockSpec(memory_space=pltpu.SMEM),
                  pl.BlockSpec((w, SWA_WIDTH), lambda b, i: (b * nb + i, 0)),
                  pl.BlockSpec((lc, kvw), lambda b, i: (b, 0)),
                  pl.BlockSpec((lc, SV_LANES), lambda b, i: (b, 0))],
        out_specs=pl.BlockSpec((w, SWA_WIDTH), lambda b, i: (b * nb + i, 0)),
        out_shape=jax.ShapeDtypeStruct((n, SWA_WIDTH), BF16),
        compiler_params=_cparams(("arbitrary", "arbitrary")),
        name="swa_ctx",
    )(sink, sq, kcd, vcd)


def _outproj_kernel(of_ref, ob_ref, gate_ref, gn_ref, swa_ref, w_ref, x_ref, g1_ref, n2_ref, sc_ref, sh_ref,
                    wr_ref, xo_ref, h2_ref, lg_ref):
    o = of_ref[...].astype(F32) + ob_ref[...].astype(F32)
    parts = []
    for h in range(GLA_HEADS):
        oh = o[:, h * GLA_DV:(h + 1) * GLA_DV]
        ms = jnp.mean(oh * oh, axis=-1, keepdims=True)
        parts.append(oh * lax.rsqrt(ms + NORM_EPS))
    on = jnp.concatenate(parts, axis=-1) * gn_ref[...]
    gate = gate_ref[...].astype(F32)
    gla = on * (gate * jax.nn.sigmoid(gate))
    mix = jnp.concatenate([gla.astype(BF16), swa_ref[...]], axis=-1)
    y = jnp.dot(mix, w_ref[...], preferred_element_type=F32)
    xo = x_ref[...] + g1_ref[...] * y
    xo_ref[...] = xo
    ms = jnp.mean(xo * xo, axis=-1, keepdims=True)
    h2 = (xo * lax.rsqrt(ms + NORM_EPS)) * (n2_ref[...] * (1.0 + sc_ref[...])) + sh_ref[...]
    hi = h2.astype(BF16)
    hi_f = hi.astype(F32)
    h2_ref[...] = _pack_rounded_pairs(hi_f)
    lo = (h2 - hi_f).astype(BF16)
    both = jnp.dot(hi, wr_ref[...], preferred_element_type=F32)
    lg = both[:, :LANES] + both[:, LANES:] + jnp.dot(lo, wr_ref[:, :LANES], preferred_element_type=F32)
    lg_ref[...] = jnp.transpose(lg)[:N_EXPERTS, :]


def _outproj_call(o_f, o_b, gate, gn, swa, w_out_b, x2d, mods5, layer, row_of_tile, n2, wr_cat, tm):
    n, d = x2d.shape
    nt = n // tm

    def mod_spec(k):
        return pl.BlockSpec((None, None, None, 1, d), lambda t: (layer, row_of_tile(t), k, 0, 0))

    return pl.pallas_call(
        _outproj_kernel,
        grid=(nt,),
        in_specs=[
            pl.BlockSpec((tm, GLA_WIDTH), lambda t: (t, 0)),
            pl.BlockSpec((tm, GLA_WIDTH), lambda t: (t, 0)),
            pl.BlockSpec((tm, GLA_WIDTH), lambda t: (t, 0)),
            pl.BlockSpec((1, GLA_WIDTH), lambda t: (0, 0)),
            pl.BlockSpec((tm, SWA_WIDTH), lambda t: (t, 0)),
            pl.BlockSpec((None, d, d), lambda t: (layer, 0, 0)),
            pl.BlockSpec((tm, d), lambda t: (t, 0)),
            mod_spec(2),
            pl.BlockSpec((1, d), lambda t: (0, 0)),
            mod_spec(4), mod_spec(3),
            pl.BlockSpec((d, 2 * LANES), lambda t: (0, 0)),
        ],
        out_specs=[
            pl.BlockSpec((tm, d), lambda t: (t, 0)),
            pl.BlockSpec((tm, d // 2), lambda t: (t, 0)),
            pl.BlockSpec((N_EXPERTS, tm), lambda t: (0, t)),
        ],
        out_shape=[
            jax.ShapeDtypeStruct((n, d), F32),
            jax.ShapeDtypeStruct((n, d // 2), U32),
            jax.ShapeDtypeStruct((N_EXPERTS, n), F32),
        ],
        compiler_params=_cparams(("arbitrary",)),
        name="outproj",
    )(o_f, o_b, gate, gn.reshape(1, GLA_WIDTH), swa, w_out_b, x2d, mods5, n2.reshape(1, d), mods5, mods5, wr_cat)


def _first_index(vals, target):
    idx = jnp.full(target.shape, len(vals) - 1, I32)
    for i in range(len(vals) - 2, -1, -1):
        idx = jnp.where(vals[i] == target, i, idx)
    return idx


def _route_kernel(lg_ref, br_ref, io_ref, wo_ref, seg_ref, pad_ref, blk_ref, cnt_scr, seg_scr, pad_scr, *, tn):
    phase = pl.program_id(0)
    step = pl.program_id(1)
    group = lg_ref.shape[1] // tn

    @pl.when((phase == 0) & (step == 0))
    def _():
        cnt_scr[...] = jnp.zeros_like(cnt_scr)
        seg_scr[...] = jnp.zeros_like(seg_scr)
        pad_scr[...] = jnp.zeros_like(pad_scr)

    @pl.when((phase == 1) & (step == 0))
    def _():
        seg_len = jnp.floor((cnt_scr[...] + (SUBLANES - 1)) * (1.0 / SUBLANES)) * SUBLANES
        rows_e = jnp.sum(seg_len, axis=1, keepdims=True)
        blocks = jnp.floor((rows_e + (MOE_BLOCK - 1)) * (1.0 / MOE_BLOCK)) * MOE_BLOCK
        r128 = lax.broadcasted_iota(I32, (LANES, LANES), 0)
        c128 = lax.broadcasted_iota(I32, (LANES, LANES), 1)
        before_tile = jnp.where(r128 < c128, 1.0, 0.0).astype(BF16)
        seg = jnp.dot((seg_len * (1.0 / SUBLANES)).astype(BF16), before_tile,
                      preferred_element_type=F32) * SUBLANES
        run = jnp.zeros((1, 1), F32)
        for e in range(N_EXPERTS):
            seg_scr[e:e + 1, :] = seg[e:e + 1, :] + run
            run = run + blocks[e:e + 1, :]
        pad_scr[...] = seg_len
        seg_ref[...] = seg_scr[...].astype(I32)
        pad_ref[...] = seg_len.astype(I32)
        blk_ref[...] = jnp.broadcast_to(blocks, blk_ref.shape).astype(I32)

    for g in range(group):
        lanes = slice(g * tn, (g + 1) * tn)
        _route_tile(lg_ref[:, lanes], br_ref, io_ref, wo_ref, lanes, step * group + g, phase, cnt_scr, pad_scr)


def _route_tile(lg, br_ref, io_ref, wo_ref, lanes, t, phase, cnt_scr, pad_scr):
    tn = lg.shape[1]
    tile_lane = lax.broadcasted_iota(I32, (N_EXPERTS, LANES), 1)
    s = jax.nn.sigmoid(lg)
    sb = s + br_ref[...]
    rows_s = [s[e:e + 1, :] for e in range(N_EXPERTS)]
    rows_b = [sb[e:e + 1, :] for e in range(N_EXPERTS)]
    gscore, gi1, gi2 = [], [], []
    epg = EXPERTS_PER_GROUP
    for g in range(N_GROUPS):
        a = rows_b[g * epg:(g + 1) * epg]
        m1 = functools.reduce(jnp.maximum, a)
        i1 = _first_index(a, m1)
        rest = [jnp.where(i1 == i, -jnp.inf, a[i]) for i in range(epg)]
        m2 = functools.reduce(jnp.maximum, rest)
        i2 = _first_index(rest, m2)
        gscore.append(m1 + m2)
        gi1.append(i1)
        gi2.append(i2)
    gm = functools.reduce(jnp.maximum, gscore)
    gsel = _first_index(gscore, gm)
    i1 = gi1[N_GROUPS - 1]
    i2 = gi2[N_GROUPS - 1]
    for g in range(N_GROUPS - 2, -1, -1):
        i1 = jnp.where(gsel == g, gi1[g], i1)
        i2 = jnp.where(gsel == g, gi2[g], i2)
    idx0 = gsel * epg + i1
    idx1 = gsel * epg + i2
    s0 = jnp.zeros_like(rows_s[0])
    s1 = jnp.zeros_like(rows_s[0])
    for e in range(N_EXPERTS):
        s0 = jnp.where(idx0 == e, rows_s[e], s0)
        s1 = jnp.where(idx1 == e, rows_s[e], s1)
    tot = s0 + s1
    w0 = s0 / tot
    w1 = s1 / tot

    eidx = lax.broadcasted_iota(I32, (N_EXPERTS, tn), 0)
    oh0 = eidx == idx0
    oh1 = eidx == idx1
    oh = jnp.where(oh0 | oh1, 1.0, 0.0)

    @pl.when(phase == 0)
    def _():
        cnt_scr[...] = cnt_scr[...] + jnp.where(tile_lane == t, jnp.sum(oh, axis=1, keepdims=True), 0.0)

    @pl.when(phase == 1)
    def _():
        rr = lax.broadcasted_iota(I32, (tn, tn), 0)
        cc = lax.broadcasted_iota(I32, (tn, tn), 1)
        upper = jnp.where(rr < cc, 1.0, 0.0).astype(BF16)
        before = jnp.dot(oh.astype(BF16), upper, preferred_element_type=F32)
        pad_col = jnp.sum(jnp.where(tile_lane == t, pad_scr[...], 0.0), axis=1, keepdims=True)
        run = jnp.zeros((1, 1), F32)
        offs = []
        for e in range(N_EXPERTS):
            offs.append(run)
            run = run + pad_col[e:e + 1, :]
        pos = before + jnp.concatenate(offs, axis=0)
        spos0 = jnp.sum(jnp.where(oh0, pos, 0.0), axis=0, keepdims=True)
        spos1 = jnp.sum(jnp.where(oh1, pos, 0.0), axis=0, keepdims=True)
        zi = jnp.zeros((SUBLANES - 4, tn), I32)
        io_ref[:, lanes] = jnp.concatenate([idx0, idx1, spos0.astype(I32), spos1.astype(I32), zi], axis=0)
        wo_ref[:, lanes] = jnp.concatenate([w0, w1, jnp.zeros((SUBLANES - 2, tn), F32)], axis=0)


def _route_call(logits_t, b_router, tn):
    ne, t_all = logits_t.shape
    ntile = t_all // tn
    assert ntile <= LANES
    group = max(g for g in range(1, ROUTE_TILES_PER_STEP + 1) if ntile % g == 0)
    tw = group * tn
    tok_out = pl.BlockSpec((SUBLANES, tw), lambda p, t: (0, t * p))
    tab_out = pl.BlockSpec((ne, LANES), lambda p, t: (0, 0))
    tab = jax.ShapeDtypeStruct((ne, LANES), I32)
    return pl.pallas_call(
        functools.partial(_route_kernel, tn=tn),
        grid=(2, ntile // group),
        in_specs=[pl.BlockSpec((ne, tw), lambda p, t: (0, t)), pl.BlockSpec((ne, 1), lambda p, t: (0, 0))],
        out_specs=[tok_out, tok_out, tab_out, tab_out, tab_out],
        out_shape=[
            jax.ShapeDtypeStruct((SUBLANES, t_all), I32),
            jax.ShapeDtypeStruct((SUBLANES, t_all), F32),
            tab, tab, tab,
        ],
        scratch_shapes=[pltpu.VMEM((ne, LANES), F32)] * 3,
        compiler_params=_cparams(("arbitrary", "arbitrary")),
        name="route",
    )(logits_t, b_router.reshape(ne, 1))


def _segment_copies(seg_ref, pad_ref, t, stage, hbm, sem, to_hbm, wait):
    if wait:
        total = functools.reduce(lambda a, b: a + b, [pad_ref[e * LANES + t] for e in range(N_EXPERTS)])
        for size in (2 * SEG_SIZES[0],) + SEG_SIZES:
            @pl.when((total & size) != 0)
            def _(size=size):
                src, dst = stage.at[pl.ds(0, size), :], hbm.at[pl.ds(0, size), :]
                if not to_hbm:
                    src, dst = dst, src
                pltpu.make_async_copy(src, dst, sem).wait()
        return
    loc = 0
    for e in range(N_EXPERTS):
        n = pad_ref[e * LANES + t]
        start = seg_ref[e * LANES + t]
        for size in SEG_SIZES:
            off = n & (-2 * size)

            @pl.when((n & size) != 0)
            def _(off=off, size=size, loc=loc, start=start):
                s_rows = pl.ds(pl.multiple_of(loc + off, SUBLANES), size)
                h_rows = pl.ds(pl.multiple_of(start + off, SUBLANES), size)
                src, dst = (stage.at[s_rows, :], hbm.at[h_rows, :])
                if not to_hbm:
                    src, dst = dst, src
                pltpu.make_async_copy(src, dst, sem).start()
        loc = loc + n


def _sort_matrix(io_ref, rows):
    r = lax.broadcasted_iota(I32, (rows, io_ref.shape[1]), 0)
    return r == io_ref[2:3, :], r == io_ref[3:4, :]


def _dispatch_kernel(seg_ref, pad_ref, zblk_ref, *refs, tiles):
    srcs = refs[:len(tiles)]
    io_ref, wo_ref, xb_hbm, stage, zbuf, sem = refs[len(tiles):]
    i = pl.program_id(0)
    m = MOE_BLOCK

    def zero_copy(j):
        start = pl.multiple_of(zblk_ref[j] * m, m)
        return pltpu.make_async_copy(zbuf, xb_hbm.at[pl.ds(start, m), :], sem.at[0])

    @pl.when(i == 0)
    def _():
        zbuf[...] = jnp.zeros_like(zbuf)
        for j in range(zblk_ref.shape[0]):
            @pl.when(zblk_ref[j] >= 0)
            def _():
                zero_copy(j).start()
        for j in range(zblk_ref.shape[0]):
            @pl.when(zblk_ref[j] >= 0)
            def _():
                zero_copy(j).wait()

    hp = srcs[0][...]
    if len(srcs) == 2:
        hp = jnp.where(i < tiles[0], hp, srcs[1][...])
    x = _unpack_bf16_pairs(hp)
    m0, m1 = _sort_matrix(io_ref, STAGE_ROWS)
    sort = jnp.where(m0 | m1, 1.0, 0.0).astype(BF16)
    xs = jnp.dot(sort, x, preferred_element_type=F32)
    dh = x.shape[1] // 2
    slot = i % 2
    cur = stage.at[slot]
    cur[:, :dh] = _pack_rounded_pairs(xs)
    ws = jnp.sum(jnp.where(m0, wo_ref[0:1, :], 0.0) + jnp.where(m1, wo_ref[1:2, :], 0.0), axis=1, keepdims=True)
    cur[:, dh:] = jnp.broadcast_to(lax.bitcast_convert_type(ws, U32), (STAGE_ROWS, LANES))
    _segment_copies(seg_ref, pad_ref, i, cur, xb_hbm, sem.at[slot], to_hbm=True, wait=False)

    @pl.when(i > 0)
    def _():
        _segment_copies(seg_ref, pad_ref, i - 1, stage.at[1 - slot], xb_hbm, sem.at[1 - slot], to_hbm=True, wait=True)

    @pl.when(i == pl.num_programs(0) - 1)
    def _():
        _segment_copies(seg_ref, pad_ref, i, cur, xb_hbm, sem.at[slot], to_hbm=True, wait=True)


def _dispatch_call(seg, pad, zblk, io, wo, sources, p_rows):
    tile = ROUTE_TILE
    dh = sources[0].shape[1]
    tiles = tuple(s.shape[0] // tile for s in sources)
    firsts = tuple(sum(tiles[:k]) for k in range(len(tiles)))

    def src_spec(first, ntile):
        return pl.BlockSpec((tile, dh), lambda i, *_: (jnp.clip(i - first, 0, ntile - 1), 0))

    tok_spec = pl.BlockSpec((SUBLANES, tile), lambda i, *_: (0, i))
    grid_spec = pltpu.PrefetchScalarGridSpec(
        num_scalar_prefetch=3,
        grid=(sum(tiles),),
        in_specs=[src_spec(f, n) for f, n in zip(firsts, tiles)] + [tok_spec, tok_spec],
        out_specs=pl.BlockSpec(memory_space=pl.ANY),
        scratch_shapes=[pltpu.VMEM((2, STAGE_ROWS, dh + LANES), U32), pltpu.VMEM((MOE_BLOCK, dh + LANES), U32),
                        pltpu.SemaphoreType.DMA((2,))],
    )
    return pl.pallas_call(
        functools.partial(_dispatch_kernel, tiles=tiles),
        grid_spec=grid_spec,
        out_shape=jax.ShapeDtypeStruct((p_rows, dh + LANES), U32),
        compiler_params=_cparams(("arbitrary",)),
        name="dispatch",
    )(seg, pad, zblk, *sources, io, wo)


def _expert_kernel(be_ref, nu_ref, va_ref, ne_ref, x_ref, wg_hbm, wu_hbm, wd_hbm, o_ref,
                   wg_b, wu_b, wd_b, wg_f, wu_f, wd_f, switch_ref, sem, *, layer):
    j = pl.program_id(0)
    dh = o_ref.shape[1]
    half = o_ref.shape[0] // 2
    valid = va_ref[j]
    e = be_ref[j]

    def weight_copies(expert, slot):
        return [pltpu.make_async_copy(hbm.at[layer, expert], buf.at[slot], sem.at[slot, k])
                for k, (hbm, buf) in enumerate(((wg_hbm, wg_f), (wu_hbm, wu_f), (wd_hbm, wd_f)))]

    @pl.when(j == 0)
    def _():
        switch_ref[0] = 0
        for cp in weight_copies(e, 0):
            cp.start()

    @pl.when((j == 0) | (e != be_ref[jnp.maximum(j - 1, 0)]))
    def _():
        slot = switch_ref[0] % 2
        for cp in weight_copies(e, slot):
            cp.wait()
        wg_b[...] = wg_f[slot].astype(BF16)
        wu_b[...] = wu_f[slot].astype(BF16)
        wd_b[...] = wd_f[slot].astype(BF16)
        nxt = ne_ref[e]

        @pl.when(nxt >= 0)
        def _():
            for cp in weight_copies(nxt, 1 - slot):
                cp.start()
        switch_ref[0] = switch_ref[0] + 1

    def ffn_halves(halves):
        gates = []
        for r in halves:
            x = _unpack_bf16_pairs(x_ref[r * half:(r + 1) * half, :dh])
            gates.append((jnp.dot(x, wg_b[...], preferred_element_type=F32),
                          jnp.dot(x, wu_b[...], preferred_element_type=F32)))
        for r, (a, u) in zip(halves, gates):
            hmid = (a * jax.nn.sigmoid(a)) * u
            y = jnp.dot(hmid.astype(BF16), wd_b[...], preferred_element_type=F32)
            row_w = lax.bitcast_convert_type(x_ref[r * half:(r + 1) * half, dh:dh + 1], F32)
            o_ref[r * half:(r + 1) * half, :] = _pack_bf16_pairs(y * row_w)

    @pl.when(valid > half)
    def _():
        ffn_halves((0, 1))

    @pl.when((valid > 0) & (valid <= half))
    def _():
        ffn_halves((0,))
        o_ref[half:, :] = jnp.zeros((half, dh), o_ref.dtype)

    @pl.when(valid == 0)
    def _():
        o_ref[...] = jnp.zeros_like(o_ref)


def _expert_call(block_e, nused, valid, next_e, xb, wg, wu, wd, layer):
    p_rows = xb.shape[0]
    dh = xb.shape[1] - LANES
    d = 2 * dh
    m = MOE_BLOCK
    nb = p_rows // m
    de = wg.shape[-1]

    def xmap(j, be, nu, va, ne):
        return (jnp.minimum(j, nu[0] - 1), 0)

    any_spec = pl.BlockSpec(memory_space=pl.ANY)
    grid_spec = pltpu.PrefetchScalarGridSpec(
        num_scalar_prefetch=4,
        grid=(nb,),
        in_specs=[pl.BlockSpec((m, dh + LANES), xmap), any_spec, any_spec, any_spec],
        out_specs=pl.BlockSpec((m, dh), lambda j, be, nu, va, ne: (j, 0)),
        scratch_shapes=[
            pltpu.VMEM((d, de), BF16), pltpu.VMEM((d, de), BF16), pltpu.VMEM((de, d), BF16),
            pltpu.VMEM((2, d, de), F32), pltpu.VMEM((2, d, de), F32), pltpu.VMEM((2, de, d), F32),
            pltpu.SMEM((1,), I32), pltpu.SemaphoreType.DMA((2, 3)),
        ],
    )
    return pl.pallas_call(
        functools.partial(_expert_kernel, layer=layer),
        grid_spec=grid_spec,
        out_shape=jax.ShapeDtypeStruct((p_rows, dh), U32),
        compiler_params=_cparams(("arbitrary",)),
        name="experts",
    )(block_e, nused, valid, next_e, xb, wg, wu, wd)


def _combine_kernel(*refs, tile0, final):
    if final:
        seg_ref, pad_ref, yb_hbm, io_ref, x_ref, g2_ref, fn_ref, o_ref, stage, sem = refs
    else:
        seg_ref, pad_ref, yb_hbm, io_ref, x_ref, g2_ref, o_ref, stage, sem = refs
    i = pl.program_id(0)

    slot = i % 2

    @pl.when(i == 0)
    def _():
        stage[...] = jnp.zeros_like(stage)
        _segment_copies(seg_ref, pad_ref, tile0, stage.at[0], yb_hbm, sem.at[0], to_hbm=False, wait=False)

    @pl.when(i + 1 < pl.num_programs(0))
    def _():
        _segment_copies(seg_ref, pad_ref, tile0 + i + 1, stage.at[1 - slot], yb_hbm, sem.at[1 - slot],
                        to_hbm=False, wait=False)

    _segment_copies(seg_ref, pad_ref, tile0 + i, stage.at[slot], yb_hbm, sem.at[slot], to_hbm=False, wait=True)
    rows = _unpack_bf16_pairs(stage[slot])
    m0, m1 = _sort_matrix(io_ref, STAGE_ROWS)
    pick = jnp.where(m0 | m1, 1.0, 0.0).astype(BF16)
    y = lax.dot_general(pick, rows, (((0,), (0,)), ((), ())), preferred_element_type=F32)
    xo = x_ref[...] + g2_ref[...] * y
    if final:
        ms = jnp.mean(xo * xo, axis=-1, keepdims=True)
        xo = (xo * lax.rsqrt(ms + NORM_EPS)) * fn_ref[...]
    o_ref[...] = xo


def _combine_call(seg, pad, yb, io, x2d, mods5, layer, row_of_tile, final_g, tile0):
    n, d = x2d.shape
    tm = ROUTE_TILE
    final = final_g is not None
    in_specs = [
        pl.BlockSpec(memory_space=pl.ANY),
        pl.BlockSpec((SUBLANES, tm), lambda t, *_: (0, tile0 + t)),
        pl.BlockSpec((tm, d), lambda t, *_: (t, 0)),
        pl.BlockSpec((None, None, None, 1, d), lambda t, *_: (layer, row_of_tile(t), 5, 0, 0)),
    ]
    args = [yb, io, x2d, mods5]
    if final:
        in_specs.append(pl.BlockSpec((1, d), lambda t, *_: (0, 0)))
        args.append(final_g.reshape(1, d))
    grid_spec = pltpu.PrefetchScalarGridSpec(
        num_scalar_prefetch=2,
        grid=(n // tm,),
        in_specs=in_specs,
        out_specs=pl.BlockSpec((tm, d), lambda t, *_: (t, 0)),
        scratch_shapes=[pltpu.VMEM((2, STAGE_ROWS, d // 2), U32), pltpu.SemaphoreType.DMA((2,))],
    )
    return pl.pallas_call(
        functools.partial(_combine_kernel, tile0=tile0, final=final),
        grid_spec=grid_spec,
        out_shape=jax.ShapeDtypeStruct((n, d), F32),
        compiler_params=_cparams(("arbitrary",)),
        name="combine",
    )(seg, pad, *args)


def _pack_w_kernel(w_ref, o_ref):
    cut = C_Z + 2 * GLA_GATE_RANK
    o_ref[:, :C_Z] = w_ref[:, :C_Z].astype(BF16)
    gate_tile = w_ref[:, C_Z:C_Z + LANES]
    lane = lax.broadcasted_iota(I32, gate_tile.shape, 1)
    o_ref[:, C_Z:C_SQ] = jnp.where(lane < 2 * GLA_GATE_RANK, gate_tile, 0.0).astype(BF16)
    o_ref[:, C_SQ:] = w_ref[:, cut:].astype(BF16)


def _pack_w_in(w):
    assert sum(IN_SIZES[:6]) == C_Z + 2 * GLA_GATE_RANK and sum(IN_SIZES) - sum(IN_SIZES[:6]) == C_END - C_SQ
    depth, d, cols = w.shape
    rows = ROW_TILE // 2
    return pl.pallas_call(
        _pack_w_kernel,
        grid=(depth, d // rows),
        in_specs=[pl.BlockSpec((None, rows, cols), lambda l, r: (l, r, 0))],
        out_specs=pl.BlockSpec((None, rows, C_END), lambda l, r: (l, r, 0)),
        out_shape=jax.ShapeDtypeStruct((depth, d, C_END), BF16),
        compiler_params=_cparams(("arbitrary", "arbitrary")),
        name="pack_w_in",
    )(w)


def _pad_up(up, row0):
    up = up.reshape(GLA_GATE_RANK, GLA_HEADS // 2, LANES).transpose(1, 0, 2)
    hi = up.astype(BF16)
    lo = (up - hi.astype(F32)).astype(BF16)
    out = jnp.zeros((GLA_HEADS // 2, LANES, LANES), BF16)
    for group, part in enumerate((hi, hi, lo)):
        r = group * 2 * GLA_GATE_RANK + row0
        out = out.at[:, r:r + GLA_GATE_RANK, :].set(part)
    return out


def _rope_tables(seq):
    rows = seq // GRID_W
    dim = jnp.arange(LANES, dtype=I32) % SWA_HD
    inv = ROPE_THETA ** (-((dim % ROPE_HALF).astype(F32) * 2.0 / ROPE_AXIS_DIM))
    sign = jnp.where(dim % ROPE_AXIS_DIM < ROPE_HALF, -1.0, 1.0)
    by_row = (dim < ROPE_AXIS_DIM)[None, None, :]
    ang_r = (jnp.arange(rows, dtype=F32)[:, None] * inv[None, :])[:, None, :]
    ang_c = (jnp.arange(GRID_W, dtype=F32)[:, None] * inv[None, :])[None, :, :]
    cos = jnp.where(by_row, jnp.cos(ang_r), jnp.cos(ang_c))
    sin = jnp.where(by_row, jnp.sin(ang_r), jnp.sin(ang_c)) * sign
    return cos.reshape(seq, LANES), sin.reshape(seq, LANES)


def _tile(n, pref):
    t = pref
    while n % t:
        t //= 2
    return t


def kernel(x, c, ctx, c_ctx, w_ada, b_ada, norm1, norm2, w_in, gla_up_f, gla_bias_f, gla_up_b, gla_bias_b,
           gla_norm, swa_sink, w_out, w_router, b_router, w_gate, w_up, w_down, final_norm):
    batch, seq, d = x.shape
    lc = ctx.shape[1]
    depth = w_ada.shape[0]
    n_l = batch * seq
    n_c = batch * lc
    npair = GLA_HEADS // 2
    assert batch + 1 <= SUBLANES
    assert n_l % ROUTE_TILE == 0 and n_c % ROUTE_TILE == 0

    tm_l = _tile(seq, ROW_TILE)
    tm_in = _tile(seq, INPROJ_TILE)
    tm_c = _tile(lc, CTX_TILE)
    tl_l = _tile(seq, GLA_TILE)
    tl_c = _tile(lc, ROW_TILE)

    rows = jnp.zeros((SUBLANES, d), F32).at[:batch].set(c).at[batch].set(c_ctx)
    mods = _ada_call(rows, w_ada, b_ada)
    mods5 = mods.reshape(depth, SUBLANES, 6, 1, d)

    tabs = _rope_tables(seq)
    w_packed = _pack_w_in(w_in)
    w_out_b = w_out.astype(BF16)
    wr_pad = jnp.zeros((d, LANES), F32).at[:, :N_EXPERTS].set(w_router)
    wr_hi = wr_pad.astype(BF16)
    wr_lo = (wr_pad - wr_hi.astype(F32)).astype(BF16)
    wr_cat = jnp.concatenate([wr_hi, wr_lo], axis=-1)

    def lat_row(tm):
        return lambda t: t // (seq // tm)

    def ctx_row(tm):
        return lambda t: batch

    xl = x.reshape(n_l, d)
    xc = ctx.reshape(n_c, d)
    for i in range(depth):
        last = i == depth - 1
        upf = _pad_up(gla_up_f[i], 0)
        upb = _pad_up(gla_up_b[i], GLA_GATE_RANK)
        bias_f = gla_bias_f[i].reshape(npair, 1, LANES)
        bias_b = gla_bias_b[i].reshape(npair, 1, LANES)

        c_qk, c_v, c_g, c_z, c_sq, c_sk, c_sv = _inproj_call(
            xc, norm1[i], mods5, i, ctx_row(tm_c), w_packed, None, tm_c, lc // tm_c)
        l_qk, l_v, l_g, l_z, l_sq, l_sk, l_sv = _inproj_call(
            xl, norm1[i], mods5, i, lat_row(tm_in), w_packed, tabs, tm_in, seq // tm_in)

        s_zero = jnp.zeros((batch, npair, 2, 2 * GLA_DV, LANES), F32)
        oc_f, oc_b, s_ctx = _gla_call(c_qk, c_v, c_z, upf, upb, bias_f, bias_b, s_zero, batch, lc, tl_c)
        ol_f, ol_b, _ = _gla_call(l_qk, l_v, l_z, upf, upb, bias_f, bias_b, s_ctx, batch, seq, tl_l)

        swa_l = _swa_call(l_sq, l_sk, l_sv, c_sk, c_sv, swa_sink[i], batch, seq, lc)
        xl, h2l, lg_l = _outproj_call(ol_f, ol_b, l_g, gla_norm[i], swa_l, w_out_b, xl, mods5, i, lat_row(tm_l),
                                      norm2[i], wr_cat, tm_l)
        if last:
            logits_t = lg_l
            t_all = n_l
        else:
            swa_c = _swa_ctx_call(c_sq, c_sk, c_sv, swa_sink[i], batch, lc)
            xc, h2c, lg_c = _outproj_call(oc_f, oc_b, c_g, gla_norm[i], swa_c, w_out_b, xc, mods5, i, ctx_row(tm_c),
                                          norm2[i], wr_cat, tm_c)
            logits_t = jnp.concatenate([lg_l, lg_c], axis=1)
            t_all = n_l + n_c

        io, wo, seg, pad, blk = _route_call(logits_t, b_router, ROUTE_TILE)
        m = MOE_BLOCK
        padded = blk[:, 0]
        pend = jnp.cumsum(padded)
        ntile = t_all // ROUTE_TILE
        nb = -(-(t_all * TOP_K + ntile * N_EXPERTS * (SUBLANES - 1)) // m) + N_EXPERTS
        p_rows = nb * m
        seg = seg.reshape(-1)
        pad = pad.reshape(-1)
        nused = (pend[-1:] // m).astype(I32)
        tail = nused + jnp.arange(nb - (t_all * TOP_K) // m, dtype=I32)
        zblk = jnp.concatenate([jnp.where(padded > 0, pend // m - 1, -1),
                                jnp.where(tail < nb, tail, -1)]).astype(I32)
        blk_row = jnp.minimum(jnp.arange(nb, dtype=I32), nused - 1) * m
        block_e = jnp.sum((pend[None, :] <= blk_row[:, None]).astype(I32), axis=1)
        rows_e = jnp.sum(pad.reshape(N_EXPERTS, LANES), axis=1)
        own = (block_e[:, None] == jnp.arange(N_EXPERTS, dtype=I32)[None, :]).astype(I32)
        seg_end = jnp.sum(own * (pend - padded + rows_e)[None, :], axis=1)
        blk_idx = jnp.arange(nb, dtype=I32)
        valid = jnp.where(blk_idx < nused, jnp.clip(seg_end - blk_idx * m, 0, m), 0).astype(I32)

        sources = (h2l,) if last else (h2l, h2c)
        xb = _dispatch_call(seg, pad, zblk, io, wo, sources, p_rows)
        eid = jnp.arange(N_EXPERTS, dtype=I32)
        later = (eid[None, :] > eid[:, None]) & (padded[None, :] > 0)
        next_e = jnp.min(jnp.where(later, eid[None, :], N_EXPERTS), axis=1)
        next_e = jnp.where(next_e < N_EXPERTS, next_e, -1).astype(I32)
        yb = _expert_call(block_e, nused, valid, next_e, xb, w_gate, w_up, w_down, i)
        xl_new = _combine_call(seg, pad, yb, io, xl, mods5, i, lat_row(ROUTE_TILE),
                               final_norm if last else None, 0)
        if not last:
            xc = _combine_call(seg, pad, yb, io, xc, mods5, i, ctx_row(0), None, n_l // ROUTE_TILE)
        xl = xl_new
    return xl.reshape(batch, seq, d)
```

```python
import functools

import jax
import jax.numpy as jnp
from jax import lax
from jax.experimental import pallas as pl
from jax.experimental.pallas import tpu as pltpu

F32 = jnp.float32
BF16 = jnp.bfloat16
I32 = jnp.int32
U32 = jnp.uint32

GRID_W = 64
NORM_EPS = 1e-6
GLA_HEADS = 4
GLA_DK = 64
GLA_DV = 128
GLA_KW = GLA_HEADS * GLA_DK
GLA_WIDTH = GLA_HEADS * GLA_DV
GLA_GATE_RANK = 16
GLA_TAU = 16.0
GLA_CHUNK = 64
SWA_HEADS = 8
SWA_KV_HEADS = 2
SWA_HD = 64
SWA_WIDTH = SWA_HEADS * SWA_HD
SWA_KVW = SWA_KV_HEADS * SWA_HD
WINDOW = 128
ATT_BLOCK = 128
ROPE_THETA = 10000.0
ROPE_AXIS_DIM = SWA_HD // 2
ROPE_HALF = ROPE_AXIS_DIM // 2
N_EXPERTS = 16
N_GROUPS = 4
EXPERTS_PER_GROUP = N_EXPERTS // N_GROUPS
TOP_K = 2
IN_SIZES = (GLA_KW, GLA_KW, GLA_WIDTH, GLA_WIDTH, GLA_GATE_RANK, GLA_GATE_RANK, SWA_WIDTH, SWA_KVW, SWA_KVW)

LANES = 128
SUBLANES = 8
VMEM_LIMIT = 48 * 1024 * 1024

ROW_TILE = 1024
GLA_TILE = 1024
GLA_SUM_ROWS = 256
INPROJ_TILE = 1024
CTX_TILE = 256
ADA_COLS = 1536
SWA_BLOCKS_PER_STEP = 8
MOE_BLOCK = 512
ROUTE_TILE = 512
ROUTE_TILES_PER_STEP = 11
SEG_SIZES = (512, 256, 128, 64, 32, 16, 8)
STAGE_ROWS = TOP_K * ROUTE_TILE + LANES

C_QK = 0
C_V = 512
C_G = 1024
C_Z = 1536
C_SQ = 1664
C_SK = 2176
C_SV = 2304
C_END = 2432
SV_LANES = 4 * LANES
NEG_BIG = -1e30
LOG2E = 1.4426950408889634

assert WINDOW == ATT_BLOCK


def _cparams(sem):
    return pltpu.CompilerParams(dimension_semantics=sem, vmem_limit_bytes=VMEM_LIMIT)


def _pack_bf16_pairs(x):
    return _pack_rounded_pairs(x.astype(BF16).astype(F32))


def _pack_rounded_pairs(xr):
    n = xr.shape[1] // 2
    lo = lax.bitcast_convert_type(xr[:, :n], U32)
    hi = lax.bitcast_convert_type(xr[:, n:], U32)
    return (lo >> 16) | hi


def _unpack_bf16_pairs(w):
    lo = lax.bitcast_convert_type(w << 16, F32)
    hi = lax.bitcast_convert_type(w & jnp.uint32(0xFFFF0000), F32)
    return jnp.concatenate([lo.astype(BF16), hi.astype(BF16)], axis=-1)


def _ada_kernel(a_ref, w_ref, b_ref, o_ref):
    a = a_ref[...]
    act = a * jax.nn.sigmoid(a)
    a_hi = act.astype(BF16)
    a_lo = (act - a_hi.astype(F32)).astype(BF16)
    w = w_ref[...]
    w_hi = w.astype(BF16)
    w_lo = (w - w_hi.astype(F32)).astype(BF16)
    t = jnp.dot(jnp.concatenate([a_hi, a_lo], axis=0), w_hi, preferred_element_type=F32)
    n = a.shape[0]
    o_ref[...] = t[:n] + t[n:] + jnp.dot(a_hi, w_lo, preferred_element_type=F32) + b_ref[...]


def _ada_call(rows, w_ada, b_ada):
    depth, d, n6 = w_ada.shape
    tn = ADA_COLS
    return pl.pallas_call(
        _ada_kernel,
        grid=(depth, n6 // tn),
        in_specs=[
            pl.BlockSpec((SUBLANES, d), lambda l, j: (0, 0)),
            pl.BlockSpec((None, d, tn), lambda l, j: (l, 0, j)),
            pl.BlockSpec((None, 1, tn), lambda l, j: (l, 0, j)),
        ],
        out_specs=pl.BlockSpec((None, SUBLANES, tn), lambda l, j: (l, 0, j)),
        out_shape=jax.ShapeDtypeStruct((depth, SUBLANES, n6), F32),
        compiler_params=_cparams(("arbitrary", "arbitrary")),
        name="adaln",
    )(rows, w_ada, b_ada.reshape(depth, 1, n6))


def _rope_tile(xj, cos, sin, lane_lo):
    partner = jnp.where(lane_lo, pltpu.roll(xj, LANES - ROPE_HALF, 1), pltpu.roll(xj, ROPE_HALF, 1))
    return xj * cos + partner * sin


def _inproj_kernel(*refs, rope):
    if rope:
        (x_ref, g_ref, sc_ref, sh_ref, w_ref, cos_ref, sin_ref,
         qk_ref, v_ref, gate_ref, z_ref, sq_ref, sk_ref, sv_ref) = refs
    else:
        (x_ref, g_ref, sc_ref, sh_ref, w_ref,
         qk_ref, v_ref, gate_ref, z_ref, sq_ref, sk_ref, sv_ref) = refs
    half = x_ref.shape[0] // 2
    scale = g_ref[...] * (1.0 + sc_ref[...])
    normed = []
    for r in range(2):
        x = x_ref[r * half:(r + 1) * half, :]
        ms = jnp.mean(x * x, axis=-1, keepdims=True)
        normed.append(((x * lax.rsqrt(ms + NORM_EPS)) * scale + sh_ref[...]).astype(BF16))

    for r in range(2):
        rows = slice(r * half, (r + 1) * half)
        hb = normed[r]

        def proj(a, b):
            return jnp.dot(hb, w_ref[:, a:b], preferred_element_type=F32)

        qk = proj(C_QK, C_V)
        qk_ref[rows, :GLA_KW] = (qk[:, :GLA_KW] * (GLA_DK ** -0.5)).astype(qk_ref.dtype)
        qk_ref[rows, GLA_KW:] = qk[:, GLA_KW:].astype(qk_ref.dtype)
        v_ref[rows, :] = proj(C_V, C_G).astype(v_ref.dtype)
        gate_ref[rows, :] = proj(C_G, C_Z).astype(gate_ref.dtype)
        z_ref[rows, :] = proj(C_Z, C_SQ)
        sq = proj(C_SQ, C_SK) * (SWA_HD ** -0.5 * LOG2E)
        sk = proj(C_SK, C_SV)
        sv = proj(C_SV, C_END)
        if rope:
            cos = cos_ref[rows, :]
            sin = sin_ref[rows, :]
            lane_lo = (lax.broadcasted_iota(I32, cos.shape, 1) % ROPE_AXIS_DIM) < ROPE_HALF
            for j in range(SWA_WIDTH // LANES):
                sl = slice(j * LANES, (j + 1) * LANES)
                sq_ref[rows, sl] = _rope_tile(sq[:, sl], cos, sin, lane_lo).astype(sq_ref.dtype)
            sk = _rope_tile(sk, cos, sin, lane_lo)
        else:
            sq_ref[rows, :] = sq.astype(sq_ref.dtype)
        first = lax.broadcasted_iota(I32, sk.shape, 1) < SWA_HD
        sk_sw = pltpu.roll(sk, SWA_HD, 1)
        sk_ref[rows, :LANES] = jnp.where(first, sk, sk_sw).astype(sk_ref.dtype)
        sk_ref[rows, LANES:] = jnp.where(first, sk_sw, sk).astype(sk_ref.dtype)
        sv_sw = pltpu.roll(sv, SWA_HD, 1)
        sv_ref[rows, 0 * LANES:1 * LANES] = jnp.where(first, sv, 1.0).astype(sv_ref.dtype)
        sv_ref[rows, 1 * LANES:2 * LANES] = jnp.where(first, 1.0, sv_sw).astype(sv_ref.dtype)
        sv_ref[rows, 2 * LANES:3 * LANES] = jnp.where(first, sv_sw, 1.0).astype(sv_ref.dtype)
        sv_ref[rows, 3 * LANES:4 * LANES] = jnp.where(first, 1.0, sv).astype(sv_ref.dtype)


def _inproj_call(x2d, g, mods5, layer, row_of_tile, w_packed, tabs, tm, tiles_per_seq):
    n, d = x2d.shape
    nt = n // tm
    rope = tabs is not None

    def mod_spec(k):
        return pl.BlockSpec((None, None, None, 1, d), lambda t: (layer, row_of_tile(t), k, 0, 0))

    in_specs = [
        pl.BlockSpec((tm, d), lambda t: (t, 0)),
        pl.BlockSpec((1, d), lambda t: (0, 0)),
        mod_spec(1), mod_spec(0),
        pl.BlockSpec((None, d, C_END), lambda t: (layer, 0, 0)),
    ]
    args = [x2d, g.reshape(1, d), mods5, mods5, w_packed]
    if rope:
        in_specs += [pl.BlockSpec((tm, LANES), lambda t: (t % tiles_per_seq, 0))] * 2
        args += [tabs[0], tabs[1]]
    widths = (2 * GLA_KW, GLA_WIDTH, GLA_WIDTH, LANES, SWA_WIDTH, 2 * SWA_KVW, SV_LANES)
    dtypes = (BF16, BF16, BF16, F32, BF16, BF16, BF16)
    return pl.pallas_call(
        functools.partial(_inproj_kernel, rope=rope),
        grid=(nt,),
        in_specs=in_specs,
        out_specs=[pl.BlockSpec((tm, w), lambda t: (t, 0)) for w in widths],
        out_shape=[jax.ShapeDtypeStruct((n, w), dt) for w, dt in zip(widths, dtypes)],
        compiler_params=_cparams(("arbitrary",)),
        name="inproj",
    )(*args)


def _log_sigmoid(x):
    return jnp.minimum(x, 0.0) - jnp.log(1.0 + jnp.exp(-jnp.abs(x)))


def _gla_prepare(q_ref, k_ref, v_ref, z_ref, up_ref, bias_ref, tri_ref, scr, fwd):
    qin_s, km0_s, km1_s, kout_s, vm0_s, vm1_s, dec_s = scr[:7]
    qbd_s = scr[9]
    tl = q_ref.shape[0]
    c = GLA_CHUNK
    z = z_ref[...]
    z_hi = z.astype(BF16).astype(F32)
    zc = z_hi + pltpu.roll(z - z_hi, 2 * GLA_GATE_RANK, 1) + pltpu.roll(z_hi, 4 * GLA_GATE_RANK, 1)
    x = jnp.dot(zc.astype(BF16), up_ref[...], preferred_element_type=F32) + bias_ref[...]
    la = _log_sigmoid(x) * (LOG2E / GLA_TAU)
    hi = la.astype(BF16)
    lo = (la - hi.astype(F32)).astype(BF16)
    hl = jnp.concatenate([hi, lo], axis=1)
    tg = tri_ref.shape[0]
    sums = jnp.concatenate([jnp.dot(tri_ref[...], hl[r0:r0 + tg, :], preferred_element_type=F32)
                            for r0 in range(0, tl, tg)], axis=0)
    b = sums[:, :LANES] + sums[:, LANES:]
    b3 = b.reshape(tl // c, c, LANES)
    edge = b3[:, c - 1:c, :] if fwd else b3[:, 0:1, :]
    btot = jnp.broadcast_to(edge, b3.shape).reshape(tl, LANES)
    q = q_ref[...].astype(F32)
    k = k_ref[...].astype(F32)
    first = (lax.broadcasted_iota(I32, (tl, LANES), 0) % (2 * c)) < c
    q_in = q * jnp.exp2(b)
    qin_s[...] = q_in.astype(BF16)
    qbd_s[...] = jnp.concatenate([jnp.where(first, q_in, 0.0), jnp.where(first, 0.0, q_in)], axis=1).astype(BF16)
    k_in = k * jnp.exp2(-b)
    head0 = lax.broadcasted_iota(I32, (tl, LANES), 1) < GLA_DK
    km0_s[...] = jnp.where(head0, k_in, 0.0).astype(BF16)
    km1_s[...] = jnp.where(head0, 0.0, k_in).astype(BF16)
    k_out = k * jnp.exp2(btot - b)
    kout_s[...] = jnp.concatenate([jnp.where(first, k_out, 0.0), jnp.where(first, 0.0, k_out)], axis=1).astype(BF16)
    dec_s[...] = jnp.exp2(btot)
    vf = v_ref[...].astype(F32)
    vhead0 = lax.broadcasted_iota(I32, (tl, 2 * GLA_DV), 1) < GLA_DV
    vm0_s[...] = jnp.where(vhead0, vf, 0.0).astype(BF16)
    vm1_s[...] = jnp.where(vhead0, 0.0, vf).astype(BF16)


def _gla_increment(v_ref, scr, pair):
    kout_s, u_s = scr[3], scr[7]
    rows = pl.ds(pair * 2 * GLA_CHUNK, 2 * GLA_CHUNK)
    u_t = lax.dot_general(v_ref[rows, :], kout_s[rows, :], (((0,), (0,)), ((), ())),
                          preferred_element_type=F32)
    srow = lax.broadcasted_iota(I32, (2 * GLA_DV, 2 * LANES), 0) // GLA_DV
    scol = (lax.broadcasted_iota(I32, (2 * GLA_DV, 2 * LANES), 1) % LANES) // GLA_DK
    u_s[pair] = jnp.where(srow == scol, u_t, 0.0)


def _gla_states(scr, s_ref, chunk_order):
    dec_s, u_s, sprev_s = scr[6], scr[7], scr[8]
    s_t = s_ref[...]
    for cidx in chunk_order:
        pair, lanes = cidx // 2, pl.ds((cidx % 2) * LANES, LANES)
        sprev_s[pair, :, lanes] = s_t.astype(BF16)
        r0 = cidx * GLA_CHUNK
        s_t = s_t * dec_s[r0:r0 + 1, :] + u_s[pair, :, lanes]
    s_ref[...] = s_t


def _gla_output(o_ref, scr, pair, fwd):
    qin_s, km0_s, km1_s, _, vm0_s, vm1_s, _, _, sprev_s, qbd_s = scr
    c = GLA_CHUNK
    r0 = pair * 2 * c
    rows = pl.ds(r0, 2 * c)
    ca, cb = pl.ds(r0, c), pl.ds(r0 + c, c)
    kst = jnp.concatenate([km0_s[ca, :], km1_s[ca, :], km0_s[cb, :], km1_s[cb, :]], axis=0)
    a = lax.dot_general(qin_s[rows, :], kst, (((1,), (1,)), ((), ())), preferred_element_type=F32)
    ri = lax.broadcasted_iota(I32, (2 * c, 4 * c), 0)
    ci = lax.broadcasted_iota(I32, (2 * c, 4 * c), 1)
    same_chunk = (ri // c) == (ci // (2 * c))
    keep = same_chunk & ((ci % c <= ri % c) if fwd else (ci % c >= ri % c))
    a = jnp.where(keep, a, 0.0).astype(BF16)
    vbd = jnp.concatenate([vm0_s[ca, :], vm1_s[ca, :], vm0_s[cb, :], vm1_s[cb, :]], axis=0)
    o = jnp.dot(a, vbd, preferred_element_type=F32)
    o = o + lax.dot_general(qbd_s[rows, :], sprev_s[pair], (((1,), (1,)), ((), ())), preferred_element_type=F32)
    o_ref[rows, :] = o.astype(o_ref.dtype)


def _gla_kernel(qf_ref, kf_ref, vf_ref, zf_ref, qb_ref, kb_ref, vb_ref, zb_ref,
                upf_ref, upb_ref, bf_ref, bb_ref, trif_ref, trib_ref, s0_ref,
                of_ref, ob_ref, sfin_ref, sf_scr, sb_scr, *scr, nchunk):
    i = pl.program_id(2)
    nt = pl.num_programs(2)
    scr_f, scr_b = scr[:len(scr) // 2], scr[len(scr) // 2:]

    @pl.when(i == 0)
    def _():
        sf_scr[...] = s0_ref[0]
        sb_scr[...] = s0_ref[1]

    _gla_prepare(qf_ref, kf_ref, vf_ref, zf_ref, upf_ref, bf_ref, trif_ref, scr_f, True)
    _gla_prepare(qb_ref, kb_ref, vb_ref, zb_ref, upb_ref, bb_ref, trib_ref, scr_b, False)
    for pair in range(nchunk // 2):
        _gla_increment(vf_ref, scr_f, pair)
        _gla_increment(vb_ref, scr_b, pair)
    _gla_states(scr_f, sf_scr, range(nchunk))
    _gla_states(scr_b, sb_scr, range(nchunk - 1, -1, -1))
    for pair in range(nchunk // 2):
        _gla_output(of_ref, scr_f, pair, True)
        _gla_output(ob_ref, scr_b, pair, False)

    @pl.when(i == nt - 1)
    def _():
        sfin_ref[0] = sf_scr[...]
        sfin_ref[1] = sb_scr[...]


def _gla_call(qk, v, z, upf_pad, upb_pad, bias_f, bias_b, s0, batch, seq, tl):
    n = qk.shape[0]
    nt = seq // tl
    npair = GLA_HEADS // 2

    def fwd_row(b, p, i):
        return b * nt + i

    def bwd_row(b, p, i):
        return b * nt + (nt - 1 - i)

    def specs(row):
        return [
            pl.BlockSpec((tl, LANES), lambda b, p, i: (row(b, p, i), p)),
            pl.BlockSpec((tl, LANES), lambda b, p, i: (row(b, p, i), npair + p)),
            pl.BlockSpec((tl, 2 * GLA_DV), lambda b, p, i: (row(b, p, i), p)),
            pl.BlockSpec((tl, LANES), lambda b, p, i: (row(b, p, i), 0)),
        ]

    in_specs = specs(fwd_row) + specs(bwd_row) + [
        pl.BlockSpec((None, LANES, LANES), lambda b, p, i: (p, 0, 0)),
        pl.BlockSpec((None, LANES, LANES), lambda b, p, i: (p, 0, 0)),
        pl.BlockSpec((None, 1, LANES), lambda b, p, i: (p, 0, 0)),
        pl.BlockSpec((None, 1, LANES), lambda b, p, i: (p, 0, 0)),
        pl.BlockSpec((None, GLA_SUM_ROWS, GLA_SUM_ROWS), lambda b, p, i: (0, 0, 0)),
        pl.BlockSpec((None, GLA_SUM_ROWS, GLA_SUM_ROWS), lambda b, p, i: (1, 0, 0)),
        pl.BlockSpec((None, None, 2, 2 * GLA_DV, LANES), lambda b, p, i: (b, p, 0, 0, 0)),
    ]
    assert tl % GLA_SUM_ROWS == 0
    ri = jnp.arange(GLA_SUM_ROWS)[:, None]
    ci = jnp.arange(GLA_SUM_ROWS)[None, :]
    same = (ri // GLA_CHUNK) == (ci // GLA_CHUNK)
    tri = jnp.stack([same & (ci <= ri), same & (ci >= ri)]).astype(BF16)
    npairs = tl // (2 * GLA_CHUNK)
    dir_scratch = [
        pltpu.VMEM((tl, LANES), BF16),
        pltpu.VMEM((tl, LANES), BF16),
        pltpu.VMEM((tl, LANES), BF16),
        pltpu.VMEM((tl, 2 * LANES), BF16),
        pltpu.VMEM((tl, 2 * GLA_DV), BF16),
        pltpu.VMEM((tl, 2 * GLA_DV), BF16),
        pltpu.VMEM((tl, LANES), F32),
        pltpu.VMEM((npairs, 2 * GLA_DV, 2 * LANES), F32),
        pltpu.VMEM((npairs, 2 * GLA_DV, 2 * LANES), BF16),
        pltpu.VMEM((tl, 2 * LANES), BF16),
    ]
    out_specs = [
        pl.BlockSpec((tl, 2 * GLA_DV), lambda b, p, i: (fwd_row(b, p, i), p)),
        pl.BlockSpec((tl, 2 * GLA_DV), lambda b, p, i: (bwd_row(b, p, i), p)),
        pl.BlockSpec((None, None, 2, 2 * GLA_DV, LANES), lambda b, p, i: (b, p, 0, 0, 0)),
    ]
    out_shape = [
        jax.ShapeDtypeStruct((n, GLA_WIDTH), BF16),
        jax.ShapeDtypeStruct((n, GLA_WIDTH), BF16),
        jax.ShapeDtypeStruct((batch, npair, 2, 2 * GLA_DV, LANES), F32),
    ]
    return pl.pallas_call(
        functools.partial(_gla_kernel, nchunk=tl // GLA_CHUNK),
        grid=(batch, npair, nt),
        in_specs=in_specs,
        out_specs=out_specs,
        out_shape=out_shape,
        scratch_shapes=[pltpu.VMEM((2 * GLA_DV, LANES), F32), pltpu.VMEM((2 * GLA_DV, LANES), F32)]
        + dir_scratch + dir_scratch,
        compiler_params=_cparams(("arbitrary", "arbitrary", "arbitrary")),
        name="gla",
    )(qk, qk, v, z, qk, qk, v, z, upf_pad, upb_pad, bias_f, bias_b, tri, tri, s0)


def _swa_block(sink_ref, q_ref, o_ref, row0, kall, vall, prev_ok, next_ok):
    w = ATT_BLOCK
    local = prev_ok is not None
    qrows = pl.ds(row0, w)
    r2 = lax.broadcasted_iota(I32, (2 * w, w), 0) % w
    c2 = lax.broadcasted_iota(I32, (2 * w, w), 1)
    if local:
        bias_prev = jnp.where((c2 >= r2) & prev_ok, 0.0, NEG_BIG)
        bias_next = jnp.where((c2 <= r2) & next_ok, 0.0, NEG_BIG)
    first = c2 < SWA_HD
    top = lax.broadcasted_iota(I32, (2 * w, 1), 0) < w
    scores = []
    for g in range(SWA_KV_HEADS):
        kd = kall[:, g * LANES:(g + 1) * LANES]
        qs = jnp.concatenate([q_ref[qrows, (2 * g) * LANES:(2 * g + 1) * LANES],
                              q_ref[qrows, (2 * g + 1) * LANES:(2 * g + 2) * LANES]], axis=0).astype(F32)
        for half in range(2):
            qm = (jnp.where(first, qs, 0.0) if half == 0 else jnp.where(first, 0.0, qs)).astype(BF16)
            s = lax.dot_general(qm, kd, (((1,), (1,)), ((), ())), preferred_element_type=F32)
            if local:
                s = jnp.concatenate([s[:, :w] + bias_prev, s[:, w:2 * w], s[:, 2 * w:3 * w] + bias_next,
                                     s[:, 3 * w:]], axis=1)
            scores.append(s)
    for g in range(SWA_KV_HEADS):
        outs = []
        for half in range(2):
            s = scores[2 * g + half]
            sk = jnp.where(top, sink_ref[4 * g + half], sink_ref[4 * g + 2 + half]) * LOG2E
            m = jnp.maximum(jnp.max(s, axis=-1, keepdims=True), sk)
            p = jnp.exp2((s - m).astype(BF16))
            va = vall[:, (2 * g + half) * LANES:(2 * g + half + 1) * LANES]
            acc = jnp.dot(p, va, preferred_element_type=F32)
            den = pltpu.roll(acc, SWA_HD, 1) + jnp.exp2(sk - m)
            outs.append(acc / den)
        out = jnp.where(first, outs[0], outs[1])
        o_ref[qrows, (2 * g) * LANES:(2 * g + 1) * LANES] = out[:w].astype(o_ref.dtype)
        o_ref[qrows, (2 * g + 1) * LANES:(2 * g + 2) * LANES] = out[w:].astype(o_ref.dtype)


def _swa_kernel(*refs, nstep, local):
    w = ATT_BLOCK
    if not local:
        sink_ref, q_ref, kx_ref, vx_ref, o_ref = refs
        _swa_block(sink_ref, q_ref, o_ref, 0, kx_ref[...], vx_ref[...], None, None)
        return
    sink_ref, q_ref, kp_ref, kc_ref, kn_ref, vp_ref, vc_ref, vn_ref, kx_ref, vx_ref, o_ref = refs
    i = pl.program_id(1)
    nblk = q_ref.shape[0] // w
    k_blocks = [kp_ref[...]] + [kc_ref[j * w:(j + 1) * w, :] for j in range(nblk)] + [kn_ref[...]]
    v_blocks = [vp_ref[...]] + [vc_ref[j * w:(j + 1) * w, :] for j in range(nblk)] + [vn_ref[...]]
    exists = [i > 0] + [True] * nblk + [i < nstep - 1]
    for sub in range(nblk):
        kall = jnp.concatenate(k_blocks[sub:sub + 3] + [kx_ref[...]], axis=0)
        vall = jnp.concatenate(v_blocks[sub:sub + 3] + [vx_ref[...]], axis=0)
        _swa_block(sink_ref, q_ref, o_ref, sub * w, kall, vall, exists[sub], exists[sub + 2])


def _swa_call(sq, skd, svd, kcd, vcd, sink, batch, seq, lc):
    n = sq.shape[0]
    w = ATT_BLOCK
    nb = seq // w
    nblk = SWA_BLOCKS_PER_STEP if nb % SWA_BLOCKS_PER_STEP == 0 else 1
    nstep = nb // nblk
    kvw = 2 * SWA_KVW

    def pair(b, i):
        return (b * nstep + i, 0)

    def before(b, i):
        return (b * nb + jnp.maximum(nblk * i - 1, 0), 0)

    def after(b, i):
        return (b * nb + jnp.minimum(nblk * (i + 1), nb - 1), 0)

    def kv_specs(width):
        return [pl.BlockSpec((w, width), before), pl.BlockSpec((nblk * w, width), pair),
                pl.BlockSpec((w, width), after)]

    def ctx_spec(width):
        return pl.BlockSpec((lc, width), lambda b, i: (b, 0))

    return pl.pallas_call(
        functools.partial(_swa_kernel, nstep=nstep, local=True),
        grid=(batch, nstep),
        in_specs=[pl.BlockSpec(memory_space=pltpu.SMEM), pl.BlockSpec((nblk * w, SWA_WIDTH), pair)]
        + kv_specs(kvw) + kv_specs(SV_LANES) + [ctx_spec(kvw), ctx_spec(SV_LANES)],
        out_specs=pl.BlockSpec((nblk * w, SWA_WIDTH), pair),
        out_shape=jax.ShapeDtypeStruct((n, SWA_WIDTH), BF16),
        compiler_params=_cparams(("arbitrary", "arbitrary")),
        name="swa",
    )(sink, sq, skd, skd, skd, svd, svd, svd, kcd, vcd)


def _swa_ctx_call(sq, kcd, vcd, sink, batch, lc):
    n = sq.shape[0]
    w = ATT_BLOCK
    nb = lc // w
    kvw = 2 * SWA_KVW
    return pl.pallas_call(
        functools.partial(_swa_kernel, nstep=0, local=False),
        grid=(batch, nb),
        in_specs=[pl.BlockSpec(memory_space=pltpu.SMEM),
                  pl.BlockSpec((w, SWA_WIDTH), lambda b, i: (b * nb + i, 0)),
                  pl.BlockSpec((lc, kvw), lambda b, i: (b, 0)),
                  pl.BlockSpec((lc, SV_LANES), lambda b, i: (b, 0))],
        out_specs=pl.BlockSpec((w, SWA_WIDTH), lambda b, i: (b * nb + i, 0)),
        out_shape=jax.ShapeDtypeStruct((n, SWA_WIDTH), BF16),
        compiler_params=_cparams(("arbitrary", "arbitrary")),
        name="swa_ctx",
    )(sink, sq, kcd, vcd)


def _outproj_kernel(of_ref, ob_ref, gate_ref, gn_ref, swa_ref, w_ref, x_ref, g1_ref, n2_ref, sc_ref, sh_ref,
                    wr_ref, xo_ref, h2_ref, lg_ref):
    o = of_ref[...].astype(F32) + ob_ref[...].astype(F32)
    parts = []
    for h in range(GLA_HEADS):
        oh = o[:, h * GLA_DV:(h + 1) * GLA_DV]
        ms = jnp.mean(oh * oh, axis=-1, keepdims=True)
        parts.append(oh * lax.rsqrt(ms + NORM_EPS))
    on = jnp.concatenate(parts, axis=-1) * gn_ref[...]
    gate = gate_ref[...].astype(F32)
    gla = on * (gate * jax.nn.sigmoid(gate))
    mix = jnp.concatenate([gla.astype(BF16), swa_ref[...]], axis=-1)
    y = jnp.dot(mix, w_ref[...], preferred_element_type=F32)
    xo = x_ref[...] + g1_ref[...] * y
    xo_ref[...] = xo
    ms = jnp.mean(xo * xo, axis=-1, keepdims=True)
    h2 = (xo * lax.rsqrt(ms + NORM_EPS)) * (n2_ref[...] * (1.0 + sc_ref[...])) + sh_ref[...]
    hi = h2.astype(BF16)
    hi_f = hi.astype(F32)
    h2_ref[...] = _pack_rounded_pairs(hi_f)
    lo = (h2 - hi_f).astype(BF16)
    both = jnp.dot(hi, wr_ref[...], preferred_element_type=F32)
    lg = both[:, :LANES] + both[:, LANES:] + jnp.dot(lo, wr_ref[:, :LANES], preferred_element_type=F32)
    lg_ref[...] = jnp.transpose(lg)[:N_EXPERTS, :]


def _outproj_call(o_f, o_b, gate, gn, swa, w_out_b, x2d, mods5, layer, row_of_tile, n2, wr_cat, tm):
    n, d = x2d.shape
    nt = n // tm

    def mod_spec(k):
        return pl.BlockSpec((None, None, None, 1, d), lambda t: (layer, row_of_tile(t), k, 0, 0))

    return pl.pallas_call(
        _outproj_kernel,
        grid=(nt,),
        in_specs=[
            pl.BlockSpec((tm, GLA_WIDTH), lambda t: (t, 0)),
            pl.BlockSpec((tm, GLA_WIDTH), lambda t: (t, 0)),
            pl.BlockSpec((tm, GLA_WIDTH), lambda t: (t, 0)),
            pl.BlockSpec((1, GLA_WIDTH), lambda t: (0, 0)),
            pl.BlockSpec((tm, SWA_WIDTH), lambda t: (t, 0)),
            pl.BlockSpec((None, d, d), lambda t: (layer, 0, 0)),
            pl.BlockSpec((tm, d), lambda t: (t, 0)),
            mod_spec(2),
            pl.BlockSpec((1, d), lambda t: (0, 0)),
            mod_spec(4), mod_spec(3),
            pl.BlockSpec((d, 2 * LANES), lambda t: (0, 0)),
        ],
        out_specs=[
            pl.BlockSpec((tm, d), lambda t: (t, 0)),
            pl.BlockSpec((tm, d // 2), lambda t: (t, 0)),
            pl.BlockSpec((N_EXPERTS, tm), lambda t: (0, t)),
        ],
        out_shape=[
            jax.ShapeDtypeStruct((n, d), F32),
            jax.ShapeDtypeStruct((n, d // 2), U32),
            jax.ShapeDtypeStruct((N_EXPERTS, n), F32),
        ],
        compiler_params=_cparams(("arbitrary",)),
        name="outproj",
    )(o_f, o_b, gate, gn.reshape(1, GLA_WIDTH), swa, w_out_b, x2d, mods5, n2.reshape(1, d), mods5, mods5, wr_cat)


def _first_index(vals, target):
    idx = jnp.full(target.shape, len(vals) - 1, I32)
    for i in range(len(vals) - 2, -1, -1):
        idx = jnp.where(vals[i] == target, i, idx)
    return idx


def _route_kernel(lg_ref, br_ref, io_ref, wo_ref, seg_ref, pad_ref, blk_ref, cnt_scr, seg_scr, pad_scr, *, tn):
    phase = pl.program_id(0)
    step = pl.program_id(1)
    group = lg_ref.shape[1] // tn

    @pl.when((phase == 0) & (step == 0))
    def _():
        cnt_scr[...] = jnp.zeros_like(cnt_scr)
        seg_scr[...] = jnp.zeros_like(seg_scr)
        pad_scr[...] = jnp.zeros_like(pad_scr)

    @pl.when((phase == 1) & (step == 0))
    def _():
        seg_len = jnp.floor((cnt_scr[...] + (SUBLANES - 1)) * (1.0 / SUBLANES)) * SUBLANES
        rows_e = jnp.sum(seg_len, axis=1, keepdims=True)
        blocks = jnp.floor((rows_e + (MOE_BLOCK - 1)) * (1.0 / MOE_BLOCK)) * MOE_BLOCK
        r128 = lax.broadcasted_iota(I32, (LANES, LANES), 0)
        c128 = lax.broadcasted_iota(I32, (LANES, LANES), 1)
        before_tile = jnp.where(r128 < c128, 1.0, 0.0).astype(BF16)
        seg = jnp.dot((seg_len * (1.0 / SUBLANES)).astype(BF16), before_tile,
                      preferred_element_type=F32) * SUBLANES
        run = jnp.zeros((1, 1), F32)
        for e in range(N_EXPERTS):
            seg_scr[e:e + 1, :] = seg[e:e + 1, :] + run
            run = run + blocks[e:e + 1, :]
        pad_scr[...] = seg_len
        seg_ref[...] = seg_scr[...].astype(I32)
        pad_ref[...] = seg_len.astype(I32)
        blk_ref[...] = jnp.broadcast_to(blocks, blk_ref.shape).astype(I32)

    for g in range(group):
        lanes = slice(g * tn, (g + 1) * tn)
        _route_tile(lg_ref[:, lanes], br_ref, io_ref, wo_ref, lanes, step * group + g, phase, cnt_scr, pad_scr)


def _route_tile(lg, br_ref, io_ref, wo_ref, lanes, t, phase, cnt_scr, pad_scr):
    tn = lg.shape[1]
    tile_lane = lax.broadcasted_iota(I32, (N_EXPERTS, LANES), 1)
    s = jax.nn.sigmoid(lg)
    sb = s + br_ref[...]
    rows_s = [s[e:e + 1, :] for e in range(N_EXPERTS)]
    rows_b = [sb[e:e + 1, :] for e in range(N_EXPERTS)]
    gscore, gi1, gi2 = [], [], []
    epg = EXPERTS_PER_GROUP
    for g in range(N_GROUPS):
        a = rows_b[g * epg:(g + 1) * epg]
        m1 = functools.reduce(jnp.maximum, a)
        i1 = _first_index(a, m1)
        rest = [jnp.where(i1 == i, -jnp.inf, a[i]) for i in range(epg)]
        m2 = functools.reduce(jnp.maximum, rest)
        i2 = _first_index(rest, m2)
        gscore.append(m1 + m2)
        gi1.append(i1)
        gi2.append(i2)
    gm = functools.reduce(jnp.maximum, gscore)
    gsel = _first_index(gscore, gm)
    i1 = gi1[N_GROUPS - 1]
    i2 = gi2[N_GROUPS - 1]
    for g in range(N_GROUPS - 2, -1, -1):
        i1 = jnp.where(gsel == g, gi1[g], i1)
        i2 = jnp.where(gsel == g, gi2[g], i2)
    idx0 = gsel * epg + i1
    idx1 = gsel * epg + i2
    s0 = jnp.zeros_like(rows_s[0])
    s1 = jnp.zeros_like(rows_s[0])
    for e in range(N_EXPERTS):
        s0 = jnp.where(idx0 == e, rows_s[e], s0)
        s1 = jnp.where(idx1 == e, rows_s[e], s1)
    tot = s0 + s1
    w0 = s0 / tot
    w1 = s1 / tot

    eidx = lax.broadcasted_iota(I32, (N_EXPERTS, tn), 0)
    oh0 = eidx == idx0
    oh1 = eidx == idx1
    oh = jnp.where(oh0 | oh1, 1.0, 0.0)

    @pl.when(phase == 0)
    def _():
        cnt_scr[...] = cnt_scr[...] + jnp.where(tile_lane == t, jnp.sum(oh, axis=1, keepdims=True), 0.0)

    @pl.when(phase == 1)
    def _():
        rr = lax.broadcasted_iota(I32, (tn, tn), 0)
        cc = lax.broadcasted_iota(I32, (tn, tn), 1)
        upper = jnp.where(rr < cc, 1.0, 0.0).astype(BF16)
        before = jnp.dot(oh.astype(BF16), upper, preferred_element_type=F32)
        pad_col = jnp.sum(jnp.where(tile_lane == t, pad_scr[...], 0.0), axis=1, keepdims=True)
        run = jnp.zeros((1, 1), F32)
        offs = []
        for e in range(N_EXPERTS):
            offs.append(run)
            run = run + pad_col[e:e + 1, :]
        pos = before + jnp.concatenate(offs, axis=0)
        spos0 = jnp.sum(jnp.where(oh0, pos, 0.0), axis=0, keepdims=True)
        spos1 = jnp.sum(jnp.where(oh1, pos, 0.0), axis=0, keepdims=True)
        zi = jnp.zeros((SUBLANES - 4, tn), I32)
        io_ref[:, lanes] = jnp.concatenate([idx0, idx1, spos0.astype(I32), spos1.astype(I32), zi], axis=0)
        wo_ref[:, lanes] = jnp.concatenate([w0, w1, jnp.zeros((SUBLANES - 2, tn), F32)], axis=0)


def _route_call(logits_t, b_router, tn):
    ne, t_all = logits_t.shape
    ntile = t_all // tn
    assert ntile <= LANES
    group = max(g for g in range(1, ROUTE_TILES_PER_STEP + 1) if ntile % g == 0)
    tw = group * tn
    tok_out = pl.BlockSpec((SUBLANES, tw), lambda p, t: (0, t * p))
    tab_out = pl.BlockSpec((ne, LANES), lambda p, t: (0, 0))
    tab = jax.ShapeDtypeStruct((ne, LANES), I32)
    return pl.pallas_call(
        functools.partial(_route_kernel, tn=tn),
        grid=(2, ntile // group),
        in_specs=[pl.BlockSpec((ne, tw), lambda p, t: (0, t)), pl.BlockSpec((ne, 1), lambda p, t: (0, 0))],
        out_specs=[tok_out, tok_out, tab_out, tab_out, tab_out],
        out_shape=[
            jax.ShapeDtypeStruct((SUBLANES, t_all), I32),
            jax.ShapeDtypeStruct((SUBLANES, t_all), F32),
            tab, tab, tab,
        ],
        scratch_shapes=[pltpu.VMEM((ne, LANES), F32)] * 3,
        compiler_params=_cparams(("arbitrary", "arbitrary")),
        name="route",
    )(logits_t, b_router.reshape(ne, 1))


def _segment_copies(seg_ref, pad_ref, t, stage, hbm, sem, to_hbm, wait):
    if wait:
        total = functools.reduce(lambda a, b: a + b, [pad_ref[e * LANES + t] for e in range(N_EXPERTS)])
        for size in (2 * SEG_SIZES[0],) + SEG_SIZES:
            @pl.when((total & size) != 0)
            def _(size=size):
                src, dst = stage.at[pl.ds(0, size), :], hbm.at[pl.ds(0, size), :]
                if not to_hbm:
                    src, dst = dst, src
                pltpu.make_async_copy(src, dst, sem).wait()
        return
    loc = 0
    for e in range(N_EXPERTS):
        n = pad_ref[e * LANES + t]
        start = seg_ref[e * LANES + t]
        for size in SEG_SIZES:
            off = n & (-2 * size)

            @pl.when((n & size) != 0)
            def _(off=off, size=size, loc=loc, start=start):
                s_rows = pl.ds(pl.multiple_of(loc + off, SUBLANES), size)
                h_rows = pl.ds(pl.multiple_of(start + off, SUBLANES), size)
                src, dst = (stage.at[s_rows, :], hbm.at[h_rows, :])
                if not to_hbm:
                    src, dst = dst, src
                pltpu.make_async_copy(src, dst, sem).start()
        loc = loc + n


def _sort_matrix(io_ref, rows):
    r = lax.broadcasted_iota(I32, (rows, io_ref.shape[1]), 0)
    return r == io_ref[2:3, :], r == io_ref[3:4, :]


def _dispatch_kernel(seg_ref, pad_ref, zblk_ref, *refs, tiles):
    srcs = refs[:len(tiles)]
    io_ref, wo_ref, xb_hbm, stage, zbuf, sem = refs[len(tiles):]
    i = pl.program_id(0)
    m = MOE_BLOCK

    def zero_copy(j):
        start = pl.multiple_of(zblk_ref[j] * m, m)
        return pltpu.make_async_copy(zbuf, xb_hbm.at[pl.ds(start, m), :], sem.at[0])

    @pl.when(i == 0)
    def _():
        zbuf[...] = jnp.zeros_like(zbuf)
        for j in range(zblk_ref.shape[0]):
            @pl.when(zblk_ref[j] >= 0)
            def _():
                zero_copy(j).start()
        for j in range(zblk_ref.shape[0]):
            @pl.when(zblk_ref[j] >= 0)
            def _():
                zero_copy(j).wait()

    hp = srcs[0][...]
    if len(srcs) == 2:
        hp = jnp.where(i < tiles[0], hp, srcs[1][...])
    x = _unpack_bf16_pairs(hp)
    m0, m1 = _sort_matrix(io_ref, STAGE_ROWS)
    sort = jnp.where(m0 | m1, 1.0, 0.0).astype(BF16)
    xs = jnp.dot(sort, x, preferred_element_type=F32)
    dh = x.shape[1] // 2
    slot = i % 2
    cur = stage.at[slot]
    cur[:, :dh] = _pack_rounded_pairs(xs)
    ws = jnp.sum(jnp.where(m0, wo_ref[0:1, :], 0.0) + jnp.where(m1, wo_ref[1:2, :], 0.0), axis=1, keepdims=True)
    cur[:, dh:] = jnp.broadcast_to(lax.bitcast_convert_type(ws, U32), (STAGE_ROWS, LANES))
    _segment_copies(seg_ref, pad_ref, i, cur, xb_hbm, sem.at[slot], to_hbm=True, wait=False)

    @pl.when(i > 0)
    def _():
        _segment_copies(seg_ref, pad_ref, i - 1, stage.at[1 - slot], xb_hbm, sem.at[1 - slot], to_hbm=True, wait=True)

    @pl.when(i == pl.num_programs(0) - 1)
    def _():
        _segment_copies(seg_ref, pad_ref, i, cur, xb_hbm, sem.at[slot], to_hbm=True, wait=True)


def _dispatch_call(seg, pad, zblk, io, wo, sources, p_rows):
    tile = ROUTE_TILE
    dh = sources[0].shape[1]
    tiles = tuple(s.shape[0] // tile for s in sources)
    firsts = tuple(sum(tiles[:k]) for k in range(len(tiles)))

    def src_spec(first, ntile):
        return pl.BlockSpec((tile, dh), lambda i, *_: (jnp.clip(i - first, 0, ntile - 1), 0))

    tok_spec = pl.BlockSpec((SUBLANES, tile), lambda i, *_: (0, i))
    grid_spec = pltpu.PrefetchScalarGridSpec(
        num_scalar_prefetch=3,
        grid=(sum(tiles),),
        in_specs=[src_spec(f, n) for f, n in zip(firsts, tiles)] + [tok_spec, tok_spec],
        out_specs=pl.BlockSpec(memory_space=pl.ANY),
        scratch_shapes=[pltpu.VMEM((2, STAGE_ROWS, dh + LANES), U32), pltpu.VMEM((MOE_BLOCK, dh + LANES), U32),
                        pltpu.SemaphoreType.DMA((2,))],
    )
    return pl.pallas_call(
        functools.partial(_dispatch_kernel, tiles=tiles),
        grid_spec=grid_spec,
        out_shape=jax.ShapeDtypeStruct((p_rows, dh + LANES), U32),
        compiler_params=_cparams(("arbitrary",)),
        name="dispatch",
    )(seg, pad, zblk, *sources, io, wo)


def _expert_kernel(be_ref, nu_ref, va_ref, ne_ref, x_ref, wg_hbm, wu_hbm, wd_hbm, o_ref,
                   wg_b, wu_b, wd_b, wg_f, wu_f, wd_f, switch_ref, sem, *, layer):
    j = pl.program_id(0)
    dh = o_ref.shape[1]
    half = o_ref.shape[0] // 2
    valid = va_ref[j]
    e = be_ref[j]

    def weight_copies(expert, slot):
        return [pltpu.make_async_copy(hbm.at[layer, expert], buf.at[slot], sem.at[slot, k])
                for k, (hbm, buf) in enumerate(((wg_hbm, wg_f), (wu_hbm, wu_f), (wd_hbm, wd_f)))]

    @pl.when(j == 0)
    def _():
        switch_ref[0] = 0
        for cp in weight_copies(e, 0):
            cp.start()

    @pl.when((j == 0) | (e != be_ref[jnp.maximum(j - 1, 0)]))
    def _():
        slot = switch_ref[0] % 2
        for cp in weight_copies(e, slot):
            cp.wait()
        wg_b[...] = wg_f[slot].astype(BF16)
        wu_b[...] = wu_f[slot].astype(BF16)
        wd_b[...] = wd_f[slot].astype(BF16)
        nxt = ne_ref[e]

        @pl.when(nxt >= 0)
        def _():
            for cp in weight_copies(nxt, 1 - slot):
                cp.start()
        switch_ref[0] = switch_ref[0] + 1

    def ffn_halves(halves):
        gates = []
        for r in halves:
            x = _unpack_bf16_pairs(x_ref[r * half:(r + 1) * half, :dh])
            gates.append((jnp.dot(x, wg_b[...], preferred_element_type=F32),
                          jnp.dot(x, wu_b[...], preferred_element_type=F32)))
        for r, (a, u) in zip(halves, gates):
            hmid = (a * jax.nn.sigmoid(a)) * u
            y = jnp.dot(hmid.astype(BF16), wd_b[...], preferred_element_type=F32)
            row_w = lax.bitcast_convert_type(x_ref[r * half:(r + 1) * half, dh:dh + 1], F32)
            o_ref[r * half:(r + 1) * half, :] = _pack_bf16_pairs(y * row_w)

    @pl.when(valid > half)
    def _():
        ffn_halves((0, 1))

    @pl.when((valid > 0) & (valid <= half))
    def _():
        ffn_halves((0,))
        o_ref[half:, :] = jnp.zeros((half, dh), o_ref.dtype)

    @pl.when(valid == 0)
    def _():
        o_ref[...] = jnp.zeros_like(o_ref)


def _expert_call(block_e, nused, valid, next_e, xb, wg, wu, wd, layer):
    p_rows = xb.shape[0]
    dh = xb.shape[1] - LANES
    d = 2 * dh
    m = MOE_BLOCK
    nb = p_rows // m
    de = wg.shape[-1]

    def xmap(j, be, nu, va, ne):
        return (jnp.minimum(j, nu[0] - 1), 0)

    any_spec = pl.BlockSpec(memory_space=pl.ANY)
    grid_spec = pltpu.PrefetchScalarGridSpec(
        num_scalar_prefetch=4,
        grid=(nb,),
        in_specs=[pl.BlockSpec((m, dh + LANES), xmap), any_spec, any_spec, any_spec],
        out_specs=pl.BlockSpec((m, dh), lambda j, be, nu, va, ne: (j, 0)),
        scratch_shapes=[
            pltpu.VMEM((d, de), BF16), pltpu.VMEM((d, de), BF16), pltpu.VMEM((de, d), BF16),
            pltpu.VMEM((2, d, de), F32), pltpu.VMEM((2, d, de), F32), pltpu.VMEM((2, de, d), F32),
            pltpu.SMEM((1,), I32), pltpu.SemaphoreType.DMA((2, 3)),
        ],
    )
    return pl.pallas_call(
        functools.partial(_expert_kernel, layer=layer),
        grid_spec=grid_spec,
        out_shape=jax.ShapeDtypeStruct((p_rows, dh), U32),
        compiler_params=_cparams(("arbitrary",)),
        name="experts",
    )(block_e, nused, valid, next_e, xb, wg, wu, wd)


def _combine_kernel(*refs, tile0, final):
    if final:
        seg_ref, pad_ref, yb_hbm, io_ref, x_ref, g2_ref, fn_ref, o_ref, stage, sem = refs
    else:
        seg_ref, pad_ref, yb_hbm, io_ref, x_ref, g2_ref, o_ref, stage, sem = refs
    i = pl.program_id(0)

    slot = i % 2

    @pl.when(i == 0)
    def _():
        stage[...] = jnp.zeros_like(stage)
        _segment_copies(seg_ref, pad_ref, tile0, stage.at[0], yb_hbm, sem.at[0], to_hbm=False, wait=False)

    @pl.when(i + 1 < pl.num_programs(0))
    def _():
        _segment_copies(seg_ref, pad_ref, tile0 + i + 1, stage.at[1 - slot], yb_hbm, sem.at[1 - slot],
                        to_hbm=False, wait=False)

    _segment_copies(seg_ref, pad_ref, tile0 + i, stage.at[slot], yb_hbm, sem.at[slot], to_hbm=False, wait=True)
    rows = _unpack_bf16_pairs(stage[slot])
    m0, m1 = _sort_matrix(io_ref, STAGE_ROWS)
    pick = jnp.where(m0 | m1, 1.0, 0.0).astype(BF16)
    y = lax.dot_general(pick, rows, (((0,), (0,)), ((), ())), preferred_element_type=F32)
    xo = x_ref[...] + g2_ref[...] * y
    if final:
        ms = jnp.mean(xo * xo, axis=-1, keepdims=True)
        xo = (xo * lax.rsqrt(ms + NORM_EPS)) * fn_ref[...]
    o_ref[...] = xo


def _combine_call(seg, pad, yb, io, x2d, mods5, layer, row_of_tile, final_g, tile0):
    n, d = x2d.shape
    tm = ROUTE_TILE
    final = final_g is not None
    in_specs = [
        pl.BlockSpec(memory_space=pl.ANY),
        pl.BlockSpec((SUBLANES, tm), lambda t, *_: (0, tile0 + t)),
        pl.BlockSpec((tm, d), lambda t, *_: (t, 0)),
        pl.BlockSpec((None, None, None, 1, d), lambda t, *_: (layer, row_of_tile(t), 5, 0, 0)),
    ]
    args = [yb, io, x2d, mods5]
    if final:
        in_specs.append(pl.BlockSpec((1, d), lambda t, *_: (0, 0)))
        args.append(final_g.reshape(1, d))
    grid_spec = pltpu.PrefetchScalarGridSpec(
        num_scalar_prefetch=2,
        grid=(n // tm,),
        in_specs=in_specs,
        out_specs=pl.BlockSpec((tm, d), lambda t, *_: (t, 0)),
        scratch_shapes=[pltpu.VMEM((2, STAGE_ROWS, d // 2), U32), pltpu.SemaphoreType.DMA((2,))],
    )
    return pl.pallas_call(
        functools.partial(_combine_kernel, tile0=tile0, final=final),
        grid_spec=grid_spec,
        out_shape=jax.ShapeDtypeStruct((n, d), F32),
        compiler_params=_cparams(("arbitrary",)),
        name="combine",
    )(seg, pad, *args)


def _pack_w_kernel(w_ref, o_ref):
    cut = C_Z + 2 * GLA_GATE_RANK
    o_ref[:, :C_Z] = w_ref[:, :C_Z].astype(BF16)
    gate_tile = w_ref[:, C_Z:C_Z + LANES]
    lane = lax.broadcasted_iota(I32, gate_tile.shape, 1)
    o_ref[:, C_Z:C_SQ] = jnp.where(lane < 2 * GLA_GATE_RANK, gate_tile, 0.0).astype(BF16)
    o_ref[:, C_SQ:] = w_ref[:, cut:].astype(BF16)


def _pack_w_in(w):
    assert sum(IN_SIZES[:6]) == C_Z + 2 * GLA_GATE_RANK and sum(IN_SIZES) - sum(IN_SIZES[:6]) == C_END - C_SQ
    depth, d, cols = w.shape
    rows = ROW_TILE // 2
    return pl.pallas_call(
        _pack_w_kernel,
        grid=(depth, d // rows),
        in_specs=[pl.BlockSpec((None, rows, cols), lambda l, r: (l, r, 0))],
        out_specs=pl.BlockSpec((None, rows, C_END), lambda l, r: (l, r, 0)),
        out_shape=jax.ShapeDtypeStruct((depth, d, C_END), BF16),
        compiler_params=_cparams(("arbitrary", "arbitrary")),
        name="pack_w_in",
    )(w)


def _pad_up(up, row0):
    up = up.reshape(GLA_GATE_RANK, GLA_HEADS // 2, LANES).transpose(1, 0, 2)
    hi = up.astype(BF16)
    lo = (up - hi.astype(F32)).astype(BF16)
    out = jnp.zeros((GLA_HEADS // 2, LANES, LANES), BF16)
    for group, part in enumerate((hi, hi, lo)):
        r = group * 2 * GLA_GATE_RANK + row0
        out = out.at[:, r:r + GLA_GATE_RANK, :].set(part)
    return out


def _rope_tables(seq):
    rows = seq // GRID_W
    dim = jnp.arange(LANES, dtype=I32) % SWA_HD
    inv = ROPE_THETA ** (-((dim % ROPE_HALF).astype(F32) * 2.0 / ROPE_AXIS_DIM))
    sign = jnp.where(dim % ROPE_AXIS_DIM < ROPE_HALF, -1.0, 1.0)
    by_row = (dim < ROPE_AXIS_DIM)[None, None, :]
    ang_r = (jnp.arange(rows, dtype=F32)[:, None] * inv[None, :])[:, None, :]
    ang_c = (jnp.arange(GRID_W, dtype=F32)[:, None] * inv[None, :])[None, :, :]
    cos = jnp.where(by_row, jnp.cos(ang_r), jnp.cos(ang_c))
    sin = jnp.where(by_row, jnp.sin(ang_r), jnp.sin(ang_c)) * sign
    return cos.reshape(seq, LANES), sin.reshape(seq, LANES)


def _tile(n, pref):
    t = pref
    while n % t:
        t //= 2
    return t


def kernel(x, c, ctx, c_ctx, w_ada, b_ada, norm1, norm2, w_in, gla_up_f, gla_bias_f, gla_up_b, gla_bias_b,
           gla_norm, swa_sink, w_out, w_router, b_router, w_gate, w_up, w_down, final_norm):
    batch, seq, d = x.shape
    lc = ctx.shape[1]
    depth = w_ada.shape[0]
    n_l = batch * seq
    n_c = batch * lc
    npair = GLA_HEADS // 2
    assert batch + 1 <= SUBLANES
    assert n_l % ROUTE_TILE == 0 and n_c % ROUTE_TILE == 0

    tm_l = _tile(seq, ROW_TILE)
    tm_in = _tile(seq, INPROJ_TILE)
    tm_c = _tile(lc, CTX_TILE)
    tl_l = _tile(seq, GLA_TILE)
    tl_c = _tile(lc, ROW_TILE)

    rows = jnp.zeros((SUBLANES, d), F32).at[:batch].set(c).at[batch].set(c_ctx)
    mods = _ada_call(rows, w_ada, b_ada)
    mods5 = mods.reshape(depth, SUBLANES, 6, 1, d)

    tabs = _rope_tables(seq)
    w_packed = _pack_w_in(w_in)
    w_out_b = w_out.astype(BF16)
    wr_pad = jnp.zeros((d, LANES), F32).at[:, :N_EXPERTS].set(w_router)
    wr_hi = wr_pad.astype(BF16)
    wr_lo = (wr_pad - wr_hi.astype(F32)).astype(BF16)
    wr_cat = jnp.concatenate([wr_hi, wr_lo], axis=-1)

    def lat_row(tm):
        return lambda t: t // (seq // tm)

    def ctx_row(tm):
        return lambda t: batch

    xl = x.reshape(n_l, d)
    xc = ctx.reshape(n_c, d)
    for i in range(depth):
        last = i == depth - 1
        upf = _pad_up(gla_up_f[i], 0)
        upb = _pad_up(gla_up_b[i], GLA_GATE_RANK)
        bias_f = gla_bias_f[i].reshape(npair, 1, LANES)
        bias_b = gla_bias_b[i].reshape(npair, 1, LANES)

        c_qk, c_v, c_g, c_z, c_sq, c_sk, c_sv = _inproj_call(
            xc, norm1[i], mods5, i, ctx_row(tm_c), w_packed, None, tm_c, lc // tm_c)
        l_qk, l_v, l_g, l_z, l_sq, l_sk, l_sv = _inproj_call(
            xl, norm1[i], mods5, i, lat_row(tm_in), w_packed, tabs, tm_in, seq // tm_in)

        s_zero = jnp.zeros((batch, npair, 2, 2 * GLA_DV, LANES), F32)
        oc_f, oc_b, s_ctx = _gla_call(c_qk, c_v, c_z, upf, upb, bias_f, bias_b, s_zero, batch, lc, tl_c)
        ol_f, ol_b, _ = _gla_call(l_qk, l_v, l_z, upf, upb, bias_f, bias_b, s_ctx, batch, seq, tl_l)

        swa_l = _swa_call(l_sq, l_sk, l_sv, c_sk, c_sv, swa_sink[i], batch, seq, lc)
        xl, h2l, lg_l = _outproj_call(ol_f, ol_b, l_g, gla_norm[i], swa_l, w_out_b, xl, mods5, i, lat_row(tm_l),
                                      norm2[i], wr_cat, tm_l)
        if last:
            logits_t = lg_l
            t_all = n_l
        else:
            swa_c = _swa_ctx_call(c_sq, c_sk, c_sv, swa_sink[i], batch, lc)
            xc, h2c, lg_c = _outproj_call(oc_f, oc_b, c_g, gla_norm[i], swa_c, w_out_b, xc, mods5, i, ctx_row(tm_c),
                                          norm2[i], wr_cat, tm_c)
            logits_t = jnp.concatenate([lg_l, lg_c], axis=1)
            t_all = n_l + n_c

        io, wo, seg, pad, blk = _route_call(logits_t, b_router, ROUTE_TILE)
        m = MOE_BLOCK
        padded = blk[:, 0]
        pend = jnp.cumsum(padded)
        ntile = t_all // ROUTE_TILE
        nb = -(-(t_all * TOP_K + ntile * N_EXPERTS * (SUBLANES - 1)) // m) + N_EXPERTS
        p_rows = nb * m
        seg = seg.reshape(-1)
        pad = pad.reshape(-1)
        nused = (pend[-1:] // m).astype(I32)
        tail = nused + jnp.arange(nb - (t_all * TOP_K) // m, dtype=I32)
        zblk = jnp.concatenate([jnp.where(padded > 0, pend // m - 1, -1),
                                jnp.where(tail < nb, tail, -1)]).astype(I32)
        blk_row = jnp.minimum(jnp.arange(nb, dtype=I32), nused - 1) * m
        block_e = jnp.sum((pend[None, :] <= blk_row[:, None]).astype(I32), axis=1)
        rows_e = jnp.sum(pad.reshape(N_EXPERTS, LANES), axis=1)
        own = (block_e[:, None] == jnp.arange(N_EXPERTS, dtype=I32)[None, :]).astype(I32)
        seg_end = jnp.sum(own * (pend - padded + rows_e)[None, :], axis=1)
        blk_idx = jnp.arange(nb, dtype=I32)
        valid = jnp.where(blk_idx < nused, jnp.clip(seg_end - blk_idx * m, 0, m), 0).astype(I32)

        sources = (h2l,) if last else (h2l, h2c)
        xb = _dispatch_call(seg, pad, zblk, io, wo, sources, p_rows)
        eid = jnp.arange(N_EXPERTS, dtype=I32)
        later = (eid[None, :] > eid[:, None]) & (padded[None, :] > 0)
        next_e = jnp.min(jnp.where(later, eid[None, :], N_EXPERTS), axis=1)
        next_e = jnp.where(next_e < N_EXPERTS, next_e, -1).astype(I32)
        yb = _expert_call(block_e, nused, valid, next_e, xb, w_gate, w_up, w_down, i)
        xl_new = _combine_call(seg, pad, yb, io, xl, mods5, i, lat_row(ROUTE_TILE),
                               final_norm if last else None, 0)
        if not last:
            xc = _combine_call(seg, pad, yb, io, xc, mods5, i, ctx_row(0), None, n_l // ROUTE_TILE)
        xl = xl_new
    return xl.reshape(batch, seq, d)
```

```python
import functools

import jax
import jax.numpy as jnp
from jax import lax
from jax.experimental import pallas as pl
from jax.experimental.pallas import tpu as pltpu

F32 = jnp.float32
BF16 = jnp.bfloat16
I32 = jnp.int32
U32 = jnp.uint32

GRID_W = 64
NORM_EPS = 1e-6
GLA_HEADS = 4
GLA_DK = 64
GLA_DV = 128
GLA_KW = GLA_HEADS * GLA_DK
GLA_WIDTH = GLA_HEADS * GLA_DV
GLA_GATE_RANK = 16
GLA_TAU = 16.0
GLA_CHUNK = 64
SWA_HEADS = 8
SWA_KV_HEADS = 2
SWA_HD = 64
SWA_WIDTH = SWA_HEADS * SWA_HD
SWA_KVW = SWA_KV_HEADS * SWA_HD
WINDOW = 128
ATT_BLOCK = 128
ROPE_THETA = 10000.0
ROPE_AXIS_DIM = SWA_HD // 2
ROPE_HALF = ROPE_AXIS_DIM // 2
N_EXPERTS = 16
N_GROUPS = 4
EXPERTS_PER_GROUP = N_EXPERTS // N_GROUPS
TOP_K = 2
IN_SIZES = (GLA_KW, GLA_KW, GLA_WIDTH, GLA_WIDTH, GLA_GATE_RANK, GLA_GATE_RANK, SWA_WIDTH, SWA_KVW, SWA_KVW)

LANES = 128
SUBLANES = 8
VMEM_LIMIT = 48 * 1024 * 1024

ROW_TILE = 1024
GLA_TILE = 2048
GLA_SUM_ROWS = 256
INPROJ_TILE = 1024
CTX_TILE = 256
ADA_COLS = 1536
SWA_BLOCKS_PER_STEP = 8
MOE_BLOCK = 512
ROUTE_TILE = 512
ROUTE_TILES_PER_STEP = 11
SEG_SIZES = (512, 256, 128, 64, 32, 16, 8)
STAGE_ROWS = TOP_K * ROUTE_TILE + LANES

C_QK = 0
C_V = 512
C_G = 1024
C_Z = 1536
C_SQ = 1664
C_SK = 2176
C_SV = 2304
C_END = 2432
SV_LANES = 4 * LANES
NEG_BIG = -1e30
LOG2E = 1.4426950408889634

assert WINDOW == ATT_BLOCK


def _cparams(sem):
    return pltpu.CompilerParams(dimension_semantics=sem, vmem_limit_bytes=VMEM_LIMIT)


def _pack_bf16_pairs(x):
    return _pack_rounded_pairs(x.astype(BF16).astype(F32))


def _pack_rounded_pairs(xr):
    n = xr.shape[1] // 2
    lo = lax.bitcast_convert_type(xr[:, :n], U32)
    hi = lax.bitcast_convert_type(xr[:, n:], U32)
    return (lo >> 16) | hi


def _unpack_bf16_pairs(w):
    lo = lax.bitcast_convert_type(w << 16, F32)
    hi = lax.bitcast_convert_type(w & jnp.uint32(0xFFFF0000), F32)
    return jnp.concatenate([lo.astype(BF16), hi.astype(BF16)], axis=-1)


def _ada_kernel(a_ref, w_ref, b_ref, o_ref):
    a = a_ref[...]
    act = a * jax.nn.sigmoid(a)
    a_hi = act.astype(BF16)
    a_lo = (act - a_hi.astype(F32)).astype(BF16)
    w = w_ref[...]
    w_hi = w.astype(BF16)
    w_lo = (w - w_hi.astype(F32)).astype(BF16)
    t = jnp.dot(jnp.concatenate([a_hi, a_lo], axis=0), w_hi, preferred_element_type=F32)
    n = a.shape[0]
    o_ref[...] = t[:n] + t[n:] + jnp.dot(a_hi, w_lo, preferred_element_type=F32) + b_ref[...]


def _ada_call(rows, w_ada, b_ada):
    depth, d, n6 = w_ada.shape
    tn = ADA_COLS
    return pl.pallas_call(
        _ada_kernel,
        grid=(depth, n6 // tn),
        in_specs=[
            pl.BlockSpec((SUBLANES, d), lambda l, j: (0, 0)),
            pl.BlockSpec((None, d, tn), lambda l, j: (l, 0, j)),
            pl.BlockSpec((None, 1, tn), lambda l, j: (l, 0, j)),
        ],
        out_specs=pl.BlockSpec((None, SUBLANES, tn), lambda l, j: (l, 0, j)),
        out_shape=jax.ShapeDtypeStruct((depth, SUBLANES, n6), F32),
        compiler_params=_cparams(("arbitrary", "arbitrary")),
        name="adaln",
    )(rows, w_ada, b_ada.reshape(depth, 1, n6))


def _rope_tile(xj, cos, sin, lane_lo):
    partner = jnp.where(lane_lo, pltpu.roll(xj, LANES - ROPE_HALF, 1), pltpu.roll(xj, ROPE_HALF, 1))
    return xj * cos + partner * sin


def _inproj_kernel(*refs, rope):
    if rope:
        (x_ref, g_ref, sc_ref, sh_ref, w_ref, cos_ref, sin_ref,
         qk_ref, v_ref, gate_ref, z_ref, sq_ref, sk_ref, sv_ref) = refs
    else:
        (x_ref, g_ref, sc_ref, sh_ref, w_ref,
         qk_ref, v_ref, gate_ref, z_ref, sq_ref, sk_ref, sv_ref) = refs
    half = x_ref.shape[0] // 2
    scale = g_ref[...] * (1.0 + sc_ref[...])
    normed = []
    for r in range(2):
        x = x_ref[r * half:(r + 1) * half, :]
        ms = jnp.mean(x * x, axis=-1, keepdims=True)
        normed.append(((x * lax.rsqrt(ms + NORM_EPS)) * scale + sh_ref[...]).astype(BF16))

    for r in range(2):
        rows = slice(r * half, (r + 1) * half)
        hb = normed[r]

        def proj(a, b):
            return jnp.dot(hb, w_ref[:, a:b], preferred_element_type=F32)

        qk = proj(C_QK, C_V)
        qk_ref[rows, :GLA_KW] = (qk[:, :GLA_KW] * (GLA_DK ** -0.5)).astype(qk_ref.dtype)
        qk_ref[rows, GLA_KW:] = qk[:, GLA_KW:].astype(qk_ref.dtype)
        v_ref[rows, :] = proj(C_V, C_G).astype(v_ref.dtype)
        gate_ref[rows, :] = proj(C_G, C_Z).astype(gate_ref.dtype)
        z_ref[rows, :] = proj(C_Z, C_SQ)
        sq = proj(C_SQ, C_SK) * (SWA_HD ** -0.5 * LOG2E)
        sk = proj(C_SK, C_SV)
        sv = proj(C_SV, C_END)
        if rope:
            cos = cos_ref[rows, :]
            sin = sin_ref[rows, :]
            lane_lo = (lax.broadcasted_iota(I32, cos.shape, 1) % ROPE_AXIS_DIM) < ROPE_HALF
            for j in range(SWA_WIDTH // LANES):
                sl = slice(j * LANES, (j + 1) * LANES)
                sq_ref[rows, sl] = _rope_tile(sq[:, sl], cos, sin, lane_lo).astype(sq_ref.dtype)
            sk = _rope_tile(sk, cos, sin, lane_lo)
        else:
            sq_ref[rows, :] = sq.astype(sq_ref.dtype)
        first = lax.broadcasted_iota(I32, sk.shape, 1) < SWA_HD
        sk_sw = pltpu.roll(sk, SWA_HD, 1)
        sk_ref[rows, :LANES] = jnp.where(first, sk, sk_sw).astype(sk_ref.dtype)
        sk_ref[rows, LANES:] = jnp.where(first, sk_sw, sk).astype(sk_ref.dtype)
        sv_sw = pltpu.roll(sv, SWA_HD, 1)
        sv_ref[rows, 0 * LANES:1 * LANES] = jnp.where(first, sv, 1.0).astype(sv_ref.dtype)
        sv_ref[rows, 1 * LANES:2 * LANES] = jnp.where(first, 1.0, sv_sw).astype(sv_ref.dtype)
        sv_ref[rows, 2 * LANES:3 * LANES] = jnp.where(first, sv_sw, 1.0).astype(sv_ref.dtype)
        sv_ref[rows, 3 * LANES:4 * LANES] = jnp.where(first, 1.0, sv).astype(sv_ref.dtype)


def _inproj_call(x2d, g, mods5, layer, row_of_tile, w_packed, tabs, tm, tiles_per_seq):
    n, d = x2d.shape
    nt = n // tm
    rope = tabs is not None

    def mod_spec(k):
        return pl.BlockSpec((None, None, None, 1, d), lambda t: (layer, row_of_tile(t), k, 0, 0))

    in_specs = [
        pl.BlockSpec((tm, d), lambda t: (t, 0)),
        pl.BlockSpec((1, d), lambda t: (0, 0)),
        mod_spec(1), mod_spec(0),
        pl.BlockSpec((None, d, C_END), lambda t: (layer, 0, 0)),
    ]
    args = [x2d, g.reshape(1, d), mods5, mods5, w_packed]
    if rope:
        in_specs += [pl.BlockSpec((tm, LANES), lambda t: (t % tiles_per_seq, 0))] * 2
        args += [tabs[0], tabs[1]]
    widths = (2 * GLA_KW, GLA_WIDTH, GLA_WIDTH, LANES, SWA_WIDTH, 2 * SWA_KVW, SV_LANES)
    dtypes = (BF16, BF16, BF16, F32, BF16, BF16, BF16)
    return pl.pallas_call(
        functools.partial(_inproj_kernel, rope=rope),
        grid=(nt,),
        in_specs=in_specs,
        out_specs=[pl.BlockSpec((tm, w), lambda t: (t, 0)) for w in widths],
        out_shape=[jax.ShapeDtypeStruct((n, w), dt) for w, dt in zip(widths, dtypes)],
        compiler_params=_cparams(("arbitrary",)),
        name="inproj",
    )(*args)


def _log_sigmoid(x):
    return jnp.minimum(x, 0.0) - jnp.log(1.0 + jnp.exp(-jnp.abs(x)))


def _gla_prepare(q_ref, k_ref, v_ref, z_ref, up_ref, bias_ref, tri_ref, scr, fwd):
    qin_s, km0_s, km1_s, kout_s, vm0_s, vm1_s, dec_s = scr[:7]
    qbd_s = scr[9]
    tl = q_ref.shape[0]
    c = GLA_CHUNK
    z = z_ref[...]
    z_hi = z.astype(BF16).astype(F32)
    zc = z_hi + pltpu.roll(z - z_hi, 2 * GLA_GATE_RANK, 1) + pltpu.roll(z_hi, 4 * GLA_GATE_RANK, 1)
    x = jnp.dot(zc.astype(BF16), up_ref[...], preferred_element_type=F32) + bias_ref[...]
    la = _log_sigmoid(x) * (LOG2E / GLA_TAU)
    hi = la.astype(BF16)
    lo = (la - hi.astype(F32)).astype(BF16)
    hl = jnp.concatenate([hi, lo], axis=1)
    tg = tri_ref.shape[0]
    sums = jnp.concatenate([jnp.dot(tri_ref[...], hl[r0:r0 + tg, :], preferred_element_type=F32)
                            for r0 in range(0, tl, tg)], axis=0)
    b = sums[:, :LANES] + sums[:, LANES:]
    b3 = b.reshape(tl // c, c, LANES)
    edge = b3[:, c - 1:c, :] if fwd else b3[:, 0:1, :]
    btot = jnp.broadcast_to(edge, b3.shape).reshape(tl, LANES)
    q = q_ref[...].astype(F32)
    k = k_ref[...].astype(F32)
    first = (lax.broadcasted_iota(I32, (tl, LANES), 0) % (2 * c)) < c
    q_in = q * jnp.exp2(b)
    qin_s[...] = q_in.astype(BF16)
    qbd_s[...] = jnp.concatenate([jnp.where(first, q_in, 0.0), jnp.where(first, 0.0, q_in)], axis=1).astype(BF16)
    k_in = k * jnp.exp2(-b)
    head0 = lax.broadcasted_iota(I32, (tl, LANES), 1) < GLA_DK
    km0_s[...] = jnp.where(head0, k_in, 0.0).astype(BF16)
    km1_s[...] = jnp.where(head0, 0.0, k_in).astype(BF16)
    k_out = k * jnp.exp2(btot - b)
    kout_s[...] = jnp.concatenate([jnp.where(first, k_out, 0.0), jnp.where(first, 0.0, k_out)], axis=1).astype(BF16)
    dec_s[...] = jnp.exp2(btot)
    vf = v_ref[...].astype(F32)
    vhead0 = lax.broadcasted_iota(I32, (tl, 2 * GLA_DV), 1) < GLA_DV
    vm0_s[...] = jnp.where(vhead0, vf, 0.0).astype(BF16)
    vm1_s[...] = jnp.where(vhead0, 0.0, vf).astype(BF16)


def _gla_increment(v_ref, scr, pair):
    kout_s, u_s = scr[3], scr[7]
    rows = pl.ds(pair * 2 * GLA_CHUNK, 2 * GLA_CHUNK)
    u_t = lax.dot_general(v_ref[rows, :], kout_s[rows, :], (((0,), (0,)), ((), ())),
                          preferred_element_type=F32)
    srow = lax.broadcasted_iota(I32, (2 * GLA_DV, 2 * LANES), 0) // GLA_DV
    scol = (lax.broadcasted_iota(I32, (2 * GLA_DV, 2 * LANES), 1) % LANES) // GLA_DK
    u_s[pair] = jnp.where(srow == scol, u_t, 0.0)


def _gla_states(scr, s_ref, chunk_order):
    dec_s, u_s, sprev_s = scr[6], scr[7], scr[8]
    s_t = s_ref[...]
    for cidx in chunk_order:
        pair, lanes = cidx // 2, pl.ds((cidx % 2) * LANES, LANES)
        sprev_s[pair, :, lanes] = s_t.astype(BF16)
        r0 = cidx * GLA_CHUNK
        s_t = s_t * dec_s[r0:r0 + 1, :] + u_s[pair, :, lanes]
    s_ref[...] = s_t


def _gla_output(o_ref, scr, pair, fwd):
    qin_s, km0_s, km1_s, _, vm0_s, vm1_s, _, _, sprev_s, qbd_s = scr
    c = GLA_CHUNK
    r0 = pair * 2 * c
    rows = pl.ds(r0, 2 * c)
    ca, cb = pl.ds(r0, c), pl.ds(r0 + c, c)
    kst = jnp.concatenate([km0_s[ca, :], km1_s[ca, :], km0_s[cb, :], km1_s[cb, :]], axis=0)
    a = lax.dot_general(qin_s[rows, :], kst, (((1,), (1,)), ((), ())), preferred_element_type=F32)
    ri = lax.broadcasted_iota(I32, (2 * c, 4 * c), 0)
    ci = lax.broadcasted_iota(I32, (2 * c, 4 * c), 1)
    same_chunk = (ri // c) == (ci // (2 * c))
    keep = same_chunk & ((ci % c <= ri % c) if fwd else (ci % c >= ri % c))
    a = jnp.where(keep, a, 0.0).astype(BF16)
    vbd = jnp.concatenate([vm0_s[ca, :], vm1_s[ca, :], vm0_s[cb, :], vm1_s[cb, :]], axis=0)
    o = jnp.dot(a, vbd, preferred_element_type=F32)
    o = o + lax.dot_general(qbd_s[rows, :], sprev_s[pair], (((1,), (1,)), ((), ())), preferred_element_type=F32)
    o_ref[rows, :] = o.astype(o_ref.dtype)


def _gla_kernel(qf_ref, kf_ref, vf_ref, zf_ref, qb_ref, kb_ref, vb_ref, zb_ref,
                upf_ref, upb_ref, bf_ref, bb_ref, trif_ref, trib_ref, s0_ref,
                of_ref, ob_ref, sfin_ref, sf_scr, sb_scr, *scr, nchunk):
    i = pl.program_id(2)
    nt = pl.num_programs(2)
    scr_f, scr_b = scr[:len(scr) // 2], scr[len(scr) // 2:]

    @pl.when(i == 0)
    def _():
        sf_scr[...] = s0_ref[0]
        sb_scr[...] = s0_ref[1]

    _gla_prepare(qf_ref, kf_ref, vf_ref, zf_ref, upf_ref, bf_ref, trif_ref, scr_f, True)
    _gla_prepare(qb_ref, kb_ref, vb_ref, zb_ref, upb_ref, bb_ref, trib_ref, scr_b, False)
    for pair in range(nchunk // 2):
        _gla_increment(vf_ref, scr_f, pair)
        _gla_increment(vb_ref, scr_b, pair)
    _gla_states(scr_f, sf_scr, range(nchunk))
    _gla_states(scr_b, sb_scr, range(nchunk - 1, -1, -1))
    for pair in range(nchunk // 2):
        _gla_output(of_ref, scr_f, pair, True)
        _gla_output(ob_ref, scr_b, pair, False)

    @pl.when(i == nt - 1)
    def _():
        sfin_ref[0] = sf_scr[...]
        sfin_ref[1] = sb_scr[...]


def _gla_call(qk, v, z, upf_pad, upb_pad, bias_f, bias_b, s0, batch, seq, tl):
    n = qk.shape[0]
    nt = seq // tl
    npair = GLA_HEADS // 2

    def fwd_row(b, p, i):
        return b * nt + i

    def bwd_row(b, p, i):
        return b * nt + (nt - 1 - i)

    def specs(row):
        return [
            pl.BlockSpec((tl, LANES), lambda b, p, i: (row(b, p, i), p)),
            pl.BlockSpec((tl, LANES), lambda b, p, i: (row(b, p, i), npair + p)),
            pl.BlockSpec((tl, 2 * GLA_DV), lambda b, p, i: (row(b, p, i), p)),
            pl.BlockSpec((tl, LANES), lambda b, p, i: (row(b, p, i), 0)),
        ]

    in_specs = specs(fwd_row) + specs(bwd_row) + [
        pl.BlockSpec((None, LANES, LANES), lambda b, p, i: (p, 0, 0)),
        pl.BlockSpec((None, LANES, LANES), lambda b, p, i: (p, 0, 0)),
        pl.BlockSpec((None, 1, LANES), lambda b, p, i: (p, 0, 0)),
        pl.BlockSpec((None, 1, LANES), lambda b, p, i: (p, 0, 0)),
        pl.BlockSpec((None, GLA_SUM_ROWS, GLA_SUM_ROWS), lambda b, p, i: (0, 0, 0)),
        pl.BlockSpec((None, GLA_SUM_ROWS, GLA_SUM_ROWS), lambda b, p, i: (1, 0, 0)),
        pl.BlockSpec((None, None, 2, 2 * GLA_DV, LANES), lambda b, p, i: (b, p, 0, 0, 0)),
    ]
    assert tl % GLA_SUM_ROWS == 0
    ri = jnp.arange(GLA_SUM_ROWS)[:, None]
    ci = jnp.arange(GLA_SUM_ROWS)[None, :]
    same = (ri // GLA_CHUNK) == (ci // GLA_CHUNK)
    tri = jnp.stack([same & (ci <= ri), same & (ci >= ri)]).astype(BF16)
    npairs = tl // (2 * GLA_CHUNK)
    dir_scratch = [
        pltpu.VMEM((tl, LANES), BF16),
        pltpu.VMEM((tl, LANES), BF16),
        pltpu.VMEM((tl, LANES), BF16),
        pltpu.VMEM((tl, 2 * LANES), BF16),
        pltpu.VMEM((tl, 2 * GLA_DV), BF16),
        pltpu.VMEM((tl, 2 * GLA_DV), BF16),
        pltpu.VMEM((tl, LANES), F32),
        pltpu.VMEM((npairs, 2 * GLA_DV, 2 * LANES), F32),
        pltpu.VMEM((npairs, 2 * GLA_DV, 2 * LANES), BF16),
        pltpu.VMEM((tl, 2 * LANES), BF16),
    ]
    out_specs = [
        pl.BlockSpec((tl, 2 * GLA_DV), lambda b, p, i: (fwd_row(b, p, i), p)),
        pl.BlockSpec((tl, 2 * GLA_DV), lambda b, p, i: (bwd_row(b, p, i), p)),
        pl.BlockSpec((None, None, 2, 2 * GLA_DV, LANES), lambda b, p, i: (b, p, 0, 0, 0)),
    ]
    out_shape = [
        jax.ShapeDtypeStruct((n, GLA_WIDTH), BF16),
        jax.ShapeDtypeStruct((n, GLA_WIDTH), BF16),
        jax.ShapeDtypeStruct((batch, npair, 2, 2 * GLA_DV, LANES), F32),
    ]
    return pl.pallas_call(
        functools.partial(_gla_kernel, nchunk=tl // GLA_CHUNK),
        grid=(batch, npair, nt),
        in_specs=in_specs,
        out_specs=out_specs,
        out_shape=out_shape,
        scratch_shapes=[pltpu.VMEM((2 * GLA_DV, LANES), F32), pltpu.VMEM((2 * GLA_DV, LANES), F32)]
        + dir_scratch + dir_scratch,
        compiler_params=_cparams(("arbitrary", "arbitrary", "arbitrary")),
        name="gla",
    )(qk, qk, v, z, qk, qk, v, z, upf_pad, upb_pad, bias_f, bias_b, tri, tri, s0)


def _swa_block(sink_ref, q_ref, o_ref, row0, kall, vall, prev_ok, next_ok):
    w = ATT_BLOCK
    local = prev_ok is not None
    qrows = pl.ds(row0, w)
    r2 = lax.broadcasted_iota(I32, (2 * w, w), 0) % w
    c2 = lax.broadcasted_iota(I32, (2 * w, w), 1)
    if local:
        bias_prev = jnp.where((c2 >= r2) & prev_ok, 0.0, NEG_BIG)
        bias_next = jnp.where((c2 <= r2) & next_ok, 0.0, NEG_BIG)
    first = c2 < SWA_HD
    top = lax.broadcasted_iota(I32, (2 * w, 1), 0) < w
    scores = []
    for g in range(SWA_KV_HEADS):
        kd = kall[:, g * LANES:(g + 1) * LANES]
        qs = jnp.concatenate([q_ref[qrows, (2 * g) * LANES:(2 * g + 1) * LANES],
                              q_ref[qrows, (2 * g + 1) * LANES:(2 * g + 2) * LANES]], axis=0).astype(F32)
        for half in range(2):
            qm = (jnp.where(first, qs, 0.0) if half == 0 else jnp.where(first, 0.0, qs)).astype(BF16)
            s = lax.dot_general(qm, kd, (((1,), (1,)), ((), ())), preferred_element_type=F32)
            if local:
                s = jnp.concatenate([s[:, :w] + bias_prev, s[:, w:2 * w], s[:, 2 * w:3 * w] + bias_next,
                                     s[:, 3 * w:]], axis=1)
            scores.append(s)
    for g in range(SWA_KV_HEADS):
        outs = []
        for half in range(2):
            s = scores[2 * g + half]
            sk = jnp.where(top, sink_ref[4 * g + half], sink_ref[4 * g + 2 + half]) * LOG2E
            m = jnp.maximum(jnp.max(s, axis=-1, keepdims=True), sk)
            p = jnp.exp2((s - m).astype(BF16))
            va = vall[:, (2 * g + half) * LANES:(2 * g + half + 1) * LANES]
            acc = jnp.dot(p, va, preferred_element_type=F32)
            den = pltpu.roll(acc, SWA_HD, 1) + jnp.exp2(sk - m)
            outs.append(acc / den)
        out = jnp.where(first, outs[0], outs[1])
        o_ref[qrows, (2 * g) * LANES:(2 * g + 1) * LANES] = out[:w].astype(o_ref.dtype)
        o_ref[qrows, (2 * g + 1) * LANES:(2 * g + 2) * LANES] = out[w:].astype(o_ref.dtype)


def _swa_kernel(*refs, nstep, local):
    w = ATT_BLOCK
    if not local:
        sink_ref, q_ref, kx_ref, vx_ref, o_ref = refs
        _swa_block(sink_ref, q_ref, o_ref, 0, kx_ref[...], vx_ref[...], None, None)
        return
    sink_ref, q_ref, kp_ref, kc_ref, kn_ref, vp_ref, vc_ref, vn_ref, kx_ref, vx_ref, o_ref = refs
    i = pl.program_id(1)
    nblk = q_ref.shape[0] // w
    k_blocks = [kp_ref[...]] + [kc_ref[j * w:(j + 1) * w, :] for j in range(nblk)] + [kn_ref[...]]
    v_blocks = [vp_ref[...]] + [vc_ref[j * w:(j + 1) * w, :] for j in range(nblk)] + [vn_ref[...]]
    exists = [i > 0] + [True] * nblk + [i < nstep - 1]
    for sub in range(nblk):
        kall = jnp.concatenate(k_blocks[sub:sub + 3] + [kx_ref[...]], axis=0)
        vall = jnp.concatenate(v_blocks[sub:sub + 3] + [vx_ref[...]], axis=0)
        _swa_block(sink_ref, q_ref, o_ref, sub * w, kall, vall, exists[sub], exists[sub + 2])


def _swa_call(sq, skd, svd, kcd, vcd, sink, batch, seq, lc):
    n = sq.shape[0]
    w = ATT_BLOCK
    nb = seq // w
    nblk = SWA_BLOCKS_PER_STEP if nb % SWA_BLOCKS_PER_STEP == 0 else 1
    nstep = nb // nblk
    kvw = 2 * SWA_KVW

    def pair(b, i):
        return (b * nstep + i, 0)

    def before(b, i):
        return (b * nb + jnp.maximum(nblk * i - 1, 0), 0)

    def after(b, i):
        return (b * nb + jnp.minimum(nblk * (i + 1), nb - 1), 0)

    def kv_specs(width):
        return [pl.BlockSpec((w, width), before), pl.BlockSpec((nblk * w, width), pair),
                pl.BlockSpec((w, width), after)]

    def ctx_spec(width):
        return pl.BlockSpec((lc, width), lambda b, i: (b, 0))

    return pl.pallas_call(
        functools.partial(_swa_kernel, nstep=nstep, local=True),
        grid=(batch, nstep),
        in_specs=[pl.BlockSpec(memory_space=pltpu.SMEM), pl.BlockSpec((nblk * w, SWA_WIDTH), pair)]
        + kv_specs(kvw) + kv_specs(SV_LANES) + [ctx_spec(kvw), ctx_spec(SV_LANES)],
        out_specs=pl.BlockSpec((nblk * w, SWA_WIDTH), pair),
        out_shape=jax.ShapeDtypeStruct((n, SWA_WIDTH), BF16),
        compiler_params=_cparams(("arbitrary", "arbitrary")),
        name="swa",
    )(sink, sq, skd, skd, skd, svd, svd, svd, kcd, vcd)


def _swa_ctx_call(sq, kcd, vcd, sink, batch, lc):
    n = sq.shape[0]
    w = ATT_BLOCK
    nb = lc // w
    kvw = 2 * SWA_KVW
    return pl.pallas_call(
        functools.partial(_swa_kernel, nstep=0, local=False),
        grid=(batch, nb),
        in_specs=[pl.BlockSpec(memory_space=pltpu.SMEM),
                  pl.BlockSpec((w, SWA_WIDTH), lambda b, i: (b * nb + i, 0)),
                  pl.BlockSpec((lc, kvw), lambda b, i: (b, 0)),
                  pl.BlockSpec((lc, SV_LANES), lambda b, i: (b, 0))],
        out_specs=pl.BlockSpec((w, SWA_WIDTH), lambda b, i: (b * nb + i, 0)),
        out_shape=jax.ShapeDtypeStruct((n, SWA_WIDTH), BF16),
        compiler_params=_cparams(("arbitrary", "arbitrary")),
        name="swa_ctx",
    )(sink, sq, kcd, vcd)


def _outproj_kernel(of_ref, ob_ref, gate_ref, gn_ref, swa_ref, w_ref, x_ref, g1_ref, n2_ref, sc_ref, sh_ref,
                    wr_ref, xo_ref, h2_ref, lg_ref):
    o = of_ref[...].astype(F32) + ob_ref[...].astype(F32)
    parts = []
    for h in range(GLA_HEADS):
        oh = o[:, h * GLA_DV:(h + 1) * GLA_DV]
        ms = jnp.mean(oh * oh, axis=-1, keepdims=True)
        parts.append(oh * lax.rsqrt(ms + NORM_EPS))
    on = jnp.concatenate(parts, axis=-1) * gn_ref[...]
    gate = gate_ref[...].astype(F32)
    gla = on * (gate * jax.nn.sigmoid(gate))
    mix = jnp.concatenate([gla.astype(BF16), swa_ref[...]], axis=-1)
    y = jnp.dot(mix, w_ref[...], preferred_element_type=F32)
    xo = x_ref[...] + g1_ref[...] * y
    xo_ref[...] = xo
    ms = jnp.mean(xo * xo, axis=-1, keepdims=True)
    h2 = (xo * lax.rsqrt(ms + NORM_EPS)) * (n2_ref[...] * (1.0 + sc_ref[...])) + sh_ref[...]
    hi = h2.astype(BF16)
    hi_f = hi.astype(F32)
    h2_ref[...] = _pack_rounded_pairs(hi_f)
    lo = (h2 - hi_f).astype(BF16)
    both = jnp.dot(hi, wr_ref[...], preferred_element_type=F32)
    lg = both[:, :LANES] + both[:, LANES:] + jnp.dot(lo, wr_ref[:, :LANES], preferred_element_type=F32)
    lg_ref[...] = jnp.transpose(lg)[:N_EXPERTS, :]


def _outproj_call(o_f, o_b, gate, gn, swa, w_out_b, x2d, mods5, layer, row_of_tile, n2, wr_cat, tm):
    n, d = x2d.shape
    nt = n // tm

    def mod_spec(k):
        return pl.BlockSpec((None, None, None, 1, d), lambda t: (layer, row_of_tile(t), k, 0, 0))

    return pl.pallas_call(
        _outproj_kernel,
        grid=(nt,),
        in_specs=[
            pl.BlockSpec((tm, GLA_WIDTH), lambda t: (t, 0)),
            pl.BlockSpec((tm, GLA_WIDTH), lambda t: (t, 0)),
            pl.BlockSpec((tm, GLA_WIDTH), lambda t: (t, 0)),
            pl.BlockSpec((1, GLA_WIDTH), lambda t: (0, 0)),
            pl.BlockSpec((tm, SWA_WIDTH), lambda t: (t, 0)),
            pl.BlockSpec((None, d, d), lambda t: (layer, 0, 0)),
            pl.BlockSpec((tm, d), lambda t: (t, 0)),
            mod_spec(2),
            pl.BlockSpec((1, d), lambda t: (0, 0)),
            mod_spec(4), mod_spec(3),
            pl.BlockSpec((d, 2 * LANES), lambda t: (0, 0)),
        ],
        out_specs=[
            pl.BlockSpec((tm, d), lambda t: (t, 0)),
            pl.BlockSpec((tm, d // 2), lambda t: (t, 0)),
            pl.BlockSpec((N_EXPERTS, tm), lambda t: (0, t)),
        ],
        out_shape=[
            jax.ShapeDtypeStruct((n, d), F32),
            jax.ShapeDtypeStruct((n, d // 2), U32),
            jax.ShapeDtypeStruct((N_EXPERTS, n), F32),
        ],
        compiler_params=_cparams(("arbitrary",)),
        name="outproj",
    )(o_f, o_b, gate, gn.reshape(1, GLA_WIDTH), swa, w_out_b, x2d, mods5, n2.reshape(1, d), mods5, mods5, wr_cat)


def _first_index(vals, target):
    idx = jnp.full(target.shape, len(vals) - 1, I32)
    for i in range(len(vals) - 2, -1, -1):
        idx = jnp.where(vals[i] == target, i, idx)
    return idx


def _route_kernel(lg_ref, br_ref, io_ref, wo_ref, seg_ref, pad_ref, blk_ref, cnt_scr, seg_scr, pad_scr, *, tn):
    phase = pl.program_id(0)
    step = pl.program_id(1)
    group = lg_ref.shape[1] // tn

    @pl.when((phase == 0) & (step == 0))
    def _():
        cnt_scr[...] = jnp.zeros_like(cnt_scr)
        seg_scr[...] = jnp.zeros_like(seg_scr)
        pad_scr[...] = jnp.zeros_like(pad_scr)

    @pl.when((phase == 1) & (step == 0))
    def _():
        seg_len = jnp.floor((cnt_scr[...] + (SUBLANES - 1)) * (1.0 / SUBLANES)) * SUBLANES
        rows_e = jnp.sum(seg_len, axis=1, keepdims=True)
        blocks = jnp.floor((rows_e + (MOE_BLOCK - 1)) * (1.0 / MOE_BLOCK)) * MOE_BLOCK
        r128 = lax.broadcasted_iota(I32, (LANES, LANES), 0)
        c128 = lax.broadcasted_iota(I32, (LANES, LANES), 1)
        before_tile = jnp.where(r128 < c128, 1.0, 0.0).astype(BF16)
        seg = jnp.dot((seg_len * (1.0 / SUBLANES)).astype(BF16), before_tile,
                      preferred_element_type=F32) * SUBLANES
        run = jnp.zeros((1, 1), F32)
        for e in range(N_EXPERTS):
            seg_scr[e:e + 1, :] = seg[e:e + 1, :] + run
            run = run + blocks[e:e + 1, :]
        pad_scr[...] = seg_len
        seg_ref[...] = seg_scr[...].astype(I32)
        pad_ref[...] = seg_len.astype(I32)
        blk_ref[...] = jnp.broadcast_to(blocks, blk_ref.shape).astype(I32)

    for g in range(group):
        lanes = slice(g * tn, (g + 1) * tn)
        _route_tile(lg_ref[:, lanes], br_ref, io_ref, wo_ref, lanes, step * group + g, phase, cnt_scr, pad_scr)


def _route_tile(lg, br_ref, io_ref, wo_ref, lanes, t, phase, cnt_scr, pad_scr):
    tn = lg.shape[1]
    tile_lane = lax.broadcasted_iota(I32, (N_EXPERTS, LANES), 1)
    s = jax.nn.sigmoid(lg)
    sb = s + br_ref[...]
    rows_s = [s[e:e + 1, :] for e in range(N_EXPERTS)]
    rows_b = [sb[e:e + 1, :] for e in range(N_EXPERTS)]
    gscore, gi1, gi2 = [], [], []
    epg = EXPERTS_PER_GROUP
    for g in range(N_GROUPS):
        a = rows_b[g * epg:(g + 1) * epg]
        m1 = functools.reduce(jnp.maximum, a)
        i1 = _first_index(a, m1)
        rest = [jnp.where(i1 == i, -jnp.inf, a[i]) for i in range(epg)]
        m2 = functools.reduce(jnp.maximum, rest)
        i2 = _first_index(rest, m2)
        gscore.append(m1 + m2)
        gi1.append(i1)
        gi2.append(i2)
    gm = functools.reduce(jnp.maximum, gscore)
    gsel = _first_index(gscore, gm)
    i1 = gi1[N_GROUPS - 1]
    i2 = gi2[N_GROUPS - 1]
    for g in range(N_GROUPS - 2, -1, -1):
        i1 = jnp.where(gsel == g, gi1[g], i1)
        i2 = jnp.where(gsel == g, gi2[g], i2)
    idx0 = gsel * epg + i1
    idx1 = gsel * epg + i2
    s0 = jnp.zeros_like(rows_s[0])
    s1 = jnp.zeros_like(rows_s[0])
    for e in range(N_EXPERTS):
        s0 = jnp.where(idx0 == e, rows_s[e], s0)
        s1 = jnp.where(idx1 == e, rows_s[e], s1)
    tot = s0 + s1
    w0 = s0 / tot
    w1 = s1 / tot

    eidx = lax.broadcasted_iota(I32, (N_EXPERTS, tn), 0)
    oh0 = eidx == idx0
    oh1 = eidx == idx1
    oh = jnp.where(oh0 | oh1, 1.0, 0.0)

    @pl.when(phase == 0)
    def _():
        cnt_scr[...] = cnt_scr[...] + jnp.where(tile_lane == t, jnp.sum(oh, axis=1, keepdims=True), 0.0)

    @pl.when(phase == 1)
    def _():
        rr = lax.broadcasted_iota(I32, (tn, tn), 0)
        cc = lax.broadcasted_iota(I32, (tn, tn), 1)
        upper = jnp.where(rr < cc, 1.0, 0.0).astype(BF16)
        before = jnp.dot(oh.astype(BF16), upper, preferred_element_type=F32)
        pad_col = jnp.sum(jnp.where(tile_lane == t, pad_scr[...], 0.0), axis=1, keepdims=True)
        run = jnp.zeros((1, 1), F32)
        offs = []
        for e in range(N_EXPERTS):
            offs.append(run)
            run = run + pad_col[e:e + 1, :]
        pos = before + jnp.concatenate(offs, axis=0)
        spos0 = jnp.sum(jnp.where(oh0, pos, 0.0), axis=0, keepdims=True)
        spos1 = jnp.sum(jnp.where(oh1, pos, 0.0), axis=0, keepdims=True)
        zi = jnp.zeros((SUBLANES - 4, tn), I32)
        io_ref[:, lanes] = jnp.concatenate([idx0, idx1, spos0.astype(I32), spos1.astype(I32), zi], axis=0)
        wo_ref[:, lanes] = jnp.concatenate([w0, w1, jnp.zeros((SUBLANES - 2, tn), F32)], axis=0)


def _route_call(logits_t, b_router, tn):
    ne, t_all = logits_t.shape
    ntile = t_all // tn
    assert ntile <= LANES
    group = max(g for g in range(1, ROUTE_TILES_PER_STEP + 1) if ntile % g == 0)
    tw = group * tn
    tok_out = pl.BlockSpec((SUBLANES, tw), lambda p, t: (0, t * p))
    tab_out = pl.BlockSpec((ne, LANES), lambda p, t: (0, 0))
    tab = jax.ShapeDtypeStruct((ne, LANES), I32)
    return pl.pallas_call(
        functools.partial(_route_kernel, tn=tn),
        grid=(2, ntile // group),
        in_specs=[pl.BlockSpec((ne, tw), lambda p, t: (0, t)), pl.BlockSpec((ne, 1), lambda p, t: (0, 0))],
        out_specs=[tok_out, tok_out, tab_out, tab_out, tab_out],
        out_shape=[
            jax.ShapeDtypeStruct((SUBLANES, t_all), I32),
            jax.ShapeDtypeStruct((SUBLANES, t_all), F32),
            tab, tab, tab,
        ],
        scratch_shapes=[pltpu.VMEM((ne, LANES), F32)] * 3,
        compiler_params=_cparams(("arbitrary", "arbitrary")),
        name="route",
    )(logits_t, b_router.reshape(ne, 1))


def _segment_copies(seg_ref, pad_ref, t, stage, hbm, sem, to_hbm, wait):
    if wait:
        total = functools.reduce(lambda a, b: a + b, [pad_ref[e * LANES + t] for e in range(N_EXPERTS)])
        for size in (2 * SEG_SIZES[0],) + SEG_SIZES:
            @pl.when((total & size) != 0)
            def _(size=size):
                src, dst = stage.at[pl.ds(0, size), :], hbm.at[pl.ds(0, size), :]
                if not to_hbm:
                    src, dst = dst, src
                pltpu.make_async_copy(src, dst, sem).wait()
        return
    loc = 0
    for e in range(N_EXPERTS):
        n = pad_ref[e * LANES + t]
        start = seg_ref[e * LANES + t]
        for size in SEG_SIZES:
            off = n & (-2 * size)

            @pl.when((n & size) != 0)
            def _(off=off, size=size, loc=loc, start=start):
                s_rows = pl.ds(pl.multiple_of(loc + off, SUBLANES), size)
                h_rows = pl.ds(pl.multiple_of(start + off, SUBLANES), size)
                src, dst = (stage.at[s_rows, :], hbm.at[h_rows, :])
                if not to_hbm:
                    src, dst = dst, src
                pltpu.make_async_copy(src, dst, sem).start()
        loc = loc + n


def _sort_matrix(io_ref, rows):
    r = lax.broadcasted_iota(I32, (rows, io_ref.shape[1]), 0)
    return r == io_ref[2:3, :], r == io_ref[3:4, :]


def _dispatch_kernel(seg_ref, pad_ref, zblk_ref, *refs, tiles):
    srcs = refs[:len(tiles)]
    io_ref, wo_ref, xb_hbm, stage, zbuf, sem = refs[len(tiles):]
    i = pl.program_id(0)
    m = MOE_BLOCK

    def zero_copy(j):
        start = pl.multiple_of(zblk_ref[j] * m, m)
        return pltpu.make_async_copy(zbuf, xb_hbm.at[pl.ds(start, m), :], sem.at[0])

    @pl.when(i == 0)
    def _():
        zbuf[...] = jnp.zeros_like(zbuf)
        for j in range(zblk_ref.shape[0]):
            @pl.when(zblk_ref[j] >= 0)
            def _():
                zero_copy(j).start()
        for j in range(zblk_ref.shape[0]):
            @pl.when(zblk_ref[j] >= 0)
            def _():
                zero_copy(j).wait()

    hp = srcs[0][...]
    if len(srcs) == 2:
        hp = jnp.where(i < tiles[0], hp, srcs[1][...])
    x = _unpack_bf16_pairs(hp)
    m0, m1 = _sort_matrix(io_ref, STAGE_ROWS)
    sort = jnp.where(m0 | m1, 1.0, 0.0).astype(BF16)
    xs = jnp.dot(sort, x, preferred_element_type=F32)
    dh = x.shape[1] // 2
    slot = i % 2
    cur = stage.at[slot]
    cur[:, :dh] = _pack_rounded_pairs(xs)
    ws = jnp.sum(jnp.where(m0, wo_ref[0:1, :], 0.0) + jnp.where(m1, wo_ref[1:2, :], 0.0), axis=1, keepdims=True)
    cur[:, dh:] = jnp.broadcast_to(lax.bitcast_convert_type(ws, U32), (STAGE_ROWS, LANES))
    _segment_copies(seg_ref, pad_ref, i, cur, xb_hbm, sem.at[slot], to_hbm=True, wait=False)

    @pl.when(i > 0)
    def _():
        _segment_copies(seg_ref, pad_ref, i - 1, stage.at[1 - slot], xb_hbm, sem.at[1 - slot], to_hbm=True, wait=True)

    @pl.when(i == pl.num_programs(0) - 1)
    def _():
        _segment_copies(seg_ref, pad_ref, i, cur, xb_hbm, sem.at[slot], to_hbm=True, wait=True)


def _dispatch_call(seg, pad, zblk, io, wo, sources, p_rows):
    tile = ROUTE_TILE
    dh = sources[0].shape[1]
    tiles = tuple(s.shape[0] // tile for s in sources)
    firsts = tuple(sum(tiles[:k]) for k in range(len(tiles)))

    def src_spec(first, ntile):
        return pl.BlockSpec((tile, dh), lambda i, *_: (jnp.clip(i - first, 0, ntile - 1), 0))

    tok_spec = pl.BlockSpec((SUBLANES, tile), lambda i, *_: (0, i))
    grid_spec = pltpu.PrefetchScalarGridSpec(
        num_scalar_prefetch=3,
        grid=(sum(tiles),),
        in_specs=[src_spec(f, n) for f, n in zip(firsts, tiles)] + [tok_spec, tok_spec],
        out_specs=pl.BlockSpec(memory_space=pl.ANY),
        scratch_shapes=[pltpu.VMEM((2, STAGE_ROWS, dh + LANES), U32), pltpu.VMEM((MOE_BLOCK, dh + LANES), U32),
                        pltpu.SemaphoreType.DMA((2,))],
    )
    return pl.pallas_call(
        functools.partial(_dispatch_kernel, tiles=tiles),
        grid_spec=grid_spec,
        out_shape=jax.ShapeDtypeStruct((p_rows, dh + LANES), U32),
        compiler_params=_cparams(("arbitrary",)),
        name="dispatch",
    )(seg, pad, zblk, *sources, io, wo)


def _expert_kernel(be_ref, nu_ref, va_ref, ne_ref, x_ref, wg_hbm, wu_hbm, wd_hbm, o_ref,
                   wg_b, wu_b, wd_b, wg_f, wu_f, wd_f, switch_ref, sem, *, layer):
    j = pl.program_id(0)
    dh = o_ref.shape[1]
    half = o_ref.shape[0] // 2
    valid = va_ref[j]
    e = be_ref[j]

    def weight_copies(expert, slot):
        return [pltpu.make_async_copy(hbm.at[layer, expert], buf.at[slot], sem.at[slot, k])
                for k, (hbm, buf) in enumerate(((wg_hbm, wg_f), (wu_hbm, wu_f), (wd_hbm, wd_f)))]

    @pl.when(j == 0)
    def _():
        switch_ref[0] = 0
        for cp in weight_copies(e, 0):
            cp.start()

    @pl.when((j == 0) | (e != be_ref[jnp.maximum(j - 1, 0)]))
    def _():
        slot = switch_ref[0] % 2
        for cp in weight_copies(e, slot):
            cp.wait()
        wg_b[...] = wg_f[slot].astype(BF16)
        wu_b[...] = wu_f[slot].astype(BF16)
        wd_b[...] = wd_f[slot].astype(BF16)
        nxt = ne_ref[e]

        @pl.when(nxt >= 0)
        def _():
            for cp in weight_copies(nxt, 1 - slot):
                cp.start()
        switch_ref[0] = switch_ref[0] + 1

    def ffn_halves(halves):
        gates = []
        for r in halves:
            x = _unpack_bf16_pairs(x_ref[r * half:(r + 1) * half, :dh])
            gates.append((jnp.dot(x, wg_b[...], preferred_element_type=F32),
                          jnp.dot(x, wu_b[...], preferred_element_type=F32)))
        for r, (a, u) in zip(halves, gates):
            hmid = (a * jax.nn.sigmoid(a)) * u
            y = jnp.dot(hmid.astype(BF16), wd_b[...], preferred_element_type=F32)
            row_w = lax.bitcast_convert_type(x_ref[r * half:(r + 1) * half, dh:dh + 1], F32)
            o_ref[r * half:(r + 1) * half, :] = _pack_bf16_pairs(y * row_w)

    @pl.when(valid > half)
    def _():
        ffn_halves((0, 1))

    @pl.when((valid > 0) & (valid <= half))
    def _():
        ffn_halves((0,))
        o_ref[half:, :] = jnp.zeros((half, dh), o_ref.dtype)

    @pl.when(valid == 0)
    def _():
        o_ref[...] = jnp.zeros_like(o_ref)


def _expert_call(block_e, nused, valid, next_e, xb, wg, wu, wd, layer):
    p_rows = xb.shape[0]
    dh = xb.shape[1] - LANES
    d = 2 * dh
    m = MOE_BLOCK
    nb = p_rows // m
    de = wg.shape[-1]

    def xmap(j, be, nu, va, ne):
        return (jnp.minimum(j, nu[0] - 1), 0)

    any_spec = pl.BlockSpec(memory_space=pl.ANY)
    grid_spec = pltpu.PrefetchScalarGridSpec(
        num_scalar_prefetch=4,
        grid=(nb,),
        in_specs=[pl.BlockSpec((m, dh + LANES), xmap), any_spec, any_spec, any_spec],
        out_specs=pl.BlockSpec((m, dh), lambda j, be, nu, va, ne: (j, 0)),
        scratch_shapes=[
            pltpu.VMEM((d, de), BF16), pltpu.VMEM((d, de), BF16), pltpu.VMEM((de, d), BF16),
            pltpu.VMEM((2, d, de), F32), pltpu.VMEM((2, d, de), F32), pltpu.VMEM((2, de, d), F32),
            pltpu.SMEM((1,), I32), pltpu.SemaphoreType.DMA((2, 3)),
        ],
    )
    return pl.pallas_call(
        functools.partial(_expert_kernel, layer=layer),
        grid_spec=grid_spec,
        out_shape=jax.ShapeDtypeStruct((p_rows, dh), U32),
        compiler_params=_cparams(("arbitrary",)),
        name="experts",
    )(block_e, nused, valid, next_e, xb, wg, wu, wd)


def _combine_kernel(*refs, tile0, final):
    if final:
        seg_ref, pad_ref, yb_hbm, io_ref, x_ref, g2_ref, fn_ref, o_ref, stage, sem = refs
    else:
        seg_ref, pad_ref, yb_hbm, io_ref, x_ref, g2_ref, o_ref, stage, sem = refs
    i = pl.program_id(0)

    slot = i % 2

    @pl.when(i == 0)
    def _():
        stage[...] = jnp.zeros_like(stage)
        _segment_copies(seg_ref, pad_ref, tile0, stage.at[0], yb_hbm, sem.at[0], to_hbm=False, wait=False)

    @pl.when(i + 1 < pl.num_programs(0))
    def _():
        _segment_copies(seg_ref, pad_ref, tile0 + i + 1, stage.at[1 - slot], yb_hbm, sem.at[1 - slot],
                        to_hbm=False, wait=False)

    _segment_copies(seg_ref, pad_ref, tile0 + i, stage.at[slot], yb_hbm, sem.at[slot], to_hbm=False, wait=True)
    rows = _unpack_bf16_pairs(stage[slot])
    m0, m1 = _sort_matrix(io_ref, STAGE_ROWS)
    pick = jnp.where(m0 | m1, 1.0, 0.0).astype(BF16)
    y = lax.dot_general(pick, rows, (((0,), (0,)), ((), ())), preferred_element_type=F32)
    xo = x_ref[...] + g2_ref[...] * y
    if final:
        ms = jnp.mean(xo * xo, axis=-1, keepdims=True)
        xo = (xo * lax.rsqrt(ms + NORM_EPS)) * fn_ref[...]
    o_ref[...] = xo


def _combine_call(seg, pad, yb, io, x2d, mods5, layer, row_of_tile, final_g, tile0):
    n, d = x2d.shape
    tm = ROUTE_TILE
    final = final_g is not None
    in_specs = [
        pl.BlockSpec(memory_space=pl.ANY),
        pl.BlockSpec((SUBLANES, tm), lambda t, *_: (0, tile0 + t)),
        pl.BlockSpec((tm, d), lambda t, *_: (t, 0)),
        pl.BlockSpec((None, None, None, 1, d), lambda t, *_: (layer, row_of_tile(t), 5, 0, 0)),
    ]
    args = [yb, io, x2d, mods5]
    if final:
        in_specs.append(pl.BlockSpec((1, d), lambda t, *_: (0, 0)))
        args.append(final_g.reshape(1, d))
    grid_spec = pltpu.PrefetchScalarGridSpec(
        num_scalar_prefetch=2,
        grid=(n // tm,),
        in_specs=in_specs,
        out_specs=pl.BlockSpec((tm, d), lambda t, *_: (t, 0)),
        scratch_shapes=[pltpu.VMEM((2, STAGE_ROWS, d // 2), U32), pltpu.SemaphoreType.DMA((2,))],
    )
    return pl.pallas_call(
        functools.partial(_combine_kernel, tile0=tile0, final=final),
        grid_spec=grid_spec,
        out_shape=jax.ShapeDtypeStruct((n, d), F32),
        compiler_params=_cparams(("arbitrary",)),
        name="combine",
    )(seg, pad, *args)


def _pack_w_kernel(w_ref, o_ref):
    cut = C_Z + 2 * GLA_GATE_RANK
    o_ref[:, :C_Z] = w_ref[:, :C_Z].astype(BF16)
    gate_tile = w_ref[:, C_Z:C_Z + LANES]
    lane = lax.broadcasted_iota(I32, gate_tile.shape, 1)
    o_ref[:, C_Z:C_SQ] = jnp.where(lane < 2 * GLA_GATE_RANK, gate_tile, 0.0).astype(BF16)
    o_ref[:, C_SQ:] = w_ref[:, cut:].astype(BF16)


def _pack_w_in(w):
    assert sum(IN_SIZES[:6]) == C_Z + 2 * GLA_GATE_RANK and sum(IN_SIZES) - sum(IN_SIZES[:6]) == C_END - C_SQ
    depth, d, cols = w.shape
    rows = ROW_TILE // 2
    return pl.pallas_call(
        _pack_w_kernel,
        grid=(depth, d // rows),
        in_specs=[pl.BlockSpec((None, rows, cols), lambda l, r: (l, r, 0))],
        out_specs=pl.BlockSpec((None, rows, C_END), lambda l, r: (l, r, 0)),
        out_shape=jax.ShapeDtypeStruct((depth, d, C_END), BF16),
        compiler_params=_cparams(("arbitrary", "arbitrary")),
        name="pack_w_in",
    )(w)


def _pad_up(up, row0):
    up = up.reshape(GLA_GATE_RANK, GLA_HEADS // 2, LANES).transpose(1, 0, 2)
    hi = up.astype(BF16)
    lo = (up - hi.astype(F32)).astype(BF16)
    out = jnp.zeros((GLA_HEADS // 2, LANES, LANES), BF16)
    for group, part in enumerate((hi, hi, lo)):
        r = group * 2 * GLA_GATE_RANK + row0
        out = out.at[:, r:r + GLA_GATE_RANK, :].set(part)
    return out


def _rope_tables(seq):
    rows = seq // GRID_W
    dim = jnp.arange(LANES, dtype=I32) % SWA_HD
    inv = ROPE_THETA ** (-((dim % ROPE_HALF).astype(F32) * 2.0 / ROPE_AXIS_DIM))
    sign = jnp.where(dim % ROPE_AXIS_DIM < ROPE_HALF, -1.0, 1.0)
    by_row = (dim < ROPE_AXIS_DIM)[None, None, :]
    ang_r = (jnp.arange(rows, dtype=F32)[:, None] * inv[None, :])[:, None, :]
    ang_c = (jnp.arange(GRID_W, dtype=F32)[:, None] * inv[None, :])[None, :, :]
    cos = jnp.where(by_row, jnp.cos(ang_r), jnp.cos(ang_c))
    sin = jnp.where(by_row, jnp.sin(ang_r), jnp.sin(ang_c)) * sign
    return cos.reshape(seq, LANES), sin.reshape(seq, LANES)


def _tile(n, pref):
    t = pref
    while n % t:
        t //= 2
    return t


def kernel(x, c, ctx, c_ctx, w_ada, b_ada, norm1, norm2, w_in, gla_up_f, gla_bias_f, gla_up_b, gla_bias_b,
           gla_norm, swa_sink, w_out, w_router, b_router, w_gate, w_up, w_down, final_norm):
    batch, seq, d = x.shape
    lc = ctx.shape[1]
    depth = w_ada.shape[0]
    n_l = batch * seq
    n_c = batch * lc
    npair = GLA_HEADS // 2
    assert batch + 1 <= SUBLANES
    assert n_l % ROUTE_TILE == 0 and n_c % ROUTE_TILE == 0

    tm_l = _tile(seq, ROW_TILE)
    tm_in = _tile(seq, INPROJ_TILE)
    tm_c = _tile(lc, CTX_TILE)
    tl_l = _tile(seq, GLA_TILE)
    tl_c = _tile(lc, ROW_TILE)

    rows = jnp.zeros((SUBLANES, d), F32).at[:batch].set(c).at[batch].set(c_ctx)
    mods = _ada_call(rows, w_ada, b_ada)
    mods5 = mods.reshape(depth, SUBLANES, 6, 1, d)

    tabs = _rope_tables(seq)
    w_packed = _pack_w_in(w_in)
    w_out_b = w_out.astype(BF16)
    wr_pad = jnp.zeros((d, LANES), F32).at[:, :N_EXPERTS].set(w_router)
    wr_hi = wr_pad.astype(BF16)
    wr_lo = (wr_pad - wr_hi.astype(F32)).astype(BF16)
    wr_cat = jnp.concatenate([wr_hi, wr_lo], axis=-1)

    def lat_row(tm):
        return lambda t: t // (seq // tm)

    def ctx_row(tm):
        return lambda t: batch

    xl = x.reshape(n_l, d)
    xc = ctx.reshape(n_c, d)
    for i in range(depth):
        last = i == depth - 1
        upf = _pad_up(gla_up_f[i], 0)
        upb = _pad_up(gla_up_b[i], GLA_GATE_RANK)
        bias_f = gla_bias_f[i].reshape(npair, 1, LANES)
        bias_b = gla_bias_b[i].reshape(npair, 1, LANES)

        c_qk, c_v, c_g, c_z, c_sq, c_sk, c_sv = _inproj_call(
            xc, norm1[i], mods5, i, ctx_row(tm_c), w_packed, None, tm_c, lc // tm_c)
        l_qk, l_v, l_g, l_z, l_sq, l_sk, l_sv = _inproj_call(
            xl, norm1[i], mods5, i, lat_row(tm_in), w_packed, tabs, tm_in, seq // tm_in)

        s_zero = jnp.zeros((batch, npair, 2, 2 * GLA_DV, LANES), F32)
        oc_f, oc_b, s_ctx = _gla_call(c_qk, c_v, c_z, upf, upb, bias_f, bias_b, s_zero, batch, lc, tl_c)
        ol_f, ol_b, _ = _gla_call(l_qk, l_v, l_z, upf, upb, bias_f, bias_b, s_ctx, batch, seq, tl_l)

        swa_l = _swa_call(l_sq, l_sk, l_sv, c_sk, c_sv, swa_sink[i], batch, seq, lc)
        xl, h2l, lg_l = _outproj_call(ol_f, ol_b, l_g, gla_norm[i], swa_l, w_out_b, xl, mods5, i, lat_row(tm_l),
                                      norm2[i], wr_cat, tm_l)
        if last:
            logits_t = lg_l
            t_all = n_l
        else:
            swa_c = _swa_ctx_call(c_sq, c_sk, c_sv, swa_sink[i], batch, lc)
            xc, h2c, lg_c = _outproj_call(oc_f, oc_b, c_g, gla_norm[i], swa_c, w_out_b, xc, mods5, i, ctx_row(tm_c),
                                          norm2[i], wr_cat, tm_c)
            logits_t = jnp.concatenate([lg_l, lg_c], axis=1)
            t_all = n_l + n_c

        io, wo, seg, pad, blk = _route_call(logits_t, b_router, ROUTE_TILE)
        m = MOE_BLOCK
        padded = blk[:, 0]
        pend = jnp.cumsum(padded)
        ntile = t_all // ROUTE_TILE
        nb = -(-(t_all * TOP_K + ntile * N_EXPERTS * (SUBLANES - 1)) // m) + N_EXPERTS
        p_rows = nb * m
        seg = seg.reshape(-1)
        pad = pad.reshape(-1)
        nused = (pend[-1:] // m).astype(I32)
        tail = nused + jnp.arange(nb - (t_all * TOP_K) // m, dtype=I32)
        zblk = jnp.concatenate([jnp.where(padded > 0, pend // m - 1, -1),
                                jnp.where(tail < nb, tail, -1)]).astype(I32)
        blk_row = jnp.minimum(jnp.arange(nb, dtype=I32), nused - 1) * m
        block_e = jnp.sum((pend[None, :] <= blk_row[:, None]).astype(I32), axis=1)
        rows_e = jnp.sum(pad.reshape(N_EXPERTS, LANES), axis=1)
        own = (block_e[:, None] == jnp.arange(N_EXPERTS, dtype=I32)[None, :]).astype(I32)
        seg_end = jnp.sum(own * (pend - padded + rows_e)[None, :], axis=1)
        blk_idx = jnp.arange(nb, dtype=I32)
        valid = jnp.where(blk_idx < nused, jnp.clip(seg_end - blk_idx * m, 0, m), 0).astype(I32)

        sources = (h2l,) if last else (h2l, h2c)
        xb = _dispatch_call(seg, pad, zblk, io, wo, sources, p_rows)
        eid = jnp.arange(N_EXPERTS, dtype=I32)
        later = (eid[None, :] > eid[:, None]) & (padded[None, :] > 0)
        next_e = jnp.min(jnp.where(later, eid[None, :], N_EXPERTS), axis=1)
        next_e = jnp.where(next_e < N_EXPERTS, next_e, -1).astype(I32)
        yb = _expert_call(block_e, nused, valid, next_e, xb, w_gate, w_up, w_down, i)
        xl_new = _combine_call(seg, pad, yb, io, xl, mods5, i, lat_row(ROUTE_TILE),
                               final_norm if last else None, 0)
        if not last:
            xc = _combine_call(seg, pad, yb, io, xc, mods5, i, ctx_row(0), None, n_l // ROUTE_TILE)
        xl = xl_new
    return xl.reshape(batch, seq, d)
```

```python
import functools

import jax
import jax.numpy as jnp
from jax import lax
from jax.experimental import pallas as pl
from jax.experimental.pallas import tpu as pltpu

F32 = jnp.float32
BF16 = jnp.bfloat16
I32 = jnp.int32
U32 = jnp.uint32

GRID_W = 64
NORM_EPS = 1e-6
GLA_HEADS = 4
GLA_DK = 64
GLA_DV = 128
GLA_KW = GLA_HEADS * GLA_DK
GLA_WIDTH = GLA_HEADS * GLA_DV
GLA_GATE_RANK = 16
GLA_TAU = 16.0
GLA_CHUNK = 64
SWA_HEADS = 8
SWA_KV_HEADS = 2
SWA_HD = 64
SWA_WIDTH = SWA_HEADS * SWA_HD
SWA_KVW = SWA_KV_HEADS * SWA_HD
WINDOW = 128
ATT_BLOCK = 128
ROPE_THETA = 10000.0
ROPE_AXIS_DIM = SWA_HD // 2
ROPE_HALF = ROPE_AXIS_DIM // 2
N_EXPERTS = 16
N_GROUPS = 4
EXPERTS_PER_GROUP = N_EXPERTS // N_GROUPS
TOP_K = 2
IN_SIZES = (GLA_KW, GLA_KW, GLA_WIDTH, GLA_WIDTH, GLA_GATE_RANK, GLA_GATE_RANK, SWA_WIDTH, SWA_KVW, SWA_KVW)

LANES = 128
SUBLANES = 8
VMEM_LIMIT = 48 * 1024 * 1024

ROW_TILE = 1024
GLA_TILE = 2048
GLA_SUM_ROWS = 256
INPROJ_TILE = 1024
CTX_TILE = 256
ADA_COLS = 1536
SWA_BLOCKS_PER_STEP = 8
MOE_BLOCK = 512
ROUTE_TILE = 512
ROUTE_TILES_PER_STEP = 11
SEG_SIZES = (512, 256, 128, 64, 32, 16, 8)
STAGE_ROWS = TOP_K * ROUTE_TILE + LANES

C_QK = 0
C_V = 512
C_G = 1024
C_Z = 1536
C_SQ = 1664
C_SK = 2176
C_SV = 2304
C_END = 2432
SV_LANES = 4 * LANES
NEG_BIG = -1e30
LOG2E = 1.4426950408889634

assert WINDOW == ATT_BLOCK


def _cparams(sem):
    return pltpu.CompilerParams(dimension_semantics=sem, vmem_limit_bytes=VMEM_LIMIT)


def _pack_bf16_pairs(x):
    return _pack_rounded_pairs(x.astype(BF16).astype(F32))


def _pack_rounded_pairs(xr):
    n = xr.shape[1] // 2
    lo = lax.bitcast_convert_type(xr[:, :n], U32)
    hi = lax.bitcast_convert_type(xr[:, n:], U32)
    return (lo >> 16) | hi


def _unpack_bf16_pairs(w):
    lo = lax.bitcast_convert_type(w << 16, F32)
    hi = lax.bitcast_convert_type(w & jnp.uint32(0xFFFF0000), F32)
    return jnp.concatenate([lo.astype(BF16), hi.astype(BF16)], axis=-1)


def _ada_kernel(a_ref, w_ref, b_ref, o_ref):
    a = a_ref[...]
    act = a * jax.nn.sigmoid(a)
    a_hi = act.astype(BF16)
    a_lo = (act - a_hi.astype(F32)).astype(BF16)
    w = w_ref[...]
    w_hi = w.astype(BF16)
    w_lo = (w - w_hi.astype(F32)).astype(BF16)
    t = jnp.dot(jnp.concatenate([a_hi, a_lo], axis=0), w_hi, preferred_element_type=F32)
    n = a.shape[0]
    o_ref[...] = t[:n] + t[n:] + jnp.dot(a_hi, w_lo, preferred_element_type=F32) + b_ref[...]


def _ada_call(rows, w_ada, b_ada):
    depth, d, n6 = w_ada.shape
    tn = ADA_COLS
    return pl.pallas_call(
        _ada_kernel,
        grid=(depth, n6 // tn),
        in_specs=[
            pl.BlockSpec((SUBLANES, d), lambda l, j: (0, 0)),
            pl.BlockSpec((None, d, tn), lambda l, j: (l, 0, j)),
            pl.BlockSpec((None, 1, tn), lambda l, j: (l, 0, j)),
        ],
        out_specs=pl.BlockSpec((None, SUBLANES, tn), lambda l, j: (l, 0, j)),
        out_shape=jax.ShapeDtypeStruct((depth, SUBLANES, n6), F32),
        compiler_params=_cparams(("arbitrary", "arbitrary")),
        name="adaln",
    )(rows, w_ada, b_ada.reshape(depth, 1, n6))


def _rope_tile(xj, cos, sin, lane_lo):
    partner = jnp.where(lane_lo, pltpu.roll(xj, LANES - ROPE_HALF, 1), pltpu.roll(xj, ROPE_HALF, 1))
    return xj * cos + partner * sin


def _inproj_kernel(*refs, rope):
    if rope:
        (x_ref, g_ref, sc_ref, sh_ref, w_ref, cos_ref, sin_ref,
         qk_ref, v_ref, gate_ref, z_ref, sq_ref, sk_ref, sv_ref) = refs
    else:
        (x_ref, g_ref, sc_ref, sh_ref, w_ref,
         qk_ref, v_ref, gate_ref, z_ref, sq_ref, sk_ref, sv_ref) = refs
    half = x_ref.shape[0] // 2
    scale = g_ref[...] * (1.0 + sc_ref[...])
    normed = []
    for r in range(2):
        x = x_ref[r * half:(r + 1) * half, :]
        ms = jnp.mean(x * x, axis=-1, keepdims=True)
        normed.append(((x * lax.rsqrt(ms + NORM_EPS)) * scale + sh_ref[...]).astype(BF16))

    for r in range(2):
        rows = slice(r * half, (r + 1) * half)
        hb = normed[r]

        def proj(a, b):
            return jnp.dot(hb, w_ref[:, a:b], preferred_element_type=F32)

        qk = proj(C_QK, C_V)
        qk_ref[rows, :GLA_KW] = (qk[:, :GLA_KW] * (GLA_DK ** -0.5)).astype(qk_ref.dtype)
        qk_ref[rows, GLA_KW:] = qk[:, GLA_KW:].astype(qk_ref.dtype)
        v_ref[rows, :] = proj(C_V, C_G).astype(v_ref.dtype)
        gate_ref[rows, :] = proj(C_G, C_Z).astype(gate_ref.dtype)
        z_ref[rows, :] = proj(C_Z, C_SQ)
        sq = proj(C_SQ, C_SK) * (SWA_HD ** -0.5 * LOG2E)
        sk = proj(C_SK, C_SV)
        sv = proj(C_SV, C_END)
        if rope:
            cos = cos_ref[rows, :]
            sin = sin_ref[rows, :]
            lane_lo = (lax.broadcasted_iota(I32, cos.shape, 1) % ROPE_AXIS_DIM) < ROPE_HALF
            for j in range(SWA_WIDTH // LANES):
                sl = slice(j * LANES, (j + 1) * LANES)
                sq_ref[rows, sl] = _rope_tile(sq[:, sl], cos, sin, lane_lo).astype(sq_ref.dtype)
            sk = _rope_tile(sk, cos, sin, lane_lo)
        else:
            sq_ref[rows, :] = sq.astype(sq_ref.dtype)
        first = lax.broadcasted_iota(I32, sk.shape, 1) < SWA_HD
        sk_sw = pltpu.roll(sk, SWA_HD, 1)
        sk_ref[rows, :LANES] = jnp.where(first, sk, sk_sw).astype(sk_ref.dtype)
        sk_ref[rows, LANES:] = jnp.where(first, sk_sw, sk).astype(sk_ref.dtype)
        sv_sw = pltpu.roll(sv, SWA_HD, 1)
        sv_ref[rows, 0 * LANES:1 * LANES] = jnp.where(first, sv, 1.0).astype(sv_ref.dtype)
        sv_ref[rows, 1 * LANES:2 * LANES] = jnp.where(first, 1.0, sv_sw).astype(sv_ref.dtype)
        sv_ref[rows, 2 * LANES:3 * LANES] = jnp.where(first, sv_sw, 1.0).astype(sv_ref.dtype)
        sv_ref[rows, 3 * LANES:4 * LANES] = jnp.where(first, 1.0, sv).astype(sv_ref.dtype)


def _inproj_call(x2d, g, mods5, layer, row_of_tile, w_packed, tabs, tm, tiles_per_seq):
    n, d = x2d.shape
    nt = n // tm
    rope = tabs is not None

    def mod_spec(k):
        return pl.BlockSpec((None, None, None, 1, d), lambda t: (layer, row_of_tile(t), k, 0, 0))

    in_specs = [
        pl.BlockSpec((tm, d), lambda t: (t, 0)),
        pl.BlockSpec((1, d), lambda t: (0, 0)),
        mod_spec(1), mod_spec(0),
        pl.BlockSpec((None, d, C_END), lambda t: (layer, 0, 0)),
    ]
    args = [x2d, g.reshape(1, d), mods5, mods5, w_packed]
    if rope:
        in_specs += [pl.BlockSpec((tm, LANES), lambda t: (t % tiles_per_seq, 0))] * 2
        args += [tabs[0], tabs[1]]
    widths = (2 * GLA_KW, GLA_WIDTH, GLA_WIDTH, LANES, SWA_WIDTH, 2 * SWA_KVW, SV_LANES)
    dtypes = (BF16, BF16, BF16, F32, BF16, BF16, BF16)
    return pl.pallas_call(
        functools.partial(_inproj_kernel, rope=rope),
        grid=(nt,),
        in_specs=in_specs,
        out_specs=[pl.BlockSpec((tm, w), lambda t: (t, 0)) for w in widths],
        out_shape=[jax.ShapeDtypeStruct((n, w), dt) for w, dt in zip(widths, dtypes)],
        compiler_params=_cparams(("arbitrary",)),
        name="inproj",
    )(*args)


def _log_sigmoid(x):
    return jnp.minimum(x, 0.0) - jnp.log(1.0 + jnp.exp(-jnp.abs(x)))


def _gla_prepare(q_ref, k_ref, v_ref, z_ref, up_ref, bias_ref, tri_ref, scr, fwd):
    qin_s, km0_s, km1_s, kout_s, vm0_s, vm1_s, dec_s = scr[:7]
    qbd_s = scr[9]
    tl = q_ref.shape[0]
    c = GLA_CHUNK
    z = z_ref[...]
    z_hi = z.astype(BF16).astype(F32)
    zc = z_hi + pltpu.roll(z - z_hi, 2 * GLA_GATE_RANK, 1) + pltpu.roll(z_hi, 4 * GLA_GATE_RANK, 1)
    x = jnp.dot(zc.astype(BF16), up_ref[...], preferred_element_type=F32) + bias_ref[...]
    la = _log_sigmoid(x) * (LOG2E / GLA_TAU)
    hi = la.astype(BF16)
    lo = (la - hi.astype(F32)).astype(BF16)
    hl = jnp.concatenate([hi, lo], axis=1)
    tg = tri_ref.shape[0]
    sums = jnp.concatenate([jnp.dot(tri_ref[...], hl[r0:r0 + tg, :], preferred_element_type=F32)
                            for r0 in range(0, tl, tg)], axis=0)
    b = sums[:, :LANES] + sums[:, LANES:]
    b3 = b.reshape(tl // c, c, LANES)
    edge = b3[:, c - 1:c, :] if fwd else b3[:, 0:1, :]
    btot = jnp.broadcast_to(edge, b3.shape).reshape(tl, LANES)
    q = q_ref[...].astype(F32)
    k = k_ref[...].astype(F32)
    first = (lax.broadcasted_iota(I32, (tl, LANES), 0) % (2 * c)) < c
    q_in = q * jnp.exp2(b)
    qin_s[...] = q_in.astype(BF16)
    qbd_s[...] = jnp.concatenate([jnp.where(first, q_in, 0.0), jnp.where(first, 0.0, q_in)], axis=1).astype(BF16)
    k_in = k * jnp.exp2(-b)
    head0 = lax.broadcasted_iota(I32, (tl, LANES), 1) < GLA_DK
    km0_s[...] = jnp.where(head0, k_in, 0.0).astype(BF16)
    km1_s[...] = jnp.where(head0, 0.0, k_in).astype(BF16)
    k_out = k * jnp.exp2(btot - b)
    kout_s[...] = jnp.concatenate([jnp.where(first, k_out, 0.0), jnp.where(first, 0.0, k_out)], axis=1).astype(BF16)
    dec_s[...] = jnp.exp2(btot)
    vf = v_ref[...].astype(F32)
    vhead0 = lax.broadcasted_iota(I32, (tl, 2 * GLA_DV), 1) < GLA_DV
    vm0_s[...] = jnp.where(vhead0, vf, 0.0).astype(BF16)
    vm1_s[...] = jnp.where(vhead0, 0.0, vf).astype(BF16)


def _gla_increment(v_ref, scr, pair):
    kout_s, u_s = scr[3], scr[7]
    rows = pl.ds(pair * 2 * GLA_CHUNK, 2 * GLA_CHUNK)
    u_t = lax.dot_general(v_ref[rows, :], kout_s[rows, :], (((0,), (0,)), ((), ())),
                          preferred_element_type=F32)
    srow = lax.broadcasted_iota(I32, (2 * GLA_DV, 2 * LANES), 0) // GLA_DV
    scol = (lax.broadcasted_iota(I32, (2 * GLA_DV, 2 * LANES), 1) % LANES) // GLA_DK
    u_s[pair] = jnp.where(srow == scol, u_t, 0.0)


def _gla_states(scr, s_ref, chunk_order):
    dec_s, u_s, sprev_s = scr[6], scr[7], scr[8]
    s_t = s_ref[...]
    for cidx in chunk_order:
        pair, lanes = cidx // 2, pl.ds((cidx % 2) * LANES, LANES)
        sprev_s[pair, :, lanes] = s_t.astype(BF16)
        r0 = cidx * GLA_CHUNK
        s_t = s_t * dec_s[r0:r0 + 1, :] + u_s[pair, :, lanes]
    s_ref[...] = s_t


def _gla_output(o_ref, scr, pair, fwd):
    qin_s, km0_s, km1_s, _, vm0_s, vm1_s, _, _, sprev_s, qbd_s = scr
    c = GLA_CHUNK
    r0 = pair * 2 * c
    rows = pl.ds(r0, 2 * c)
    ca, cb = pl.ds(r0, c), pl.ds(r0 + c, c)
    kst = jnp.concatenate([km0_s[ca, :], km1_s[ca, :], km0_s[cb, :], km1_s[cb, :]], axis=0)
    a = lax.dot_general(qin_s[rows, :], kst, (((1,), (1,)), ((), ())), preferred_element_type=F32)
    ri = lax.broadcasted_iota(I32, (2 * c, 4 * c), 0)
    ci = lax.broadcasted_iota(I32, (2 * c, 4 * c), 1)
    same_chunk = (ri // c) == (ci // (2 * c))
    keep = same_chunk & ((ci % c <= ri % c) if fwd else (ci % c >= ri % c))
    a = jnp.where(keep, a, 0.0).astype(BF16)
    vbd = jnp.concatenate([vm0_s[ca, :], vm1_s[ca, :], vm0_s[cb, :], vm1_s[cb, :]], axis=0)
    o = jnp.dot(a, vbd, preferred_element_type=F32)
    o = o + lax.dot_general(qbd_s[rows, :], sprev_s[pair], (((1,), (1,)), ((), ())), preferred_element_type=F32)
    o_ref[rows, :] = o.astype(o_ref.dtype)


def _gla_kernel(qf_ref, kf_ref, vf_ref, zf_ref, qb_ref, kb_ref, vb_ref, zb_ref,
                upf_ref, upb_ref, bf_ref, bb_ref, trif_ref, trib_ref, s0_ref,
                of_ref, ob_ref, sfin_ref, sf_scr, sb_scr, *scr, nchunk):
    i = pl.program_id(2)
    nt = pl.num_programs(2)
    scr_f, scr_b = scr[:len(scr) // 2], scr[len(scr) // 2:]

    @pl.when(i == 0)
    def _():
        sf_scr[...] = s0_ref[0]
        sb_scr[...] = s0_ref[1]

    _gla_prepare(qf_ref, kf_ref, vf_ref, zf_ref, upf_ref, bf_ref, trif_ref, scr_f, True)
    _gla_prepare(qb_ref, kb_ref, vb_ref, zb_ref, upb_ref, bb_ref, trib_ref, scr_b, False)
    for pair in range(nchunk // 2):
        _gla_increment(vf_ref, scr_f, pair)
        _gla_increment(vb_ref, scr_b, pair)
    _gla_states(scr_f, sf_scr, range(nchunk))
    _gla_states(scr_b, sb_scr, range(nchunk - 1, -1, -1))
    for pair in range(nchunk // 2):
        _gla_output(of_ref, scr_f, pair, True)
        _gla_output(ob_ref, scr_b, pair, False)

    @pl.when(i == nt - 1)
    def _():
        sfin_ref[0] = sf_scr[...]
        sfin_ref[1] = sb_scr[...]


def _gla_call(qk, v, z, upf_pad, upb_pad, bias_f, bias_b, s0, batch, seq, tl):
    n = qk.shape[0]
    nt = seq // tl
    npair = GLA_HEADS // 2

    def fwd_row(b, p, i):
        return b * nt + i

    def bwd_row(b, p, i):
        return b * nt + (nt - 1 - i)

    def specs(row):
        return [
            pl.BlockSpec((tl, LANES), lambda b, p, i: (row(b, p, i), p)),
            pl.BlockSpec((tl, LANES), lambda b, p, i: (row(b, p, i), npair + p)),
            pl.BlockSpec((tl, 2 * GLA_DV), lambda b, p, i: (row(b, p, i), p)),
            pl.BlockSpec((tl, LANES), lambda b, p, i: (row(b, p, i), 0)),
        ]

    in_specs = specs(fwd_row) + specs(bwd_row) + [
        pl.BlockSpec((None, LANES, LANES), lambda b, p, i: (p, 0, 0)),
        pl.BlockSpec((None, LANES, LANES), lambda b, p, i: (p, 0, 0)),
        pl.BlockSpec((None, 1, LANES), lambda b, p, i: (p, 0, 0)),
        pl.BlockSpec((None, 1, LANES), lambda b, p, i: (p, 0, 0)),
        pl.BlockSpec((None, GLA_SUM_ROWS, GLA_SUM_ROWS), lambda b, p, i: (0, 0, 0)),
        pl.BlockSpec((None, GLA_SUM_ROWS, GLA_SUM_ROWS), lambda b, p, i: (1, 0, 0)),
        pl.BlockSpec((None, None, 2, 2 * GLA_DV, LANES), lambda b, p, i: (b, p, 0, 0, 0)),
    ]
    assert tl % GLA_SUM_ROWS == 0
    ri = jnp.arange(GLA_SUM_ROWS)[:, None]
    ci = jnp.arange(GLA_SUM_ROWS)[None, :]
    same = (ri // GLA_CHUNK) == (ci // GLA_CHUNK)
    tri = jnp.stack([same & (ci <= ri), same & (ci >= ri)]).astype(BF16)
    npairs = tl // (2 * GLA_CHUNK)
    dir_scratch = [
        pltpu.VMEM((tl, LANES), BF16),
        pltpu.VMEM((tl, LANES), BF16),
        pltpu.VMEM((tl, LANES), BF16),
        pltpu.VMEM((tl, 2 * LANES), BF16),
        pltpu.VMEM((tl, 2 * GLA_DV), BF16),
        pltpu.VMEM((tl, 2 * GLA_DV), BF16),
        pltpu.VMEM((tl, LANES), F32),
        pltpu.VMEM((npairs, 2 * GLA_DV, 2 * LANES), F32),
        pltpu.VMEM((npairs, 2 * GLA_DV, 2 * LANES), BF16),
        pltpu.VMEM((tl, 2 * LANES), BF16),
    ]
    out_specs = [
        pl.BlockSpec((tl, 2 * GLA_DV), lambda b, p, i: (fwd_row(b, p, i), p)),
        pl.BlockSpec((tl, 2 * GLA_DV), lambda b, p, i: (bwd_row(b, p, i), p)),
        pl.BlockSpec((None, None, 2, 2 * GLA_DV, LANES), lambda b, p, i: (b, p, 0, 0, 0)),
    ]
    out_shape = [
        jax.ShapeDtypeStruct((n, GLA_WIDTH), BF16),
        jax.ShapeDtypeStruct((n, GLA_WIDTH), BF16),
        jax.ShapeDtypeStruct((batch, npair, 2, 2 * GLA_DV, LANES), F32),
    ]
    return pl.pallas_call(
        functools.partial(_gla_kernel, nchunk=tl // GLA_CHUNK),
        grid=(batch, npair, nt),
        in_specs=in_specs,
        out_specs=out_specs,
        out_shape=out_shape,
        scratch_shapes=[pltpu.VMEM((2 * GLA_DV, LANES), F32), pltpu.VMEM((2 * GLA_DV, LANES), F32)]
        + dir_scratch + dir_scratch,
        compiler_params=_cparams(("arbitrary", "arbitrary", "arbitrary")),
        name="gla",
    )(qk, qk, v, z, qk, qk, v, z, upf_pad, upb_pad, bias_f, bias_b, tri, tri, s0)


def _swa_block(sink_ref, q_ref, o_ref, row0, kall, vall, prev_ok, next_ok):
    w = ATT_BLOCK
    local = prev_ok is not None
    qrows = pl.ds(row0, w)
    r2 = lax.broadcasted_iota(I32, (2 * w, w), 0) % w
    c2 = lax.broadcasted_iota(I32, (2 * w, w), 1)
    if local:
        bias_prev = jnp.where((c2 >= r2) & prev_ok, 0.0, NEG_BIG)
        bias_next = jnp.where((c2 <= r2) & next_ok, 0.0, NEG_BIG)
    first = c2 < SWA_HD
    top = lax.broadcasted_iota(I32, (2 * w, 1), 0) < w
    scores = []
    for g in range(SWA_KV_HEADS):
        kd = kall[:, g * LANES:(g + 1) * LANES]
        qs = jnp.concatenate([q_ref[qrows, (2 * g) * LANES:(2 * g + 1) * LANES],
                              q_ref[qrows, (2 * g + 1) * LANES:(2 * g + 2) * LANES]], axis=0).astype(F32)
        for half in range(2):
            qm = (jnp.where(first, qs, 0.0) if half == 0 else jnp.where(first, 0.0, qs)).astype(BF16)
            s = lax.dot_general(qm, kd, (((1,), (1,)), ((), ())), preferred_element_type=F32)
            if local:
                s = jnp.concatenate([s[:, :w] + bias_prev, s[:, w:2 * w], s[:, 2 * w:3 * w] + bias_next,
                                     s[:, 3 * w:]], axis=1)
            scores.append(s)
    for g in range(SWA_KV_HEADS):
        outs = []
        for half in range(2):
            s = scores[2 * g + half]
            sk = jnp.where(top, sink_ref[4 * g + half], sink_ref[4 * g + 2 + half]) * LOG2E
            m = jnp.maximum(jnp.max(s, axis=-1, keepdims=True), sk)
            p = jnp.exp2((s - m).astype(BF16))
            va = vall[:, (2 * g + half) * LANES:(2 * g + half + 1) * LANES]
            acc = jnp.dot(p, va, preferred_element_type=F32)
            den = pltpu.roll(acc, SWA_HD, 1) + jnp.exp2(sk - m)
            outs.append(acc / den)
        out = jnp.where(first, outs[0], outs[1])
        o_ref[qrows, (2 * g) * LANES:(2 * g + 1) * LANES] = out[:w].astype(o_ref.dtype)
        o_ref[qrows, (2 * g + 1) * LANES:(2 * g + 2) * LANES] = out[w:].astype(o_ref.dtype)


def _swa_kernel(*refs, nstep, local):
    w = ATT_BLOCK
    if not local:
        sink_ref, q_ref, kx_ref, vx_ref, o_ref = refs
        _swa_block(sink_ref, q_ref, o_ref, 0, kx_ref[...], vx_ref[...], None, None)
        return
    sink_ref, q_ref, kp_ref, kc_ref, kn_ref, vp_ref, vc_ref, vn_ref, kx_ref, vx_ref, o_ref = refs
    i = pl.program_id(1)
    nblk = q_ref.shape[0] // w
    k_blocks = [kp_ref[...]] + [kc_ref[j * w:(j + 1) * w, :] for j in range(nblk)] + [kn_ref[...]]
    v_blocks = [vp_ref[...]] + [vc_ref[j * w:(j + 1) * w, :] for j in range(nblk)] + [vn_ref[...]]
    exists = [i > 0] + [True] * nblk + [i < nstep - 1]
    for sub in range(nblk):
        kall = jnp.concatenate(k_blocks[sub:sub + 3] + [kx_ref[...]], axis=0)
        vall = jnp.concatenate(v_blocks[sub:sub + 3] + [vx_ref[...]], axis=0)
        _swa_block(sink_ref, q_ref, o_ref, sub * w, kall, vall, exists[sub], exists[sub + 2])


def _swa_call(sq, skd, svd, kcd, vcd, sink, batch, seq, lc):
    n = sq.shape[0]
    w = ATT_BLOCK
    nb = seq // w
    nblk = SWA_BLOCKS_PER_STEP if nb % SWA_BLOCKS_PER_STEP == 0 else 1
    nstep = nb // nblk
    kvw = 2 * SWA_KVW

    def pair(b, i):
        return (b * nstep + i, 0)

    def before(b, i):
        return (b * nb + jnp.maximum(nblk * i - 1, 0), 0)

    def after(b, i):
        return (b * nb + jnp.minimum(nblk * (i + 1), nb - 1), 0)

    def kv_specs(width):
        return [pl.BlockSpec((w, width), before), pl.BlockSpec((nblk * w, width), pair),
                pl.BlockSpec((w, width), after)]

    def ctx_spec(width):
        return pl.BlockSpec((lc, width), lambda b, i: (b, 0))

    return pl.pallas_call(
        functools.partial(_swa_kernel, nstep=nstep, local=True),
        grid=(batch, nstep),
        in_specs=[pl.BlockSpec(memory_space=pltpu.SMEM), pl.BlockSpec((nblk * w, SWA_WIDTH), pair)]
        + kv_specs(kvw) + kv_specs(SV_LANES) + [ctx_spec(kvw), ctx_spec(SV_LANES)],
        out_specs=pl.BlockSpec((nblk * w, SWA_WIDTH), pair),
        out_shape=jax.ShapeDtypeStruct((n, SWA_WIDTH), BF16),
        compiler_params=_cparams(("arbitrary", "arbitrary")),
        name="swa",
    )(sink, sq, skd, skd, skd, svd, svd, svd, kcd, vcd)


def _swa_ctx_call(sq, kcd, vcd, sink, batch, lc):
    n = sq.shape[0]
    w = ATT_BLOCK
    nb = lc // w
    kvw = 2 * SWA_KVW
    return pl.pallas_call(
        functools.partial(_swa_kernel, nstep=0, local=False),
        grid=(batch, nb),
        in_specs=[pl.BlockSpec(memory_space=pltpu.SMEM),
                  pl.BlockSpec((w, SWA_WIDTH), lambda b, i: (b * nb + i, 0)),
                  pl.BlockSpec((lc, kvw), lambda b, i: (b, 0)),
                  pl.BlockSpec((lc, SV_LANES), lambda b, i: (b, 0))],
        out_specs=pl.BlockSpec((w, SWA_WIDTH), lambda b, i: (b * nb + i, 0)),
        out_shape=jax.ShapeDtypeStruct((n, SWA_WIDTH), BF16),
        compiler_params=_cparams(("arbitrary", "arbitrary")),
        name="swa_ctx",
    )(sink, sq, kcd, vcd)


def _outproj_kernel(of_ref, ob_ref, gate_ref, gn_ref, swa_ref, w_ref, x_ref, g1_ref, n2_ref, sc_ref, sh_ref,
                    wr_ref, xo_ref, h2_ref, lg_ref):
    o = of_ref[...].astype(F32) + ob_ref[...].astype(F32)
    parts = []
    for h in range(GLA_HEADS):
        oh = o[:, h * GLA_DV:(h + 1) * GLA_DV]
        ms = jnp.mean(oh * oh, axis=-1, keepdims=True)
        parts.append(oh * lax.rsqrt(ms + NORM_EPS))
    on = jnp.concatenate(parts, axis=-1) * gn_ref[...]
    gate = gate_ref[...].astype(F32)
    gla = on * (gate * jax.nn.sigmoid(gate))
    mix = jnp.concatenate([gla.astype(BF16), swa_ref[...]], axis=-1)
    y = jnp.dot(mix, w_ref[...], preferred_element_type=F32)
    xo = x_ref[...] + g1_ref[...] * y
    xo_ref[...] = xo
    ms = jnp.mean(xo * xo, axis=-1, keepdims=True)
    h2 = (xo * lax.rsqrt(ms + NORM_EPS)) * (n2_ref[...] * (1.0 + sc_ref[...])) + sh_ref[...]
    hi = h2.astype(BF16)
    hi_f = hi.astype(F32)
    h2_ref[...] = _pack_rounded_pairs(hi_f)
    lo = (h2 - hi_f).astype(BF16)
    both = jnp.dot(hi, wr_ref[...], preferred_element_type=F32)
    lg = both[:, :LANES] + both[:, LANES:] + jnp.dot(lo, wr_ref[:, :LANES], preferred_element_type=F32)
    lg_ref[...] = jnp.transpose(lg)[:N_EXPERTS, :]


def _outproj_call(o_f, o_b, gate, gn, swa, w_out_b, x2d, mods5, layer, row_of_tile, n2, wr_cat, tm):
    n, d = x2d.shape
    nt = n // tm

    def mod_spec(k):
        return pl.BlockSpec((None, None, None, 1, d), lambda t: (layer, row_of_tile(t), k, 0, 0))

    return pl.pallas_call(
        _outproj_kernel,
        grid=(nt,),
        in_specs=[
            pl.BlockSpec((tm, GLA_WIDTH), lambda t: (t, 0)),
            pl.BlockSpec((tm, GLA_WIDTH), lambda t: (t, 0)),
            pl.BlockSpec((tm, GLA_WIDTH), lambda t: (t, 0)),
            pl.BlockSpec((1, GLA_WIDTH), lambda t: (0, 0)),
            pl.BlockSpec((tm, SWA_WIDTH), lambda t: (t, 0)),
            pl.BlockSpec((None, d, d), lambda t: (layer, 0, 0)),
            pl.BlockSpec((tm, d), lambda t: (t, 0)),
            mod_spec(2),
            pl.BlockSpec((1, d), lambda t: (0, 0)),
            mod_spec(4), mod_spec(3),
            pl.BlockSpec((d, 2 * LANES), lambda t: (0, 0)),
        ],
        out_specs=[
            pl.BlockSpec((tm, d), lambda t: (t, 0)),
            pl.BlockSpec((tm, d // 2), lambda t: (t, 0)),
            pl.BlockSpec((N_EXPERTS, tm), lambda t: (0, t)),
        ],
        out_shape=[
            jax.ShapeDtypeStruct((n, d), F32),
            jax.ShapeDtypeStruct((n, d // 2), U32),
            jax.ShapeDtypeStruct((N_EXPERTS, n), F32),
        ],
        compiler_params=_cparams(("arbitrary",)),
        name="outproj",
    )(o_f, o_b, gate, gn.reshape(1, GLA_WIDTH), swa, w_out_b, x2d, mods5, n2.reshape(1, d), mods5, mods5, wr_cat)


def _first_index(vals, target):
    idx = jnp.full(target.shape, len(vals) - 1, I32)
    for i in range(len(vals) - 2, -1, -1):
        idx = jnp.where(vals[i] == target, i, idx)
    return idx


def _route_kernel(lg_ref, br_ref, io_ref, wo_ref, seg_ref, pad_ref, blk_ref, cnt_scr, seg_scr, pad_scr, *, tn):
    phase = pl.program_id(0)
    step = pl.program_id(1)
    group = lg_ref.shape[1] // tn

    @pl.when((phase == 0) & (step == 0))
    def _():
        cnt_scr[...] = jnp.zeros_like(cnt_scr)
        seg_scr[...] = jnp.zeros_like(seg_scr)
        pad_scr[...] = jnp.zeros_like(pad_scr)

    @pl.when((phase == 1) & (step == 0))
    def _():
        seg_len = jnp.floor((cnt_scr[...] + (SUBLANES - 1)) * (1.0 / SUBLANES)) * SUBLANES
        rows_e = jnp.sum(seg_len, axis=1, keepdims=True)
        blocks = jnp.floor((rows_e + (MOE_BLOCK - 1)) * (1.0 / MOE_BLOCK)) * MOE_BLOCK
        r128 = lax.broadcasted_iota(I32, (LANES, LANES), 0)
        c128 = lax.broadcasted_iota(I32, (LANES, LANES), 1)
        before_tile = jnp.where(r128 < c128, 1.0, 0.0).astype(BF16)
        seg = jnp.dot((seg_len * (1.0 / SUBLANES)).astype(BF16), before_tile,
                      preferred_element_type=F32) * SUBLANES
        run = jnp.zeros((1, 1), F32)
        for e in range(N_EXPERTS):
            seg_scr[e:e + 1, :] = seg[e:e + 1, :] + run
            run = run + blocks[e:e + 1, :]
        pad_scr[...] = seg_len
        seg_ref[...] = seg_scr[...].astype(I32)
        pad_ref[...] = seg_len.astype(I32)
        blk_ref[...] = jnp.broadcast_to(blocks, blk_ref.shape).astype(I32)

    for g in range(group):
        lanes = slice(g * tn, (g + 1) * tn)
        _route_tile(lg_ref[:, lanes], br_ref, io_ref, wo_ref, lanes, step * group + g, phase, cnt_scr, pad_scr)


def _route_tile(lg, br_ref, io_ref, wo_ref, lanes, t, phase, cnt_scr, pad_scr):
    tn = lg.shape[1]
    tile_lane = lax.broadcasted_iota(I32, (N_EXPERTS, LANES), 1)
    s = jax.nn.sigmoid(lg)
    sb = s + br_ref[...]
    rows_s = [s[e:e + 1, :] for e in range(N_EXPERTS)]
    rows_b = [sb[e:e + 1, :] for e in range(N_EXPERTS)]
    gscore, gi1, gi2 = [], [], []
    epg = EXPERTS_PER_GROUP
    for g in range(N_GROUPS):
        a = rows_b[g * epg:(g + 1) * epg]
        m1 = functools.reduce(jnp.maximum, a)
        i1 = _first_index(a, m1)
        rest = [jnp.where(i1 == i, -jnp.inf, a[i]) for i in range(epg)]
        m2 = functools.reduce(jnp.maximum, rest)
        i2 = _first_index(rest, m2)
        gscore.append(m1 + m2)
        gi1.append(i1)
        gi2.append(i2)
    gm = functools.reduce(jnp.maximum, gscore)
    gsel = _first_index(gscore, gm)
    i1 = gi1[N_GROUPS - 1]
    i2 = gi2[N_GROUPS - 1]
    for g in range(N_GROUPS - 2, -1, -1):
        i1 = jnp.where(gsel == g, gi1[g], i1)
        i2 = jnp.where(gsel == g, gi2[g], i2)
    idx0 = gsel * epg + i1
    idx1 = gsel * epg + i2
    s0 = jnp.zeros_like(rows_s[0])
    s1 = jnp.zeros_like(rows_s[0])
    for e in range(N_EXPERTS):
        s0 = jnp.where(idx0 == e, rows_s[e], s0)
        s1 = jnp.where(idx1 == e, rows_s[e], s1)
    tot = s0 + s1
    w0 = s0 / tot
    w1 = s1 / tot

    eidx = lax.broadcasted_iota(I32, (N_EXPERTS, tn), 0)
    oh0 = eidx == idx0
    oh1 = eidx == idx1
    oh = jnp.where(oh0 | oh1, 1.0, 0.0)

    @pl.when(phase == 0)
    def _():
        cnt_scr[...] = cnt_scr[...] + jnp.where(tile_lane == t, jnp.sum(oh, axis=1, keepdims=True), 0.0)

    @pl.when(phase == 1)
    def _():
        rr = lax.broadcasted_iota(I32, (tn, tn), 0)
        cc = lax.broadcasted_iota(I32, (tn, tn), 1)
        upper = jnp.where(rr < cc, 1.0, 0.0).astype(BF16)
        before = jnp.dot(oh.astype(BF16), upper, preferred_element_type=F32)
        pad_col = jnp.sum(jnp.where(tile_lane == t, pad_scr[...], 0.0), axis=1, keepdims=True)
        run = jnp.zeros((1, 1), F32)
        offs = []
        for e in range(N_EXPERTS):
            offs.append(run)
            run = run + pad_col[e:e + 1, :]
        pos = before + jnp.concatenate(offs, axis=0)
        spos0 = jnp.sum(jnp.where(oh0, pos, 0.0), axis=0, keepdims=True)
        spos1 = jnp.sum(jnp.where(oh1, pos, 0.0), axis=0, keepdims=True)
        zi = jnp.zeros((SUBLANES - 4, tn), I32)
        io_ref[:, lanes] = jnp.concatenate([idx0, idx1, spos0.astype(I32), spos1.astype(I32), zi], axis=0)
        wo_ref[:, lanes] = jnp.concatenate([w0, w1, jnp.zeros((SUBLANES - 2, tn), F32)], axis=0)


def _route_call(logits_t, b_router, tn):
    ne, t_all = logits_t.shape
    ntile = t_all // tn
    assert ntile <= LANES
    group = max(g for g in range(1, ROUTE_TILES_PER_STEP + 1) if ntile % g == 0)
    tw = group * tn
    tok_out = pl.BlockSpec((SUBLANES, tw), lambda p, t: (0, t * p))
    tab_out = pl.BlockSpec((ne, LANES), lambda p, t: (0, 0))
    tab = jax.ShapeDtypeStruct((ne, LANES), I32)
    return pl.pallas_call(
        functools.partial(_route_kernel, tn=tn),
        grid=(2, ntile // group),
        in_specs=[pl.BlockSpec((ne, tw), lambda p, t: (0, t)), pl.BlockSpec((ne, 1), lambda p, t: (0, 0))],
        out_specs=[tok_out, tok_out, tab_out, tab_out, tab_out],
        out_shape=[
            jax.ShapeDtypeStruct((SUBLANES, t_all), I32),
            jax.ShapeDtypeStruct((SUBLANES, t_all), F32),
            tab, tab, tab,
        ],
        scratch_shapes=[pltpu.VMEM((ne, LANES), F32)] * 3,
        compiler_params=_cparams(("arbitrary", "arbitrary")),
        name="route",
    )(logits_t, b_router.reshape(ne, 1))


def _segment_copies(seg_ref, pad_ref, t, stage, hbm, sem, to_hbm, wait):
    if wait:
        total = functools.reduce(lambda a, b: a + b, [pad_ref[e * LANES + t] for e in range(N_EXPERTS)])
        for size in (2 * SEG_SIZES[0],) + SEG_SIZES:
            @pl.when((total & size) != 0)
            def _(size=size):
                src, dst = stage.at[pl.ds(0, size), :], hbm.at[pl.ds(0, size), :]
                if not to_hbm:
                    src, dst = dst, src
                pltpu.make_async_copy(src, dst, sem).wait()
        return
    loc = 0
    for e in range(N_EXPERTS):
        n = pad_ref[e * LANES + t]
        start = seg_ref[e * LANES + t]
        for size in SEG_SIZES:
            off = n & (-2 * size)

            @pl.when((n & size) != 0)
            def _(off=off, size=size, loc=loc, start=start, e=e):
                s_rows = pl.ds(pl.multiple_of(loc + off, SUBLANES), size)
                h_rows = pl.ds(pl.multiple_of(start + off, SUBLANES), size)
                src, dst = (stage.at[s_rows, :], hbm.at[h_rows, :])
                if not to_hbm:
                    src, dst = dst, src
                pltpu.make_async_copy(src, dst, sem).start(priority=e % 2)
        loc = loc + n


def _sort_matrix(io_ref, rows):
    r = lax.broadcasted_iota(I32, (rows, io_ref.shape[1]), 0)
    return r == io_ref[2:3, :], r == io_ref[3:4, :]


def _dispatch_kernel(seg_ref, pad_ref, zblk_ref, *refs, tiles):
    srcs = refs[:len(tiles)]
    io_ref, wo_ref, xb_hbm, stage, zbuf, sem = refs[len(tiles):]
    i = pl.program_id(0)
    m = MOE_BLOCK

    def zero_copy(j):
        start = pl.multiple_of(zblk_ref[j] * m, m)
        return pltpu.make_async_copy(zbuf, xb_hbm.at[pl.ds(start, m), :], sem.at[0])

    @pl.when(i == 0)
    def _():
        zbuf[...] = jnp.zeros_like(zbuf)
        for j in range(zblk_ref.shape[0]):
            @pl.when(zblk_ref[j] >= 0)
            def _():
                zero_copy(j).start()
        for j in range(zblk_ref.shape[0]):
            @pl.when(zblk_ref[j] >= 0)
            def _():
                zero_copy(j).wait()

    hp = srcs[0][...]
    if len(srcs) == 2:
        hp = jnp.where(i < tiles[0], hp, srcs[1][...])
    x = _unpack_bf16_pairs(hp)
    m0, m1 = _sort_matrix(io_ref, STAGE_ROWS)
    sort = jnp.where(m0 | m1, 1.0, 0.0).astype(BF16)
    xs = jnp.dot(sort, x, preferred_element_type=F32)
    dh = x.shape[1] // 2
    slot = i % 2
    cur = stage.at[slot]
    cur[:, :dh] = _pack_rounded_pairs(xs)
    ws = jnp.sum(jnp.where(m0, wo_ref[0:1, :], 0.0) + jnp.where(m1, wo_ref[1:2, :], 0.0), axis=1, keepdims=True)
    cur[:, dh:] = jnp.broadcast_to(lax.bitcast_convert_type(ws, U32), (STAGE_ROWS, LANES))
    _segment_copies(seg_ref, pad_ref, i, cur, xb_hbm, sem.at[slot], to_hbm=True, wait=False)

    @pl.when(i > 0)
    def _():
        _segment_copies(seg_ref, pad_ref, i - 1, stage.at[1 - slot], xb_hbm, sem.at[1 - slot], to_hbm=True, wait=True)

    @pl.when(i == pl.num_programs(0) - 1)
    def _():
        _segment_copies(seg_ref, pad_ref, i, cur, xb_hbm, sem.at[slot], to_hbm=True, wait=True)


def _dispatch_call(seg, pad, zblk, io, wo, sources, p_rows):
    tile = ROUTE_TILE
    dh = sources[0].shape[1]
    tiles = tuple(s.shape[0] // tile for s in sources)
    firsts = tuple(sum(tiles[:k]) for k in range(len(tiles)))

    def src_spec(first, ntile):
        return pl.BlockSpec((tile, dh), lambda i, *_: (jnp.clip(i - first, 0, ntile - 1), 0))

    tok_spec = pl.BlockSpec((SUBLANES, tile), lambda i, *_: (0, i))
    grid_spec = pltpu.PrefetchScalarGridSpec(
        num_scalar_prefetch=3,
        grid=(sum(tiles),),
        in_specs=[src_spec(f, n) for f, n in zip(firsts, tiles)] + [tok_spec, tok_spec],
        out_specs=pl.BlockSpec(memory_space=pl.ANY),
        scratch_shapes=[pltpu.VMEM((2, STAGE_ROWS, dh + LANES), U32), pltpu.VMEM((MOE_BLOCK, dh + LANES), U32),
                        pltpu.SemaphoreType.DMA((2,))],
    )
    return pl.pallas_call(
        functools.partial(_dispatch_kernel, tiles=tiles),
        grid_spec=grid_spec,
        out_shape=jax.ShapeDtypeStruct((p_rows, dh + LANES), U32),
        compiler_params=_cparams(("arbitrary",)),
        name="dispatch",
    )(seg, pad, zblk, *sources, io, wo)


def _expert_kernel(be_ref, nu_ref, va_ref, ne_ref, x_ref, wg_hbm, wu_hbm, wd_hbm, o_ref,
                   wg_b, wu_b, wd_b, wg_f, wu_f, wd_f, switch_ref, sem, *, layer):
    j = pl.program_id(0)
    dh = o_ref.shape[1]
    half = o_ref.shape[0] // 2
    valid = va_ref[j]
    e = be_ref[j]

    def weight_copies(expert, slot):
        return [pltpu.make_async_copy(hbm.at[layer, expert], buf.at[slot], sem.at[slot, k])
                for k, (hbm, buf) in enumerate(((wg_hbm, wg_f), (wu_hbm, wu_f), (wd_hbm, wd_f)))]

    @pl.when(j == 0)
    def _():
        switch_ref[0] = 0
        for cp in weight_copies(e, 0):
            cp.start()

    @pl.when((j == 0) | (e != be_ref[jnp.maximum(j - 1, 0)]))
    def _():
        slot = switch_ref[0] % 2
        for cp in weight_copies(e, slot):
            cp.wait()
        wg_b[...] = wg_f[slot].astype(BF16)
        wu_b[...] = wu_f[slot].astype(BF16)
        wd_b[...] = wd_f[slot].astype(BF16)
        nxt = ne_ref[e]

        @pl.when(nxt >= 0)
        def _():
            for cp in weight_copies(nxt, 1 - slot):
                cp.start()
        switch_ref[0] = switch_ref[0] + 1

    def ffn_halves(halves):
        gates = []
        for r in halves:
            x = _unpack_bf16_pairs(x_ref[r * half:(r + 1) * half, :dh])
            gates.append((jnp.dot(x, wg_b[...], preferred_element_type=F32),
                          jnp.dot(x, wu_b[...], preferred_element_type=F32)))
        for r, (a, u) in zip(halves, gates):
            hmid = (a * jax.nn.sigmoid(a)) * u
            y = jnp.dot(hmid.astype(BF16), wd_b[...], preferred_element_type=F32)
            row_w = lax.bitcast_convert_type(x_ref[r * half:(r + 1) * half, dh:dh + 1], F32)
            o_ref[r * half:(r + 1) * half, :] = _pack_bf16_pairs(y * row_w)

    @pl.when(valid > half)
    def _():
        ffn_halves((0, 1))

    @pl.when((valid > 0) & (valid <= half))
    def _():
        ffn_halves((0,))
        o_ref[half:, :] = jnp.zeros((half, dh), o_ref.dtype)

    @pl.when(valid == 0)
    def _():
        o_ref[...] = jnp.zeros_like(o_ref)


def _expert_call(block_e, nused, valid, next_e, xb, wg, wu, wd, layer):
    p_rows = xb.shape[0]
    dh = xb.shape[1] - LANES
    d = 2 * dh
    m = MOE_BLOCK
    nb = p_rows // m
    de = wg.shape[-1]

    def xmap(j, be, nu, va, ne):
        return (jnp.minimum(j, nu[0] - 1), 0)

    any_spec = pl.BlockSpec(memory_space=pl.ANY)
    grid_spec = pltpu.PrefetchScalarGridSpec(
        num_scalar_prefetch=4,
        grid=(nb,),
        in_specs=[pl.BlockSpec((m, dh + LANES), xmap), any_spec, any_spec, any_spec],
        out_specs=pl.BlockSpec((m, dh), lambda j, be, nu, va, ne: (j, 0)),
        scratch_shapes=[
            pltpu.VMEM((d, de), BF16), pltpu.VMEM((d, de), BF16), pltpu.VMEM((de, d), BF16),
            pltpu.VMEM((2, d, de), F32), pltpu.VMEM((2, d, de), F32), pltpu.VMEM((2, de, d), F32),
            pltpu.SMEM((1,), I32), pltpu.SemaphoreType.DMA((2, 3)),
        ],
    )
    return pl.pallas_call(
        functools.partial(_expert_kernel, layer=layer),
        grid_spec=grid_spec,
        out_shape=jax.ShapeDtypeStruct((p_rows, dh), U32),
        compiler_params=_cparams(("arbitrary",)),
        name="experts",
    )(block_e, nused, valid, next_e, xb, wg, wu, wd)


def _combine_kernel(*refs, tile0, final):
    if final:
        seg_ref, pad_ref, yb_hbm, io_ref, x_ref, g2_ref, fn_ref, o_ref, stage, sem = refs
    else:
        seg_ref, pad_ref, yb_hbm, io_ref, x_ref, g2_ref, o_ref, stage, sem = refs
    i = pl.program_id(0)

    slot = i % 2

    @pl.when(i == 0)
    def _():
        stage[...] = jnp.zeros_like(stage)
        _segment_copies(seg_ref, pad_ref, tile0, stage.at[0], yb_hbm, sem.at[0], to_hbm=False, wait=False)

    @pl.when(i + 1 < pl.num_programs(0))
    def _():
        _segment_copies(seg_ref, pad_ref, tile0 + i + 1, stage.at[1 - slot], yb_hbm, sem.at[1 - slot],
                        to_hbm=False, wait=False)

    _segment_copies(seg_ref, pad_ref, tile0 + i, stage.at[slot], yb_hbm, sem.at[slot], to_hbm=False, wait=True)
    rows = _unpack_bf16_pairs(stage[slot])
    m0, m1 = _sort_matrix(io_ref, STAGE_ROWS)
    pick = jnp.where(m0 | m1, 1.0, 0.0).astype(BF16)
    y = lax.dot_general(pick, rows, (((0,), (0,)), ((), ())), preferred_element_type=F32)
    xo = x_ref[...] + g2_ref[...] * y
    if final:
        ms = jnp.mean(xo * xo, axis=-1, keepdims=True)
        xo = (xo * lax.rsqrt(ms + NORM_EPS)) * fn_ref[...]
    o_ref[...] = xo


def _combine_call(seg, pad, yb, io, x2d, mods5, layer, row_of_tile, final_g, tile0):
    n, d = x2d.shape
    tm = ROUTE_TILE
    final = final_g is not None
    in_specs = [
        pl.BlockSpec(memory_space=pl.ANY),
        pl.BlockSpec((SUBLANES, tm), lambda t, *_: (0, tile0 + t)),
        pl.BlockSpec((tm, d), lambda t, *_: (t, 0)),
        pl.BlockSpec((None, None, None, 1, d), lambda t, *_: (layer, row_of_tile(t), 5, 0, 0)),
    ]
    args = [yb, io, x2d, mods5]
    if final:
        in_specs.append(pl.BlockSpec((1, d), lambda t, *_: (0, 0)))
        args.append(final_g.reshape(1, d))
    grid_spec = pltpu.PrefetchScalarGridSpec(
        num_scalar_prefetch=2,
        grid=(n // tm,),
        in_specs=in_specs,
        out_specs=pl.BlockSpec((tm, d), lambda t, *_: (t, 0)),
        scratch_shapes=[pltpu.VMEM((2, STAGE_ROWS, d // 2), U32), pltpu.SemaphoreType.DMA((2,))],
    )
    return pl.pallas_call(
        functools.partial(_combine_kernel, tile0=tile0, final=final),
        grid_spec=grid_spec,
        out_shape=jax.ShapeDtypeStruct((n, d), F32),
        compiler_params=_cparams(("arbitrary",)),
        name="combine",
    )(seg, pad, *args)


def _pack_w_kernel(w_ref, o_ref):
    cut = C_Z + 2 * GLA_GATE_RANK
    o_ref[:, :C_Z] = w_ref[:, :C_Z].astype(BF16)
    gate_tile = w_ref[:, C_Z:C_Z + LANES]
    lane = lax.broadcasted_iota(I32, gate_tile.shape, 1)
    o_ref[:, C_Z:C_SQ] = jnp.where(lane < 2 * GLA_GATE_RANK, gate_tile, 0.0).astype(BF16)
    o_ref[:, C_SQ:] = w_ref[:, cut:].astype(BF16)


def _pack_w_in(w):
    assert sum(IN_SIZES[:6]) == C_Z + 2 * GLA_GATE_RANK and sum(IN_SIZES) - sum(IN_SIZES[:6]) == C_END - C_SQ
    depth, d, cols = w.shape
    rows = ROW_TILE // 2
    return pl.pallas_call(
        _pack_w_kernel,
        grid=(depth, d // rows),
        in_specs=[pl.BlockSpec((None, rows, cols), lambda l, r: (l, r, 0))],
        out_specs=pl.BlockSpec((None, rows, C_END), lambda l, r: (l, r, 0)),
        out_shape=jax.ShapeDtypeStruct((depth, d, C_END), BF16),
        compiler_params=_cparams(("arbitrary", "arbitrary")),
        name="pack_w_in",
    )(w)


def _pad_up(up, row0):
    up = up.reshape(GLA_GATE_RANK, GLA_HEADS // 2, LANES).transpose(1, 0, 2)
    hi = up.astype(BF16)
    lo = (up - hi.astype(F32)).astype(BF16)
    out = jnp.zeros((GLA_HEADS // 2, LANES, LANES), BF16)
    for group, part in enumerate((hi, hi, lo)):
        r = group * 2 * GLA_GATE_RANK + row0
        out = out.at[:, r:r + GLA_GATE_RANK, :].set(part)
    return out


def _rope_tables(seq):
    rows = seq // GRID_W
    dim = jnp.arange(LANES, dtype=I32) % SWA_HD
    inv = ROPE_THETA ** (-((dim % ROPE_HALF).astype(F32) * 2.0 / ROPE_AXIS_DIM))
    sign = jnp.where(dim % ROPE_AXIS_DIM < ROPE_HALF, -1.0, 1.0)
    by_row = (dim < ROPE_AXIS_DIM)[None, None, :]
    ang_r = (jnp.arange(rows, dtype=F32)[:, None] * inv[None, :])[:, None, :]
    ang_c = (jnp.arange(GRID_W, dtype=F32)[:, None] * inv[None, :])[None, :, :]
    cos = jnp.where(by_row, jnp.cos(ang_r), jnp.cos(ang_c))
    sin = jnp.where(by_row, jnp.sin(ang_r), jnp.sin(ang_c)) * sign
    return cos.reshape(seq, LANES), sin.reshape(seq, LANES)


def _tile(n, pref):
    t = pref
    while n % t:
        t //= 2
    return t


def kernel(x, c, ctx, c_ctx, w_ada, b_ada, norm1, norm2, w_in, gla_up_f, gla_bias_f, gla_up_b, gla_bias_b,
           gla_norm, swa_sink, w_out, w_router, b_router, w_gate, w_up, w_down, final_norm):
    batch, seq, d = x.shape
    lc = ctx.shape[1]
    depth = w_ada.shape[0]
    n_l = batch * seq
    n_c = batch * lc
    npair = GLA_HEADS // 2
    assert batch + 1 <= SUBLANES
    assert n_l % ROUTE_TILE == 0 and n_c % ROUTE_TILE == 0

    tm_l = _tile(seq, ROW_TILE)
    tm_in = _tile(seq, INPROJ_TILE)
    tm_c = _tile(lc, CTX_TILE)
    tl_l = _tile(seq, GLA_TILE)
    tl_c = _tile(lc, ROW_TILE)

    rows = jnp.zeros((SUBLANES, d), F32).at[:batch].set(c).at[batch].set(c_ctx)
    mods = _ada_call(rows, w_ada, b_ada)
    mods5 = mods.reshape(depth, SUBLANES, 6, 1, d)

    tabs = _rope_tables(seq)
    w_packed = _pack_w_in(w_in)
    w_out_b = w_out.astype(BF16)
    wr_pad = jnp.zeros((d, LANES), F32).at[:, :N_EXPERTS].set(w_router)
    wr_hi = wr_pad.astype(BF16)
    wr_lo = (wr_pad - wr_hi.astype(F32)).astype(BF16)
    wr_cat = jnp.concatenate([wr_hi, wr_lo], axis=-1)

    def lat_row(tm):
        return lambda t: t // (seq // tm)

    def ctx_row(tm):
        return lambda t: batch

    xl = x.reshape(n_l, d)
    xc = ctx.reshape(n_c, d)
    for i in range(depth):
        last = i == depth - 1
        upf = _pad_up(gla_up_f[i], 0)
        upb = _pad_up(gla_up_b[i], GLA_GATE_RANK)
        bias_f = gla_bias_f[i].reshape(npair, 1, LANES)
        bias_b = gla_bias_b[i].reshape(npair, 1, LANES)

        c_qk, c_v, c_g, c_z, c_sq, c_sk, c_sv = _inproj_call(
            xc, norm1[i], mods5, i, ctx_row(tm_c), w_packed, None, tm_c, lc // tm_c)
        l_qk, l_v, l_g, l_z, l_sq, l_sk, l_sv = _inproj_call(
            xl, norm1[i], mods5, i, lat_row(tm_in), w_packed, tabs, tm_in, seq // tm_in)

        s_zero = jnp.zeros((batch, npair, 2, 2 * GLA_DV, LANES), F32)
        oc_f, oc_b, s_ctx = _gla_call(c_qk, c_v, c_z, upf, upb, bias_f, bias_b, s_zero, batch, lc, tl_c)
        ol_f, ol_b, _ = _gla_call(l_qk, l_v, l_z, upf, upb, bias_f, bias_b, s_ctx, batch, seq, tl_l)

        swa_l = _swa_call(l_sq, l_sk, l_sv, c_sk, c_sv, swa_sink[i], batch, seq, lc)
        xl, h2l, lg_l = _outproj_call(ol_f, ol_b, l_g, gla_norm[i], swa_l, w_out_b, xl, mods5, i, lat_row(tm_l),
                                      norm2[i], wr_cat, tm_l)
        if last:
            logits_t = lg_l
            t_all = n_l
        else:
            swa_c = _swa_ctx_call(c_sq, c_sk, c_sv, swa_sink[i], batch, lc)
            xc, h2c, lg_c = _outproj_call(oc_f, oc_b, c_g, gla_norm[i], swa_c, w_out_b, xc, mods5, i, ctx_row(tm_c),
                                          norm2[i], wr_cat, tm_c)
            logits_t = jnp.concatenate([lg_l, lg_c], axis=1)
            t_all = n_l + n_c

        io, wo, seg, pad, blk = _route_call(logits_t, b_router, ROUTE_TILE)
        m = MOE_BLOCK
        padded = blk[:, 0]
        pend = jnp.cumsum(padded)
        ntile = t_all // ROUTE_TILE
        nb = -(-(t_all * TOP_K + ntile * N_EXPERTS * (SUBLANES - 1)) // m) + N_EXPERTS
        p_rows = nb * m
        seg = seg.reshape(-1)
        pad = pad.reshape(-1)
        nused = (pend[-1:] // m).astype(I32)
        tail = nused + jnp.arange(nb - (t_all * TOP_K) // m, dtype=I32)
        zblk = jnp.concatenate([jnp.where(padded > 0, pend // m - 1, -1),
                                jnp.where(tail < nb, tail, -1)]).astype(I32)
        blk_row = jnp.minimum(jnp.arange(nb, dtype=I32), nused - 1) * m
        block_e = jnp.sum((pend[None, :] <= blk_row[:, None]).astype(I32), axis=1)
        rows_e = jnp.sum(pad.reshape(N_EXPERTS, LANES), axis=1)
        own = (block_e[:, None] == jnp.arange(N_EXPERTS, dtype=I32)[None, :]).astype(I32)
        seg_end = jnp.sum(own * (pend - padded + rows_e)[None, :], axis=1)
        blk_idx = jnp.arange(nb, dtype=I32)
        valid = jnp.where(blk_idx < nused, jnp.clip(seg_end - blk_idx * m, 0, m), 0).astype(I32)

        sources = (h2l,) if last else (h2l, h2c)
        xb = _dispatch_call(seg, pad, zblk, io, wo, sources, p_rows)
        eid = jnp.arange(N_EXPERTS, dtype=I32)
        later = (eid[None, :] > eid[:, None]) & (padded[None, :] > 0)
        next_e = jnp.min(jnp.where(later, eid[None, :], N_EXPERTS), axis=1)
        next_e = jnp.where(next_e < N_EXPERTS, next_e, -1).astype(I32)
        yb = _expert_call(block_e, nused, valid, next_e, xb, w_gate, w_up, w_down, i)
        xl_new = _combine_call(seg, pad, yb, io, xl, mods5, i, lat_row(ROUTE_TILE),
                               final_norm if last else None, 0)
        if not last:
            xc = _combine_call(seg, pad, yb, io, xc, mods5, i, ctx_row(0), None, n_l // ROUTE_TILE)
        xl = xl_new
    return xl.reshape(batch, seq, d)
```
